```python
import jax, jax.numpy as jnp
from jax import lax
import numpy as np

D_MODEL = 1024
BATCH = 16
SEQ = 4096
DEPTH = 1

D_MIX = D_MODEL
SGU_WIDTH = D_MIX // 2
SGU_GROUPS = 4
SGU_GROUP_DIM = SGU_WIDTH // SGU_GROUPS
SGU_CHUNK = 128
DN_WIDTH = D_MIX - SGU_WIDTH
DN_HEADS = 4
DN_HEAD_DIM = DN_WIDTH // DN_HEADS
DN_CHUNK = 64
CONV_K = 4
PLE_DIM = 256
EPS = 1e-6

IN_SIZES = (SGU_WIDTH, SGU_WIDTH, SGU_WIDTH,
            DN_WIDTH, DN_WIDTH, DN_WIDTH, DN_WIDTH,
            DN_HEADS, DN_HEADS)
IN_COLS = sum(IN_SIZES)
IN_SPLITS = tuple(int(s) for s in np.cumsum(IN_SIZES)[:-1])

kernel_name = "hybrid_sgu_gated_deltanet_ple"


def rms_norm(x, g):
    xf = x.astype(jnp.float32)
    y = xf * lax.rsqrt(jnp.mean(xf * xf, axis=-1, keepdims=True) + EPS)
    return (y * g.astype(jnp.float32)).astype(x.dtype)


def layer_norm(x, g, b):
    xf = x.astype(jnp.float32)
    mu = jnp.mean(xf, axis=-1, keepdims=True)
    xc = xf - mu
    y = xc * lax.rsqrt(jnp.mean(xc * xc, axis=-1, keepdims=True) + EPS)
    return (y * g.astype(jnp.float32) + b.astype(jnp.float32)).astype(x.dtype)


def l2_norm(x):
    return x * lax.rsqrt(jnp.sum(x * x, axis=-1, keepdims=True) + EPS)


def chunked_sgu(u, v, ln_g, ln_b, w_s, b_s):
    B, S, _ = u.shape
    nc = S // SGU_CHUNK
    u = jax.nn.gelu(u, approximate=False)
    v = jax.nn.gelu(v, approximate=False)
    v = v.reshape(B, nc, SGU_CHUNK, SGU_GROUPS, SGU_GROUP_DIM)
    v = layer_norm(v, ln_g.reshape(SGU_GROUPS, SGU_GROUP_DIM), ln_b.reshape(SGU_GROUPS, SGU_GROUP_DIM))
    causal = jnp.tril(jnp.ones((SGU_CHUNK, SGU_CHUNK), dtype=bool))
    w = jnp.where(causal, w_s, jnp.zeros_like(w_s)).astype(v.dtype)
    s = jnp.einsum('gts,bnsgd->bntgd', w, v) + b_s.T.astype(v.dtype)[:, :, None]
    return u * s.reshape(B, S, SGU_WIDTH)


def causal_depthwise_conv(x, w):
    ch = x.shape[-1]
    return lax.conv_general_dilated(
        x, w[:, None, :].astype(x.dtype), window_strides=(1,), padding=[(CONV_K - 1, 0)],
        dimension_numbers=('NWC', 'WIO', 'NWC'), feature_group_count=ch)


def gated_delta_rule(q, k, v, beta, g):
    B, S, H, dk = q.shape
    dv = v.shape[-1]
    C = DN_CHUNK
    N = S // C
    q = q * (dk ** -0.5)

    def to_chunks(t):
        t = t.reshape((B, N, C, H) + t.shape[3:])
        return jnp.moveaxis(t, 3, 1)

    q, k, v, beta, g = (to_chunks(t) for t in (q, k, v, beta, g))
    g = jnp.cumsum(g, axis=-1)
    tril = jnp.tril(jnp.ones((C, C), dtype=bool))
    strict = jnp.tril(jnp.ones((C, C), dtype=bool), -1)
    diff = g[..., :, None] - g[..., None, :]
    decay = jnp.exp(jnp.where(tril, diff, -jnp.inf))

    k_beta = k * beta[..., None]
    kk = jnp.einsum('bhnid,bhnjd->bhnij', k_beta, k) * decay
    a_mat = jnp.eye(C, dtype=q.dtype) + jnp.where(strict, kk, jnp.zeros_like(kk))
    rhs = jnp.concatenate([v * beta[..., None], k_beta * jnp.exp(g)[..., None]], axis=-1)
    sol = lax.linalg.triangular_solve(a_mat, rhs, left_side=True, lower=True, unit_diagonal=True)
    u_val, w_dec = sol[..., :dv], sol[..., dv:]

    qk = jnp.einsum('bhnid,bhnjd->bhnij', q, k) * decay
    q_dec = q * jnp.exp(g)[..., None]
    g_last = g[..., -1]
    k_dec = k * jnp.exp(g_last[..., None] - g)[..., None]

    xs = tuple(jnp.moveaxis(t, 2, 0) for t in (q_dec, k_dec, u_val, w_dec, qk, g_last))

    def step(state, inp):
        qd, kd, uv, wd, a, gl = inp
        v_new = uv - jnp.einsum('bhcd,bhde->bhce', wd, state)
        out = jnp.einsum('bhcd,bhde->bhce', qd, state) + jnp.einsum('bhij,bhje->bhie', a, v_new)
        state = state * jnp.exp(gl)[..., None, None] + jnp.einsum('bhcd,bhce->bhde', kd, v_new)
        return state, out

    state0 = jnp.zeros((B, H, dk, dv), dtype=q.dtype)
    _, out = lax.scan(step, state0, xs)
    out = jnp.moveaxis(out, 0, 2)
    return jnp.moveaxis(out, 1, 3).reshape(B, S, H, dv)


def gated_deltanet(q, k, v, z, b_logit, a_logit, conv_w, a_log, dt_bias, o_norm_g):
    B, S, _ = q.shape
    dt = q.dtype
    qkv = jax.nn.silu(causal_depthwise_conv(jnp.concatenate([q, k, v], axis=-1), conv_w))
    q, k, v = jnp.split(qkv, 3, axis=-1)
    heads = lambda t: t.reshape(B, S, DN_HEADS, DN_HEAD_DIM).astype(jnp.float32)
    q, k, v = l2_norm(heads(q)), l2_norm(heads(k)), heads(v)
    beta = jax.nn.sigmoid(b_logit.astype(jnp.float32))
    g = -jnp.exp(a_log.astype(jnp.float32)) * jax.nn.softplus(a_logit.astype(jnp.float32) + dt_bias.astype(jnp.float32))
    o = gated_delta_rule(q, k, v, beta, g)
    o = rms_norm(o, o_norm_g).reshape(B, S, DN_WIDTH)
    return (o * jax.nn.silu(z.astype(jnp.float32))).astype(dt)


def _fwd_setup_inputs(seed: int = 0) -> dict:
    key = jax.random.key(seed)
    ks = jax.random.split(key, 20)
    nrm = lambda k, shape, scale: jax.random.normal(k, shape, jnp.float32) * scale
    x = jax.random.normal(ks[0], (BATCH, SEQ, D_MODEL), jnp.float32)
    p = jax.random.normal(ks[1], (DEPTH, BATCH, SEQ, PLE_DIM), jnp.float32)
    norm_g = 1.0 + nrm(ks[2], (DEPTH, D_MODEL), 0.02)
    w_in = nrm(ks[3], (DEPTH, D_MODEL, IN_COLS), D_MODEL ** -0.5)
    sgu_ln_g = 1.0 + nrm(ks[4], (DEPTH, SGU_WIDTH), 0.02)
    sgu_ln_b = nrm(ks[5], (DEPTH, SGU_WIDTH), 0.02)
    sgu_w_s = nrm(ks[6], (DEPTH, SGU_GROUPS, SGU_CHUNK, SGU_CHUNK), 0.5 * SGU_CHUNK ** -0.5)
    sgu_b_s = 1.0 + nrm(ks[7], (DEPTH, SGU_GROUPS, SGU_CHUNK), 0.01)
    dn_conv_w = nrm(ks[8], (DEPTH, CONV_K, 3 * DN_WIDTH), CONV_K ** -0.5)
    dn_a_log = jnp.log(jax.random.uniform(ks[9], (DEPTH, DN_HEADS), jnp.float32, 1.0, 16.0))
    dt0 = jnp.exp(jax.random.uniform(ks[10], (DEPTH, DN_HEADS), jnp.float32, np.log(1e-3), np.log(1e-1)))
    dn_dt_bias = jnp.log(jnp.expm1(dt0))
    dn_o_norm_g = 1.0 + nrm(ks[11], (DEPTH, DN_HEAD_DIM), 0.02)
    w_out = nrm(ks[12], (DEPTH, D_MIX, D_MODEL), D_MIX ** -0.5)
    ple_norm_g = 1.0 + nrm(ks[13], (DEPTH, D_MODEL), 0.02)
    ple_gate_w = nrm(ks[14], (DEPTH, D_MODEL, D_MODEL), D_MODEL ** -0.5)
    ple_proj_w = nrm(ks[15], (DEPTH, PLE_DIM, D_MODEL), PLE_DIM ** -0.5)
    final_norm_g = 1.0 + nrm(ks[16], (D_MODEL,), 0.02)
    return {"x": x, "p": p, "norm_g": norm_g, "w_in": w_in,
            "sgu_ln_g": sgu_ln_g, "sgu_ln_b": sgu_ln_b, "sgu_w_s": sgu_w_s, "sgu_b_s": sgu_b_s,
            "dn_conv_w": dn_conv_w, "dn_a_log": dn_a_log, "dn_dt_bias": dn_dt_bias,
            "dn_o_norm_g": dn_o_norm_g, "w_out": w_out,
            "ple_norm_g": ple_norm_g, "ple_gate_w": ple_gate_w, "ple_proj_w": ple_proj_w,
            "final_norm_g": final_norm_g}


def _fwd_reference(x, p, norm_g, w_in, sgu_ln_g, sgu_ln_b, sgu_w_s, sgu_b_s, dn_conv_w, dn_a_log,
              dn_dt_bias, dn_o_norm_g, w_out, ple_norm_g, ple_gate_w, ple_proj_w, final_norm_g):
    h = x
    for i in range(DEPTH):
        xn = rms_norm(h, norm_g[i])
        proj = jnp.einsum('bsd,dc->bsc', xn, w_in[i])
        (a_u, a_v, a_z, b_q, b_k, b_v, b_z, b_beta, b_a) = jnp.split(proj, IN_SPLITS, axis=-1)
        a_out = chunked_sgu(a_u, a_v, sgu_ln_g[i], sgu_ln_b[i], sgu_w_s[i], sgu_b_s[i]) * jax.nn.silu(a_z)
        b_out = gated_deltanet(b_q, b_k, b_v, b_z, b_beta, b_a, dn_conv_w[i], dn_a_log[i],
                               dn_dt_bias[i], dn_o_norm_g[i])
        mixed = jnp.concatenate([a_out, b_out], axis=-1)
        h = h + jnp.einsum('bsc,cd->bsd', mixed, w_out[i])
        gate = jax.nn.sigmoid(jnp.einsum('bsd,de->bse', rms_norm(h, ple_norm_g[i]), ple_gate_w[i]))
        h = h + gate * jnp.einsum('bse,ed->bsd', p[i].astype(h.dtype), ple_proj_w[i])
    return rms_norm(h, final_norm_g)


import jax as _jax
import jax.numpy as _jnp

TWIN_FORMAT = 'train_step'
FWD_PARAMS = ['x', 'p', 'norm_g', 'w_in', 'sgu_ln_g', 'sgu_ln_b', 'sgu_w_s', 'sgu_b_s', 'dn_conv_w', 'dn_a_log', 'dn_dt_bias', 'dn_o_norm_g', 'w_out', 'ple_norm_g', 'ple_gate_w', 'ple_proj_w', 'final_norm_g']
TWIN_WEIGHTS = ['norm_g', 'w_in', 'sgu_ln_g', 'sgu_ln_b', 'sgu_w_s', 'sgu_b_s', 'dn_conv_w', 'dn_a_log', 'dn_dt_bias', 'dn_o_norm_g', 'w_out', 'ple_norm_g', 'ple_gate_w', 'ple_proj_w', 'final_norm_g']
TWIN_DIFF_INPUT = 'x'
TWIN_INPUTS = ['x', 'p', 'norm_g', 'w_in', 'sgu_ln_g', 'sgu_ln_b', 'sgu_w_s', 'sgu_b_s', 'dn_conv_w', 'dn_a_log', 'dn_dt_bias', 'dn_o_norm_g', 'w_out', 'ple_norm_g', 'ple_gate_w', 'ple_proj_w', 'final_norm_g', 'loss_target', 'm_norm_g', 'm_w_in', 'm_sgu_ln_g', 'm_sgu_ln_b', 'm_sgu_w_s', 'm_sgu_b_s', 'm_dn_conv_w', 'm_dn_a_log', 'm_dn_dt_bias', 'm_dn_o_norm_g', 'm_w_out', 'm_ple_norm_g', 'm_ple_gate_w', 'm_ple_proj_w', 'm_final_norm_g', 'v_norm_g', 'v_w_in', 'v_sgu_ln_g', 'v_sgu_ln_b', 'v_sgu_w_s', 'v_sgu_b_s', 'v_dn_conv_w', 'v_dn_a_log', 'v_dn_dt_bias', 'v_dn_o_norm_g', 'v_w_out', 'v_ple_norm_g', 'v_ple_gate_w', 'v_ple_proj_w', 'v_final_norm_g']
TWIN_OUTPUTS = ['loss', 'grad_x', 'grad_norm_g', 'grad_w_in', 'grad_sgu_ln_g', 'grad_sgu_ln_b', 'grad_sgu_w_s', 'grad_sgu_b_s', 'grad_dn_conv_w', 'grad_dn_a_log', 'grad_dn_dt_bias', 'grad_dn_o_norm_g', 'grad_w_out', 'grad_ple_norm_g', 'grad_ple_gate_w', 'grad_ple_proj_w', 'grad_final_norm_g', 'delta_norm_g', 'delta_w_in', 'delta_sgu_ln_g', 'delta_sgu_ln_b', 'delta_sgu_w_s', 'delta_sgu_b_s', 'delta_dn_conv_w', 'delta_dn_a_log', 'delta_dn_dt_bias', 'delta_dn_o_norm_g', 'delta_w_out', 'delta_ple_norm_g', 'delta_ple_gate_w', 'delta_ple_proj_w', 'delta_final_norm_g', 'new_m_norm_g', 'new_m_w_in', 'new_m_sgu_ln_g', 'new_m_sgu_ln_b', 'new_m_sgu_w_s', 'new_m_sgu_b_s', 'new_m_dn_conv_w', 'new_m_dn_a_log', 'new_m_dn_dt_bias', 'new_m_dn_o_norm_g', 'new_m_w_out', 'new_m_ple_norm_g', 'new_m_ple_gate_w', 'new_m_ple_proj_w', 'new_m_final_norm_g', 'new_v_norm_g', 'new_v_w_in', 'new_v_sgu_ln_g', 'new_v_sgu_ln_b', 'new_v_sgu_w_s', 'new_v_sgu_b_s', 'new_v_dn_conv_w', 'new_v_dn_a_log', 'new_v_dn_dt_bias', 'new_v_dn_o_norm_g', 'new_v_w_out', 'new_v_ple_norm_g', 'new_v_ple_gate_w', 'new_v_ple_proj_w', 'new_v_final_norm_g']
TWIN_LEAF_KINDS = {'loss': 'loss', 'grad_x': 'grad_x', 'grad_norm_g': 'grad_w', 'grad_w_in': 'grad_w', 'grad_sgu_ln_g': 'grad_w', 'grad_sgu_ln_b': 'grad_w', 'grad_sgu_w_s': 'grad_w', 'grad_sgu_b_s': 'grad_w', 'grad_dn_conv_w': 'grad_w', 'grad_dn_a_log': 'grad_w', 'grad_dn_dt_bias': 'grad_w', 'grad_dn_o_norm_g': 'grad_w', 'grad_w_out': 'grad_w', 'grad_ple_norm_g': 'grad_w', 'grad_ple_gate_w': 'grad_w', 'grad_ple_proj_w': 'grad_w', 'grad_final_norm_g': 'grad_w', 'delta_norm_g': 'delta_w', 'delta_w_in': 'delta_w', 'delta_sgu_ln_g': 'delta_w', 'delta_sgu_ln_b': 'delta_w', 'delta_sgu_w_s': 'delta_w', 'delta_sgu_b_s': 'delta_w', 'delta_dn_conv_w': 'delta_w', 'delta_dn_a_log': 'delta_w', 'delta_dn_dt_bias': 'delta_w', 'delta_dn_o_norm_g': 'delta_w', 'delta_w_out': 'delta_w', 'delta_ple_norm_g': 'delta_w', 'delta_ple_gate_w': 'delta_w', 'delta_ple_proj_w': 'delta_w', 'delta_final_norm_g': 'delta_w', 'new_m_norm_g': 'new_m', 'new_m_w_in': 'new_m', 'new_m_sgu_ln_g': 'new_m', 'new_m_sgu_ln_b': 'new_m', 'new_m_sgu_w_s': 'new_m', 'new_m_sgu_b_s': 'new_m', 'new_m_dn_conv_w': 'new_m', 'new_m_dn_a_log': 'new_m', 'new_m_dn_dt_bias': 'new_m', 'new_m_dn_o_norm_g': 'new_m', 'new_m_w_out': 'new_m', 'new_m_ple_norm_g': 'new_m', 'new_m_ple_gate_w': 'new_m', 'new_m_ple_proj_w': 'new_m', 'new_m_final_norm_g': 'new_m', 'new_v_norm_g': 'new_v', 'new_v_w_in': 'new_v', 'new_v_sgu_ln_g': 'new_v', 'new_v_sgu_ln_b': 'new_v', 'new_v_sgu_w_s': 'new_v', 'new_v_sgu_b_s': 'new_v', 'new_v_dn_conv_w': 'new_v', 'new_v_dn_a_log': 'new_v', 'new_v_dn_dt_bias': 'new_v', 'new_v_dn_o_norm_g': 'new_v', 'new_v_w_out': 'new_v', 'new_v_ple_norm_g': 'new_v', 'new_v_ple_gate_w': 'new_v', 'new_v_ple_proj_w': 'new_v', 'new_v_final_norm_g': 'new_v'}


def _forward(args):
    return _fwd_reference(*[args[k] for k in FWD_PARAMS])


def _output_shape():
    out = _jax.eval_shape(lambda: _forward(_fwd_setup_inputs(0)))
    return out.shape, out.dtype

N_MICROBATCH = 1
ADAM_LR = 0.001
ADAM_B1 = 0.9
ADAM_B2 = 0.999
ADAM_EPS = 1e-08
ADAM_WD = 0.01
ADAM_STEP = 10
PER_EXAMPLE_BATCH_AXIS = {'x': 0, 'p': 1, 'loss_target': 0}
SHARED_INPUTS = []
_WEIGHT_DTYPES = {'norm_g': _jnp.float32, 'w_in': _jnp.float32, 'sgu_ln_g': _jnp.float32, 'sgu_ln_b': _jnp.float32, 'sgu_w_s': _jnp.float32, 'sgu_b_s': _jnp.float32, 'dn_conv_w': _jnp.float32, 'dn_a_log': _jnp.float32, 'dn_dt_bias': _jnp.float32, 'dn_o_norm_g': _jnp.float32, 'w_out': _jnp.float32, 'ple_norm_g': _jnp.float32, 'ple_gate_w': _jnp.float32, 'ple_proj_w': _jnp.float32, 'final_norm_g': _jnp.float32}
MOMENT_SCALE = {'norm_g': 1.658403e-01, 'w_in': 8.945670e-02, 'sgu_ln_g': 2.724484e-02, 'sgu_ln_b': 3.136396e-02, 'sgu_w_s': 5.601804e-02, 'sgu_b_s': 8.082101e-02, 'dn_conv_w': 9.388943e-02, 'dn_a_log': 1.624923e+00, 'dn_dt_bias': 1.540351e+00, 'dn_o_norm_g': 3.151360e-01, 'w_out': 1.046574e-01, 'ple_norm_g': 4.293379e-02, 'ple_gate_w': 4.309527e-02, 'ple_proj_w': 1.094448e-01, 'final_norm_g': 6.392219e+01}


def _to_microbatches(a, axis):
    t = _jnp.moveaxis(a, axis, 0)
    t = t.reshape((N_MICROBATCH, t.shape[0] // N_MICROBATCH) + t.shape[1:])
    return _jnp.moveaxis(t, 1, axis + 1)


def setup_inputs(seed: int = 0) -> dict:
    inp = _fwd_setup_inputs(seed)
    key = _jax.random.fold_in(_jax.random.key(seed), 7919)
    shape, _ = _output_shape()
    out = dict(inp)
    out["loss_target"] = _jax.random.normal(_jax.random.fold_in(key, 0), shape, _jnp.float32)
    for i, name in enumerate(TWIN_WEIGHTS):
        w = inp[name].astype(_jnp.float32)
        if MOMENT_SCALE is None:
            s = _jnp.sqrt(_jnp.mean(_jnp.square(w)) + 1e-30)
        else:
            s = MOMENT_SCALE[name]
        km, kv = _jax.random.split(_jax.random.fold_in(key, i + 1))
        out[name] = w
        out["m_" + name] = s * _jax.random.normal(km, w.shape, _jnp.float32)
        out["v_" + name] = (s * s) * _jax.random.uniform(kv, w.shape, _jnp.float32, 0.5, 1.5)
    if N_MICROBATCH > 1:
        for name, axis in PER_EXAMPLE_BATCH_AXIS.items():
            out[name] = _to_microbatches(out[name], axis)
    return {'x': out['x'], 'p': out['p'], 'norm_g': out['norm_g'], 'w_in': out['w_in'], 'sgu_ln_g': out['sgu_ln_g'], 'sgu_ln_b': out['sgu_ln_b'], 'sgu_w_s': out['sgu_w_s'], 'sgu_b_s': out['sgu_b_s'], 'dn_conv_w': out['dn_conv_w'], 'dn_a_log': out['dn_a_log'], 'dn_dt_bias': out['dn_dt_bias'], 'dn_o_norm_g': out['dn_o_norm_g'], 'w_out': out['w_out'], 'ple_norm_g': out['ple_norm_g'], 'ple_gate_w': out['ple_gate_w'], 'ple_proj_w': out['ple_proj_w'], 'final_norm_g': out['final_norm_g'], 'loss_target': out['loss_target'], 'm_norm_g': out['m_norm_g'], 'm_w_in': out['m_w_in'], 'm_sgu_ln_g': out['m_sgu_ln_g'], 'm_sgu_ln_b': out['m_sgu_ln_b'], 'm_sgu_w_s': out['m_sgu_w_s'], 'm_sgu_b_s': out['m_sgu_b_s'], 'm_dn_conv_w': out['m_dn_conv_w'], 'm_dn_a_log': out['m_dn_a_log'], 'm_dn_dt_bias': out['m_dn_dt_bias'], 'm_dn_o_norm_g': out['m_dn_o_norm_g'], 'm_w_out': out['m_w_out'], 'm_ple_norm_g': out['m_ple_norm_g'], 'm_ple_gate_w': out['m_ple_gate_w'], 'm_ple_proj_w': out['m_ple_proj_w'], 'm_final_norm_g': out['m_final_norm_g'], 'v_norm_g': out['v_norm_g'], 'v_w_in': out['v_w_in'], 'v_sgu_ln_g': out['v_sgu_ln_g'], 'v_sgu_ln_b': out['v_sgu_ln_b'], 'v_sgu_w_s': out['v_sgu_w_s'], 'v_sgu_b_s': out['v_sgu_b_s'], 'v_dn_conv_w': out['v_dn_conv_w'], 'v_dn_a_log': out['v_dn_a_log'], 'v_dn_dt_bias': out['v_dn_dt_bias'], 'v_dn_o_norm_g': out['v_dn_o_norm_g'], 'v_w_out': out['v_w_out'], 'v_ple_norm_g': out['v_ple_norm_g'], 'v_ple_gate_w': out['v_ple_gate_w'], 'v_ple_proj_w': out['v_ple_proj_w'], 'v_final_norm_g': out['v_final_norm_g']}


def _loss(weights, diff, rest, loss_target):
    with _jax.named_scope("forward"):
        args = {**rest, TWIN_DIFF_INPUT: diff, **{k: w.astype(_WEIGHT_DTYPES[k]) for k, w in weights.items()}}
        y = _forward(args)
    with _jax.named_scope("loss_head"):
        err = _jnp.square(y.astype(_jnp.float32) - loss_target)
        return 0.5 * _jnp.sum(_jnp.mean(err, axis=-1)) if err.ndim else 0.5 * err


def _adamw(w, g, m, v):
    m = ADAM_B1 * m + (1.0 - ADAM_B1) * g
    v = ADAM_B2 * v + (1.0 - ADAM_B2) * _jnp.square(g)
    m_hat = m / (1.0 - ADAM_B1 ** ADAM_STEP)
    v_hat = v / (1.0 - ADAM_B2 ** ADAM_STEP)
    delta = -ADAM_LR * (m_hat / (_jnp.sqrt(v_hat) + ADAM_EPS) + ADAM_WD * w)
    return delta, m, v


def reference(x, p, norm_g, w_in, sgu_ln_g, sgu_ln_b, sgu_w_s, sgu_b_s, dn_conv_w, dn_a_log, dn_dt_bias, dn_o_norm_g, w_out, ple_norm_g, ple_gate_w, ple_proj_w, final_norm_g, loss_target, m_norm_g, m_w_in, m_sgu_ln_g, m_sgu_ln_b, m_sgu_w_s, m_sgu_b_s, m_dn_conv_w, m_dn_a_log, m_dn_dt_bias, m_dn_o_norm_g, m_w_out, m_ple_norm_g, m_ple_gate_w, m_ple_proj_w, m_final_norm_g, v_norm_g, v_w_in, v_sgu_ln_g, v_sgu_ln_b, v_sgu_w_s, v_sgu_b_s, v_dn_conv_w, v_dn_a_log, v_dn_dt_bias, v_dn_o_norm_g, v_w_out, v_ple_norm_g, v_ple_gate_w, v_ple_proj_w, v_final_norm_g):
    given = dict(x=x, p=p, norm_g=norm_g, w_in=w_in, sgu_ln_g=sgu_ln_g, sgu_ln_b=sgu_ln_b, sgu_w_s=sgu_w_s, sgu_b_s=sgu_b_s, dn_conv_w=dn_conv_w, dn_a_log=dn_a_log, dn_dt_bias=dn_dt_bias, dn_o_norm_g=dn_o_norm_g, w_out=w_out, ple_norm_g=ple_norm_g, ple_gate_w=ple_gate_w, ple_proj_w=ple_proj_w, final_norm_g=final_norm_g, loss_target=loss_target, m_norm_g=m_norm_g, m_w_in=m_w_in, m_sgu_ln_g=m_sgu_ln_g, m_sgu_ln_b=m_sgu_ln_b, m_sgu_w_s=m_sgu_w_s, m_sgu_b_s=m_sgu_b_s, m_dn_conv_w=m_dn_conv_w, m_dn_a_log=m_dn_a_log, m_dn_dt_bias=m_dn_dt_bias, m_dn_o_norm_g=m_dn_o_norm_g, m_w_out=m_w_out, m_ple_norm_g=m_ple_norm_g, m_ple_gate_w=m_ple_gate_w, m_ple_proj_w=m_ple_proj_w, m_final_norm_g=m_final_norm_g, v_norm_g=v_norm_g, v_w_in=v_w_in, v_sgu_ln_g=v_sgu_ln_g, v_sgu_ln_b=v_sgu_ln_b, v_sgu_w_s=v_sgu_w_s, v_sgu_b_s=v_sgu_b_s, v_dn_conv_w=v_dn_conv_w, v_dn_a_log=v_dn_a_log, v_dn_dt_bias=v_dn_dt_bias, v_dn_o_norm_g=v_dn_o_norm_g, v_w_out=v_w_out, v_ple_norm_g=v_ple_norm_g, v_ple_gate_w=v_ple_gate_w, v_ple_proj_w=v_ple_proj_w, v_final_norm_g=v_final_norm_g)
    weights = {n: given[n] for n in TWIN_WEIGHTS}
    shared = {n: given[n] for n in SHARED_INPUTS}
    per_example = {n: given[n] for n in ['x', 'p']}
    grad_fn = _jax.value_and_grad(_loss, argnums=(0, 1))

    def one_microbatch(ex, loss_target):
        ex = dict(ex)
        diff = ex.pop(TWIN_DIFF_INPUT)
        return grad_fn(weights, diff, {**shared, **ex}, loss_target)

    if N_MICROBATCH == 1:
        loss, (grad_w, grad_x) = one_microbatch(per_example, given["loss_target"])
    else:
        def body(carry, xs):
            loss_sum, grad_sum = carry
            l_k, (gw_k, gx_k) = one_microbatch(xs[0], xs[1])
            with _jax.named_scope("update"):
                return (loss_sum + l_k, _jax.tree.map(_jnp.add, grad_sum, gw_k)), gx_k

        init = (_jnp.zeros((), _jnp.float32), _jax.tree.map(_jnp.zeros_like, weights))
        (loss, grad_w), grad_x = _jax.lax.scan(body, init, (per_example, given["loss_target"]))
    with _jax.named_scope("update"):
        delta_w, new_m, new_v = {}, {}, {}
        for n in TWIN_WEIGHTS:
            delta_w[n], new_m[n], new_v[n] = _adamw(weights[n], grad_w[n], given["m_" + n], given["v_" + n])
    return (loss, grad_x, *[grad_w[n] for n in TWIN_WEIGHTS], *[delta_w[n] for n in TWIN_WEIGHTS],
            *[new_m[n] for n in TWIN_WEIGHTS], *[new_v[n] for n in TWIN_WEIGHTS])
```

```python
import functools

import jax
import jax.numpy as jnp
from jax import lax
from jax.experimental import pallas as pl
from jax.experimental.pallas import tpu as pltpu

F32 = jnp.float32
BF16 = jnp.bfloat16

N_DEV = 8
D_MODEL = 1024
SGU_WIDTH = 512
SGU_GROUPS = 4
SGU_CHUNK = 128
DN_WIDTH = 512
DN_HEADS = 4
DN_HEAD_DIM = 128
DN_CHUNK = 64
CONV_K = 4
CONV_HALO = 8
PLE_DIM = 256
EPS = 1e-6
IN_COLS = 3592
IN_SHARD = IN_COLS // N_DEV
GATE_PAD = 128

ADAM_LR = 0.001
ADAM_B1 = 0.9
ADAM_B2 = 0.999
ADAM_EPS = 1e-08
ADAM_WD = 0.01
ADAM_STEP = 10

LANES = 128
VMEM_LIMIT = 56 * 1024 * 1024
MESH = pl.DeviceIdType.MESH
HIGHEST = lax.Precision.HIGHEST

SHARDED = (("w_in", (D_MODEL, IN_SHARD)), ("w_out", (128, D_MODEL)), ("ple_gate_w", (128, D_MODEL)),
           ("ple_proj_w", (PLE_DIM, 128)), ("dn_conv_w", (CONV_K, 192)))
REPLICATED = (("norm_g", (1, D_MODEL)), ("sgu_ln_g", (1, SGU_WIDTH)), ("sgu_ln_b", (1, SGU_WIDTH)),
              ("sgu_w_s", (1, SGU_GROUPS, SGU_CHUNK, SGU_CHUNK)), ("sgu_b_s", (1, SGU_GROUPS, SGU_CHUNK)),
              ("dn_a_log", (1, DN_HEADS)), ("dn_dt_bias", (1, DN_HEADS)), ("dn_o_norm_g", (1, DN_HEAD_DIM)),
              ("ple_norm_g", (1, D_MODEL)), ("final_norm_g", (D_MODEL,)))
WEIGHT_ORDER = ("norm_g", "w_in", "sgu_ln_g", "sgu_ln_b", "sgu_w_s", "sgu_b_s", "dn_conv_w", "dn_a_log",
                "dn_dt_bias", "dn_o_norm_g", "w_out", "ple_norm_g", "ple_gate_w", "ple_proj_w", "final_norm_g")


def _size(shape):
    n = 1
    for s in shape:
        n *= s
    return n


N_SHARDED = sum(_size(s) for _, s in SHARDED)
N_PACK = N_SHARDED + sum(_size(s) for _, s in REPLICATED) + 1
PACK_BLOCK = 512
PACK_ROWS = -(-N_PACK // (PACK_BLOCK * LANES)) * PACK_BLOCK


def _bdot(a, b):
    return jnp.dot(a.astype(BF16), b.astype(BF16), preferred_element_type=F32)


def _bdot_nt(a, b):
    return lax.dot_general(a.astype(BF16), b.astype(BF16), (((1,), (1,)), ((), ())), preferred_element_type=F32)


def _bdot_tn(a, b):
    return lax.dot_general(a.astype(BF16), b.astype(BF16), (((0,), (0,)), ((), ())), preferred_element_type=F32)


def _hdot(a, b):
    return jnp.dot(a, b, precision=HIGHEST, preferred_element_type=F32)


def _hdot_nt(a, b):
    return lax.dot_general(a, b, (((1,), (1,)), ((), ())), precision=HIGHEST, preferred_element_type=F32)


def _hdot_tn(a, b):
    return lax.dot_general(a, b, (((0,), (0,)), ((), ())), precision=HIGHEST, preferred_element_type=F32)


def _silu(x):
    return x * jax.nn.sigmoid(x)


def _gelu(x):
    return 0.5 * x * (1.0 + lax.erf(x * (0.5 ** 0.5)))


def _softplus(x):
    return jnp.maximum(x, 0.0) + jnp.log1p(jnp.exp(-jnp.abs(x)))


def _l2n(x):
    return x * lax.rsqrt(jnp.sum(x * x, axis=-1, keepdims=True) + EPS)


def _rms(x):
    r = lax.rsqrt(jnp.mean(x * x, axis=-1, keepdims=True) + EPS)
    return x * r, r


def _rms_bwd(dn, n, r):
    return r * (dn - n * jnp.mean(dn * n, axis=-1, keepdims=True))


def _onehot_row(idx, width):
    return (lax.broadcasted_iota(jnp.int32, (1, width), 1) == idx).astype(F32)


def _rowsum(x):
    return jnp.sum(x, axis=0, keepdims=True)


def _iota2(n):
    return lax.broadcasted_iota(jnp.int32, (n, n), 0), lax.broadcasted_iota(jnp.int32, (n, n), 1)


def _tri_inv_impl(a):
    n = a.shape[0]
    r, c = _iota2(n)
    x = r ^ c
    blk16 = x < 16
    blk32 = x < 32
    eye = (r == c).astype(F32)
    ad = jnp.where(blk16, a, 0.0)
    p2 = _hdot(ad, ad)
    p4 = _hdot(p2, p2)
    p8 = _hdot(p4, p4)
    d = _hdot(_hdot(_hdot(eye - ad, eye + p2), eye + p4), eye + p8)
    m1 = jnp.where(jnp.logical_and(blk32, jnp.logical_not(blk16)), a, 0.0)
    d = d - _hdot(d, _hdot(m1, d))
    m2 = jnp.where(blk32, 0.0, a)
    return d - _hdot(d, _hdot(m2, d))


@jax.custom_vjp
def _tri_inv(a):
    return _tri_inv_impl(a)


def _tri_inv_fwd(a):
    t = _tri_inv_impl(a)
    return t, t


def _tri_inv_bwd(t, dt):
    return (-_hdot_tn(t, _hdot_nt(dt, t)),)


_tri_inv.defvjp(_tri_inv_fwd, _tri_inv_bwd)


def _sgu_core(u, v, z, lg, lb, ws, bcol):
    n = ws.shape[0]
    r, c = _iota2(n)
    wm = jnp.where(r >= c, ws, 0.0)
    gu = _gelu(u)
    gv = _gelu(v)
    xc = gv - jnp.mean(gv, axis=-1, keepdims=True)
    ln = xc * lax.rsqrt(jnp.mean(xc * xc, axis=-1, keepdims=True) + EPS) * lg + lb
    s = _bdot(wm, ln) + bcol
    return gu * s * _silu(z)


def _dn_core(cq, ck, cv, z, bl, al, state, alog, dtb, og):
    cn = cq.shape[0]
    dh = cq.shape[1]
    q = _l2n(_silu(cq)) * (dh ** -0.5)
    k = _l2n(_silu(ck))
    v = _silu(cv)
    beta = jax.nn.sigmoid(bl)
    g = -jnp.exp(alog) * _softplus(al + dtb)
    r, c = _iota2(cn)
    tril = r >= c
    lower = tril.astype(F32)
    upper = (r <= c).astype(F32)
    g_wide = jnp.broadcast_to(g, (cn, dh))
    g_sq = jnp.broadcast_to(g, (cn, cn))
    gc_wide = _hdot(lower, g_wide)
    gc_rows = _hdot(lower, g_sq)
    gc_cols = _hdot_tn(g_sq, upper)
    decay = jnp.exp(jnp.where(tril, gc_rows - gc_cols, -1e30))
    kb = k * beta
    kk = _bdot_nt(kb, k) * decay
    t = _tri_inv(jnp.where(r > c, kk, 0.0))
    eg = jnp.exp(gc_wide)
    u_val = _hdot(t, v * beta)
    w_dec = _hdot(t, kb * eg)
    qk = _bdot_nt(q, k) * decay
    g_last = _rowsum(g_wide)
    k_dec = k * jnp.exp(g_last - gc_wide)
    v_new = u_val - _bdot(w_dec, state)
    o = _bdot(q * eg, state) + _bdot(qk, v_new)
    new_state = state * jnp.exp(g_last) + _bdot_tn(k_dec, v_new)
    on, _ = _rms(o)
    return on * og * _silu(z), new_state


def _peer(k):
    x, y, c = lax.axis_index("x"), lax.axis_index("y"), lax.axis_index("c")
    px = 1 - x if k & 4 else x
    py = 1 - y if k & 2 else y
    pc = 1 - c if k & 1 else c
    return (px, py, pc), 4 * px + 2 * py + pc


def _my_index():
    return 4 * lax.axis_index("x") + 2 * lax.axis_index("y") + lax.axis_index("c")


def _all_gather(pack):
    rows, width = pack.shape

    def body(src_ref, out_ref, send_sems, recv_sems, local_sem):
        me = _my_index()
        mine = pltpu.make_async_copy(src_ref, out_ref.at[me], local_sem)
        mine.start()
        copies = []
        for k in range(1, N_DEV):
            peer, _ = _peer(k)
            cp = pltpu.make_async_remote_copy(src_ref=src_ref, dst_ref=out_ref.at[me], send_sem=send_sems.at[k - 1],
                                              recv_sem=recv_sems.at[k - 1], device_id=peer, device_id_type=MESH)
            cp.start()
            copies.append(cp)
        for k in range(1, N_DEV):
            peer, peer_index = _peer(k)
            pltpu.make_async_remote_copy(src_ref=src_ref, dst_ref=out_ref.at[peer_index], send_sem=send_sems.at[k - 1],
                                         recv_sem=recv_sems.at[k - 1], device_id=peer, device_id_type=MESH).wait_recv()
        for cp in copies:
            cp.wait_send()
        mine.wait()

    return pl.pallas_call(
        body, name="all_gather_weights",
        out_shape=jax.ShapeDtypeStruct((N_DEV, rows, width), pack.dtype),
        in_specs=[pl.BlockSpec(memory_space=pl.ANY)],
        out_specs=pl.BlockSpec(memory_space=pl.ANY),
        scratch_shapes=[pltpu.SemaphoreType.DMA((N_DEV - 1,)), pltpu.SemaphoreType.DMA((N_DEV - 1,)),
                        pltpu.SemaphoreType.DMA(())],
    )(pack)


def _all_to_all(send):
    def body(src_ref, out_ref, send_sems, recv_sems, local_sem):
        me = _my_index()
        mine = pltpu.make_async_copy(src_ref.at[me], out_ref.at[me], local_sem)
        mine.start()
        copies = []
        for k in range(1, N_DEV):
            peer, peer_index = _peer(k)
            cp = pltpu.make_async_remote_copy(src_ref=src_ref.at[peer_index], dst_ref=out_ref.at[me],
                                              send_sem=send_sems.at[k - 1], recv_sem=recv_sems.at[k - 1],
                                              device_id=peer, device_id_type=MESH)
            cp.start()
            copies.append(cp)
        for k in range(1, N_DEV):
            peer, peer_index = _peer(k)
            pltpu.make_async_remote_copy(src_ref=src_ref.at[peer_index], dst_ref=out_ref.at[peer_index],
                                         send_sem=send_sems.at[k - 1], recv_sem=recv_sems.at[k - 1],
                                         device_id=peer, device_id_type=MESH).wait_recv()
        for cp in copies:
            cp.wait_send()
        mine.wait()

    return pl.pallas_call(
        body, name="all_to_all_grads",
        out_shape=jax.ShapeDtypeStruct(send.shape, send.dtype),
        in_specs=[pl.BlockSpec(memory_space=pl.ANY)],
        out_specs=pl.BlockSpec(memory_space=pl.ANY),
        scratch_shapes=[pltpu.SemaphoreType.DMA((N_DEV - 1,)), pltpu.SemaphoreType.DMA((N_DEV - 1,)),
                        pltpu.SemaphoreType.DMA(())],
    )(send)


def _params(n_axes):
    return pltpu.CompilerParams(dimension_semantics=("arbitrary",) * n_axes, vmem_limit_bytes=VMEM_LIMIT)


def _whole(shape):
    return pl.BlockSpec(shape, lambda *_: (0,) * len(shape))


VMEM_SPEC = pl.BlockSpec(memory_space=pltpu.VMEM)
HBM_SPEC = pl.BlockSpec(memory_space=pl.ANY)


def _inproj_fwd(x2, norm_g, wa, wq, wz, wg):
    t = x2.shape[0]
    tm = min(512, t)

    def body(x_ref, g_ref, wa_ref, wq_ref, wz_ref, wg_ref, a_ref, q_ref, z_ref, l_ref):
        n, _ = _rms(x_ref[...])
        xn = (n * g_ref[...]).astype(BF16)
        for w_ref, o_ref in ((wa_ref, a_ref), (wq_ref, q_ref), (wz_ref, z_ref), (wg_ref, l_ref)):
            width = w_ref.shape[1]
            for c0 in range(0, width, 512):
                c1 = min(c0 + 512, width)
                o_ref[:, c0:c1] = jnp.dot(xn, w_ref[:, c0:c1], preferred_element_type=F32)

    widths = (wa.shape[1], wq.shape[1], wz.shape[1], wg.shape[1])
    return pl.pallas_call(
        body, name="inproj_fwd", grid=(t // tm,),
        out_shape=tuple(jax.ShapeDtypeStruct((t, w), F32) for w in widths),
        in_specs=[pl.BlockSpec((tm, D_MODEL), lambda i: (i, 0)), _whole((1, D_MODEL)),
                  VMEM_SPEC, VMEM_SPEC, VMEM_SPEC, VMEM_SPEC],
        out_specs=tuple(pl.BlockSpec((tm, w), lambda i: (i, 0)) for w in widths),
        compiler_params=_params(1),
    )(x2, norm_g, wa, wq, wz, wg)


def _sgu_pieces(uvz_ref, lg_ref, lb_ref, ws_ref, bt_ref, row0, grp):
    rows = pl.ds(row0, SGU_CHUNK)
    lanes = pl.ds(grp * 128, 128)
    u = uvz_ref[rows, pl.ds(grp * 128, 128)]
    v = uvz_ref[rows, pl.ds(SGU_WIDTH + grp * 128, 128)]
    z = uvz_ref[rows, pl.ds(2 * SGU_WIDTH + grp * 128, 128)]
    bcol = jnp.sum(bt_ref[...] * _onehot_row(grp, SGU_GROUPS), axis=-1, keepdims=True)
    return u, v, z, lg_ref[:, lanes], lb_ref[:, lanes], ws_ref[grp], bcol


def _sgu_fwd(a_uvz, ln_g, ln_b, w_s, b_t):
    t = a_uvz.shape[0]
    tm = min(512, t)

    def body(uvz_ref, lg_ref, lb_ref, ws_ref, bt_ref, out_ref):
        for row0 in range(0, tm, SGU_CHUNK):
            for grp in range(SGU_GROUPS):
                args = _sgu_pieces(uvz_ref, lg_ref, lb_ref, ws_ref, bt_ref, row0, grp)
                out_ref[pl.ds(row0, SGU_CHUNK), pl.ds(grp * 128, 128)] = _sgu_core(*args).astype(out_ref.dtype)

    return pl.pallas_call(
        body, name="sgu_fwd", grid=(t // tm,),
        out_shape=jax.ShapeDtypeStruct((t, SGU_WIDTH), BF16),
        in_specs=[pl.BlockSpec((tm, 3 * SGU_WIDTH), lambda i: (i, 0)), _whole((1, SGU_WIDTH)), _whole((1, SGU_WIDTH)),
                  _whole((SGU_GROUPS, SGU_CHUNK, SGU_CHUNK)), _whole((SGU_CHUNK, SGU_GROUPS))],
        out_specs=pl.BlockSpec((tm, SGU_WIDTH), lambda i: (i, 0)),
        compiler_params=_params(1),
    )(a_uvz, ln_g, ln_b, w_s, b_t)


def _sgu_bwd(a_uvz, d_out, ln_g, ln_b, w_s, b_t):
    t = a_uvz.shape[0]
    tm = min(512, t)

    def body(uvz_ref, do_ref, lg_ref, lb_ref, ws_ref, bt_ref, duvz_ref, dlg_ref, dlb_ref, dws_ref, dbt_ref):
        @pl.when(pl.program_id(0) == 0)
        def _():
            dlg_ref[...] = jnp.zeros_like(dlg_ref)
            dlb_ref[...] = jnp.zeros_like(dlb_ref)
            dws_ref[...] = jnp.zeros_like(dws_ref)
            dbt_ref[...] = jnp.zeros_like(dbt_ref)

        for row0 in range(0, tm, SGU_CHUNK):
            rows = pl.ds(row0, SGU_CHUNK)
            for grp in range(SGU_GROUPS):
                lanes = pl.ds(grp * 128, 128)
                args = _sgu_pieces(uvz_ref, lg_ref, lb_ref, ws_ref, bt_ref, row0, grp)
                _, pull = jax.vjp(_sgu_core, *args)
                du, dv, dz, dlg, dlb, dws, dbcol = pull(do_ref[rows, lanes])
                duvz_ref[rows, pl.ds(grp * 128, 128)] = du
                duvz_ref[rows, pl.ds(SGU_WIDTH + grp * 128, 128)] = dv
                duvz_ref[rows, pl.ds(2 * SGU_WIDTH + grp * 128, 128)] = dz
                dlg_ref[:, lanes] += dlg
                dlb_ref[:, lanes] += dlb
                dws_ref[grp] += dws
                dbt_ref[...] += dbcol * _onehot_row(grp, SGU_GROUPS)

    return pl.pallas_call(
        body, name="sgu_bwd", grid=(t // tm,),
        out_shape=(jax.ShapeDtypeStruct((t, 3 * SGU_WIDTH), F32), jax.ShapeDtypeStruct((1, SGU_WIDTH), F32),
                   jax.ShapeDtypeStruct((1, SGU_WIDTH), F32), jax.ShapeDtypeStruct((SGU_GROUPS, SGU_CHUNK, SGU_CHUNK), F32),
                   jax.ShapeDtypeStruct((SGU_CHUNK, SGU_GROUPS), F32)),
        in_specs=[pl.BlockSpec((tm, 3 * SGU_WIDTH), lambda i: (i, 0)), pl.BlockSpec((tm, SGU_WIDTH), lambda i: (i, 0)),
                  _whole((1, SGU_WIDTH)), _whole((1, SGU_WIDTH)),
                  _whole((SGU_GROUPS, SGU_CHUNK, SGU_CHUNK)), _whole((SGU_CHUNK, SGU_GROUPS))],
        out_specs=(pl.BlockSpec((tm, 3 * SGU_WIDTH), lambda i: (i, 0)), _whole((1, SGU_WIDTH)), _whole((1, SGU_WIDTH)),
                   _whole((SGU_GROUPS, SGU_CHUNK, SGU_CHUNK)), _whole((SGU_CHUNK, SGU_GROUPS))),
        compiler_params=_params(1),
    )(a_uvz, d_out, ln_g, ln_b, w_s, b_t)


def _dn_conv(xpad_ref, cur_ref, prev_ref, w_ref, first):
    xpad_ref[0:CONV_HALO, :] = jnp.where(first, 0.0, prev_ref[...])
    xpad_ref[CONV_HALO:, :] = cur_ref[...]
    acc = None
    for j in range(CONV_K):
        term = w_ref[j:j + 1, :] * xpad_ref[pl.ds(CONV_HALO - CONV_K + 1 + j, DN_CHUNK), :]
        acc = term if acc is None else acc + term
    return acc


def _dn_head_args(c_ref, z_ref, l_ref, alog_ref, dtb_ref, h):
    lanes = pl.ds(h * DN_HEAD_DIM, DN_HEAD_DIM)
    cq = c_ref[:, lanes]
    ck = c_ref[:, pl.ds(DN_WIDTH + h * DN_HEAD_DIM, DN_HEAD_DIM)]
    cv = c_ref[:, pl.ds(2 * DN_WIDTH + h * DN_HEAD_DIM, DN_HEAD_DIM)]
    logits = l_ref[...]
    bl = jnp.sum(logits * _onehot_row(h, GATE_PAD), axis=-1, keepdims=True)
    al = jnp.sum(logits * _onehot_row(DN_HEADS + h, GATE_PAD), axis=-1, keepdims=True)
    alog = jnp.sum(alog_ref[...] * _onehot_row(h, GATE_PAD), axis=-1, keepdims=True)
    dtb = jnp.sum(dtb_ref[...] * _onehot_row(h, GATE_PAD), axis=-1, keepdims=True)
    return cq, ck, cv, z_ref[:, lanes], bl, al, alog, dtb


def _dn_in_specs(order):
    prev = lambda b, n: (b, jnp.maximum(order(n) * (DN_CHUNK // CONV_HALO) - 1, 0), 0)
    return [pl.BlockSpec((None, DN_CHUNK, 3 * DN_WIDTH), lambda b, n: (b, order(n), 0)),
            pl.BlockSpec((None, CONV_HALO, 3 * DN_WIDTH), prev),
            pl.BlockSpec((None, DN_CHUNK, DN_WIDTH), lambda b, n: (b, order(n), 0)),
            pl.BlockSpec((None, DN_CHUNK, GATE_PAD), lambda b, n: (b, order(n), 0)),
            _whole((CONV_K, 3 * DN_WIDTH)), _whole((1, GATE_PAD)), _whole((1, GATE_PAD)), _whole((1, DN_HEAD_DIM))]


def _dn_fwd(qkv, zg, logits, conv_w, alog, dtb, og):
    nb, s, _ = qkv.shape
    nc = s // DN_CHUNK

    def body(cur_ref, prev_ref, z_ref, l_ref, w_ref, alog_ref, dtb_ref, og_ref, out_ref, st_ref,
             state_ref, xpad_ref, c_ref):
        n = pl.program_id(1)

        @pl.when(n == 0)
        def _():
            state_ref[...] = jnp.zeros_like(state_ref)

        c_ref[...] = _dn_conv(xpad_ref, cur_ref, prev_ref, w_ref, n == 0)
        for h in range(DN_HEADS):
            cq, ck, cv, z, bl, al, a_h, d_h = _dn_head_args(c_ref, z_ref, l_ref, alog_ref, dtb_ref, h)
            state = state_ref[h]
            st_ref[h] = state
            out, new_state = _dn_core(cq, ck, cv, z, bl, al, state, a_h, d_h, og_ref[...])
            state_ref[h] = new_state
            out_ref[:, pl.ds(h * DN_HEAD_DIM, DN_HEAD_DIM)] = out.astype(out_ref.dtype)

    return pl.pallas_call(
        body, name="deltanet_fwd", grid=(nb, nc),
        out_shape=(jax.ShapeDtypeStruct((nb, s, DN_WIDTH), BF16),
                   jax.ShapeDtypeStruct((nb, nc, DN_HEADS, DN_HEAD_DIM, DN_HEAD_DIM), F32)),
        in_specs=_dn_in_specs(lambda n: n),
        out_specs=(pl.BlockSpec((None, DN_CHUNK, DN_WIDTH), lambda b, n: (b, n, 0)),
                   pl.BlockSpec((None, None, DN_HEADS, DN_HEAD_DIM, DN_HEAD_DIM), lambda b, n: (b, n, 0, 0, 0))),
        scratch_shapes=[pltpu.VMEM((DN_HEADS, DN_HEAD_DIM, DN_HEAD_DIM), F32),
                        pltpu.VMEM((CONV_HALO + DN_CHUNK, 3 * DN_WIDTH), F32),
                        pltpu.VMEM((DN_CHUNK, 3 * DN_WIDTH), F32)],
        compiler_params=_params(2),
    )(qkv, qkv, zg, logits, conv_w, alog, dtb, og)


def _dn_bwd(qkv, zg, logits, conv_w, alog, dtb, og, states, d_out):
    nb, s, _ = qkv.shape
    nc = s // DN_CHUNK
    rev = lambda n: nc - 1 - n

    def body(cur_ref, prev_ref, z_ref, l_ref, w_ref, alog_ref, dtb_ref, og_ref, st_ref, do_ref,
             dqkv_ref, dz_ref, dl_ref, dw_ref, dalog_ref, ddtb_ref, dog_ref,
             dstate_ref, xpad_ref, c_ref, dcpad_ref):
        b = pl.program_id(0)
        n = pl.program_id(1)

        @pl.when(jnp.logical_and(b == 0, n == 0))
        def _():
            dw_ref[...] = jnp.zeros_like(dw_ref)
            dalog_ref[...] = jnp.zeros_like(dalog_ref)
            ddtb_ref[...] = jnp.zeros_like(ddtb_ref)
            dog_ref[...] = jnp.zeros_like(dog_ref)

        @pl.when(n == 0)
        def _():
            dstate_ref[...] = jnp.zeros_like(dstate_ref)
            dcpad_ref[DN_CHUNK:, :] = jnp.zeros((CONV_HALO, 3 * DN_WIDTH), F32)

        c_ref[...] = _dn_conv(xpad_ref, cur_ref, prev_ref, w_ref, n == nc - 1)
        dlog = jnp.zeros((DN_CHUNK, GATE_PAD), F32)
        for h in range(DN_HEADS):
            lanes = pl.ds(h * DN_HEAD_DIM, DN_HEAD_DIM)
            cq, ck, cv, z, bl, al, a_h, d_h = _dn_head_args(c_ref, z_ref, l_ref, alog_ref, dtb_ref, h)
            _, pull = jax.vjp(_dn_core, cq, ck, cv, z, bl, al, st_ref[h], a_h, d_h, og_ref[...])
            dcq, dck, dcv, dz, dbl, dal, dstate, da_h, dd_h, dog = pull((do_ref[:, lanes], dstate_ref[h]))
            dstate_ref[h] = dstate
            dcpad_ref[0:DN_CHUNK, lanes] = dcq
            dcpad_ref[0:DN_CHUNK, pl.ds(DN_WIDTH + h * DN_HEAD_DIM, DN_HEAD_DIM)] = dck
            dcpad_ref[0:DN_CHUNK, pl.ds(2 * DN_WIDTH + h * DN_HEAD_DIM, DN_HEAD_DIM)] = dcv
            dz_ref[:, lanes] = dz
            dlog = dlog + dbl * _onehot_row(h, GATE_PAD) + dal * _onehot_row(DN_HEADS + h, GATE_PAD)
            dalog_ref[...] += da_h * _onehot_row(h, GATE_PAD)
            ddtb_ref[...] += dd_h * _onehot_row(h, GATE_PAD)
            dog_ref[...] += dog
        dl_ref[...] = dlog
        dc = dcpad_ref[0:DN_CHUNK, :]
        dx = None
        for j in range(CONV_K):
            term = w_ref[j:j + 1, :] * dcpad_ref[pl.ds(CONV_K - 1 - j, DN_CHUNK), :]
            dx = term if dx is None else dx + term
            dw_ref[j:j + 1, :] += _rowsum(dc * xpad_ref[pl.ds(CONV_HALO - CONV_K + 1 + j, DN_CHUNK), :])
        dqkv_ref[...] = dx
        dcpad_ref[DN_CHUNK:, :] = dcpad_ref[0:CONV_HALO, :]

    chunk = lambda w: pl.BlockSpec((None, DN_CHUNK, w), lambda b, n: (b, rev(n), 0))
    return pl.pallas_call(
        body, name="deltanet_bwd", grid=(nb, nc),
        out_shape=(jax.ShapeDtypeStruct((nb, s, 3 * DN_WIDTH), F32), jax.ShapeDtypeStruct((nb, s, DN_WIDTH), F32),
                   jax.ShapeDtypeStruct((nb, s, GATE_PAD), F32), jax.ShapeDtypeStruct((CONV_K, 3 * DN_WIDTH), F32),
                   jax.ShapeDtypeStruct((1, GATE_PAD), F32), jax.ShapeDtypeStruct((1, GATE_PAD), F32),
                   jax.ShapeDtypeStruct((1, DN_HEAD_DIM), F32)),
        in_specs=_dn_in_specs(rev) + [
            pl.BlockSpec((None, None, DN_HEADS, DN_HEAD_DIM, DN_HEAD_DIM), lambda b, n: (b, rev(n), 0, 0, 0)),
            chunk(DN_WIDTH)],
        out_specs=(chunk(3 * DN_WIDTH), chunk(DN_WIDTH), chunk(GATE_PAD), _whole((CONV_K, 3 * DN_WIDTH)),
                   _whole((1, GATE_PAD)), _whole((1, GATE_PAD)), _whole((1, DN_HEAD_DIM))),
        scratch_shapes=[pltpu.VMEM((DN_HEADS, DN_HEAD_DIM, DN_HEAD_DIM), F32),
                        pltpu.VMEM((CONV_HALO + DN_CHUNK, 3 * DN_WIDTH), F32),
                        pltpu.VMEM((DN_CHUNK, 3 * DN_WIDTH), F32),
                        pltpu.VMEM((DN_CHUNK + CONV_HALO, 3 * DN_WIDTH), F32)],
        compiler_params=_params(2),
    )(qkv, qkv, zg, logits, conv_w, alog, dtb, og, states, d_out)


def _head(a_out, b_out, x2, p2, target, w_out, w_out_t, w_gate, w_gate_t, w_proj, ple_g, fin_g):
    t = x2.shape[0]
    tm = min(256, t)
    steps = t // tm

    def body(a_ref, b_ref, x_ref, p_ref, y_ref, wo_ref, wot_ref, wg_ref, wgt_ref, wp_ref, pg_ref, fg_ref,
             da_ref, db_ref, dh_ref, dwo_hbm, dwg_hbm, dwp_hbm, dpg_ref, dfg_ref, loss_ref,
             dwo_acc, dwg_acc, dwp_acc):
        i = pl.program_id(0)

        @pl.when(i == 0)
        def _():
            dwo_acc[...] = jnp.zeros_like(dwo_acc)
            dwg_acc[...] = jnp.zeros_like(dwg_acc)
            dwp_acc[...] = jnp.zeros_like(dwp_acc)
            dpg_ref[...] = jnp.zeros_like(dpg_ref)
            dfg_ref[...] = jnp.zeros_like(dfg_ref)
            loss_ref[...] = jnp.zeros_like(loss_ref)

        a = a_ref[...]
        bb = b_ref[...]
        pb = p_ref[...].astype(BF16)
        pg = pg_ref[...]
        fg = fg_ref[...]
        h1 = (x_ref[...] + jnp.dot(a, wo_ref[0:SGU_WIDTH, :], preferred_element_type=F32)
              + jnp.dot(bb, wo_ref[SGU_WIDTH:, :], preferred_element_type=F32))
        n1, r1 = _rms(h1)
        rn = (n1 * pg).astype(BF16)
        gate = jax.nn.sigmoid(jnp.dot(rn, wg_ref[...], preferred_element_type=F32))
        pp = jnp.dot(pb, wp_ref[...], preferred_element_type=F32)
        h2 = h1 + gate * pp
        n2, r2 = _rms(h2)
        err = n2 * fg - y_ref[...]
        loss_ref[...] += jnp.broadcast_to(_rowsum(jnp.sum(err * err, axis=-1, keepdims=True)), loss_ref.shape)

        dy = err * (1.0 / D_MODEL)
        dfg_ref[...] += _rowsum(dy * n2)
        dh2 = _rms_bwd(dy * fg, n2, r2)
        dpp = (dh2 * gate).astype(BF16)
        dgl = (dh2 * pp * gate * (1.0 - gate)).astype(BF16)
        dwp_acc[...] += lax.dot_general(pb, dpp, (((0,), (0,)), ((), ())), preferred_element_type=F32)
        dwg_acc[...] += lax.dot_general(rn, dgl, (((0,), (0,)), ((), ())), preferred_element_type=F32)
        drn = jnp.dot(dgl, wgt_ref[...], preferred_element_type=F32)
        dpg_ref[...] += _rowsum(drn * n1)
        dh1 = dh2 + _rms_bwd(drn * pg, n1, r1)
        dh_ref[...] = dh1
        dhb = dh1.astype(BF16)
        da_ref[...] = jnp.dot(dhb, wot_ref[:, 0:SGU_WIDTH], preferred_element_type=F32)
        db_ref[...] = jnp.dot(dhb, wot_ref[:, SGU_WIDTH:], preferred_element_type=F32)
        dwo_acc[0:SGU_WIDTH, :] += lax.dot_general(a, dhb, (((0,), (0,)), ((), ())), preferred_element_type=F32)
        dwo_acc[SGU_WIDTH:, :] += lax.dot_general(bb, dhb, (((0,), (0,)), ((), ())), preferred_element_type=F32)

        @pl.when(i == steps - 1)
        def _():
            pltpu.sync_copy(dwo_acc, dwo_hbm)
            pltpu.sync_copy(dwg_acc, dwg_hbm)
            pltpu.sync_copy(dwp_acc, dwp_hbm)

    tile = lambda w: pl.BlockSpec((tm, w), lambda i: (i, 0))
    return pl.pallas_call(
        body, name="head_fwd_bwd", grid=(steps,),
        out_shape=(jax.ShapeDtypeStruct((t, SGU_WIDTH), F32), jax.ShapeDtypeStruct((t, DN_WIDTH), F32),
                   jax.ShapeDtypeStruct((t, D_MODEL), F32), jax.ShapeDtypeStruct((D_MODEL, D_MODEL), F32),
                   jax.ShapeDtypeStruct((D_MODEL, D_MODEL), F32), jax.ShapeDtypeStruct((PLE_DIM, D_MODEL), F32),
                   jax.ShapeDtypeStruct((1, D_MODEL), F32), jax.ShapeDtypeStruct((1, D_MODEL), F32),
                   jax.ShapeDtypeStruct((8, LANES), F32)),
        in_specs=[tile(SGU_WIDTH), tile(DN_WIDTH), tile(D_MODEL), tile(PLE_DIM), tile(D_MODEL),
                  VMEM_SPEC, VMEM_SPEC, VMEM_SPEC, VMEM_SPEC, VMEM_SPEC, _whole((1, D_MODEL)), _whole((1, D_MODEL))],
        out_specs=(tile(SGU_WIDTH), tile(DN_WIDTH), tile(D_MODEL), HBM_SPEC, HBM_SPEC, HBM_SPEC,
                   _whole((1, D_MODEL)), _whole((1, D_MODEL)), _whole((8, LANES))),
        scratch_shapes=[pltpu.VMEM((D_MODEL, D_MODEL), F32), pltpu.VMEM((D_MODEL, D_MODEL), F32),
                        pltpu.VMEM((PLE_DIM, D_MODEL), F32)],
        compiler_params=_params(1),
    )(a_out, b_out, x2, p2, target, w_out, w_out_t, w_gate, w_gate_t, w_proj, ple_g, fin_g)


def _inproj_bwd(x2, dh1, d_a, d_q, d_z, d_l, norm_g, wat, wqt, wzt, wgt):
    t = x2.shape[0]
    tm = min(256, t)
    steps = t // tm

    def body(x_ref, dh_ref, da_ref, dq_ref, dz_ref, dl_ref, g_ref, wat_ref, wqt_ref, wzt_ref, wgt_ref,
             dx_ref, dwa_hbm, dwq_hbm, dwz_hbm, dwg_hbm, dg_ref, dwa_acc, dwq_acc, dwz_acc, dwg_acc):
        i = pl.program_id(0)

        @pl.when(i == 0)
        def _():
            for acc in (dwa_acc, dwq_acc, dwz_acc, dwg_acc, dg_ref):
                acc[...] = jnp.zeros_like(acc)

        g = g_ref[...]
        n, r = _rms(x_ref[...])
        xn = (n * g).astype(BF16)
        dxn = None
        for d_ref, wt_ref, acc in ((da_ref, wat_ref, dwa_acc), (dq_ref, wqt_ref, dwq_acc),
                                   (dz_ref, wzt_ref, dwz_acc), (dl_ref, wgt_ref, dwg_acc)):
            width = d_ref.shape[1]
            for c0 in range(0, width, 512):
                c1 = min(c0 + 512, width)
                d = d_ref[:, c0:c1].astype(BF16)
                term = jnp.dot(d, wt_ref[c0:c1, :], preferred_element_type=F32)
                dxn = term if dxn is None else dxn + term
                acc[:, c0:c1] += lax.dot_general(xn, d, (((0,), (0,)), ((), ())), preferred_element_type=F32)
        dg_ref[...] += _rowsum(dxn * n)
        dx_ref[...] = dh_ref[...] + _rms_bwd(dxn * g, n, r)

        @pl.when(i == steps - 1)
        def _():
            pltpu.sync_copy(dwa_acc, dwa_hbm)
            pltpu.sync_copy(dwq_acc, dwq_hbm)
            pltpu.sync_copy(dwz_acc, dwz_hbm)
            pltpu.sync_copy(dwg_acc, dwg_hbm)

    widths = (d_a.shape[1], d_q.shape[1], d_z.shape[1], d_l.shape[1])
    tile = lambda w: pl.BlockSpec((tm, w), lambda i: (i, 0))
    return pl.pallas_call(
        body, name="inproj_bwd", grid=(steps,),
        out_shape=(jax.ShapeDtypeStruct((t, D_MODEL), F32),) + tuple(jax.ShapeDtypeStruct((D_MODEL, w), F32) for w in widths)
        + (jax.ShapeDtypeStruct((1, D_MODEL), F32),),
        in_specs=[tile(D_MODEL), tile(D_MODEL)] + [tile(w) for w in widths] + [_whole((1, D_MODEL))] + [VMEM_SPEC] * 4,
        out_specs=(tile(D_MODEL), HBM_SPEC, HBM_SPEC, HBM_SPEC, HBM_SPEC, _whole((1, D_MODEL))),
        scratch_shapes=[pltpu.VMEM((D_MODEL, w), F32) for w in widths],
        compiler_params=_params(1),
    )(x2, dh1, d_a, d_q, d_z, d_l, norm_g, wat, wqt, wzt, wgt)


def _reduce_adamw(recv, w, m, v):
    rows = w.shape[0]

    def body(r_ref, w_ref, m_ref, v_ref, g_ref, d_ref, nm_ref, nv_ref):
        g = r_ref[0]
        for i in range(1, N_DEV):
            g = g + r_ref[i]
        m_new = ADAM_B1 * m_ref[...] + (1.0 - ADAM_B1) * g
        v_new = ADAM_B2 * v_ref[...] + (1.0 - ADAM_B2) * jnp.square(g)
        m_hat = m_new / (1.0 - ADAM_B1 ** ADAM_STEP)
        v_hat = v_new / (1.0 - ADAM_B2 ** ADAM_STEP)
        g_ref[...] = g
        d_ref[...] = -ADAM_LR * (m_hat / (jnp.sqrt(v_hat) + ADAM_EPS) + ADAM_WD * w_ref[...])
        nm_ref[...] = m_new
        nv_ref[...] = v_new

    blk = pl.BlockSpec((PACK_BLOCK, LANES), lambda i: (i, 0))
    return pl.pallas_call(
        body, name="reduce_adamw", grid=(rows // PACK_BLOCK,),
        out_shape=tuple(jax.ShapeDtypeStruct((rows, LANES), F32) for _ in range(4)),
        in_specs=[pl.BlockSpec((N_DEV, PACK_BLOCK, LANES), lambda i: (0, i, 0)), blk, blk, blk],
        out_specs=(blk, blk, blk, blk),
        compiler_params=_params(1),
    )(recv, w, m, v)


def _pack_rows(pieces, rows, dtype):
    flat = jnp.concatenate([jnp.ravel(p).astype(dtype) for p in pieces])
    flat = jnp.pad(flat, (0, rows * LANES - flat.shape[0]))
    return flat.reshape(rows, LANES)


def _unpack(pack, layout):
    flat = pack.reshape(-1)
    out, off = {}, 0
    for name, shape in layout:
        n = _size(shape)
        out[name] = flat[off:off + n].reshape(shape)
        off += n
    return out


def _by_device(full, axis):
    parts = full.shape[axis] // N_DEV
    shape = full.shape[:axis] + (N_DEV, parts) + full.shape[axis + 1:]
    return jnp.moveaxis(full.reshape(shape), axis, 0)


def kernel(x, p, norm_g, w_in, sgu_ln_g, sgu_ln_b, sgu_w_s, sgu_b_s, dn_conv_w, dn_a_log, dn_dt_bias, dn_o_norm_g, w_out, ple_norm_g, ple_gate_w, ple_proj_w, final_norm_g, loss_target, m_norm_g, m_w_in, m_sgu_ln_g, m_sgu_ln_b, m_sgu_w_s, m_sgu_b_s, m_dn_conv_w, m_dn_a_log, m_dn_dt_bias, m_dn_o_norm_g, m_w_out, m_ple_norm_g, m_ple_gate_w, m_ple_proj_w, m_final_norm_g, v_norm_g, v_w_in, v_sgu_ln_g, v_sgu_ln_b, v_sgu_w_s, v_sgu_b_s, v_dn_conv_w, v_dn_a_log, v_dn_dt_bias, v_dn_o_norm_g, v_w_out, v_ple_norm_g, v_ple_gate_w, v_ple_proj_w, v_final_norm_g):
    weights = dict(norm_g=norm_g, w_in=w_in, sgu_ln_g=sgu_ln_g, sgu_ln_b=sgu_ln_b, sgu_w_s=sgu_w_s, sgu_b_s=sgu_b_s,
                   dn_conv_w=dn_conv_w, dn_a_log=dn_a_log, dn_dt_bias=dn_dt_bias, dn_o_norm_g=dn_o_norm_g, w_out=w_out,
                   ple_norm_g=ple_norm_g, ple_gate_w=ple_gate_w, ple_proj_w=ple_proj_w, final_norm_g=final_norm_g)
    mom1 = dict(norm_g=m_norm_g, w_in=m_w_in, sgu_ln_g=m_sgu_ln_g, sgu_ln_b=m_sgu_ln_b, sgu_w_s=m_sgu_w_s,
                sgu_b_s=m_sgu_b_s, dn_conv_w=m_dn_conv_w, dn_a_log=m_dn_a_log, dn_dt_bias=m_dn_dt_bias,
                dn_o_norm_g=m_dn_o_norm_g, w_out=m_w_out, ple_norm_g=m_ple_norm_g, ple_gate_w=m_ple_gate_w,
                ple_proj_w=m_ple_proj_w, final_norm_g=m_final_norm_g)
    mom2 = dict(norm_g=v_norm_g, w_in=v_w_in, sgu_ln_g=v_sgu_ln_g, sgu_ln_b=v_sgu_ln_b, sgu_w_s=v_sgu_w_s,
                sgu_b_s=v_sgu_b_s, dn_conv_w=v_dn_conv_w, dn_a_log=v_dn_a_log, dn_dt_bias=v_dn_dt_bias,
                dn_o_norm_g=v_dn_o_norm_g, w_out=v_w_out, ple_norm_g=v_ple_norm_g, ple_gate_w=v_ple_gate_w,
                ple_proj_w=v_ple_proj_w, final_norm_g=v_final_norm_g)
    layout = tuple((name, weights[name].shape) for name, _ in SHARDED + REPLICATED)
    nb, s, _ = x.shape
    t = nb * s

    conv_hi = dn_conv_w.astype(BF16)
    conv_mid = (dn_conv_w - conv_hi.astype(F32)).astype(BF16)
    conv_lo = (dn_conv_w - conv_hi.astype(F32) - conv_mid.astype(F32)).astype(BF16)
    ag_pieces = [("w_in", w_in), ("w_out", w_out), ("ple_gate_w", ple_gate_w), ("ple_proj_w", ple_proj_w),
                 ("conv_hi", conv_hi), ("conv_mid", conv_mid), ("conv_lo", conv_lo)]
    ag_rows = -(-sum(a.size for _, a in ag_pieces) // (16 * LANES)) * 16
    gathered = _all_gather(_pack_rows([a for _, a in ag_pieces], ag_rows, BF16))
    flat = gathered.reshape(N_DEV, -1)
    full, off = {}, 0
    for name, a in ag_pieces:
        shape = a.shape[1:]
        full[name] = flat[:, off:off + a.size].reshape((N_DEV,) + shape)
        off += a.size
    full["dn_conv_w"] = full["conv_hi"].astype(F32) + full["conv_mid"].astype(F32) + full["conv_lo"].astype(F32)
    w_in_full = jnp.moveaxis(full["w_in"], 0, 1).reshape(D_MODEL, IN_COLS)
    wa = w_in_full[:, :3 * SGU_WIDTH]
    wq = w_in_full[:, 3 * SGU_WIDTH:3 * SGU_WIDTH + 3 * DN_WIDTH]
    wz = w_in_full[:, 3 * SGU_WIDTH + 3 * DN_WIDTH:3 * SGU_WIDTH + 4 * DN_WIDTH]
    wg = jnp.pad(w_in_full[:, 3 * SGU_WIDTH + 4 * DN_WIDTH:], ((0, 0), (0, GATE_PAD - 2 * DN_HEADS)))
    w_out_full = full["w_out"].reshape(D_MODEL, D_MODEL)
    w_gate_full = full["ple_gate_w"].reshape(D_MODEL, D_MODEL)
    w_proj_full = jnp.moveaxis(full["ple_proj_w"], 0, 1).reshape(PLE_DIM, D_MODEL)
    conv_full = jnp.moveaxis(full["dn_conv_w"], 0, 1).reshape(CONV_K, 3 * DN_WIDTH)

    pad_row = lambda a: jnp.pad(a.reshape(1, -1), ((0, 0), (0, GATE_PAD - a.size)))
    alog, dtb = pad_row(dn_a_log), pad_row(dn_dt_bias)
    og = dn_o_norm_g.reshape(1, DN_HEAD_DIM)
    ws = sgu_w_s.reshape(SGU_GROUPS, SGU_CHUNK, SGU_CHUNK)
    b_t = sgu_b_s.reshape(SGU_GROUPS, SGU_CHUNK).T
    fin_g = final_norm_g.reshape(1, D_MODEL)

    x2 = x.reshape(t, D_MODEL)
    a_uvz, b_qkv, b_z, b_l = _inproj_fwd(x2, norm_g, wa, wq, wz, wg)
    a_out = _sgu_fwd(a_uvz, sgu_ln_g, sgu_ln_b, ws, b_t)
    qkv3 = b_qkv.reshape(nb, s, 3 * DN_WIDTH)
    z3 = b_z.reshape(nb, s, DN_WIDTH)
    l3 = b_l.reshape(nb, s, GATE_PAD)
    b_out, states = _dn_fwd(qkv3, z3, l3, conv_full, alog, dtb, og)

    d_a, d_b, dh1, g_w_out, g_gate, g_proj, g_ple_g, g_fin_g, loss_tile = _head(
        a_out, b_out.reshape(t, DN_WIDTH), x2, p.reshape(t, PLE_DIM), loss_target.reshape(t, D_MODEL),
        w_out_full, w_out_full.T, w_gate_full, w_gate_full.T, w_proj_full, ple_norm_g, fin_g)
    d_qkv, d_z, d_l, g_conv, g_alog, g_dtb, g_og = _dn_bwd(
        qkv3, z3, l3, conv_full, alog, dtb, og, states, d_b.reshape(nb, s, DN_WIDTH))
    d_uvz, g_ln_g, g_ln_b, g_ws, g_bt = _sgu_bwd(a_uvz, d_a, sgu_ln_g, sgu_ln_b, ws, b_t)
    grad_x, g_wa, g_wq, g_wz, g_wg, g_norm = _inproj_bwd(
        x2, dh1, d_uvz, d_qkv.reshape(t, 3 * DN_WIDTH), d_z.reshape(t, DN_WIDTH), d_l.reshape(t, GATE_PAD),
        norm_g, wa.T, wq.T, wz.T, wg.T)

    g_w_in = jnp.concatenate([g_wa, g_wq, g_wz, g_wg[:, :2 * DN_HEADS]], axis=1)
    sharded_parts = [_by_device(g_w_in, 1), g_w_out.reshape(N_DEV, 128, D_MODEL), g_gate.reshape(N_DEV, 128, D_MODEL),
                     _by_device(g_proj, 1), _by_device(g_conv, 1)]
    small = [g_norm, g_ln_g, g_ln_b, g_ws, g_bt.T, g_alog[:, :DN_HEADS], g_dtb[:, :DN_HEADS], g_og, g_ple_g, g_fin_g,
             (0.5 / D_MODEL) * loss_tile[0:1, 0:1]]
    small_flat = jnp.concatenate([jnp.ravel(a) for a in small])
    send = jnp.concatenate([a.reshape(N_DEV, -1) for a in sharded_parts]
                           + [jnp.broadcast_to(small_flat, (N_DEV, small_flat.shape[0]))], axis=1)
    send = jnp.pad(send, ((0, 0), (0, PACK_ROWS * LANES - send.shape[1]))).reshape(N_DEV, PACK_ROWS, LANES)
    recv = _all_to_all(send)

    order = [name for name, _ in SHARDED + REPLICATED]
    g_pack, d_pack, m_pack, v_pack = _reduce_adamw(
        recv, _pack_rows([weights[k] for k in order], PACK_ROWS, F32),
        _pack_rows([mom1[k] for k in order], PACK_ROWS, F32), _pack_rows([mom2[k] for k in order], PACK_ROWS, F32))
    grads, deltas, new_m, new_v = (_unpack(a, layout) for a in (g_pack, d_pack, m_pack, v_pack))
    loss = g_pack.reshape(-1)[N_PACK - 1]

    return (loss, grad_x.reshape(nb, s, D_MODEL), *[grads[k] for k in WEIGHT_ORDER], *[deltas[k] for k in WEIGHT_ORDER],
            *[new_m[k] for k in WEIGHT_ORDER], *[new_v[k] for k in WEIGHT_ORDER])
```

```python
import jax
import jax.numpy as jnp
from jax import lax
from jax.experimental import pallas as pl
from jax.experimental.pallas import tpu as pltpu

F32 = jnp.float32
BF16 = jnp.bfloat16

N_DEV = 8
D_MODEL = 1024
SGU_WIDTH = 512
SGU_GROUPS = 4
SGU_CHUNK = 128
DN_WIDTH = 512
DN_HEADS = 4
DN_HEAD_DIM = 128
DN_CHUNK = 64
CONV_K = 4
CONV_HALO = 8
PLE_DIM = 256
EPS = 1e-6
IN_COLS = 3592
IN_SHARD = IN_COLS // N_DEV
GATE_PAD = 128

ADAM_LR = 0.001
ADAM_B1 = 0.9
ADAM_B2 = 0.999
ADAM_EPS = 1e-08
ADAM_WD = 0.01
ADAM_STEP = 10

LANES = 128
VMEM_LIMIT = 56 * 1024 * 1024
MESH = pl.DeviceIdType.MESH
HIGHEST = lax.Precision.HIGHEST

SHARDED = (("w_in", (D_MODEL, IN_SHARD)), ("w_out", (128, D_MODEL)), ("ple_gate_w", (128, D_MODEL)),
           ("ple_proj_w", (PLE_DIM, 128)), ("dn_conv_w", (CONV_K, 192)))
REPLICATED = (("norm_g", (1, D_MODEL)), ("sgu_ln_g", (1, SGU_WIDTH)), ("sgu_ln_b", (1, SGU_WIDTH)),
              ("sgu_w_s", (1, SGU_GROUPS, SGU_CHUNK, SGU_CHUNK)), ("sgu_b_s", (1, SGU_GROUPS, SGU_CHUNK)),
              ("dn_a_log", (1, DN_HEADS)), ("dn_dt_bias", (1, DN_HEADS)), ("dn_o_norm_g", (1, DN_HEAD_DIM)),
              ("ple_norm_g", (1, D_MODEL)), ("final_norm_g", (D_MODEL,)))
WEIGHT_ORDER = ("norm_g", "w_in", "sgu_ln_g", "sgu_ln_b", "sgu_w_s", "sgu_b_s", "dn_conv_w", "dn_a_log",
                "dn_dt_bias", "dn_o_norm_g", "w_out", "ple_norm_g", "ple_gate_w", "ple_proj_w", "final_norm_g")


def _size(shape):
    n = 1
    for s in shape:
        n *= s
    return n


N_SHARDED = sum(_size(s) for _, s in SHARDED)
N_PACK = N_SHARDED + sum(_size(s) for _, s in REPLICATED) + 1
PACK_BLOCK = 512
PACK_ROWS = -(-N_PACK // (PACK_BLOCK * LANES)) * PACK_BLOCK


def _bdot(a, b):
    return jnp.dot(a.astype(BF16), b.astype(BF16), preferred_element_type=F32)


def _hdot(a, b):
    return jnp.dot(a, b, precision=HIGHEST, preferred_element_type=F32)


def _silu(x):
    return x * jax.nn.sigmoid(x)


def _gelu(x):
    return 0.5 * x * (1.0 + lax.erf(x * (0.5 ** 0.5)))


def _softplus(x):
    return jnp.maximum(x, 0.0) + jnp.log1p(jnp.exp(-jnp.abs(x)))


def _l2n(x):
    return x * lax.rsqrt(jnp.sum(x * x, axis=-1, keepdims=True) + EPS)


def _rms(x):
    r = lax.rsqrt(jnp.mean(x * x, axis=-1, keepdims=True) + EPS)
    return x * r, r


def _rms_bwd(dn, n, r):
    return r * (dn - n * jnp.mean(dn * n, axis=-1, keepdims=True))


def _onehot_row(idx, width):
    return (lax.broadcasted_iota(jnp.int32, (1, width), 1) == idx).astype(F32)


def _rowsum(x):
    return jnp.sum(x, axis=0, keepdims=True)


def _iota2(n):
    return lax.broadcasted_iota(jnp.int32, (n, n), 0), lax.broadcasted_iota(jnp.int32, (n, n), 1)


def _bmm(a, b):
    return lax.dot_general(a.astype(BF16), b.astype(BF16), (((2,), (1,)), ((0,), (0,))), preferred_element_type=F32)


def _bmm_nt(a, b):
    return lax.dot_general(a.astype(BF16), b.astype(BF16), (((2,), (2,)), ((0,), (0,))), preferred_element_type=F32)


def _bmm_tn(a, b):
    return lax.dot_general(a.astype(BF16), b.astype(BF16), (((1,), (1,)), ((0,), (0,))), preferred_element_type=F32)


def _tri_inv_impl(a):
    n = a.shape[-1]
    r, c = _iota2(n)
    x = r ^ c
    blk16 = x < 16
    blk32 = x < 32
    eye = (r == c).astype(F32)
    ad = jnp.where(blk16, a, 0.0)
    p2 = _bmm(ad, ad)
    e = p2 - ad - _bmm(ad, p2)
    p4 = _bmm(p2, p2)
    e = e + p4 + _bmm(e, p4)
    p8 = _bmm(p4, p4)
    e = e + p8 + _bmm(e, p8)
    m1 = jnp.where(jnp.logical_and(blk32, jnp.logical_not(blk16)), a, 0.0)
    f = m1 + _bmm(m1, e)
    e = e - f - _bmm(e, f)
    m2 = jnp.where(blk32, 0.0, a)
    f = m2 + _bmm(m2, e)
    e = e - f - _bmm(e, f)
    return e + eye


@jax.custom_vjp
def _tri_inv(a):
    return _tri_inv_impl(a)


def _tri_inv_fwd(a):
    t = _tri_inv_impl(a)
    return t, t


def _tri_inv_bwd(t, dt):
    return (-_bmm_tn(t, _bmm_nt(dt, t)),)


_tri_inv.defvjp(_tri_inv_fwd, _tri_inv_bwd)


def _sgu_core(u, v, z, lg, lb, ws, bcol):
    n = ws.shape[0]
    r, c = _iota2(n)
    wm = jnp.where(r >= c, ws, 0.0)
    gu = _gelu(u)
    gv = _gelu(v)
    xc = gv - jnp.mean(gv, axis=-1, keepdims=True)
    ln = xc * lax.rsqrt(jnp.mean(xc * xc, axis=-1, keepdims=True) + EPS) * lg + lb
    s = _bdot(wm, ln) + bcol
    return gu * s * _silu(z)


def _lanes_of(x):
    return jnp.concatenate([x[i] for i in range(x.shape[0])], axis=1)


def _batch_of(x, width):
    return jnp.concatenate([x[None, :, i * width:(i + 1) * width] for i in range(x.shape[1] // width)], axis=0)


def _dn_core(cq, ck, cv, z, bl, al, state, alog, dtb, og):
    gn, cn, dh = cq.shape
    q = _l2n(_silu(cq)) * (dh ** -0.5)
    k = _l2n(_silu(ck))
    v = _silu(cv)
    beta = jax.nn.sigmoid(bl)
    g = -jnp.exp(alog) * _softplus(al + dtb)
    r, c = _iota2(cn)
    tril = r >= c
    lower = tril.astype(F32)
    rw = lax.broadcasted_iota(jnp.int32, (cn, dh), 0)
    cw = lax.broadcasted_iota(jnp.int32, (cn, dh), 1)
    upper_wide = (rw <= cw).astype(F32)
    g_wide = jnp.broadcast_to(g, (gn, cn, dh))
    gc_wide = _batch_of(_hdot(lower, _lanes_of(g_wide)), dh)
    gc_cols = _batch_of(_hdot(jnp.ones((cn, cn), F32), _lanes_of(g_wide * upper_wide)), dh)[:, :, :cn]
    decay = jnp.exp(jnp.where(tril, gc_wide[:, :, :cn] - gc_cols, -1e30))
    kb = k * beta
    kk = _bmm_nt(kb, k) * decay
    t = _tri_inv(jnp.where(r > c, kk, 0.0))
    eg = jnp.exp(gc_wide)
    sol = _bmm(t, jnp.concatenate([v * beta, kb * eg], axis=-1))
    u_val, w_dec = sol[:, :, :dh], sol[:, :, dh:]
    qk = _bmm_nt(q, k) * decay
    g_last = jnp.sum(g_wide, axis=1, keepdims=True)
    k_dec = k * jnp.exp(g_last - gc_wide)
    ws = _bmm(jnp.concatenate([w_dec, q * eg], axis=1), state)
    v_new = u_val - ws[:, :cn]
    o = ws[:, cn:] + _bmm(qk, v_new)
    new_state = state * jnp.exp(g_last) + _bmm_tn(k_dec, v_new)
    on, _ = _rms(o)
    return on * og * _silu(z), new_state


def _peer(k):
    x, y, c = lax.axis_index("x"), lax.axis_index("y"), lax.axis_index("c")
    px = 1 - x if k & 4 else x
    py = 1 - y if k & 2 else y
    pc = 1 - c if k & 1 else c
    return (px, py, pc), 4 * px + 2 * py + pc


def _my_index():
    return 4 * lax.axis_index("x") + 2 * lax.axis_index("y") + lax.axis_index("c")


def _all_gather(pack):
    rows, width = pack.shape

    def body(src_ref, out_ref, send_sems, recv_sems, local_sem):
        me = _my_index()
        mine = pltpu.make_async_copy(src_ref, out_ref.at[me], local_sem)
        mine.start()
        copies = []
        for k in range(1, N_DEV):
            peer, _ = _peer(k)
            cp = pltpu.make_async_remote_copy(src_ref=src_ref, dst_ref=out_ref.at[me], send_sem=send_sems.at[k - 1],
                                              recv_sem=recv_sems.at[k - 1], device_id=peer, device_id_type=MESH)
            cp.start()
            copies.append(cp)
        for k in range(1, N_DEV):
            peer, peer_index = _peer(k)
            pltpu.make_async_remote_copy(src_ref=src_ref, dst_ref=out_ref.at[peer_index], send_sem=send_sems.at[k - 1],
                                         recv_sem=recv_sems.at[k - 1], device_id=peer, device_id_type=MESH).wait_recv()
        for cp in copies:
            cp.wait_send()
        mine.wait()

    return pl.pallas_call(
        body, name="all_gather_weights",
        out_shape=jax.ShapeDtypeStruct((N_DEV, rows, width), pack.dtype),
        in_specs=[pl.BlockSpec(memory_space=pl.ANY)],
        out_specs=pl.BlockSpec(memory_space=pl.ANY),
        scratch_shapes=[pltpu.SemaphoreType.DMA((N_DEV - 1,)), pltpu.SemaphoreType.DMA((N_DEV - 1,)),
                        pltpu.SemaphoreType.DMA(())],
    )(pack)


def _all_to_all(send):
    def body(src_ref, out_ref, send_sems, recv_sems, local_sem):
        me = _my_index()
        mine = pltpu.make_async_copy(src_ref.at[me], out_ref.at[me], local_sem)
        mine.start()
        copies = []
        for k in range(1, N_DEV):
            peer, peer_index = _peer(k)
            cp = pltpu.make_async_remote_copy(src_ref=src_ref.at[peer_index], dst_ref=out_ref.at[me],
                                              send_sem=send_sems.at[k - 1], recv_sem=recv_sems.at[k - 1],
                                              device_id=peer, device_id_type=MESH)
            cp.start()
            copies.append(cp)
        for k in range(1, N_DEV):
            peer, peer_index = _peer(k)
            pltpu.make_async_remote_copy(src_ref=src_ref.at[peer_index], dst_ref=out_ref.at[peer_index],
                                         send_sem=send_sems.at[k - 1], recv_sem=recv_sems.at[k - 1],
                                         device_id=peer, device_id_type=MESH).wait_recv()
        for cp in copies:
            cp.wait_send()
        mine.wait()

    return pl.pallas_call(
        body, name="all_to_all_grads",
        out_shape=jax.ShapeDtypeStruct(send.shape, send.dtype),
        in_specs=[pl.BlockSpec(memory_space=pl.ANY)],
        out_specs=pl.BlockSpec(memory_space=pl.ANY),
        scratch_shapes=[pltpu.SemaphoreType.DMA((N_DEV - 1,)), pltpu.SemaphoreType.DMA((N_DEV - 1,)),
                        pltpu.SemaphoreType.DMA(())],
    )(send)


def _params(n_axes):
    return pltpu.CompilerParams(dimension_semantics=("arbitrary",) * n_axes, vmem_limit_bytes=VMEM_LIMIT)


def _whole(shape):
    return pl.BlockSpec(shape, lambda *_: (0,) * len(shape))


VMEM_SPEC = pl.BlockSpec(memory_space=pltpu.VMEM)
HBM_SPEC = pl.BlockSpec(memory_space=pl.ANY)


def _inproj_fwd(x2, norm_g, wa, wq, wz, wg):
    t = x2.shape[0]
    tm = min(512, t)

    def body(x_ref, g_ref, wa_ref, wq_ref, wz_ref, wg_ref, a_ref, q_ref, z_ref, l_ref):
        n, _ = _rms(x_ref[...])
        xn = (n * g_ref[...]).astype(BF16)
        for w_ref, o_ref in ((wa_ref, a_ref), (wq_ref, q_ref), (wz_ref, z_ref), (wg_ref, l_ref)):
            width = w_ref.shape[1]
            for c0 in range(0, width, 512):
                c1 = min(c0 + 512, width)
                o_ref[:, c0:c1] = jnp.dot(xn, w_ref[:, c0:c1], preferred_element_type=F32)

    widths = (wa.shape[1], wq.shape[1], wz.shape[1], wg.shape[1])
    return pl.pallas_call(
        body, name="inproj_fwd", grid=(t // tm,),
        out_shape=tuple(jax.ShapeDtypeStruct((t, w), F32) for w in widths),
        in_specs=[pl.BlockSpec((tm, D_MODEL), lambda i: (i, 0)), _whole((1, D_MODEL)),
                  VMEM_SPEC, VMEM_SPEC, VMEM_SPEC, VMEM_SPEC],
        out_specs=tuple(pl.BlockSpec((tm, w), lambda i: (i, 0)) for w in widths),
        compiler_params=_params(1),
    )(x2, norm_g, wa, wq, wz, wg)


def _sgu_pieces(uvz_ref, lg_ref, lb_ref, ws_ref, bt_ref, row0, grp):
    rows = pl.ds(row0, SGU_CHUNK)
    lanes = pl.ds(grp * 128, 128)
    u = uvz_ref[rows, pl.ds(grp * 128, 128)]
    v = uvz_ref[rows, pl.ds(SGU_WIDTH + grp * 128, 128)]
    z = uvz_ref[rows, pl.ds(2 * SGU_WIDTH + grp * 128, 128)]
    bcol = jnp.sum(bt_ref[...] * _onehot_row(grp, SGU_GROUPS), axis=-1, keepdims=True)
    return u, v, z, lg_ref[:, lanes], lb_ref[:, lanes], ws_ref[grp], bcol


def _sgu_fwd(a_uvz, ln_g, ln_b, w_s, b_t):
    t = a_uvz.shape[0]
    tm = min(512, t)

    def body(uvz_ref, lg_ref, lb_ref, ws_ref, bt_ref, out_ref):
        for row0 in range(0, tm, SGU_CHUNK):
            for grp in range(SGU_GROUPS):
                args = _sgu_pieces(uvz_ref, lg_ref, lb_ref, ws_ref, bt_ref, row0, grp)
                out_ref[pl.ds(row0, SGU_CHUNK), pl.ds(grp * 128, 128)] = _sgu_core(*args).astype(out_ref.dtype)

    return pl.pallas_call(
        body, name="sgu_fwd", grid=(t // tm,),
        out_shape=jax.ShapeDtypeStruct((t, SGU_WIDTH), BF16),
        in_specs=[pl.BlockSpec((tm, 3 * SGU_WIDTH), lambda i: (i, 0)), _whole((1, SGU_WIDTH)), _whole((1, SGU_WIDTH)),
                  _whole((SGU_GROUPS, SGU_CHUNK, SGU_CHUNK)), _whole((SGU_CHUNK, SGU_GROUPS))],
        out_specs=pl.BlockSpec((tm, SGU_WIDTH), lambda i: (i, 0)),
        compiler_params=_params(1),
    )(a_uvz, ln_g, ln_b, w_s, b_t)


def _sgu_bwd(a_uvz, d_out, ln_g, ln_b, w_s, b_t):
    t = a_uvz.shape[0]
    tm = min(512, t)

    def body(uvz_ref, do_ref, lg_ref, lb_ref, ws_ref, bt_ref, duvz_ref, dlg_ref, dlb_ref, dws_ref, dbt_ref):
        @pl.when(pl.program_id(0) == 0)
        def _():
            dlg_ref[...] = jnp.zeros_like(dlg_ref)
            dlb_ref[...] = jnp.zeros_like(dlb_ref)
            dws_ref[...] = jnp.zeros_like(dws_ref)
            dbt_ref[...] = jnp.zeros_like(dbt_ref)

        for row0 in range(0, tm, SGU_CHUNK):
            rows = pl.ds(row0, SGU_CHUNK)
            for grp in range(SGU_GROUPS):
                lanes = pl.ds(grp * 128, 128)
                args = _sgu_pieces(uvz_ref, lg_ref, lb_ref, ws_ref, bt_ref, row0, grp)
                _, pull = jax.vjp(_sgu_core, *args)
                du, dv, dz, dlg, dlb, dws, dbcol = pull(do_ref[rows, lanes])
                duvz_ref[rows, pl.ds(grp * 128, 128)] = du
                duvz_ref[rows, pl.ds(SGU_WIDTH + grp * 128, 128)] = dv
                duvz_ref[rows, pl.ds(2 * SGU_WIDTH + grp * 128, 128)] = dz
                dlg_ref[:, lanes] += dlg
                dlb_ref[:, lanes] += dlb
                dws_ref[grp] += dws
                dbt_ref[...] += dbcol * _onehot_row(grp, SGU_GROUPS)

    return pl.pallas_call(
        body, name="sgu_bwd", grid=(t // tm,),
        out_shape=(jax.ShapeDtypeStruct((t, 3 * SGU_WIDTH), F32), jax.ShapeDtypeStruct((1, SGU_WIDTH), F32),
                   jax.ShapeDtypeStruct((1, SGU_WIDTH), F32), jax.ShapeDtypeStruct((SGU_GROUPS, SGU_CHUNK, SGU_CHUNK), F32),
                   jax.ShapeDtypeStruct((SGU_CHUNK, SGU_GROUPS), F32)),
        in_specs=[pl.BlockSpec((tm, 3 * SGU_WIDTH), lambda i: (i, 0)), pl.BlockSpec((tm, SGU_WIDTH), lambda i: (i, 0)),
                  _whole((1, SGU_WIDTH)), _whole((1, SGU_WIDTH)),
                  _whole((SGU_GROUPS, SGU_CHUNK, SGU_CHUNK)), _whole((SGU_CHUNK, SGU_GROUPS))],
        out_specs=(pl.BlockSpec((tm, 3 * SGU_WIDTH), lambda i: (i, 0)), _whole((1, SGU_WIDTH)), _whole((1, SGU_WIDTH)),
                   _whole((SGU_GROUPS, SGU_CHUNK, SGU_CHUNK)), _whole((SGU_CHUNK, SGU_GROUPS))),
        compiler_params=_params(1),
    )(a_uvz, d_out, ln_g, ln_b, w_s, b_t)


def _dn_conv(xpad_ref, c_ref, cur_ref, prev_ref, w_ref, first):
    for b in range(cur_ref.shape[0]):
        xpad_ref[b, 0:CONV_HALO, :] = jnp.where(first, 0.0, prev_ref[b])
        xpad_ref[b, CONV_HALO:, :] = cur_ref[b]
        acc = None
        for j in range(CONV_K):
            term = w_ref[j:j + 1, :] * xpad_ref[b, pl.ds(CONV_HALO - CONV_K + 1 + j, DN_CHUNK), :]
            acc = term if acc is None else acc + term
        c_ref[b] = acc


def _dn_pairs(nb):
    return [(b, h) for b in range(nb) for h in range(DN_HEADS)]


def _dn_batch_args(c_ref, z_ref, l_ref, alog_ref, dtb_ref):
    pairs = _dn_pairs(c_ref.shape[0])
    pick = lambda ref, b, col: ref[b, :, pl.ds(col, DN_HEAD_DIM)]
    column = lambda row, idx: jnp.sum(row * _onehot_row(idx, GATE_PAD), axis=-1, keepdims=True)
    cq = jnp.stack([pick(c_ref, b, h * DN_HEAD_DIM) for b, h in pairs])
    ck = jnp.stack([pick(c_ref, b, DN_WIDTH + h * DN_HEAD_DIM) for b, h in pairs])
    cv = jnp.stack([pick(c_ref, b, 2 * DN_WIDTH + h * DN_HEAD_DIM) for b, h in pairs])
    z = jnp.stack([pick(z_ref, b, h * DN_HEAD_DIM) for b, h in pairs])
    bl = jnp.stack([column(l_ref[b], h) for b, h in pairs])
    al = jnp.stack([column(l_ref[b], DN_HEADS + h) for b, h in pairs])
    alog = jnp.stack([column(alog_ref[...], h) for _, h in pairs])
    dtb = jnp.stack([column(dtb_ref[...], h) for _, h in pairs])
    return cq, ck, cv, z, bl, al, alog, dtb


def _dn_in_specs(nb, order):
    prev = lambda n: (0, jnp.maximum(order(n) * (DN_CHUNK // CONV_HALO) - 1, 0), 0)
    return [pl.BlockSpec((nb, DN_CHUNK, 3 * DN_WIDTH), lambda n: (0, order(n), 0)),
            pl.BlockSpec((nb, CONV_HALO, 3 * DN_WIDTH), prev),
            pl.BlockSpec((nb, DN_CHUNK, DN_WIDTH), lambda n: (0, order(n), 0)),
            pl.BlockSpec((nb, DN_CHUNK, GATE_PAD), lambda n: (0, order(n), 0)),
            _whole((CONV_K, 3 * DN_WIDTH)), _whole((1, GATE_PAD)), _whole((1, GATE_PAD)), _whole((1, DN_HEAD_DIM))]


def _dn_fwd(qkv, zg, logits, conv_w, alog, dtb, og):
    nb, s, _ = qkv.shape
    nc = s // DN_CHUNK

    pairs = _dn_pairs(nb)
    gn = len(pairs)

    def body(cur_ref, prev_ref, z_ref, l_ref, w_ref, alog_ref, dtb_ref, og_ref, out_ref, st_ref,
             state_ref, xpad_ref, c_ref):
        n = pl.program_id(0)

        @pl.when(n == 0)
        def _():
            state_ref[...] = jnp.zeros_like(state_ref)

        _dn_conv(xpad_ref, c_ref, cur_ref, prev_ref, w_ref, n == 0)
        cq, ck, cv, z, bl, al, a_g, d_g = _dn_batch_args(c_ref, z_ref, l_ref, alog_ref, dtb_ref)
        state = state_ref[...]
        st_ref[...] = state
        out, new_state = _dn_core(cq, ck, cv, z, bl, al, state, a_g, d_g, og_ref[...])
        state_ref[...] = new_state
        for i, (b, h) in enumerate(pairs):
            out_ref[b, :, pl.ds(h * DN_HEAD_DIM, DN_HEAD_DIM)] = out[i].astype(out_ref.dtype)

    return pl.pallas_call(
        body, name="deltanet_fwd", grid=(nc,),
        out_shape=(jax.ShapeDtypeStruct((nb, s, DN_WIDTH), BF16),
                   jax.ShapeDtypeStruct((nc, gn, DN_HEAD_DIM, DN_HEAD_DIM), F32)),
        in_specs=_dn_in_specs(nb, lambda n: n),
        out_specs=(pl.BlockSpec((nb, DN_CHUNK, DN_WIDTH), lambda n: (0, n, 0)),
                   pl.BlockSpec((None, gn, DN_HEAD_DIM, DN_HEAD_DIM), lambda n: (n, 0, 0, 0))),
        scratch_shapes=[pltpu.VMEM((gn, DN_HEAD_DIM, DN_HEAD_DIM), F32),
                        pltpu.VMEM((nb, CONV_HALO + DN_CHUNK, 3 * DN_WIDTH), F32),
                        pltpu.VMEM((nb, DN_CHUNK, 3 * DN_WIDTH), F32)],
        compiler_params=_params(1),
    )(qkv, qkv, zg, logits, conv_w, alog, dtb, og)


def _dn_bwd(qkv, zg, logits, conv_w, alog, dtb, og, states, d_out):
    nb, s, _ = qkv.shape
    nc = s // DN_CHUNK
    rev = lambda n: nc - 1 - n
    pairs = _dn_pairs(nb)
    gn = len(pairs)

    def body(cur_ref, prev_ref, z_ref, l_ref, w_ref, alog_ref, dtb_ref, og_ref, st_ref, do_ref,
             dqkv_ref, dz_ref, dl_ref, dw_ref, dalog_ref, ddtb_ref, dog_ref,
             dstate_ref, xpad_ref, c_ref, dcpad_ref):
        n = pl.program_id(0)

        @pl.when(n == 0)
        def _():
            dw_ref[...] = jnp.zeros_like(dw_ref)
            dalog_ref[...] = jnp.zeros_like(dalog_ref)
            ddtb_ref[...] = jnp.zeros_like(ddtb_ref)
            dog_ref[...] = jnp.zeros_like(dog_ref)
            dstate_ref[...] = jnp.zeros_like(dstate_ref)
            dcpad_ref[:, DN_CHUNK:, :] = jnp.zeros((nb, CONV_HALO, 3 * DN_WIDTH), F32)

        _dn_conv(xpad_ref, c_ref, cur_ref, prev_ref, w_ref, n == nc - 1)
        cq, ck, cv, z, bl, al, a_g, d_g = _dn_batch_args(c_ref, z_ref, l_ref, alog_ref, dtb_ref)
        d_out_g = jnp.stack([do_ref[b, :, pl.ds(h * DN_HEAD_DIM, DN_HEAD_DIM)] for b, h in pairs])
        _, pull = jax.vjp(_dn_core, cq, ck, cv, z, bl, al, st_ref[...], a_g, d_g, og_ref[...])
        dcq, dck, dcv, dz, dbl, dal, dstate, da_g, dd_g, dog = pull((d_out_g, dstate_ref[...]))
        dstate_ref[...] = dstate
        dog_ref[...] += dog
        dlog = [jnp.zeros((DN_CHUNK, GATE_PAD), F32) for _ in range(nb)]
        for i, (b, h) in enumerate(pairs):
            dcpad_ref[b, 0:DN_CHUNK, pl.ds(h * DN_HEAD_DIM, DN_HEAD_DIM)] = dcq[i]
            dcpad_ref[b, 0:DN_CHUNK, pl.ds(DN_WIDTH + h * DN_HEAD_DIM, DN_HEAD_DIM)] = dck[i]
            dcpad_ref[b, 0:DN_CHUNK, pl.ds(2 * DN_WIDTH + h * DN_HEAD_DIM, DN_HEAD_DIM)] = dcv[i]
            dz_ref[b, :, pl.ds(h * DN_HEAD_DIM, DN_HEAD_DIM)] = dz[i]
            dlog[b] = dlog[b] + dbl[i] * _onehot_row(h, GATE_PAD) + dal[i] * _onehot_row(DN_HEADS + h, GATE_PAD)
            dalog_ref[...] += da_g[i] * _onehot_row(h, GATE_PAD)
            ddtb_ref[...] += dd_g[i] * _onehot_row(h, GATE_PAD)
        for b in range(nb):
            dl_ref[b] = dlog[b]
            dc = dcpad_ref[b, 0:DN_CHUNK, :]
            dx = None
            for j in range(CONV_K):
                term = w_ref[j:j + 1, :] * dcpad_ref[b, pl.ds(CONV_K - 1 - j, DN_CHUNK), :]
                dx = term if dx is None else dx + term
                dw_ref[j:j + 1, :] += _rowsum(dc * xpad_ref[b, pl.ds(CONV_HALO - CONV_K + 1 + j, DN_CHUNK), :])
            dqkv_ref[b] = dx
            dcpad_ref[b, DN_CHUNK:, :] = dcpad_ref[b, 0:CONV_HALO, :]

    chunk = lambda w: pl.BlockSpec((nb, DN_CHUNK, w), lambda n: (0, rev(n), 0))
    return pl.pallas_call(
        body, name="deltanet_bwd", grid=(nc,),
        out_shape=(jax.ShapeDtypeStruct((nb, s, 3 * DN_WIDTH), F32), jax.ShapeDtypeStruct((nb, s, DN_WIDTH), F32),
                   jax.ShapeDtypeStruct((nb, s, GATE_PAD), F32), jax.ShapeDtypeStruct((CONV_K, 3 * DN_WIDTH), F32),
                   jax.ShapeDtypeStruct((1, GATE_PAD), F32), jax.ShapeDtypeStruct((1, GATE_PAD), F32),
                   jax.ShapeDtypeStruct((1, DN_HEAD_DIM), F32)),
        in_specs=_dn_in_specs(nb, rev) + [
            pl.BlockSpec((None, gn, DN_HEAD_DIM, DN_HEAD_DIM), lambda n: (rev(n), 0, 0, 0)),
            chunk(DN_WIDTH)],
        out_specs=(chunk(3 * DN_WIDTH), chunk(DN_WIDTH), chunk(GATE_PAD), _whole((CONV_K, 3 * DN_WIDTH)),
                   _whole((1, GATE_PAD)), _whole((1, GATE_PAD)), _whole((1, DN_HEAD_DIM))),
        scratch_shapes=[pltpu.VMEM((gn, DN_HEAD_DIM, DN_HEAD_DIM), F32),
                        pltpu.VMEM((nb, CONV_HALO + DN_CHUNK, 3 * DN_WIDTH), F32),
                        pltpu.VMEM((nb, DN_CHUNK, 3 * DN_WIDTH), F32),
                        pltpu.VMEM((nb, DN_CHUNK + CONV_HALO, 3 * DN_WIDTH), F32)],
        compiler_params=_params(1),
    )(qkv, qkv, zg, logits, conv_w, alog, dtb, og, states, d_out)


def _head(a_out, b_out, x2, p2, target, w_out, w_out_t, w_gate, w_gate_t, w_proj, ple_g, fin_g):
    t = x2.shape[0]
    tm = min(256, t)
    steps = t // tm

    def body(a_ref, b_ref, x_ref, p_ref, y_ref, wo_ref, wot_ref, wg_ref, wgt_ref, wp_ref, pg_ref, fg_ref,
             da_ref, db_ref, dh_ref, dwo_hbm, dwg_hbm, dwp_hbm, dpg_ref, dfg_ref, loss_ref,
             dwo_acc, dwg_acc, dwp_acc):
        i = pl.program_id(0)

        @pl.when(i == 0)
        def _():
            dwo_acc[...] = jnp.zeros_like(dwo_acc)
            dwg_acc[...] = jnp.zeros_like(dwg_acc)
            dwp_acc[...] = jnp.zeros_like(dwp_acc)
            dpg_ref[...] = jnp.zeros_like(dpg_ref)
            dfg_ref[...] = jnp.zeros_like(dfg_ref)
            loss_ref[...] = jnp.zeros_like(loss_ref)

        a = a_ref[...]
        bb = b_ref[...]
        pb = p_ref[...].astype(BF16)
        pg = pg_ref[...]
        fg = fg_ref[...]
        h1 = (x_ref[...] + jnp.dot(a, wo_ref[0:SGU_WIDTH, :], preferred_element_type=F32)
              + jnp.dot(bb, wo_ref[SGU_WIDTH:, :], preferred_element_type=F32))
        n1, r1 = _rms(h1)
        rn = (n1 * pg).astype(BF16)
        gate = jax.nn.sigmoid(jnp.dot(rn, wg_ref[...], preferred_element_type=F32))
        pp = jnp.dot(pb, wp_ref[...], preferred_element_type=F32)
        h2 = h1 + gate * pp
        n2, r2 = _rms(h2)
        err = n2 * fg - y_ref[...]
        loss_ref[...] += jnp.broadcast_to(_rowsum(jnp.sum(err * err, axis=-1, keepdims=True)), loss_ref.shape)

        dy = err * (1.0 / D_MODEL)
        dfg_ref[...] += _rowsum(dy * n2)
        dh2 = _rms_bwd(dy * fg, n2, r2)
        dpp = (dh2 * gate).astype(BF16)
        dgl = (dh2 * pp * gate * (1.0 - gate)).astype(BF16)
        dwp_acc[...] += lax.dot_general(pb, dpp, (((0,), (0,)), ((), ())), preferred_element_type=F32)
        dwg_acc[...] += lax.dot_general(rn, dgl, (((0,), (0,)), ((), ())), preferred_element_type=F32)
        drn = jnp.dot(dgl, wgt_ref[...], preferred_element_type=F32)
        dpg_ref[...] += _rowsum(drn * n1)
        dh1 = dh2 + _rms_bwd(drn * pg, n1, r1)
        dh_ref[...] = dh1
        dhb = dh1.astype(BF16)
        da_ref[...] = jnp.dot(dhb, wot_ref[:, 0:SGU_WIDTH], preferred_element_type=F32)
        db_ref[...] = jnp.dot(dhb, wot_ref[:, SGU_WIDTH:], preferred_element_type=F32)
        dwo_acc[0:SGU_WIDTH, :] += lax.dot_general(a, dhb, (((0,), (0,)), ((), ())), preferred_element_type=F32)
        dwo_acc[SGU_WIDTH:, :] += lax.dot_general(bb, dhb, (((0,), (0,)), ((), ())), preferred_element_type=F32)

        @pl.when(i == steps - 1)
        def _():
            pltpu.sync_copy(dwo_acc, dwo_hbm)
            pltpu.sync_copy(dwg_acc, dwg_hbm)
            pltpu.sync_copy(dwp_acc, dwp_hbm)

    tile = lambda w: pl.BlockSpec((tm, w), lambda i: (i, 0))
    return pl.pallas_call(
        body, name="head_fwd_bwd", grid=(steps,),
        out_shape=(jax.ShapeDtypeStruct((t, SGU_WIDTH), F32), jax.ShapeDtypeStruct((t, DN_WIDTH), F32),
                   jax.ShapeDtypeStruct((t, D_MODEL), F32), jax.ShapeDtypeStruct((D_MODEL, D_MODEL), F32),
                   jax.ShapeDtypeStruct((D_MODEL, D_MODEL), F32), jax.ShapeDtypeStruct((PLE_DIM, D_MODEL), F32),
                   jax.ShapeDtypeStruct((1, D_MODEL), F32), jax.ShapeDtypeStruct((1, D_MODEL), F32),
                   jax.ShapeDtypeStruct((8, LANES), F32)),
        in_specs=[tile(SGU_WIDTH), tile(DN_WIDTH), tile(D_MODEL), tile(PLE_DIM), tile(D_MODEL),
                  VMEM_SPEC, VMEM_SPEC, VMEM_SPEC, VMEM_SPEC, VMEM_SPEC, _whole((1, D_MODEL)), _whole((1, D_MODEL))],
        out_specs=(tile(SGU_WIDTH), tile(DN_WIDTH), tile(D_MODEL), HBM_SPEC, HBM_SPEC, HBM_SPEC,
                   _whole((1, D_MODEL)), _whole((1, D_MODEL)), _whole((8, LANES))),
        scratch_shapes=[pltpu.VMEM((D_MODEL, D_MODEL), F32), pltpu.VMEM((D_MODEL, D_MODEL), F32),
                        pltpu.VMEM((PLE_DIM, D_MODEL), F32)],
        compiler_params=_params(1),
    )(a_out, b_out, x2, p2, target, w_out, w_out_t, w_gate, w_gate_t, w_proj, ple_g, fin_g)


def _inproj_bwd(x2, dh1, d_a, d_q, d_z, d_l, norm_g, wat, wqt, wzt, wgt):
    t = x2.shape[0]
    tm = min(256, t)
    steps = t // tm

    def body(x_ref, dh_ref, da_ref, dq_ref, dz_ref, dl_ref, g_ref, wat_ref, wqt_ref, wzt_ref, wgt_ref,
             dx_ref, dwa_hbm, dwq_hbm, dwz_hbm, dwg_hbm, dg_ref, dwa_acc, dwq_acc, dwz_acc, dwg_acc):
        i = pl.program_id(0)

        @pl.when(i == 0)
        def _():
            for acc in (dwa_acc, dwq_acc, dwz_acc, dwg_acc, dg_ref):
                acc[...] = jnp.zeros_like(acc)

        g = g_ref[...]
        n, r = _rms(x_ref[...])
        xn = (n * g).astype(BF16)
        dxn = None
        for d_ref, wt_ref, acc in ((da_ref, wat_ref, dwa_acc), (dq_ref, wqt_ref, dwq_acc),
                                   (dz_ref, wzt_ref, dwz_acc), (dl_ref, wgt_ref, dwg_acc)):
            width = d_ref.shape[1]
            for c0 in range(0, width, 512):
                c1 = min(c0 + 512, width)
                d = d_ref[:, c0:c1].astype(BF16)
                term = jnp.dot(d, wt_ref[c0:c1, :], preferred_element_type=F32)
                dxn = term if dxn is None else dxn + term
                acc[:, c0:c1] += lax.dot_general(xn, d, (((0,), (0,)), ((), ())), preferred_element_type=F32)
        dg_ref[...] += _rowsum(dxn * n)
        dx_ref[...] = dh_ref[...] + _rms_bwd(dxn * g, n, r)

        @pl.when(i == steps - 1)
        def _():
            pltpu.sync_copy(dwa_acc, dwa_hbm)
            pltpu.sync_copy(dwq_acc, dwq_hbm)
            pltpu.sync_copy(dwz_acc, dwz_hbm)
            pltpu.sync_copy(dwg_acc, dwg_hbm)

    widths = (d_a.shape[1], d_q.shape[1], d_z.shape[1], d_l.shape[1])
    tile = lambda w: pl.BlockSpec((tm, w), lambda i: (i, 0))
    return pl.pallas_call(
        body, name="inproj_bwd", grid=(steps,),
        out_shape=(jax.ShapeDtypeStruct((t, D_MODEL), F32),) + tuple(jax.ShapeDtypeStruct((D_MODEL, w), F32) for w in widths)
        + (jax.ShapeDtypeStruct((1, D_MODEL), F32),),
        in_specs=[tile(D_MODEL), tile(D_MODEL)] + [tile(w) for w in widths] + [_whole((1, D_MODEL))] + [VMEM_SPEC] * 4,
        out_specs=(tile(D_MODEL), HBM_SPEC, HBM_SPEC, HBM_SPEC, HBM_SPEC, _whole((1, D_MODEL))),
        scratch_shapes=[pltpu.VMEM((D_MODEL, w), F32) for w in widths],
        compiler_params=_params(1),
    )(x2, dh1, d_a, d_q, d_z, d_l, norm_g, wat, wqt, wzt, wgt)


def _reduce_adamw(recv, w, m, v):
    rows = w.shape[0]

    def body(r_ref, w_ref, m_ref, v_ref, g_ref, d_ref, nm_ref, nv_ref):
        g = r_ref[0]
        for i in range(1, N_DEV):
            g = g + r_ref[i]
        m_new = ADAM_B1 * m_ref[...] + (1.0 - ADAM_B1) * g
        v_new = ADAM_B2 * v_ref[...] + (1.0 - ADAM_B2) * jnp.square(g)
        m_hat = m_new / (1.0 - ADAM_B1 ** ADAM_STEP)
        v_hat = v_new / (1.0 - ADAM_B2 ** ADAM_STEP)
        g_ref[...] = g
        d_ref[...] = -ADAM_LR * (m_hat / (jnp.sqrt(v_hat) + ADAM_EPS) + ADAM_WD * w_ref[...])
        nm_ref[...] = m_new
        nv_ref[...] = v_new

    blk = pl.BlockSpec((PACK_BLOCK, LANES), lambda i: (i, 0))
    return pl.pallas_call(
        body, name="reduce_adamw", grid=(rows // PACK_BLOCK,),
        out_shape=tuple(jax.ShapeDtypeStruct((rows, LANES), F32) for _ in range(4)),
        in_specs=[pl.BlockSpec((N_DEV, PACK_BLOCK, LANES), lambda i: (0, i, 0)), blk, blk, blk],
        out_specs=(blk, blk, blk, blk),
        compiler_params=_params(1),
    )(recv, w, m, v)


def _pack_rows(pieces, rows, dtype):
    flat = jnp.concatenate([jnp.ravel(p).astype(dtype) for p in pieces])
    flat = jnp.pad(flat, (0, rows * LANES - flat.shape[0]))
    return flat.reshape(rows, LANES)


def _unpack(pack, layout):
    flat = pack.reshape(-1)
    out, off = {}, 0
    for name, shape in layout:
        n = _size(shape)
        out[name] = flat[off:off + n].reshape(shape)
        off += n
    return out


def _by_device(full, axis):
    parts = full.shape[axis] // N_DEV
    shape = full.shape[:axis] + (N_DEV, parts) + full.shape[axis + 1:]
    return jnp.moveaxis(full.reshape(shape), axis, 0)


def kernel(x, p, norm_g, w_in, sgu_ln_g, sgu_ln_b, sgu_w_s, sgu_b_s, dn_conv_w, dn_a_log, dn_dt_bias, dn_o_norm_g, w_out, ple_norm_g, ple_gate_w, ple_proj_w, final_norm_g, loss_target, m_norm_g, m_w_in, m_sgu_ln_g, m_sgu_ln_b, m_sgu_w_s, m_sgu_b_s, m_dn_conv_w, m_dn_a_log, m_dn_dt_bias, m_dn_o_norm_g, m_w_out, m_ple_norm_g, m_ple_gate_w, m_ple_proj_w, m_final_norm_g, v_norm_g, v_w_in, v_sgu_ln_g, v_sgu_ln_b, v_sgu_w_s, v_sgu_b_s, v_dn_conv_w, v_dn_a_log, v_dn_dt_bias, v_dn_o_norm_g, v_w_out, v_ple_norm_g, v_ple_gate_w, v_ple_proj_w, v_final_norm_g):
    weights = dict(norm_g=norm_g, w_in=w_in, sgu_ln_g=sgu_ln_g, sgu_ln_b=sgu_ln_b, sgu_w_s=sgu_w_s, sgu_b_s=sgu_b_s,
                   dn_conv_w=dn_conv_w, dn_a_log=dn_a_log, dn_dt_bias=dn_dt_bias, dn_o_norm_g=dn_o_norm_g, w_out=w_out,
                   ple_norm_g=ple_norm_g, ple_gate_w=ple_gate_w, ple_proj_w=ple_proj_w, final_norm_g=final_norm_g)
    mom1 = dict(norm_g=m_norm_g, w_in=m_w_in, sgu_ln_g=m_sgu_ln_g, sgu_ln_b=m_sgu_ln_b, sgu_w_s=m_sgu_w_s,
                sgu_b_s=m_sgu_b_s, dn_conv_w=m_dn_conv_w, dn_a_log=m_dn_a_log, dn_dt_bias=m_dn_dt_bias,
                dn_o_norm_g=m_dn_o_norm_g, w_out=m_w_out, ple_norm_g=m_ple_norm_g, ple_gate_w=m_ple_gate_w,
                ple_proj_w=m_ple_proj_w, final_norm_g=m_final_norm_g)
    mom2 = dict(norm_g=v_norm_g, w_in=v_w_in, sgu_ln_g=v_sgu_ln_g, sgu_ln_b=v_sgu_ln_b, sgu_w_s=v_sgu_w_s,
                sgu_b_s=v_sgu_b_s, dn_conv_w=v_dn_conv_w, dn_a_log=v_dn_a_log, dn_dt_bias=v_dn_dt_bias,
                dn_o_norm_g=v_dn_o_norm_g, w_out=v_w_out, ple_norm_g=v_ple_norm_g, ple_gate_w=v_ple_gate_w,
                ple_proj_w=v_ple_proj_w, final_norm_g=v_final_norm_g)
    layout = tuple((name, weights[name].shape) for name, _ in SHARDED + REPLICATED)
    nb, s, _ = x.shape
    t = nb * s

    conv_hi = dn_conv_w.astype(BF16)
    conv_mid = (dn_conv_w - conv_hi.astype(F32)).astype(BF16)
    conv_lo = (dn_conv_w - conv_hi.astype(F32) - conv_mid.astype(F32)).astype(BF16)
    ag_pieces = [("w_in", w_in), ("w_out", w_out), ("ple_gate_w", ple_gate_w), ("ple_proj_w", ple_proj_w),
                 ("conv_hi", conv_hi), ("conv_mid", conv_mid), ("conv_lo", conv_lo)]
    ag_rows = -(-sum(a.size for _, a in ag_pieces) // (16 * LANES)) * 16
    gathered = _all_gather(_pack_rows([a for _, a in ag_pieces], ag_rows, BF16))
    flat = gathered.reshape(N_DEV, -1)
    full, off = {}, 0
    for name, a in ag_pieces:
        shape = a.shape[1:]
        full[name] = flat[:, off:off + a.size].reshape((N_DEV,) + shape)
        off += a.size
    full["dn_conv_w"] = full["conv_hi"].astype(F32) + full["conv_mid"].astype(F32) + full["conv_lo"].astype(F32)
    w_in_full = jnp.moveaxis(full["w_in"], 0, 1).reshape(D_MODEL, IN_COLS)
    wa = w_in_full[:, :3 * SGU_WIDTH]
    wq = w_in_full[:, 3 * SGU_WIDTH:3 * SGU_WIDTH + 3 * DN_WIDTH]
    wz = w_in_full[:, 3 * SGU_WIDTH + 3 * DN_WIDTH:3 * SGU_WIDTH + 4 * DN_WIDTH]
    wg = jnp.pad(w_in_full[:, 3 * SGU_WIDTH + 4 * DN_WIDTH:], ((0, 0), (0, GATE_PAD - 2 * DN_HEADS)))
    w_out_full = full["w_out"].reshape(D_MODEL, D_MODEL)
    w_gate_full = full["ple_gate_w"].reshape(D_MODEL, D_MODEL)
    w_proj_full = jnp.moveaxis(full["ple_proj_w"], 0, 1).reshape(PLE_DIM, D_MODEL)
    conv_full = jnp.moveaxis(full["dn_conv_w"], 0, 1).reshape(CONV_K, 3 * DN_WIDTH)

    pad_row = lambda a: jnp.pad(a.reshape(1, -1), ((0, 0), (0, GATE_PAD - a.size)))
    alog, dtb = pad_row(dn_a_log), pad_row(dn_dt_bias)
    og = dn_o_norm_g.reshape(1, DN_HEAD_DIM)
    ws = sgu_w_s.reshape(SGU_GROUPS, SGU_CHUNK, SGU_CHUNK)
    b_t = sgu_b_s.reshape(SGU_GROUPS, SGU_CHUNK).T
    fin_g = final_norm_g.reshape(1, D_MODEL)

    x2 = x.reshape(t, D_MODEL)
    a_uvz, b_qkv, b_z, b_l = _inproj_fwd(x2, norm_g, wa, wq, wz, wg)
    a_out = _sgu_fwd(a_uvz, sgu_ln_g, sgu_ln_b, ws, b_t)
    qkv3 = b_qkv.reshape(nb, s, 3 * DN_WIDTH)
    z3 = b_z.reshape(nb, s, DN_WIDTH)
    l3 = b_l.reshape(nb, s, GATE_PAD)
    b_out, states = _dn_fwd(qkv3, z3, l3, conv_full, alog, dtb, og)

    d_a, d_b, dh1, g_w_out, g_gate, g_proj, g_ple_g, g_fin_g, loss_tile = _head(
        a_out, b_out.reshape(t, DN_WIDTH), x2, p.reshape(t, PLE_DIM), loss_target.reshape(t, D_MODEL),
        w_out_full, w_out_full.T, w_gate_full, w_gate_full.T, w_proj_full, ple_norm_g, fin_g)
    d_qkv, d_z, d_l, g_conv, g_alog, g_dtb, g_og = _dn_bwd(
        qkv3, z3, l3, conv_full, alog, dtb, og, states, d_b.reshape(nb, s, DN_WIDTH))
    d_uvz, g_ln_g, g_ln_b, g_ws, g_bt = _sgu_bwd(a_uvz, d_a, sgu_ln_g, sgu_ln_b, ws, b_t)
    grad_x, g_wa, g_wq, g_wz, g_wg, g_norm = _inproj_bwd(
        x2, dh1, d_uvz, d_qkv.reshape(t, 3 * DN_WIDTH), d_z.reshape(t, DN_WIDTH), d_l.reshape(t, GATE_PAD),
        norm_g, wa.T, wq.T, wz.T, wg.T)

    g_w_in = jnp.concatenate([g_wa, g_wq, g_wz, g_wg[:, :2 * DN_HEADS]], axis=1)
    sharded_parts = [_by_device(g_w_in, 1), g_w_out.reshape(N_DEV, 128, D_MODEL), g_gate.reshape(N_DEV, 128, D_MODEL),
                     _by_device(g_proj, 1), _by_device(g_conv, 1)]
    small = [g_norm, g_ln_g, g_ln_b, g_ws, g_bt.T, g_alog[:, :DN_HEADS], g_dtb[:, :DN_HEADS], g_og, g_ple_g, g_fin_g,
             (0.5 / D_MODEL) * loss_tile[0:1, 0:1]]
    small_flat = jnp.concatenate([jnp.ravel(a) for a in small])
    send = jnp.concatenate([a.reshape(N_DEV, -1) for a in sharded_parts]
                           + [jnp.broadcast_to(small_flat, (N_DEV, small_flat.shape[0]))], axis=1)
    send = jnp.pad(send, ((0, 0), (0, PACK_ROWS * LANES - send.shape[1]))).reshape(N_DEV, PACK_ROWS, LANES)
    recv = _all_to_all(send)

    order = [name for name, _ in SHARDED + REPLICATED]
    g_pack, d_pack, m_pack, v_pack = _reduce_adamw(
        recv, _pack_rows([weights[k] for k in order], PACK_ROWS, F32),
        _pack_rows([mom1[k] for k in order], PACK_ROWS, F32), _pack_rows([mom2[k] for k in order], PACK_ROWS, F32))
    grads, deltas, new_m, new_v = (_unpack(a, layout) for a in (g_pack, d_pack, m_pack, v_pack))
    loss = g_pack.reshape(-1)[N_PACK - 1]

    return (loss, grad_x.reshape(nb, s, D_MODEL), *[grads[k] for k in WEIGHT_ORDER], *[deltas[k] for k in WEIGHT_ORDER],
            *[new_m[k] for k in WEIGHT_ORDER], *[new_v[k] for k in WEIGHT_ORDER])
```

```python
import jax
import jax.numpy as jnp
from jax import lax
from jax.experimental import pallas as pl
from jax.experimental.pallas import tpu as pltpu

F32 = jnp.float32
BF16 = jnp.bfloat16

N_DEV = 8
D_MODEL = 1024
SGU_WIDTH = 512
SGU_GROUPS = 4
SGU_CHUNK = 128
DN_WIDTH = 512
DN_HEADS = 4
DN_HEAD_DIM = 128
DN_CHUNK = 64
CONV_K = 4
CONV_HALO = 8
PLE_DIM = 256
EPS = 1e-6
IN_COLS = 3592
IN_SHARD = IN_COLS // N_DEV
GATE_PAD = 128

ADAM_LR = 0.001
ADAM_B1 = 0.9
ADAM_B2 = 0.999
ADAM_EPS = 1e-08
ADAM_WD = 0.01
ADAM_STEP = 10

LANES = 128
VMEM_LIMIT = 56 * 1024 * 1024
MESH = pl.DeviceIdType.MESH
HIGHEST = lax.Precision.HIGHEST

REPLICATED = (("norm_g", (1, D_MODEL)), ("sgu_ln_g", (1, SGU_WIDTH)), ("sgu_ln_b", (1, SGU_WIDTH)),
              ("sgu_w_s", (1, SGU_GROUPS, SGU_CHUNK, SGU_CHUNK)), ("sgu_b_s", (1, SGU_GROUPS, SGU_CHUNK)),
              ("dn_a_log", (1, DN_HEADS)), ("dn_dt_bias", (1, DN_HEADS)), ("dn_o_norm_g", (1, DN_HEAD_DIM)),
              ("ple_norm_g", (1, D_MODEL)), ("final_norm_g", (D_MODEL,)))
WEIGHT_ORDER = ("norm_g", "w_in", "sgu_ln_g", "sgu_ln_b", "sgu_w_s", "sgu_b_s", "dn_conv_w", "dn_a_log",
                "dn_dt_bias", "dn_o_norm_g", "w_out", "ple_norm_g", "ple_gate_w", "ple_proj_w", "final_norm_g")


def _size(shape):
    n = 1
    for s in shape:
        n *= s
    return n


SMALL_LAYOUT = (("conv", (CONV_K, 3 * DN_WIDTH)),) + REPLICATED + (("loss", (1,)),)
N_SMALL = sum(_size(s) for _, s in SMALL_LAYOUT)
SMALL_ROWS = -(-N_SMALL // (8 * LANES)) * 8


def _bdot(a, b):
    return jnp.dot(a.astype(BF16), b.astype(BF16), preferred_element_type=F32)


def _hdot(a, b):
    return jnp.dot(a, b, precision=HIGHEST, preferred_element_type=F32)


def _silu(x):
    return x * jax.nn.sigmoid(x)


def _gelu(x):
    return 0.5 * x * (1.0 + lax.erf(x * (0.5 ** 0.5)))


def _softplus(x):
    return jnp.maximum(x, 0.0) + jnp.log1p(jnp.exp(-jnp.abs(x)))


def _l2n(x):
    return x * lax.rsqrt(jnp.sum(x * x, axis=-1, keepdims=True) + EPS)


def _rms(x):
    r = lax.rsqrt(jnp.mean(x * x, axis=-1, keepdims=True) + EPS)
    return x * r, r


def _rms_bwd(dn, n, r):
    return r * (dn - n * jnp.mean(dn * n, axis=-1, keepdims=True))


def _onehot_row(idx, width):
    return (lax.broadcasted_iota(jnp.int32, (1, width), 1) == idx).astype(F32)


def _rowsum(x):
    return jnp.sum(x, axis=0, keepdims=True)


def _iota2(n):
    return lax.broadcasted_iota(jnp.int32, (n, n), 0), lax.broadcasted_iota(jnp.int32, (n, n), 1)


def _bmm(a, b):
    return lax.dot_general(a.astype(BF16), b.astype(BF16), (((2,), (1,)), ((0,), (0,))), preferred_element_type=F32)


def _bmm_nt(a, b):
    return lax.dot_general(a.astype(BF16), b.astype(BF16), (((2,), (2,)), ((0,), (0,))), preferred_element_type=F32)


def _bmm_tn(a, b):
    return lax.dot_general(a.astype(BF16), b.astype(BF16), (((1,), (1,)), ((0,), (0,))), preferred_element_type=F32)


def _tri_inv_impl(a):
    n = a.shape[-1]
    r, c = _iota2(n)
    x = r ^ c
    blk16 = x < 16
    blk32 = x < 32
    eye = (r == c).astype(F32)
    ad = jnp.where(blk16, a, 0.0)
    p2 = _bmm(ad, ad)
    e = p2 - ad - _bmm(ad, p2)
    p4 = _bmm(p2, p2)
    e = e + p4 + _bmm(e, p4)
    p8 = _bmm(p4, p4)
    e = e + p8 + _bmm(e, p8)
    m1 = jnp.where(jnp.logical_and(blk32, jnp.logical_not(blk16)), a, 0.0)
    f = m1 + _bmm(m1, e)
    e = e - f - _bmm(e, f)
    m2 = jnp.where(blk32, 0.0, a)
    f = m2 + _bmm(m2, e)
    e = e - f - _bmm(e, f)
    return e + eye


@jax.custom_vjp
def _tri_inv(a):
    return _tri_inv_impl(a)


def _tri_inv_fwd(a):
    t = _tri_inv_impl(a)
    return t, t


def _tri_inv_bwd(t, dt):
    return (-_bmm_tn(t, _bmm_nt(dt, t)),)


_tri_inv.defvjp(_tri_inv_fwd, _tri_inv_bwd)


def _sgu_core(u, v, z, lg, lb, ws, bcol):
    n = ws.shape[0]
    r, c = _iota2(n)
    wm = jnp.where(r >= c, ws, 0.0)
    gu = _gelu(u)
    gv = _gelu(v)
    xc = gv - jnp.mean(gv, axis=-1, keepdims=True)
    ln = xc * lax.rsqrt(jnp.mean(xc * xc, axis=-1, keepdims=True) + EPS) * lg + lb
    s = _bdot(wm, ln) + bcol
    return gu * s * _silu(z)


def _lanes_of(x):
    return jnp.concatenate([x[i] for i in range(x.shape[0])], axis=1)


def _batch_of(x, width):
    return jnp.concatenate([x[None, :, i * width:(i + 1) * width] for i in range(x.shape[1] // width)], axis=0)


def _dn_core(cq, ck, cv, z, bl, al, state, alog, dtb, og):
    gn, cn, dh = cq.shape
    q = _l2n(_silu(cq)) * (dh ** -0.5)
    k = _l2n(_silu(ck))
    v = _silu(cv)
    beta = jax.nn.sigmoid(bl)
    g = -jnp.exp(alog) * _softplus(al + dtb)
    r, c = _iota2(cn)
    tril = r >= c
    lower = tril.astype(F32)
    rw = lax.broadcasted_iota(jnp.int32, (cn, dh), 0)
    cw = lax.broadcasted_iota(jnp.int32, (cn, dh), 1)
    upper_wide = (rw <= cw).astype(F32)
    g_wide = jnp.broadcast_to(g, (gn, cn, dh))
    gc_wide = _batch_of(_hdot(lower, _lanes_of(g_wide)), dh)
    gc_cols = _batch_of(_hdot(jnp.ones((cn, cn), F32), _lanes_of(g_wide * upper_wide)), dh)[:, :, :cn]
    decay = jnp.exp(jnp.where(tril, gc_wide[:, :, :cn] - gc_cols, -1e30))
    kb = k * beta
    kk = _bmm_nt(kb, k) * decay
    t = _tri_inv(jnp.where(r > c, kk, 0.0))
    eg = jnp.exp(gc_wide)
    sol = _bmm(t, jnp.concatenate([v * beta, kb * eg], axis=-1))
    u_val, w_dec = sol[:, :, :dh], sol[:, :, dh:]
    qk = _bmm_nt(q, k) * decay
    g_last = jnp.sum(g_wide, axis=1, keepdims=True)
    k_dec = k * jnp.exp(g_last - gc_wide)
    ws = _bmm(jnp.concatenate([w_dec, q * eg], axis=1), state)
    v_new = u_val - ws[:, :cn]
    o = ws[:, cn:] + _bmm(qk, v_new)
    new_state = state * jnp.exp(g_last) + _bmm_tn(k_dec, v_new)
    on, _ = _rms(o)
    return on * og * _silu(z), new_state


N_CHIPS = 4
HBM_SPEC = pl.BlockSpec(memory_space=pl.ANY)


def _place():
    return lax.axis_index("x"), lax.axis_index("y"), lax.axis_index("c")


def _other_chip(k):
    x, y, _ = _place()
    px = 1 - x if k & 2 else x
    py = 1 - y if k & 1 else y
    return px, py, 2 * px + py


def _remote(src, dst, send_sem, recv_sem, device):
    return pltpu.make_async_remote_copy(src_ref=src, dst_ref=dst, send_sem=send_sem, recv_sem=recv_sem,
                                        device_id=device, device_id_type=MESH)


def _all_gather(shards):
    n = len(shards)

    def body(*refs):
        srcs, outs = refs[:n], refs[n:2 * n]
        send_sems, recv_sems, local_sems = refs[2 * n:]
        x, y, c = _place()
        me = 4 * x + 2 * y + c
        sibling = (x, y, 1 - c)
        local = [pltpu.make_async_copy(srcs[a], outs[a].at[me], local_sems.at[a]) for a in range(n)]
        for cp in local:
            cp.start()
        sends = []
        for a in range(n):
            sends.append(_remote(srcs[a], outs[a].at[me], send_sems.at[a, 0], recv_sems.at[a, 0], sibling))
        for k in range(1, N_CHIPS):
            px, py, _ = _other_chip(k)
            for a in range(n):
                sends.append(_remote(srcs[a], outs[a].at[me], send_sems.at[a, k], recv_sems.at[a, k], (px, py, c)))
        for cp in sends:
            cp.start()
        passed = []
        for k in range(1, N_CHIPS):
            px, py, _ = _other_chip(k)
            blk = 4 * px + 2 * py + c
            for a in range(n):
                _remote(srcs[a], outs[a].at[blk], send_sems.at[a, k], recv_sems.at[a, k], (px, py, c)).wait_recv()
            for a in range(n):
                cp = _remote(outs[a].at[blk], outs[a].at[blk], send_sems.at[a, 3 + k], recv_sems.at[a, 3 + k], sibling)
                cp.start()
                passed.append(cp)
        for a in range(n):
            _remote(srcs[a], outs[a].at[me + 1 - 2 * c], send_sems.at[a, 0], recv_sems.at[a, 0], sibling).wait_recv()
        for k in range(1, N_CHIPS):
            px, py, _ = _other_chip(k)
            blk = 4 * px + 2 * py + 1 - c
            for a in range(n):
                _remote(srcs[a], outs[a].at[blk], send_sems.at[a, 3 + k], recv_sems.at[a, 3 + k], sibling).wait_recv()
        for cp in sends + passed:
            cp.wait_send()
        for cp in local:
            cp.wait()

    return pl.pallas_call(
        body, name="all_gather_weights",
        out_shape=tuple(jax.ShapeDtypeStruct((N_DEV,) + a.shape, a.dtype) for a in shards),
        in_specs=[HBM_SPEC] * n, out_specs=(HBM_SPEC,) * n,
        scratch_shapes=[pltpu.SemaphoreType.DMA((n, N_DEV - 1)), pltpu.SemaphoreType.DMA((n, N_DEV - 1)),
                        pltpu.SemaphoreType.DMA((n,))],
    )(*shards)


def _sibling_exchange(by_device, small):
    n = len(by_device)

    def body(*refs):
        srcs, small_src = refs[:n], refs[n]
        outs, small_out = refs[n + 1:2 * n + 1], refs[2 * n + 1]
        send_sems, recv_sems = refs[2 * n + 2:]
        x, y, c = _place()
        sibling = (x, y, 1 - c)
        copies = [_remote(small_src, small_out, send_sems.at[n, 0], recv_sems.at[n, 0], sibling)]
        for a in range(n):
            for q in range(N_CHIPS):
                copies.append(_remote(srcs[a].at[2 * q + 1 - c], outs[a].at[q], send_sems.at[a, q], recv_sems.at[a, q],
                                      sibling))
        for cp in copies:
            cp.start()
        for cp in copies:
            cp.wait_recv()
        for cp in copies:
            cp.wait_send()

    return pl.pallas_call(
        body, name="grad_sibling_exchange",
        out_shape=tuple(jax.ShapeDtypeStruct((N_CHIPS,) + a.shape[1:], a.dtype) for a in by_device)
        + (jax.ShapeDtypeStruct(small.shape, small.dtype),),
        in_specs=[HBM_SPEC] * (n + 1), out_specs=(HBM_SPEC,) * (n + 1),
        scratch_shapes=[pltpu.SemaphoreType.DMA((n + 1, N_CHIPS)), pltpu.SemaphoreType.DMA((n + 1, N_CHIPS))],
    )(*by_device, small)


def _chip_exchange(chip_sums, small):
    n = len(chip_sums)

    def body(*refs):
        srcs, small_src = refs[:n], refs[n]
        outs, small_out = refs[n + 1:2 * n + 1], refs[2 * n + 1]
        send_sems, recv_sems, local_sems = refs[2 * n + 2:]
        x, y, c = _place()
        mine = 2 * x + y
        local = [pltpu.make_async_copy(srcs[a].at[mine], outs[a].at[mine], local_sems.at[a]) for a in range(n)]
        local.append(pltpu.make_async_copy(small_src, small_out.at[mine], local_sems.at[n]))
        for cp in local:
            cp.start()
        sends = []
        for k in range(1, N_CHIPS):
            px, py, chip = _other_chip(k)
            for a in range(n):
                sends.append(_remote(srcs[a].at[chip], outs[a].at[mine], send_sems.at[a, k - 1], recv_sems.at[a, k - 1],
                                     (px, py, c)))
            sends.append(_remote(small_src, small_out.at[mine], send_sems.at[n, k - 1], recv_sems.at[n, k - 1], (px, py, c)))
        for cp in sends:
            cp.start()
        for k in range(1, N_CHIPS):
            px, py, chip = _other_chip(k)
            for a in range(n):
                _remote(srcs[a].at[chip], outs[a].at[chip], send_sems.at[a, k - 1], recv_sems.at[a, k - 1],
                        (px, py, c)).wait_recv()
            _remote(small_src, small_out.at[chip], send_sems.at[n, k - 1], recv_sems.at[n, k - 1], (px, py, c)).wait_recv()
        for cp in sends:
            cp.wait_send()
        for cp in local:
            cp.wait()

    return pl.pallas_call(
        body, name="grad_chip_exchange",
        out_shape=tuple(jax.ShapeDtypeStruct(a.shape, a.dtype) for a in chip_sums)
        + (jax.ShapeDtypeStruct((N_CHIPS,) + small.shape, small.dtype),),
        in_specs=[HBM_SPEC] * (n + 1), out_specs=(HBM_SPEC,) * (n + 1),
        scratch_shapes=[pltpu.SemaphoreType.DMA((n + 1, N_CHIPS - 1)), pltpu.SemaphoreType.DMA((n + 1, N_CHIPS - 1)),
                        pltpu.SemaphoreType.DMA((n + 1,))],
    )(*chip_sums, small)


def _pair_sum(core, by_device, from_sibling, small, small_from_sibling):
    n = len(by_device)

    def body(core_ref, *refs):
        own, sib = refs[:n], refs[n:2 * n]
        small_own, small_sib = refs[2 * n], refs[2 * n + 1]
        outs, small_out = refs[2 * n + 2:3 * n + 2], refs[3 * n + 2]
        for a in range(n):
            outs[a][...] = (own[a][...] + sib[a][...]).astype(outs[a].dtype)
        small_out[...] = small_own[...] + small_sib[...]

    def block(a):
        return (None,) + a.shape[1:], (0,) * (a.ndim - 1)

    own_specs = [pl.BlockSpec(block(a)[0], lambda q, core_ref, z=block(a)[1]: (2 * q + core_ref[0],) + z) for a in by_device]
    sib_specs = [pl.BlockSpec(block(a)[0], lambda q, core_ref, z=block(a)[1]: (q,) + z) for a in by_device]
    small_spec = pl.BlockSpec(small.shape, lambda q, core_ref: (0,) * small.ndim)
    return pl.pallas_call(
        body, name="grad_pair_sum",
        grid_spec=pltpu.PrefetchScalarGridSpec(
            num_scalar_prefetch=1, grid=(N_CHIPS,),
            in_specs=own_specs + sib_specs + [small_spec, small_spec],
            out_specs=tuple(sib_specs) + (small_spec,)),
        out_shape=tuple(jax.ShapeDtypeStruct(a.shape, BF16) for a in from_sibling)
        + (jax.ShapeDtypeStruct(small.shape, F32),),
        compiler_params=_params(1),
    )(core, *by_device, *from_sibling, small, small_from_sibling)


def _params(n_axes):
    return pltpu.CompilerParams(dimension_semantics=("arbitrary",) * n_axes, vmem_limit_bytes=VMEM_LIMIT)


def _whole(shape):
    return pl.BlockSpec(shape, lambda *_: (0,) * len(shape))


VMEM_SPEC = pl.BlockSpec(memory_space=pltpu.VMEM)


def _inproj_fwd(x2, norm_g, wa, wq, wz, wg):
    t = x2.shape[0]
    tm = min(512, t)

    def body(x_ref, g_ref, wa_ref, wq_ref, wz_ref, wg_ref, a_ref, q_ref, z_ref, l_ref):
        n, _ = _rms(x_ref[...])
        xn = (n * g_ref[...]).astype(BF16)
        for w_ref, o_ref in ((wa_ref, a_ref), (wq_ref, q_ref), (wz_ref, z_ref), (wg_ref, l_ref)):
            width = w_ref.shape[1]
            for c0 in range(0, width, 512):
                c1 = min(c0 + 512, width)
                o_ref[:, c0:c1] = jnp.dot(xn, w_ref[:, c0:c1], preferred_element_type=F32)

    widths = (wa.shape[1], wq.shape[1], wz.shape[1], wg.shape[1])
    return pl.pallas_call(
        body, name="inproj_fwd", grid=(t // tm,),
        out_shape=tuple(jax.ShapeDtypeStruct((t, w), F32) for w in widths),
        in_specs=[pl.BlockSpec((tm, D_MODEL), lambda i: (i, 0)), _whole((1, D_MODEL)),
                  VMEM_SPEC, VMEM_SPEC, VMEM_SPEC, VMEM_SPEC],
        out_specs=tuple(pl.BlockSpec((tm, w), lambda i: (i, 0)) for w in widths),
        compiler_params=_params(1),
    )(x2, norm_g, wa, wq, wz, wg)


def _sgu_pieces(uvz_ref, lg_ref, lb_ref, ws_ref, bt_ref, row0, grp):
    rows = pl.ds(row0, SGU_CHUNK)
    lanes = pl.ds(grp * 128, 128)
    u = uvz_ref[rows, pl.ds(grp * 128, 128)]
    v = uvz_ref[rows, pl.ds(SGU_WIDTH + grp * 128, 128)]
    z = uvz_ref[rows, pl.ds(2 * SGU_WIDTH + grp * 128, 128)]
    bcol = jnp.sum(bt_ref[...] * _onehot_row(grp, SGU_GROUPS), axis=-1, keepdims=True)
    return u, v, z, lg_ref[:, lanes], lb_ref[:, lanes], ws_ref[grp], bcol


def _sgu_fwd(a_uvz, ln_g, ln_b, w_s, b_t):
    t = a_uvz.shape[0]
    tm = min(512, t)

    def body(uvz_ref, lg_ref, lb_ref, ws_ref, bt_ref, out_ref):
        for row0 in range(0, tm, SGU_CHUNK):
            for grp in range(SGU_GROUPS):
                args = _sgu_pieces(uvz_ref, lg_ref, lb_ref, ws_ref, bt_ref, row0, grp)
                out_ref[pl.ds(row0, SGU_CHUNK), pl.ds(grp * 128, 128)] = _sgu_core(*args).astype(out_ref.dtype)

    return pl.pallas_call(
        body, name="sgu_fwd", grid=(t // tm,),
        out_shape=jax.ShapeDtypeStruct((t, SGU_WIDTH), BF16),
        in_specs=[pl.BlockSpec((tm, 3 * SGU_WIDTH), lambda i: (i, 0)), _whole((1, SGU_WIDTH)), _whole((1, SGU_WIDTH)),
                  _whole((SGU_GROUPS, SGU_CHUNK, SGU_CHUNK)), _whole((SGU_CHUNK, SGU_GROUPS))],
        out_specs=pl.BlockSpec((tm, SGU_WIDTH), lambda i: (i, 0)),
        compiler_params=_params(1),
    )(a_uvz, ln_g, ln_b, w_s, b_t)


def _sgu_bwd(a_uvz, d_out, ln_g, ln_b, w_s, b_t):
    t = a_uvz.shape[0]
    tm = min(512, t)

    def body(uvz_ref, do_ref, lg_ref, lb_ref, ws_ref, bt_ref, duvz_ref, dlg_ref, dlb_ref, dws_ref, dbt_ref):
        @pl.when(pl.program_id(0) == 0)
        def _():
            dlg_ref[...] = jnp.zeros_like(dlg_ref)
            dlb_ref[...] = jnp.zeros_like(dlb_ref)
            dws_ref[...] = jnp.zeros_like(dws_ref)
            dbt_ref[...] = jnp.zeros_like(dbt_ref)

        for row0 in range(0, tm, SGU_CHUNK):
            rows = pl.ds(row0, SGU_CHUNK)
            for grp in range(SGU_GROUPS):
                lanes = pl.ds(grp * 128, 128)
                args = _sgu_pieces(uvz_ref, lg_ref, lb_ref, ws_ref, bt_ref, row0, grp)
                _, pull = jax.vjp(_sgu_core, *args)
                du, dv, dz, dlg, dlb, dws, dbcol = pull(do_ref[rows, lanes])
                duvz_ref[rows, pl.ds(grp * 128, 128)] = du
                duvz_ref[rows, pl.ds(SGU_WIDTH + grp * 128, 128)] = dv
                duvz_ref[rows, pl.ds(2 * SGU_WIDTH + grp * 128, 128)] = dz
                dlg_ref[:, lanes] += dlg
                dlb_ref[:, lanes] += dlb
                dws_ref[grp] += dws
                dbt_ref[...] += dbcol * _onehot_row(grp, SGU_GROUPS)

    return pl.pallas_call(
        body, name="sgu_bwd", grid=(t // tm,),
        out_shape=(jax.ShapeDtypeStruct((t, 3 * SGU_WIDTH), F32), jax.ShapeDtypeStruct((1, SGU_WIDTH), F32),
                   jax.ShapeDtypeStruct((1, SGU_WIDTH), F32), jax.ShapeDtypeStruct((SGU_GROUPS, SGU_CHUNK, SGU_CHUNK), F32),
                   jax.ShapeDtypeStruct((SGU_CHUNK, SGU_GROUPS), F32)),
        in_specs=[pl.BlockSpec((tm, 3 * SGU_WIDTH), lambda i: (i, 0)), pl.BlockSpec((tm, SGU_WIDTH), lambda i: (i, 0)),
                  _whole((1, SGU_WIDTH)), _whole((1, SGU_WIDTH)),
                  _whole((SGU_GROUPS, SGU_CHUNK, SGU_CHUNK)), _whole((SGU_CHUNK, SGU_GROUPS))],
        out_specs=(pl.BlockSpec((tm, 3 * SGU_WIDTH), lambda i: (i, 0)), _whole((1, SGU_WIDTH)), _whole((1, SGU_WIDTH)),
                   _whole((SGU_GROUPS, SGU_CHUNK, SGU_CHUNK)), _whole((SGU_CHUNK, SGU_GROUPS))),
        compiler_params=_params(1),
    )(a_uvz, d_out, ln_g, ln_b, w_s, b_t)


def _dn_conv(xpad_ref, c_ref, cur_ref, prev_ref, w_ref, first):
    for b in range(cur_ref.shape[0]):
        xpad_ref[b, 0:CONV_HALO, :] = jnp.where(first, 0.0, prev_ref[b])
        xpad_ref[b, CONV_HALO:, :] = cur_ref[b]
        acc = None
        for j in range(CONV_K):
            term = w_ref[j:j + 1, :] * xpad_ref[b, pl.ds(CONV_HALO - CONV_K + 1 + j, DN_CHUNK), :]
            acc = term if acc is None else acc + term
        c_ref[b] = acc


def _dn_pairs(nb):
    return [(b, h) for b in range(nb) for h in range(DN_HEADS)]


def _dn_batch_args(c_ref, z_ref, l_ref, alog_ref, dtb_ref):
    pairs = _dn_pairs(c_ref.shape[0])
    pick = lambda ref, b, col: ref[b, :, pl.ds(col, DN_HEAD_DIM)]
    column = lambda row, idx: jnp.sum(row * _onehot_row(idx, GATE_PAD), axis=-1, keepdims=True)
    cq = jnp.stack([pick(c_ref, b, h * DN_HEAD_DIM) for b, h in pairs])
    ck = jnp.stack([pick(c_ref, b, DN_WIDTH + h * DN_HEAD_DIM) for b, h in pairs])
    cv = jnp.stack([pick(c_ref, b, 2 * DN_WIDTH + h * DN_HEAD_DIM) for b, h in pairs])
    z = jnp.stack([pick(z_ref, b, h * DN_HEAD_DIM) for b, h in pairs])
    bl = jnp.stack([column(l_ref[b], h) for b, h in pairs])
    al = jnp.stack([column(l_ref[b], DN_HEADS + h) for b, h in pairs])
    alog = jnp.stack([column(alog_ref[...], h) for _, h in pairs])
    dtb = jnp.stack([column(dtb_ref[...], h) for _, h in pairs])
    return cq, ck, cv, z, bl, al, alog, dtb


def _dn_in_specs(nb, order):
    prev = lambda n: (0, jnp.maximum(order(n) * (DN_CHUNK // CONV_HALO) - 1, 0), 0)
    return [pl.BlockSpec((nb, DN_CHUNK, 3 * DN_WIDTH), lambda n: (0, order(n), 0)),
            pl.BlockSpec((nb, CONV_HALO, 3 * DN_WIDTH), prev),
            pl.BlockSpec((nb, DN_CHUNK, DN_WIDTH), lambda n: (0, order(n), 0)),
            pl.BlockSpec((nb, DN_CHUNK, GATE_PAD), lambda n: (0, order(n), 0)),
            _whole((CONV_K, 3 * DN_WIDTH)), _whole((1, GATE_PAD)), _whole((1, GATE_PAD)), _whole((1, DN_HEAD_DIM))]


def _dn_fwd(qkv, zg, logits, conv_w, alog, dtb, og):
    nb, s, _ = qkv.shape
    nc = s // DN_CHUNK

    pairs = _dn_pairs(nb)
    gn = len(pairs)

    def body(cur_ref, prev_ref, z_ref, l_ref, w_ref, alog_ref, dtb_ref, og_ref, out_ref, st_ref,
             state_ref, xpad_ref, c_ref):
        n = pl.program_id(0)

        @pl.when(n == 0)
        def _():
            state_ref[...] = jnp.zeros_like(state_ref)

        _dn_conv(xpad_ref, c_ref, cur_ref, prev_ref, w_ref, n == 0)
        cq, ck, cv, z, bl, al, a_g, d_g = _dn_batch_args(c_ref, z_ref, l_ref, alog_ref, dtb_ref)
        state = state_ref[...]
        st_ref[...] = state
        out, new_state = _dn_core(cq, ck, cv, z, bl, al, state, a_g, d_g, og_ref[...])
        state_ref[...] = new_state
        for i, (b, h) in enumerate(pairs):
            out_ref[b, :, pl.ds(h * DN_HEAD_DIM, DN_HEAD_DIM)] = out[i].astype(out_ref.dtype)

    return pl.pallas_call(
        body, name="deltanet_fwd", grid=(nc,),
        out_shape=(jax.ShapeDtypeStruct((nb, s, DN_WIDTH), BF16),
                   jax.ShapeDtypeStruct((nc, gn, DN_HEAD_DIM, DN_HEAD_DIM), F32)),
        in_specs=_dn_in_specs(nb, lambda n: n),
        out_specs=(pl.BlockSpec((nb, DN_CHUNK, DN_WIDTH), lambda n: (0, n, 0)),
                   pl.BlockSpec((None, gn, DN_HEAD_DIM, DN_HEAD_DIM), lambda n: (n, 0, 0, 0))),
        scratch_shapes=[pltpu.VMEM((gn, DN_HEAD_DIM, DN_HEAD_DIM), F32),
                        pltpu.VMEM((nb, CONV_HALO + DN_CHUNK, 3 * DN_WIDTH), F32),
                        pltpu.VMEM((nb, DN_CHUNK, 3 * DN_WIDTH), F32)],
        compiler_params=_params(1),
    )(qkv, qkv, zg, logits, conv_w, alog, dtb, og)


def _dn_bwd(qkv, zg, logits, conv_w, alog, dtb, og, states, d_out):
    nb, s, _ = qkv.shape
    nc = s // DN_CHUNK
    rev = lambda n: nc - 1 - n
    pairs = _dn_pairs(nb)
    gn = len(pairs)

    def body(cur_ref, prev_ref, z_ref, l_ref, w_ref, alog_ref, dtb_ref, og_ref, st_ref, do_ref,
             dqkv_ref, dz_ref, dl_ref, dw_ref, dalog_ref, ddtb_ref, dog_ref,
             dstate_ref, xpad_ref, c_ref, dcpad_ref):
        n = pl.program_id(0)

        @pl.when(n == 0)
        def _():
            dw_ref[...] = jnp.zeros_like(dw_ref)
            dalog_ref[...] = jnp.zeros_like(dalog_ref)
            ddtb_ref[...] = jnp.zeros_like(ddtb_ref)
            dog_ref[...] = jnp.zeros_like(dog_ref)
            dstate_ref[...] = jnp.zeros_like(dstate_ref)
            dcpad_ref[:, DN_CHUNK:, :] = jnp.zeros((nb, CONV_HALO, 3 * DN_WIDTH), F32)

        _dn_conv(xpad_ref, c_ref, cur_ref, prev_ref, w_ref, n == nc - 1)
        cq, ck, cv, z, bl, al, a_g, d_g = _dn_batch_args(c_ref, z_ref, l_ref, alog_ref, dtb_ref)
        d_out_g = jnp.stack([do_ref[b, :, pl.ds(h * DN_HEAD_DIM, DN_HEAD_DIM)] for b, h in pairs])
        _, pull = jax.vjp(_dn_core, cq, ck, cv, z, bl, al, st_ref[...], a_g, d_g, og_ref[...])
        dcq, dck, dcv, dz, dbl, dal, dstate, da_g, dd_g, dog = pull((d_out_g, dstate_ref[...]))
        dstate_ref[...] = dstate
        dog_ref[...] += dog
        dlog = [jnp.zeros((DN_CHUNK, GATE_PAD), F32) for _ in range(nb)]
        for i, (b, h) in enumerate(pairs):
            dcpad_ref[b, 0:DN_CHUNK, pl.ds(h * DN_HEAD_DIM, DN_HEAD_DIM)] = dcq[i]
            dcpad_ref[b, 0:DN_CHUNK, pl.ds(DN_WIDTH + h * DN_HEAD_DIM, DN_HEAD_DIM)] = dck[i]
            dcpad_ref[b, 0:DN_CHUNK, pl.ds(2 * DN_WIDTH + h * DN_HEAD_DIM, DN_HEAD_DIM)] = dcv[i]
            dz_ref[b, :, pl.ds(h * DN_HEAD_DIM, DN_HEAD_DIM)] = dz[i]
            dlog[b] = dlog[b] + dbl[i] * _onehot_row(h, GATE_PAD) + dal[i] * _onehot_row(DN_HEADS + h, GATE_PAD)
            dalog_ref[...] += da_g[i] * _onehot_row(h, GATE_PAD)
            ddtb_ref[...] += dd_g[i] * _onehot_row(h, GATE_PAD)
        for b in range(nb):
            dl_ref[b] = dlog[b]
            dc = dcpad_ref[b, 0:DN_CHUNK, :]
            dx = None
            for j in range(CONV_K):
                term = w_ref[j:j + 1, :] * dcpad_ref[b, pl.ds(CONV_K - 1 - j, DN_CHUNK), :]
                dx = term if dx is None else dx + term
                dw_ref[j:j + 1, :] += _rowsum(dc * xpad_ref[b, pl.ds(CONV_HALO - CONV_K + 1 + j, DN_CHUNK), :])
            dqkv_ref[b] = dx
            dcpad_ref[b, DN_CHUNK:, :] = dcpad_ref[b, 0:CONV_HALO, :]

    chunk = lambda w: pl.BlockSpec((nb, DN_CHUNK, w), lambda n: (0, rev(n), 0))
    return pl.pallas_call(
        body, name="deltanet_bwd", grid=(nc,),
        out_shape=(jax.ShapeDtypeStruct((nb, s, 3 * DN_WIDTH), F32), jax.ShapeDtypeStruct((nb, s, DN_WIDTH), F32),
                   jax.ShapeDtypeStruct((nb, s, GATE_PAD), F32), jax.ShapeDtypeStruct((CONV_K, 3 * DN_WIDTH), F32),
                   jax.ShapeDtypeStruct((1, GATE_PAD), F32), jax.ShapeDtypeStruct((1, GATE_PAD), F32),
                   jax.ShapeDtypeStruct((1, DN_HEAD_DIM), F32)),
        in_specs=_dn_in_specs(nb, rev) + [
            pl.BlockSpec((None, gn, DN_HEAD_DIM, DN_HEAD_DIM), lambda n: (rev(n), 0, 0, 0)),
            chunk(DN_WIDTH)],
        out_specs=(chunk(3 * DN_WIDTH), chunk(DN_WIDTH), chunk(GATE_PAD), _whole((CONV_K, 3 * DN_WIDTH)),
                   _whole((1, GATE_PAD)), _whole((1, GATE_PAD)), _whole((1, DN_HEAD_DIM))),
        scratch_shapes=[pltpu.VMEM((gn, DN_HEAD_DIM, DN_HEAD_DIM), F32),
                        pltpu.VMEM((nb, CONV_HALO + DN_CHUNK, 3 * DN_WIDTH), F32),
                        pltpu.VMEM((nb, DN_CHUNK, 3 * DN_WIDTH), F32),
                        pltpu.VMEM((nb, DN_CHUNK + CONV_HALO, 3 * DN_WIDTH), F32)],
        compiler_params=_params(1),
    )(qkv, qkv, zg, logits, conv_w, alog, dtb, og, states, d_out)


def _head(a_out, b_out, x2, p2, target, w_out, w_out_t, w_gate, w_gate_t, w_proj, ple_g, fin_g):
    t = x2.shape[0]
    tm = min(256, t)
    steps = t // tm

    def body(a_ref, b_ref, x_ref, p_ref, y_ref, wo_ref, wot_ref, wg_ref, wgt_ref, wp_ref, pg_ref, fg_ref,
             da_ref, db_ref, dh_ref, dwo_hbm, dwg_hbm, dwp_hbm, dpg_ref, dfg_ref, loss_ref,
             dwo_acc, dwg_acc, dwp_acc):
        i = pl.program_id(0)

        @pl.when(i == 0)
        def _():
            dwo_acc[...] = jnp.zeros_like(dwo_acc)
            dwg_acc[...] = jnp.zeros_like(dwg_acc)
            dwp_acc[...] = jnp.zeros_like(dwp_acc)
            dpg_ref[...] = jnp.zeros_like(dpg_ref)
            dfg_ref[...] = jnp.zeros_like(dfg_ref)
            loss_ref[...] = jnp.zeros_like(loss_ref)

        a = a_ref[...]
        bb = b_ref[...]
        pb = p_ref[...].astype(BF16)
        pg = pg_ref[...]
        fg = fg_ref[...]
        h1 = (x_ref[...] + jnp.dot(a, wo_ref[0:SGU_WIDTH, :], preferred_element_type=F32)
              + jnp.dot(bb, wo_ref[SGU_WIDTH:, :], preferred_element_type=F32))
        n1, r1 = _rms(h1)
        rn = (n1 * pg).astype(BF16)
        gate = jax.nn.sigmoid(jnp.dot(rn, wg_ref[...], preferred_element_type=F32))
        pp = jnp.dot(pb, wp_ref[...], preferred_element_type=F32)
        h2 = h1 + gate * pp
        n2, r2 = _rms(h2)
        err = n2 * fg - y_ref[...]
        loss_ref[...] += jnp.broadcast_to(_rowsum(jnp.sum(err * err, axis=-1, keepdims=True)), loss_ref.shape)

        dy = err * (1.0 / D_MODEL)
        dfg_ref[...] += _rowsum(dy * n2)
        dh2 = _rms_bwd(dy * fg, n2, r2)
        dpp = (dh2 * gate).astype(BF16)
        dgl = (dh2 * pp * gate * (1.0 - gate)).astype(BF16)
        dwp_acc[...] += lax.dot_general(pb, dpp, (((0,), (0,)), ((), ())), preferred_element_type=F32)
        dwg_acc[...] += lax.dot_general(rn, dgl, (((0,), (0,)), ((), ())), preferred_element_type=F32)
        drn = jnp.dot(dgl, wgt_ref[...], preferred_element_type=F32)
        dpg_ref[...] += _rowsum(drn * n1)
        dh1 = dh2 + _rms_bwd(drn * pg, n1, r1)
        dh_ref[...] = dh1
        dhb = dh1.astype(BF16)
        da_ref[...] = jnp.dot(dhb, wot_ref[:, 0:SGU_WIDTH], preferred_element_type=F32)
        db_ref[...] = jnp.dot(dhb, wot_ref[:, SGU_WIDTH:], preferred_element_type=F32)
        dwo_acc[0:SGU_WIDTH, :] += lax.dot_general(a, dhb, (((0,), (0,)), ((), ())), preferred_element_type=F32)
        dwo_acc[SGU_WIDTH:, :] += lax.dot_general(bb, dhb, (((0,), (0,)), ((), ())), preferred_element_type=F32)

        @pl.when(i == steps - 1)
        def _():
            pltpu.sync_copy(dwo_acc, dwo_hbm)
            pltpu.sync_copy(dwg_acc, dwg_hbm)
            for j in range(N_DEV):
                pltpu.sync_copy(dwp_acc.at[:, pl.ds(j * LANES, LANES)], dwp_hbm.at[j])

    tile = lambda w: pl.BlockSpec((tm, w), lambda i: (i, 0))
    return pl.pallas_call(
        body, name="head_fwd_bwd", grid=(steps,),
        out_shape=(jax.ShapeDtypeStruct((t, SGU_WIDTH), F32), jax.ShapeDtypeStruct((t, DN_WIDTH), F32),
                   jax.ShapeDtypeStruct((t, D_MODEL), F32), jax.ShapeDtypeStruct((D_MODEL, D_MODEL), F32),
                   jax.ShapeDtypeStruct((D_MODEL, D_MODEL), F32), jax.ShapeDtypeStruct((N_DEV, PLE_DIM, LANES), F32),
                   jax.ShapeDtypeStruct((1, D_MODEL), F32), jax.ShapeDtypeStruct((1, D_MODEL), F32),
                   jax.ShapeDtypeStruct((8, LANES), F32)),
        in_specs=[tile(SGU_WIDTH), tile(DN_WIDTH), tile(D_MODEL), tile(PLE_DIM), tile(D_MODEL),
                  VMEM_SPEC, VMEM_SPEC, VMEM_SPEC, VMEM_SPEC, VMEM_SPEC, _whole((1, D_MODEL)), _whole((1, D_MODEL))],
        out_specs=(tile(SGU_WIDTH), tile(DN_WIDTH), tile(D_MODEL), HBM_SPEC, HBM_SPEC, HBM_SPEC,
                   _whole((1, D_MODEL)), _whole((1, D_MODEL)), _whole((8, LANES))),
        scratch_shapes=[pltpu.VMEM((D_MODEL, D_MODEL), F32), pltpu.VMEM((D_MODEL, D_MODEL), F32),
                        pltpu.VMEM((PLE_DIM, D_MODEL), F32)],
        compiler_params=_params(1),
    )(a_out, b_out, x2, p2, target, w_out, w_out_t, w_gate, w_gate_t, w_proj, ple_g, fin_g)


def _inproj_bwd(x2, dh1, d_a, d_q, d_z, d_l, norm_g, wat, wqt, wzt, wgt):
    t = x2.shape[0]
    tm = min(256, t)
    steps = t // tm

    widths = (d_a.shape[1], d_q.shape[1], d_z.shape[1], d_l.shape[1])
    starts = (0, widths[0], widths[0] + widths[1], widths[0] + widths[1] + widths[2])

    def body(x_ref, dh_ref, da_ref, dq_ref, dz_ref, dl_ref, g_ref, wat_ref, wqt_ref, wzt_ref, wgt_ref,
             dx_ref, dw_hbm, dg_ref, dw_acc, stage_ref):
        i = pl.program_id(0)

        @pl.when(i == 0)
        def _():
            dw_acc[...] = jnp.zeros_like(dw_acc)
            dg_ref[...] = jnp.zeros_like(dg_ref)

        g = g_ref[...]
        n, r = _rms(x_ref[...])
        xn = (n * g).astype(BF16)
        dxn = None
        for d_ref, wt_ref, col0 in zip((da_ref, dq_ref, dz_ref, dl_ref), (wat_ref, wqt_ref, wzt_ref, wgt_ref), starts):
            width = d_ref.shape[1]
            for c0 in range(0, width, 512):
                c1 = min(c0 + 512, width)
                d = d_ref[:, c0:c1].astype(BF16)
                term = jnp.dot(d, wt_ref[c0:c1, :], preferred_element_type=F32)
                dxn = term if dxn is None else dxn + term
                dw_acc[:, col0 + c0:col0 + c1] += lax.dot_general(xn, d, (((0,), (0,)), ((), ())),
                                                                  preferred_element_type=F32)
        dg_ref[...] += _rowsum(dxn * n)
        dx_ref[...] = dh_ref[...] + _rms_bwd(dxn * g, n, r)

        @pl.when(i == steps - 1)
        def _():
            for j in range(N_DEV):
                stage_ref[...] = dw_acc[:, j * IN_SHARD:(j + 1) * IN_SHARD]
                pltpu.sync_copy(stage_ref, dw_hbm.at[j])

    tile = lambda w: pl.BlockSpec((tm, w), lambda i: (i, 0))
    return pl.pallas_call(
        body, name="inproj_bwd", grid=(steps,),
        out_shape=(jax.ShapeDtypeStruct((t, D_MODEL), F32), jax.ShapeDtypeStruct((N_DEV, D_MODEL, IN_SHARD), F32),
                   jax.ShapeDtypeStruct((1, D_MODEL), F32)),
        in_specs=[tile(D_MODEL), tile(D_MODEL)] + [tile(w) for w in widths] + [_whole((1, D_MODEL))] + [VMEM_SPEC] * 4,
        out_specs=(tile(D_MODEL), HBM_SPEC, _whole((1, D_MODEL))),
        scratch_shapes=[pltpu.VMEM((D_MODEL, sum(widths)), F32), pltpu.VMEM((D_MODEL, IN_SHARD), F32)],
        compiler_params=_params(1),
    )(x2, dh1, d_a, d_q, d_z, d_l, norm_g, wat, wqt, wzt, wgt)


def _reduce_adamw(recv, w, m, v, name, row_block=None):
    n, rows, cols = recv.shape
    rb = row_block or rows

    def body(r_ref, w_ref, m_ref, v_ref, g_ref, d_ref, nm_ref, nv_ref):
        g = r_ref[0].astype(F32)
        for i in range(1, n):
            g = g + r_ref[i].astype(F32)
        m_new = ADAM_B1 * m_ref[...] + (1.0 - ADAM_B1) * g
        v_new = ADAM_B2 * v_ref[...] + (1.0 - ADAM_B2) * jnp.square(g)
        m_hat = m_new / (1.0 - ADAM_B1 ** ADAM_STEP)
        v_hat = v_new / (1.0 - ADAM_B2 ** ADAM_STEP)
        g_ref[...] = g
        d_ref[...] = -ADAM_LR * (m_hat / (jnp.sqrt(v_hat) + ADAM_EPS) + ADAM_WD * w_ref[...])
        nm_ref[...] = m_new
        nv_ref[...] = v_new

    blk = pl.BlockSpec((rb, cols), lambda i: (i, 0))
    return pl.pallas_call(
        body, name=name, grid=(rows // rb,),
        out_shape=tuple(jax.ShapeDtypeStruct((rows, cols), F32) for _ in range(4)),
        in_specs=[pl.BlockSpec((n, rb, cols), lambda i: (0, i, 0)), blk, blk, blk],
        out_specs=(blk, blk, blk, blk),
        compiler_params=_params(1),
    )(recv, w, m, v)


def _pack_rows(pieces, rows, dtype):
    flat = jnp.concatenate([jnp.ravel(p).astype(dtype) for p in pieces])
    flat = jnp.pad(flat, (0, rows * LANES - flat.shape[0]))
    return flat.reshape(rows, LANES)


def _unpack(pack, layout):
    flat = pack.reshape(-1)
    out, off = {}, 0
    for name, shape in layout:
        n = _size(shape)
        out[name] = flat[off:off + n].reshape(shape)
        off += n
    return out


def kernel(x, p, norm_g, w_in, sgu_ln_g, sgu_ln_b, sgu_w_s, sgu_b_s, dn_conv_w, dn_a_log, dn_dt_bias, dn_o_norm_g, w_out, ple_norm_g, ple_gate_w, ple_proj_w, final_norm_g, loss_target, m_norm_g, m_w_in, m_sgu_ln_g, m_sgu_ln_b, m_sgu_w_s, m_sgu_b_s, m_dn_conv_w, m_dn_a_log, m_dn_dt_bias, m_dn_o_norm_g, m_w_out, m_ple_norm_g, m_ple_gate_w, m_ple_proj_w, m_final_norm_g, v_norm_g, v_w_in, v_sgu_ln_g, v_sgu_ln_b, v_sgu_w_s, v_sgu_b_s, v_dn_conv_w, v_dn_a_log, v_dn_dt_bias, v_dn_o_norm_g, v_w_out, v_ple_norm_g, v_ple_gate_w, v_ple_proj_w, v_final_norm_g):
    weights = dict(norm_g=norm_g, w_in=w_in, sgu_ln_g=sgu_ln_g, sgu_ln_b=sgu_ln_b, sgu_w_s=sgu_w_s, sgu_b_s=sgu_b_s,
                   dn_conv_w=dn_conv_w, dn_a_log=dn_a_log, dn_dt_bias=dn_dt_bias, dn_o_norm_g=dn_o_norm_g, w_out=w_out,
                   ple_norm_g=ple_norm_g, ple_gate_w=ple_gate_w, ple_proj_w=ple_proj_w, final_norm_g=final_norm_g)
    mom1 = dict(norm_g=m_norm_g, w_in=m_w_in, sgu_ln_g=m_sgu_ln_g, sgu_ln_b=m_sgu_ln_b, sgu_w_s=m_sgu_w_s,
                sgu_b_s=m_sgu_b_s, dn_conv_w=m_dn_conv_w, dn_a_log=m_dn_a_log, dn_dt_bias=m_dn_dt_bias,
                dn_o_norm_g=m_dn_o_norm_g, w_out=m_w_out, ple_norm_g=m_ple_norm_g, ple_gate_w=m_ple_gate_w,
                ple_proj_w=m_ple_proj_w, final_norm_g=m_final_norm_g)
    mom2 = dict(norm_g=v_norm_g, w_in=v_w_in, sgu_ln_g=v_sgu_ln_g, sgu_ln_b=v_sgu_ln_b, sgu_w_s=v_sgu_w_s,
                sgu_b_s=v_sgu_b_s, dn_conv_w=v_dn_conv_w, dn_a_log=v_dn_a_log, dn_dt_bias=v_dn_dt_bias,
                dn_o_norm_g=v_dn_o_norm_g, w_out=v_w_out, ple_norm_g=v_ple_norm_g, ple_gate_w=v_ple_gate_w,
                ple_proj_w=v_ple_proj_w, final_norm_g=v_final_norm_g)
    nb, s, _ = x.shape
    t = nb * s

    full = dict(zip(("w_in", "w_out", "ple_gate_w", "ple_proj_w", "dn_conv_w"), _all_gather(
        [w_in[0].astype(BF16), w_out[0].astype(BF16), ple_gate_w[0].astype(BF16), ple_proj_w[0].astype(BF16),
         dn_conv_w[0]])))
    w_in_full = jnp.moveaxis(full["w_in"], 0, 1).reshape(D_MODEL, IN_COLS)
    wa = w_in_full[:, :3 * SGU_WIDTH]
    wq = w_in_full[:, 3 * SGU_WIDTH:3 * SGU_WIDTH + 3 * DN_WIDTH]
    wz = w_in_full[:, 3 * SGU_WIDTH + 3 * DN_WIDTH:3 * SGU_WIDTH + 4 * DN_WIDTH]
    wg = jnp.pad(w_in_full[:, 3 * SGU_WIDTH + 4 * DN_WIDTH:], ((0, 0), (0, GATE_PAD - 2 * DN_HEADS)))
    w_out_full = full["w_out"].reshape(D_MODEL, D_MODEL)
    w_gate_full = full["ple_gate_w"].reshape(D_MODEL, D_MODEL)
    w_proj_full = jnp.moveaxis(full["ple_proj_w"], 0, 1).reshape(PLE_DIM, D_MODEL)
    conv_full = jnp.moveaxis(full["dn_conv_w"], 0, 1).reshape(CONV_K, 3 * DN_WIDTH)

    pad_row = lambda a: jnp.pad(a.reshape(1, -1), ((0, 0), (0, GATE_PAD - a.size)))
    alog, dtb = pad_row(dn_a_log), pad_row(dn_dt_bias)
    og = dn_o_norm_g.reshape(1, DN_HEAD_DIM)
    ws = sgu_w_s.reshape(SGU_GROUPS, SGU_CHUNK, SGU_CHUNK)
    b_t = sgu_b_s.reshape(SGU_GROUPS, SGU_CHUNK).T
    fin_g = final_norm_g.reshape(1, D_MODEL)

    x2 = x.reshape(t, D_MODEL)
    a_uvz, b_qkv, b_z, b_l = _inproj_fwd(x2, norm_g, wa, wq, wz, wg)
    a_out = _sgu_fwd(a_uvz, sgu_ln_g, sgu_ln_b, ws, b_t)
    qkv3 = b_qkv.reshape(nb, s, 3 * DN_WIDTH)
    z3 = b_z.reshape(nb, s, DN_WIDTH)
    l3 = b_l.reshape(nb, s, GATE_PAD)
    b_out, states = _dn_fwd(qkv3, z3, l3, conv_full, alog, dtb, og)

    d_a, d_b, dh1, g_w_out, g_gate, g_proj, g_ple_g, g_fin_g, loss_tile = _head(
        a_out, b_out.reshape(t, DN_WIDTH), x2, p.reshape(t, PLE_DIM), loss_target.reshape(t, D_MODEL),
        w_out_full, w_out_full.T, w_gate_full, w_gate_full.T, w_proj_full, ple_norm_g, fin_g)
    d_qkv, d_z, d_l, g_conv, g_alog, g_dtb, g_og = _dn_bwd(
        qkv3, z3, l3, conv_full, alog, dtb, og, states, d_b.reshape(nb, s, DN_WIDTH))
    d_uvz, g_ln_g, g_ln_b, g_ws, g_bt = _sgu_bwd(a_uvz, d_a, sgu_ln_g, sgu_ln_b, ws, b_t)
    grad_x, g_w_in, g_norm = _inproj_bwd(
        x2, dh1, d_uvz, d_qkv.reshape(t, 3 * DN_WIDTH), d_z.reshape(t, DN_WIDTH), d_l.reshape(t, GATE_PAD),
        norm_g, wa.T, wq.T, wz.T, wg.T)

    by_device = [g_w_in, g_w_out.reshape(N_DEV, 128, D_MODEL), g_gate.reshape(N_DEV, 128, D_MODEL), g_proj]
    small = _pack_rows([g_conv, g_norm, g_ln_g, g_ln_b, g_ws, g_bt.T, g_alog[:, :DN_HEADS], g_dtb[:, :DN_HEADS], g_og,
                        g_ple_g, g_fin_g, (0.5 / D_MODEL) * loss_tile[0:1, 0:1]], SMALL_ROWS, F32)
    *from_sibling, small_sibling = _sibling_exchange(by_device, small)
    core = lax.axis_index("c").astype(jnp.int32).reshape(1)
    *chip_sums, small_sum = _pair_sum(core, by_device, from_sibling, small, small_sibling)
    *received, small_received = _chip_exchange(chip_sums, small_sum)

    results = {}
    for name, recv, rb in zip(("w_in", "w_out", "ple_gate_w", "ple_proj_w"), received, (128, None, None, None)):
        shape = weights[name].shape
        outs = _reduce_adamw(recv, weights[name][0], mom1[name][0], mom2[name][0], "adamw_" + name, rb)
        results[name] = [a.reshape(shape) for a in outs]
    names = [name for name, _ in REPLICATED]
    zeros = jnp.zeros((CONV_K, 3 * DN_WIDTH), F32)
    small_outs = _reduce_adamw(small_received, *[_pack_rows([zeros] + [src[k] for k in names], SMALL_ROWS, F32)
                                                 for src in (weights, mom1, mom2)], "adamw_replicated")
    layout = (SMALL_LAYOUT[0],) + tuple((k, weights[k].shape) for k in names) + (SMALL_LAYOUT[-1],)
    unpacked = [_unpack(a, layout) for a in small_outs]
    for k in names:
        results[k] = [u[k] for u in unpacked]
    loss = unpacked[0]["loss"][0]
    me = 4 * lax.axis_index("x") + 2 * lax.axis_index("y") + lax.axis_index("c")
    conv_mine = lax.dynamic_slice(unpacked[0]["conv"], (0, me * 192), (CONV_K, 192))
    outs = _reduce_adamw(conv_mine[None], dn_conv_w[0], m_dn_conv_w[0], v_dn_conv_w[0], "adamw_dn_conv_w")
    results["dn_conv_w"] = [a.reshape(dn_conv_w.shape) for a in outs]

    return (loss, grad_x.reshape(nb, s, D_MODEL), *[results[k][0] for k in WEIGHT_ORDER],
            *[results[k][1] for k in WEIGHT_ORDER], *[results[k][2] for k in WEIGHT_ORDER],
            *[results[k][3] for k in WEIGHT_ORDER])
```

```python
import jax
import jax.numpy as jnp
from jax import lax
from jax.experimental import pallas as pl
from jax.experimental.pallas import tpu as pltpu

F32 = jnp.float32
BF16 = jnp.bfloat16

N_DEV = 8
D_MODEL = 1024
SGU_WIDTH = 512
SGU_GROUPS = 4
SGU_CHUNK = 128
DN_WIDTH = 512
DN_HEADS = 4
DN_HEAD_DIM = 128
DN_CHUNK = 64
CONV_K = 4
CONV_HALO = 8
PLE_DIM = 256
EPS = 1e-6
IN_COLS = 3592
IN_SHARD = IN_COLS // N_DEV
GATE_PAD = 128

ADAM_LR = 0.001
ADAM_B1 = 0.9
ADAM_B2 = 0.999
ADAM_EPS = 1e-08
ADAM_WD = 0.01
ADAM_STEP = 10

LANES = 128
VMEM_LIMIT = 56 * 1024 * 1024
MESH = pl.DeviceIdType.MESH

REPLICATED = (("norm_g", (1, D_MODEL)), ("sgu_ln_g", (1, SGU_WIDTH)), ("sgu_ln_b", (1, SGU_WIDTH)),
              ("sgu_w_s", (1, SGU_GROUPS, SGU_CHUNK, SGU_CHUNK)), ("sgu_b_s", (1, SGU_GROUPS, SGU_CHUNK)),
              ("dn_a_log", (1, DN_HEADS)), ("dn_dt_bias", (1, DN_HEADS)), ("dn_o_norm_g", (1, DN_HEAD_DIM)),
              ("ple_norm_g", (1, D_MODEL)), ("final_norm_g", (D_MODEL,)))
WEIGHT_ORDER = ("norm_g", "w_in", "sgu_ln_g", "sgu_ln_b", "sgu_w_s", "sgu_b_s", "dn_conv_w", "dn_a_log",
                "dn_dt_bias", "dn_o_norm_g", "w_out", "ple_norm_g", "ple_gate_w", "ple_proj_w", "final_norm_g")


def _size(shape):
    n = 1
    for s in shape:
        n *= s
    return n


SMALL_LAYOUT = (("conv", (CONV_K, 3 * DN_WIDTH)),) + REPLICATED + (("loss", (1,)),)
N_SMALL = sum(_size(s) for _, s in SMALL_LAYOUT)
SMALL_ROWS = -(-N_SMALL // (8 * LANES)) * 8


def _bdot(a, b):
    return jnp.dot(a.astype(BF16), b.astype(BF16), preferred_element_type=F32)


def _sigmoid(x):
    return pl.reciprocal(1.0 + jnp.exp(-x), approx=True)


@jax.custom_vjp
def _silu(x):
    return x * _sigmoid(x)


def _silu_fwd(x):
    s = _sigmoid(x)
    return x * s, (x, s)


def _silu_bwd(res, ct):
    x, s = res
    return (ct * (s * (1.0 + x * (1.0 - s))),)


_silu.defvjp(_silu_fwd, _silu_bwd)


def _gelu(x):
    return 0.5 * x * (1.0 + lax.erf(x * (0.5 ** 0.5)))


def _softplus(x):
    return jnp.maximum(x, 0.0) + jnp.log1p(jnp.exp(-jnp.abs(x)))


def _l2n(x):
    return x * lax.rsqrt(jnp.sum(x * x, axis=-1, keepdims=True) + EPS)


def _rms(x):
    r = lax.rsqrt(jnp.mean(x * x, axis=-1, keepdims=True) + EPS)
    return x * r, r


def _rms_bwd(dn, n, r):
    return r * (dn - n * jnp.mean(dn * n, axis=-1, keepdims=True))


def _onehot_row(idx, width):
    return (lax.broadcasted_iota(jnp.int32, (1, width), 1) == idx).astype(F32)


def _rowsum(x):
    return jnp.sum(x, axis=0, keepdims=True)


def _iota2(n):
    return lax.broadcasted_iota(jnp.int32, (n, n), 0), lax.broadcasted_iota(jnp.int32, (n, n), 1)


def _bmm(a, b):
    return lax.dot_general(a.astype(BF16), b.astype(BF16), (((2,), (1,)), ((0,), (0,))), preferred_element_type=F32)


def _bmm_nt(a, b):
    return lax.dot_general(a.astype(BF16), b.astype(BF16), (((2,), (2,)), ((0,), (0,))), preferred_element_type=F32)


def _bmm_tn(a, b):
    return lax.dot_general(a.astype(BF16), b.astype(BF16), (((1,), (1,)), ((0,), (0,))), preferred_element_type=F32)


def _tri_inv_impl(a):
    n = a.shape[-1]
    r, c = _iota2(n)
    x = r ^ c
    blk16 = x < 16
    blk32 = x < 32
    eye = (r == c).astype(F32)
    ad = jnp.where(blk16, a, 0.0)
    p2 = _bmm(ad, ad)
    e = p2 - ad - _bmm(ad, p2)
    p4 = _bmm(p2, p2)
    e = e + p4 + _bmm(e, p4)
    p8 = _bmm(p4, p4)
    e = e + p8 + _bmm(e, p8)
    m1 = jnp.where(jnp.logical_and(blk32, jnp.logical_not(blk16)), a, 0.0)
    f = m1 + _bmm(m1, e)
    e = e - f - _bmm(e, f)
    m2 = jnp.where(blk32, 0.0, a)
    f = m2 + _bmm(m2, e)
    e = e - f - _bmm(e, f)
    return e + eye


@jax.custom_vjp
def _tri_inv(a):
    return _tri_inv_impl(a)


def _tri_inv_fwd(a):
    t = _tri_inv_impl(a)
    return t, t


def _tri_inv_bwd(t, dt):
    return (-_bmm_tn(t, _bmm_nt(dt, t)),)


_tri_inv.defvjp(_tri_inv_fwd, _tri_inv_bwd)


def _sgu_core(u, v, z, lg, lb, ws, bcol):
    n = ws.shape[0]
    r, c = _iota2(n)
    wm = jnp.where(r >= c, ws, 0.0)
    gu = _gelu(u)
    gv = _gelu(v)
    xc = gv - jnp.mean(gv, axis=-1, keepdims=True)
    ln = xc * lax.rsqrt(jnp.mean(xc * xc, axis=-1, keepdims=True) + EPS) * lg + lb
    s = _bdot(wm, ln) + bcol
    return gu * s * _silu(z)


def _lanes_of(x):
    return jnp.concatenate([x[i] for i in range(x.shape[0])], axis=1)


def _batch_of(x, width):
    return jnp.concatenate([x[None, :, i * width:(i + 1) * width] for i in range(x.shape[1] // width)], axis=0)


def _mask_dot(mask, x):
    hi = x.astype(BF16)
    lo = (x - hi.astype(F32)).astype(BF16)
    m = mask.astype(BF16)
    return jnp.dot(m, hi, preferred_element_type=F32) + jnp.dot(m, lo, preferred_element_type=F32)


def _tri_mask(n, upper):
    r, c = _iota2(n)
    return (r <= c) if upper else (r >= c)


@jax.custom_vjp
def _cumsum_rows(x):
    return _mask_dot(_tri_mask(x.shape[0], False), x)


def _cumsum_rows_fwd(x):
    return _cumsum_rows(x), None


def _cumsum_rows_bwd(_, ct):
    return (_bdot(_tri_mask(ct.shape[0], True), ct),)


_cumsum_rows.defvjp(_cumsum_rows_fwd, _cumsum_rows_bwd)


@jax.custom_vjp
def _colsum_all_rows(x):
    return _mask_dot(jnp.ones((x.shape[0], x.shape[0]), jnp.bool_), x)


def _colsum_all_rows_fwd(x):
    return _colsum_all_rows(x), None


def _colsum_all_rows_bwd(_, ct):
    return (_bdot(jnp.ones((ct.shape[0], ct.shape[0]), F32), ct),)


_colsum_all_rows.defvjp(_colsum_all_rows_fwd, _colsum_all_rows_bwd)


def _dn_core(cq, ck, cv, z, logits, state, alog, dtb, og):
    gn, cn, dh = cq.shape
    heads = gn // logits.shape[0]
    q = _l2n(_silu(cq)) * (dh ** -0.5)
    k = _l2n(_silu(ck))
    v = _silu(cv)
    beta_lanes = jax.nn.sigmoid(logits)
    g_lanes = -jnp.exp(alog) * _softplus(logits + dtb)
    column = lambda rows, lane: jnp.sum(rows * _onehot_row(lane, rows.shape[-1]), axis=-1, keepdims=True)[None]
    beta = jnp.concatenate([column(beta_lanes[i // heads], i % heads) for i in range(gn)], axis=0)
    g = jnp.concatenate([column(g_lanes[i // heads], heads + i % heads) for i in range(gn)], axis=0)
    r, c = _iota2(cn)
    tril = r >= c
    rw = lax.broadcasted_iota(jnp.int32, (cn, dh), 0)
    cw = lax.broadcasted_iota(jnp.int32, (cn, dh), 1)
    upper_wide = (rw <= cw).astype(F32)
    g_wide = jnp.broadcast_to(g, (gn, cn, dh))
    gc_wide = _batch_of(_cumsum_rows(_lanes_of(g_wide)), dh)
    gc_cols = _batch_of(_colsum_all_rows(_lanes_of(g_wide * upper_wide)), dh)[:, :, :cn]
    decay = jnp.exp(jnp.where(tril, gc_wide[:, :, :cn] - gc_cols, -1e30))
    kb = k * beta
    kk = _bmm_nt(kb, k) * decay
    t = _tri_inv(jnp.where(r > c, kk, 0.0))
    eg = jnp.exp(gc_wide)
    sol = _bmm(t, jnp.concatenate([v * beta, kb * eg], axis=-1))
    u_val, w_dec = sol[:, :, :dh], sol[:, :, dh:]
    qk = _bmm_nt(q, k) * decay
    g_last = jnp.sum(g_wide, axis=1, keepdims=True)
    k_dec = k * jnp.exp(g_last - gc_wide)
    ws = _bmm(jnp.concatenate([w_dec, q * eg], axis=1), state)
    v_new = u_val - ws[:, :cn]
    o = ws[:, cn:] + _bmm(qk, v_new)
    new_state = state * jnp.exp(g_last) + _bmm_tn(k_dec, v_new)
    on, _ = _rms(o)
    return on * og * _silu(z), new_state


N_CHIPS = 4
HBM_SPEC = pl.BlockSpec(memory_space=pl.ANY)


def _place():
    return lax.axis_index("x"), lax.axis_index("y"), lax.axis_index("c")


def _other_chip(k):
    x, y, _ = _place()
    px = 1 - x if k & 2 else x
    py = 1 - y if k & 1 else y
    return px, py, 2 * px + py


def _remote(src, dst, send_sem, recv_sem, device):
    return pltpu.make_async_remote_copy(src_ref=src, dst_ref=dst, send_sem=send_sem, recv_sem=recv_sem,
                                        device_id=device, device_id_type=MESH)


def _all_gather(shards):
    n = len(shards)

    def body(*refs):
        srcs, outs = refs[:n], refs[n:2 * n]
        send_sems, recv_sems, local_sems = refs[2 * n:]
        x, y, c = _place()
        me = 4 * x + 2 * y + c
        sibling = (x, y, 1 - c)
        local = [pltpu.make_async_copy(srcs[a], outs[a].at[me], local_sems.at[a]) for a in range(n)]
        for cp in local:
            cp.start()
        sends = []
        for a in range(n):
            sends.append(_remote(srcs[a], outs[a].at[me], send_sems.at[a, 0], recv_sems.at[a, 0], sibling))
        for k in range(1, N_CHIPS):
            px, py, _ = _other_chip(k)
            for a in range(n):
                sends.append(_remote(srcs[a], outs[a].at[me], send_sems.at[a, k], recv_sems.at[a, k], (px, py, c)))
        for cp in sends:
            cp.start()
        passed = []
        for k in range(1, N_CHIPS):
            px, py, _ = _other_chip(k)
            blk = 4 * px + 2 * py + c
            for a in range(n):
                _remote(srcs[a], outs[a].at[blk], send_sems.at[a, k], recv_sems.at[a, k], (px, py, c)).wait_recv()
            for a in range(n):
                cp = _remote(outs[a].at[blk], outs[a].at[blk], send_sems.at[a, 3 + k], recv_sems.at[a, 3 + k], sibling)
                cp.start()
                passed.append(cp)
        for a in range(n):
            _remote(srcs[a], outs[a].at[me + 1 - 2 * c], send_sems.at[a, 0], recv_sems.at[a, 0], sibling).wait_recv()
        for k in range(1, N_CHIPS):
            px, py, _ = _other_chip(k)
            blk = 4 * px + 2 * py + 1 - c
            for a in range(n):
                _remote(srcs[a], outs[a].at[blk], send_sems.at[a, 3 + k], recv_sems.at[a, 3 + k], sibling).wait_recv()
        for cp in sends + passed:
            cp.wait_send()
        for cp in local:
            cp.wait()

    return pl.pallas_call(
        body, name="all_gather_weights",
        out_shape=tuple(jax.ShapeDtypeStruct((N_DEV,) + a.shape, a.dtype) for a in shards),
        in_specs=[HBM_SPEC] * n, out_specs=(HBM_SPEC,) * n,
        scratch_shapes=[pltpu.SemaphoreType.DMA((n, N_DEV - 1)), pltpu.SemaphoreType.DMA((n, N_DEV - 1)),
                        pltpu.SemaphoreType.DMA((n,))],
    )(*shards)


def _sibling_exchange(by_device, small):
    n = len(by_device)

    def body(*refs):
        srcs, small_src = refs[:n], refs[n]
        outs, small_out = refs[n + 1:2 * n + 1], refs[2 * n + 1]
        send_sems, recv_sems = refs[2 * n + 2:]
        x, y, c = _place()
        sibling = (x, y, 1 - c)
        copies = [_remote(small_src, small_out, send_sems.at[n, 0], recv_sems.at[n, 0], sibling)]
        for a in range(n):
            for q in range(N_CHIPS):
                copies.append(_remote(srcs[a].at[2 * q + 1 - c], outs[a].at[q], send_sems.at[a, q], recv_sems.at[a, q],
                                      sibling))
        for cp in copies:
            cp.start()
        for cp in copies:
            cp.wait_recv()
        for cp in copies:
            cp.wait_send()

    return pl.pallas_call(
        body, name="grad_sibling_exchange",
        out_shape=tuple(jax.ShapeDtypeStruct((N_CHIPS,) + a.shape[1:], a.dtype) for a in by_device)
        + (jax.ShapeDtypeStruct(small.shape, small.dtype),),
        in_specs=[HBM_SPEC] * (n + 1), out_specs=(HBM_SPEC,) * (n + 1),
        scratch_shapes=[pltpu.SemaphoreType.DMA((n + 1, N_CHIPS)), pltpu.SemaphoreType.DMA((n + 1, N_CHIPS))],
    )(*by_device, small)


def _chip_exchange(chip_sums, small):
    n = len(chip_sums)

    def body(*refs):
        srcs, small_src = refs[:n], refs[n]
        outs, small_out = refs[n + 1:2 * n + 1], refs[2 * n + 1]
        send_sems, recv_sems, local_sems = refs[2 * n + 2:]
        x, y, c = _place()
        mine = 2 * x + y
        local = [pltpu.make_async_copy(srcs[a].at[mine], outs[a].at[mine], local_sems.at[a]) for a in range(n)]
        local.append(pltpu.make_async_copy(small_src, small_out.at[mine], local_sems.at[n]))
        for cp in local:
            cp.start()
        sends = []
        for k in range(1, N_CHIPS):
            px, py, chip = _other_chip(k)
            for a in range(n):
                sends.append(_remote(srcs[a].at[chip], outs[a].at[mine], send_sems.at[a, k - 1], recv_sems.at[a, k - 1],
                                     (px, py, c)))
            sends.append(_remote(small_src, small_out.at[mine], send_sems.at[n, k - 1], recv_sems.at[n, k - 1], (px, py, c)))
        for cp in sends:
            cp.start()
        for k in range(1, N_CHIPS):
            px, py, chip = _other_chip(k)
            for a in range(n):
                _remote(srcs[a].at[chip], outs[a].at[chip], send_sems.at[a, k - 1], recv_sems.at[a, k - 1],
                        (px, py, c)).wait_recv()
            _remote(small_src, small_out.at[chip], send_sems.at[n, k - 1], recv_sems.at[n, k - 1], (px, py, c)).wait_recv()
        for cp in sends:
            cp.wait_send()
        for cp in local:
            cp.wait()

    return pl.pallas_call(
        body, name="grad_chip_exchange",
        out_shape=tuple(jax.ShapeDtypeStruct(a.shape, a.dtype) for a in chip_sums)
        + (jax.ShapeDtypeStruct((N_CHIPS,) + small.shape, small.dtype),),
        in_specs=[HBM_SPEC] * (n + 1), out_specs=(HBM_SPEC,) * (n + 1),
        scratch_shapes=[pltpu.SemaphoreType.DMA((n + 1, N_CHIPS - 1)), pltpu.SemaphoreType.DMA((n + 1, N_CHIPS - 1)),
                        pltpu.SemaphoreType.DMA((n + 1,))],
    )(*chip_sums, small)


def _pair_sum(core, by_device, from_sibling, small, small_from_sibling):
    n = len(by_device)

    def body(core_ref, *refs):
        own, sib = refs[:n], refs[n:2 * n]
        small_own, small_sib = refs[2 * n], refs[2 * n + 1]
        outs, small_out = refs[2 * n + 2:3 * n + 2], refs[3 * n + 2]
        for a in range(n):
            outs[a][...] = (own[a][...] + sib[a][...]).astype(outs[a].dtype)
        small_out[...] = small_own[...] + small_sib[...]

    def block(a):
        return (None,) + a.shape[1:], (0,) * (a.ndim - 1)

    own_specs = [pl.BlockSpec(block(a)[0], lambda q, core_ref, z=block(a)[1]: (2 * q + core_ref[0],) + z) for a in by_device]
    sib_specs = [pl.BlockSpec(block(a)[0], lambda q, core_ref, z=block(a)[1]: (q,) + z) for a in by_device]
    small_spec = pl.BlockSpec(small.shape, lambda q, core_ref: (0,) * small.ndim)
    return pl.pallas_call(
        body, name="grad_pair_sum",
        grid_spec=pltpu.PrefetchScalarGridSpec(
            num_scalar_prefetch=1, grid=(N_CHIPS,),
            in_specs=own_specs + sib_specs + [small_spec, small_spec],
            out_specs=tuple(sib_specs) + (small_spec,)),
        out_shape=tuple(jax.ShapeDtypeStruct(a.shape, BF16) for a in from_sibling)
        + (jax.ShapeDtypeStruct(small.shape, F32),),
        compiler_params=_params(1),
    )(core, *by_device, *from_sibling, small, small_from_sibling)


def _params(n_axes):
    return pltpu.CompilerParams(dimension_semantics=("arbitrary",) * n_axes, vmem_limit_bytes=VMEM_LIMIT)


def _whole(shape):
    return pl.BlockSpec(shape, lambda *_: (0,) * len(shape))


VMEM_SPEC = pl.BlockSpec(memory_space=pltpu.VMEM)


def _inproj_fwd(x2, norm_g, wa, wq, wz, wg):
    t = x2.shape[0]
    tm = min(512, t)

    def body(x_ref, g_ref, wa_ref, wq_ref, wz_ref, wg_ref, a_ref, q_ref, z_ref, l_ref):
        n, _ = _rms(x_ref[...])
        xn = (n * g_ref[...]).astype(BF16)
        for w_ref, o_ref in ((wa_ref, a_ref), (wq_ref, q_ref), (wz_ref, z_ref), (wg_ref, l_ref)):
            width = w_ref.shape[1]
            for c0 in range(0, width, 512):
                c1 = min(c0 + 512, width)
                o_ref[:, c0:c1] = jnp.dot(xn, w_ref[:, c0:c1], preferred_element_type=F32)

    widths = (wa.shape[1], wq.shape[1], wz.shape[1], wg.shape[1])
    return pl.pallas_call(
        body, name="inproj_fwd", grid=(t // tm,),
        out_shape=tuple(jax.ShapeDtypeStruct((t, w), F32) for w in widths),
        in_specs=[pl.BlockSpec((tm, D_MODEL), lambda i: (i, 0)), _whole((1, D_MODEL)),
                  VMEM_SPEC, VMEM_SPEC, VMEM_SPEC, VMEM_SPEC],
        out_specs=tuple(pl.BlockSpec((tm, w), lambda i: (i, 0)) for w in widths),
        compiler_params=_params(1),
    )(x2, norm_g, wa, wq, wz, wg)


def _sgu_pieces(uvz_ref, lg_ref, lb_ref, ws_ref, bt_ref, row0, grp):
    rows = pl.ds(row0, SGU_CHUNK)
    lanes = pl.ds(grp * 128, 128)
    u = uvz_ref[rows, pl.ds(grp * 128, 128)]
    v = uvz_ref[rows, pl.ds(SGU_WIDTH + grp * 128, 128)]
    z = uvz_ref[rows, pl.ds(2 * SGU_WIDTH + grp * 128, 128)]
    bcol = jnp.sum(bt_ref[...] * _onehot_row(grp, SGU_GROUPS), axis=-1, keepdims=True)
    return u, v, z, lg_ref[:, lanes], lb_ref[:, lanes], ws_ref[grp], bcol


def _sgu_fwd(a_uvz, ln_g, ln_b, w_s, b_t):
    t = a_uvz.shape[0]
    tm = min(512, t)

    def body(uvz_ref, lg_ref, lb_ref, ws_ref, bt_ref, out_ref):
        for row0 in range(0, tm, SGU_CHUNK):
            for grp in range(SGU_GROUPS):
                args = _sgu_pieces(uvz_ref, lg_ref, lb_ref, ws_ref, bt_ref, row0, grp)
                out_ref[pl.ds(row0, SGU_CHUNK), pl.ds(grp * 128, 128)] = _sgu_core(*args).astype(out_ref.dtype)

    return pl.pallas_call(
        body, name="sgu_fwd", grid=(t // tm,),
        out_shape=jax.ShapeDtypeStruct((t, SGU_WIDTH), BF16),
        in_specs=[pl.BlockSpec((tm, 3 * SGU_WIDTH), lambda i: (i, 0)), _whole((1, SGU_WIDTH)), _whole((1, SGU_WIDTH)),
                  _whole((SGU_GROUPS, SGU_CHUNK, SGU_CHUNK)), _whole((SGU_CHUNK, SGU_GROUPS))],
        out_specs=pl.BlockSpec((tm, SGU_WIDTH), lambda i: (i, 0)),
        compiler_params=_params(1),
    )(a_uvz, ln_g, ln_b, w_s, b_t)


def _sgu_bwd(a_uvz, d_out, ln_g, ln_b, w_s, b_t):
    t = a_uvz.shape[0]
    tm = min(512, t)

    def body(uvz_ref, do_ref, lg_ref, lb_ref, ws_ref, bt_ref, duvz_ref, dlg_ref, dlb_ref, dws_ref, dbt_ref):
        @pl.when(pl.program_id(0) == 0)
        def _():
            dlg_ref[...] = jnp.zeros_like(dlg_ref)
            dlb_ref[...] = jnp.zeros_like(dlb_ref)
            dws_ref[...] = jnp.zeros_like(dws_ref)
            dbt_ref[...] = jnp.zeros_like(dbt_ref)

        for row0 in range(0, tm, SGU_CHUNK):
            rows = pl.ds(row0, SGU_CHUNK)
            for grp in range(SGU_GROUPS):
                lanes = pl.ds(grp * 128, 128)
                args = _sgu_pieces(uvz_ref, lg_ref, lb_ref, ws_ref, bt_ref, row0, grp)
                _, pull = jax.vjp(_sgu_core, *args)
                du, dv, dz, dlg, dlb, dws, dbcol = pull(do_ref[rows, lanes])
                duvz_ref[rows, pl.ds(grp * 128, 128)] = du
                duvz_ref[rows, pl.ds(SGU_WIDTH + grp * 128, 128)] = dv
                duvz_ref[rows, pl.ds(2 * SGU_WIDTH + grp * 128, 128)] = dz
                dlg_ref[:, lanes] += dlg
                dlb_ref[:, lanes] += dlb
                dws_ref[grp] += dws
                dbt_ref[...] += dbcol * _onehot_row(grp, SGU_GROUPS)

    return pl.pallas_call(
        body, name="sgu_bwd", grid=(t // tm,),
        out_shape=(jax.ShapeDtypeStruct((t, 3 * SGU_WIDTH), F32), jax.ShapeDtypeStruct((1, SGU_WIDTH), F32),
                   jax.ShapeDtypeStruct((1, SGU_WIDTH), F32), jax.ShapeDtypeStruct((SGU_GROUPS, SGU_CHUNK, SGU_CHUNK), F32),
                   jax.ShapeDtypeStruct((SGU_CHUNK, SGU_GROUPS), F32)),
        in_specs=[pl.BlockSpec((tm, 3 * SGU_WIDTH), lambda i: (i, 0)), pl.BlockSpec((tm, SGU_WIDTH), lambda i: (i, 0)),
                  _whole((1, SGU_WIDTH)), _whole((1, SGU_WIDTH)),
                  _whole((SGU_GROUPS, SGU_CHUNK, SGU_CHUNK)), _whole((SGU_CHUNK, SGU_GROUPS))],
        out_specs=(pl.BlockSpec((tm, 3 * SGU_WIDTH), lambda i: (i, 0)), _whole((1, SGU_WIDTH)), _whole((1, SGU_WIDTH)),
                   _whole((SGU_GROUPS, SGU_CHUNK, SGU_CHUNK)), _whole((SGU_CHUNK, SGU_GROUPS))),
        compiler_params=_params(1),
    )(a_uvz, d_out, ln_g, ln_b, w_s, b_t)


def _dn_conv(xpad_ref, c_ref, cur_ref, prev_ref, w_ref, first):
    for b in range(cur_ref.shape[0]):
        xpad_ref[b, 0:CONV_HALO, :] = jnp.where(first, 0.0, prev_ref[b])
        xpad_ref[b, CONV_HALO:, :] = cur_ref[b]
        acc = None
        for j in range(CONV_K):
            term = w_ref[j:j + 1, :] * xpad_ref[b, pl.ds(CONV_HALO - CONV_K + 1 + j, DN_CHUNK), :]
            acc = term if acc is None else acc + term
        c_ref[b] = acc


def _dn_pairs(nb):
    return [(b, h) for b in range(nb) for h in range(DN_HEADS)]


def _dn_batch_args(c_ref, z_ref):
    pairs = _dn_pairs(c_ref.shape[0])
    pick = lambda ref, b, col: ref[b, :, pl.ds(col, DN_HEAD_DIM)]
    cq = jnp.stack([pick(c_ref, b, h * DN_HEAD_DIM) for b, h in pairs])
    ck = jnp.stack([pick(c_ref, b, DN_WIDTH + h * DN_HEAD_DIM) for b, h in pairs])
    cv = jnp.stack([pick(c_ref, b, 2 * DN_WIDTH + h * DN_HEAD_DIM) for b, h in pairs])
    z = jnp.stack([pick(z_ref, b, h * DN_HEAD_DIM) for b, h in pairs])
    return cq, ck, cv, z


def _dn_weight_specs():
    return [_whole((CONV_K, 3 * DN_WIDTH)), _whole((1, GATE_PAD)), _whole((1, GATE_PAD)), _whole((1, DN_HEAD_DIM))]


def _dn_fwd(qkv, zg, logits, conv_w, alog, dtb, og):
    nb, s, _ = qkv.shape
    nc = s // DN_CHUNK
    pairs = _dn_pairs(nb)
    gn = len(pairs)
    chunk = lambda w: pl.BlockSpec((nb, DN_CHUNK, w), lambda n: (0, n, 0))
    prev = lambda n: (0, jnp.maximum(n * (DN_CHUNK // CONV_HALO) - 1, 0), 0)

    def body(cur_ref, prev_ref, z_ref, l_ref, w_ref, alog_ref, dtb_ref, og_ref, out_ref, st_ref, c_ref,
             state_ref, xpad_ref):
        n = pl.program_id(0)

        @pl.when(n == 0)
        def _():
            state_ref[...] = jnp.zeros_like(state_ref)

        _dn_conv(xpad_ref, c_ref, cur_ref, prev_ref, w_ref, n == 0)
        cq, ck, cv, z = _dn_batch_args(c_ref, z_ref)
        state = state_ref[...]
        st_ref[...] = state
        out, new_state = _dn_core(cq, ck, cv, z, l_ref[...], state, alog_ref[...], dtb_ref[...], og_ref[...])
        state_ref[...] = new_state
        for i, (b, h) in enumerate(pairs):
            out_ref[b, :, pl.ds(h * DN_HEAD_DIM, DN_HEAD_DIM)] = out[i].astype(out_ref.dtype)

    return pl.pallas_call(
        body, name="deltanet_fwd", grid=(nc,),
        out_shape=(jax.ShapeDtypeStruct((nb, s, DN_WIDTH), BF16),
                   jax.ShapeDtypeStruct((nc, gn, DN_HEAD_DIM, DN_HEAD_DIM), F32),
                   jax.ShapeDtypeStruct((nb, s, 3 * DN_WIDTH), F32)),
        in_specs=[chunk(3 * DN_WIDTH), pl.BlockSpec((nb, CONV_HALO, 3 * DN_WIDTH), prev), chunk(DN_WIDTH),
                  chunk(GATE_PAD)] + _dn_weight_specs(),
        out_specs=(chunk(DN_WIDTH), pl.BlockSpec((None, gn, DN_HEAD_DIM, DN_HEAD_DIM), lambda n: (n, 0, 0, 0)),
                   chunk(3 * DN_WIDTH)),
        scratch_shapes=[pltpu.VMEM((gn, DN_HEAD_DIM, DN_HEAD_DIM), F32),
                        pltpu.VMEM((nb, CONV_HALO + DN_CHUNK, 3 * DN_WIDTH), F32)],
        compiler_params=_params(1),
    )(qkv, qkv, zg, logits, conv_w, alog, dtb, og)


def _dn_bwd(qkv, conv_out, zg, logits, conv_w, alog, dtb, og, states, d_out):
    nb, s, _ = qkv.shape
    nc = s // DN_CHUNK
    rev = lambda n: nc - 1 - n
    pairs = _dn_pairs(nb)
    gn = len(pairs)

    def body(cur_ref, c_ref, z_ref, l_ref, w_ref, alog_ref, dtb_ref, og_ref, st_ref, do_ref,
             dqkv_ref, dz_ref, dl_ref, dw_ref, dalog_ref, ddtb_ref, dog_ref,
             dstate_ref, dcpad_ref):
        n = pl.program_id(0)

        @pl.when(n == 0)
        def _():
            dw_ref[...] = jnp.zeros_like(dw_ref)
            dalog_ref[...] = jnp.zeros_like(dalog_ref)
            ddtb_ref[...] = jnp.zeros_like(ddtb_ref)
            dog_ref[...] = jnp.zeros_like(dog_ref)
            dstate_ref[...] = jnp.zeros_like(dstate_ref)
            dcpad_ref[:, DN_CHUNK:, :] = jnp.zeros((nb, CONV_HALO, 3 * DN_WIDTH), F32)

        cq, ck, cv, z = _dn_batch_args(c_ref, z_ref)
        d_out_g = jnp.stack([do_ref[b, :, pl.ds(h * DN_HEAD_DIM, DN_HEAD_DIM)] for b, h in pairs])
        _, pull = jax.vjp(_dn_core, cq, ck, cv, z, l_ref[...], st_ref[...], alog_ref[...], dtb_ref[...], og_ref[...])
        dcq, dck, dcv, dz, dlog, dstate, dalog, ddtb, dog = pull((d_out_g, dstate_ref[...]))
        dstate_ref[...] = dstate
        dl_ref[...] = dlog
        dalog_ref[...] += dalog
        ddtb_ref[...] += ddtb
        dog_ref[...] += dog
        for i, (b, h) in enumerate(pairs):
            dcpad_ref[b, 0:DN_CHUNK, pl.ds(h * DN_HEAD_DIM, DN_HEAD_DIM)] = dcq[i]
            dcpad_ref[b, 0:DN_CHUNK, pl.ds(DN_WIDTH + h * DN_HEAD_DIM, DN_HEAD_DIM)] = dck[i]
            dcpad_ref[b, 0:DN_CHUNK, pl.ds(2 * DN_WIDTH + h * DN_HEAD_DIM, DN_HEAD_DIM)] = dcv[i]
            dz_ref[b, :, pl.ds(h * DN_HEAD_DIM, DN_HEAD_DIM)] = dz[i]
        for b in range(nb):
            xb = cur_ref[b]
            dx = None
            for j in range(CONV_K):
                shifted = dcpad_ref[b, pl.ds(CONV_K - 1 - j, DN_CHUNK), :]
                term = w_ref[j:j + 1, :] * shifted
                dx = term if dx is None else dx + term
                dw_ref[j:j + 1, :] += _rowsum(shifted * xb)
            dqkv_ref[b] = dx
            dcpad_ref[b, DN_CHUNK:, :] = dcpad_ref[b, 0:CONV_HALO, :]

    chunk = lambda w: pl.BlockSpec((nb, DN_CHUNK, w), lambda n: (0, rev(n), 0))
    return pl.pallas_call(
        body, name="deltanet_bwd", grid=(nc,),
        out_shape=(jax.ShapeDtypeStruct((nb, s, 3 * DN_WIDTH), F32), jax.ShapeDtypeStruct((nb, s, DN_WIDTH), F32),
                   jax.ShapeDtypeStruct((nb, s, GATE_PAD), F32), jax.ShapeDtypeStruct((CONV_K, 3 * DN_WIDTH), F32),
                   jax.ShapeDtypeStruct((1, GATE_PAD), F32), jax.ShapeDtypeStruct((1, GATE_PAD), F32),
                   jax.ShapeDtypeStruct((1, DN_HEAD_DIM), F32)),
        in_specs=[chunk(3 * DN_WIDTH), chunk(3 * DN_WIDTH), chunk(DN_WIDTH), chunk(GATE_PAD)] + _dn_weight_specs() + [
            pl.BlockSpec((None, gn, DN_HEAD_DIM, DN_HEAD_DIM), lambda n: (rev(n), 0, 0, 0)),
            chunk(DN_WIDTH)],
        out_specs=(chunk(3 * DN_WIDTH), chunk(DN_WIDTH), chunk(GATE_PAD), _whole((CONV_K, 3 * DN_WIDTH)),
                   _whole((1, GATE_PAD)), _whole((1, GATE_PAD)), _whole((1, DN_HEAD_DIM))),
        scratch_shapes=[pltpu.VMEM((gn, DN_HEAD_DIM, DN_HEAD_DIM), F32),
                        pltpu.VMEM((nb, DN_CHUNK + CONV_HALO, 3 * DN_WIDTH), F32)],
        compiler_params=_params(1),
    )(qkv, conv_out, zg, logits, conv_w, alog, dtb, og, states, d_out)


def _head(a_out, b_out, x2, p2, target, w_out, w_out_t, w_gate, w_gate_t, w_proj, ple_g, fin_g):
    t = x2.shape[0]
    tm = min(256, t)
    steps = t // tm

    def body(a_ref, b_ref, x_ref, p_ref, y_ref, wo_ref, wot_ref, wg_ref, wgt_ref, wp_ref, pg_ref, fg_ref,
             da_ref, db_ref, dh_ref, dwo_hbm, dwg_hbm, dwp_hbm, dpg_ref, dfg_ref, loss_ref,
             dwo_acc, dwg_acc, dwp_acc):
        i = pl.program_id(0)

        @pl.when(i == 0)
        def _():
            dwo_acc[...] = jnp.zeros_like(dwo_acc)
            dwg_acc[...] = jnp.zeros_like(dwg_acc)
            dwp_acc[...] = jnp.zeros_like(dwp_acc)
            dpg_ref[...] = jnp.zeros_like(dpg_ref)
            dfg_ref[...] = jnp.zeros_like(dfg_ref)
            loss_ref[...] = jnp.zeros_like(loss_ref)

        a = a_ref[...]
        bb = b_ref[...]
        pb = p_ref[...].astype(BF16)
        pg = pg_ref[...]
        fg = fg_ref[...]
        h1 = (x_ref[...] + jnp.dot(a, wo_ref[0:SGU_WIDTH, :], preferred_element_type=F32)
              + jnp.dot(bb, wo_ref[SGU_WIDTH:, :], preferred_element_type=F32))
        n1, r1 = _rms(h1)
        rn = (n1 * pg).astype(BF16)
        gate = jax.nn.sigmoid(jnp.dot(rn, wg_ref[...], preferred_element_type=F32))
        pp = jnp.dot(pb, wp_ref[...], preferred_element_type=F32)
        h2 = h1 + gate * pp
        n2, r2 = _rms(h2)
        err = n2 * fg - y_ref[...]
        loss_ref[...] += jnp.broadcast_to(_rowsum(jnp.sum(err * err, axis=-1, keepdims=True)), loss_ref.shape)

        dy = err * (1.0 / D_MODEL)
        dfg_ref[...] += _rowsum(dy * n2)
        dh2 = _rms_bwd(dy * fg, n2, r2)
        dpp = (dh2 * gate).astype(BF16)
        dgl = (dh2 * pp * gate * (1.0 - gate)).astype(BF16)
        dwp_acc[...] += lax.dot_general(pb, dpp, (((0,), (0,)), ((), ())), preferred_element_type=F32)
        dwg_acc[...] += lax.dot_general(rn, dgl, (((0,), (0,)), ((), ())), preferred_element_type=F32)
        drn = jnp.dot(dgl, wgt_ref[...], preferred_element_type=F32)
        dpg_ref[...] += _rowsum(drn * n1)
        dh1 = dh2 + _rms_bwd(drn * pg, n1, r1)
        dh_ref[...] = dh1
        dhb = dh1.astype(BF16)
        da_ref[...] = jnp.dot(dhb, wot_ref[:, 0:SGU_WIDTH], preferred_element_type=F32)
        db_ref[...] = jnp.dot(dhb, wot_ref[:, SGU_WIDTH:], preferred_element_type=F32)
        dwo_acc[0:SGU_WIDTH, :] += lax.dot_general(a, dhb, (((0,), (0,)), ((), ())), preferred_element_type=F32)
        dwo_acc[SGU_WIDTH:, :] += lax.dot_general(bb, dhb, (((0,), (0,)), ((), ())), preferred_element_type=F32)

        @pl.when(i == steps - 1)
        def _():
            pltpu.sync_copy(dwo_acc, dwo_hbm)
            pltpu.sync_copy(dwg_acc, dwg_hbm)
            for j in range(N_DEV):
                pltpu.sync_copy(dwp_acc.at[:, pl.ds(j * LANES, LANES)], dwp_hbm.at[j])

    tile = lambda w: pl.BlockSpec((tm, w), lambda i: (i, 0))
    return pl.pallas_call(
        body, name="head_fwd_bwd", grid=(steps,),
        out_shape=(jax.ShapeDtypeStruct((t, SGU_WIDTH), F32), jax.ShapeDtypeStruct((t, DN_WIDTH), F32),
                   jax.ShapeDtypeStruct((t, D_MODEL), F32), jax.ShapeDtypeStruct((D_MODEL, D_MODEL), F32),
                   jax.ShapeDtypeStruct((D_MODEL, D_MODEL), F32), jax.ShapeDtypeStruct((N_DEV, PLE_DIM, LANES), F32),
                   jax.ShapeDtypeStruct((1, D_MODEL), F32), jax.ShapeDtypeStruct((1, D_MODEL), F32),
                   jax.ShapeDtypeStruct((8, LANES), F32)),
        in_specs=[tile(SGU_WIDTH), tile(DN_WIDTH), tile(D_MODEL), tile(PLE_DIM), tile(D_MODEL),
                  VMEM_SPEC, VMEM_SPEC, VMEM_SPEC, VMEM_SPEC, VMEM_SPEC, _whole((1, D_MODEL)), _whole((1, D_MODEL))],
        out_specs=(tile(SGU_WIDTH), tile(DN_WIDTH), tile(D_MODEL), HBM_SPEC, HBM_SPEC, HBM_SPEC,
                   _whole((1, D_MODEL)), _whole((1, D_MODEL)), _whole((8, LANES))),
        scratch_shapes=[pltpu.VMEM((D_MODEL, D_MODEL), F32), pltpu.VMEM((D_MODEL, D_MODEL), F32),
                        pltpu.VMEM((PLE_DIM, D_MODEL), F32)],
        compiler_params=_params(1),
    )(a_out, b_out, x2, p2, target, w_out, w_out_t, w_gate, w_gate_t, w_proj, ple_g, fin_g)


def _inproj_bwd(x2, dh1, d_a, d_q, d_z, d_l, norm_g, wat, wqt, wzt, wgt):
    t = x2.shape[0]
    tm = min(256, t)
    steps = t // tm

    widths = (d_a.shape[1], d_q.shape[1], d_z.shape[1], d_l.shape[1])
    starts = (0, widths[0], widths[0] + widths[1], widths[0] + widths[1] + widths[2])

    def body(x_ref, dh_ref, da_ref, dq_ref, dz_ref, dl_ref, g_ref, wat_ref, wqt_ref, wzt_ref, wgt_ref,
             dx_ref, dw_hbm, dg_ref, dw_acc, stage_ref):
        i = pl.program_id(0)

        @pl.when(i == 0)
        def _():
            dw_acc[...] = jnp.zeros_like(dw_acc)
            dg_ref[...] = jnp.zeros_like(dg_ref)

        g = g_ref[...]
        n, r = _rms(x_ref[...])
        xn = (n * g).astype(BF16)
        dxn = None
        for d_ref, wt_ref, col0 in zip((da_ref, dq_ref, dz_ref, dl_ref), (wat_ref, wqt_ref, wzt_ref, wgt_ref), starts):
            width = d_ref.shape[1]
            for c0 in range(0, width, 512):
                c1 = min(c0 + 512, width)
                d = d_ref[:, c0:c1].astype(BF16)
                term = jnp.dot(d, wt_ref[c0:c1, :], preferred_element_type=F32)
                dxn = term if dxn is None else dxn + term
                dw_acc[:, col0 + c0:col0 + c1] += lax.dot_general(xn, d, (((0,), (0,)), ((), ())),
                                                                  preferred_element_type=F32)
        dg_ref[...] += _rowsum(dxn * n)
        dx_ref[...] = dh_ref[...] + _rms_bwd(dxn * g, n, r)

        @pl.when(i == steps - 1)
        def _():
            for j in range(N_DEV):
                stage_ref[...] = dw_acc[:, j * IN_SHARD:(j + 1) * IN_SHARD]
                pltpu.sync_copy(stage_ref, dw_hbm.at[j])

    tile = lambda w: pl.BlockSpec((tm, w), lambda i: (i, 0))
    return pl.pallas_call(
        body, name="inproj_bwd", grid=(steps,),
        out_shape=(jax.ShapeDtypeStruct((t, D_MODEL), F32), jax.ShapeDtypeStruct((N_DEV, D_MODEL, IN_SHARD), F32),
                   jax.ShapeDtypeStruct((1, D_MODEL), F32)),
        in_specs=[tile(D_MODEL), tile(D_MODEL)] + [tile(w) for w in widths] + [_whole((1, D_MODEL))] + [VMEM_SPEC] * 4,
        out_specs=(tile(D_MODEL), HBM_SPEC, _whole((1, D_MODEL))),
        scratch_shapes=[pltpu.VMEM((D_MODEL, sum(widths)), F32), pltpu.VMEM((D_MODEL, IN_SHARD), F32)],
        compiler_params=_params(1),
    )(x2, dh1, d_a, d_q, d_z, d_l, norm_g, wat, wqt, wzt, wgt)


def _reduce_adamw(recv, w, m, v, name, row_block=None):
    n, rows, cols = recv.shape
    rb = row_block or rows

    def body(r_ref, w_ref, m_ref, v_ref, g_ref, d_ref, nm_ref, nv_ref):
        g = r_ref[0].astype(F32)
        for i in range(1, n):
            g = g + r_ref[i].astype(F32)
        m_new = ADAM_B1 * m_ref[...] + (1.0 - ADAM_B1) * g
        v_new = ADAM_B2 * v_ref[...] + (1.0 - ADAM_B2) * jnp.square(g)
        m_hat = m_new / (1.0 - ADAM_B1 ** ADAM_STEP)
        v_hat = v_new / (1.0 - ADAM_B2 ** ADAM_STEP)
        g_ref[...] = g
        d_ref[...] = -ADAM_LR * (m_hat / (jnp.sqrt(v_hat) + ADAM_EPS) + ADAM_WD * w_ref[...])
        nm_ref[...] = m_new
        nv_ref[...] = v_new

    blk = pl.BlockSpec((rb, cols), lambda i: (i, 0))
    return pl.pallas_call(
        body, name=name, grid=(rows // rb,),
        out_shape=tuple(jax.ShapeDtypeStruct((rows, cols), F32) for _ in range(4)),
        in_specs=[pl.BlockSpec((n, rb, cols), lambda i: (0, i, 0)), blk, blk, blk],
        out_specs=(blk, blk, blk, blk),
        compiler_params=_params(1),
    )(recv, w, m, v)


def _pack_rows(pieces, rows, dtype):
    flat = jnp.concatenate([jnp.ravel(p).astype(dtype) for p in pieces])
    flat = jnp.pad(flat, (0, rows * LANES - flat.shape[0]))
    return flat.reshape(rows, LANES)


def _unpack(pack, layout):
    flat = pack.reshape(-1)
    out, off = {}, 0
    for name, shape in layout:
        n = _size(shape)
        out[name] = flat[off:off + n].reshape(shape)
        off += n
    return out


def kernel(x, p, norm_g, w_in, sgu_ln_g, sgu_ln_b, sgu_w_s, sgu_b_s, dn_conv_w, dn_a_log, dn_dt_bias, dn_o_norm_g, w_out, ple_norm_g, ple_gate_w, ple_proj_w, final_norm_g, loss_target, m_norm_g, m_w_in, m_sgu_ln_g, m_sgu_ln_b, m_sgu_w_s, m_sgu_b_s, m_dn_conv_w, m_dn_a_log, m_dn_dt_bias, m_dn_o_norm_g, m_w_out, m_ple_norm_g, m_ple_gate_w, m_ple_proj_w, m_final_norm_g, v_norm_g, v_w_in, v_sgu_ln_g, v_sgu_ln_b, v_sgu_w_s, v_sgu_b_s, v_dn_conv_w, v_dn_a_log, v_dn_dt_bias, v_dn_o_norm_g, v_w_out, v_ple_norm_g, v_ple_gate_w, v_ple_proj_w, v_final_norm_g):
    weights = dict(norm_g=norm_g, w_in=w_in, sgu_ln_g=sgu_ln_g, sgu_ln_b=sgu_ln_b, sgu_w_s=sgu_w_s, sgu_b_s=sgu_b_s,
                   dn_conv_w=dn_conv_w, dn_a_log=dn_a_log, dn_dt_bias=dn_dt_bias, dn_o_norm_g=dn_o_norm_g, w_out=w_out,
                   ple_norm_g=ple_norm_g, ple_gate_w=ple_gate_w, ple_proj_w=ple_proj_w, final_norm_g=final_norm_g)
    mom1 = dict(norm_g=m_norm_g, w_in=m_w_in, sgu_ln_g=m_sgu_ln_g, sgu_ln_b=m_sgu_ln_b, sgu_w_s=m_sgu_w_s,
                sgu_b_s=m_sgu_b_s, dn_conv_w=m_dn_conv_w, dn_a_log=m_dn_a_log, dn_dt_bias=m_dn_dt_bias,
                dn_o_norm_g=m_dn_o_norm_g, w_out=m_w_out, ple_norm_g=m_ple_norm_g, ple_gate_w=m_ple_gate_w,
                ple_proj_w=m_ple_proj_w, final_norm_g=m_final_norm_g)
    mom2 = dict(norm_g=v_norm_g, w_in=v_w_in, sgu_ln_g=v_sgu_ln_g, sgu_ln_b=v_sgu_ln_b, sgu_w_s=v_sgu_w_s,
                sgu_b_s=v_sgu_b_s, dn_conv_w=v_dn_conv_w, dn_a_log=v_dn_a_log, dn_dt_bias=v_dn_dt_bias,
                dn_o_norm_g=v_dn_o_norm_g, w_out=v_w_out, ple_norm_g=v_ple_norm_g, ple_gate_w=v_ple_gate_w,
                ple_proj_w=v_ple_proj_w, final_norm_g=v_final_norm_g)
    nb, s, _ = x.shape
    t = nb * s

    full = dict(zip(("w_in", "w_out", "ple_gate_w", "ple_proj_w", "dn_conv_w"), _all_gather(
        [w_in[0].astype(BF16), w_out[0].astype(BF16), ple_gate_w[0].astype(BF16), ple_proj_w[0].astype(BF16),
         dn_conv_w[0]])))
    w_in_full = jnp.moveaxis(full["w_in"], 0, 1).reshape(D_MODEL, IN_COLS)
    wa = w_in_full[:, :3 * SGU_WIDTH]
    wq = w_in_full[:, 3 * SGU_WIDTH:3 * SGU_WIDTH + 3 * DN_WIDTH]
    wz = w_in_full[:, 3 * SGU_WIDTH + 3 * DN_WIDTH:3 * SGU_WIDTH + 4 * DN_WIDTH]
    wg = jnp.pad(w_in_full[:, 3 * SGU_WIDTH + 4 * DN_WIDTH:], ((0, 0), (0, GATE_PAD - 2 * DN_HEADS)))
    w_out_full = full["w_out"].reshape(D_MODEL, D_MODEL)
    w_gate_full = full["ple_gate_w"].reshape(D_MODEL, D_MODEL)
    w_proj_full = jnp.moveaxis(full["ple_proj_w"], 0, 1).reshape(PLE_DIM, D_MODEL)
    conv_full = jnp.moveaxis(full["dn_conv_w"], 0, 1).reshape(CONV_K, 3 * DN_WIDTH)

    pad_row = lambda a: jnp.pad(a.reshape(1, -1), ((0, 0), (DN_HEADS, GATE_PAD - DN_HEADS - a.size)))
    alog, dtb = pad_row(dn_a_log), pad_row(dn_dt_bias)
    og = dn_o_norm_g.reshape(1, DN_HEAD_DIM)
    ws = sgu_w_s.reshape(SGU_GROUPS, SGU_CHUNK, SGU_CHUNK)
    b_t = sgu_b_s.reshape(SGU_GROUPS, SGU_CHUNK).T
    fin_g = final_norm_g.reshape(1, D_MODEL)

    x2 = x.reshape(t, D_MODEL)
    a_uvz, b_qkv, b_z, b_l = _inproj_fwd(x2, norm_g, wa, wq, wz, wg)
    a_out = _sgu_fwd(a_uvz, sgu_ln_g, sgu_ln_b, ws, b_t)
    qkv3 = b_qkv.reshape(nb, s, 3 * DN_WIDTH)
    z3 = b_z.reshape(nb, s, DN_WIDTH)
    l3 = b_l.reshape(nb, s, GATE_PAD)
    b_out, states, conv_out = _dn_fwd(qkv3, z3, l3, conv_full, alog, dtb, og)

    d_a, d_b, dh1, g_w_out, g_gate, g_proj, g_ple_g, g_fin_g, loss_tile = _head(
        a_out, b_out.reshape(t, DN_WIDTH), x2, p.reshape(t, PLE_DIM), loss_target.reshape(t, D_MODEL),
        w_out_full, w_out_full.T, w_gate_full, w_gate_full.T, w_proj_full, ple_norm_g, fin_g)
    d_qkv, d_z, d_l, g_conv, g_alog, g_dtb, g_og = _dn_bwd(
        qkv3, conv_out, z3, l3, conv_full, alog, dtb, og, states, d_b.reshape(nb, s, DN_WIDTH))
    d_uvz, g_ln_g, g_ln_b, g_ws, g_bt = _sgu_bwd(a_uvz, d_a, sgu_ln_g, sgu_ln_b, ws, b_t)
    grad_x, g_w_in, g_norm = _inproj_bwd(
        x2, dh1, d_uvz, d_qkv.reshape(t, 3 * DN_WIDTH), d_z.reshape(t, DN_WIDTH), d_l.reshape(t, GATE_PAD),
        norm_g, wa.T, wq.T, wz.T, wg.T)

    by_device = [g_w_in, g_w_out.reshape(N_DEV, 128, D_MODEL), g_gate.reshape(N_DEV, 128, D_MODEL), g_proj]
    small = _pack_rows([g_conv, g_norm, g_ln_g, g_ln_b, g_ws, g_bt.T, g_alog[:, DN_HEADS:2 * DN_HEADS], g_dtb[:, DN_HEADS:2 * DN_HEADS], g_og,
                        g_ple_g, g_fin_g, (0.5 / D_MODEL) * loss_tile[0:1, 0:1]], SMALL_ROWS, F32)
    *from_sibling, small_sibling = _sibling_exchange(by_device, small)
    core = lax.axis_index("c").astype(jnp.int32).reshape(1)
    *chip_sums, small_sum = _pair_sum(core, by_device, from_sibling, small, small_sibling)
    *received, small_received = _chip_exchange(chip_sums, small_sum)

    results = {}
    for name, recv, rb in zip(("w_in", "w_out", "ple_gate_w", "ple_proj_w"), received, (128, None, None, None)):
        shape = weights[name].shape
        outs = _reduce_adamw(recv, weights[name][0], mom1[name][0], mom2[name][0], "adamw_" + name, rb)
        results[name] = [a.reshape(shape) for a in outs]
    names = [name for name, _ in REPLICATED]
    zeros = jnp.zeros((CONV_K, 3 * DN_WIDTH), F32)
    small_outs = _reduce_adamw(small_received, *[_pack_rows([zeros] + [src[k] for k in names], SMALL_ROWS, F32)
                                                 for src in (weights, mom1, mom2)], "adamw_replicated")
    layout = (SMALL_LAYOUT[0],) + tuple((k, weights[k].shape) for k in names) + (SMALL_LAYOUT[-1],)
    unpacked = [_unpack(a, layout) for a in small_outs]
    for k in names:
        results[k] = [u[k] for u in unpacked]
    loss = unpacked[0]["loss"][0]
    me = 4 * lax.axis_index("x") + 2 * lax.axis_index("y") + lax.axis_index("c")
    conv_mine = lax.dynamic_slice(unpacked[0]["conv"], (0, me * 192), (CONV_K, 192))
    outs = _reduce_adamw(conv_mine[None], dn_conv_w[0], m_dn_conv_w[0], v_dn_conv_w[0], "adamw_dn_conv_w")
    results["dn_conv_w"] = [a.reshape(dn_conv_w.shape) for a in outs]

    return (loss, grad_x.reshape(nb, s, D_MODEL), *[results[k][0] for k in WEIGHT_ORDER],
            *[results[k][1] for k in WEIGHT_ORDER], *[results[k][2] for k in WEIGHT_ORDER],
            *[results[k][3] for k in WEIGHT_ORDER])
```

```python
import jax
import jax.numpy as jnp
from jax import lax
from jax.experimental import pallas as pl
from jax.experimental.pallas import tpu as pltpu

F32 = jnp.float32
BF16 = jnp.bfloat16

N_DEV = 8
D_MODEL = 1024
SGU_WIDTH = 512
SGU_GROUPS = 4
SGU_CHUNK = 128
DN_WIDTH = 512
DN_HEADS = 4
DN_HEAD_DIM = 128
DN_CHUNK = 128
CONV_K = 4
CONV_HALO = 8
PLE_DIM = 256
EPS = 1e-6
IN_COLS = 3592
IN_SHARD = IN_COLS // N_DEV
GATE_PAD = 128

ADAM_LR = 0.001
ADAM_B1 = 0.9
ADAM_B2 = 0.999
ADAM_EPS = 1e-08
ADAM_WD = 0.01
ADAM_STEP = 10

LANES = 128
VMEM_LIMIT = 56 * 1024 * 1024
MESH = pl.DeviceIdType.MESH

REPLICATED = (("norm_g", (1, D_MODEL)), ("sgu_ln_g", (1, SGU_WIDTH)), ("sgu_ln_b", (1, SGU_WIDTH)),
              ("sgu_w_s", (1, SGU_GROUPS, SGU_CHUNK, SGU_CHUNK)), ("sgu_b_s", (1, SGU_GROUPS, SGU_CHUNK)),
              ("dn_a_log", (1, DN_HEADS)), ("dn_dt_bias", (1, DN_HEADS)), ("dn_o_norm_g", (1, DN_HEAD_DIM)),
              ("ple_norm_g", (1, D_MODEL)), ("final_norm_g", (D_MODEL,)))
WEIGHT_ORDER = ("norm_g", "w_in", "sgu_ln_g", "sgu_ln_b", "sgu_w_s", "sgu_b_s", "dn_conv_w", "dn_a_log",
                "dn_dt_bias", "dn_o_norm_g", "w_out", "ple_norm_g", "ple_gate_w", "ple_proj_w", "final_norm_g")


def _size(shape):
    n = 1
    for s in shape:
        n *= s
    return n


SMALL_LAYOUT = (("conv", (CONV_K, 3 * DN_WIDTH)),) + REPLICATED + (("loss", (1,)),)
N_SMALL = sum(_size(s) for _, s in SMALL_LAYOUT)
SMALL_ROWS = -(-N_SMALL // (8 * LANES)) * 8


def _bdot(a, b):
    return jnp.dot(a.astype(BF16), b.astype(BF16), preferred_element_type=F32)


def _sigmoid(x):
    return pl.reciprocal(1.0 + jnp.exp(-x), approx=True)


@jax.custom_vjp
def _silu(x):
    return x * _sigmoid(x)


def _silu_fwd(x):
    s = _sigmoid(x)
    return x * s, (x, s)


def _silu_bwd(res, ct):
    x, s = res
    return (ct * (s * (1.0 + x * (1.0 - s))),)


_silu.defvjp(_silu_fwd, _silu_bwd)


def _gelu(x):
    return 0.5 * x * (1.0 + lax.erf(x * (0.5 ** 0.5)))


def _softplus(x):
    return jnp.maximum(x, 0.0) + jnp.log1p(jnp.exp(-jnp.abs(x)))


def _l2n(x):
    return x * lax.rsqrt(jnp.sum(x * x, axis=-1, keepdims=True) + EPS)


def _rms(x):
    r = lax.rsqrt(jnp.mean(x * x, axis=-1, keepdims=True) + EPS)
    return x * r, r


def _rms_bwd(dn, n, r):
    return r * (dn - n * jnp.mean(dn * n, axis=-1, keepdims=True))


def _onehot_row(idx, width):
    return (lax.broadcasted_iota(jnp.int32, (1, width), 1) == idx).astype(F32)


def _rowsum(x):
    return jnp.sum(x, axis=0, keepdims=True)


def _iota2(n):
    return lax.broadcasted_iota(jnp.int32, (n, n), 0), lax.broadcasted_iota(jnp.int32, (n, n), 1)


def _bmm(a, b):
    return lax.dot_general(a.astype(BF16), b.astype(BF16), (((2,), (1,)), ((0,), (0,))), preferred_element_type=F32)


def _bmm_nt(a, b):
    return lax.dot_general(a.astype(BF16), b.astype(BF16), (((2,), (2,)), ((0,), (0,))), preferred_element_type=F32)


def _bmm_tn(a, b):
    return lax.dot_general(a.astype(BF16), b.astype(BF16), (((1,), (1,)), ((0,), (0,))), preferred_element_type=F32)


def _tri_inv_impl(a):
    n = a.shape[-1]
    r, c = _iota2(n)
    x = r ^ c
    eye = (r == c).astype(F32)
    ad = jnp.where(x < 16, a, 0.0)
    p2 = _bmm(ad, ad)
    e = p2 - ad - _bmm(ad, p2)
    p4 = _bmm(p2, p2)
    e = e + p4 + _bmm(e, p4)
    p8 = _bmm(p4, p4)
    e = e + p8 + _bmm(e, p8)
    size = 16
    while size < n:
        m = jnp.where(jnp.logical_and(x < 2 * size, x >= size), a, 0.0)
        f = m + _bmm(m, e)
        e = e - f - _bmm(e, f)
        size *= 2
    return e + eye


@jax.custom_vjp
def _tri_inv(a):
    return _tri_inv_impl(a)


def _tri_inv_fwd(a):
    t = _tri_inv_impl(a)
    return t, t


def _tri_inv_bwd(t, dt):
    return (-_bmm_tn(t, _bmm_nt(dt, t)),)


_tri_inv.defvjp(_tri_inv_fwd, _tri_inv_bwd)


def _sgu_core(u, v, z, lg, lb, ws, bcol):
    n = ws.shape[0]
    r, c = _iota2(n)
    wm = jnp.where(r >= c, ws, 0.0)
    gu = _gelu(u)
    gv = _gelu(v)
    xc = gv - jnp.mean(gv, axis=-1, keepdims=True)
    ln = xc * lax.rsqrt(jnp.mean(xc * xc, axis=-1, keepdims=True) + EPS) * lg + lb
    s = _bdot(wm, ln) + bcol
    return gu * s * _silu(z)


def _lanes_of(x):
    return jnp.concatenate([x[i] for i in range(x.shape[0])], axis=1)


def _batch_of(x, width):
    return jnp.concatenate([x[None, :, i * width:(i + 1) * width] for i in range(x.shape[1] // width)], axis=0)


def _mask_dot(mask, x):
    hi = x.astype(BF16)
    lo = (x - hi.astype(F32)).astype(BF16)
    m = mask.astype(BF16)
    return jnp.dot(m, hi, preferred_element_type=F32) + jnp.dot(m, lo, preferred_element_type=F32)


def _tri_mask(n, upper):
    r, c = _iota2(n)
    return (r <= c) if upper else (r >= c)


@jax.custom_vjp
def _cumsum_rows(x):
    return _mask_dot(_tri_mask(x.shape[0], False), x)


def _cumsum_rows_fwd(x):
    return _cumsum_rows(x), None


def _cumsum_rows_bwd(_, ct):
    return (_bdot(_tri_mask(ct.shape[0], True), ct),)


_cumsum_rows.defvjp(_cumsum_rows_fwd, _cumsum_rows_bwd)


@jax.custom_vjp
def _colsum_all_rows(x):
    return _mask_dot(jnp.ones((x.shape[0], x.shape[0]), jnp.bool_), x)


def _colsum_all_rows_fwd(x):
    return _colsum_all_rows(x), None


def _colsum_all_rows_bwd(_, ct):
    return (_bdot(jnp.ones((ct.shape[0], ct.shape[0]), F32), ct),)


_colsum_all_rows.defvjp(_colsum_all_rows_fwd, _colsum_all_rows_bwd)


def _dn_core(cq, ck, cv, z, logits, state, alog, dtb, og):
    gn, cn, dh = cq.shape
    heads = gn // logits.shape[0]
    q = _l2n(_silu(cq)) * (dh ** -0.5)
    k = _l2n(_silu(ck))
    v = _silu(cv)
    beta_lanes = jax.nn.sigmoid(logits)
    g_lanes = -jnp.exp(alog) * _softplus(logits + dtb)
    column = lambda rows, lane: jnp.sum(rows * _onehot_row(lane, rows.shape[-1]), axis=-1, keepdims=True)[None]
    beta = jnp.concatenate([column(beta_lanes[i // heads], i % heads) for i in range(gn)], axis=0)
    g = jnp.concatenate([column(g_lanes[i // heads], heads + i % heads) for i in range(gn)], axis=0)
    r, c = _iota2(cn)
    tril = r >= c
    rw = lax.broadcasted_iota(jnp.int32, (cn, dh), 0)
    cw = lax.broadcasted_iota(jnp.int32, (cn, dh), 1)
    upper_wide = (rw <= cw).astype(F32)
    g_wide = jnp.broadcast_to(g, (gn, cn, dh))
    gc_wide = _batch_of(_cumsum_rows(_lanes_of(g_wide)), dh)
    gc_cols = _batch_of(_colsum_all_rows(_lanes_of(g_wide * upper_wide)), dh)[:, :, :cn]
    decay = jnp.exp(jnp.where(tril, gc_wide[:, :, :cn] - gc_cols, -1e30))
    kb = k * beta
    kk = _bmm_nt(kb, k) * decay
    t = _tri_inv(jnp.where(r > c, kk, 0.0))
    eg = jnp.exp(gc_wide)
    sol = _bmm(t, jnp.concatenate([v * beta, kb * eg], axis=-1))
    u_val, w_dec = sol[:, :, :dh], sol[:, :, dh:]
    qk = _bmm_nt(q, k) * decay
    g_last = jnp.sum(g_wide, axis=1, keepdims=True)
    k_dec = k * jnp.exp(g_last - gc_wide)
    ws = _bmm(jnp.concatenate([w_dec, q * eg], axis=1), state)
    v_new = u_val - ws[:, :cn]
    o = ws[:, cn:] + _bmm(qk, v_new)
    new_state = state * jnp.exp(g_last) + _bmm_tn(k_dec, v_new)
    on, _ = _rms(o)
    return on * og * _silu(z), new_state


N_CHIPS = 4
HBM_SPEC = pl.BlockSpec(memory_space=pl.ANY)


def _place():
    return lax.axis_index("x"), lax.axis_index("y"), lax.axis_index("c")


def _other_chip(k):
    x, y, _ = _place()
    px = 1 - x if k & 2 else x
    py = 1 - y if k & 1 else y
    return px, py, 2 * px + py


def _remote(src, dst, send_sem, recv_sem, device):
    return pltpu.make_async_remote_copy(src_ref=src, dst_ref=dst, send_sem=send_sem, recv_sem=recv_sem,
                                        device_id=device, device_id_type=MESH)


def _all_gather(shards):
    n = len(shards)

    def body(*refs):
        srcs, outs = refs[:n], refs[n:2 * n]
        send_sems, recv_sems, local_sems = refs[2 * n:]
        x, y, c = _place()
        me = 4 * x + 2 * y + c
        sibling = (x, y, 1 - c)
        local = [pltpu.make_async_copy(srcs[a], outs[a].at[me], local_sems.at[a]) for a in range(n)]
        for cp in local:
            cp.start()
        sends = []
        for a in range(n):
            sends.append(_remote(srcs[a], outs[a].at[me], send_sems.at[a, 0], recv_sems.at[a, 0], sibling))
        for k in range(1, N_CHIPS):
            px, py, _ = _other_chip(k)
            for a in range(n):
                sends.append(_remote(srcs[a], outs[a].at[me], send_sems.at[a, k], recv_sems.at[a, k], (px, py, c)))
        for cp in sends:
            cp.start()
        passed = []
        for k in range(1, N_CHIPS):
            px, py, _ = _other_chip(k)
            blk = 4 * px + 2 * py + c
            for a in range(n):
                _remote(srcs[a], outs[a].at[blk], send_sems.at[a, k], recv_sems.at[a, k], (px, py, c)).wait_recv()
            for a in range(n):
                cp = _remote(outs[a].at[blk], outs[a].at[blk], send_sems.at[a, 3 + k], recv_sems.at[a, 3 + k], sibling)
                cp.start()
                passed.append(cp)
        for a in range(n):
            _remote(srcs[a], outs[a].at[me + 1 - 2 * c], send_sems.at[a, 0], recv_sems.at[a, 0], sibling).wait_recv()
        for k in range(1, N_CHIPS):
            px, py, _ = _other_chip(k)
            blk = 4 * px + 2 * py + 1 - c
            for a in range(n):
                _remote(srcs[a], outs[a].at[blk], send_sems.at[a, 3 + k], recv_sems.at[a, 3 + k], sibling).wait_recv()
        for cp in sends + passed:
            cp.wait_send()
        for cp in local:
            cp.wait()

    return pl.pallas_call(
        body, name="all_gather_weights",
        out_shape=tuple(jax.ShapeDtypeStruct((N_DEV,) + a.shape, a.dtype) for a in shards),
        in_specs=[HBM_SPEC] * n, out_specs=(HBM_SPEC,) * n,
        scratch_shapes=[pltpu.SemaphoreType.DMA((n, N_DEV - 1)), pltpu.SemaphoreType.DMA((n, N_DEV - 1)),
                        pltpu.SemaphoreType.DMA((n,))],
    )(*shards)


def _sibling_exchange(by_device, small):
    n = len(by_device)

    def body(*refs):
        srcs, small_src = refs[:n], refs[n]
        outs, small_out = refs[n + 1:2 * n + 1], refs[2 * n + 1]
        send_sems, recv_sems = refs[2 * n + 2:]
        x, y, c = _place()
        sibling = (x, y, 1 - c)
        copies = [_remote(small_src, small_out, send_sems.at[n, 0], recv_sems.at[n, 0], sibling)]
        for a in range(n):
            for q in range(N_CHIPS):
                copies.append(_remote(srcs[a].at[2 * q + 1 - c], outs[a].at[q], send_sems.at[a, q], recv_sems.at[a, q],
                                      sibling))
        for cp in copies:
            cp.start()
        for cp in copies:
            cp.wait_recv()
        for cp in copies:
            cp.wait_send()

    return pl.pallas_call(
        body, name="grad_sibling_exchange",
        out_shape=tuple(jax.ShapeDtypeStruct((N_CHIPS,) + a.shape[1:], a.dtype) for a in by_device)
        + (jax.ShapeDtypeStruct(small.shape, small.dtype),),
        in_specs=[HBM_SPEC] * (n + 1), out_specs=(HBM_SPEC,) * (n + 1),
        scratch_shapes=[pltpu.SemaphoreType.DMA((n + 1, N_CHIPS)), pltpu.SemaphoreType.DMA((n + 1, N_CHIPS))],
    )(*by_device, small)


def _chip_exchange(chip_sums, small):
    n = len(chip_sums)

    def body(*refs):
        srcs, small_src = refs[:n], refs[n]
        outs, small_out = refs[n + 1:2 * n + 1], refs[2 * n + 1]
        send_sems, recv_sems, local_sems = refs[2 * n + 2:]
        x, y, c = _place()
        mine = 2 * x + y
        local = [pltpu.make_async_copy(srcs[a].at[mine], outs[a].at[mine], local_sems.at[a]) for a in range(n)]
        local.append(pltpu.make_async_copy(small_src, small_out.at[mine], local_sems.at[n]))
        for cp in local:
            cp.start()
        sends = []
        for k in range(1, N_CHIPS):
            px, py, chip = _other_chip(k)
            for a in range(n):
                sends.append(_remote(srcs[a].at[chip], outs[a].at[mine], send_sems.at[a, k - 1], recv_sems.at[a, k - 1],
                                     (px, py, c)))
            sends.append(_remote(small_src, small_out.at[mine], send_sems.at[n, k - 1], recv_sems.at[n, k - 1], (px, py, c)))
        for cp in sends:
            cp.start()
        for k in range(1, N_CHIPS):
            px, py, chip = _other_chip(k)
            for a in range(n):
                _remote(srcs[a].at[chip], outs[a].at[chip], send_sems.at[a, k - 1], recv_sems.at[a, k - 1],
                        (px, py, c)).wait_recv()
            _remote(small_src, small_out.at[chip], send_sems.at[n, k - 1], recv_sems.at[n, k - 1], (px, py, c)).wait_recv()
        for cp in sends:
            cp.wait_send()
        for cp in local:
            cp.wait()

    return pl.pallas_call(
        body, name="grad_chip_exchange",
        out_shape=tuple(jax.ShapeDtypeStruct(a.shape, a.dtype) for a in chip_sums)
        + (jax.ShapeDtypeStruct((N_CHIPS,) + small.shape, small.dtype),),
        in_specs=[HBM_SPEC] * (n + 1), out_specs=(HBM_SPEC,) * (n + 1),
        scratch_shapes=[pltpu.SemaphoreType.DMA((n + 1, N_CHIPS - 1)), pltpu.SemaphoreType.DMA((n + 1, N_CHIPS - 1)),
                        pltpu.SemaphoreType.DMA((n + 1,))],
    )(*chip_sums, small)


def _pair_sum(core, by_device, from_sibling, small, small_from_sibling):
    n = len(by_device)

    def body(core_ref, *refs):
        own, sib = refs[:n], refs[n:2 * n]
        small_own, small_sib = refs[2 * n], refs[2 * n + 1]
        outs, small_out = refs[2 * n + 2:3 * n + 2], refs[3 * n + 2]
        for a in range(n):
            outs[a][...] = (own[a][...] + sib[a][...]).astype(outs[a].dtype)
        small_out[...] = small_own[...] + small_sib[...]

    def block(a):
        return (None,) + a.shape[1:], (0,) * (a.ndim - 1)

    own_specs = [pl.BlockSpec(block(a)[0], lambda q, core_ref, z=block(a)[1]: (2 * q + core_ref[0],) + z) for a in by_device]
    sib_specs = [pl.BlockSpec(block(a)[0], lambda q, core_ref, z=block(a)[1]: (q,) + z) for a in by_device]
    small_spec = pl.BlockSpec(small.shape, lambda q, core_ref: (0,) * small.ndim)
    return pl.pallas_call(
        body, name="grad_pair_sum",
        grid_spec=pltpu.PrefetchScalarGridSpec(
            num_scalar_prefetch=1, grid=(N_CHIPS,),
            in_specs=own_specs + sib_specs + [small_spec, small_spec],
            out_specs=tuple(sib_specs) + (small_spec,)),
        out_shape=tuple(jax.ShapeDtypeStruct(a.shape, BF16) for a in from_sibling)
        + (jax.ShapeDtypeStruct(small.shape, F32),),
        compiler_params=_params(1),
    )(core, *by_device, *from_sibling, small, small_from_sibling)


def _params(n_axes):
    return pltpu.CompilerParams(dimension_semantics=("arbitrary",) * n_axes, vmem_limit_bytes=VMEM_LIMIT)


def _whole(shape):
    return pl.BlockSpec(shape, lambda *_: (0,) * len(shape))


VMEM_SPEC = pl.BlockSpec(memory_space=pltpu.VMEM)


def _inproj_fwd(x2, norm_g, wa, wq, wz, wg):
    t = x2.shape[0]
    tm = min(512, t)

    def body(x_ref, g_ref, wa_ref, wq_ref, wz_ref, wg_ref, a_ref, q_ref, z_ref, l_ref):
        n, _ = _rms(x_ref[...])
        xn = (n * g_ref[...]).astype(BF16)
        for w_ref, o_ref in ((wa_ref, a_ref), (wq_ref, q_ref), (wz_ref, z_ref), (wg_ref, l_ref)):
            width = w_ref.shape[1]
            for c0 in range(0, width, 512):
                c1 = min(c0 + 512, width)
                o_ref[:, c0:c1] = jnp.dot(xn, w_ref[:, c0:c1], preferred_element_type=F32)

    widths = (wa.shape[1], wq.shape[1], wz.shape[1], wg.shape[1])
    return pl.pallas_call(
        body, name="inproj_fwd", grid=(t // tm,),
        out_shape=tuple(jax.ShapeDtypeStruct((t, w), F32) for w in widths),
        in_specs=[pl.BlockSpec((tm, D_MODEL), lambda i: (i, 0)), _whole((1, D_MODEL)),
                  VMEM_SPEC, VMEM_SPEC, VMEM_SPEC, VMEM_SPEC],
        out_specs=tuple(pl.BlockSpec((tm, w), lambda i: (i, 0)) for w in widths),
        compiler_params=_params(1),
    )(x2, norm_g, wa, wq, wz, wg)


def _sgu_pieces(uvz_ref, lg_ref, lb_ref, ws_ref, bt_ref, row0, grp):
    rows = pl.ds(row0, SGU_CHUNK)
    lanes = pl.ds(grp * 128, 128)
    u = uvz_ref[rows, pl.ds(grp * 128, 128)]
    v = uvz_ref[rows, pl.ds(SGU_WIDTH + grp * 128, 128)]
    z = uvz_ref[rows, pl.ds(2 * SGU_WIDTH + grp * 128, 128)]
    bcol = jnp.sum(bt_ref[...] * _onehot_row(grp, SGU_GROUPS), axis=-1, keepdims=True)
    return u, v, z, lg_ref[:, lanes], lb_ref[:, lanes], ws_ref[grp], bcol


def _sgu_fwd(a_uvz, ln_g, ln_b, w_s, b_t):
    t = a_uvz.shape[0]
    tm = min(512, t)

    def body(uvz_ref, lg_ref, lb_ref, ws_ref, bt_ref, out_ref):
        for row0 in range(0, tm, SGU_CHUNK):
            for grp in range(SGU_GROUPS):
                args = _sgu_pieces(uvz_ref, lg_ref, lb_ref, ws_ref, bt_ref, row0, grp)
                out_ref[pl.ds(row0, SGU_CHUNK), pl.ds(grp * 128, 128)] = _sgu_core(*args).astype(out_ref.dtype)

    return pl.pallas_call(
        body, name="sgu_fwd", grid=(t // tm,),
        out_shape=jax.ShapeDtypeStruct((t, SGU_WIDTH), BF16),
        in_specs=[pl.BlockSpec((tm, 3 * SGU_WIDTH), lambda i: (i, 0)), _whole((1, SGU_WIDTH)), _whole((1, SGU_WIDTH)),
                  _whole((SGU_GROUPS, SGU_CHUNK, SGU_CHUNK)), _whole((SGU_CHUNK, SGU_GROUPS))],
        out_specs=pl.BlockSpec((tm, SGU_WIDTH), lambda i: (i, 0)),
        compiler_params=_params(1),
    )(a_uvz, ln_g, ln_b, w_s, b_t)


def _sgu_bwd(a_uvz, d_out, ln_g, ln_b, w_s, b_t):
    t = a_uvz.shape[0]
    tm = min(512, t)

    def body(uvz_ref, do_ref, lg_ref, lb_ref, ws_ref, bt_ref, duvz_ref, dlg_ref, dlb_ref, dws_ref, dbt_ref):
        @pl.when(pl.program_id(0) == 0)
        def _():
            dlg_ref[...] = jnp.zeros_like(dlg_ref)
            dlb_ref[...] = jnp.zeros_like(dlb_ref)
            dws_ref[...] = jnp.zeros_like(dws_ref)
            dbt_ref[...] = jnp.zeros_like(dbt_ref)

        for row0 in range(0, tm, SGU_CHUNK):
            rows = pl.ds(row0, SGU_CHUNK)
            for grp in range(SGU_GROUPS):
                lanes = pl.ds(grp * 128, 128)
                args = _sgu_pieces(uvz_ref, lg_ref, lb_ref, ws_ref, bt_ref, row0, grp)
                _, pull = jax.vjp(_sgu_core, *args)
                du, dv, dz, dlg, dlb, dws, dbcol = pull(do_ref[rows, lanes])
                duvz_ref[rows, pl.ds(grp * 128, 128)] = du.astype(duvz_ref.dtype)
                duvz_ref[rows, pl.ds(SGU_WIDTH + grp * 128, 128)] = dv.astype(duvz_ref.dtype)
                duvz_ref[rows, pl.ds(2 * SGU_WIDTH + grp * 128, 128)] = dz.astype(duvz_ref.dtype)
                dlg_ref[:, lanes] += dlg
                dlb_ref[:, lanes] += dlb
                dws_ref[grp] += dws
                dbt_ref[...] += dbcol * _onehot_row(grp, SGU_GROUPS)

    return pl.pallas_call(
        body, name="sgu_bwd", grid=(t // tm,),
        out_shape=(jax.ShapeDtypeStruct((t, 3 * SGU_WIDTH), BF16), jax.ShapeDtypeStruct((1, SGU_WIDTH), F32),
                   jax.ShapeDtypeStruct((1, SGU_WIDTH), F32), jax.ShapeDtypeStruct((SGU_GROUPS, SGU_CHUNK, SGU_CHUNK), F32),
                   jax.ShapeDtypeStruct((SGU_CHUNK, SGU_GROUPS), F32)),
        in_specs=[pl.BlockSpec((tm, 3 * SGU_WIDTH), lambda i: (i, 0)), pl.BlockSpec((tm, SGU_WIDTH), lambda i: (i, 0)),
                  _whole((1, SGU_WIDTH)), _whole((1, SGU_WIDTH)),
                  _whole((SGU_GROUPS, SGU_CHUNK, SGU_CHUNK)), _whole((SGU_CHUNK, SGU_GROUPS))],
        out_specs=(pl.BlockSpec((tm, 3 * SGU_WIDTH), lambda i: (i, 0)), _whole((1, SGU_WIDTH)), _whole((1, SGU_WIDTH)),
                   _whole((SGU_GROUPS, SGU_CHUNK, SGU_CHUNK)), _whole((SGU_CHUNK, SGU_GROUPS))),
        compiler_params=_params(1),
    )(a_uvz, d_out, ln_g, ln_b, w_s, b_t)


def _dn_conv(xpad_ref, c_ref, cur_ref, prev_ref, w_ref, first):
    for b in range(cur_ref.shape[0]):
        xpad_ref[b, 0:CONV_HALO, :] = jnp.where(first, 0.0, prev_ref[b])
        xpad_ref[b, CONV_HALO:, :] = cur_ref[b]
        acc = None
        for j in range(CONV_K):
            term = w_ref[j:j + 1, :] * xpad_ref[b, pl.ds(CONV_HALO - CONV_K + 1 + j, DN_CHUNK), :]
            acc = term if acc is None else acc + term
        c_ref[b] = acc


def _dn_pairs(nb):
    return [(b, h) for b in range(nb) for h in range(DN_HEADS)]


def _dn_batch_args(c_ref, z_ref):
    pairs = _dn_pairs(c_ref.shape[0])
    pick = lambda ref, b, col: ref[b, :, pl.ds(col, DN_HEAD_DIM)]
    cq = jnp.stack([pick(c_ref, b, h * DN_HEAD_DIM) for b, h in pairs])
    ck = jnp.stack([pick(c_ref, b, DN_WIDTH + h * DN_HEAD_DIM) for b, h in pairs])
    cv = jnp.stack([pick(c_ref, b, 2 * DN_WIDTH + h * DN_HEAD_DIM) for b, h in pairs])
    z = jnp.stack([pick(z_ref, b, h * DN_HEAD_DIM) for b, h in pairs])
    return cq, ck, cv, z


def _dn_weight_specs():
    return [_whole((CONV_K, 3 * DN_WIDTH)), _whole((1, GATE_PAD)), _whole((1, GATE_PAD)), _whole((1, DN_HEAD_DIM))]


def _dn_fwd(qkv, zg, logits, conv_w, alog, dtb, og):
    nb, s, _ = qkv.shape
    nc = s // DN_CHUNK
    pairs = _dn_pairs(nb)
    gn = len(pairs)
    chunk = lambda w: pl.BlockSpec((nb, DN_CHUNK, w), lambda n: (0, n, 0))
    prev = lambda n: (0, jnp.maximum(n * (DN_CHUNK // CONV_HALO) - 1, 0), 0)

    def body(cur_ref, prev_ref, z_ref, l_ref, w_ref, alog_ref, dtb_ref, og_ref, out_ref, st_ref, c_ref,
             state_ref, xpad_ref):
        n = pl.program_id(0)

        @pl.when(n == 0)
        def _():
            state_ref[...] = jnp.zeros_like(state_ref)

        _dn_conv(xpad_ref, c_ref, cur_ref, prev_ref, w_ref, n == 0)
        cq, ck, cv, z = _dn_batch_args(c_ref, z_ref)
        state = state_ref[...]
        st_ref[...] = state
        out, new_state = _dn_core(cq, ck, cv, z, l_ref[...], state, alog_ref[...], dtb_ref[...], og_ref[...])
        state_ref[...] = new_state
        for i, (b, h) in enumerate(pairs):
            out_ref[b, :, pl.ds(h * DN_HEAD_DIM, DN_HEAD_DIM)] = out[i].astype(out_ref.dtype)

    return pl.pallas_call(
        body, name="deltanet_fwd", grid=(nc,),
        out_shape=(jax.ShapeDtypeStruct((nb, s, DN_WIDTH), BF16),
                   jax.ShapeDtypeStruct((nc, gn, DN_HEAD_DIM, DN_HEAD_DIM), F32),
                   jax.ShapeDtypeStruct((nb, s, 3 * DN_WIDTH), F32)),
        in_specs=[chunk(3 * DN_WIDTH), pl.BlockSpec((nb, CONV_HALO, 3 * DN_WIDTH), prev), chunk(DN_WIDTH),
                  chunk(GATE_PAD)] + _dn_weight_specs(),
        out_specs=(chunk(DN_WIDTH), pl.BlockSpec((None, gn, DN_HEAD_DIM, DN_HEAD_DIM), lambda n: (n, 0, 0, 0)),
                   chunk(3 * DN_WIDTH)),
        scratch_shapes=[pltpu.VMEM((gn, DN_HEAD_DIM, DN_HEAD_DIM), F32),
                        pltpu.VMEM((nb, CONV_HALO + DN_CHUNK, 3 * DN_WIDTH), F32)],
        compiler_params=_params(1),
    )(qkv, qkv, zg, logits, conv_w, alog, dtb, og)


def _dn_bwd(qkv, conv_out, zg, logits, conv_w, alog, dtb, og, states, d_out):
    nb, s, _ = qkv.shape
    nc = s // DN_CHUNK
    rev = lambda n: nc - 1 - n
    pairs = _dn_pairs(nb)
    gn = len(pairs)

    def body(cur_ref, c_ref, z_ref, l_ref, w_ref, alog_ref, dtb_ref, og_ref, st_ref, do_ref,
             dqkv_ref, dz_ref, dl_ref, dw_ref, dalog_ref, ddtb_ref, dog_ref,
             dstate_ref, dcpad_ref):
        n = pl.program_id(0)

        @pl.when(n == 0)
        def _():
            dw_ref[...] = jnp.zeros_like(dw_ref)
            dalog_ref[...] = jnp.zeros_like(dalog_ref)
            ddtb_ref[...] = jnp.zeros_like(ddtb_ref)
            dog_ref[...] = jnp.zeros_like(dog_ref)
            dstate_ref[...] = jnp.zeros_like(dstate_ref)
            dcpad_ref[:, DN_CHUNK:, :] = jnp.zeros((nb, CONV_HALO, 3 * DN_WIDTH), F32)

        cq, ck, cv, z = _dn_batch_args(c_ref, z_ref)
        d_out_g = jnp.stack([do_ref[b, :, pl.ds(h * DN_HEAD_DIM, DN_HEAD_DIM)] for b, h in pairs])
        _, pull = jax.vjp(_dn_core, cq, ck, cv, z, l_ref[...], st_ref[...], alog_ref[...], dtb_ref[...], og_ref[...])
        dcq, dck, dcv, dz, dlog, dstate, dalog, ddtb, dog = pull((d_out_g, dstate_ref[...]))
        dstate_ref[...] = dstate
        dl_ref[...] = dlog.astype(dl_ref.dtype)
        dalog_ref[...] += dalog
        ddtb_ref[...] += ddtb
        dog_ref[...] += dog
        for i, (b, h) in enumerate(pairs):
            dcpad_ref[b, 0:DN_CHUNK, pl.ds(h * DN_HEAD_DIM, DN_HEAD_DIM)] = dcq[i]
            dcpad_ref[b, 0:DN_CHUNK, pl.ds(DN_WIDTH + h * DN_HEAD_DIM, DN_HEAD_DIM)] = dck[i]
            dcpad_ref[b, 0:DN_CHUNK, pl.ds(2 * DN_WIDTH + h * DN_HEAD_DIM, DN_HEAD_DIM)] = dcv[i]
            dz_ref[b, :, pl.ds(h * DN_HEAD_DIM, DN_HEAD_DIM)] = dz[i].astype(dz_ref.dtype)
        for b in range(nb):
            xb = cur_ref[b]
            dx = None
            for j in range(CONV_K):
                shifted = dcpad_ref[b, pl.ds(CONV_K - 1 - j, DN_CHUNK), :]
                term = w_ref[j:j + 1, :] * shifted
                dx = term if dx is None else dx + term
                dw_ref[j:j + 1, :] += _rowsum(shifted * xb)
            dqkv_ref[b] = dx.astype(dqkv_ref.dtype)
            dcpad_ref[b, DN_CHUNK:, :] = dcpad_ref[b, 0:CONV_HALO, :]

    chunk = lambda w: pl.BlockSpec((nb, DN_CHUNK, w), lambda n: (0, rev(n), 0))
    return pl.pallas_call(
        body, name="deltanet_bwd", grid=(nc,),
        out_shape=(jax.ShapeDtypeStruct((nb, s, 3 * DN_WIDTH), BF16), jax.ShapeDtypeStruct((nb, s, DN_WIDTH), BF16),
                   jax.ShapeDtypeStruct((nb, s, GATE_PAD), BF16), jax.ShapeDtypeStruct((CONV_K, 3 * DN_WIDTH), F32),
                   jax.ShapeDtypeStruct((1, GATE_PAD), F32), jax.ShapeDtypeStruct((1, GATE_PAD), F32),
                   jax.ShapeDtypeStruct((1, DN_HEAD_DIM), F32)),
        in_specs=[chunk(3 * DN_WIDTH), chunk(3 * DN_WIDTH), chunk(DN_WIDTH), chunk(GATE_PAD)] + _dn_weight_specs() + [
            pl.BlockSpec((None, gn, DN_HEAD_DIM, DN_HEAD_DIM), lambda n: (rev(n), 0, 0, 0)),
            chunk(DN_WIDTH)],
        out_specs=(chunk(3 * DN_WIDTH), chunk(DN_WIDTH), chunk(GATE_PAD), _whole((CONV_K, 3 * DN_WIDTH)),
                   _whole((1, GATE_PAD)), _whole((1, GATE_PAD)), _whole((1, DN_HEAD_DIM))),
        scratch_shapes=[pltpu.VMEM((gn, DN_HEAD_DIM, DN_HEAD_DIM), F32),
                        pltpu.VMEM((nb, DN_CHUNK + CONV_HALO, 3 * DN_WIDTH), F32)],
        compiler_params=_params(1),
    )(qkv, conv_out, zg, logits, conv_w, alog, dtb, og, states, d_out)


def _head(a_out, b_out, x2, p2, target, w_out, w_out_t, w_gate, w_gate_t, w_proj, ple_g, fin_g):
    t = x2.shape[0]
    tm = min(512, t)
    steps = t // tm

    def body(a_ref, b_ref, x_ref, p_ref, y_ref, wo_ref, wot_ref, wg_ref, wgt_ref, wp_ref, pg_ref, fg_ref,
             da_ref, db_ref, dh_ref, dwo_hbm, dwg_hbm, dwp_hbm, dpg_ref, dfg_ref, loss_ref,
             dwo_acc, dwg_acc, dwp_acc):
        i = pl.program_id(0)

        @pl.when(i == 0)
        def _():
            dwo_acc[...] = jnp.zeros_like(dwo_acc)
            dwg_acc[...] = jnp.zeros_like(dwg_acc)
            dwp_acc[...] = jnp.zeros_like(dwp_acc)
            dpg_ref[...] = jnp.zeros_like(dpg_ref)
            dfg_ref[...] = jnp.zeros_like(dfg_ref)
            loss_ref[...] = jnp.zeros_like(loss_ref)

        a = a_ref[...]
        bb = b_ref[...]
        pb = p_ref[...].astype(BF16)
        pg = pg_ref[...]
        fg = fg_ref[...]
        h1 = (x_ref[...] + jnp.dot(a, wo_ref[0:SGU_WIDTH, :], preferred_element_type=F32)
              + jnp.dot(bb, wo_ref[SGU_WIDTH:, :], preferred_element_type=F32))
        n1, r1 = _rms(h1)
        rn = (n1 * pg).astype(BF16)
        gate = jax.nn.sigmoid(jnp.dot(rn, wg_ref[...], preferred_element_type=F32))
        pp = jnp.dot(pb, wp_ref[...], preferred_element_type=F32)
        h2 = h1 + gate * pp
        n2, r2 = _rms(h2)
        err = n2 * fg - y_ref[...]
        loss_ref[...] += jnp.broadcast_to(_rowsum(jnp.sum(err * err, axis=-1, keepdims=True)), loss_ref.shape)

        dy = err * (1.0 / D_MODEL)
        dfg_ref[...] += _rowsum(dy * n2)
        dh2 = _rms_bwd(dy * fg, n2, r2)
        dpp = (dh2 * gate).astype(BF16)
        dgl = (dh2 * pp * gate * (1.0 - gate)).astype(BF16)
        dwp_acc[...] += lax.dot_general(pb, dpp, (((0,), (0,)), ((), ())), preferred_element_type=F32)
        dwg_acc[...] += lax.dot_general(rn, dgl, (((0,), (0,)), ((), ())), preferred_element_type=F32)
        drn = jnp.dot(dgl, wgt_ref[...], preferred_element_type=F32)
        dpg_ref[...] += _rowsum(drn * n1)
        dh1 = dh2 + _rms_bwd(drn * pg, n1, r1)
        dh_ref[...] = dh1
        dhb = dh1.astype(BF16)
        da_ref[...] = jnp.dot(dhb, wot_ref[:, 0:SGU_WIDTH], preferred_element_type=F32)
        db_ref[...] = jnp.dot(dhb, wot_ref[:, SGU_WIDTH:], preferred_element_type=F32)
        dwo_acc[0:SGU_WIDTH, :] += lax.dot_general(a, dhb, (((0,), (0,)), ((), ())), preferred_element_type=F32)
        dwo_acc[SGU_WIDTH:, :] += lax.dot_general(bb, dhb, (((0,), (0,)), ((), ())), preferred_element_type=F32)

        @pl.when(i == steps - 1)
        def _():
            pltpu.sync_copy(dwo_acc, dwo_hbm)
            pltpu.sync_copy(dwg_acc, dwg_hbm)
            for j in range(N_DEV):
                pltpu.sync_copy(dwp_acc.at[:, pl.ds(j * LANES, LANES)], dwp_hbm.at[j])

    tile = lambda w: pl.BlockSpec((tm, w), lambda i: (i, 0))
    return pl.pallas_call(
        body, name="head_fwd_bwd", grid=(steps,),
        out_shape=(jax.ShapeDtypeStruct((t, SGU_WIDTH), F32), jax.ShapeDtypeStruct((t, DN_WIDTH), F32),
                   jax.ShapeDtypeStruct((t, D_MODEL), F32), jax.ShapeDtypeStruct((D_MODEL, D_MODEL), F32),
                   jax.ShapeDtypeStruct((D_MODEL, D_MODEL), F32), jax.ShapeDtypeStruct((N_DEV, PLE_DIM, LANES), F32),
                   jax.ShapeDtypeStruct((1, D_MODEL), F32), jax.ShapeDtypeStruct((1, D_MODEL), F32),
                   jax.ShapeDtypeStruct((8, LANES), F32)),
        in_specs=[tile(SGU_WIDTH), tile(DN_WIDTH), tile(D_MODEL), tile(PLE_DIM), tile(D_MODEL),
                  VMEM_SPEC, VMEM_SPEC, VMEM_SPEC, VMEM_SPEC, VMEM_SPEC, _whole((1, D_MODEL)), _whole((1, D_MODEL))],
        out_specs=(tile(SGU_WIDTH), tile(DN_WIDTH), tile(D_MODEL), HBM_SPEC, HBM_SPEC, HBM_SPEC,
                   _whole((1, D_MODEL)), _whole((1, D_MODEL)), _whole((8, LANES))),
        scratch_shapes=[pltpu.VMEM((D_MODEL, D_MODEL), F32), pltpu.VMEM((D_MODEL, D_MODEL), F32),
                        pltpu.VMEM((PLE_DIM, D_MODEL), F32)],
        compiler_params=_params(1),
    )(a_out, b_out, x2, p2, target, w_out, w_out_t, w_gate, w_gate_t, w_proj, ple_g, fin_g)


def _inproj_bwd(x2, dh1, d_a, d_q, d_z, d_l, norm_g, wat, wqt, wzt, wgt):
    t = x2.shape[0]
    tm = min(512, t)
    steps = t // tm

    widths = (d_a.shape[1], d_q.shape[1], d_z.shape[1], d_l.shape[1])
    starts = (0, widths[0], widths[0] + widths[1], widths[0] + widths[1] + widths[2])

    def body(x_ref, dh_ref, da_ref, dq_ref, dz_ref, dl_ref, g_ref, wat_ref, wqt_ref, wzt_ref, wgt_ref,
             dx_ref, dw_hbm, dg_ref, dw_acc, stage_ref):
        i = pl.program_id(0)

        @pl.when(i == 0)
        def _():
            dw_acc[...] = jnp.zeros_like(dw_acc)
            dg_ref[...] = jnp.zeros_like(dg_ref)

        g = g_ref[...]
        n, r = _rms(x_ref[...])
        xn = (n * g).astype(BF16)
        dxn = None
        for d_ref, wt_ref, col0 in zip((da_ref, dq_ref, dz_ref, dl_ref), (wat_ref, wqt_ref, wzt_ref, wgt_ref), starts):
            width = d_ref.shape[1]
            for c0 in range(0, width, 512):
                c1 = min(c0 + 512, width)
                d = d_ref[:, c0:c1].astype(BF16)
                term = jnp.dot(d, wt_ref[c0:c1, :], preferred_element_type=F32)
                dxn = term if dxn is None else dxn + term
                dw_acc[:, col0 + c0:col0 + c1] += lax.dot_general(xn, d, (((0,), (0,)), ((), ())),
                                                                  preferred_element_type=F32)
        dg_ref[...] += _rowsum(dxn * n)
        dx_ref[...] = dh_ref[...] + _rms_bwd(dxn * g, n, r)

        @pl.when(i == steps - 1)
        def _():
            for j in range(N_DEV):
                stage_ref[...] = dw_acc[:, j * IN_SHARD:(j + 1) * IN_SHARD]
                pltpu.sync_copy(stage_ref, dw_hbm.at[j])

    tile = lambda w: pl.BlockSpec((tm, w), lambda i: (i, 0))
    return pl.pallas_call(
        body, name="inproj_bwd", grid=(steps,),
        out_shape=(jax.ShapeDtypeStruct((t, D_MODEL), F32), jax.ShapeDtypeStruct((N_DEV, D_MODEL, IN_SHARD), F32),
                   jax.ShapeDtypeStruct((1, D_MODEL), F32)),
        in_specs=[tile(D_MODEL), tile(D_MODEL)] + [tile(w) for w in widths] + [_whole((1, D_MODEL))] + [VMEM_SPEC] * 4,
        out_specs=(tile(D_MODEL), HBM_SPEC, _whole((1, D_MODEL))),
        scratch_shapes=[pltpu.VMEM((D_MODEL, sum(widths)), F32), pltpu.VMEM((D_MODEL, IN_SHARD), F32)],
        compiler_params=_params(1),
    )(x2, dh1, d_a, d_q, d_z, d_l, norm_g, wat, wqt, wzt, wgt)


def _reduce_adamw(recv, w, m, v, name, row_block=None):
    n, rows, cols = recv.shape
    rb = row_block or rows

    def body(r_ref, w_ref, m_ref, v_ref, g_ref, d_ref, nm_ref, nv_ref):
        g = r_ref[0].astype(F32)
        for i in range(1, n):
            g = g + r_ref[i].astype(F32)
        m_new = ADAM_B1 * m_ref[...] + (1.0 - ADAM_B1) * g
        v_new = ADAM_B2 * v_ref[...] + (1.0 - ADAM_B2) * jnp.square(g)
        m_hat = m_new / (1.0 - ADAM_B1 ** ADAM_STEP)
        v_hat = v_new / (1.0 - ADAM_B2 ** ADAM_STEP)
        g_ref[...] = g
        d_ref[...] = -ADAM_LR * (m_hat / (jnp.sqrt(v_hat) + ADAM_EPS) + ADAM_WD * w_ref[...])
        nm_ref[...] = m_new
        nv_ref[...] = v_new

    blk = pl.BlockSpec((rb, cols), lambda i: (i, 0))
    return pl.pallas_call(
        body, name=name, grid=(rows // rb,),
        out_shape=tuple(jax.ShapeDtypeStruct((rows, cols), F32) for _ in range(4)),
        in_specs=[pl.BlockSpec((n, rb, cols), lambda i: (0, i, 0)), blk, blk, blk],
        out_specs=(blk, blk, blk, blk),
        compiler_params=_params(1),
    )(recv, w, m, v)


def _pack_rows(pieces, rows, dtype):
    flat = jnp.concatenate([jnp.ravel(p).astype(dtype) for p in pieces])
    flat = jnp.pad(flat, (0, rows * LANES - flat.shape[0]))
    return flat.reshape(rows, LANES)


def _unpack(pack, layout):
    flat = pack.reshape(-1)
    out, off = {}, 0
    for name, shape in layout:
        n = _size(shape)
        out[name] = flat[off:off + n].reshape(shape)
        off += n
    return out


def kernel(x, p, norm_g, w_in, sgu_ln_g, sgu_ln_b, sgu_w_s, sgu_b_s, dn_conv_w, dn_a_log, dn_dt_bias, dn_o_norm_g, w_out, ple_norm_g, ple_gate_w, ple_proj_w, final_norm_g, loss_target, m_norm_g, m_w_in, m_sgu_ln_g, m_sgu_ln_b, m_sgu_w_s, m_sgu_b_s, m_dn_conv_w, m_dn_a_log, m_dn_dt_bias, m_dn_o_norm_g, m_w_out, m_ple_norm_g, m_ple_gate_w, m_ple_proj_w, m_final_norm_g, v_norm_g, v_w_in, v_sgu_ln_g, v_sgu_ln_b, v_sgu_w_s, v_sgu_b_s, v_dn_conv_w, v_dn_a_log, v_dn_dt_bias, v_dn_o_norm_g, v_w_out, v_ple_norm_g, v_ple_gate_w, v_ple_proj_w, v_final_norm_g):
    weights = dict(norm_g=norm_g, w_in=w_in, sgu_ln_g=sgu_ln_g, sgu_ln_b=sgu_ln_b, sgu_w_s=sgu_w_s, sgu_b_s=sgu_b_s,
                   dn_conv_w=dn_conv_w, dn_a_log=dn_a_log, dn_dt_bias=dn_dt_bias, dn_o_norm_g=dn_o_norm_g, w_out=w_out,
                   ple_norm_g=ple_norm_g, ple_gate_w=ple_gate_w, ple_proj_w=ple_proj_w, final_norm_g=final_norm_g)
    mom1 = dict(norm_g=m_norm_g, w_in=m_w_in, sgu_ln_g=m_sgu_ln_g, sgu_ln_b=m_sgu_ln_b, sgu_w_s=m_sgu_w_s,
                sgu_b_s=m_sgu_b_s, dn_conv_w=m_dn_conv_w, dn_a_log=m_dn_a_log, dn_dt_bias=m_dn_dt_bias,
                dn_o_norm_g=m_dn_o_norm_g, w_out=m_w_out, ple_norm_g=m_ple_norm_g, ple_gate_w=m_ple_gate_w,
                ple_proj_w=m_ple_proj_w, final_norm_g=m_final_norm_g)
    mom2 = dict(norm_g=v_norm_g, w_in=v_w_in, sgu_ln_g=v_sgu_ln_g, sgu_ln_b=v_sgu_ln_b, sgu_w_s=v_sgu_w_s,
                sgu_b_s=v_sgu_b_s, dn_conv_w=v_dn_conv_w, dn_a_log=v_dn_a_log, dn_dt_bias=v_dn_dt_bias,
                dn_o_norm_g=v_dn_o_norm_g, w_out=v_w_out, ple_norm_g=v_ple_norm_g, ple_gate_w=v_ple_gate_w,
                ple_proj_w=v_ple_proj_w, final_norm_g=v_final_norm_g)
    nb, s, _ = x.shape
    t = nb * s

    full = dict(zip(("w_in", "w_out", "ple_gate_w", "ple_proj_w", "dn_conv_w"), _all_gather(
        [w_in[0].astype(BF16), w_out[0].astype(BF16), ple_gate_w[0].astype(BF16), ple_proj_w[0].astype(BF16),
         dn_conv_w[0]])))
    w_in_full = jnp.moveaxis(full["w_in"], 0, 1).reshape(D_MODEL, IN_COLS)
    wa = w_in_full[:, :3 * SGU_WIDTH]
    wq = w_in_full[:, 3 * SGU_WIDTH:3 * SGU_WIDTH + 3 * DN_WIDTH]
    wz = w_in_full[:, 3 * SGU_WIDTH + 3 * DN_WIDTH:3 * SGU_WIDTH + 4 * DN_WIDTH]
    wg = jnp.pad(w_in_full[:, 3 * SGU_WIDTH + 4 * DN_WIDTH:], ((0, 0), (0, GATE_PAD - 2 * DN_HEADS)))
    w_out_full = full["w_out"].reshape(D_MODEL, D_MODEL)
    w_gate_full = full["ple_gate_w"].reshape(D_MODEL, D_MODEL)
    w_proj_full = jnp.moveaxis(full["ple_proj_w"], 0, 1).reshape(PLE_DIM, D_MODEL)
    conv_full = jnp.moveaxis(full["dn_conv_w"], 0, 1).reshape(CONV_K, 3 * DN_WIDTH)

    pad_row = lambda a: jnp.pad(a.reshape(1, -1), ((0, 0), (DN_HEADS, GATE_PAD - DN_HEADS - a.size)))
    alog, dtb = pad_row(dn_a_log), pad_row(dn_dt_bias)
    og = dn_o_norm_g.reshape(1, DN_HEAD_DIM)
    ws = sgu_w_s.reshape(SGU_GROUPS, SGU_CHUNK, SGU_CHUNK)
    b_t = sgu_b_s.reshape(SGU_GROUPS, SGU_CHUNK).T
    fin_g = final_norm_g.reshape(1, D_MODEL)

    x2 = x.reshape(t, D_MODEL)
    a_uvz, b_qkv, b_z, b_l = _inproj_fwd(x2, norm_g, wa, wq, wz, wg)
    a_out = _sgu_fwd(a_uvz, sgu_ln_g, sgu_ln_b, ws, b_t)
    qkv3 = b_qkv.reshape(nb, s, 3 * DN_WIDTH)
    z3 = b_z.reshape(nb, s, DN_WIDTH)
    l3 = b_l.reshape(nb, s, GATE_PAD)
    b_out, states, conv_out = _dn_fwd(qkv3, z3, l3, conv_full, alog, dtb, og)

    d_a, d_b, dh1, g_w_out, g_gate, g_proj, g_ple_g, g_fin_g, loss_tile = _head(
        a_out, b_out.reshape(t, DN_WIDTH), x2, p.reshape(t, PLE_DIM), loss_target.reshape(t, D_MODEL),
        w_out_full, w_out_full.T, w_gate_full, w_gate_full.T, w_proj_full, ple_norm_g, fin_g)
    d_qkv, d_z, d_l, g_conv, g_alog, g_dtb, g_og = _dn_bwd(
        qkv3, conv_out, z3, l3, conv_full, alog, dtb, og, states, d_b.reshape(nb, s, DN_WIDTH))
    d_uvz, g_ln_g, g_ln_b, g_ws, g_bt = _sgu_bwd(a_uvz, d_a, sgu_ln_g, sgu_ln_b, ws, b_t)
    grad_x, g_w_in, g_norm = _inproj_bwd(
        x2, dh1, d_uvz, d_qkv.reshape(t, 3 * DN_WIDTH), d_z.reshape(t, DN_WIDTH), d_l.reshape(t, GATE_PAD),
        norm_g, wa.T, wq.T, wz.T, wg.T)

    by_device = [g_w_in, g_w_out.reshape(N_DEV, 128, D_MODEL), g_gate.reshape(N_DEV, 128, D_MODEL), g_proj]
    small = _pack_rows([g_conv, g_norm, g_ln_g, g_ln_b, g_ws, g_bt.T, g_alog[:, DN_HEADS:2 * DN_HEADS], g_dtb[:, DN_HEADS:2 * DN_HEADS], g_og,
                        g_ple_g, g_fin_g, (0.5 / D_MODEL) * loss_tile[0:1, 0:1]], SMALL_ROWS, F32)
    *from_sibling, small_sibling = _sibling_exchange(by_device, small)
    core = lax.axis_index("c").astype(jnp.int32).reshape(1)
    *chip_sums, small_sum = _pair_sum(core, by_device, from_sibling, small, small_sibling)
    *received, small_received = _chip_exchange(chip_sums, small_sum)

    results = {}
    for name, recv, rb in zip(("w_in", "w_out", "ple_gate_w", "ple_proj_w"), received, (128, None, None, None)):
        shape = weights[name].shape
        outs = _reduce_adamw(recv, weights[name][0], mom1[name][0], mom2[name][0], "adamw_" + name, rb)
        results[name] = [a.reshape(shape) for a in outs]
    names = [name for name, _ in REPLICATED]
    zeros = jnp.zeros((CONV_K, 3 * DN_WIDTH), F32)
    small_outs = _reduce_adamw(small_received, *[_pack_rows([zeros] + [src[k] for k in names], SMALL_ROWS, F32)
                                                 for src in (weights, mom1, mom2)], "adamw_replicated")
    layout = (SMALL_LAYOUT[0],) + tuple((k, weights[k].shape) for k in names) + (SMALL_LAYOUT[-1],)
    unpacked = [_unpack(a, layout) for a in small_outs]
    for k in names:
        results[k] = [u[k] for u in unpacked]
    loss = unpacked[0]["loss"][0]
    me = 4 * lax.axis_index("x") + 2 * lax.axis_index("y") + lax.axis_index("c")
    conv_mine = lax.dynamic_slice(unpacked[0]["conv"], (0, me * 192), (CONV_K, 192))
    outs = _reduce_adamw(conv_mine[None], dn_conv_w[0], m_dn_conv_w[0], v_dn_conv_w[0], "adamw_dn_conv_w")
    results["dn_conv_w"] = [a.reshape(dn_conv_w.shape) for a in outs]

    return (loss, grad_x.reshape(nb, s, D_MODEL), *[results[k][0] for k in WEIGHT_ORDER],
            *[results[k][1] for k in WEIGHT_ORDER], *[results[k][2] for k in WEIGHT_ORDER],
            *[results[k][3] for k in WEIGHT_ORDER])
```

```python
import jax
import jax.numpy as jnp
from jax import lax
from jax.experimental import pallas as pl
from jax.experimental.pallas import tpu as pltpu

F32 = jnp.float32
BF16 = jnp.bfloat16

N_DEV = 8
D_MODEL = 1024
SGU_WIDTH = 512
SGU_GROUPS = 4
SGU_CHUNK = 128
DN_WIDTH = 512
DN_HEADS = 4
DN_HEAD_DIM = 128
DN_CHUNK = 128
CONV_K = 4
CONV_HALO = 8
PLE_DIM = 256
EPS = 1e-6
IN_COLS = 3592
IN_SHARD = IN_COLS // N_DEV
GATE_PAD = 128

ADAM_LR = 0.001
ADAM_B1 = 0.9
ADAM_B2 = 0.999
ADAM_EPS = 1e-08
ADAM_WD = 0.01
ADAM_STEP = 10

LANES = 128
VMEM_LIMIT = 56 * 1024 * 1024
MESH = pl.DeviceIdType.MESH

REPLICATED = (("norm_g", (1, D_MODEL)), ("sgu_ln_g", (1, SGU_WIDTH)), ("sgu_ln_b", (1, SGU_WIDTH)),
              ("sgu_w_s", (1, SGU_GROUPS, SGU_CHUNK, SGU_CHUNK)), ("sgu_b_s", (1, SGU_GROUPS, SGU_CHUNK)),
              ("dn_a_log", (1, DN_HEADS)), ("dn_dt_bias", (1, DN_HEADS)), ("dn_o_norm_g", (1, DN_HEAD_DIM)),
              ("ple_norm_g", (1, D_MODEL)), ("final_norm_g", (D_MODEL,)))
WEIGHT_ORDER = ("norm_g", "w_in", "sgu_ln_g", "sgu_ln_b", "sgu_w_s", "sgu_b_s", "dn_conv_w", "dn_a_log",
                "dn_dt_bias", "dn_o_norm_g", "w_out", "ple_norm_g", "ple_gate_w", "ple_proj_w", "final_norm_g")


def _size(shape):
    n = 1
    for s in shape:
        n *= s
    return n


SMALL_LAYOUT = (("conv", (CONV_K, 3 * DN_WIDTH)),) + REPLICATED + (("loss", (1,)),)
N_SMALL = sum(_size(s) for _, s in SMALL_LAYOUT)
SMALL_ROWS = -(-N_SMALL // (8 * LANES)) * 8


def _bdot(a, b):
    return jnp.dot(a.astype(BF16), b.astype(BF16), preferred_element_type=F32)


def _sigmoid(x):
    return pl.reciprocal(1.0 + jnp.exp(-x), approx=True)


@jax.custom_vjp
def _silu(x):
    return x * _sigmoid(x)


def _silu_fwd(x):
    s = _sigmoid(x)
    return x * s, (x, s)


def _silu_bwd(res, ct):
    x, s = res
    return (ct * (s * (1.0 + x * (1.0 - s))),)


_silu.defvjp(_silu_fwd, _silu_bwd)


def _gelu(x):
    return 0.5 * x * (1.0 + lax.erf(x * (0.5 ** 0.5)))


def _softplus(x):
    return jnp.maximum(x, 0.0) + jnp.log1p(jnp.exp(-jnp.abs(x)))


def _l2n(x):
    return x * lax.rsqrt(jnp.sum(x * x, axis=-1, keepdims=True) + EPS)


def _rms(x):
    r = lax.rsqrt(jnp.mean(x * x, axis=-1, keepdims=True) + EPS)
    return x * r, r


def _rms_bwd(dn, n, r):
    return r * (dn - n * jnp.mean(dn * n, axis=-1, keepdims=True))


def _onehot_row(idx, width):
    return (lax.broadcasted_iota(jnp.int32, (1, width), 1) == idx).astype(F32)


def _rowsum(x):
    return jnp.sum(x, axis=0, keepdims=True)


def _iota2(n):
    return lax.broadcasted_iota(jnp.int32, (n, n), 0), lax.broadcasted_iota(jnp.int32, (n, n), 1)


def _bmm(a, b):
    return lax.dot_general(a.astype(BF16), b.astype(BF16), (((2,), (1,)), ((0,), (0,))), preferred_element_type=F32)


def _bmm_nt(a, b):
    return lax.dot_general(a.astype(BF16), b.astype(BF16), (((2,), (2,)), ((0,), (0,))), preferred_element_type=F32)


def _bmm_tn(a, b):
    return lax.dot_general(a.astype(BF16), b.astype(BF16), (((1,), (1,)), ((0,), (0,))), preferred_element_type=F32)


def _tri_inv_impl(a):
    n = a.shape[-1]
    r, c = _iota2(n)
    x = r ^ c
    eye = (r == c).astype(F32)
    ad = jnp.where(x < 16, a, 0.0)
    p2 = _bmm(ad, ad)
    e = p2 - ad - _bmm(ad, p2)
    p4 = _bmm(p2, p2)
    e = e + p4 + _bmm(e, p4)
    p8 = _bmm(p4, p4)
    e = e + p8 + _bmm(e, p8)
    size = 16
    while size < n:
        m = jnp.where(jnp.logical_and(x < 2 * size, x >= size), a, 0.0)
        f = m + _bmm(m, e)
        e = e - f - _bmm(e, f)
        size *= 2
    return e + eye


@jax.custom_vjp
def _tri_inv(a):
    return _tri_inv_impl(a)


def _tri_inv_fwd(a):
    t = _tri_inv_impl(a)
    return t, t


def _tri_inv_bwd(t, dt):
    return (-_bmm_tn(t, _bmm_nt(dt, t)),)


_tri_inv.defvjp(_tri_inv_fwd, _tri_inv_bwd)


def _sgu_core(u, v, z, lg, lb, ws, bcol):
    n = ws.shape[0]
    r, c = _iota2(n)
    wm = jnp.where(r >= c, ws, 0.0)
    gu = _gelu(u)
    gv = _gelu(v)
    xc = gv - jnp.mean(gv, axis=-1, keepdims=True)
    ln = xc * lax.rsqrt(jnp.mean(xc * xc, axis=-1, keepdims=True) + EPS) * lg + lb
    s = _bdot(wm, ln) + bcol
    return gu * s * _silu(z)


def _lanes_of(x):
    return jnp.concatenate([x[i] for i in range(x.shape[0])], axis=1)


def _batch_of(x, width):
    return jnp.concatenate([x[None, :, i * width:(i + 1) * width] for i in range(x.shape[1] // width)], axis=0)


def _mask_dot(mask, x):
    hi = x.astype(BF16)
    lo = (x - hi.astype(F32)).astype(BF16)
    m = mask.astype(BF16)
    return jnp.dot(m, hi, preferred_element_type=F32) + jnp.dot(m, lo, preferred_element_type=F32)


def _tri_mask(n, upper):
    r, c = _iota2(n)
    return (r <= c) if upper else (r >= c)


@jax.custom_vjp
def _cumsum_rows(x):
    return _mask_dot(_tri_mask(x.shape[0], False), x)


def _cumsum_rows_fwd(x):
    return _cumsum_rows(x), None


def _cumsum_rows_bwd(_, ct):
    return (_bdot(_tri_mask(ct.shape[0], True), ct),)


_cumsum_rows.defvjp(_cumsum_rows_fwd, _cumsum_rows_bwd)


@jax.custom_vjp
def _colsum_all_rows(x):
    return _mask_dot(jnp.ones((x.shape[0], x.shape[0]), jnp.bool_), x)


def _colsum_all_rows_fwd(x):
    return _colsum_all_rows(x), None


def _colsum_all_rows_bwd(_, ct):
    return (_bdot(jnp.ones((ct.shape[0], ct.shape[0]), F32), ct),)


_colsum_all_rows.defvjp(_colsum_all_rows_fwd, _colsum_all_rows_bwd)


def _dn_core(cq, ck, cv, z, logits, state, alog, dtb, og):
    gn, cn, dh = cq.shape
    heads = gn // logits.shape[0]
    q = _l2n(_silu(cq)) * (dh ** -0.5)
    k = _l2n(_silu(ck))
    v = _silu(cv)
    beta_lanes = jax.nn.sigmoid(logits)
    g_lanes = -jnp.exp(alog) * _softplus(logits + dtb)
    column = lambda rows, lane: jnp.sum(rows * _onehot_row(lane, rows.shape[-1]), axis=-1, keepdims=True)[None]
    beta = jnp.concatenate([column(beta_lanes[i // heads], i % heads) for i in range(gn)], axis=0)
    g = jnp.concatenate([column(g_lanes[i // heads], heads + i % heads) for i in range(gn)], axis=0)
    r, c = _iota2(cn)
    tril = r >= c
    rw = lax.broadcasted_iota(jnp.int32, (cn, dh), 0)
    cw = lax.broadcasted_iota(jnp.int32, (cn, dh), 1)
    upper_wide = (rw <= cw).astype(F32)
    g_wide = jnp.broadcast_to(g, (gn, cn, dh))
    gc_wide = _batch_of(_cumsum_rows(_lanes_of(g_wide)), dh)
    gc_cols = _batch_of(_colsum_all_rows(_lanes_of(g_wide * upper_wide)), dh)[:, :, :cn]
    decay = jnp.exp(jnp.where(tril, gc_wide[:, :, :cn] - gc_cols, -1e30))
    kb = k * beta
    kk = _bmm_nt(kb, k) * decay
    t = _tri_inv(jnp.where(r > c, kk, 0.0))
    eg = jnp.exp(gc_wide)
    sol = _bmm(t, jnp.concatenate([v * beta, kb * eg], axis=-1))
    u_val, w_dec = sol[:, :, :dh], sol[:, :, dh:]
    qk = _bmm_nt(q, k) * decay
    g_last = jnp.sum(g_wide, axis=1, keepdims=True)
    k_dec = k * jnp.exp(g_last - gc_wide)
    ws = _bmm(jnp.concatenate([w_dec, q * eg], axis=1), state)
    v_new = u_val - ws[:, :cn]
    o = ws[:, cn:] + _bmm(qk, v_new)
    new_state = state * jnp.exp(g_last) + _bmm_tn(k_dec, v_new)
    on, _ = _rms(o)
    return on * og * _silu(z), new_state


N_CHIPS = 4
HBM_SPEC = pl.BlockSpec(memory_space=pl.ANY)


def _place():
    return lax.axis_index("x"), lax.axis_index("y"), lax.axis_index("c")


def _other_chip(k):
    x, y, _ = _place()
    px = 1 - x if k & 2 else x
    py = 1 - y if k & 1 else y
    return px, py, 2 * px + py


def _remote(src, dst, send_sem, recv_sem, device):
    return pltpu.make_async_remote_copy(src_ref=src, dst_ref=dst, send_sem=send_sem, recv_sem=recv_sem,
                                        device_id=device, device_id_type=MESH)


def _all_gather(shards):
    n = len(shards)

    def body(*refs):
        srcs, outs = refs[:n], refs[n:2 * n]
        send_sems, recv_sems, local_sems = refs[2 * n:]
        x, y, c = _place()
        me = 4 * x + 2 * y + c
        sibling = (x, y, 1 - c)
        local = [pltpu.make_async_copy(srcs[a], outs[a].at[me], local_sems.at[a]) for a in range(n)]
        for cp in local:
            cp.start()
        sends = []
        for a in range(n):
            sends.append(_remote(srcs[a], outs[a].at[me], send_sems.at[a, 0], recv_sems.at[a, 0], sibling))
        for k in range(1, N_CHIPS):
            px, py, _ = _other_chip(k)
            for a in range(n):
                sends.append(_remote(srcs[a], outs[a].at[me], send_sems.at[a, k], recv_sems.at[a, k], (px, py, c)))
        for cp in sends:
            cp.start()
        passed = []
        for k in range(1, N_CHIPS):
            px, py, _ = _other_chip(k)
            blk = 4 * px + 2 * py + c
            for a in range(n):
                _remote(srcs[a], outs[a].at[blk], send_sems.at[a, k], recv_sems.at[a, k], (px, py, c)).wait_recv()
            for a in range(n):
                cp = _remote(outs[a].at[blk], outs[a].at[blk], send_sems.at[a, 3 + k], recv_sems.at[a, 3 + k], sibling)
                cp.start()
                passed.append(cp)
        for a in range(n):
            _remote(srcs[a], outs[a].at[me + 1 - 2 * c], send_sems.at[a, 0], recv_sems.at[a, 0], sibling).wait_recv()
        for k in range(1, N_CHIPS):
            px, py, _ = _other_chip(k)
            blk = 4 * px + 2 * py + 1 - c
            for a in range(n):
                _remote(srcs[a], outs[a].at[blk], send_sems.at[a, 3 + k], recv_sems.at[a, 3 + k], sibling).wait_recv()
        for cp in sends + passed:
            cp.wait_send()
        for cp in local:
            cp.wait()

    return pl.pallas_call(
        body, name="all_gather_weights",
        out_shape=tuple(jax.ShapeDtypeStruct((N_DEV,) + a.shape, a.dtype) for a in shards),
        in_specs=[HBM_SPEC] * n, out_specs=(HBM_SPEC,) * n,
        scratch_shapes=[pltpu.SemaphoreType.DMA((n, N_DEV - 1)), pltpu.SemaphoreType.DMA((n, N_DEV - 1)),
                        pltpu.SemaphoreType.DMA((n,))],
    )(*shards)


def _sibling_exchange(by_device, small):
    n = len(by_device)

    def body(*refs):
        srcs, small_src = refs[:n], refs[n]
        outs, small_out = refs[n + 1:2 * n + 1], refs[2 * n + 1]
        send_sems, recv_sems = refs[2 * n + 2:]
        x, y, c = _place()
        sibling = (x, y, 1 - c)
        copies = [_remote(small_src, small_out, send_sems.at[n, 0], recv_sems.at[n, 0], sibling)]
        for a in range(n):
            for q in range(N_CHIPS):
                copies.append(_remote(srcs[a].at[2 * q + 1 - c], outs[a].at[q], send_sems.at[a, q], recv_sems.at[a, q],
                                      sibling))
        for cp in copies:
            cp.start()
        for cp in copies:
            cp.wait_recv()
        for cp in copies:
            cp.wait_send()

    return pl.pallas_call(
        body, name="grad_sibling_exchange",
        out_shape=tuple(jax.ShapeDtypeStruct((N_CHIPS,) + a.shape[1:], a.dtype) for a in by_device)
        + (jax.ShapeDtypeStruct(small.shape, small.dtype),),
        in_specs=[HBM_SPEC] * (n + 1), out_specs=(HBM_SPEC,) * (n + 1),
        scratch_shapes=[pltpu.SemaphoreType.DMA((n + 1, N_CHIPS)), pltpu.SemaphoreType.DMA((n + 1, N_CHIPS))],
    )(*by_device, small)


def _chip_exchange(chip_sums, small):
    n = len(chip_sums)

    def body(*refs):
        srcs, small_src = refs[:n], refs[n]
        outs, small_out = refs[n + 1:2 * n + 1], refs[2 * n + 1]
        send_sems, recv_sems, local_sems = refs[2 * n + 2:]
        x, y, c = _place()
        mine = 2 * x + y
        local = [pltpu.make_async_copy(srcs[a].at[mine], outs[a].at[mine], local_sems.at[a]) for a in range(n)]
        local.append(pltpu.make_async_copy(small_src, small_out.at[mine], local_sems.at[n]))
        for cp in local:
            cp.start()
        sends = []
        for k in range(1, N_CHIPS):
            px, py, chip = _other_chip(k)
            for a in range(n):
                sends.append(_remote(srcs[a].at[chip], outs[a].at[mine], send_sems.at[a, k - 1], recv_sems.at[a, k - 1],
                                     (px, py, c)))
            sends.append(_remote(small_src, small_out.at[mine], send_sems.at[n, k - 1], recv_sems.at[n, k - 1], (px, py, c)))
        for cp in sends:
            cp.start()
        for k in range(1, N_CHIPS):
            px, py, chip = _other_chip(k)
            for a in range(n):
                _remote(srcs[a].at[chip], outs[a].at[chip], send_sems.at[a, k - 1], recv_sems.at[a, k - 1],
                        (px, py, c)).wait_recv()
            _remote(small_src, small_out.at[chip], send_sems.at[n, k - 1], recv_sems.at[n, k - 1], (px, py, c)).wait_recv()
        for cp in sends:
            cp.wait_send()
        for cp in local:
            cp.wait()

    return pl.pallas_call(
        body, name="grad_chip_exchange",
        out_shape=tuple(jax.ShapeDtypeStruct(a.shape, a.dtype) for a in chip_sums)
        + (jax.ShapeDtypeStruct((N_CHIPS,) + small.shape, small.dtype),),
        in_specs=[HBM_SPEC] * (n + 1), out_specs=(HBM_SPEC,) * (n + 1),
        scratch_shapes=[pltpu.SemaphoreType.DMA((n + 1, N_CHIPS - 1)), pltpu.SemaphoreType.DMA((n + 1, N_CHIPS - 1)),
                        pltpu.SemaphoreType.DMA((n + 1,))],
    )(*chip_sums, small)


def _pair_sum(core, by_device, from_sibling, small, small_from_sibling):
    n = len(by_device)

    def body(core_ref, *refs):
        own, sib = refs[:n], refs[n:2 * n]
        small_own, small_sib = refs[2 * n], refs[2 * n + 1]
        outs, small_out = refs[2 * n + 2:3 * n + 2], refs[3 * n + 2]
        for a in range(n):
            outs[a][...] = (own[a][...] + sib[a][...]).astype(outs[a].dtype)
        small_out[...] = small_own[...] + small_sib[...]

    def block(a):
        return (None,) + a.shape[1:], (0,) * (a.ndim - 1)

    own_specs = [pl.BlockSpec(block(a)[0], lambda q, core_ref, z=block(a)[1]: (2 * q + core_ref[0],) + z) for a in by_device]
    sib_specs = [pl.BlockSpec(block(a)[0], lambda q, core_ref, z=block(a)[1]: (q,) + z) for a in by_device]
    small_spec = pl.BlockSpec(small.shape, lambda q, core_ref: (0,) * small.ndim)
    return pl.pallas_call(
        body, name="grad_pair_sum",
        grid_spec=pltpu.PrefetchScalarGridSpec(
            num_scalar_prefetch=1, grid=(N_CHIPS,),
            in_specs=own_specs + sib_specs + [small_spec, small_spec],
            out_specs=tuple(sib_specs) + (small_spec,)),
        out_shape=tuple(jax.ShapeDtypeStruct(a.shape, BF16) for a in from_sibling)
        + (jax.ShapeDtypeStruct(small.shape, F32),),
        compiler_params=_params(1),
    )(core, *by_device, *from_sibling, small, small_from_sibling)


def _params(n_axes):
    return pltpu.CompilerParams(dimension_semantics=("arbitrary",) * n_axes, vmem_limit_bytes=VMEM_LIMIT)


def _whole(shape):
    return pl.BlockSpec(shape, lambda *_: (0,) * len(shape))


VMEM_SPEC = pl.BlockSpec(memory_space=pltpu.VMEM)


def _inproj_fwd(x2, seq_len, norm_g, wa, wq, wz, wg, sgu_weights, conv_w):
    t = x2.shape[0]
    tm = min(512, seq_len)
    tiles_per_seq = seq_len // tm

    def body(x_ref, g_ref, wa_ref, wq_ref, wz_ref, wg_ref, lg_ref, lb_ref, ws_ref, bt_ref, cw_ref,
             a_ref, q_ref, z_ref, l_ref, sgu_ref, c_ref, xpad_ref):
        n, _ = _rms(x_ref[...])
        xn = (n * g_ref[...]).astype(BF16)
        for w_ref, o_ref in ((wa_ref, a_ref), (wq_ref, q_ref), (wz_ref, z_ref), (wg_ref, l_ref)):
            width = w_ref.shape[1]
            for c0 in range(0, width, 512):
                c1 = min(c0 + 512, width)
                o_ref[:, c0:c1] = jnp.dot(xn, w_ref[:, c0:c1], preferred_element_type=F32)
        for row0 in range(0, tm, SGU_CHUNK):
            for grp in range(SGU_GROUPS):
                args = _sgu_pieces(a_ref, lg_ref, lb_ref, ws_ref, bt_ref, row0, grp)
                sgu_ref[pl.ds(row0, SGU_CHUNK), pl.ds(grp * 128, 128)] = _sgu_core(*args).astype(sgu_ref.dtype)

        @pl.when(pl.program_id(0) % tiles_per_seq == 0)
        def _():
            xpad_ref[0:CONV_HALO, :] = jnp.zeros((CONV_HALO, xpad_ref.shape[1]), F32)

        xpad_ref[CONV_HALO:, :] = q_ref[...]
        acc = None
        for j in range(CONV_K):
            term = cw_ref[j:j + 1, :] * xpad_ref[pl.ds(CONV_HALO - CONV_K + 1 + j, tm), :]
            acc = term if acc is None else acc + term
        c_ref[...] = acc
        xpad_ref[0:CONV_HALO, :] = xpad_ref[tm:tm + CONV_HALO, :]

    widths = (wa.shape[1], wq.shape[1], wz.shape[1], wg.shape[1])
    tile = lambda w: pl.BlockSpec((tm, w), lambda i: (i, 0))
    sgu_shapes = ((1, SGU_WIDTH), (1, SGU_WIDTH), (SGU_GROUPS, SGU_CHUNK, SGU_CHUNK), (SGU_CHUNK, SGU_GROUPS))
    return pl.pallas_call(
        body, name="inproj_sgu_conv_fwd", grid=(t // tm,),
        out_shape=tuple(jax.ShapeDtypeStruct((t, w), F32) for w in widths)
        + (jax.ShapeDtypeStruct((t, SGU_WIDTH), BF16), jax.ShapeDtypeStruct((t, widths[1]), F32)),
        in_specs=[tile(D_MODEL), _whole((1, D_MODEL)), VMEM_SPEC, VMEM_SPEC, VMEM_SPEC, VMEM_SPEC]
        + [_whole(s) for s in sgu_shapes] + [_whole((CONV_K, widths[1]))],
        out_specs=tuple(tile(w) for w in widths) + (tile(SGU_WIDTH), tile(widths[1])),
        scratch_shapes=[pltpu.VMEM((CONV_HALO + tm, widths[1]), F32)],
        compiler_params=_params(1),
    )(x2, norm_g, wa, wq, wz, wg, *sgu_weights, conv_w)


def _sgu_pieces(uvz_ref, lg_ref, lb_ref, ws_ref, bt_ref, row0, grp):
    rows = pl.ds(row0, SGU_CHUNK)
    lanes = pl.ds(grp * 128, 128)
    u = uvz_ref[rows, pl.ds(grp * 128, 128)]
    v = uvz_ref[rows, pl.ds(SGU_WIDTH + grp * 128, 128)]
    z = uvz_ref[rows, pl.ds(2 * SGU_WIDTH + grp * 128, 128)]
    bcol = jnp.sum(bt_ref[...] * _onehot_row(grp, SGU_GROUPS), axis=-1, keepdims=True)
    return u, v, z, lg_ref[:, lanes], lb_ref[:, lanes], ws_ref[grp], bcol


def _sgu_bwd_tile(uvz_ref, do_ref, sgu_refs, duvz_ref, grad_refs):
    lg_ref, lb_ref, ws_ref, bt_ref = sgu_refs
    dlg_ref, dlb_ref, dws_ref, dbt_ref = grad_refs
    for row0 in range(0, uvz_ref.shape[0], SGU_CHUNK):
        rows = pl.ds(row0, SGU_CHUNK)
        for grp in range(SGU_GROUPS):
            lanes = pl.ds(grp * 128, 128)
            args = _sgu_pieces(uvz_ref, lg_ref, lb_ref, ws_ref, bt_ref, row0, grp)
            _, pull = jax.vjp(_sgu_core, *args)
            du, dv, dz, dlg, dlb, dws, dbcol = pull(do_ref[rows, lanes])
            duvz_ref[rows, pl.ds(grp * 128, 128)] = du.astype(duvz_ref.dtype)
            duvz_ref[rows, pl.ds(SGU_WIDTH + grp * 128, 128)] = dv.astype(duvz_ref.dtype)
            duvz_ref[rows, pl.ds(2 * SGU_WIDTH + grp * 128, 128)] = dz.astype(duvz_ref.dtype)
            dlg_ref[:, lanes] += dlg
            dlb_ref[:, lanes] += dlb
            dws_ref[grp] += dws
            dbt_ref[...] += dbcol * _onehot_row(grp, SGU_GROUPS)


def _dn_pairs(nb):
    return [(b, h) for b in range(nb) for h in range(DN_HEADS)]


def _dn_batch_args(c_ref, z_ref):
    pairs = _dn_pairs(c_ref.shape[0])
    pick = lambda ref, b, col: ref[b, :, pl.ds(col, DN_HEAD_DIM)]
    cq = jnp.stack([pick(c_ref, b, h * DN_HEAD_DIM) for b, h in pairs])
    ck = jnp.stack([pick(c_ref, b, DN_WIDTH + h * DN_HEAD_DIM) for b, h in pairs])
    cv = jnp.stack([pick(c_ref, b, 2 * DN_WIDTH + h * DN_HEAD_DIM) for b, h in pairs])
    z = jnp.stack([pick(z_ref, b, h * DN_HEAD_DIM) for b, h in pairs])
    return cq, ck, cv, z


def _dn_weight_specs():
    return [_whole((CONV_K, 3 * DN_WIDTH)), _whole((1, GATE_PAD)), _whole((1, GATE_PAD)), _whole((1, DN_HEAD_DIM))]


def _dn_fwd(conv_out, zg, logits, alog, dtb, og):
    nb, s, _ = conv_out.shape
    nc = s // DN_CHUNK
    pairs = _dn_pairs(nb)
    gn = len(pairs)
    chunk = lambda w: pl.BlockSpec((nb, DN_CHUNK, w), lambda n: (0, n, 0))

    def body(c_ref, z_ref, l_ref, alog_ref, dtb_ref, og_ref, out_ref, st_ref, state_ref):
        n = pl.program_id(0)

        @pl.when(n == 0)
        def _():
            state_ref[...] = jnp.zeros_like(state_ref)

        cq, ck, cv, z = _dn_batch_args(c_ref, z_ref)
        state = state_ref[...]
        st_ref[...] = state
        out, new_state = _dn_core(cq, ck, cv, z, l_ref[...], state, alog_ref[...], dtb_ref[...], og_ref[...])
        state_ref[...] = new_state
        for i, (b, h) in enumerate(pairs):
            out_ref[b, :, pl.ds(h * DN_HEAD_DIM, DN_HEAD_DIM)] = out[i].astype(out_ref.dtype)

    return pl.pallas_call(
        body, name="deltanet_fwd", grid=(nc,),
        out_shape=(jax.ShapeDtypeStruct((nb, s, DN_WIDTH), BF16),
                   jax.ShapeDtypeStruct((nc, gn, DN_HEAD_DIM, DN_HEAD_DIM), F32)),
        in_specs=[chunk(3 * DN_WIDTH), chunk(DN_WIDTH), chunk(GATE_PAD)] + _dn_weight_specs()[1:],
        out_specs=(chunk(DN_WIDTH), pl.BlockSpec((None, gn, DN_HEAD_DIM, DN_HEAD_DIM), lambda n: (n, 0, 0, 0))),
        scratch_shapes=[pltpu.VMEM((gn, DN_HEAD_DIM, DN_HEAD_DIM), F32)],
        compiler_params=_params(1),
    )(conv_out, zg, logits, alog, dtb, og)


def _dn_bwd(qkv, conv_out, zg, logits, conv_w, alog, dtb, og, states, d_out):
    nb, s, _ = qkv.shape
    nc = s // DN_CHUNK
    rev = lambda n: nc - 1 - n
    pairs = _dn_pairs(nb)
    gn = len(pairs)

    def body(cur_ref, c_ref, z_ref, l_ref, w_ref, alog_ref, dtb_ref, og_ref, st_ref, do_ref,
             dqkv_ref, dz_ref, dl_ref, dw_ref, dalog_ref, ddtb_ref, dog_ref,
             dstate_ref, dcpad_ref):
        n = pl.program_id(0)

        @pl.when(n == 0)
        def _():
            dw_ref[...] = jnp.zeros_like(dw_ref)
            dalog_ref[...] = jnp.zeros_like(dalog_ref)
            ddtb_ref[...] = jnp.zeros_like(ddtb_ref)
            dog_ref[...] = jnp.zeros_like(dog_ref)
            dstate_ref[...] = jnp.zeros_like(dstate_ref)
            dcpad_ref[:, DN_CHUNK:, :] = jnp.zeros((nb, CONV_HALO, 3 * DN_WIDTH), F32)

        cq, ck, cv, z = _dn_batch_args(c_ref, z_ref)
        d_out_g = jnp.stack([do_ref[b, :, pl.ds(h * DN_HEAD_DIM, DN_HEAD_DIM)] for b, h in pairs])
        _, pull = jax.vjp(_dn_core, cq, ck, cv, z, l_ref[...], st_ref[...], alog_ref[...], dtb_ref[...], og_ref[...])
        dcq, dck, dcv, dz, dlog, dstate, dalog, ddtb, dog = pull((d_out_g, dstate_ref[...]))
        dstate_ref[...] = dstate
        dl_ref[...] = dlog.astype(dl_ref.dtype)
        dalog_ref[...] += dalog
        ddtb_ref[...] += ddtb
        dog_ref[...] += dog
        for i, (b, h) in enumerate(pairs):
            dcpad_ref[b, 0:DN_CHUNK, pl.ds(h * DN_HEAD_DIM, DN_HEAD_DIM)] = dcq[i]
            dcpad_ref[b, 0:DN_CHUNK, pl.ds(DN_WIDTH + h * DN_HEAD_DIM, DN_HEAD_DIM)] = dck[i]
            dcpad_ref[b, 0:DN_CHUNK, pl.ds(2 * DN_WIDTH + h * DN_HEAD_DIM, DN_HEAD_DIM)] = dcv[i]
            dz_ref[b, :, pl.ds(h * DN_HEAD_DIM, DN_HEAD_DIM)] = dz[i].astype(dz_ref.dtype)
        for b in range(nb):
            xb = cur_ref[b]
            dx = None
            for j in range(CONV_K):
                shifted = dcpad_ref[b, pl.ds(CONV_K - 1 - j, DN_CHUNK), :]
                term = w_ref[j:j + 1, :] * shifted
                dx = term if dx is None else dx + term
                dw_ref[j:j + 1, :] += _rowsum(shifted * xb)
            dqkv_ref[b] = dx.astype(dqkv_ref.dtype)
            dcpad_ref[b, DN_CHUNK:, :] = dcpad_ref[b, 0:CONV_HALO, :]

    chunk = lambda w: pl.BlockSpec((nb, DN_CHUNK, w), lambda n: (0, rev(n), 0))
    return pl.pallas_call(
        body, name="deltanet_bwd", grid=(nc,),
        out_shape=(jax.ShapeDtypeStruct((nb, s, 3 * DN_WIDTH), BF16), jax.ShapeDtypeStruct((nb, s, DN_WIDTH), BF16),
                   jax.ShapeDtypeStruct((nb, s, GATE_PAD), BF16), jax.ShapeDtypeStruct((CONV_K, 3 * DN_WIDTH), F32),
                   jax.ShapeDtypeStruct((1, GATE_PAD), F32), jax.ShapeDtypeStruct((1, GATE_PAD), F32),
                   jax.ShapeDtypeStruct((1, DN_HEAD_DIM), F32)),
        in_specs=[chunk(3 * DN_WIDTH), chunk(3 * DN_WIDTH), chunk(DN_WIDTH), chunk(GATE_PAD)] + _dn_weight_specs() + [
            pl.BlockSpec((None, gn, DN_HEAD_DIM, DN_HEAD_DIM), lambda n: (rev(n), 0, 0, 0)),
            chunk(DN_WIDTH)],
        out_specs=(chunk(3 * DN_WIDTH), chunk(DN_WIDTH), chunk(GATE_PAD), _whole((CONV_K, 3 * DN_WIDTH)),
                   _whole((1, GATE_PAD)), _whole((1, GATE_PAD)), _whole((1, DN_HEAD_DIM))),
        scratch_shapes=[pltpu.VMEM((gn, DN_HEAD_DIM, DN_HEAD_DIM), F32),
                        pltpu.VMEM((nb, DN_CHUNK + CONV_HALO, 3 * DN_WIDTH), F32)],
        compiler_params=_params(1),
    )(qkv, conv_out, zg, logits, conv_w, alog, dtb, og, states, d_out)


def _head(a_out, b_out, x2, p2, target, w_out, w_out_t, w_gate, w_gate_t, w_proj, ple_g, fin_g):
    t = x2.shape[0]
    tm = min(512, t)
    steps = t // tm

    def body(a_ref, b_ref, x_ref, p_ref, y_ref, wo_ref, wot_ref, wg_ref, wgt_ref, wp_ref, pg_ref, fg_ref,
             da_ref, db_ref, dh_ref, dwo_hbm, dwg_hbm, dwp_hbm, dpg_ref, dfg_ref, loss_ref,
             dwo_acc, dwg_acc, dwp_acc):
        i = pl.program_id(0)

        @pl.when(i == 0)
        def _():
            dwo_acc[...] = jnp.zeros_like(dwo_acc)
            dwg_acc[...] = jnp.zeros_like(dwg_acc)
            dwp_acc[...] = jnp.zeros_like(dwp_acc)
            dpg_ref[...] = jnp.zeros_like(dpg_ref)
            dfg_ref[...] = jnp.zeros_like(dfg_ref)
            loss_ref[...] = jnp.zeros_like(loss_ref)

        a = a_ref[...]
        bb = b_ref[...]
        pb = p_ref[...].astype(BF16)
        pg = pg_ref[...]
        fg = fg_ref[...]
        h1 = (x_ref[...] + jnp.dot(a, wo_ref[0:SGU_WIDTH, :], preferred_element_type=F32)
              + jnp.dot(bb, wo_ref[SGU_WIDTH:, :], preferred_element_type=F32))
        n1, r1 = _rms(h1)
        rn = (n1 * pg).astype(BF16)
        gate = jax.nn.sigmoid(jnp.dot(rn, wg_ref[...], preferred_element_type=F32))
        pp = jnp.dot(pb, wp_ref[...], preferred_element_type=F32)
        h2 = h1 + gate * pp
        n2, r2 = _rms(h2)
        err = n2 * fg - y_ref[...]
        loss_ref[...] += jnp.broadcast_to(_rowsum(jnp.sum(err * err, axis=-1, keepdims=True)), loss_ref.shape)

        dy = err * (1.0 / D_MODEL)
        dfg_ref[...] += _rowsum(dy * n2)
        dh2 = _rms_bwd(dy * fg, n2, r2)
        dpp = (dh2 * gate).astype(BF16)
        dgl = (dh2 * pp * gate * (1.0 - gate)).astype(BF16)
        dwp_acc[...] += lax.dot_general(pb, dpp, (((0,), (0,)), ((), ())), preferred_element_type=F32)
        dwg_acc[...] += lax.dot_general(rn, dgl, (((0,), (0,)), ((), ())), preferred_element_type=F32)
        drn = jnp.dot(dgl, wgt_ref[...], preferred_element_type=F32)
        dpg_ref[...] += _rowsum(drn * n1)
        dh1 = dh2 + _rms_bwd(drn * pg, n1, r1)
        dh_ref[...] = dh1
        dhb = dh1.astype(BF16)
        da_ref[...] = jnp.dot(dhb, wot_ref[:, 0:SGU_WIDTH], preferred_element_type=F32)
        db_ref[...] = jnp.dot(dhb, wot_ref[:, SGU_WIDTH:], preferred_element_type=F32)
        dwo_acc[0:SGU_WIDTH, :] += lax.dot_general(a, dhb, (((0,), (0,)), ((), ())), preferred_element_type=F32)
        dwo_acc[SGU_WIDTH:, :] += lax.dot_general(bb, dhb, (((0,), (0,)), ((), ())), preferred_element_type=F32)

        @pl.when(i == steps - 1)
        def _():
            pltpu.sync_copy(dwo_acc, dwo_hbm)
            pltpu.sync_copy(dwg_acc, dwg_hbm)
            for j in range(N_DEV):
                pltpu.sync_copy(dwp_acc.at[:, pl.ds(j * LANES, LANES)], dwp_hbm.at[j])

    tile = lambda w: pl.BlockSpec((tm, w), lambda i: (i, 0))
    return pl.pallas_call(
        body, name="head_fwd_bwd", grid=(steps,),
        out_shape=(jax.ShapeDtypeStruct((t, SGU_WIDTH), F32), jax.ShapeDtypeStruct((t, DN_WIDTH), F32),
                   jax.ShapeDtypeStruct((t, D_MODEL), F32), jax.ShapeDtypeStruct((D_MODEL, D_MODEL), F32),
                   jax.ShapeDtypeStruct((D_MODEL, D_MODEL), F32), jax.ShapeDtypeStruct((N_DEV, PLE_DIM, LANES), F32),
                   jax.ShapeDtypeStruct((1, D_MODEL), F32), jax.ShapeDtypeStruct((1, D_MODEL), F32),
                   jax.ShapeDtypeStruct((8, LANES), F32)),
        in_specs=[tile(SGU_WIDTH), tile(DN_WIDTH), tile(D_MODEL), tile(PLE_DIM), tile(D_MODEL),
                  VMEM_SPEC, VMEM_SPEC, VMEM_SPEC, VMEM_SPEC, VMEM_SPEC, _whole((1, D_MODEL)), _whole((1, D_MODEL))],
        out_specs=(tile(SGU_WIDTH), tile(DN_WIDTH), tile(D_MODEL), HBM_SPEC, HBM_SPEC, HBM_SPEC,
                   _whole((1, D_MODEL)), _whole((1, D_MODEL)), _whole((8, LANES))),
        scratch_shapes=[pltpu.VMEM((D_MODEL, D_MODEL), F32), pltpu.VMEM((D_MODEL, D_MODEL), F32),
                        pltpu.VMEM((PLE_DIM, D_MODEL), F32)],
        compiler_params=_params(1),
    )(a_out, b_out, x2, p2, target, w_out, w_out_t, w_gate, w_gate_t, w_proj, ple_g, fin_g)


def _inproj_bwd(x2, dh1, a_uvz, d_sgu, d_q, d_z, d_l, norm_g, sgu_weights, wat, wqt, wzt, wgt):
    t = x2.shape[0]
    tm = min(256, t)
    steps = t // tm

    widths = (a_uvz.shape[1], d_q.shape[1], d_z.shape[1], d_l.shape[1])
    starts = (0, widths[0], widths[0] + widths[1], widths[0] + widths[1] + widths[2])

    def body(x_ref, dh_ref, uvz_ref, dsgu_ref, dq_ref, dz_ref, dl_ref, g_ref, lg_ref, lb_ref, ws_ref, bt_ref,
             wat_ref, wqt_ref, wzt_ref, wgt_ref,
             dx_ref, dw_hbm, dg_ref, dlg_ref, dlb_ref, dws_ref, dbt_ref, dw_acc, stage_ref, da_ref):
        i = pl.program_id(0)

        @pl.when(i == 0)
        def _():
            dw_acc[...] = jnp.zeros_like(dw_acc)
            for ref in (dg_ref, dlg_ref, dlb_ref, dws_ref, dbt_ref):
                ref[...] = jnp.zeros_like(ref)

        _sgu_bwd_tile(uvz_ref, dsgu_ref, (lg_ref, lb_ref, ws_ref, bt_ref), da_ref, (dlg_ref, dlb_ref, dws_ref, dbt_ref))
        g = g_ref[...]
        n, r = _rms(x_ref[...])
        xn = (n * g).astype(BF16)
        dxn = None
        for d_ref, wt_ref, col0 in zip((da_ref, dq_ref, dz_ref, dl_ref), (wat_ref, wqt_ref, wzt_ref, wgt_ref), starts):
            term = jnp.dot(d_ref[...], wt_ref[...], preferred_element_type=F32)
            dxn = term if dxn is None else dxn + term
            width = d_ref.shape[1]
            for c0 in range(0, width, 512):
                c1 = min(c0 + 512, width)
                dw_acc[:, col0 + c0:col0 + c1] += lax.dot_general(xn, d_ref[:, c0:c1], (((0,), (0,)), ((), ())),
                                                                  preferred_element_type=F32)
        dg_ref[...] += _rowsum(dxn * n)
        dx_ref[...] = dh_ref[...] + _rms_bwd(dxn * g, n, r)

        @pl.when(i == steps - 1)
        def _():
            for j in range(N_DEV):
                stage_ref[...] = dw_acc[:, j * IN_SHARD:(j + 1) * IN_SHARD]
                pltpu.sync_copy(stage_ref, dw_hbm.at[j])

    tile = lambda w: pl.BlockSpec((tm, w), lambda i: (i, 0))
    sgu_shapes = ((1, SGU_WIDTH), (1, SGU_WIDTH), (SGU_GROUPS, SGU_CHUNK, SGU_CHUNK), (SGU_CHUNK, SGU_GROUPS))
    return pl.pallas_call(
        body, name="inproj_sgu_bwd", grid=(steps,),
        out_shape=(jax.ShapeDtypeStruct((t, D_MODEL), F32), jax.ShapeDtypeStruct((N_DEV, D_MODEL, IN_SHARD), F32),
                   jax.ShapeDtypeStruct((1, D_MODEL), F32)) + tuple(jax.ShapeDtypeStruct(s, F32) for s in sgu_shapes),
        in_specs=[tile(D_MODEL), tile(D_MODEL), tile(widths[0]), tile(SGU_WIDTH)] + [tile(w) for w in widths[1:]]
        + [_whole((1, D_MODEL))] + [_whole(s) for s in sgu_shapes] + [VMEM_SPEC] * 4,
        out_specs=(tile(D_MODEL), HBM_SPEC, _whole((1, D_MODEL))) + tuple(_whole(s) for s in sgu_shapes),
        scratch_shapes=[pltpu.VMEM((D_MODEL, sum(widths)), F32), pltpu.VMEM((D_MODEL, IN_SHARD), F32),
                        pltpu.VMEM((tm, widths[0]), BF16)],
        compiler_params=_params(1),
    )(x2, dh1, a_uvz, d_sgu, d_q, d_z, d_l, norm_g, *sgu_weights, wat, wqt, wzt, wgt)


def _reduce_adamw(recv, w, m, v, name, row_block=None):
    n, rows, cols = recv.shape
    rb = row_block or rows

    def body(r_ref, w_ref, m_ref, v_ref, g_ref, d_ref, nm_ref, nv_ref):
        g = r_ref[0].astype(F32)
        for i in range(1, n):
            g = g + r_ref[i].astype(F32)
        m_new = ADAM_B1 * m_ref[...] + (1.0 - ADAM_B1) * g
        v_new = ADAM_B2 * v_ref[...] + (1.0 - ADAM_B2) * jnp.square(g)
        m_hat = m_new / (1.0 - ADAM_B1 ** ADAM_STEP)
        v_hat = v_new / (1.0 - ADAM_B2 ** ADAM_STEP)
        g_ref[...] = g
        d_ref[...] = -ADAM_LR * (m_hat / (jnp.sqrt(v_hat) + ADAM_EPS) + ADAM_WD * w_ref[...])
        nm_ref[...] = m_new
        nv_ref[...] = v_new

    blk = pl.BlockSpec((rb, cols), lambda i: (i, 0))
    return pl.pallas_call(
        body, name=name, grid=(rows // rb,),
        out_shape=tuple(jax.ShapeDtypeStruct((rows, cols), F32) for _ in range(4)),
        in_specs=[pl.BlockSpec((n, rb, cols), lambda i: (0, i, 0)), blk, blk, blk],
        out_specs=(blk, blk, blk, blk),
        compiler_params=_params(1),
    )(recv, w, m, v)


def _pack_rows(pieces, rows, dtype):
    flat = jnp.concatenate([jnp.ravel(p).astype(dtype) for p in pieces])
    flat = jnp.pad(flat, (0, rows * LANES - flat.shape[0]))
    return flat.reshape(rows, LANES)


def _unpack(pack, layout):
    flat = pack.reshape(-1)
    out, off = {}, 0
    for name, shape in layout:
        n = _size(shape)
        out[name] = flat[off:off + n].reshape(shape)
        off += n
    return out


def kernel(x, p, norm_g, w_in, sgu_ln_g, sgu_ln_b, sgu_w_s, sgu_b_s, dn_conv_w, dn_a_log, dn_dt_bias, dn_o_norm_g, w_out, ple_norm_g, ple_gate_w, ple_proj_w, final_norm_g, loss_target, m_norm_g, m_w_in, m_sgu_ln_g, m_sgu_ln_b, m_sgu_w_s, m_sgu_b_s, m_dn_conv_w, m_dn_a_log, m_dn_dt_bias, m_dn_o_norm_g, m_w_out, m_ple_norm_g, m_ple_gate_w, m_ple_proj_w, m_final_norm_g, v_norm_g, v_w_in, v_sgu_ln_g, v_sgu_ln_b, v_sgu_w_s, v_sgu_b_s, v_dn_conv_w, v_dn_a_log, v_dn_dt_bias, v_dn_o_norm_g, v_w_out, v_ple_norm_g, v_ple_gate_w, v_ple_proj_w, v_final_norm_g):
    weights = dict(norm_g=norm_g, w_in=w_in, sgu_ln_g=sgu_ln_g, sgu_ln_b=sgu_ln_b, sgu_w_s=sgu_w_s, sgu_b_s=sgu_b_s,
                   dn_conv_w=dn_conv_w, dn_a_log=dn_a_log, dn_dt_bias=dn_dt_bias, dn_o_norm_g=dn_o_norm_g, w_out=w_out,
                   ple_norm_g=ple_norm_g, ple_gate_w=ple_gate_w, ple_proj_w=ple_proj_w, final_norm_g=final_norm_g)
    mom1 = dict(norm_g=m_norm_g, w_in=m_w_in, sgu_ln_g=m_sgu_ln_g, sgu_ln_b=m_sgu_ln_b, sgu_w_s=m_sgu_w_s,
                sgu_b_s=m_sgu_b_s, dn_conv_w=m_dn_conv_w, dn_a_log=m_dn_a_log, dn_dt_bias=m_dn_dt_bias,
                dn_o_norm_g=m_dn_o_norm_g, w_out=m_w_out, ple_norm_g=m_ple_norm_g, ple_gate_w=m_ple_gate_w,
                ple_proj_w=m_ple_proj_w, final_norm_g=m_final_norm_g)
    mom2 = dict(norm_g=v_norm_g, w_in=v_w_in, sgu_ln_g=v_sgu_ln_g, sgu_ln_b=v_sgu_ln_b, sgu_w_s=v_sgu_w_s,
                sgu_b_s=v_sgu_b_s, dn_conv_w=v_dn_conv_w, dn_a_log=v_dn_a_log, dn_dt_bias=v_dn_dt_bias,
                dn_o_norm_g=v_dn_o_norm_g, w_out=v_w_out, ple_norm_g=v_ple_norm_g, ple_gate_w=v_ple_gate_w,
                ple_proj_w=v_ple_proj_w, final_norm_g=v_final_norm_g)
    nb, s, _ = x.shape
    t = nb * s

    full = dict(zip(("w_in", "w_out", "ple_gate_w", "ple_proj_w", "dn_conv_w"), _all_gather(
        [w_in[0].astype(BF16), w_out[0].astype(BF16), ple_gate_w[0].astype(BF16), ple_proj_w[0].astype(BF16),
         dn_conv_w[0]])))
    w_in_full = jnp.moveaxis(full["w_in"], 0, 1).reshape(D_MODEL, IN_COLS)
    wa = w_in_full[:, :3 * SGU_WIDTH]
    wq = w_in_full[:, 3 * SGU_WIDTH:3 * SGU_WIDTH + 3 * DN_WIDTH]
    wz = w_in_full[:, 3 * SGU_WIDTH + 3 * DN_WIDTH:3 * SGU_WIDTH + 4 * DN_WIDTH]
    wg = jnp.pad(w_in_full[:, 3 * SGU_WIDTH + 4 * DN_WIDTH:], ((0, 0), (0, GATE_PAD - 2 * DN_HEADS)))
    w_out_full = full["w_out"].reshape(D_MODEL, D_MODEL)
    w_gate_full = full["ple_gate_w"].reshape(D_MODEL, D_MODEL)
    w_proj_full = jnp.moveaxis(full["ple_proj_w"], 0, 1).reshape(PLE_DIM, D_MODEL)
    conv_full = jnp.moveaxis(full["dn_conv_w"], 0, 1).reshape(CONV_K, 3 * DN_WIDTH)

    pad_row = lambda a: jnp.pad(a.reshape(1, -1), ((0, 0), (DN_HEADS, GATE_PAD - DN_HEADS - a.size)))
    alog, dtb = pad_row(dn_a_log), pad_row(dn_dt_bias)
    og = dn_o_norm_g.reshape(1, DN_HEAD_DIM)
    ws = sgu_w_s.reshape(SGU_GROUPS, SGU_CHUNK, SGU_CHUNK)
    b_t = sgu_b_s.reshape(SGU_GROUPS, SGU_CHUNK).T
    fin_g = final_norm_g.reshape(1, D_MODEL)

    x2 = x.reshape(t, D_MODEL)
    sgu_weights = (sgu_ln_g, sgu_ln_b, ws, b_t)
    a_uvz, b_qkv, b_z, b_l, a_out, conv_out = _inproj_fwd(x2, s, norm_g, wa, wq, wz, wg, sgu_weights, conv_full)
    qkv3 = b_qkv.reshape(nb, s, 3 * DN_WIDTH)
    conv_out = conv_out.reshape(nb, s, 3 * DN_WIDTH)
    z3 = b_z.reshape(nb, s, DN_WIDTH)
    l3 = b_l.reshape(nb, s, GATE_PAD)
    b_out, states = _dn_fwd(conv_out, z3, l3, alog, dtb, og)

    d_a, d_b, dh1, g_w_out, g_gate, g_proj, g_ple_g, g_fin_g, loss_tile = _head(
        a_out, b_out.reshape(t, DN_WIDTH), x2, p.reshape(t, PLE_DIM), loss_target.reshape(t, D_MODEL),
        w_out_full, w_out_full.T, w_gate_full, w_gate_full.T, w_proj_full, ple_norm_g, fin_g)
    d_qkv, d_z, d_l, g_conv, g_alog, g_dtb, g_og = _dn_bwd(
        qkv3, conv_out, z3, l3, conv_full, alog, dtb, og, states, d_b.reshape(nb, s, DN_WIDTH))
    grad_x, g_w_in, g_norm, g_ln_g, g_ln_b, g_ws, g_bt = _inproj_bwd(
        x2, dh1, a_uvz, d_a, d_qkv.reshape(t, 3 * DN_WIDTH), d_z.reshape(t, DN_WIDTH), d_l.reshape(t, GATE_PAD),
        norm_g, sgu_weights, wa.T, wq.T, wz.T, wg.T)

    by_device = [g_w_in, g_w_out.reshape(N_DEV, 128, D_MODEL), g_gate.reshape(N_DEV, 128, D_MODEL), g_proj]
    small = _pack_rows([g_conv, g_norm, g_ln_g, g_ln_b, g_ws, g_bt.T, g_alog[:, DN_HEADS:2 * DN_HEADS], g_dtb[:, DN_HEADS:2 * DN_HEADS], g_og,
                        g_ple_g, g_fin_g, (0.5 / D_MODEL) * loss_tile[0:1, 0:1]], SMALL_ROWS, F32)
    *from_sibling, small_sibling = _sibling_exchange(by_device, small)
    core = lax.axis_index("c").astype(jnp.int32).reshape(1)
    *chip_sums, small_sum = _pair_sum(core, by_device, from_sibling, small, small_sibling)
    *received, small_received = _chip_exchange(chip_sums, small_sum)

    results = {}
    for name, recv, rb in zip(("w_in", "w_out", "ple_gate_w", "ple_proj_w"), received, (128, None, None, None)):
        shape = weights[name].shape
        outs = _reduce_adamw(recv, weights[name][0], mom1[name][0], mom2[name][0], "adamw_" + name, rb)
        results[name] = [a.reshape(shape) for a in outs]
    names = [name for name, _ in REPLICATED]
    zeros = jnp.zeros((CONV_K, 3 * DN_WIDTH), F32)
    small_outs = _reduce_adamw(small_received, *[_pack_rows([zeros] + [src[k] for k in names], SMALL_ROWS, F32)
                                                 for src in (weights, mom1, mom2)], "adamw_replicated")
    layout = (SMALL_LAYOUT[0],) + tuple((k, weights[k].shape) for k in names) + (SMALL_LAYOUT[-1],)
    unpacked = [_unpack(a, layout) for a in small_outs]
    for k in names:
        results[k] = [u[k] for u in unpacked]
    loss = unpacked[0]["loss"][0]
    me = 4 * lax.axis_index("x") + 2 * lax.axis_index("y") + lax.axis_index("c")
    conv_mine = lax.dynamic_slice(unpacked[0]["conv"], (0, me * 192), (CONV_K, 192))
    outs = _reduce_adamw(conv_mine[None], dn_conv_w[0], m_dn_conv_w[0], v_dn_conv_w[0], "adamw_dn_conv_w")
    results["dn_conv_w"] = [a.reshape(dn_conv_w.shape) for a in outs]

    return (loss, grad_x.reshape(nb, s, D_MODEL), *[results[k][0] for k in WEIGHT_ORDER],
            *[results[k][1] for k in WEIGHT_ORDER], *[results[k][2] for k in WEIGHT_ORDER],
            *[results[k][3] for k in WEIGHT_ORDER])
```

```python
import jax
import jax.numpy as jnp
from jax import lax
from jax.experimental import pallas as pl
from jax.experimental.pallas import tpu as pltpu

F32 = jnp.float32
BF16 = jnp.bfloat16

N_DEV = 8
D_MODEL = 1024
SGU_WIDTH = 512
SGU_GROUPS = 4
SGU_CHUNK = 128
DN_WIDTH = 512
DN_HEADS = 4
DN_HEAD_DIM = 128
DN_CHUNK = 128
CONV_K = 4
CONV_HALO = 8
PLE_DIM = 256
EPS = 1e-6
IN_COLS = 3592
IN_SHARD = IN_COLS // N_DEV
GATE_PAD = 128

ADAM_LR = 0.001
ADAM_B1 = 0.9
ADAM_B2 = 0.999
ADAM_EPS = 1e-08
ADAM_WD = 0.01
ADAM_STEP = 10

LANES = 128
VMEM_LIMIT = 56 * 1024 * 1024
MESH = pl.DeviceIdType.MESH

REPLICATED = (("norm_g", (1, D_MODEL)), ("sgu_ln_g", (1, SGU_WIDTH)), ("sgu_ln_b", (1, SGU_WIDTH)),
              ("sgu_w_s", (1, SGU_GROUPS, SGU_CHUNK, SGU_CHUNK)), ("sgu_b_s", (1, SGU_GROUPS, SGU_CHUNK)),
              ("dn_a_log", (1, DN_HEADS)), ("dn_dt_bias", (1, DN_HEADS)), ("dn_o_norm_g", (1, DN_HEAD_DIM)),
              ("ple_norm_g", (1, D_MODEL)), ("final_norm_g", (D_MODEL,)))
WEIGHT_ORDER = ("norm_g", "w_in", "sgu_ln_g", "sgu_ln_b", "sgu_w_s", "sgu_b_s", "dn_conv_w", "dn_a_log",
                "dn_dt_bias", "dn_o_norm_g", "w_out", "ple_norm_g", "ple_gate_w", "ple_proj_w", "final_norm_g")


def _size(shape):
    n = 1
    for s in shape:
        n *= s
    return n


SMALL_LAYOUT = (("conv", (CONV_K, 3 * DN_WIDTH)),) + REPLICATED + (("loss", (1,)),)
N_SMALL = sum(_size(s) for _, s in SMALL_LAYOUT)
SMALL_ROWS = -(-N_SMALL // (8 * LANES)) * 8


def _bdot(a, b):
    return jnp.dot(a.astype(BF16), b.astype(BF16), preferred_element_type=F32)


def _sigmoid(x):
    return pl.reciprocal(1.0 + jnp.exp(-x), approx=True)


@jax.custom_vjp
def _silu(x):
    return x * _sigmoid(x)


def _silu_fwd(x):
    s = _sigmoid(x)
    return x * s, (x, s)


def _silu_bwd(res, ct):
    x, s = res
    return (ct * (s * (1.0 + x * (1.0 - s))),)


_silu.defvjp(_silu_fwd, _silu_bwd)


def _gelu(x):
    return 0.5 * x * (1.0 + lax.erf(x * (0.5 ** 0.5)))


def _softplus(x):
    return jnp.maximum(x, 0.0) + jnp.log1p(jnp.exp(-jnp.abs(x)))


def _l2n(x):
    return x * lax.rsqrt(jnp.sum(x * x, axis=-1, keepdims=True) + EPS)


def _rms(x):
    r = lax.rsqrt(jnp.mean(x * x, axis=-1, keepdims=True) + EPS)
    return x * r, r


def _rms_bwd(dn, n, r):
    return r * (dn - n * jnp.mean(dn * n, axis=-1, keepdims=True))


def _onehot_row(idx, width):
    return (lax.broadcasted_iota(jnp.int32, (1, width), 1) == idx).astype(F32)


def _rowsum(x):
    return jnp.sum(x, axis=0, keepdims=True)


def _iota2(n):
    return lax.broadcasted_iota(jnp.int32, (n, n), 0), lax.broadcasted_iota(jnp.int32, (n, n), 1)


def _bmm(a, b):
    return lax.dot_general(a.astype(BF16), b.astype(BF16), (((2,), (1,)), ((0,), (0,))), preferred_element_type=F32)


def _bmm_nt(a, b):
    return lax.dot_general(a.astype(BF16), b.astype(BF16), (((2,), (2,)), ((0,), (0,))), preferred_element_type=F32)


def _bmm_tn(a, b):
    return lax.dot_general(a.astype(BF16), b.astype(BF16), (((1,), (1,)), ((0,), (0,))), preferred_element_type=F32)


def _tri_inv_impl(a):
    n = a.shape[-1]
    r, c = _iota2(n)
    x = r ^ c
    eye = (r == c).astype(F32)
    ad = jnp.where(x < 16, a, 0.0)
    p2 = _bmm(ad, ad)
    e = p2 - ad - _bmm(ad, p2)
    p4 = _bmm(p2, p2)
    e = e + p4 + _bmm(e, p4)
    p8 = _bmm(p4, p4)
    e = e + p8 + _bmm(e, p8)
    size = 16
    while size < n:
        m = jnp.where(jnp.logical_and(x < 2 * size, x >= size), a, 0.0)
        f = m + _bmm(m, e)
        e = e - f - _bmm(e, f)
        size *= 2
    return e + eye


@jax.custom_vjp
def _tri_inv(a):
    return _tri_inv_impl(a)


def _tri_inv_fwd(a):
    t = _tri_inv_impl(a)
    return t, t


def _tri_inv_bwd(t, dt):
    return (-_bmm_tn(t, _bmm_nt(dt, t)),)


_tri_inv.defvjp(_tri_inv_fwd, _tri_inv_bwd)


def _sgu_core(u, v, z, lg, lb, ws, bcol):
    n = ws.shape[0]
    r, c = _iota2(n)
    wm = jnp.where(r >= c, ws, 0.0)
    gu = _gelu(u)
    gv = _gelu(v)
    xc = gv - jnp.mean(gv, axis=-1, keepdims=True)
    ln = xc * lax.rsqrt(jnp.mean(xc * xc, axis=-1, keepdims=True) + EPS) * lg + lb
    s = _bdot(wm, ln) + bcol
    return gu * s * _silu(z)


def _lanes_of(x):
    return jnp.concatenate([x[i] for i in range(x.shape[0])], axis=1)


def _batch_of(x, width):
    return jnp.concatenate([x[None, :, i * width:(i + 1) * width] for i in range(x.shape[1] // width)], axis=0)


def _mask_dot(mask, x):
    hi = x.astype(BF16)
    lo = (x - hi.astype(F32)).astype(BF16)
    m = mask.astype(BF16)
    return jnp.dot(m, hi, preferred_element_type=F32) + jnp.dot(m, lo, preferred_element_type=F32)


def _tri_mask(n, upper):
    r, c = _iota2(n)
    return (r <= c) if upper else (r >= c)


@jax.custom_vjp
def _cumsum_rows(x):
    return _mask_dot(_tri_mask(x.shape[0], False), x)


def _cumsum_rows_fwd(x):
    return _cumsum_rows(x), None


def _cumsum_rows_bwd(_, ct):
    return (_mask_dot(_tri_mask(ct.shape[0], True), ct),)


_cumsum_rows.defvjp(_cumsum_rows_fwd, _cumsum_rows_bwd)


@jax.custom_vjp
def _colsum_all_rows(x):
    return _mask_dot(jnp.ones((x.shape[0], x.shape[0]), jnp.bool_), x)


def _colsum_all_rows_fwd(x):
    return _colsum_all_rows(x), None


def _colsum_all_rows_bwd(_, ct):
    return (_mask_dot(jnp.ones((ct.shape[0], ct.shape[0]), jnp.bool_), ct),)


_colsum_all_rows.defvjp(_colsum_all_rows_fwd, _colsum_all_rows_bwd)


def _dn_core(cq, ck, cv, z, logits, state, alog, dtb, og):
    gn, cn, dh = cq.shape
    heads = gn // logits.shape[0]
    q = _l2n(_silu(cq)) * (dh ** -0.5)
    k = _l2n(_silu(ck))
    v = _silu(cv)
    beta_lanes = jax.nn.sigmoid(logits)
    g_lanes = -jnp.exp(alog) * _softplus(logits + dtb)
    column = lambda rows, lane: jnp.sum(rows * _onehot_row(lane, rows.shape[-1]), axis=-1, keepdims=True)[None]
    beta = jnp.concatenate([column(beta_lanes[i // heads], i % heads) for i in range(gn)], axis=0)
    g = jnp.concatenate([column(g_lanes[i // heads], heads + i % heads) for i in range(gn)], axis=0)
    r, c = _iota2(cn)
    tril = r >= c
    rw = lax.broadcasted_iota(jnp.int32, (cn, dh), 0)
    cw = lax.broadcasted_iota(jnp.int32, (cn, dh), 1)
    upper_wide = (rw <= cw).astype(F32)
    g_wide = jnp.broadcast_to(g, (gn, cn, dh))
    gc_wide = _batch_of(_cumsum_rows(_lanes_of(g_wide)), dh)
    gc_cols = _batch_of(_colsum_all_rows(_lanes_of(g_wide * upper_wide)), dh)[:, :, :cn]
    decay = jnp.exp(jnp.where(tril, gc_wide[:, :, :cn] - gc_cols, -1e30))
    kb = k * beta
    kk = _bmm_nt(kb, k) * decay
    t = _tri_inv(jnp.where(r > c, kk, 0.0))
    eg = jnp.exp(gc_wide)
    sol = _bmm(t, jnp.concatenate([v * beta, kb * eg], axis=-1))
    u_val, w_dec = sol[:, :, :dh], sol[:, :, dh:]
    qk = _bmm_nt(q, k) * decay
    g_last = jnp.sum(g_wide, axis=1, keepdims=True)
    k_dec = k * jnp.exp(g_last - gc_wide)
    ws = _bmm(jnp.concatenate([w_dec, q * eg], axis=1), state)
    v_new = u_val - ws[:, :cn]
    o = ws[:, cn:] + _bmm(qk, v_new)
    new_state = state * jnp.exp(g_last) + _bmm_tn(k_dec, v_new)
    on, _ = _rms(o)
    return on * og * _silu(z), new_state


N_CHIPS = 4
HBM_SPEC = pl.BlockSpec(memory_space=pl.ANY)


def _place():
    return lax.axis_index("x"), lax.axis_index("y"), lax.axis_index("c")


def _other_chip(k):
    x, y, _ = _place()
    px = 1 - x if k & 2 else x
    py = 1 - y if k & 1 else y
    return px, py, 2 * px + py


def _remote(src, dst, send_sem, recv_sem, device):
    return pltpu.make_async_remote_copy(src_ref=src, dst_ref=dst, send_sem=send_sem, recv_sem=recv_sem,
                                        device_id=device, device_id_type=MESH)


def _other_device(k):
    x, y, c = _place()
    px = 1 - x if k & 4 else x
    py = 1 - y if k & 2 else y
    pc = 1 - c if k & 1 else c
    return (px, py, pc), 4 * px + 2 * py + pc


def _direct_exchange(srcs, outs, send_sems, recv_sems, local_sems, gather):
    x, y, c = _place()
    me = 4 * x + 2 * y + c

    def copies(arriving):
        out_list = []
        for a, (src, out) in enumerate(zip(srcs, outs)):
            for k in range(1, N_DEV):
                peer, index = _other_device(k)
                mine = src if gather else src.at[index]
                out_list.append(_remote(mine, out.at[index if arriving else me], send_sems.at[a, k - 1],
                                        recv_sems.at[a, k - 1], peer))
        return out_list

    def local_copies():
        return [pltpu.make_async_copy(src if gather else src.at[me], out.at[me], local_sems.at[a])
                for a, (src, out) in enumerate(zip(srcs, outs))]

    def start():
        for cp in local_copies() + copies(False):
            cp.start()

    def wait():
        for cp in copies(True):
            cp.wait_recv()
        for cp in copies(False):
            cp.wait_send()
        for cp in local_copies():
            cp.wait()

    return start, wait


def _exchange_scratch(n):
    return [pltpu.SemaphoreType.DMA((n, N_DEV - 1)), pltpu.SemaphoreType.DMA((n, N_DEV - 1)), pltpu.SemaphoreType.DMA((n,))]


def _all_gather(shards):
    n = len(shards)

    def body(*refs):
        srcs, outs = refs[:n], refs[n:2 * n]
        send_sems, recv_sems, local_sems = refs[2 * n:]
        x, y, c = _place()
        me = 4 * x + 2 * y + c
        sibling = (x, y, 1 - c)
        local = [pltpu.make_async_copy(srcs[a], outs[a].at[me], local_sems.at[a]) for a in range(n)]
        for cp in local:
            cp.start()
        sends = []
        for a in range(n):
            sends.append(_remote(srcs[a], outs[a].at[me], send_sems.at[a, 0], recv_sems.at[a, 0], sibling))
        for k in range(1, N_CHIPS):
            px, py, _ = _other_chip(k)
            for a in range(n):
                sends.append(_remote(srcs[a], outs[a].at[me], send_sems.at[a, k], recv_sems.at[a, k], (px, py, c)))
        for cp in sends:
            cp.start()
        passed = []
        for k in range(1, N_CHIPS):
            px, py, _ = _other_chip(k)
            blk = 4 * px + 2 * py + c
            for a in range(n):
                _remote(srcs[a], outs[a].at[blk], send_sems.at[a, k], recv_sems.at[a, k], (px, py, c)).wait_recv()
            for a in range(n):
                cp = _remote(outs[a].at[blk], outs[a].at[blk], send_sems.at[a, 3 + k], recv_sems.at[a, 3 + k], sibling)
                cp.start()
                passed.append(cp)
        for a in range(n):
            _remote(srcs[a], outs[a].at[me + 1 - 2 * c], send_sems.at[a, 0], recv_sems.at[a, 0], sibling).wait_recv()
        for k in range(1, N_CHIPS):
            px, py, _ = _other_chip(k)
            blk = 4 * px + 2 * py + 1 - c
            for a in range(n):
                _remote(srcs[a], outs[a].at[blk], send_sems.at[a, 3 + k], recv_sems.at[a, 3 + k], sibling).wait_recv()
        for cp in sends + passed:
            cp.wait_send()
        for cp in local:
            cp.wait()

    return pl.pallas_call(
        body, name="all_gather_weights",
        out_shape=tuple(jax.ShapeDtypeStruct((N_DEV,) + a.shape, a.dtype) for a in shards),
        in_specs=[HBM_SPEC] * n, out_specs=(HBM_SPEC,) * n,
        scratch_shapes=[pltpu.SemaphoreType.DMA((n, N_DEV - 1)), pltpu.SemaphoreType.DMA((n, N_DEV - 1)),
                        pltpu.SemaphoreType.DMA((n,))],
    )(*shards)


def _sibling_exchange(by_device, small):
    n = len(by_device)

    def body(*refs):
        srcs, small_src = refs[:n], refs[n]
        outs, small_out = refs[n + 1:2 * n + 1], refs[2 * n + 1]
        send_sems, recv_sems = refs[2 * n + 2:]
        x, y, c = _place()
        sibling = (x, y, 1 - c)
        copies = [_remote(small_src, small_out, send_sems.at[n, 0], recv_sems.at[n, 0], sibling)]
        for a in range(n):
            for q in range(N_CHIPS):
                copies.append(_remote(srcs[a].at[2 * q + 1 - c], outs[a].at[q], send_sems.at[a, q], recv_sems.at[a, q],
                                      sibling))
        for cp in copies:
            cp.start()
        for cp in copies:
            cp.wait_recv()
        for cp in copies:
            cp.wait_send()

    return pl.pallas_call(
        body, name="grad_sibling_exchange",
        out_shape=tuple(jax.ShapeDtypeStruct((N_CHIPS,) + a.shape[1:], a.dtype) for a in by_device)
        + (jax.ShapeDtypeStruct(small.shape, small.dtype),),
        in_specs=[HBM_SPEC] * (n + 1), out_specs=(HBM_SPEC,) * (n + 1),
        scratch_shapes=[pltpu.SemaphoreType.DMA((n + 1, N_CHIPS)), pltpu.SemaphoreType.DMA((n + 1, N_CHIPS))],
    )(*by_device, small)


def _chip_exchange(chip_sums, small):
    n = len(chip_sums)

    def body(*refs):
        srcs, small_src = refs[:n], refs[n]
        outs, small_out = refs[n + 1:2 * n + 1], refs[2 * n + 1]
        send_sems, recv_sems, local_sems = refs[2 * n + 2:]
        x, y, c = _place()
        mine = 2 * x + y
        local = [pltpu.make_async_copy(srcs[a].at[mine], outs[a].at[mine], local_sems.at[a]) for a in range(n)]
        local.append(pltpu.make_async_copy(small_src, small_out.at[mine], local_sems.at[n]))
        for cp in local:
            cp.start()
        sends = []
        for k in range(1, N_CHIPS):
            px, py, chip = _other_chip(k)
            for a in range(n):
                sends.append(_remote(srcs[a].at[chip], outs[a].at[mine], send_sems.at[a, k - 1], recv_sems.at[a, k - 1],
                                     (px, py, c)))
            sends.append(_remote(small_src, small_out.at[mine], send_sems.at[n, k - 1], recv_sems.at[n, k - 1], (px, py, c)))
        for cp in sends:
            cp.start()
        for k in range(1, N_CHIPS):
            px, py, chip = _other_chip(k)
            for a in range(n):
                _remote(srcs[a].at[chip], outs[a].at[chip], send_sems.at[a, k - 1], recv_sems.at[a, k - 1],
                        (px, py, c)).wait_recv()
            _remote(small_src, small_out.at[chip], send_sems.at[n, k - 1], recv_sems.at[n, k - 1], (px, py, c)).wait_recv()
        for cp in sends:
            cp.wait_send()
        for cp in local:
            cp.wait()

    return pl.pallas_call(
        body, name="grad_chip_exchange",
        out_shape=tuple(jax.ShapeDtypeStruct(a.shape, a.dtype) for a in chip_sums)
        + (jax.ShapeDtypeStruct((N_CHIPS,) + small.shape, small.dtype),),
        in_specs=[HBM_SPEC] * (n + 1), out_specs=(HBM_SPEC,) * (n + 1),
        scratch_shapes=[pltpu.SemaphoreType.DMA((n + 1, N_CHIPS - 1)), pltpu.SemaphoreType.DMA((n + 1, N_CHIPS - 1)),
                        pltpu.SemaphoreType.DMA((n + 1,))],
    )(*chip_sums, small)


def _pair_sum(core, by_device, from_sibling, small, small_from_sibling):
    n = len(by_device)

    def body(core_ref, *refs):
        own, sib = refs[:n], refs[n:2 * n]
        small_own, small_sib = refs[2 * n], refs[2 * n + 1]
        outs, small_out = refs[2 * n + 2:3 * n + 2], refs[3 * n + 2]
        for a in range(n):
            outs[a][...] = (own[a][...] + sib[a][...]).astype(outs[a].dtype)
        small_out[...] = small_own[...] + small_sib[...]

    def block(a):
        return (None,) + a.shape[1:], (0,) * (a.ndim - 1)

    own_specs = [pl.BlockSpec(block(a)[0], lambda q, core_ref, z=block(a)[1]: (2 * q + core_ref[0],) + z) for a in by_device]
    sib_specs = [pl.BlockSpec(block(a)[0], lambda q, core_ref, z=block(a)[1]: (q,) + z) for a in by_device]
    small_spec = pl.BlockSpec(small.shape, lambda q, core_ref: (0,) * small.ndim)
    return pl.pallas_call(
        body, name="grad_pair_sum",
        grid_spec=pltpu.PrefetchScalarGridSpec(
            num_scalar_prefetch=1, grid=(N_CHIPS,),
            in_specs=own_specs + sib_specs + [small_spec, small_spec],
            out_specs=tuple(sib_specs) + (small_spec,)),
        out_shape=tuple(jax.ShapeDtypeStruct(a.shape, BF16) for a in from_sibling)
        + (jax.ShapeDtypeStruct(small.shape, F32),),
        compiler_params=_params(1),
    )(core, *by_device, *from_sibling, small, small_from_sibling)


def _params(n_axes):
    return pltpu.CompilerParams(dimension_semantics=("arbitrary",) * n_axes, vmem_limit_bytes=VMEM_LIMIT)


def _whole(shape):
    return pl.BlockSpec(shape, lambda *_: (0,) * len(shape))


VMEM_SPEC = pl.BlockSpec(memory_space=pltpu.VMEM)


def _inproj_fwd(x2, seq_len, norm_g, wa, wq, wz, wg, sgu_weights, conv_w, later_shards):
    t = x2.shape[0]
    tm = min(512, seq_len)
    tiles_per_seq = seq_len // tm
    steps = t // tm
    ns = len(later_shards)

    def body(x_ref, g_ref, wa_ref, wq_ref, wz_ref, wg_ref, lg_ref, lb_ref, ws_ref, bt_ref, cw_ref, *rest):
        shard_refs, rest = rest[:ns], rest[ns:]
        a_ref, q_ref, z_ref, l_ref, sgu_ref, c_ref = rest[:6]
        gathered_refs, (xpad_ref, send_sems, recv_sems, local_sems) = rest[6:6 + ns], rest[6 + ns:]
        start_gather, wait_gather = _direct_exchange(shard_refs, gathered_refs, send_sems, recv_sems, local_sems, True)
        pl.when(pl.program_id(0) == 0)(start_gather)
        n, _ = _rms(x_ref[...])
        xn = (n * g_ref[...]).astype(BF16)
        for w_ref, o_ref in ((wa_ref, a_ref), (wq_ref, q_ref), (wz_ref, z_ref), (wg_ref, l_ref)):
            width = w_ref.shape[1]
            for c0 in range(0, width, 512):
                c1 = min(c0 + 512, width)
                o_ref[:, c0:c1] = jnp.dot(xn, w_ref[:, c0:c1], preferred_element_type=F32)
        for row0 in range(0, tm, SGU_CHUNK):
            for grp in range(SGU_GROUPS):
                args = _sgu_pieces(a_ref, lg_ref, lb_ref, ws_ref, bt_ref, row0, grp)
                sgu_ref[pl.ds(row0, SGU_CHUNK), pl.ds(grp * 128, 128)] = _sgu_core(*args).astype(sgu_ref.dtype)

        @pl.when(pl.program_id(0) % tiles_per_seq == 0)
        def _():
            xpad_ref[0:CONV_HALO, :] = jnp.zeros((CONV_HALO, xpad_ref.shape[1]), F32)

        xpad_ref[CONV_HALO:, :] = q_ref[...]
        acc = None
        for j in range(CONV_K):
            term = cw_ref[j:j + 1, :] * xpad_ref[pl.ds(CONV_HALO - CONV_K + 1 + j, tm), :]
            acc = term if acc is None else acc + term
        c_ref[...] = acc
        xpad_ref[0:CONV_HALO, :] = xpad_ref[tm:tm + CONV_HALO, :]
        pl.when(pl.program_id(0) == steps - 1)(wait_gather)

    widths = (wa.shape[1], wq.shape[1], wz.shape[1], wg.shape[1])
    tile = lambda w: pl.BlockSpec((tm, w), lambda i: (i, 0))
    sgu_shapes = ((1, SGU_WIDTH), (1, SGU_WIDTH), (SGU_GROUPS, SGU_CHUNK, SGU_CHUNK), (SGU_CHUNK, SGU_GROUPS))
    return pl.pallas_call(
        body, name="inproj_sgu_conv_fwd", grid=(steps,),
        out_shape=tuple(jax.ShapeDtypeStruct((t, w), F32) for w in widths)
        + (jax.ShapeDtypeStruct((t, SGU_WIDTH), BF16), jax.ShapeDtypeStruct((t, widths[1]), F32))
        + tuple(jax.ShapeDtypeStruct((N_DEV,) + a.shape, a.dtype) for a in later_shards),
        in_specs=[tile(D_MODEL), _whole((1, D_MODEL)), VMEM_SPEC, VMEM_SPEC, VMEM_SPEC, VMEM_SPEC]
        + [_whole(s) for s in sgu_shapes] + [_whole((CONV_K, widths[1]))] + [HBM_SPEC] * ns,
        out_specs=tuple(tile(w) for w in widths) + (tile(SGU_WIDTH), tile(widths[1])) + (HBM_SPEC,) * ns,
        scratch_shapes=[pltpu.VMEM((CONV_HALO + tm, widths[1]), F32)] + _exchange_scratch(ns),
        compiler_params=_params(1),
    )(x2, norm_g, wa, wq, wz, wg, *sgu_weights, conv_w, *later_shards)


def _sgu_pieces(uvz_ref, lg_ref, lb_ref, ws_ref, bt_ref, row0, grp):
    rows = pl.ds(row0, SGU_CHUNK)
    lanes = pl.ds(grp * 128, 128)
    u = uvz_ref[rows, pl.ds(grp * 128, 128)]
    v = uvz_ref[rows, pl.ds(SGU_WIDTH + grp * 128, 128)]
    z = uvz_ref[rows, pl.ds(2 * SGU_WIDTH + grp * 128, 128)]
    bcol = jnp.sum(bt_ref[...] * _onehot_row(grp, SGU_GROUPS), axis=-1, keepdims=True)
    return u, v, z, lg_ref[:, lanes], lb_ref[:, lanes], ws_ref[grp], bcol


def _sgu_bwd_tile(uvz_ref, do_ref, sgu_refs, duvz_ref, grad_refs):
    lg_ref, lb_ref, ws_ref, bt_ref = sgu_refs
    dlg_ref, dlb_ref, dws_ref, dbt_ref = grad_refs
    for row0 in range(0, uvz_ref.shape[0], SGU_CHUNK):
        rows = pl.ds(row0, SGU_CHUNK)
        for grp in range(SGU_GROUPS):
            lanes = pl.ds(grp * 128, 128)
            args = _sgu_pieces(uvz_ref, lg_ref, lb_ref, ws_ref, bt_ref, row0, grp)
            _, pull = jax.vjp(_sgu_core, *args)
            du, dv, dz, dlg, dlb, dws, dbcol = pull(do_ref[rows, lanes])
            duvz_ref[rows, pl.ds(grp * 128, 128)] = du.astype(duvz_ref.dtype)
            duvz_ref[rows, pl.ds(SGU_WIDTH + grp * 128, 128)] = dv.astype(duvz_ref.dtype)
            duvz_ref[rows, pl.ds(2 * SGU_WIDTH + grp * 128, 128)] = dz.astype(duvz_ref.dtype)
            dlg_ref[:, lanes] += dlg
            dlb_ref[:, lanes] += dlb
            dws_ref[grp] += dws
            dbt_ref[...] += dbcol * _onehot_row(grp, SGU_GROUPS)


def _dn_pairs(nb):
    return [(b, h) for b in range(nb) for h in range(DN_HEADS)]


def _dn_batch_args(c_ref, z_ref):
    pairs = _dn_pairs(c_ref.shape[0])
    pick = lambda ref, b, col: ref[b, :, pl.ds(col, DN_HEAD_DIM)]
    cq = jnp.stack([pick(c_ref, b, h * DN_HEAD_DIM) for b, h in pairs])
    ck = jnp.stack([pick(c_ref, b, DN_WIDTH + h * DN_HEAD_DIM) for b, h in pairs])
    cv = jnp.stack([pick(c_ref, b, 2 * DN_WIDTH + h * DN_HEAD_DIM) for b, h in pairs])
    z = jnp.stack([pick(z_ref, b, h * DN_HEAD_DIM) for b, h in pairs])
    return cq, ck, cv, z


def _dn_weight_specs():
    return [_whole((CONV_K, 3 * DN_WIDTH)), _whole((1, GATE_PAD)), _whole((1, GATE_PAD)), _whole((1, DN_HEAD_DIM))]


def _dn_fwd(conv_out, zg, logits, alog, dtb, og):
    nb, s, _ = conv_out.shape
    nc = s // DN_CHUNK
    pairs = _dn_pairs(nb)
    gn = len(pairs)
    chunk = lambda w: pl.BlockSpec((nb, DN_CHUNK, w), lambda n: (0, n, 0))

    def body(c_ref, z_ref, l_ref, alog_ref, dtb_ref, og_ref, out_ref, st_ref, state_ref):
        n = pl.program_id(0)

        @pl.when(n == 0)
        def _():
            state_ref[...] = jnp.zeros_like(state_ref)

        cq, ck, cv, z = _dn_batch_args(c_ref, z_ref)
        state = state_ref[...]
        st_ref[...] = state
        out, new_state = _dn_core(cq, ck, cv, z, l_ref[...], state, alog_ref[...], dtb_ref[...], og_ref[...])
        state_ref[...] = new_state
        for i, (b, h) in enumerate(pairs):
            out_ref[b, :, pl.ds(h * DN_HEAD_DIM, DN_HEAD_DIM)] = out[i].astype(out_ref.dtype)

    return pl.pallas_call(
        body, name="deltanet_fwd", grid=(nc,),
        out_shape=(jax.ShapeDtypeStruct((nb, s, DN_WIDTH), BF16),
                   jax.ShapeDtypeStruct((nc, gn, DN_HEAD_DIM, DN_HEAD_DIM), F32)),
        in_specs=[chunk(3 * DN_WIDTH), chunk(DN_WIDTH), chunk(GATE_PAD)] + _dn_weight_specs()[1:],
        out_specs=(chunk(DN_WIDTH), pl.BlockSpec((None, gn, DN_HEAD_DIM, DN_HEAD_DIM), lambda n: (n, 0, 0, 0))),
        scratch_shapes=[pltpu.VMEM((gn, DN_HEAD_DIM, DN_HEAD_DIM), F32)],
        compiler_params=_params(1),
    )(conv_out, zg, logits, alog, dtb, og)


def _dn_bwd(qkv, conv_out, zg, logits, conv_w, alog, dtb, og, states, d_out, head_grads):
    nb, s, _ = qkv.shape
    nc = s // DN_CHUNK
    rev = lambda n: nc - 1 - n
    pairs = _dn_pairs(nb)
    gn = len(pairs)
    ng = len(head_grads)

    def body(cur_ref, c_ref, z_ref, l_ref, w_ref, alog_ref, dtb_ref, og_ref, st_ref, do_ref, *rest):
        grad_refs, rest = rest[:ng], rest[ng:]
        dqkv_ref, dz_ref, dl_ref, dw_ref, dalog_ref, ddtb_ref, dog_ref = rest[:7]
        recv_refs, (dstate_ref, dcpad_ref, send_sems, recv_sems, local_sems) = rest[7:7 + ng], rest[7 + ng:]
        n = pl.program_id(0)
        start_exchange, wait_exchange = _direct_exchange(grad_refs, recv_refs, send_sems, recv_sems, local_sems, False)
        pl.when(n == 0)(start_exchange)

        @pl.when(n == 0)
        def _():
            dw_ref[...] = jnp.zeros_like(dw_ref)
            dalog_ref[...] = jnp.zeros_like(dalog_ref)
            ddtb_ref[...] = jnp.zeros_like(ddtb_ref)
            dog_ref[...] = jnp.zeros_like(dog_ref)
            dstate_ref[...] = jnp.zeros_like(dstate_ref)
            dcpad_ref[:, DN_CHUNK:, :] = jnp.zeros((nb, CONV_HALO, 3 * DN_WIDTH), F32)

        cq, ck, cv, z = _dn_batch_args(c_ref, z_ref)
        d_out_g = jnp.stack([do_ref[b, :, pl.ds(h * DN_HEAD_DIM, DN_HEAD_DIM)] for b, h in pairs])
        _, pull = jax.vjp(_dn_core, cq, ck, cv, z, l_ref[...], st_ref[...], alog_ref[...], dtb_ref[...], og_ref[...])
        dcq, dck, dcv, dz, dlog, dstate, dalog, ddtb, dog = pull((d_out_g, dstate_ref[...]))
        dstate_ref[...] = dstate
        dl_ref[...] = dlog.astype(dl_ref.dtype)
        dalog_ref[...] += dalog
        ddtb_ref[...] += ddtb
        dog_ref[...] += dog
        for i, (b, h) in enumerate(pairs):
            dcpad_ref[b, 0:DN_CHUNK, pl.ds(h * DN_HEAD_DIM, DN_HEAD_DIM)] = dcq[i]
            dcpad_ref[b, 0:DN_CHUNK, pl.ds(DN_WIDTH + h * DN_HEAD_DIM, DN_HEAD_DIM)] = dck[i]
            dcpad_ref[b, 0:DN_CHUNK, pl.ds(2 * DN_WIDTH + h * DN_HEAD_DIM, DN_HEAD_DIM)] = dcv[i]
            dz_ref[b, :, pl.ds(h * DN_HEAD_DIM, DN_HEAD_DIM)] = dz[i].astype(dz_ref.dtype)
        for b in range(nb):
            xb = cur_ref[b]
            dx = None
            for j in range(CONV_K):
                shifted = dcpad_ref[b, pl.ds(CONV_K - 1 - j, DN_CHUNK), :]
                term = w_ref[j:j + 1, :] * shifted
                dx = term if dx is None else dx + term
                dw_ref[j:j + 1, :] += _rowsum(shifted * xb)
            dqkv_ref[b] = dx.astype(dqkv_ref.dtype)
            dcpad_ref[b, DN_CHUNK:, :] = dcpad_ref[b, 0:CONV_HALO, :]
        pl.when(n == nc - 1)(wait_exchange)

    chunk = lambda w: pl.BlockSpec((nb, DN_CHUNK, w), lambda n: (0, rev(n), 0))
    return pl.pallas_call(
        body, name="deltanet_bwd", grid=(nc,),
        out_shape=(jax.ShapeDtypeStruct((nb, s, 3 * DN_WIDTH), BF16), jax.ShapeDtypeStruct((nb, s, DN_WIDTH), BF16),
                   jax.ShapeDtypeStruct((nb, s, GATE_PAD), BF16), jax.ShapeDtypeStruct((CONV_K, 3 * DN_WIDTH), F32),
                   jax.ShapeDtypeStruct((1, GATE_PAD), F32), jax.ShapeDtypeStruct((1, GATE_PAD), F32),
                   jax.ShapeDtypeStruct((1, DN_HEAD_DIM), F32))
        + tuple(jax.ShapeDtypeStruct(a.shape, a.dtype) for a in head_grads),
        in_specs=[chunk(3 * DN_WIDTH), chunk(3 * DN_WIDTH), chunk(DN_WIDTH), chunk(GATE_PAD)] + _dn_weight_specs() + [
            pl.BlockSpec((None, gn, DN_HEAD_DIM, DN_HEAD_DIM), lambda n: (rev(n), 0, 0, 0)),
            chunk(DN_WIDTH)] + [HBM_SPEC] * ng,
        out_specs=(chunk(3 * DN_WIDTH), chunk(DN_WIDTH), chunk(GATE_PAD), _whole((CONV_K, 3 * DN_WIDTH)),
                   _whole((1, GATE_PAD)), _whole((1, GATE_PAD)), _whole((1, DN_HEAD_DIM))) + (HBM_SPEC,) * ng,
        scratch_shapes=[pltpu.VMEM((gn, DN_HEAD_DIM, DN_HEAD_DIM), F32),
                        pltpu.VMEM((nb, DN_CHUNK + CONV_HALO, 3 * DN_WIDTH), F32)] + _exchange_scratch(ng),
        compiler_params=_params(1),
    )(qkv, conv_out, zg, logits, conv_w, alog, dtb, og, states, d_out, *head_grads)


def _head(a_out, b_out, x2, p2, target, w_out, w_out_t, w_gate, w_gate_t, w_proj, ple_g, fin_g):
    t = x2.shape[0]
    tm = min(512, t)
    steps = t // tm

    def body(a_ref, b_ref, x_ref, p_ref, y_ref, wo_ref, wot_ref, wg_ref, wgt_ref, wp_ref, pg_ref, fg_ref,
             da_ref, db_ref, dh_ref, dwo_hbm, dwg_hbm, dwp_hbm, dpg_ref, dfg_ref, loss_ref,
             dwo_acc, dwg_acc, dwp_acc, rows_stage, cols_stage):
        i = pl.program_id(0)

        @pl.when(i == 0)
        def _():
            dwo_acc[...] = jnp.zeros_like(dwo_acc)
            dwg_acc[...] = jnp.zeros_like(dwg_acc)
            dwp_acc[...] = jnp.zeros_like(dwp_acc)
            dpg_ref[...] = jnp.zeros_like(dpg_ref)
            dfg_ref[...] = jnp.zeros_like(dfg_ref)
            loss_ref[...] = jnp.zeros_like(loss_ref)

        a = a_ref[...]
        bb = b_ref[...]
        pb = p_ref[...].astype(BF16)
        pg = pg_ref[...]
        fg = fg_ref[...]
        h1 = (x_ref[...] + jnp.dot(a, wo_ref[0:SGU_WIDTH, :], preferred_element_type=F32)
              + jnp.dot(bb, wo_ref[SGU_WIDTH:, :], preferred_element_type=F32))
        n1, r1 = _rms(h1)
        rn = (n1 * pg).astype(BF16)
        gate = jax.nn.sigmoid(jnp.dot(rn, wg_ref[...], preferred_element_type=F32))
        pp = jnp.dot(pb, wp_ref[...], preferred_element_type=F32)
        h2 = h1 + gate * pp
        n2, r2 = _rms(h2)
        err = n2 * fg - y_ref[...]
        loss_ref[...] += jnp.broadcast_to(_rowsum(jnp.sum(err * err, axis=-1, keepdims=True)), loss_ref.shape)

        dy = err * (1.0 / D_MODEL)
        dfg_ref[...] += _rowsum(dy * n2)
        dh2 = _rms_bwd(dy * fg, n2, r2)
        dpp = (dh2 * gate).astype(BF16)
        dgl = (dh2 * pp * gate * (1.0 - gate)).astype(BF16)
        dwp_acc[...] += lax.dot_general(pb, dpp, (((0,), (0,)), ((), ())), preferred_element_type=F32)
        dwg_acc[...] += lax.dot_general(rn, dgl, (((0,), (0,)), ((), ())), preferred_element_type=F32)
        drn = jnp.dot(dgl, wgt_ref[...], preferred_element_type=F32)
        dpg_ref[...] += _rowsum(drn * n1)
        dh1 = dh2 + _rms_bwd(drn * pg, n1, r1)
        dh_ref[...] = dh1
        dhb = dh1.astype(BF16)
        da_ref[...] = jnp.dot(dhb, wot_ref[:, 0:SGU_WIDTH], preferred_element_type=F32)
        db_ref[...] = jnp.dot(dhb, wot_ref[:, SGU_WIDTH:], preferred_element_type=F32)
        dwo_acc[0:SGU_WIDTH, :] += lax.dot_general(a, dhb, (((0,), (0,)), ((), ())), preferred_element_type=F32)
        dwo_acc[SGU_WIDTH:, :] += lax.dot_general(bb, dhb, (((0,), (0,)), ((), ())), preferred_element_type=F32)

        @pl.when(i == steps - 1)
        def _():
            for j in range(N_DEV):
                for acc, hbm in ((dwo_acc, dwo_hbm), (dwg_acc, dwg_hbm)):
                    rows_stage[...] = acc[j * LANES:(j + 1) * LANES, :].astype(BF16)
                    pltpu.sync_copy(rows_stage, hbm.at[j])
                cols_stage[...] = dwp_acc[:, j * LANES:(j + 1) * LANES].astype(BF16)
                pltpu.sync_copy(cols_stage, dwp_hbm.at[j])

    tile = lambda w: pl.BlockSpec((tm, w), lambda i: (i, 0))
    return pl.pallas_call(
        body, name="head_fwd_bwd", grid=(steps,),
        out_shape=(jax.ShapeDtypeStruct((t, SGU_WIDTH), F32), jax.ShapeDtypeStruct((t, DN_WIDTH), F32),
                   jax.ShapeDtypeStruct((t, D_MODEL), F32), jax.ShapeDtypeStruct((N_DEV, LANES, D_MODEL), BF16),
                   jax.ShapeDtypeStruct((N_DEV, LANES, D_MODEL), BF16), jax.ShapeDtypeStruct((N_DEV, PLE_DIM, LANES), BF16),
                   jax.ShapeDtypeStruct((1, D_MODEL), F32), jax.ShapeDtypeStruct((1, D_MODEL), F32),
                   jax.ShapeDtypeStruct((8, LANES), F32)),
        in_specs=[tile(SGU_WIDTH), tile(DN_WIDTH), tile(D_MODEL), tile(PLE_DIM), tile(D_MODEL),
                  VMEM_SPEC, VMEM_SPEC, VMEM_SPEC, VMEM_SPEC, VMEM_SPEC, _whole((1, D_MODEL)), _whole((1, D_MODEL))],
        out_specs=(tile(SGU_WIDTH), tile(DN_WIDTH), tile(D_MODEL), HBM_SPEC, HBM_SPEC, HBM_SPEC,
                   _whole((1, D_MODEL)), _whole((1, D_MODEL)), _whole((8, LANES))),
        scratch_shapes=[pltpu.VMEM((D_MODEL, D_MODEL), F32), pltpu.VMEM((D_MODEL, D_MODEL), F32),
                        pltpu.VMEM((PLE_DIM, D_MODEL), F32), pltpu.VMEM((LANES, D_MODEL), BF16),
                        pltpu.VMEM((PLE_DIM, LANES), BF16)],
        compiler_params=_params(1),
    )(a_out, b_out, x2, p2, target, w_out, w_out_t, w_gate, w_gate_t, w_proj, ple_g, fin_g)


def _inproj_bwd(x2, dh1, a_uvz, d_sgu, d_q, d_z, d_l, norm_g, sgu_weights, wat, wqt, wzt, wgt):
    t = x2.shape[0]
    tm = min(256, t)
    steps = t // tm

    widths = (a_uvz.shape[1], d_q.shape[1], d_z.shape[1], d_l.shape[1])
    starts = (0, widths[0], widths[0] + widths[1], widths[0] + widths[1] + widths[2])

    def body(x_ref, dh_ref, uvz_ref, dsgu_ref, dq_ref, dz_ref, dl_ref, g_ref, lg_ref, lb_ref, ws_ref, bt_ref,
             wat_ref, wqt_ref, wzt_ref, wgt_ref,
             dx_ref, dw_hbm, dg_ref, dlg_ref, dlb_ref, dws_ref, dbt_ref, dw_acc, stage_ref, da_ref):
        i = pl.program_id(0)

        @pl.when(i == 0)
        def _():
            dw_acc[...] = jnp.zeros_like(dw_acc)
            for ref in (dg_ref, dlg_ref, dlb_ref, dws_ref, dbt_ref):
                ref[...] = jnp.zeros_like(ref)

        _sgu_bwd_tile(uvz_ref, dsgu_ref, (lg_ref, lb_ref, ws_ref, bt_ref), da_ref, (dlg_ref, dlb_ref, dws_ref, dbt_ref))
        g = g_ref[...]
        n, r = _rms(x_ref[...])
        xn = (n * g).astype(BF16)
        dxn = None
        for d_ref, wt_ref, col0 in zip((da_ref, dq_ref, dz_ref, dl_ref), (wat_ref, wqt_ref, wzt_ref, wgt_ref), starts):
            term = jnp.dot(d_ref[...], wt_ref[...], preferred_element_type=F32)
            dxn = term if dxn is None else dxn + term
            width = d_ref.shape[1]
            for c0 in range(0, width, 512):
                c1 = min(c0 + 512, width)
                dw_acc[:, col0 + c0:col0 + c1] += lax.dot_general(xn, d_ref[:, c0:c1], (((0,), (0,)), ((), ())),
                                                                  preferred_element_type=F32)
        dg_ref[...] += _rowsum(dxn * n)
        dx_ref[...] = dh_ref[...] + _rms_bwd(dxn * g, n, r)

        @pl.when(i == steps - 1)
        def _():
            for j in range(N_DEV):
                stage_ref[...] = dw_acc[:, j * IN_SHARD:(j + 1) * IN_SHARD]
                pltpu.sync_copy(stage_ref, dw_hbm.at[j])

    tile = lambda w: pl.BlockSpec((tm, w), lambda i: (i, 0))
    sgu_shapes = ((1, SGU_WIDTH), (1, SGU_WIDTH), (SGU_GROUPS, SGU_CHUNK, SGU_CHUNK), (SGU_CHUNK, SGU_GROUPS))
    return pl.pallas_call(
        body, name="inproj_sgu_bwd", grid=(steps,),
        out_shape=(jax.ShapeDtypeStruct((t, D_MODEL), F32), jax.ShapeDtypeStruct((N_DEV, D_MODEL, IN_SHARD), F32),
                   jax.ShapeDtypeStruct((1, D_MODEL), F32)) + tuple(jax.ShapeDtypeStruct(s, F32) for s in sgu_shapes),
        in_specs=[tile(D_MODEL), tile(D_MODEL), tile(widths[0]), tile(SGU_WIDTH)] + [tile(w) for w in widths[1:]]
        + [_whole((1, D_MODEL))] + [_whole(s) for s in sgu_shapes] + [VMEM_SPEC] * 4,
        out_specs=(tile(D_MODEL), HBM_SPEC, _whole((1, D_MODEL))) + tuple(_whole(s) for s in sgu_shapes),
        scratch_shapes=[pltpu.VMEM((D_MODEL, sum(widths)), F32), pltpu.VMEM((D_MODEL, IN_SHARD), F32),
                        pltpu.VMEM((tm, widths[0]), BF16)],
        compiler_params=_params(1),
    )(x2, dh1, a_uvz, d_sgu, d_q, d_z, d_l, norm_g, *sgu_weights, wat, wqt, wzt, wgt)


def _reduce_adamw(recv, w, m, v, name, row_block=None):
    n, rows, cols = recv.shape
    rb = row_block or rows

    def body(r_ref, w_ref, m_ref, v_ref, g_ref, d_ref, nm_ref, nv_ref):
        g = r_ref[0].astype(F32)
        for i in range(1, n):
            g = g + r_ref[i].astype(F32)
        m_new = ADAM_B1 * m_ref[...] + (1.0 - ADAM_B1) * g
        v_new = ADAM_B2 * v_ref[...] + (1.0 - ADAM_B2) * jnp.square(g)
        m_hat = m_new / (1.0 - ADAM_B1 ** ADAM_STEP)
        v_hat = v_new / (1.0 - ADAM_B2 ** ADAM_STEP)
        g_ref[...] = g
        d_ref[...] = -ADAM_LR * (m_hat / (jnp.sqrt(v_hat) + ADAM_EPS) + ADAM_WD * w_ref[...])
        nm_ref[...] = m_new
        nv_ref[...] = v_new

    blk = pl.BlockSpec((rb, cols), lambda i: (i, 0))
    return pl.pallas_call(
        body, name=name, grid=(rows // rb,),
        out_shape=tuple(jax.ShapeDtypeStruct((rows, cols), F32) for _ in range(4)),
        in_specs=[pl.BlockSpec((n, rb, cols), lambda i: (0, i, 0)), blk, blk, blk],
        out_specs=(blk, blk, blk, blk),
        compiler_params=_params(1),
    )(recv, w, m, v)


def _pack_rows(pieces, rows, dtype):
    flat = jnp.concatenate([jnp.ravel(p).astype(dtype) for p in pieces])
    flat = jnp.pad(flat, (0, rows * LANES - flat.shape[0]))
    return flat.reshape(rows, LANES)


def _unpack(pack, layout):
    flat = pack.reshape(-1)
    out, off = {}, 0
    for name, shape in layout:
        n = _size(shape)
        out[name] = flat[off:off + n].reshape(shape)
        off += n
    return out


def kernel(x, p, norm_g, w_in, sgu_ln_g, sgu_ln_b, sgu_w_s, sgu_b_s, dn_conv_w, dn_a_log, dn_dt_bias, dn_o_norm_g, w_out, ple_norm_g, ple_gate_w, ple_proj_w, final_norm_g, loss_target, m_norm_g, m_w_in, m_sgu_ln_g, m_sgu_ln_b, m_sgu_w_s, m_sgu_b_s, m_dn_conv_w, m_dn_a_log, m_dn_dt_bias, m_dn_o_norm_g, m_w_out, m_ple_norm_g, m_ple_gate_w, m_ple_proj_w, m_final_norm_g, v_norm_g, v_w_in, v_sgu_ln_g, v_sgu_ln_b, v_sgu_w_s, v_sgu_b_s, v_dn_conv_w, v_dn_a_log, v_dn_dt_bias, v_dn_o_norm_g, v_w_out, v_ple_norm_g, v_ple_gate_w, v_ple_proj_w, v_final_norm_g):
    weights = dict(norm_g=norm_g, w_in=w_in, sgu_ln_g=sgu_ln_g, sgu_ln_b=sgu_ln_b, sgu_w_s=sgu_w_s, sgu_b_s=sgu_b_s,
                   dn_conv_w=dn_conv_w, dn_a_log=dn_a_log, dn_dt_bias=dn_dt_bias, dn_o_norm_g=dn_o_norm_g, w_out=w_out,
                   ple_norm_g=ple_norm_g, ple_gate_w=ple_gate_w, ple_proj_w=ple_proj_w, final_norm_g=final_norm_g)
    mom1 = dict(norm_g=m_norm_g, w_in=m_w_in, sgu_ln_g=m_sgu_ln_g, sgu_ln_b=m_sgu_ln_b, sgu_w_s=m_sgu_w_s,
                sgu_b_s=m_sgu_b_s, dn_conv_w=m_dn_conv_w, dn_a_log=m_dn_a_log, dn_dt_bias=m_dn_dt_bias,
                dn_o_norm_g=m_dn_o_norm_g, w_out=m_w_out, ple_norm_g=m_ple_norm_g, ple_gate_w=m_ple_gate_w,
                ple_proj_w=m_ple_proj_w, final_norm_g=m_final_norm_g)
    mom2 = dict(norm_g=v_norm_g, w_in=v_w_in, sgu_ln_g=v_sgu_ln_g, sgu_ln_b=v_sgu_ln_b, sgu_w_s=v_sgu_w_s,
                sgu_b_s=v_sgu_b_s, dn_conv_w=v_dn_conv_w, dn_a_log=v_dn_a_log, dn_dt_bias=v_dn_dt_bias,
                dn_o_norm_g=v_dn_o_norm_g, w_out=v_w_out, ple_norm_g=v_ple_norm_g, ple_gate_w=v_ple_gate_w,
                ple_proj_w=v_ple_proj_w, final_norm_g=v_final_norm_g)
    nb, s, _ = x.shape
    t = nb * s

    w_in_blocks, conv_blocks = _all_gather([w_in[0].astype(BF16), dn_conv_w[0]])
    w_in_full = jnp.moveaxis(w_in_blocks, 0, 1).reshape(D_MODEL, IN_COLS)
    wa = w_in_full[:, :3 * SGU_WIDTH]
    wq = w_in_full[:, 3 * SGU_WIDTH:3 * SGU_WIDTH + 3 * DN_WIDTH]
    wz = w_in_full[:, 3 * SGU_WIDTH + 3 * DN_WIDTH:3 * SGU_WIDTH + 4 * DN_WIDTH]
    wg = jnp.pad(w_in_full[:, 3 * SGU_WIDTH + 4 * DN_WIDTH:], ((0, 0), (0, GATE_PAD - 2 * DN_HEADS)))
    conv_full = jnp.moveaxis(conv_blocks, 0, 1).reshape(CONV_K, 3 * DN_WIDTH)
    later_shards = [w_out[0].astype(BF16), ple_gate_w[0].astype(BF16), ple_proj_w[0].astype(BF16)]

    pad_row = lambda a: jnp.pad(a.reshape(1, -1), ((0, 0), (DN_HEADS, GATE_PAD - DN_HEADS - a.size)))
    alog, dtb = pad_row(dn_a_log), pad_row(dn_dt_bias)
    og = dn_o_norm_g.reshape(1, DN_HEAD_DIM)
    ws = sgu_w_s.reshape(SGU_GROUPS, SGU_CHUNK, SGU_CHUNK)
    b_t = sgu_b_s.reshape(SGU_GROUPS, SGU_CHUNK).T
    fin_g = final_norm_g.reshape(1, D_MODEL)

    x2 = x.reshape(t, D_MODEL)
    sgu_weights = (sgu_ln_g, sgu_ln_b, ws, b_t)
    a_uvz, b_qkv, b_z, b_l, a_out, conv_out, w_out_blocks, w_gate_blocks, w_proj_blocks = _inproj_fwd(
        x2, s, norm_g, wa, wq, wz, wg, sgu_weights, conv_full, later_shards)
    w_out_full = w_out_blocks.reshape(D_MODEL, D_MODEL)
    w_gate_full = w_gate_blocks.reshape(D_MODEL, D_MODEL)
    w_proj_full = jnp.moveaxis(w_proj_blocks, 0, 1).reshape(PLE_DIM, D_MODEL)
    qkv3 = b_qkv.reshape(nb, s, 3 * DN_WIDTH)
    conv_out = conv_out.reshape(nb, s, 3 * DN_WIDTH)
    z3 = b_z.reshape(nb, s, DN_WIDTH)
    l3 = b_l.reshape(nb, s, GATE_PAD)
    b_out, states = _dn_fwd(conv_out, z3, l3, alog, dtb, og)

    d_a, d_b, dh1, g_w_out, g_gate, g_proj, g_ple_g, g_fin_g, loss_tile = _head(
        a_out, b_out.reshape(t, DN_WIDTH), x2, p.reshape(t, PLE_DIM), loss_target.reshape(t, D_MODEL),
        w_out_full, w_out_full.T, w_gate_full, w_gate_full.T, w_proj_full, ple_norm_g, fin_g)
    d_qkv, d_z, d_l, g_conv, g_alog, g_dtb, g_og, *head_received = _dn_bwd(
        qkv3, conv_out, z3, l3, conv_full, alog, dtb, og, states, d_b.reshape(nb, s, DN_WIDTH),
        [g_w_out, g_gate, g_proj])
    grad_x, g_w_in, g_norm, g_ln_g, g_ln_b, g_ws, g_bt = _inproj_bwd(
        x2, dh1, a_uvz, d_a, d_qkv.reshape(t, 3 * DN_WIDTH), d_z.reshape(t, DN_WIDTH), d_l.reshape(t, GATE_PAD),
        norm_g, sgu_weights, wa.T, wq.T, wz.T, wg.T)

    by_device = [g_w_in]
    small = _pack_rows([g_conv, g_norm, g_ln_g, g_ln_b, g_ws, g_bt.T, g_alog[:, DN_HEADS:2 * DN_HEADS], g_dtb[:, DN_HEADS:2 * DN_HEADS], g_og,
                        g_ple_g, g_fin_g, (0.5 / D_MODEL) * loss_tile[0:1, 0:1]], SMALL_ROWS, F32)
    *from_sibling, small_sibling = _sibling_exchange(by_device, small)
    core = lax.axis_index("c").astype(jnp.int32).reshape(1)
    *chip_sums, small_sum = _pair_sum(core, by_device, from_sibling, small, small_sibling)
    *received, small_received = _chip_exchange(chip_sums, small_sum)

    results = {}
    for name, recv, rb in zip(("w_in", "w_out", "ple_gate_w", "ple_proj_w"), received + head_received,
                              (128, None, None, None)):
        shape = weights[name].shape
        outs = _reduce_adamw(recv, weights[name][0], mom1[name][0], mom2[name][0], "adamw_" + name, rb)
        results[name] = [a.reshape(shape) for a in outs]
    names = [name for name, _ in REPLICATED]
    zeros = jnp.zeros((CONV_K, 3 * DN_WIDTH), F32)
    small_outs = _reduce_adamw(small_received, *[_pack_rows([zeros] + [src[k] for k in names], SMALL_ROWS, F32)
                                                 for src in (weights, mom1, mom2)], "adamw_replicated")
    layout = (SMALL_LAYOUT[0],) + tuple((k, weights[k].shape) for k in names) + (SMALL_LAYOUT[-1],)
    unpacked = [_unpack(a, layout) for a in small_outs]
    for k in names:
        results[k] = [u[k] for u in unpacked]
    loss = unpacked[0]["loss"][0]
    me = 4 * lax.axis_index("x") + 2 * lax.axis_index("y") + lax.axis_index("c")
    conv_mine = lax.dynamic_slice(unpacked[0]["conv"], (0, me * 192), (CONV_K, 192))
    outs = _reduce_adamw(conv_mine[None], dn_conv_w[0], m_dn_conv_w[0], v_dn_conv_w[0], "adamw_dn_conv_w")
    results["dn_conv_w"] = [a.reshape(dn_conv_w.shape) for a in outs]

    return (loss, grad_x.reshape(nb, s, D_MODEL), *[results[k][0] for k in WEIGHT_ORDER],
            *[results[k][1] for k in WEIGHT_ORDER], *[results[k][2] for k in WEIGHT_ORDER],
            *[results[k][3] for k in WEIGHT_ORDER])
```

```python
import jax
import jax.numpy as jnp
from jax import lax
from jax.experimental import pallas as pl
from jax.experimental.pallas import tpu as pltpu

F32 = jnp.float32
BF16 = jnp.bfloat16

N_DEV = 8
D_MODEL = 1024
SGU_WIDTH = 512
SGU_GROUPS = 4
SGU_CHUNK = 128
DN_WIDTH = 512
DN_HEADS = 4
DN_HEAD_DIM = 128
DN_CHUNK = 128
CONV_K = 4
CONV_HALO = 8
PLE_DIM = 256
EPS = 1e-6
IN_COLS = 3592
IN_SHARD = IN_COLS // N_DEV
GATE_PAD = 128

ADAM_LR = 0.001
ADAM_B1 = 0.9
ADAM_B2 = 0.999
ADAM_EPS = 1e-08
ADAM_WD = 0.01
ADAM_STEP = 10

LANES = 128
VMEM_LIMIT = 56 * 1024 * 1024
MESH = pl.DeviceIdType.MESH

REPLICATED = (("norm_g", (1, D_MODEL)), ("sgu_ln_g", (1, SGU_WIDTH)), ("sgu_ln_b", (1, SGU_WIDTH)),
              ("sgu_w_s", (1, SGU_GROUPS, SGU_CHUNK, SGU_CHUNK)), ("sgu_b_s", (1, SGU_GROUPS, SGU_CHUNK)),
              ("dn_a_log", (1, DN_HEADS)), ("dn_dt_bias", (1, DN_HEADS)), ("dn_o_norm_g", (1, DN_HEAD_DIM)),
              ("ple_norm_g", (1, D_MODEL)), ("final_norm_g", (D_MODEL,)))
WEIGHT_ORDER = ("norm_g", "w_in", "sgu_ln_g", "sgu_ln_b", "sgu_w_s", "sgu_b_s", "dn_conv_w", "dn_a_log",
                "dn_dt_bias", "dn_o_norm_g", "w_out", "ple_norm_g", "ple_gate_w", "ple_proj_w", "final_norm_g")


def _size(shape):
    n = 1
    for s in shape:
        n *= s
    return n


SMALL_LAYOUT = (("conv", (CONV_K, 3 * DN_WIDTH)),) + REPLICATED + (("loss", (1,)),)
SMALL_PIECE_ROWS = tuple(-(-_size(s) // LANES) for _, s in SMALL_LAYOUT)
SMALL_ROWS = -(-sum(SMALL_PIECE_ROWS) // 8) * 8


def _bdot(a, b):
    return jnp.dot(a.astype(BF16), b.astype(BF16), preferred_element_type=F32)


def _sigmoid(x):
    return pl.reciprocal(1.0 + jnp.exp(-x), approx=True)


@jax.custom_vjp
def _silu(x):
    return x * _sigmoid(x)


def _silu_fwd(x):
    s = _sigmoid(x)
    return x * s, (x, s)


def _silu_bwd(res, ct):
    x, s = res
    return (ct * (s * (1.0 + x * (1.0 - s))),)


_silu.defvjp(_silu_fwd, _silu_bwd)


def _gelu(x):
    return 0.5 * x * (1.0 + lax.erf(x * (0.5 ** 0.5)))


def _softplus(x):
    return jnp.maximum(x, 0.0) + jnp.log1p(jnp.exp(-jnp.abs(x)))


def _l2n(x):
    return x * lax.rsqrt(jnp.sum(x * x, axis=-1, keepdims=True) + EPS)


def _rms(x):
    r = lax.rsqrt(jnp.mean(x * x, axis=-1, keepdims=True) + EPS)
    return x * r, r


def _rms_bwd(dn, n, r):
    return r * (dn - n * jnp.mean(dn * n, axis=-1, keepdims=True))


def _onehot_row(idx, width):
    return (lax.broadcasted_iota(jnp.int32, (1, width), 1) == idx).astype(F32)


def _rowsum(x):
    return jnp.sum(x, axis=0, keepdims=True)


def _iota2(n):
    return lax.broadcasted_iota(jnp.int32, (n, n), 0), lax.broadcasted_iota(jnp.int32, (n, n), 1)


def _bmm(a, b):
    return lax.dot_general(a.astype(BF16), b.astype(BF16), (((2,), (1,)), ((0,), (0,))), preferred_element_type=F32)


def _bmm_nt(a, b):
    return lax.dot_general(a.astype(BF16), b.astype(BF16), (((2,), (2,)), ((0,), (0,))), preferred_element_type=F32)


def _bmm_tn(a, b):
    return lax.dot_general(a.astype(BF16), b.astype(BF16), (((1,), (1,)), ((0,), (0,))), preferred_element_type=F32)


def _tri_inv_impl(a):
    n = a.shape[-1]
    r, c = _iota2(n)
    x = r ^ c
    eye = (r == c).astype(F32)
    ad = jnp.where(x < 16, a, 0.0)
    p2 = _bmm(ad, ad)
    e = p2 - ad - _bmm(ad, p2)
    p4 = _bmm(p2, p2)
    e = e + p4 + _bmm(e, p4)
    p8 = _bmm(p4, p4)
    e = e + p8 + _bmm(e, p8)
    size = 16
    while size < n:
        m = jnp.where(jnp.logical_and(x < 2 * size, x >= size), a, 0.0)
        f = m + _bmm(m, e)
        e = e - f - _bmm(e, f)
        size *= 2
    return e + eye


@jax.custom_vjp
def _tri_inv(a):
    return _tri_inv_impl(a)


def _tri_inv_fwd(a):
    t = _tri_inv_impl(a)
    return t, t


def _tri_inv_bwd(t, dt):
    return (-_bmm_tn(t, _bmm_nt(dt, t)),)


_tri_inv.defvjp(_tri_inv_fwd, _tri_inv_bwd)


def _sgu_core(u, v, z, lg, lb, ws, bcol):
    n = ws.shape[0]
    r, c = _iota2(n)
    wm = jnp.where(r >= c, ws, 0.0)
    gu = _gelu(u)
    gv = _gelu(v)
    xc = gv - jnp.mean(gv, axis=-1, keepdims=True)
    ln = xc * lax.rsqrt(jnp.mean(xc * xc, axis=-1, keepdims=True) + EPS) * lg + lb
    s = _bdot(wm, ln) + bcol
    return gu * s * _silu(z)


def _lanes_of(x):
    return jnp.concatenate([x[i] for i in range(x.shape[0])], axis=1)


def _batch_of(x, width):
    return jnp.concatenate([x[None, :, i * width:(i + 1) * width] for i in range(x.shape[1] // width)], axis=0)


def _mask_dot(mask, x):
    hi = x.astype(BF16)
    lo = (x - hi.astype(F32)).astype(BF16)
    m = mask.astype(BF16)
    return jnp.dot(m, hi, preferred_element_type=F32) + jnp.dot(m, lo, preferred_element_type=F32)


def _tri_mask(n, upper):
    r, c = _iota2(n)
    return (r <= c) if upper else (r >= c)


@jax.custom_vjp
def _cumsum_rows(x):
    return _mask_dot(_tri_mask(x.shape[0], False), x)


def _cumsum_rows_fwd(x):
    return _cumsum_rows(x), None


def _cumsum_rows_bwd(_, ct):
    return (_mask_dot(_tri_mask(ct.shape[0], True), ct),)


_cumsum_rows.defvjp(_cumsum_rows_fwd, _cumsum_rows_bwd)


@jax.custom_vjp
def _colsum_all_rows(x):
    return _mask_dot(jnp.ones((x.shape[0], x.shape[0]), jnp.bool_), x)


def _colsum_all_rows_fwd(x):
    return _colsum_all_rows(x), None


def _colsum_all_rows_bwd(_, ct):
    return (_mask_dot(jnp.ones((ct.shape[0], ct.shape[0]), jnp.bool_), ct),)


_colsum_all_rows.defvjp(_colsum_all_rows_fwd, _colsum_all_rows_bwd)


def _dn_core(cq, ck, cv, z, logits, state, alog, dtb, og):
    gn, cn, dh = cq.shape
    heads = gn // logits.shape[0]
    q = _l2n(_silu(cq)) * (dh ** -0.5)
    k = _l2n(_silu(ck))
    v = _silu(cv)
    beta_lanes = jax.nn.sigmoid(logits)
    g_lanes = -jnp.exp(alog) * _softplus(logits + dtb)
    column = lambda rows, lane: jnp.sum(rows * _onehot_row(lane, rows.shape[-1]), axis=-1, keepdims=True)[None]
    beta = jnp.concatenate([column(beta_lanes[i // heads], i % heads) for i in range(gn)], axis=0)
    g = jnp.concatenate([column(g_lanes[i // heads], heads + i % heads) for i in range(gn)], axis=0)
    r, c = _iota2(cn)
    tril = r >= c
    rw = lax.broadcasted_iota(jnp.int32, (cn, dh), 0)
    cw = lax.broadcasted_iota(jnp.int32, (cn, dh), 1)
    upper_wide = (rw <= cw).astype(F32)
    g_wide = jnp.broadcast_to(g, (gn, cn, dh))
    gc_wide = _batch_of(_cumsum_rows(_lanes_of(g_wide)), dh)
    gc_cols = _batch_of(_colsum_all_rows(_lanes_of(g_wide * upper_wide)), dh)[:, :, :cn]
    decay = jnp.exp(jnp.where(tril, gc_wide[:, :, :cn] - gc_cols, -1e30))
    kb = k * beta
    kk = _bmm_nt(kb, k) * decay
    t = _tri_inv(jnp.where(r > c, kk, 0.0))
    eg = jnp.exp(gc_wide)
    sol = _bmm(t, jnp.concatenate([v * beta, kb * eg], axis=-1))
    u_val, w_dec = sol[:, :, :dh], sol[:, :, dh:]
    qk = _bmm_nt(q, k) * decay
    g_last = jnp.sum(g_wide, axis=1, keepdims=True)
    k_dec = k * jnp.exp(g_last - gc_wide)
    ws = _bmm(jnp.concatenate([w_dec, q * eg], axis=1), state)
    v_new = u_val - ws[:, :cn]
    o = ws[:, cn:] + _bmm(qk, v_new)
    new_state = state * jnp.exp(g_last) + _bmm_tn(k_dec, v_new)
    on, _ = _rms(o)
    return on * og * _silu(z), new_state


N_CHIPS = 4
HBM_SPEC = pl.BlockSpec(memory_space=pl.ANY)


def _place():
    return lax.axis_index("x"), lax.axis_index("y"), lax.axis_index("c")


def _other_chip(k):
    x, y, _ = _place()
    px = 1 - x if k & 2 else x
    py = 1 - y if k & 1 else y
    return px, py, 2 * px + py


def _remote(src, dst, send_sem, recv_sem, device):
    return pltpu.make_async_remote_copy(src_ref=src, dst_ref=dst, send_sem=send_sem, recv_sem=recv_sem,
                                        device_id=device, device_id_type=MESH)


def _other_device(k):
    x, y, c = _place()
    px = 1 - x if k & 4 else x
    py = 1 - y if k & 2 else y
    pc = 1 - c if k & 1 else c
    return (px, py, pc), 4 * px + 2 * py + pc


def _direct_exchange(srcs, outs, send_sems, recv_sems, local_sems, gather):
    x, y, c = _place()
    me = 4 * x + 2 * y + c

    def copies(arriving):
        out_list = []
        for a, (src, out) in enumerate(zip(srcs, outs)):
            for k in range(1, N_DEV):
                peer, index = _other_device(k)
                mine = src if gather else src.at[index]
                out_list.append(_remote(mine, out.at[index if arriving else me], send_sems.at[a, k - 1],
                                        recv_sems.at[a, k - 1], peer))
        return out_list

    def local_copies():
        return [pltpu.make_async_copy(src if gather else src.at[me], out.at[me], local_sems.at[a])
                for a, (src, out) in enumerate(zip(srcs, outs))]

    def start():
        for cp in local_copies() + copies(False):
            cp.start()

    def wait():
        for cp in copies(True):
            cp.wait_recv()
        for cp in copies(False):
            cp.wait_send()
        for cp in local_copies():
            cp.wait()

    return start, wait


def _exchange_scratch(n):
    return [pltpu.SemaphoreType.DMA((n, N_DEV - 1)), pltpu.SemaphoreType.DMA((n, N_DEV - 1)), pltpu.SemaphoreType.DMA((n,))]


def _all_gather(shards):
    n = len(shards)

    def body(*refs):
        srcs, outs = refs[:n], refs[n:2 * n]
        send_sems, recv_sems, local_sems = refs[2 * n:]
        x, y, c = _place()
        me = 4 * x + 2 * y + c
        sibling = (x, y, 1 - c)
        local = [pltpu.make_async_copy(srcs[a], outs[a].at[me], local_sems.at[a]) for a in range(n)]
        for cp in local:
            cp.start()
        sends = []
        for a in range(n):
            sends.append(_remote(srcs[a], outs[a].at[me], send_sems.at[a, 0], recv_sems.at[a, 0], sibling))
        for k in range(1, N_CHIPS):
            px, py, _ = _other_chip(k)
            for a in range(n):
                sends.append(_remote(srcs[a], outs[a].at[me], send_sems.at[a, k], recv_sems.at[a, k], (px, py, c)))
        for cp in sends:
            cp.start()
        passed = []
        for k in range(1, N_CHIPS):
            px, py, _ = _other_chip(k)
            blk = 4 * px + 2 * py + c
            for a in range(n):
                _remote(srcs[a], outs[a].at[blk], send_sems.at[a, k], recv_sems.at[a, k], (px, py, c)).wait_recv()
            for a in range(n):
                cp = _remote(outs[a].at[blk], outs[a].at[blk], send_sems.at[a, 3 + k], recv_sems.at[a, 3 + k], sibling)
                cp.start()
                passed.append(cp)
        for a in range(n):
            _remote(srcs[a], outs[a].at[me + 1 - 2 * c], send_sems.at[a, 0], recv_sems.at[a, 0], sibling).wait_recv()
        for k in range(1, N_CHIPS):
            px, py, _ = _other_chip(k)
            blk = 4 * px + 2 * py + 1 - c
            for a in range(n):
                _remote(srcs[a], outs[a].at[blk], send_sems.at[a, 3 + k], recv_sems.at[a, 3 + k], sibling).wait_recv()
        for cp in sends + passed:
            cp.wait_send()
        for cp in local:
            cp.wait()

    return pl.pallas_call(
        body, name="all_gather_weights",
        out_shape=tuple(jax.ShapeDtypeStruct((N_DEV,) + a.shape, a.dtype) for a in shards),
        in_specs=[HBM_SPEC] * n, out_specs=(HBM_SPEC,) * n,
        scratch_shapes=[pltpu.SemaphoreType.DMA((n, N_DEV - 1)), pltpu.SemaphoreType.DMA((n, N_DEV - 1)),
                        pltpu.SemaphoreType.DMA((n,))],
    )(*shards)


def _sibling_exchange(by_device, small):
    n = len(by_device)

    def body(*refs):
        srcs, small_src = refs[:n], refs[n]
        outs, small_out = refs[n + 1:2 * n + 1], refs[2 * n + 1]
        send_sems, recv_sems = refs[2 * n + 2:]
        x, y, c = _place()
        sibling = (x, y, 1 - c)
        copies = [_remote(small_src, small_out, send_sems.at[n, 0], recv_sems.at[n, 0], sibling)]
        for a in range(n):
            for q in range(N_CHIPS):
                copies.append(_remote(srcs[a].at[2 * q + 1 - c], outs[a].at[q], send_sems.at[a, q], recv_sems.at[a, q],
                                      sibling))
        for cp in copies:
            cp.start()
        for cp in copies:
            cp.wait_recv()
        for cp in copies:
            cp.wait_send()

    return pl.pallas_call(
        body, name="grad_sibling_exchange",
        out_shape=tuple(jax.ShapeDtypeStruct((N_CHIPS,) + a.shape[1:], a.dtype) for a in by_device)
        + (jax.ShapeDtypeStruct(small.shape, small.dtype),),
        in_specs=[HBM_SPEC] * (n + 1), out_specs=(HBM_SPEC,) * (n + 1),
        scratch_shapes=[pltpu.SemaphoreType.DMA((n + 1, N_CHIPS)), pltpu.SemaphoreType.DMA((n + 1, N_CHIPS))],
    )(*by_device, small)


def _chip_exchange(chip_sums, small):
    n = len(chip_sums)

    def body(*refs):
        srcs, small_src = refs[:n], refs[n]
        outs, small_out = refs[n + 1:2 * n + 1], refs[2 * n + 1]
        send_sems, recv_sems, local_sems = refs[2 * n + 2:]
        x, y, c = _place()
        mine = 2 * x + y
        local = [pltpu.make_async_copy(srcs[a].at[mine], outs[a].at[mine], local_sems.at[a]) for a in range(n)]
        local.append(pltpu.make_async_copy(small_src, small_out.at[mine], local_sems.at[n]))
        for cp in local:
            cp.start()
        sends = []
        for k in range(1, N_CHIPS):
            px, py, chip = _other_chip(k)
            for a in range(n):
                sends.append(_remote(srcs[a].at[chip], outs[a].at[mine], send_sems.at[a, k - 1], recv_sems.at[a, k - 1],
                                     (px, py, c)))
            sends.append(_remote(small_src, small_out.at[mine], send_sems.at[n, k - 1], recv_sems.at[n, k - 1], (px, py, c)))
        for cp in sends:
            cp.start()
        for k in range(1, N_CHIPS):
            px, py, chip = _other_chip(k)
            for a in range(n):
                _remote(srcs[a].at[chip], outs[a].at[chip], send_sems.at[a, k - 1], recv_sems.at[a, k - 1],
                        (px, py, c)).wait_recv()
            _remote(small_src, small_out.at[chip], send_sems.at[n, k - 1], recv_sems.at[n, k - 1], (px, py, c)).wait_recv()
        for cp in sends:
            cp.wait_send()
        for cp in local:
            cp.wait()

    return pl.pallas_call(
        body, name="grad_chip_exchange",
        out_shape=tuple(jax.ShapeDtypeStruct(a.shape, a.dtype) for a in chip_sums)
        + (jax.ShapeDtypeStruct((N_CHIPS,) + small.shape, small.dtype),),
        in_specs=[HBM_SPEC] * (n + 1), out_specs=(HBM_SPEC,) * (n + 1),
        scratch_shapes=[pltpu.SemaphoreType.DMA((n + 1, N_CHIPS - 1)), pltpu.SemaphoreType.DMA((n + 1, N_CHIPS - 1)),
                        pltpu.SemaphoreType.DMA((n + 1,))],
    )(*chip_sums, small)


def _pair_sum(core, by_device, from_sibling, small, small_from_sibling):
    n = len(by_device)

    def body(core_ref, *refs):
        own, sib = refs[:n], refs[n:2 * n]
        small_own, small_sib = refs[2 * n], refs[2 * n + 1]
        outs, small_out = refs[2 * n + 2:3 * n + 2], refs[3 * n + 2]
        for a in range(n):
            outs[a][...] = (own[a][...] + sib[a][...]).astype(outs[a].dtype)
        small_out[...] = small_own[...] + small_sib[...]

    def block(a):
        return (None,) + a.shape[1:], (0,) * (a.ndim - 1)

    own_specs = [pl.BlockSpec(block(a)[0], lambda q, core_ref, z=block(a)[1]: (2 * q + core_ref[0],) + z) for a in by_device]
    sib_specs = [pl.BlockSpec(block(a)[0], lambda q, core_ref, z=block(a)[1]: (q,) + z) for a in by_device]
    small_spec = pl.BlockSpec(small.shape, lambda q, core_ref: (0,) * small.ndim)
    return pl.pallas_call(
        body, name="grad_pair_sum",
        grid_spec=pltpu.PrefetchScalarGridSpec(
            num_scalar_prefetch=1, grid=(N_CHIPS,),
            in_specs=own_specs + sib_specs + [small_spec, small_spec],
            out_specs=tuple(sib_specs) + (small_spec,)),
        out_shape=tuple(jax.ShapeDtypeStruct(a.shape, BF16) for a in from_sibling)
        + (jax.ShapeDtypeStruct(small.shape, F32),),
        compiler_params=_params(1),
    )(core, *by_device, *from_sibling, small, small_from_sibling)


def _params(n_axes):
    return pltpu.CompilerParams(dimension_semantics=("arbitrary",) * n_axes, vmem_limit_bytes=VMEM_LIMIT)


def _whole(shape):
    return pl.BlockSpec(shape, lambda *_: (0,) * len(shape))


VMEM_SPEC = pl.BlockSpec(memory_space=pltpu.VMEM)


def _inproj_fwd(x2, seq_len, norm_g, wa, wq, wz, wg, sgu_weights, conv_w, later_shards):
    t = x2.shape[0]
    tm = min(512, seq_len)
    tiles_per_seq = seq_len // tm
    steps = t // tm
    ns = len(later_shards)

    def body(x_ref, g_ref, wa_ref, wq_ref, wz_ref, wg_ref, lg_ref, lb_ref, ws_ref, bt_ref, cw_ref, *rest):
        shard_refs, rest = rest[:ns], rest[ns:]
        a_ref, q_ref, z_ref, l_ref, sgu_ref, c_ref = rest[:6]
        gathered_refs, (xpad_ref, send_sems, recv_sems, local_sems) = rest[6:6 + ns], rest[6 + ns:]
        start_gather, wait_gather = _direct_exchange(shard_refs, gathered_refs, send_sems, recv_sems, local_sems, True)
        pl.when(pl.program_id(0) == 0)(start_gather)
        n, _ = _rms(x_ref[...])
        xn = (n * g_ref[...]).astype(BF16)
        for w_ref, o_ref in ((wa_ref, a_ref), (wq_ref, q_ref), (wz_ref, z_ref), (wg_ref, l_ref)):
            width = w_ref.shape[1]
            for c0 in range(0, width, 512):
                c1 = min(c0 + 512, width)
                o_ref[:, c0:c1] = jnp.dot(xn, w_ref[:, c0:c1], preferred_element_type=F32)
        for row0 in range(0, tm, SGU_CHUNK):
            for grp in range(SGU_GROUPS):
                args = _sgu_pieces(a_ref, lg_ref, lb_ref, ws_ref, bt_ref, row0, grp)
                sgu_ref[pl.ds(row0, SGU_CHUNK), pl.ds(grp * 128, 128)] = _sgu_core(*args).astype(sgu_ref.dtype)

        @pl.when(pl.program_id(0) % tiles_per_seq == 0)
        def _():
            xpad_ref[0:CONV_HALO, :] = jnp.zeros((CONV_HALO, xpad_ref.shape[1]), F32)

        xpad_ref[CONV_HALO:, :] = q_ref[...]
        acc = None
        for j in range(CONV_K):
            term = cw_ref[j:j + 1, :] * xpad_ref[pl.ds(CONV_HALO - CONV_K + 1 + j, tm), :]
            acc = term if acc is None else acc + term
        c_ref[...] = acc
        xpad_ref[0:CONV_HALO, :] = xpad_ref[tm:tm + CONV_HALO, :]
        pl.when(pl.program_id(0) == steps - 1)(wait_gather)

    widths = (wa.shape[1], wq.shape[1], wz.shape[1], wg.shape[1])
    tile = lambda w: pl.BlockSpec((tm, w), lambda i: (i, 0))
    sgu_shapes = ((1, SGU_WIDTH), (1, SGU_WIDTH), (SGU_GROUPS, SGU_CHUNK, SGU_CHUNK), (SGU_CHUNK, SGU_GROUPS))
    return pl.pallas_call(
        body, name="inproj_sgu_conv_fwd", grid=(steps,),
        out_shape=tuple(jax.ShapeDtypeStruct((t, w), F32) for w in widths)
        + (jax.ShapeDtypeStruct((t, SGU_WIDTH), BF16), jax.ShapeDtypeStruct((t, widths[1]), F32))
        + tuple(jax.ShapeDtypeStruct((N_DEV,) + a.shape, a.dtype) for a in later_shards),
        in_specs=[tile(D_MODEL), _whole((1, D_MODEL)), VMEM_SPEC, VMEM_SPEC, VMEM_SPEC, VMEM_SPEC]
        + [_whole(s) for s in sgu_shapes] + [_whole((CONV_K, widths[1]))] + [HBM_SPEC] * ns,
        out_specs=tuple(tile(w) for w in widths) + (tile(SGU_WIDTH), tile(widths[1])) + (HBM_SPEC,) * ns,
        scratch_shapes=[pltpu.VMEM((CONV_HALO + tm, widths[1]), F32)] + _exchange_scratch(ns),
        compiler_params=_params(1),
    )(x2, norm_g, wa, wq, wz, wg, *sgu_weights, conv_w, *later_shards)


def _sgu_pieces(uvz_ref, lg_ref, lb_ref, ws_ref, bt_ref, row0, grp):
    rows = pl.ds(row0, SGU_CHUNK)
    lanes = pl.ds(grp * 128, 128)
    u = uvz_ref[rows, pl.ds(grp * 128, 128)]
    v = uvz_ref[rows, pl.ds(SGU_WIDTH + grp * 128, 128)]
    z = uvz_ref[rows, pl.ds(2 * SGU_WIDTH + grp * 128, 128)]
    bcol = jnp.sum(bt_ref[...] * _onehot_row(grp, SGU_GROUPS), axis=-1, keepdims=True)
    return u, v, z, lg_ref[:, lanes], lb_ref[:, lanes], ws_ref[grp], bcol


def _sgu_bwd_tile(uvz_ref, do_ref, sgu_refs, duvz_ref, grad_refs):
    lg_ref, lb_ref, ws_ref, bt_ref = sgu_refs
    dlg_ref, dlb_ref, dws_ref, dbt_ref = grad_refs
    for row0 in range(0, uvz_ref.shape[0], SGU_CHUNK):
        rows = pl.ds(row0, SGU_CHUNK)
        for grp in range(SGU_GROUPS):
            lanes = pl.ds(grp * 128, 128)
            args = _sgu_pieces(uvz_ref, lg_ref, lb_ref, ws_ref, bt_ref, row0, grp)
            _, pull = jax.vjp(_sgu_core, *args)
            du, dv, dz, dlg, dlb, dws, dbcol = pull(do_ref[rows, lanes])
            duvz_ref[rows, pl.ds(grp * 128, 128)] = du.astype(duvz_ref.dtype)
            duvz_ref[rows, pl.ds(SGU_WIDTH + grp * 128, 128)] = dv.astype(duvz_ref.dtype)
            duvz_ref[rows, pl.ds(2 * SGU_WIDTH + grp * 128, 128)] = dz.astype(duvz_ref.dtype)
            dlg_ref[:, lanes] += dlg
            dlb_ref[:, lanes] += dlb
            dws_ref[grp] += dws
            dbt_ref[...] += dbcol * _onehot_row(grp, SGU_GROUPS)


def _dn_pairs(nb):
    return [(b, h) for b in range(nb) for h in range(DN_HEADS)]


def _dn_batch_args(c_ref, z_ref):
    pairs = _dn_pairs(c_ref.shape[0])
    pick = lambda ref, b, col: ref[b, :, pl.ds(col, DN_HEAD_DIM)]
    cq = jnp.stack([pick(c_ref, b, h * DN_HEAD_DIM) for b, h in pairs])
    ck = jnp.stack([pick(c_ref, b, DN_WIDTH + h * DN_HEAD_DIM) for b, h in pairs])
    cv = jnp.stack([pick(c_ref, b, 2 * DN_WIDTH + h * DN_HEAD_DIM) for b, h in pairs])
    z = jnp.stack([pick(z_ref, b, h * DN_HEAD_DIM) for b, h in pairs])
    return cq, ck, cv, z


def _dn_weight_specs():
    return [_whole((CONV_K, 3 * DN_WIDTH)), _whole((1, GATE_PAD)), _whole((1, GATE_PAD)), _whole((1, DN_HEAD_DIM))]


def _dn_fwd(conv_out, zg, logits, alog, dtb, og):
    nb, s, _ = conv_out.shape
    nc = s // DN_CHUNK
    pairs = _dn_pairs(nb)
    gn = len(pairs)
    chunk = lambda w: pl.BlockSpec((nb, DN_CHUNK, w), lambda n: (0, n, 0))

    def body(c_ref, z_ref, l_ref, alog_ref, dtb_ref, og_ref, out_ref, st_ref, state_ref):
        n = pl.program_id(0)

        @pl.when(n == 0)
        def _():
            state_ref[...] = jnp.zeros_like(state_ref)

        cq, ck, cv, z = _dn_batch_args(c_ref, z_ref)
        state = state_ref[...]
        st_ref[...] = state
        out, new_state = _dn_core(cq, ck, cv, z, l_ref[...], state, alog_ref[...], dtb_ref[...], og_ref[...])
        state_ref[...] = new_state
        for i, (b, h) in enumerate(pairs):
            out_ref[b, :, pl.ds(h * DN_HEAD_DIM, DN_HEAD_DIM)] = out[i].astype(out_ref.dtype)

    return pl.pallas_call(
        body, name="deltanet_fwd", grid=(nc,),
        out_shape=(jax.ShapeDtypeStruct((nb, s, DN_WIDTH), BF16),
                   jax.ShapeDtypeStruct((nc, gn, DN_HEAD_DIM, DN_HEAD_DIM), F32)),
        in_specs=[chunk(3 * DN_WIDTH), chunk(DN_WIDTH), chunk(GATE_PAD)] + _dn_weight_specs()[1:],
        out_specs=(chunk(DN_WIDTH), pl.BlockSpec((None, gn, DN_HEAD_DIM, DN_HEAD_DIM), lambda n: (n, 0, 0, 0))),
        scratch_shapes=[pltpu.VMEM((gn, DN_HEAD_DIM, DN_HEAD_DIM), F32)],
        compiler_params=_params(1),
    )(conv_out, zg, logits, alog, dtb, og)


def _dn_bwd(qkv, conv_out, zg, logits, conv_w, alog, dtb, og, states, d_out, head_grads):
    nb, s, _ = qkv.shape
    nc = s // DN_CHUNK
    rev = lambda n: nc - 1 - n
    pairs = _dn_pairs(nb)
    gn = len(pairs)
    ng = len(head_grads)

    def body(cur_ref, c_ref, z_ref, l_ref, w_ref, alog_ref, dtb_ref, og_ref, st_ref, do_ref, *rest):
        grad_refs, rest = rest[:ng], rest[ng:]
        dqkv_ref, dz_ref, dl_ref, dw_ref, dalog_ref, ddtb_ref, dog_ref = rest[:7]
        recv_refs, (dstate_ref, dcpad_ref, send_sems, recv_sems, local_sems) = rest[7:7 + ng], rest[7 + ng:]
        n = pl.program_id(0)
        start_exchange, wait_exchange = _direct_exchange(grad_refs, recv_refs, send_sems, recv_sems, local_sems, False)
        pl.when(n == 0)(start_exchange)

        @pl.when(n == 0)
        def _():
            dw_ref[...] = jnp.zeros_like(dw_ref)
            dalog_ref[...] = jnp.zeros_like(dalog_ref)
            ddtb_ref[...] = jnp.zeros_like(ddtb_ref)
            dog_ref[...] = jnp.zeros_like(dog_ref)
            dstate_ref[...] = jnp.zeros_like(dstate_ref)
            dcpad_ref[:, DN_CHUNK:, :] = jnp.zeros((nb, CONV_HALO, 3 * DN_WIDTH), F32)

        cq, ck, cv, z = _dn_batch_args(c_ref, z_ref)
        d_out_g = jnp.stack([do_ref[b, :, pl.ds(h * DN_HEAD_DIM, DN_HEAD_DIM)] for b, h in pairs])
        _, pull = jax.vjp(_dn_core, cq, ck, cv, z, l_ref[...], st_ref[...], alog_ref[...], dtb_ref[...], og_ref[...])
        dcq, dck, dcv, dz, dlog, dstate, dalog, ddtb, dog = pull((d_out_g, dstate_ref[...]))
        dstate_ref[...] = dstate
        dl_ref[...] = dlog.astype(dl_ref.dtype)
        dalog_ref[...] += dalog
        ddtb_ref[...] += ddtb
        dog_ref[...] += dog
        for i, (b, h) in enumerate(pairs):
            dcpad_ref[b, 0:DN_CHUNK, pl.ds(h * DN_HEAD_DIM, DN_HEAD_DIM)] = dcq[i]
            dcpad_ref[b, 0:DN_CHUNK, pl.ds(DN_WIDTH + h * DN_HEAD_DIM, DN_HEAD_DIM)] = dck[i]
            dcpad_ref[b, 0:DN_CHUNK, pl.ds(2 * DN_WIDTH + h * DN_HEAD_DIM, DN_HEAD_DIM)] = dcv[i]
            dz_ref[b, :, pl.ds(h * DN_HEAD_DIM, DN_HEAD_DIM)] = dz[i].astype(dz_ref.dtype)
        for b in range(nb):
            xb = cur_ref[b]
            dx = None
            for j in range(CONV_K):
                shifted = dcpad_ref[b, pl.ds(CONV_K - 1 - j, DN_CHUNK), :]
                term = w_ref[j:j + 1, :] * shifted
                dx = term if dx is None else dx + term
                dw_ref[j:j + 1, :] += _rowsum(shifted * xb)
            dqkv_ref[b] = dx.astype(dqkv_ref.dtype)
            dcpad_ref[b, DN_CHUNK:, :] = dcpad_ref[b, 0:CONV_HALO, :]
        pl.when(n == nc - 1)(wait_exchange)

    chunk = lambda w: pl.BlockSpec((nb, DN_CHUNK, w), lambda n: (0, rev(n), 0))
    return pl.pallas_call(
        body, name="deltanet_bwd", grid=(nc,),
        out_shape=(jax.ShapeDtypeStruct((nb, s, 3 * DN_WIDTH), BF16), jax.ShapeDtypeStruct((nb, s, DN_WIDTH), BF16),
                   jax.ShapeDtypeStruct((nb, s, GATE_PAD), BF16), jax.ShapeDtypeStruct((CONV_K, 3 * DN_WIDTH), F32),
                   jax.ShapeDtypeStruct((1, GATE_PAD), F32), jax.ShapeDtypeStruct((1, GATE_PAD), F32),
                   jax.ShapeDtypeStruct((1, DN_HEAD_DIM), F32))
        + tuple(jax.ShapeDtypeStruct(a.shape, a.dtype) for a in head_grads),
        in_specs=[chunk(3 * DN_WIDTH), chunk(3 * DN_WIDTH), chunk(DN_WIDTH), chunk(GATE_PAD)] + _dn_weight_specs() + [
            pl.BlockSpec((None, gn, DN_HEAD_DIM, DN_HEAD_DIM), lambda n: (rev(n), 0, 0, 0)),
            chunk(DN_WIDTH)] + [HBM_SPEC] * ng,
        out_specs=(chunk(3 * DN_WIDTH), chunk(DN_WIDTH), chunk(GATE_PAD), _whole((CONV_K, 3 * DN_WIDTH)),
                   _whole((1, GATE_PAD)), _whole((1, GATE_PAD)), _whole((1, DN_HEAD_DIM))) + (HBM_SPEC,) * ng,
        scratch_shapes=[pltpu.VMEM((gn, DN_HEAD_DIM, DN_HEAD_DIM), F32),
                        pltpu.VMEM((nb, DN_CHUNK + CONV_HALO, 3 * DN_WIDTH), F32)] + _exchange_scratch(ng),
        compiler_params=_params(1),
    )(qkv, conv_out, zg, logits, conv_w, alog, dtb, og, states, d_out, *head_grads)


def _head(a_out, b_out, x2, p2, target, w_out, w_out_t, w_gate, w_gate_t, w_proj, ple_g, fin_g):
    t = x2.shape[0]
    tm = min(512, t)
    steps = t // tm

    def body(a_ref, b_ref, x_ref, p_ref, y_ref, wo_ref, wot_ref, wg_ref, wgt_ref, wp_ref, pg_ref, fg_ref,
             da_ref, db_ref, dh_ref, dwo_hbm, dwg_hbm, dwp_hbm, dpg_ref, dfg_ref, loss_ref,
             dwo_acc, dwg_acc, dwp_acc, rows_stage, cols_stage):
        i = pl.program_id(0)

        @pl.when(i == 0)
        def _():
            dwo_acc[...] = jnp.zeros_like(dwo_acc)
            dwg_acc[...] = jnp.zeros_like(dwg_acc)
            dwp_acc[...] = jnp.zeros_like(dwp_acc)
            dpg_ref[...] = jnp.zeros_like(dpg_ref)
            dfg_ref[...] = jnp.zeros_like(dfg_ref)
            loss_ref[...] = jnp.zeros_like(loss_ref)

        a = a_ref[...]
        bb = b_ref[...]
        pb = p_ref[...].astype(BF16)
        pg = pg_ref[...]
        fg = fg_ref[...]
        h1 = (x_ref[...] + jnp.dot(a, wo_ref[0:SGU_WIDTH, :], preferred_element_type=F32)
              + jnp.dot(bb, wo_ref[SGU_WIDTH:, :], preferred_element_type=F32))
        n1, r1 = _rms(h1)
        rn = (n1 * pg).astype(BF16)
        gate = jax.nn.sigmoid(jnp.dot(rn, wg_ref[...], preferred_element_type=F32))
        pp = jnp.dot(pb, wp_ref[...], preferred_element_type=F32)
        h2 = h1 + gate * pp
        n2, r2 = _rms(h2)
        err = n2 * fg - y_ref[...]
        loss_ref[...] += jnp.broadcast_to(_rowsum(jnp.sum(err * err, axis=-1, keepdims=True)), loss_ref.shape)

        dy = err * (1.0 / D_MODEL)
        dfg_ref[...] += _rowsum(dy * n2)
        dh2 = _rms_bwd(dy * fg, n2, r2)
        dpp = (dh2 * gate).astype(BF16)
        dgl = (dh2 * pp * gate * (1.0 - gate)).astype(BF16)
        dwp_acc[...] += lax.dot_general(pb, dpp, (((0,), (0,)), ((), ())), preferred_element_type=F32)
        dwg_acc[...] += lax.dot_general(rn, dgl, (((0,), (0,)), ((), ())), preferred_element_type=F32)
        drn = jnp.dot(dgl, wgt_ref[...], preferred_element_type=F32)
        dpg_ref[...] += _rowsum(drn * n1)
        dh1 = dh2 + _rms_bwd(drn * pg, n1, r1)
        dh_ref[...] = dh1
        dhb = dh1.astype(BF16)
        da_ref[...] = jnp.dot(dhb, wot_ref[:, 0:SGU_WIDTH], preferred_element_type=F32)
        db_ref[...] = jnp.dot(dhb, wot_ref[:, SGU_WIDTH:], preferred_element_type=F32)
        dwo_acc[0:SGU_WIDTH, :] += lax.dot_general(a, dhb, (((0,), (0,)), ((), ())), preferred_element_type=F32)
        dwo_acc[SGU_WIDTH:, :] += lax.dot_general(bb, dhb, (((0,), (0,)), ((), ())), preferred_element_type=F32)

        @pl.when(i == steps - 1)
        def _():
            for j in range(N_DEV):
                for acc, hbm in ((dwo_acc, dwo_hbm), (dwg_acc, dwg_hbm)):
                    rows_stage[...] = acc[j * LANES:(j + 1) * LANES, :].astype(BF16)
                    pltpu.sync_copy(rows_stage, hbm.at[j])
                cols_stage[...] = dwp_acc[:, j * LANES:(j + 1) * LANES].astype(BF16)
                pltpu.sync_copy(cols_stage, dwp_hbm.at[j])

    tile = lambda w: pl.BlockSpec((tm, w), lambda i: (i, 0))
    return pl.pallas_call(
        body, name="head_fwd_bwd", grid=(steps,),
        out_shape=(jax.ShapeDtypeStruct((t, SGU_WIDTH), F32), jax.ShapeDtypeStruct((t, DN_WIDTH), F32),
                   jax.ShapeDtypeStruct((t, D_MODEL), F32), jax.ShapeDtypeStruct((N_DEV, LANES, D_MODEL), BF16),
                   jax.ShapeDtypeStruct((N_DEV, LANES, D_MODEL), BF16), jax.ShapeDtypeStruct((N_DEV, PLE_DIM, LANES), BF16),
                   jax.ShapeDtypeStruct((1, D_MODEL), F32), jax.ShapeDtypeStruct((1, D_MODEL), F32),
                   jax.ShapeDtypeStruct((8, LANES), F32)),
        in_specs=[tile(SGU_WIDTH), tile(DN_WIDTH), tile(D_MODEL), tile(PLE_DIM), tile(D_MODEL),
                  VMEM_SPEC, VMEM_SPEC, VMEM_SPEC, VMEM_SPEC, VMEM_SPEC, _whole((1, D_MODEL)), _whole((1, D_MODEL))],
        out_specs=(tile(SGU_WIDTH), tile(DN_WIDTH), tile(D_MODEL), HBM_SPEC, HBM_SPEC, HBM_SPEC,
                   _whole((1, D_MODEL)), _whole((1, D_MODEL)), _whole((8, LANES))),
        scratch_shapes=[pltpu.VMEM((D_MODEL, D_MODEL), F32), pltpu.VMEM((D_MODEL, D_MODEL), F32),
                        pltpu.VMEM((PLE_DIM, D_MODEL), F32), pltpu.VMEM((LANES, D_MODEL), BF16),
                        pltpu.VMEM((PLE_DIM, LANES), BF16)],
        compiler_params=_params(1),
    )(a_out, b_out, x2, p2, target, w_out, w_out_t, w_gate, w_gate_t, w_proj, ple_g, fin_g)


def _inproj_bwd(x2, dh1, a_uvz, d_sgu, d_q, d_z, d_l, norm_g, sgu_weights, wat, wqt, wzt, wgt):
    t = x2.shape[0]
    tm = min(256, t)
    steps = t // tm

    widths = (a_uvz.shape[1], d_q.shape[1], d_z.shape[1], d_l.shape[1])
    starts = (0, widths[0], widths[0] + widths[1], widths[0] + widths[1] + widths[2])

    def body(x_ref, dh_ref, uvz_ref, dsgu_ref, dq_ref, dz_ref, dl_ref, g_ref, lg_ref, lb_ref, ws_ref, bt_ref,
             wat_ref, wqt_ref, wzt_ref, wgt_ref,
             dx_ref, dw_hbm, dg_ref, dlg_ref, dlb_ref, dws_ref, dbt_ref, dw_acc, stage_ref, da_ref):
        i = pl.program_id(0)

        @pl.when(i == 0)
        def _():
            dw_acc[...] = jnp.zeros_like(dw_acc)
            for ref in (dg_ref, dlg_ref, dlb_ref, dws_ref, dbt_ref):
                ref[...] = jnp.zeros_like(ref)

        _sgu_bwd_tile(uvz_ref, dsgu_ref, (lg_ref, lb_ref, ws_ref, bt_ref), da_ref, (dlg_ref, dlb_ref, dws_ref, dbt_ref))
        g = g_ref[...]
        n, r = _rms(x_ref[...])
        xn = (n * g).astype(BF16)
        dxn = None
        for d_ref, wt_ref, col0 in zip((da_ref, dq_ref, dz_ref, dl_ref), (wat_ref, wqt_ref, wzt_ref, wgt_ref), starts):
            term = jnp.dot(d_ref[...], wt_ref[...], preferred_element_type=F32)
            dxn = term if dxn is None else dxn + term
            width = d_ref.shape[1]
            for c0 in range(0, width, 512):
                c1 = min(c0 + 512, width)
                dw_acc[:, col0 + c0:col0 + c1] += lax.dot_general(xn, d_ref[:, c0:c1], (((0,), (0,)), ((), ())),
                                                                  preferred_element_type=F32)
        dg_ref[...] += _rowsum(dxn * n)
        dx_ref[...] = dh_ref[...] + _rms_bwd(dxn * g, n, r)

        @pl.when(i == steps - 1)
        def _():
            for j in range(N_DEV):
                stage_ref[...] = dw_acc[:, j * IN_SHARD:(j + 1) * IN_SHARD]
                pltpu.sync_copy(stage_ref, dw_hbm.at[j])

    tile = lambda w: pl.BlockSpec((tm, w), lambda i: (i, 0))
    sgu_shapes = ((1, SGU_WIDTH), (1, SGU_WIDTH), (SGU_GROUPS, SGU_CHUNK, SGU_CHUNK), (SGU_CHUNK, SGU_GROUPS))
    return pl.pallas_call(
        body, name="inproj_sgu_bwd", grid=(steps,),
        out_shape=(jax.ShapeDtypeStruct((t, D_MODEL), F32), jax.ShapeDtypeStruct((N_DEV, D_MODEL, IN_SHARD), F32),
                   jax.ShapeDtypeStruct((1, D_MODEL), F32)) + tuple(jax.ShapeDtypeStruct(s, F32) for s in sgu_shapes),
        in_specs=[tile(D_MODEL), tile(D_MODEL), tile(widths[0]), tile(SGU_WIDTH)] + [tile(w) for w in widths[1:]]
        + [_whole((1, D_MODEL))] + [_whole(s) for s in sgu_shapes] + [VMEM_SPEC] * 4,
        out_specs=(tile(D_MODEL), HBM_SPEC, _whole((1, D_MODEL))) + tuple(_whole(s) for s in sgu_shapes),
        scratch_shapes=[pltpu.VMEM((D_MODEL, sum(widths)), F32), pltpu.VMEM((D_MODEL, IN_SHARD), F32),
                        pltpu.VMEM((tm, widths[0]), BF16)],
        compiler_params=_params(1),
    )(x2, dh1, a_uvz, d_sgu, d_q, d_z, d_l, norm_g, *sgu_weights, wat, wqt, wzt, wgt)


def _reduce_adamw(recv, w, m, v, name, row_block=None):
    n, rows, cols = recv.shape
    rb = row_block or rows
    lead = w.ndim - 2

    def body(r_ref, w_ref, m_ref, v_ref, g_ref, d_ref, nm_ref, nv_ref):
        g = r_ref[0].astype(F32)
        for i in range(1, n):
            g = g + r_ref[i].astype(F32)
        m_new = ADAM_B1 * m_ref[...] + (1.0 - ADAM_B1) * g
        v_new = ADAM_B2 * v_ref[...] + (1.0 - ADAM_B2) * jnp.square(g)
        m_hat = m_new / (1.0 - ADAM_B1 ** ADAM_STEP)
        v_hat = v_new / (1.0 - ADAM_B2 ** ADAM_STEP)
        g_ref[...] = g
        d_ref[...] = -ADAM_LR * (m_hat / (jnp.sqrt(v_hat) + ADAM_EPS) + ADAM_WD * w_ref[...])
        nm_ref[...] = m_new
        nv_ref[...] = v_new

    blk = pl.BlockSpec((None,) * lead + (rb, cols), lambda i: (0,) * lead + (i, 0))
    return pl.pallas_call(
        body, name=name, grid=(rows // rb,),
        out_shape=tuple(jax.ShapeDtypeStruct(w.shape, F32) for _ in range(4)),
        in_specs=[pl.BlockSpec((n, rb, cols), lambda i: (0, i, 0)), blk, blk, blk],
        out_specs=(blk, blk, blk, blk),
        compiler_params=_params(1),
    )(recv, w, m, v)


def _adamw_replicated(received, ws, ms, vs):
    nw = len(ws)
    starts = [sum(SMALL_PIECE_ROWS[:i]) for i in range(len(SMALL_PIECE_ROWS))]

    def natural(g_ref, row0, shape):
        cols, rows = shape[-1], _size(shape[:-1])
        if cols == LANES:
            return g_ref[row0:row0 + rows, :].reshape(shape)
        if cols < LANES:
            return g_ref[row0:row0 + 1, 0:cols].reshape(shape)
        per = cols // LANES
        return jnp.concatenate(
            [jnp.concatenate([g_ref[row0 + r * per + k:row0 + r * per + k + 1, :] for k in range(per)], axis=1)
             for r in range(rows)], axis=0).reshape(shape)

    def body(r_ref, *refs):
        w_refs, m_refs, v_refs = refs[:nw], refs[nw:2 * nw], refs[2 * nw:3 * nw]
        conv_ref, loss_ref = refs[3 * nw], refs[3 * nw + 1]
        out_refs, g_ref = refs[3 * nw + 2:-1], refs[-1]
        g = r_ref[0]
        for q in range(1, N_CHIPS):
            g = g + r_ref[q]
        g_ref[...] = g
        conv_ref[...] = natural(g_ref, starts[0], (CONV_K, 3 * DN_WIDTH))
        loss_ref[...] = natural(g_ref, starts[-1], (1, 1))
        for i in range(nw):
            gi = natural(g_ref, starts[1 + i], w_refs[i].shape)
            m_new = ADAM_B1 * m_refs[i][...] + (1.0 - ADAM_B1) * gi
            v_new = ADAM_B2 * v_refs[i][...] + (1.0 - ADAM_B2) * jnp.square(gi)
            m_hat = m_new / (1.0 - ADAM_B1 ** ADAM_STEP)
            v_hat = v_new / (1.0 - ADAM_B2 ** ADAM_STEP)
            out_refs[4 * i][...] = gi
            out_refs[4 * i + 1][...] = -ADAM_LR * (m_hat / (jnp.sqrt(v_hat) + ADAM_EPS) + ADAM_WD * w_refs[i][...])
            out_refs[4 * i + 2][...] = m_new
            out_refs[4 * i + 3][...] = v_new

    def spec(a):
        lead = max(a.ndim - 3, 0)
        return pl.BlockSpec((None,) * lead + a.shape[lead:], lambda: (0,) * a.ndim)

    weight_specs = [spec(a) for a in ws]
    return pl.pallas_call(
        body, name="adamw_replicated",
        out_shape=(jax.ShapeDtypeStruct((CONV_K, 3 * DN_WIDTH), F32), jax.ShapeDtypeStruct((1, 1), F32))
        + tuple(jax.ShapeDtypeStruct(a.shape, F32) for a in ws for _ in range(4)),
        in_specs=[pl.BlockSpec(received.shape, lambda: (0, 0, 0))] + weight_specs * 3,
        out_specs=(pl.BlockSpec((CONV_K, 3 * DN_WIDTH), lambda: (0, 0)), pl.BlockSpec((1, 1), lambda: (0, 0)))
        + tuple(s for s in weight_specs for _ in range(4)),
        scratch_shapes=[pltpu.VMEM(received.shape[1:], F32)],
        compiler_params=pltpu.CompilerParams(vmem_limit_bytes=VMEM_LIMIT),
    )(received, *ws, *ms, *vs)


def _pack_rows(pieces, rows):
    padded = [jnp.pad(jnp.ravel(p), (0, -p.size % LANES)) for p in pieces]
    flat = jnp.concatenate(padded)
    return jnp.pad(flat, (0, rows * LANES - flat.shape[0])).reshape(rows, LANES)


def kernel(x, p, norm_g, w_in, sgu_ln_g, sgu_ln_b, sgu_w_s, sgu_b_s, dn_conv_w, dn_a_log, dn_dt_bias, dn_o_norm_g, w_out, ple_norm_g, ple_gate_w, ple_proj_w, final_norm_g, loss_target, m_norm_g, m_w_in, m_sgu_ln_g, m_sgu_ln_b, m_sgu_w_s, m_sgu_b_s, m_dn_conv_w, m_dn_a_log, m_dn_dt_bias, m_dn_o_norm_g, m_w_out, m_ple_norm_g, m_ple_gate_w, m_ple_proj_w, m_final_norm_g, v_norm_g, v_w_in, v_sgu_ln_g, v_sgu_ln_b, v_sgu_w_s, v_sgu_b_s, v_dn_conv_w, v_dn_a_log, v_dn_dt_bias, v_dn_o_norm_g, v_w_out, v_ple_norm_g, v_ple_gate_w, v_ple_proj_w, v_final_norm_g):
    weights = dict(norm_g=norm_g, w_in=w_in, sgu_ln_g=sgu_ln_g, sgu_ln_b=sgu_ln_b, sgu_w_s=sgu_w_s, sgu_b_s=sgu_b_s,
                   dn_conv_w=dn_conv_w, dn_a_log=dn_a_log, dn_dt_bias=dn_dt_bias, dn_o_norm_g=dn_o_norm_g, w_out=w_out,
                   ple_norm_g=ple_norm_g, ple_gate_w=ple_gate_w, ple_proj_w=ple_proj_w, final_norm_g=final_norm_g)
    mom1 = dict(norm_g=m_norm_g, w_in=m_w_in, sgu_ln_g=m_sgu_ln_g, sgu_ln_b=m_sgu_ln_b, sgu_w_s=m_sgu_w_s,
                sgu_b_s=m_sgu_b_s, dn_conv_w=m_dn_conv_w, dn_a_log=m_dn_a_log, dn_dt_bias=m_dn_dt_bias,
                dn_o_norm_g=m_dn_o_norm_g, w_out=m_w_out, ple_norm_g=m_ple_norm_g, ple_gate_w=m_ple_gate_w,
                ple_proj_w=m_ple_proj_w, final_norm_g=m_final_norm_g)
    mom2 = dict(norm_g=v_norm_g, w_in=v_w_in, sgu_ln_g=v_sgu_ln_g, sgu_ln_b=v_sgu_ln_b, sgu_w_s=v_sgu_w_s,
                sgu_b_s=v_sgu_b_s, dn_conv_w=v_dn_conv_w, dn_a_log=v_dn_a_log, dn_dt_bias=v_dn_dt_bias,
                dn_o_norm_g=v_dn_o_norm_g, w_out=v_w_out, ple_norm_g=v_ple_norm_g, ple_gate_w=v_ple_gate_w,
                ple_proj_w=v_ple_proj_w, final_norm_g=v_final_norm_g)
    nb, s, _ = x.shape
    t = nb * s

    w_in_blocks, conv_blocks = _all_gather([w_in[0].astype(BF16), dn_conv_w[0]])
    w_in_full = jnp.moveaxis(w_in_blocks, 0, 1).reshape(D_MODEL, IN_COLS)
    wa = w_in_full[:, :3 * SGU_WIDTH]
    wq = w_in_full[:, 3 * SGU_WIDTH:3 * SGU_WIDTH + 3 * DN_WIDTH]
    wz = w_in_full[:, 3 * SGU_WIDTH + 3 * DN_WIDTH:3 * SGU_WIDTH + 4 * DN_WIDTH]
    wg = jnp.pad(w_in_full[:, 3 * SGU_WIDTH + 4 * DN_WIDTH:], ((0, 0), (0, GATE_PAD - 2 * DN_HEADS)))
    conv_full = jnp.moveaxis(conv_blocks, 0, 1).reshape(CONV_K, 3 * DN_WIDTH)
    later_shards = [w_out[0].astype(BF16), ple_gate_w[0].astype(BF16), ple_proj_w[0].astype(BF16)]

    pad_row = lambda a: jnp.pad(a.reshape(1, -1), ((0, 0), (DN_HEADS, GATE_PAD - DN_HEADS - a.size)))
    alog, dtb = pad_row(dn_a_log), pad_row(dn_dt_bias)
    og = dn_o_norm_g.reshape(1, DN_HEAD_DIM)
    ws = sgu_w_s.reshape(SGU_GROUPS, SGU_CHUNK, SGU_CHUNK)
    b_t = sgu_b_s.reshape(SGU_GROUPS, SGU_CHUNK).T
    fin_g = final_norm_g.reshape(1, D_MODEL)

    x2 = x.reshape(t, D_MODEL)
    sgu_weights = (sgu_ln_g, sgu_ln_b, ws, b_t)
    a_uvz, b_qkv, b_z, b_l, a_out, conv_out, w_out_blocks, w_gate_blocks, w_proj_blocks = _inproj_fwd(
        x2, s, norm_g, wa, wq, wz, wg, sgu_weights, conv_full, later_shards)
    w_out_full = w_out_blocks.reshape(D_MODEL, D_MODEL)
    w_gate_full = w_gate_blocks.reshape(D_MODEL, D_MODEL)
    w_proj_full = jnp.moveaxis(w_proj_blocks, 0, 1).reshape(PLE_DIM, D_MODEL)
    qkv3 = b_qkv.reshape(nb, s, 3 * DN_WIDTH)
    conv_out = conv_out.reshape(nb, s, 3 * DN_WIDTH)
    z3 = b_z.reshape(nb, s, DN_WIDTH)
    l3 = b_l.reshape(nb, s, GATE_PAD)
    b_out, states = _dn_fwd(conv_out, z3, l3, alog, dtb, og)

    d_a, d_b, dh1, g_w_out, g_gate, g_proj, g_ple_g, g_fin_g, loss_tile = _head(
        a_out, b_out.reshape(t, DN_WIDTH), x2, p.reshape(t, PLE_DIM), loss_target.reshape(t, D_MODEL),
        w_out_full, w_out_full.T, w_gate_full, w_gate_full.T, w_proj_full, ple_norm_g, fin_g)
    d_qkv, d_z, d_l, g_conv, g_alog, g_dtb, g_og, *head_received = _dn_bwd(
        qkv3, conv_out, z3, l3, conv_full, alog, dtb, og, states, d_b.reshape(nb, s, DN_WIDTH),
        [g_w_out, g_gate, g_proj])
    grad_x, g_w_in, g_norm, g_ln_g, g_ln_b, g_ws, g_bt = _inproj_bwd(
        x2, dh1, a_uvz, d_a, d_qkv.reshape(t, 3 * DN_WIDTH), d_z.reshape(t, DN_WIDTH), d_l.reshape(t, GATE_PAD),
        norm_g, sgu_weights, wa.T, wq.T, wz.T, wg.T)

    by_device = [g_w_in]
    small = _pack_rows([g_conv, g_norm, g_ln_g, g_ln_b, g_ws, g_bt.T, g_alog[:, DN_HEADS:2 * DN_HEADS], g_dtb[:, DN_HEADS:2 * DN_HEADS], g_og,
                        g_ple_g, g_fin_g, (0.5 / D_MODEL) * loss_tile[0:1, 0:1]], SMALL_ROWS)
    *from_sibling, small_sibling = _sibling_exchange(by_device, small)
    core = lax.axis_index("c").astype(jnp.int32).reshape(1)
    *chip_sums, small_sum = _pair_sum(core, by_device, from_sibling, small, small_sibling)
    *received, small_received = _chip_exchange(chip_sums, small_sum)

    results = {}
    for name, recv, rb in zip(("w_in", "w_out", "ple_gate_w", "ple_proj_w"), received + head_received,
                              (128, None, None, None)):
        results[name] = _reduce_adamw(recv, weights[name], mom1[name], mom2[name], "adamw_" + name, rb)
    names = [name for name, _ in REPLICATED]
    two_d = lambda a: a.reshape(1, -1) if a.ndim == 1 else a
    g_conv_sum, loss_sum, *flat_outs = _adamw_replicated(
        small_received, *[[two_d(src[k]) for k in names] for src in (weights, mom1, mom2)])
    for i, k in enumerate(names):
        results[k] = [a.reshape(weights[k].shape) for a in flat_outs[4 * i:4 * i + 4]]
    loss = loss_sum[0, 0]
    me = 4 * lax.axis_index("x") + 2 * lax.axis_index("y") + lax.axis_index("c")
    conv_mine = lax.dynamic_slice(g_conv_sum, (0, me * 192), (CONV_K, 192))
    results["dn_conv_w"] = _reduce_adamw(conv_mine[None], dn_conv_w, m_dn_conv_w, v_dn_conv_w, "adamw_dn_conv_w")

    return (loss, grad_x.reshape(nb, s, D_MODEL), *[results[k][0] for k in WEIGHT_ORDER],
            *[results[k][1] for k in WEIGHT_ORDER], *[results[k][2] for k in WEIGHT_ORDER],
            *[results[k][3] for k in WEIGHT_ORDER])
```

```python
import jax
import jax.numpy as jnp
from jax import lax
from jax.experimental import pallas as pl
from jax.experimental.pallas import tpu as pltpu

F32 = jnp.float32
BF16 = jnp.bfloat16

N_DEV = 8
D_MODEL = 1024
SGU_WIDTH = 512
SGU_GROUPS = 4
SGU_CHUNK = 128
DN_WIDTH = 512
DN_HEADS = 4
DN_HEAD_DIM = 128
DN_CHUNK = 128
CONV_K = 4
CONV_HALO = 8
PLE_DIM = 256
EPS = 1e-6
IN_COLS = 3592
IN_SHARD = IN_COLS // N_DEV
GATE_PAD = 128

ADAM_LR = 0.001
ADAM_B1 = 0.9
ADAM_B2 = 0.999
ADAM_EPS = 1e-08
ADAM_WD = 0.01
ADAM_STEP = 10

LANES = 128
VMEM_LIMIT = 56 * 1024 * 1024
MESH = pl.DeviceIdType.MESH

REPLICATED = (("norm_g", (1, D_MODEL)), ("sgu_ln_g", (1, SGU_WIDTH)), ("sgu_ln_b", (1, SGU_WIDTH)),
              ("sgu_w_s", (1, SGU_GROUPS, SGU_CHUNK, SGU_CHUNK)), ("sgu_b_s", (1, SGU_GROUPS, SGU_CHUNK)),
              ("dn_a_log", (1, DN_HEADS)), ("dn_dt_bias", (1, DN_HEADS)), ("dn_o_norm_g", (1, DN_HEAD_DIM)),
              ("ple_norm_g", (1, D_MODEL)), ("final_norm_g", (D_MODEL,)))
WEIGHT_ORDER = ("norm_g", "w_in", "sgu_ln_g", "sgu_ln_b", "sgu_w_s", "sgu_b_s", "dn_conv_w", "dn_a_log",
                "dn_dt_bias", "dn_o_norm_g", "w_out", "ple_norm_g", "ple_gate_w", "ple_proj_w", "final_norm_g")


def _size(shape):
    n = 1
    for s in shape:
        n *= s
    return n


SMALL_LAYOUT = (("conv", (CONV_K, 3 * DN_WIDTH)),) + REPLICATED + (("loss", (1,)),)
SMALL_PIECE_ROWS = tuple(-(-_size(s) // LANES) for _, s in SMALL_LAYOUT)
SMALL_ROWS = -(-sum(SMALL_PIECE_ROWS) // 8) * 8


def _bdot(a, b):
    return jnp.dot(a.astype(BF16), b.astype(BF16), preferred_element_type=F32)


def _sigmoid(x):
    return pl.reciprocal(1.0 + jnp.exp(-x), approx=True)


@jax.custom_vjp
def _silu(x):
    return x * _sigmoid(x)


def _silu_fwd(x):
    s = _sigmoid(x)
    return x * s, (x, s)


def _silu_bwd(res, ct):
    x, s = res
    return (ct * (s * (1.0 + x * (1.0 - s))),)


_silu.defvjp(_silu_fwd, _silu_bwd)


def _gelu(x):
    return 0.5 * x * (1.0 + lax.erf(x * (0.5 ** 0.5)))


def _softplus(x):
    return jnp.maximum(x, 0.0) + jnp.log1p(jnp.exp(-jnp.abs(x)))


def _l2n(x):
    return x * lax.rsqrt(jnp.sum(x * x, axis=-1, keepdims=True) + EPS)


def _rms(x):
    r = lax.rsqrt(jnp.mean(x * x, axis=-1, keepdims=True) + EPS)
    return x * r, r


def _rms_bwd(dn, n, r):
    return r * (dn - n * jnp.mean(dn * n, axis=-1, keepdims=True))


def _onehot_row(idx, width):
    return (lax.broadcasted_iota(jnp.int32, (1, width), 1) == idx).astype(F32)


def _rowsum(x):
    return jnp.sum(x, axis=0, keepdims=True)


def _iota2(n):
    return lax.broadcasted_iota(jnp.int32, (n, n), 0), lax.broadcasted_iota(jnp.int32, (n, n), 1)


def _bmm(a, b):
    return lax.dot_general(a.astype(BF16), b.astype(BF16), (((2,), (1,)), ((0,), (0,))), preferred_element_type=F32)


def _bmm_nt(a, b):
    return lax.dot_general(a.astype(BF16), b.astype(BF16), (((2,), (2,)), ((0,), (0,))), preferred_element_type=F32)


def _bmm_tn(a, b):
    return lax.dot_general(a.astype(BF16), b.astype(BF16), (((1,), (1,)), ((0,), (0,))), preferred_element_type=F32)


def _tri_inv_impl(a):
    n = a.shape[-1]
    r, c = _iota2(n)
    x = r ^ c
    eye = (r == c).astype(F32)
    ad = jnp.where(x < 16, a, 0.0)
    p2 = _bmm(ad, ad)
    e = p2 - ad - _bmm(ad, p2)
    p4 = _bmm(p2, p2)
    e = e + p4 + _bmm(e, p4)
    p8 = _bmm(p4, p4)
    e = e + p8 + _bmm(e, p8)
    size = 16
    while size < n:
        m = jnp.where(jnp.logical_and(x < 2 * size, x >= size), a, 0.0)
        f = m + _bmm(m, e)
        e = e - f - _bmm(e, f)
        size *= 2
    return e + eye


@jax.custom_vjp
def _tri_inv(a):
    return _tri_inv_impl(a)


def _tri_inv_fwd(a):
    t = _tri_inv_impl(a)
    return t, t


def _tri_inv_bwd(t, dt):
    return (-_bmm_tn(t, _bmm_nt(dt, t)),)


_tri_inv.defvjp(_tri_inv_fwd, _tri_inv_bwd)


def _sgu_core(u, v, z, lg, lb, ws, bcol):
    n = ws.shape[0]
    r, c = _iota2(n)
    wm = jnp.where(r >= c, ws, 0.0)
    gu = _gelu(u)
    gv = _gelu(v)
    xc = gv - jnp.mean(gv, axis=-1, keepdims=True)
    ln = xc * lax.rsqrt(jnp.mean(xc * xc, axis=-1, keepdims=True) + EPS) * lg + lb
    s = _bdot(wm, ln) + bcol
    return gu * s * _silu(z)


def _lanes_of(x):
    return jnp.concatenate([x[i] for i in range(x.shape[0])], axis=1)


def _batch_of(x, width):
    return jnp.concatenate([x[None, :, i * width:(i + 1) * width] for i in range(x.shape[1] // width)], axis=0)


def _mask_dot(mask, x):
    hi = x.astype(BF16)
    lo = (x - hi.astype(F32)).astype(BF16)
    m = mask.astype(BF16)
    return jnp.dot(m, hi, preferred_element_type=F32) + jnp.dot(m, lo, preferred_element_type=F32)


def _tri_mask(n, upper):
    r, c = _iota2(n)
    return (r <= c) if upper else (r >= c)


@jax.custom_vjp
def _cumsum_rows(x):
    return _mask_dot(_tri_mask(x.shape[0], False), x)


def _cumsum_rows_fwd(x):
    return _cumsum_rows(x), None


def _cumsum_rows_bwd(_, ct):
    return (_mask_dot(_tri_mask(ct.shape[0], True), ct),)


_cumsum_rows.defvjp(_cumsum_rows_fwd, _cumsum_rows_bwd)


@jax.custom_vjp
def _colsum_all_rows(x):
    return _mask_dot(jnp.ones((x.shape[0], x.shape[0]), jnp.bool_), x)


def _colsum_all_rows_fwd(x):
    return _colsum_all_rows(x), None


def _colsum_all_rows_bwd(_, ct):
    return (_mask_dot(jnp.ones((ct.shape[0], ct.shape[0]), jnp.bool_), ct),)


_colsum_all_rows.defvjp(_colsum_all_rows_fwd, _colsum_all_rows_bwd)


def _dn_core(cq, ck, cv, z, logits, state, alog, dtb, og):
    gn, cn, dh = cq.shape
    heads = gn // logits.shape[0]
    q = _l2n(_silu(cq)) * (dh ** -0.5)
    k = _l2n(_silu(ck))
    v = _silu(cv)
    beta_lanes = jax.nn.sigmoid(logits)
    g_lanes = -jnp.exp(alog) * _softplus(logits + dtb)
    column = lambda rows, lane: jnp.sum(rows * _onehot_row(lane, rows.shape[-1]), axis=-1, keepdims=True)[None]
    beta = jnp.concatenate([column(beta_lanes[i // heads], i % heads) for i in range(gn)], axis=0)
    g = jnp.concatenate([column(g_lanes[i // heads], heads + i % heads) for i in range(gn)], axis=0)
    r, c = _iota2(cn)
    tril = r >= c
    rw = lax.broadcasted_iota(jnp.int32, (cn, dh), 0)
    cw = lax.broadcasted_iota(jnp.int32, (cn, dh), 1)
    upper_wide = (rw <= cw).astype(F32)
    g_wide = jnp.broadcast_to(g, (gn, cn, dh))
    gc_wide = _batch_of(_cumsum_rows(_lanes_of(g_wide)), dh)
    gc_cols = _batch_of(_colsum_all_rows(_lanes_of(g_wide * upper_wide)), dh)[:, :, :cn]
    decay = jnp.exp(jnp.where(tril, gc_wide[:, :, :cn] - gc_cols, -1e30))
    kb = k * beta
    kk = _bmm_nt(kb, k) * decay
    t = _tri_inv(jnp.where(r > c, kk, 0.0))
    eg = jnp.exp(gc_wide)
    sol = _bmm(t, jnp.concatenate([v * beta, kb * eg], axis=-1))
    u_val, w_dec = sol[:, :, :dh], sol[:, :, dh:]
    qk = _bmm_nt(q, k) * decay
    g_last = jnp.sum(g_wide, axis=1, keepdims=True)
    k_dec = k * jnp.exp(g_last - gc_wide)
    ws = _bmm(jnp.concatenate([w_dec, q * eg], axis=1), state)
    v_new = u_val - ws[:, :cn]
    o = ws[:, cn:] + _bmm(qk, v_new)
    new_state = state * jnp.exp(g_last) + _bmm_tn(k_dec, v_new)
    on, _ = _rms(o)
    return on * og * _silu(z), new_state


N_CHIPS = 4
HBM_SPEC = pl.BlockSpec(memory_space=pl.ANY)


def _place():
    return lax.axis_index("x"), lax.axis_index("y"), lax.axis_index("c")


def _other_chip(k):
    x, y, _ = _place()
    px = 1 - x if k & 2 else x
    py = 1 - y if k & 1 else y
    return px, py, 2 * px + py


def _remote(src, dst, send_sem, recv_sem, device):
    return pltpu.make_async_remote_copy(src_ref=src, dst_ref=dst, send_sem=send_sem, recv_sem=recv_sem,
                                        device_id=device, device_id_type=MESH)


def _other_device(k):
    x, y, c = _place()
    px = 1 - x if k & 4 else x
    py = 1 - y if k & 2 else y
    pc = 1 - c if k & 1 else c
    return (px, py, pc), 4 * px + 2 * py + pc


def _direct_exchange(srcs, outs, send_sems, recv_sems, local_sems, gather):
    x, y, c = _place()
    me = 4 * x + 2 * y + c

    def copies(arriving):
        out_list = []
        for a, (src, out) in enumerate(zip(srcs, outs)):
            for k in range(1, N_DEV):
                peer, index = _other_device(k)
                mine = src if gather else src.at[index]
                out_list.append(_remote(mine, out.at[index if arriving else me], send_sems.at[a, k - 1],
                                        recv_sems.at[a, k - 1], peer))
        return out_list

    def local_copies():
        return [pltpu.make_async_copy(src if gather else src.at[me], out.at[me], local_sems.at[a])
                for a, (src, out) in enumerate(zip(srcs, outs))]

    def start():
        for cp in local_copies() + copies(False):
            cp.start()

    def wait():
        for cp in copies(True):
            cp.wait_recv()
        for cp in copies(False):
            cp.wait_send()
        for cp in local_copies():
            cp.wait()

    return start, wait


def _exchange_scratch(n):
    return [pltpu.SemaphoreType.DMA((n, N_DEV - 1)), pltpu.SemaphoreType.DMA((n, N_DEV - 1)), pltpu.SemaphoreType.DMA((n,))]


def _all_gather(shards):
    n = len(shards)

    def body(*refs):
        srcs, outs = refs[:n], refs[n:2 * n]
        send_sems, recv_sems, local_sems = refs[2 * n:]
        x, y, c = _place()
        me = 4 * x + 2 * y + c
        sibling = (x, y, 1 - c)
        local = [pltpu.make_async_copy(srcs[a], outs[a].at[me], local_sems.at[a]) for a in range(n)]
        for cp in local:
            cp.start()
        sends = []
        for a in range(n):
            sends.append(_remote(srcs[a], outs[a].at[me], send_sems.at[a, 0], recv_sems.at[a, 0], sibling))
        for k in range(1, N_CHIPS):
            px, py, _ = _other_chip(k)
            for a in range(n):
                sends.append(_remote(srcs[a], outs[a].at[me], send_sems.at[a, k], recv_sems.at[a, k], (px, py, c)))
        for cp in sends:
            cp.start()
        passed = []
        for k in range(1, N_CHIPS):
            px, py, _ = _other_chip(k)
            blk = 4 * px + 2 * py + c
            for a in range(n):
                _remote(srcs[a], outs[a].at[blk], send_sems.at[a, k], recv_sems.at[a, k], (px, py, c)).wait_recv()
            for a in range(n):
                cp = _remote(outs[a].at[blk], outs[a].at[blk], send_sems.at[a, 3 + k], recv_sems.at[a, 3 + k], sibling)
                cp.start()
                passed.append(cp)
        for a in range(n):
            _remote(srcs[a], outs[a].at[me + 1 - 2 * c], send_sems.at[a, 0], recv_sems.at[a, 0], sibling).wait_recv()
        for k in range(1, N_CHIPS):
            px, py, _ = _other_chip(k)
            blk = 4 * px + 2 * py + 1 - c
            for a in range(n):
                _remote(srcs[a], outs[a].at[blk], send_sems.at[a, 3 + k], recv_sems.at[a, 3 + k], sibling).wait_recv()
        for cp in sends + passed:
            cp.wait_send()
        for cp in local:
            cp.wait()

    return pl.pallas_call(
        body, name="all_gather_weights",
        out_shape=tuple(jax.ShapeDtypeStruct((N_DEV,) + a.shape, a.dtype) for a in shards),
        in_specs=[HBM_SPEC] * n, out_specs=(HBM_SPEC,) * n,
        scratch_shapes=[pltpu.SemaphoreType.DMA((n, N_DEV - 1)), pltpu.SemaphoreType.DMA((n, N_DEV - 1)),
                        pltpu.SemaphoreType.DMA((n,))],
    )(*shards)


def _sibling_exchange(by_device, small):
    n = len(by_device)

    def body(*refs):
        srcs, small_src = refs[:n], refs[n]
        outs, small_out = refs[n + 1:2 * n + 1], refs[2 * n + 1]
        send_sems, recv_sems = refs[2 * n + 2:]
        x, y, c = _place()
        sibling = (x, y, 1 - c)
        copies = [_remote(small_src, small_out, send_sems.at[n, 0], recv_sems.at[n, 0], sibling)]
        for a in range(n):
            for q in range(N_CHIPS):
                copies.append(_remote(srcs[a].at[2 * q + 1 - c], outs[a].at[q], send_sems.at[a, q], recv_sems.at[a, q],
                                      sibling))
        for cp in copies:
            cp.start()
        for cp in copies:
            cp.wait_recv()
        for cp in copies:
            cp.wait_send()

    return pl.pallas_call(
        body, name="grad_sibling_exchange",
        out_shape=tuple(jax.ShapeDtypeStruct((N_CHIPS,) + a.shape[1:], a.dtype) for a in by_device)
        + (jax.ShapeDtypeStruct(small.shape, small.dtype),),
        in_specs=[HBM_SPEC] * (n + 1), out_specs=(HBM_SPEC,) * (n + 1),
        scratch_shapes=[pltpu.SemaphoreType.DMA((n + 1, N_CHIPS)), pltpu.SemaphoreType.DMA((n + 1, N_CHIPS))],
    )(*by_device, small)


def _chip_exchange(chip_sums, small):
    n = len(chip_sums)

    def body(*refs):
        srcs, small_src = refs[:n], refs[n]
        outs, small_out = refs[n + 1:2 * n + 1], refs[2 * n + 1]
        send_sems, recv_sems, local_sems = refs[2 * n + 2:]
        x, y, c = _place()
        mine = 2 * x + y
        local = [pltpu.make_async_copy(srcs[a].at[mine], outs[a].at[mine], local_sems.at[a]) for a in range(n)]
        local.append(pltpu.make_async_copy(small_src, small_out.at[mine], local_sems.at[n]))
        for cp in local:
            cp.start()
        sends = []
        for k in range(1, N_CHIPS):
            px, py, chip = _other_chip(k)
            for a in range(n):
                sends.append(_remote(srcs[a].at[chip], outs[a].at[mine], send_sems.at[a, k - 1], recv_sems.at[a, k - 1],
                                     (px, py, c)))
            sends.append(_remote(small_src, small_out.at[mine], send_sems.at[n, k - 1], recv_sems.at[n, k - 1], (px, py, c)))
        for cp in sends:
            cp.start()
        for k in range(1, N_CHIPS):
            px, py, chip = _other_chip(k)
            for a in range(n):
                _remote(srcs[a].at[chip], outs[a].at[chip], send_sems.at[a, k - 1], recv_sems.at[a, k - 1],
                        (px, py, c)).wait_recv()
            _remote(small_src, small_out.at[chip], send_sems.at[n, k - 1], recv_sems.at[n, k - 1], (px, py, c)).wait_recv()
        for cp in sends:
            cp.wait_send()
        for cp in local:
            cp.wait()

    return pl.pallas_call(
        body, name="grad_chip_exchange",
        out_shape=tuple(jax.ShapeDtypeStruct(a.shape, a.dtype) for a in chip_sums)
        + (jax.ShapeDtypeStruct((N_CHIPS,) + small.shape, small.dtype),),
        in_specs=[HBM_SPEC] * (n + 1), out_specs=(HBM_SPEC,) * (n + 1),
        scratch_shapes=[pltpu.SemaphoreType.DMA((n + 1, N_CHIPS - 1)), pltpu.SemaphoreType.DMA((n + 1, N_CHIPS - 1)),
                        pltpu.SemaphoreType.DMA((n + 1,))],
    )(*chip_sums, small)


def _pair_sum(core, by_device, from_sibling, small, small_from_sibling):
    n = len(by_device)

    def body(core_ref, *refs):
        own, sib = refs[:n], refs[n:2 * n]
        small_own, small_sib = refs[2 * n], refs[2 * n + 1]
        outs, small_out = refs[2 * n + 2:3 * n + 2], refs[3 * n + 2]
        for a in range(n):
            outs[a][...] = (own[a][...] + sib[a][...]).astype(outs[a].dtype)
        small_out[...] = small_own[...] + small_sib[...]

    def block(a):
        return (None,) + a.shape[1:], (0,) * (a.ndim - 1)

    own_specs = [pl.BlockSpec(block(a)[0], lambda q, core_ref, z=block(a)[1]: (2 * q + core_ref[0],) + z) for a in by_device]
    sib_specs = [pl.BlockSpec(block(a)[0], lambda q, core_ref, z=block(a)[1]: (q,) + z) for a in by_device]
    small_spec = pl.BlockSpec(small.shape, lambda q, core_ref: (0,) * small.ndim)
    return pl.pallas_call(
        body, name="grad_pair_sum",
        grid_spec=pltpu.PrefetchScalarGridSpec(
            num_scalar_prefetch=1, grid=(N_CHIPS,),
            in_specs=own_specs + sib_specs + [small_spec, small_spec],
            out_specs=tuple(sib_specs) + (small_spec,)),
        out_shape=tuple(jax.ShapeDtypeStruct(a.shape, BF16) for a in from_sibling)
        + (jax.ShapeDtypeStruct(small.shape, F32),),
        compiler_params=_params(1),
    )(core, *by_device, *from_sibling, small, small_from_sibling)


def _params(n_axes):
    return pltpu.CompilerParams(dimension_semantics=("arbitrary",) * n_axes, vmem_limit_bytes=VMEM_LIMIT)


def _whole(shape):
    return pl.BlockSpec(shape, lambda *_: (0,) * len(shape))


VMEM_SPEC = pl.BlockSpec(memory_space=pltpu.VMEM)


def _inproj_fwd(x2, seq_len, norm_g, wat, wqt, wzt, wgt, sgu_weights, conv_w, later_shards):
    t = x2.shape[0]
    tm = min(512, seq_len)
    tiles_per_seq = seq_len // tm
    steps = t // tm
    ns = len(later_shards)

    def body(x_ref, g_ref, wa_ref, wq_ref, wz_ref, wg_ref, lg_ref, lb_ref, ws_ref, bt_ref, cw_ref, *rest):
        shard_refs, rest = rest[:ns], rest[ns:]
        a_ref, q_ref, z_ref, l_ref, sgu_ref, c_ref = rest[:6]
        gathered_refs, (xpad_ref, send_sems, recv_sems, local_sems) = rest[6:6 + ns], rest[6 + ns:]
        start_gather, wait_gather = _direct_exchange(shard_refs, gathered_refs, send_sems, recv_sems, local_sems, True)
        pl.when(pl.program_id(0) == 0)(start_gather)
        n, _ = _rms(x_ref[...])
        xn = (n * g_ref[...]).astype(BF16)
        for w_ref, o_ref in ((wa_ref, a_ref), (wq_ref, q_ref), (wz_ref, z_ref), (wg_ref, l_ref)):
            width = w_ref.shape[0]
            for c0 in range(0, width, 512):
                c1 = min(c0 + 512, width)
                o_ref[:, c0:c1] = lax.dot_general(xn, w_ref[c0:c1, :], (((1,), (1,)), ((), ())),
                                                  preferred_element_type=F32)
        for row0 in range(0, tm, SGU_CHUNK):
            for grp in range(SGU_GROUPS):
                args = _sgu_pieces(a_ref, lg_ref, lb_ref, ws_ref, bt_ref, row0, grp)
                sgu_ref[pl.ds(row0, SGU_CHUNK), pl.ds(grp * 128, 128)] = _sgu_core(*args).astype(sgu_ref.dtype)

        @pl.when(pl.program_id(0) % tiles_per_seq == 0)
        def _():
            xpad_ref[0:CONV_HALO, :] = jnp.zeros((CONV_HALO, xpad_ref.shape[1]), F32)

        xpad_ref[CONV_HALO:, :] = q_ref[...]
        acc = None
        for j in range(CONV_K):
            term = cw_ref[j:j + 1, :] * xpad_ref[pl.ds(CONV_HALO - CONV_K + 1 + j, tm), :]
            acc = term if acc is None else acc + term
        c_ref[...] = acc
        xpad_ref[0:CONV_HALO, :] = xpad_ref[tm:tm + CONV_HALO, :]
        pl.when(pl.program_id(0) == steps - 1)(wait_gather)

    widths = (wat.shape[0], wqt.shape[0], wzt.shape[0], wgt.shape[0])
    tile = lambda w: pl.BlockSpec((tm, w), lambda i: (i, 0))
    sgu_shapes = ((1, SGU_WIDTH), (1, SGU_WIDTH), (SGU_GROUPS, SGU_CHUNK, SGU_CHUNK), (SGU_CHUNK, SGU_GROUPS))
    return pl.pallas_call(
        body, name="inproj_sgu_conv_fwd", grid=(steps,),
        out_shape=tuple(jax.ShapeDtypeStruct((t, w), F32) for w in widths)
        + (jax.ShapeDtypeStruct((t, SGU_WIDTH), BF16), jax.ShapeDtypeStruct((t, widths[1]), F32))
        + tuple(jax.ShapeDtypeStruct((N_DEV,) + a.shape, a.dtype) for a in later_shards),
        in_specs=[tile(D_MODEL), _whole((1, D_MODEL)), VMEM_SPEC, VMEM_SPEC, VMEM_SPEC, VMEM_SPEC]
        + [_whole(s) for s in sgu_shapes] + [_whole((CONV_K, widths[1]))] + [HBM_SPEC] * ns,
        out_specs=tuple(tile(w) for w in widths) + (tile(SGU_WIDTH), tile(widths[1])) + (HBM_SPEC,) * ns,
        scratch_shapes=[pltpu.VMEM((CONV_HALO + tm, widths[1]), F32)] + _exchange_scratch(ns),
        compiler_params=_params(1),
    )(x2, norm_g, wat, wqt, wzt, wgt, *sgu_weights, conv_w, *later_shards)


def _sgu_pieces(uvz_ref, lg_ref, lb_ref, ws_ref, bt_ref, row0, grp):
    rows = pl.ds(row0, SGU_CHUNK)
    lanes = pl.ds(grp * 128, 128)
    u = uvz_ref[rows, pl.ds(grp * 128, 128)]
    v = uvz_ref[rows, pl.ds(SGU_WIDTH + grp * 128, 128)]
    z = uvz_ref[rows, pl.ds(2 * SGU_WIDTH + grp * 128, 128)]
    bcol = jnp.sum(bt_ref[...] * _onehot_row(grp, SGU_GROUPS), axis=-1, keepdims=True)
    return u, v, z, lg_ref[:, lanes], lb_ref[:, lanes], ws_ref[grp], bcol


def _sgu_bwd_tile(uvz_ref, do_ref, sgu_refs, duvz_ref, grad_refs):
    lg_ref, lb_ref, ws_ref, bt_ref = sgu_refs
    dlg_ref, dlb_ref, dws_ref, dbt_ref = grad_refs
    for row0 in range(0, uvz_ref.shape[0], SGU_CHUNK):
        rows = pl.ds(row0, SGU_CHUNK)
        for grp in range(SGU_GROUPS):
            lanes = pl.ds(grp * 128, 128)
            args = _sgu_pieces(uvz_ref, lg_ref, lb_ref, ws_ref, bt_ref, row0, grp)
            _, pull = jax.vjp(_sgu_core, *args)
            du, dv, dz, dlg, dlb, dws, dbcol = pull(do_ref[rows, lanes])
            duvz_ref[rows, pl.ds(grp * 128, 128)] = du.astype(duvz_ref.dtype)
            duvz_ref[rows, pl.ds(SGU_WIDTH + grp * 128, 128)] = dv.astype(duvz_ref.dtype)
            duvz_ref[rows, pl.ds(2 * SGU_WIDTH + grp * 128, 128)] = dz.astype(duvz_ref.dtype)
            dlg_ref[:, lanes] += dlg
            dlb_ref[:, lanes] += dlb
            dws_ref[grp] += dws
            dbt_ref[...] += dbcol * _onehot_row(grp, SGU_GROUPS)


def _dn_pairs(nb):
    return [(b, h) for b in range(nb) for h in range(DN_HEADS)]


def _dn_batch_args(c_ref, z_ref):
    pairs = _dn_pairs(c_ref.shape[0])
    pick = lambda ref, b, col: ref[b, :, pl.ds(col, DN_HEAD_DIM)]
    cq = jnp.stack([pick(c_ref, b, h * DN_HEAD_DIM) for b, h in pairs])
    ck = jnp.stack([pick(c_ref, b, DN_WIDTH + h * DN_HEAD_DIM) for b, h in pairs])
    cv = jnp.stack([pick(c_ref, b, 2 * DN_WIDTH + h * DN_HEAD_DIM) for b, h in pairs])
    z = jnp.stack([pick(z_ref, b, h * DN_HEAD_DIM) for b, h in pairs])
    return cq, ck, cv, z


def _dn_weight_specs():
    return [_whole((CONV_K, 3 * DN_WIDTH)), _whole((1, GATE_PAD)), _whole((1, GATE_PAD)), _whole((1, DN_HEAD_DIM))]


def _dn_fwd(conv_out, zg, logits, alog, dtb, og):
    nb, s, _ = conv_out.shape
    nc = s // DN_CHUNK
    pairs = _dn_pairs(nb)
    gn = len(pairs)
    chunk = lambda w: pl.BlockSpec((nb, DN_CHUNK, w), lambda n: (0, n, 0))

    def body(c_ref, z_ref, l_ref, alog_ref, dtb_ref, og_ref, out_ref, st_ref, state_ref):
        n = pl.program_id(0)

        @pl.when(n == 0)
        def _():
            state_ref[...] = jnp.zeros_like(state_ref)

        cq, ck, cv, z = _dn_batch_args(c_ref, z_ref)
        state = state_ref[...]
        st_ref[...] = state
        out, new_state = _dn_core(cq, ck, cv, z, l_ref[...], state, alog_ref[...], dtb_ref[...], og_ref[...])
        state_ref[...] = new_state
        for i, (b, h) in enumerate(pairs):
            out_ref[b, :, pl.ds(h * DN_HEAD_DIM, DN_HEAD_DIM)] = out[i].astype(out_ref.dtype)

    return pl.pallas_call(
        body, name="deltanet_fwd", grid=(nc,),
        out_shape=(jax.ShapeDtypeStruct((nb, s, DN_WIDTH), BF16),
                   jax.ShapeDtypeStruct((nc, gn, DN_HEAD_DIM, DN_HEAD_DIM), F32)),
        in_specs=[chunk(3 * DN_WIDTH), chunk(DN_WIDTH), chunk(GATE_PAD)] + _dn_weight_specs()[1:],
        out_specs=(chunk(DN_WIDTH), pl.BlockSpec((None, gn, DN_HEAD_DIM, DN_HEAD_DIM), lambda n: (n, 0, 0, 0))),
        scratch_shapes=[pltpu.VMEM((gn, DN_HEAD_DIM, DN_HEAD_DIM), F32)],
        compiler_params=_params(1),
    )(conv_out, zg, logits, alog, dtb, og)


def _dn_bwd(qkv, conv_out, zg, logits, conv_w, alog, dtb, og, states, d_out, head_grads):
    nb, s, _ = qkv.shape
    nc = s // DN_CHUNK
    rev = lambda n: nc - 1 - n
    pairs = _dn_pairs(nb)
    gn = len(pairs)
    ng = len(head_grads)

    def body(cur_ref, c_ref, z_ref, l_ref, w_ref, alog_ref, dtb_ref, og_ref, st_ref, do_ref, *rest):
        grad_refs, rest = rest[:ng], rest[ng:]
        dqkv_ref, dz_ref, dl_ref, dw_ref, dalog_ref, ddtb_ref, dog_ref = rest[:7]
        recv_refs, (dstate_ref, dcpad_ref, send_sems, recv_sems, local_sems) = rest[7:7 + ng], rest[7 + ng:]
        n = pl.program_id(0)
        start_exchange, wait_exchange = _direct_exchange(grad_refs, recv_refs, send_sems, recv_sems, local_sems, False)
        pl.when(n == 0)(start_exchange)

        @pl.when(n == 0)
        def _():
            dw_ref[...] = jnp.zeros_like(dw_ref)
            dalog_ref[...] = jnp.zeros_like(dalog_ref)
            ddtb_ref[...] = jnp.zeros_like(ddtb_ref)
            dog_ref[...] = jnp.zeros_like(dog_ref)
            dstate_ref[...] = jnp.zeros_like(dstate_ref)
            dcpad_ref[:, DN_CHUNK:, :] = jnp.zeros((nb, CONV_HALO, 3 * DN_WIDTH), F32)

        cq, ck, cv, z = _dn_batch_args(c_ref, z_ref)
        d_out_g = jnp.stack([do_ref[b, :, pl.ds(h * DN_HEAD_DIM, DN_HEAD_DIM)] for b, h in pairs])
        _, pull = jax.vjp(_dn_core, cq, ck, cv, z, l_ref[...], st_ref[...], alog_ref[...], dtb_ref[...], og_ref[...])
        dcq, dck, dcv, dz, dlog, dstate, dalog, ddtb, dog = pull((d_out_g, dstate_ref[...]))
        dstate_ref[...] = dstate
        dl_ref[...] = dlog.astype(dl_ref.dtype)
        dalog_ref[...] += dalog
        ddtb_ref[...] += ddtb
        dog_ref[...] += dog
        for i, (b, h) in enumerate(pairs):
            dcpad_ref[b, 0:DN_CHUNK, pl.ds(h * DN_HEAD_DIM, DN_HEAD_DIM)] = dcq[i]
            dcpad_ref[b, 0:DN_CHUNK, pl.ds(DN_WIDTH + h * DN_HEAD_DIM, DN_HEAD_DIM)] = dck[i]
            dcpad_ref[b, 0:DN_CHUNK, pl.ds(2 * DN_WIDTH + h * DN_HEAD_DIM, DN_HEAD_DIM)] = dcv[i]
            dz_ref[b, :, pl.ds(h * DN_HEAD_DIM, DN_HEAD_DIM)] = dz[i].astype(dz_ref.dtype)
        for b in range(nb):
            xb = cur_ref[b]
            dx = None
            for j in range(CONV_K):
                shifted = dcpad_ref[b, pl.ds(CONV_K - 1 - j, DN_CHUNK), :]
                term = w_ref[j:j + 1, :] * shifted
                dx = term if dx is None else dx + term
                dw_ref[j:j + 1, :] += _rowsum(shifted * xb)
            dqkv_ref[b] = dx.astype(dqkv_ref.dtype)
            dcpad_ref[b, DN_CHUNK:, :] = dcpad_ref[b, 0:CONV_HALO, :]
        pl.when(n == nc - 1)(wait_exchange)

    chunk = lambda w: pl.BlockSpec((nb, DN_CHUNK, w), lambda n: (0, rev(n), 0))
    return pl.pallas_call(
        body, name="deltanet_bwd", grid=(nc,),
        out_shape=(jax.ShapeDtypeStruct((nb, s, 3 * DN_WIDTH), BF16), jax.ShapeDtypeStruct((nb, s, DN_WIDTH), BF16),
                   jax.ShapeDtypeStruct((nb, s, GATE_PAD), BF16), jax.ShapeDtypeStruct((CONV_K, 3 * DN_WIDTH), F32),
                   jax.ShapeDtypeStruct((1, GATE_PAD), F32), jax.ShapeDtypeStruct((1, GATE_PAD), F32),
                   jax.ShapeDtypeStruct((1, DN_HEAD_DIM), F32))
        + tuple(jax.ShapeDtypeStruct(a.shape, a.dtype) for a in head_grads),
        in_specs=[chunk(3 * DN_WIDTH), chunk(3 * DN_WIDTH), chunk(DN_WIDTH), chunk(GATE_PAD)] + _dn_weight_specs() + [
            pl.BlockSpec((None, gn, DN_HEAD_DIM, DN_HEAD_DIM), lambda n: (rev(n), 0, 0, 0)),
            chunk(DN_WIDTH)] + [HBM_SPEC] * ng,
        out_specs=(chunk(3 * DN_WIDTH), chunk(DN_WIDTH), chunk(GATE_PAD), _whole((CONV_K, 3 * DN_WIDTH)),
                   _whole((1, GATE_PAD)), _whole((1, GATE_PAD)), _whole((1, DN_HEAD_DIM))) + (HBM_SPEC,) * ng,
        scratch_shapes=[pltpu.VMEM((gn, DN_HEAD_DIM, DN_HEAD_DIM), F32),
                        pltpu.VMEM((nb, DN_CHUNK + CONV_HALO, 3 * DN_WIDTH), F32)] + _exchange_scratch(ng),
        compiler_params=_params(1),
    )(qkv, conv_out, zg, logits, conv_w, alog, dtb, og, states, d_out, *head_grads)


def _head(a_out, b_out, x2, p2, target, w_out, w_out_t, w_gate, w_gate_t, w_proj, ple_g, fin_g):
    t = x2.shape[0]
    tm = min(512, t)
    steps = t // tm

    def body(a_ref, b_ref, x_ref, p_ref, y_ref, wo_ref, wot_ref, wg_ref, wgt_ref, wp_ref, pg_ref, fg_ref,
             da_ref, db_ref, dh_ref, dwo_hbm, dwg_hbm, dwp_hbm, dpg_ref, dfg_ref, loss_ref,
             dwo_acc, dwg_acc, dwp_acc, rows_stage, cols_stage):
        i = pl.program_id(0)

        @pl.when(i == 0)
        def _():
            dwo_acc[...] = jnp.zeros_like(dwo_acc)
            dwg_acc[...] = jnp.zeros_like(dwg_acc)
            dwp_acc[...] = jnp.zeros_like(dwp_acc)
            dpg_ref[...] = jnp.zeros_like(dpg_ref)
            dfg_ref[...] = jnp.zeros_like(dfg_ref)
            loss_ref[...] = jnp.zeros_like(loss_ref)

        a = a_ref[...]
        bb = b_ref[...]
        pb = p_ref[...].astype(BF16)
        pg = pg_ref[...]
        fg = fg_ref[...]
        h1 = (x_ref[...] + jnp.dot(a, wo_ref[0:SGU_WIDTH, :], preferred_element_type=F32)
              + jnp.dot(bb, wo_ref[SGU_WIDTH:, :], preferred_element_type=F32))
        n1, r1 = _rms(h1)
        rn = (n1 * pg).astype(BF16)
        gate = jax.nn.sigmoid(jnp.dot(rn, wg_ref[...], preferred_element_type=F32))
        pp = jnp.dot(pb, wp_ref[...], preferred_element_type=F32)
        h2 = h1 + gate * pp
        n2, r2 = _rms(h2)
        err = n2 * fg - y_ref[...]
        loss_ref[...] += jnp.broadcast_to(_rowsum(jnp.sum(err * err, axis=-1, keepdims=True)), loss_ref.shape)

        dy = err * (1.0 / D_MODEL)
        dfg_ref[...] += _rowsum(dy * n2)
        dh2 = _rms_bwd(dy * fg, n2, r2)
        dpp = (dh2 * gate).astype(BF16)
        dgl = (dh2 * pp * gate * (1.0 - gate)).astype(BF16)
        dwp_acc[...] += lax.dot_general(pb, dpp, (((0,), (0,)), ((), ())), preferred_element_type=F32)
        dwg_acc[...] += lax.dot_general(rn, dgl, (((0,), (0,)), ((), ())), preferred_element_type=F32)
        drn = jnp.dot(dgl, wgt_ref[...], preferred_element_type=F32)
        dpg_ref[...] += _rowsum(drn * n1)
        dh1 = dh2 + _rms_bwd(drn * pg, n1, r1)
        dh_ref[...] = dh1
        dhb = dh1.astype(BF16)
        da_ref[...] = jnp.dot(dhb, wot_ref[:, 0:SGU_WIDTH], preferred_element_type=F32)
        db_ref[...] = jnp.dot(dhb, wot_ref[:, SGU_WIDTH:], preferred_element_type=F32)
        dwo_acc[0:SGU_WIDTH, :] += lax.dot_general(a, dhb, (((0,), (0,)), ((), ())), preferred_element_type=F32)
        dwo_acc[SGU_WIDTH:, :] += lax.dot_general(bb, dhb, (((0,), (0,)), ((), ())), preferred_element_type=F32)

        @pl.when(i == steps - 1)
        def _():
            for j in range(N_DEV):
                for acc, hbm in ((dwo_acc, dwo_hbm), (dwg_acc, dwg_hbm)):
                    rows_stage[...] = acc[j * LANES:(j + 1) * LANES, :].astype(BF16)
                    pltpu.sync_copy(rows_stage, hbm.at[j])
                cols_stage[...] = dwp_acc[:, j * LANES:(j + 1) * LANES].astype(BF16)
                pltpu.sync_copy(cols_stage, dwp_hbm.at[j])

    tile = lambda w: pl.BlockSpec((tm, w), lambda i: (i, 0))
    return pl.pallas_call(
        body, name="head_fwd_bwd", grid=(steps,),
        out_shape=(jax.ShapeDtypeStruct((t, SGU_WIDTH), F32), jax.ShapeDtypeStruct((t, DN_WIDTH), F32),
                   jax.ShapeDtypeStruct((t, D_MODEL), F32), jax.ShapeDtypeStruct((N_DEV, LANES, D_MODEL), BF16),
                   jax.ShapeDtypeStruct((N_DEV, LANES, D_MODEL), BF16), jax.ShapeDtypeStruct((N_DEV, PLE_DIM, LANES), BF16),
                   jax.ShapeDtypeStruct((1, D_MODEL), F32), jax.ShapeDtypeStruct((1, D_MODEL), F32),
                   jax.ShapeDtypeStruct((8, LANES), F32)),
        in_specs=[tile(SGU_WIDTH), tile(DN_WIDTH), tile(D_MODEL), tile(PLE_DIM), tile(D_MODEL),
                  VMEM_SPEC, VMEM_SPEC, VMEM_SPEC, VMEM_SPEC, VMEM_SPEC, _whole((1, D_MODEL)), _whole((1, D_MODEL))],
        out_specs=(tile(SGU_WIDTH), tile(DN_WIDTH), tile(D_MODEL), HBM_SPEC, HBM_SPEC, HBM_SPEC,
                   _whole((1, D_MODEL)), _whole((1, D_MODEL)), _whole((8, LANES))),
        scratch_shapes=[pltpu.VMEM((D_MODEL, D_MODEL), F32), pltpu.VMEM((D_MODEL, D_MODEL), F32),
                        pltpu.VMEM((PLE_DIM, D_MODEL), F32), pltpu.VMEM((LANES, D_MODEL), BF16),
                        pltpu.VMEM((PLE_DIM, LANES), BF16)],
        compiler_params=_params(1),
    )(a_out, b_out, x2, p2, target, w_out, w_out_t, w_gate, w_gate_t, w_proj, ple_g, fin_g)


def _inproj_bwd(x2, dh1, a_uvz, d_sgu, d_q, d_z, d_l, norm_g, sgu_weights, wat, wqt, wzt, wgt):
    t = x2.shape[0]
    tm = min(256, t)
    steps = t // tm

    widths = (a_uvz.shape[1], d_q.shape[1], d_z.shape[1], d_l.shape[1])
    starts = (0, widths[0], widths[0] + widths[1], widths[0] + widths[1] + widths[2])

    def body(x_ref, dh_ref, uvz_ref, dsgu_ref, dq_ref, dz_ref, dl_ref, g_ref, lg_ref, lb_ref, ws_ref, bt_ref,
             wat_ref, wqt_ref, wzt_ref, wgt_ref,
             dx_ref, dw_hbm, dg_ref, dlg_ref, dlb_ref, dws_ref, dbt_ref, dw_acc, stage_ref, da_ref):
        i = pl.program_id(0)

        @pl.when(i == 0)
        def _():
            dw_acc[...] = jnp.zeros_like(dw_acc)
            for ref in (dg_ref, dlg_ref, dlb_ref, dws_ref, dbt_ref):
                ref[...] = jnp.zeros_like(ref)

        _sgu_bwd_tile(uvz_ref, dsgu_ref, (lg_ref, lb_ref, ws_ref, bt_ref), da_ref, (dlg_ref, dlb_ref, dws_ref, dbt_ref))
        g = g_ref[...]
        n, r = _rms(x_ref[...])
        xn = (n * g).astype(BF16)
        dxn = None
        for d_ref, wt_ref, col0 in zip((da_ref, dq_ref, dz_ref, dl_ref), (wat_ref, wqt_ref, wzt_ref, wgt_ref), starts):
            term = jnp.dot(d_ref[...], wt_ref[...], preferred_element_type=F32)
            dxn = term if dxn is None else dxn + term
            width = d_ref.shape[1]
            for c0 in range(0, width, 512):
                c1 = min(c0 + 512, width)
                dw_acc[col0 + c0:col0 + c1, :] += lax.dot_general(d_ref[:, c0:c1], xn, (((0,), (0,)), ((), ())),
                                                                  preferred_element_type=F32)
        dg_ref[...] += _rowsum(dxn * n)
        dx_ref[...] = dh_ref[...] + _rms_bwd(dxn * g, n, r)

        @pl.when(i == steps - 1)
        def _():
            for j in range(N_DEV):
                stage_ref[...] = dw_acc[j * IN_SHARD:(j + 1) * IN_SHARD, :]
                pltpu.sync_copy(stage_ref, dw_hbm.at[j])

    tile = lambda w: pl.BlockSpec((tm, w), lambda i: (i, 0))
    sgu_shapes = ((1, SGU_WIDTH), (1, SGU_WIDTH), (SGU_GROUPS, SGU_CHUNK, SGU_CHUNK), (SGU_CHUNK, SGU_GROUPS))
    return pl.pallas_call(
        body, name="inproj_sgu_bwd", grid=(steps,),
        out_shape=(jax.ShapeDtypeStruct((t, D_MODEL), F32), jax.ShapeDtypeStruct((N_DEV, IN_SHARD, D_MODEL), F32),
                   jax.ShapeDtypeStruct((1, D_MODEL), F32)) + tuple(jax.ShapeDtypeStruct(s, F32) for s in sgu_shapes),
        in_specs=[tile(D_MODEL), tile(D_MODEL), tile(widths[0]), tile(SGU_WIDTH)] + [tile(w) for w in widths[1:]]
        + [_whole((1, D_MODEL))] + [_whole(s) for s in sgu_shapes] + [VMEM_SPEC] * 4,
        out_specs=(tile(D_MODEL), HBM_SPEC, _whole((1, D_MODEL))) + tuple(_whole(s) for s in sgu_shapes),
        scratch_shapes=[pltpu.VMEM((sum(widths), D_MODEL), F32), pltpu.VMEM((IN_SHARD, D_MODEL), F32),
                        pltpu.VMEM((tm, widths[0]), BF16)],
        compiler_params=_params(1),
    )(x2, dh1, a_uvz, d_sgu, d_q, d_z, d_l, norm_g, *sgu_weights, wat, wqt, wzt, wgt)


def _reduce_adamw(recv, w, m, v, name, col_block=None):
    n, rows, cols = recv.shape
    cb = col_block or cols
    lead = w.ndim - 2

    def body(r_ref, w_ref, m_ref, v_ref, g_ref, d_ref, nm_ref, nv_ref):
        g = r_ref[0].astype(F32)
        for i in range(1, n):
            g = g + r_ref[i].astype(F32)
        m_new = ADAM_B1 * m_ref[...] + (1.0 - ADAM_B1) * g
        v_new = ADAM_B2 * v_ref[...] + (1.0 - ADAM_B2) * jnp.square(g)
        m_hat = m_new / (1.0 - ADAM_B1 ** ADAM_STEP)
        v_hat = v_new / (1.0 - ADAM_B2 ** ADAM_STEP)
        g_ref[...] = g
        d_ref[...] = -ADAM_LR * (m_hat / (jnp.sqrt(v_hat) + ADAM_EPS) + ADAM_WD * w_ref[...])
        nm_ref[...] = m_new
        nv_ref[...] = v_new

    blk = pl.BlockSpec((None,) * lead + (rows, cb), lambda i: (0,) * lead + (0, i))
    return pl.pallas_call(
        body, name=name, grid=(cols // cb,),
        out_shape=tuple(jax.ShapeDtypeStruct(w.shape, F32) for _ in range(4)),
        in_specs=[pl.BlockSpec((n, rows, cb), lambda i: (0, 0, i)), blk, blk, blk],
        out_specs=(blk, blk, blk, blk),
        compiler_params=_params(1),
    )(recv, w, m, v)


def _adamw_replicated(received, ws, ms, vs):
    nw = len(ws)
    starts = [sum(SMALL_PIECE_ROWS[:i]) for i in range(len(SMALL_PIECE_ROWS))]

    def natural(g_ref, row0, shape):
        cols, rows = shape[-1], _size(shape[:-1])
        if cols == LANES:
            return g_ref[row0:row0 + rows, :].reshape(shape)
        if cols < LANES:
            return g_ref[row0:row0 + 1, 0:cols].reshape(shape)
        per = cols // LANES
        return jnp.concatenate(
            [jnp.concatenate([g_ref[row0 + r * per + k:row0 + r * per + k + 1, :] for k in range(per)], axis=1)
             for r in range(rows)], axis=0).reshape(shape)

    def body(r_ref, *refs):
        w_refs, m_refs, v_refs = refs[:nw], refs[nw:2 * nw], refs[2 * nw:3 * nw]
        conv_ref, loss_ref = refs[3 * nw], refs[3 * nw + 1]
        out_refs, g_ref = refs[3 * nw + 2:-1], refs[-1]
        g = r_ref[0]
        for q in range(1, N_CHIPS):
            g = g + r_ref[q]
        g_ref[...] = g
        conv_ref[...] = natural(g_ref, starts[0], (CONV_K, 3 * DN_WIDTH))
        loss_ref[...] = natural(g_ref, starts[-1], (1, 1))
        for i in range(nw):
            gi = natural(g_ref, starts[1 + i], w_refs[i].shape)
            m_new = ADAM_B1 * m_refs[i][...] + (1.0 - ADAM_B1) * gi
            v_new = ADAM_B2 * v_refs[i][...] + (1.0 - ADAM_B2) * jnp.square(gi)
            m_hat = m_new / (1.0 - ADAM_B1 ** ADAM_STEP)
            v_hat = v_new / (1.0 - ADAM_B2 ** ADAM_STEP)
            out_refs[4 * i][...] = gi
            out_refs[4 * i + 1][...] = -ADAM_LR * (m_hat / (jnp.sqrt(v_hat) + ADAM_EPS) + ADAM_WD * w_refs[i][...])
            out_refs[4 * i + 2][...] = m_new
            out_refs[4 * i + 3][...] = v_new

    def spec(a):
        lead = max(a.ndim - 3, 0)
        return pl.BlockSpec((None,) * lead + a.shape[lead:], lambda: (0,) * a.ndim)

    weight_specs = [spec(a) for a in ws]
    return pl.pallas_call(
        body, name="adamw_replicated",
        out_shape=(jax.ShapeDtypeStruct((CONV_K, 3 * DN_WIDTH), F32), jax.ShapeDtypeStruct((1, 1), F32))
        + tuple(jax.ShapeDtypeStruct(a.shape, F32) for a in ws for _ in range(4)),
        in_specs=[pl.BlockSpec(received.shape, lambda: (0, 0, 0))] + weight_specs * 3,
        out_specs=(pl.BlockSpec((CONV_K, 3 * DN_WIDTH), lambda: (0, 0)), pl.BlockSpec((1, 1), lambda: (0, 0)))
        + tuple(s for s in weight_specs for _ in range(4)),
        scratch_shapes=[pltpu.VMEM(received.shape[1:], F32)],
        compiler_params=pltpu.CompilerParams(vmem_limit_bytes=VMEM_LIMIT),
    )(received, *ws, *ms, *vs)


def _pack_rows(pieces, rows):
    padded = [jnp.pad(jnp.ravel(p), (0, -p.size % LANES)) for p in pieces]
    flat = jnp.concatenate(padded)
    return jnp.pad(flat, (0, rows * LANES - flat.shape[0])).reshape(rows, LANES)


def kernel(x, p, norm_g, w_in, sgu_ln_g, sgu_ln_b, sgu_w_s, sgu_b_s, dn_conv_w, dn_a_log, dn_dt_bias, dn_o_norm_g, w_out, ple_norm_g, ple_gate_w, ple_proj_w, final_norm_g, loss_target, m_norm_g, m_w_in, m_sgu_ln_g, m_sgu_ln_b, m_sgu_w_s, m_sgu_b_s, m_dn_conv_w, m_dn_a_log, m_dn_dt_bias, m_dn_o_norm_g, m_w_out, m_ple_norm_g, m_ple_gate_w, m_ple_proj_w, m_final_norm_g, v_norm_g, v_w_in, v_sgu_ln_g, v_sgu_ln_b, v_sgu_w_s, v_sgu_b_s, v_dn_conv_w, v_dn_a_log, v_dn_dt_bias, v_dn_o_norm_g, v_w_out, v_ple_norm_g, v_ple_gate_w, v_ple_proj_w, v_final_norm_g):
    weights = dict(norm_g=norm_g, w_in=w_in, sgu_ln_g=sgu_ln_g, sgu_ln_b=sgu_ln_b, sgu_w_s=sgu_w_s, sgu_b_s=sgu_b_s,
                   dn_conv_w=dn_conv_w, dn_a_log=dn_a_log, dn_dt_bias=dn_dt_bias, dn_o_norm_g=dn_o_norm_g, w_out=w_out,
                   ple_norm_g=ple_norm_g, ple_gate_w=ple_gate_w, ple_proj_w=ple_proj_w, final_norm_g=final_norm_g)
    mom1 = dict(norm_g=m_norm_g, w_in=m_w_in, sgu_ln_g=m_sgu_ln_g, sgu_ln_b=m_sgu_ln_b, sgu_w_s=m_sgu_w_s,
                sgu_b_s=m_sgu_b_s, dn_conv_w=m_dn_conv_w, dn_a_log=m_dn_a_log, dn_dt_bias=m_dn_dt_bias,
                dn_o_norm_g=m_dn_o_norm_g, w_out=m_w_out, ple_norm_g=m_ple_norm_g, ple_gate_w=m_ple_gate_w,
                ple_proj_w=m_ple_proj_w, final_norm_g=m_final_norm_g)
    mom2 = dict(norm_g=v_norm_g, w_in=v_w_in, sgu_ln_g=v_sgu_ln_g, sgu_ln_b=v_sgu_ln_b, sgu_w_s=v_sgu_w_s,
                sgu_b_s=v_sgu_b_s, dn_conv_w=v_dn_conv_w, dn_a_log=v_dn_a_log, dn_dt_bias=v_dn_dt_bias,
                dn_o_norm_g=v_dn_o_norm_g, w_out=v_w_out, ple_norm_g=v_ple_norm_g, ple_gate_w=v_ple_gate_w,
                ple_proj_w=v_ple_proj_w, final_norm_g=v_final_norm_g)
    nb, s, _ = x.shape
    t = nb * s

    transposed = lambda a: jnp.transpose(a, (2, 0, 1)).reshape(IN_SHARD, D_MODEL)
    w_in_t, m_in_t, v_in_t = transposed(w_in), transposed(m_w_in), transposed(v_w_in)
    w_in_blocks, conv_blocks = _all_gather([w_in_t.astype(BF16), dn_conv_w[0]])
    w_in_full_t = w_in_blocks.reshape(IN_COLS, D_MODEL)
    wat = w_in_full_t[:3 * SGU_WIDTH]
    wqt = w_in_full_t[3 * SGU_WIDTH:3 * SGU_WIDTH + 3 * DN_WIDTH]
    wzt = w_in_full_t[3 * SGU_WIDTH + 3 * DN_WIDTH:3 * SGU_WIDTH + 4 * DN_WIDTH]
    wgt = jnp.pad(w_in_full_t[3 * SGU_WIDTH + 4 * DN_WIDTH:], ((0, GATE_PAD - 2 * DN_HEADS), (0, 0)))
    conv_full = jnp.moveaxis(conv_blocks, 0, 1).reshape(CONV_K, 3 * DN_WIDTH)
    later_shards = [w_out[0].astype(BF16), ple_gate_w[0].astype(BF16), ple_proj_w[0].astype(BF16)]

    pad_row = lambda a: jnp.pad(a.reshape(1, -1), ((0, 0), (DN_HEADS, GATE_PAD - DN_HEADS - a.size)))
    alog, dtb = pad_row(dn_a_log), pad_row(dn_dt_bias)
    og = dn_o_norm_g.reshape(1, DN_HEAD_DIM)
    ws = sgu_w_s.reshape(SGU_GROUPS, SGU_CHUNK, SGU_CHUNK)
    b_t = sgu_b_s.reshape(SGU_GROUPS, SGU_CHUNK).T
    fin_g = final_norm_g.reshape(1, D_MODEL)

    x2 = x.reshape(t, D_MODEL)
    sgu_weights = (sgu_ln_g, sgu_ln_b, ws, b_t)
    a_uvz, b_qkv, b_z, b_l, a_out, conv_out, w_out_blocks, w_gate_blocks, w_proj_blocks = _inproj_fwd(
        x2, s, norm_g, wat, wqt, wzt, wgt, sgu_weights, conv_full, later_shards)
    w_out_full = w_out_blocks.reshape(D_MODEL, D_MODEL)
    w_gate_full = w_gate_blocks.reshape(D_MODEL, D_MODEL)
    w_proj_full = jnp.moveaxis(w_proj_blocks, 0, 1).reshape(PLE_DIM, D_MODEL)
    qkv3 = b_qkv.reshape(nb, s, 3 * DN_WIDTH)
    conv_out = conv_out.reshape(nb, s, 3 * DN_WIDTH)
    z3 = b_z.reshape(nb, s, DN_WIDTH)
    l3 = b_l.reshape(nb, s, GATE_PAD)
    b_out, states = _dn_fwd(conv_out, z3, l3, alog, dtb, og)

    d_a, d_b, dh1, g_w_out, g_gate, g_proj, g_ple_g, g_fin_g, loss_tile = _head(
        a_out, b_out.reshape(t, DN_WIDTH), x2, p.reshape(t, PLE_DIM), loss_target.reshape(t, D_MODEL),
        w_out_full, w_out_full.T, w_gate_full, w_gate_full.T, w_proj_full, ple_norm_g, fin_g)
    d_qkv, d_z, d_l, g_conv, g_alog, g_dtb, g_og, *head_received = _dn_bwd(
        qkv3, conv_out, z3, l3, conv_full, alog, dtb, og, states, d_b.reshape(nb, s, DN_WIDTH),
        [g_w_out, g_gate, g_proj])
    grad_x, g_w_in, g_norm, g_ln_g, g_ln_b, g_ws, g_bt = _inproj_bwd(
        x2, dh1, a_uvz, d_a, d_qkv.reshape(t, 3 * DN_WIDTH), d_z.reshape(t, DN_WIDTH), d_l.reshape(t, GATE_PAD),
        norm_g, sgu_weights, wat, wqt, wzt, wgt)

    by_device = [g_w_in]
    small = _pack_rows([g_conv, g_norm, g_ln_g, g_ln_b, g_ws, g_bt.T, g_alog[:, DN_HEADS:2 * DN_HEADS], g_dtb[:, DN_HEADS:2 * DN_HEADS], g_og,
                        g_ple_g, g_fin_g, (0.5 / D_MODEL) * loss_tile[0:1, 0:1]], SMALL_ROWS)
    *from_sibling, small_sibling = _sibling_exchange(by_device, small)
    core = lax.axis_index("c").astype(jnp.int32).reshape(1)
    *chip_sums, small_sum = _pair_sum(core, by_device, from_sibling, small, small_sibling)
    *received, small_received = _chip_exchange(chip_sums, small_sum)

    results = {}
    outs = _reduce_adamw(received[0], w_in_t, m_in_t, v_in_t, "adamw_w_in", LANES)
    results["w_in"] = [jnp.transpose(a.reshape(IN_SHARD, 1, D_MODEL), (1, 2, 0)) for a in outs]
    for name, recv in zip(("w_out", "ple_gate_w", "ple_proj_w"), head_received):
        results[name] = _reduce_adamw(recv, weights[name], mom1[name], mom2[name], "adamw_" + name)
    names = [name for name, _ in REPLICATED]
    two_d = lambda a: a.reshape(1, -1) if a.ndim == 1 else a
    g_conv_sum, loss_sum, *flat_outs = _adamw_replicated(
        small_received, *[[two_d(src[k]) for k in names] for src in (weights, mom1, mom2)])
    for i, k in enumerate(names):
        results[k] = [a.reshape(weights[k].shape) for a in flat_outs[4 * i:4 * i + 4]]
    loss = loss_sum[0, 0]
    me = 4 * lax.axis_index("x") + 2 * lax.axis_index("y") + lax.axis_index("c")
    conv_mine = lax.dynamic_slice(g_conv_sum, (0, me * 192), (CONV_K, 192))
    results["dn_conv_w"] = _reduce_adamw(conv_mine[None], dn_conv_w, m_dn_conv_w, v_dn_conv_w, "adamw_dn_conv_w")

    return (loss, grad_x.reshape(nb, s, D_MODEL), *[results[k][0] for k in WEIGHT_ORDER],
            *[results[k][1] for k in WEIGHT_ORDER], *[results[k][2] for k in WEIGHT_ORDER],
            *[results[k][3] for k in WEIGHT_ORDER])
```

```python
import jax
import jax.numpy as jnp
from jax import lax
from jax.experimental import pallas as pl
from jax.experimental.pallas import tpu as pltpu

F32 = jnp.float32
BF16 = jnp.bfloat16

N_DEV = 8
D_MODEL = 1024
SGU_WIDTH = 512
SGU_GROUPS = 4
SGU_CHUNK = 128
DN_WIDTH = 512
DN_HEADS = 4
DN_HEAD_DIM = 128
DN_CHUNK = 128
CONV_K = 4
CONV_HALO = 8
PLE_DIM = 256
EPS = 1e-6
IN_COLS = 3592
IN_SHARD = IN_COLS // N_DEV
GATE_PAD = 128

ADAM_LR = 0.001
ADAM_B1 = 0.9
ADAM_B2 = 0.999
ADAM_EPS = 1e-08
ADAM_WD = 0.01
ADAM_STEP = 10

LANES = 128
VMEM_LIMIT = 56 * 1024 * 1024
MESH = pl.DeviceIdType.MESH

REPLICATED = (("norm_g", (1, D_MODEL)), ("sgu_ln_g", (1, SGU_WIDTH)), ("sgu_ln_b", (1, SGU_WIDTH)),
              ("sgu_w_s", (1, SGU_GROUPS, SGU_CHUNK, SGU_CHUNK)), ("sgu_b_s", (1, SGU_GROUPS, SGU_CHUNK)),
              ("dn_a_log", (1, DN_HEADS)), ("dn_dt_bias", (1, DN_HEADS)), ("dn_o_norm_g", (1, DN_HEAD_DIM)),
              ("ple_norm_g", (1, D_MODEL)), ("final_norm_g", (D_MODEL,)))
WEIGHT_ORDER = ("norm_g", "w_in", "sgu_ln_g", "sgu_ln_b", "sgu_w_s", "sgu_b_s", "dn_conv_w", "dn_a_log",
                "dn_dt_bias", "dn_o_norm_g", "w_out", "ple_norm_g", "ple_gate_w", "ple_proj_w", "final_norm_g")


def _size(shape):
    n = 1
    for s in shape:
        n *= s
    return n


SMALL_LAYOUT = (("conv", (CONV_K, 3 * DN_WIDTH)),) + REPLICATED + (("loss", (1,)),)
SMALL_PIECE_ROWS = tuple(-(-_size(s) // LANES) for _, s in SMALL_LAYOUT)
SMALL_ROWS = -(-sum(SMALL_PIECE_ROWS) // 8) * 8


def _bdot(a, b):
    return jnp.dot(a.astype(BF16), b.astype(BF16), preferred_element_type=F32)


def _sigmoid(x):
    return 0.5 * jnp.tanh(0.5 * x) + 0.5


@jax.custom_vjp
def _silu(x):
    return x * _sigmoid(x)


def _silu_fwd(x):
    s = _sigmoid(x)
    return x * s, (x, s)


def _silu_bwd(res, ct):
    x, s = res
    return (ct * (s * (1.0 + x * (1.0 - s))),)


_silu.defvjp(_silu_fwd, _silu_bwd)


def _normal_cdf(x):
    return 0.5 + 0.5 * lax.erf(x * (0.5 ** 0.5))


@jax.custom_vjp
def _gelu(x):
    return x * _normal_cdf(x)


def _gelu_fwd(x):
    cdf = _normal_cdf(x)
    return x * cdf, (x, cdf)


def _gelu_bwd(res, ct):
    x, cdf = res
    pdf = jnp.exp(-0.5 * x * x) * ((2.0 * jnp.pi) ** -0.5)
    return (ct * (cdf + x * pdf),)


_gelu.defvjp(_gelu_fwd, _gelu_bwd)


def _softplus(x):
    return jnp.maximum(x, 0.0) + jnp.log1p(jnp.exp(-jnp.abs(x)))


def _l2n(x):
    return x * lax.rsqrt(jnp.sum(x * x, axis=-1, keepdims=True) + EPS)


def _rms(x):
    r = lax.rsqrt(jnp.mean(x * x, axis=-1, keepdims=True) + EPS)
    return x * r, r


def _rms_bwd(dn, n, r):
    return r * (dn - n * jnp.mean(dn * n, axis=-1, keepdims=True))


def _onehot_row(idx, width):
    return (lax.broadcasted_iota(jnp.int32, (1, width), 1) == idx).astype(F32)


def _rowsum(x):
    return jnp.sum(x, axis=0, keepdims=True)


def _iota2(n):
    return lax.broadcasted_iota(jnp.int32, (n, n), 0), lax.broadcasted_iota(jnp.int32, (n, n), 1)


def _bmm(a, b):
    return lax.dot_general(a.astype(BF16), b.astype(BF16), (((2,), (1,)), ((0,), (0,))), preferred_element_type=F32)


def _bmm_nt(a, b):
    return lax.dot_general(a.astype(BF16), b.astype(BF16), (((2,), (2,)), ((0,), (0,))), preferred_element_type=F32)


def _bmm_tn(a, b):
    return lax.dot_general(a.astype(BF16), b.astype(BF16), (((1,), (1,)), ((0,), (0,))), preferred_element_type=F32)


def _tri_inv_impl(a):
    n = a.shape[-1]
    r, c = _iota2(n)
    x = r ^ c
    eye = (r == c).astype(F32)
    ad = jnp.where(x < 16, a, 0.0)
    p2 = _bmm(ad, ad)
    e = p2 - ad - _bmm(ad, p2)
    p4 = _bmm(p2, p2)
    e = e + p4 + _bmm(e, p4)
    p8 = _bmm(p4, p4)
    e = e + p8 + _bmm(e, p8)
    size = 16
    while size < n:
        m = jnp.where(jnp.logical_and(x < 2 * size, x >= size), a, 0.0)
        f = m + _bmm(m, e)
        e = e - f - _bmm(e, f)
        size *= 2
    return e + eye


@jax.custom_vjp
def _tri_inv(a, known):
    return _tri_inv_impl(a) if known is None else known


def _tri_inv_fwd(a, known):
    t = _tri_inv(a, known)
    return t, (t, known)


def _tri_inv_bwd(res, dt):
    t, known = res
    return -_bmm_tn(t, _bmm_nt(dt, t)), None if known is None else jnp.zeros_like(known)


_tri_inv.defvjp(_tri_inv_fwd, _tri_inv_bwd)


def _sgu_core(u, v, z, lg, lb, ws, bcol):
    n = ws.shape[0]
    r, c = _iota2(n)
    wm = jnp.where(r >= c, ws, 0.0)
    gu = _gelu(u)
    gv = _gelu(v)
    xc = gv - jnp.mean(gv, axis=-1, keepdims=True)
    ln = xc * lax.rsqrt(jnp.mean(xc * xc, axis=-1, keepdims=True) + EPS) * lg + lb
    s = _bdot(wm, ln) + bcol
    return gu * s * _silu(z)


def _lanes_of(x):
    return jnp.concatenate([x[i] for i in range(x.shape[0])], axis=1)


def _batch_of(x, width):
    return jnp.concatenate([x[None, :, i * width:(i + 1) * width] for i in range(x.shape[1] // width)], axis=0)


def _mask_dot(mask, x):
    hi = x.astype(BF16)
    lo = (x - hi.astype(F32)).astype(BF16)
    m = mask.astype(BF16)
    return jnp.dot(m, hi, preferred_element_type=F32) + jnp.dot(m, lo, preferred_element_type=F32)


def _tri_mask(n, upper):
    r, c = _iota2(n)
    return (r <= c) if upper else (r >= c)


@jax.custom_vjp
def _cumsum_rows(x):
    return _mask_dot(_tri_mask(x.shape[0], False), x)


def _cumsum_rows_fwd(x):
    return _cumsum_rows(x), None


def _cumsum_rows_bwd(_, ct):
    return (_mask_dot(_tri_mask(ct.shape[0], True), ct),)


_cumsum_rows.defvjp(_cumsum_rows_fwd, _cumsum_rows_bwd)


@jax.custom_vjp
def _colsum_all_rows(x):
    return _mask_dot(jnp.ones((x.shape[0], x.shape[0]), jnp.bool_), x)


def _colsum_all_rows_fwd(x):
    return _colsum_all_rows(x), None


def _colsum_all_rows_bwd(_, ct):
    return (_mask_dot(jnp.ones((ct.shape[0], ct.shape[0]), jnp.bool_), ct),)


_colsum_all_rows.defvjp(_colsum_all_rows_fwd, _colsum_all_rows_bwd)


def _dn_core(cq, ck, cv, z, logits, state, alog, dtb, og, t_known=None):
    gn, cn, dh = cq.shape
    heads = gn // logits.shape[0]
    q = _l2n(_silu(cq)) * (dh ** -0.5)
    k = _l2n(_silu(ck))
    v = _silu(cv)
    beta_lanes = _sigmoid(logits)
    g_lanes = -jnp.exp(alog) * _softplus(logits + dtb)
    column = lambda rows, lane: jnp.sum(rows * _onehot_row(lane, rows.shape[-1]), axis=-1, keepdims=True)[None]
    beta = jnp.concatenate([column(beta_lanes[i // heads], i % heads) for i in range(gn)], axis=0)
    g = jnp.concatenate([column(g_lanes[i // heads], heads + i % heads) for i in range(gn)], axis=0)
    r, c = _iota2(cn)
    tril = r >= c
    rw = lax.broadcasted_iota(jnp.int32, (cn, dh), 0)
    cw = lax.broadcasted_iota(jnp.int32, (cn, dh), 1)
    upper_wide = (rw <= cw).astype(F32)
    g_wide = jnp.broadcast_to(g, (gn, cn, dh))
    gc_wide = _batch_of(_cumsum_rows(_lanes_of(g_wide)), dh)
    gc_cols = _batch_of(_colsum_all_rows(_lanes_of(g_wide * upper_wide)), dh)[:, :, :cn]
    decay = jnp.exp(jnp.where(tril, gc_wide[:, :, :cn] - gc_cols, -1e30))
    kb = k * beta
    kk = _bmm_nt(kb, k) * decay
    t = _tri_inv(jnp.where(r > c, kk, 0.0), t_known)
    eg = jnp.exp(gc_wide)
    sol = _bmm(t, jnp.concatenate([v * beta, kb * eg], axis=-1))
    u_val, w_dec = sol[:, :, :dh], sol[:, :, dh:]
    qk = _bmm_nt(q, k) * decay
    g_last = jnp.sum(g_wide, axis=1, keepdims=True)
    k_dec = k * jnp.exp(g_last - gc_wide)
    ws = _bmm(jnp.concatenate([w_dec, q * eg], axis=1), state)
    v_new = u_val - ws[:, :cn]
    o = ws[:, cn:] + _bmm(qk, v_new)
    new_state = state * jnp.exp(g_last) + _bmm_tn(k_dec, v_new)
    on, _ = _rms(o)
    return on * og * _silu(z), new_state, t


N_CHIPS = 4
HBM_SPEC = pl.BlockSpec(memory_space=pl.ANY)


def _place():
    return lax.axis_index("x"), lax.axis_index("y"), lax.axis_index("c")


def _other_chip(k):
    x, y, _ = _place()
    px = 1 - x if k & 2 else x
    py = 1 - y if k & 1 else y
    return px, py, 2 * px + py


def _remote(src, dst, send_sem, recv_sem, device):
    return pltpu.make_async_remote_copy(src_ref=src, dst_ref=dst, send_sem=send_sem, recv_sem=recv_sem,
                                        device_id=device, device_id_type=MESH)


def _other_device(k):
    x, y, c = _place()
    px = 1 - x if k & 4 else x
    py = 1 - y if k & 2 else y
    pc = 1 - c if k & 1 else c
    return (px, py, pc), 4 * px + 2 * py + pc


def _direct_exchange(srcs, outs, send_sems, recv_sems, local_sems, gather):
    x, y, c = _place()
    me = 4 * x + 2 * y + c

    def copies(arriving):
        out_list = []
        for a, (src, out) in enumerate(zip(srcs, outs)):
            for k in range(1, N_DEV):
                peer, index = _other_device(k)
                mine = src if gather else src.at[index]
                out_list.append(_remote(mine, out.at[index if arriving else me], send_sems.at[a, k - 1],
                                        recv_sems.at[a, k - 1], peer))
        return out_list

    def local_copies():
        return [pltpu.make_async_copy(src if gather else src.at[me], out.at[me], local_sems.at[a])
                for a, (src, out) in enumerate(zip(srcs, outs))]

    def start():
        for cp in local_copies() + copies(False):
            cp.start()

    def wait():
        for cp in copies(True):
            cp.wait_recv()
        for cp in copies(False):
            cp.wait_send()
        for cp in local_copies():
            cp.wait()

    return start, wait


def _exchange_scratch(n):
    return [pltpu.SemaphoreType.DMA((n, N_DEV - 1)), pltpu.SemaphoreType.DMA((n, N_DEV - 1)), pltpu.SemaphoreType.DMA((n,))]


def _all_gather(shards):
    n = len(shards)

    def body(*refs):
        srcs, outs = refs[:n], refs[n:2 * n]
        send_sems, recv_sems, local_sems = refs[2 * n:]
        x, y, c = _place()
        me = 4 * x + 2 * y + c
        sibling = (x, y, 1 - c)
        local = [pltpu.make_async_copy(srcs[a], outs[a].at[me], local_sems.at[a]) for a in range(n)]
        for cp in local:
            cp.start()
        sends = []
        for a in range(n):
            sends.append(_remote(srcs[a], outs[a].at[me], send_sems.at[a, 0], recv_sems.at[a, 0], sibling))
        for k in range(1, N_CHIPS):
            px, py, _ = _other_chip(k)
            for a in range(n):
                sends.append(_remote(srcs[a], outs[a].at[me], send_sems.at[a, k], recv_sems.at[a, k], (px, py, c)))
        for cp in sends:
            cp.start()
        passed = []
        for k in range(1, N_CHIPS):
            px, py, _ = _other_chip(k)
            blk = 4 * px + 2 * py + c
            for a in range(n):
                _remote(srcs[a], outs[a].at[blk], send_sems.at[a, k], recv_sems.at[a, k], (px, py, c)).wait_recv()
            for a in range(n):
                cp = _remote(outs[a].at[blk], outs[a].at[blk], send_sems.at[a, 3 + k], recv_sems.at[a, 3 + k], sibling)
                cp.start()
                passed.append(cp)
        for a in range(n):
            _remote(srcs[a], outs[a].at[me + 1 - 2 * c], send_sems.at[a, 0], recv_sems.at[a, 0], sibling).wait_recv()
        for k in range(1, N_CHIPS):
            px, py, _ = _other_chip(k)
            blk = 4 * px + 2 * py + 1 - c
            for a in range(n):
                _remote(srcs[a], outs[a].at[blk], send_sems.at[a, 3 + k], recv_sems.at[a, 3 + k], sibling).wait_recv()
        for cp in sends + passed:
            cp.wait_send()
        for cp in local:
            cp.wait()

    return pl.pallas_call(
        body, name="all_gather_weights",
        out_shape=tuple(jax.ShapeDtypeStruct((N_DEV,) + a.shape, a.dtype) for a in shards),
        in_specs=[HBM_SPEC] * n, out_specs=(HBM_SPEC,) * n,
        scratch_shapes=[pltpu.SemaphoreType.DMA((n, N_DEV - 1)), pltpu.SemaphoreType.DMA((n, N_DEV - 1)),
                        pltpu.SemaphoreType.DMA((n,))],
    )(*shards)


def _sibling_exchange(by_device, small):
    n = len(by_device)

    def body(*refs):
        srcs, small_src = refs[:n], refs[n]
        outs, small_out = refs[n + 1:2 * n + 1], refs[2 * n + 1]
        send_sems, recv_sems = refs[2 * n + 2:]
        x, y, c = _place()
        sibling = (x, y, 1 - c)
        copies = [_remote(small_src, small_out, send_sems.at[n, 0], recv_sems.at[n, 0], sibling)]
        for a in range(n):
            for q in range(N_CHIPS):
                copies.append(_remote(srcs[a].at[2 * q + 1 - c], outs[a].at[q], send_sems.at[a, q], recv_sems.at[a, q],
                                      sibling))
        for cp in copies:
            cp.start()
        for cp in copies:
            cp.wait_recv()
        for cp in copies:
            cp.wait_send()

    return pl.pallas_call(
        body, name="grad_sibling_exchange",
        out_shape=tuple(jax.ShapeDtypeStruct((N_CHIPS,) + a.shape[1:], a.dtype) for a in by_device)
        + (jax.ShapeDtypeStruct(small.shape, small.dtype),),
        in_specs=[HBM_SPEC] * (n + 1), out_specs=(HBM_SPEC,) * (n + 1),
        scratch_shapes=[pltpu.SemaphoreType.DMA((n + 1, N_CHIPS)), pltpu.SemaphoreType.DMA((n + 1, N_CHIPS))],
    )(*by_device, small)


def _chip_exchange(chip_sums, small):
    n = len(chip_sums)

    def body(*refs):
        srcs, small_src = refs[:n], refs[n]
        outs, small_out = refs[n + 1:2 * n + 1], refs[2 * n + 1]
        send_sems, recv_sems, local_sems = refs[2 * n + 2:]
        x, y, c = _place()
        mine = 2 * x + y
        local = [pltpu.make_async_copy(srcs[a].at[mine], outs[a].at[mine], local_sems.at[a]) for a in range(n)]
        local.append(pltpu.make_async_copy(small_src, small_out.at[mine], local_sems.at[n]))
        for cp in local:
            cp.start()
        sends = []
        for k in range(1, N_CHIPS):
            px, py, chip = _other_chip(k)
            for a in range(n):
                sends.append(_remote(srcs[a].at[chip], outs[a].at[mine], send_sems.at[a, k - 1], recv_sems.at[a, k - 1],
                                     (px, py, c)))
            sends.append(_remote(small_src, small_out.at[mine], send_sems.at[n, k - 1], recv_sems.at[n, k - 1], (px, py, c)))
        for cp in sends:
            cp.start()
        for k in range(1, N_CHIPS):
            px, py, chip = _other_chip(k)
            for a in range(n):
                _remote(srcs[a].at[chip], outs[a].at[chip], send_sems.at[a, k - 1], recv_sems.at[a, k - 1],
                        (px, py, c)).wait_recv()
            _remote(small_src, small_out.at[chip], send_sems.at[n, k - 1], recv_sems.at[n, k - 1], (px, py, c)).wait_recv()
        for cp in sends:
            cp.wait_send()
        for cp in local:
            cp.wait()

    return pl.pallas_call(
        body, name="grad_chip_exchange",
        out_shape=tuple(jax.ShapeDtypeStruct(a.shape, a.dtype) for a in chip_sums)
        + (jax.ShapeDtypeStruct((N_CHIPS,) + small.shape, small.dtype),),
        in_specs=[HBM_SPEC] * (n + 1), out_specs=(HBM_SPEC,) * (n + 1),
        scratch_shapes=[pltpu.SemaphoreType.DMA((n + 1, N_CHIPS - 1)), pltpu.SemaphoreType.DMA((n + 1, N_CHIPS - 1)),
                        pltpu.SemaphoreType.DMA((n + 1,))],
    )(*chip_sums, small)


def _pair_sum(core, by_device, from_sibling, small, small_from_sibling):
    n = len(by_device)

    def body(core_ref, *refs):
        own, sib = refs[:n], refs[n:2 * n]
        small_own, small_sib = refs[2 * n], refs[2 * n + 1]
        outs, small_out = refs[2 * n + 2:3 * n + 2], refs[3 * n + 2]
        for a in range(n):
            outs[a][...] = (own[a][...] + sib[a][...]).astype(outs[a].dtype)
        small_out[...] = small_own[...] + small_sib[...]

    def block(a):
        return (None,) + a.shape[1:], (0,) * (a.ndim - 1)

    own_specs = [pl.BlockSpec(block(a)[0], lambda q, core_ref, z=block(a)[1]: (2 * q + core_ref[0],) + z) for a in by_device]
    sib_specs = [pl.BlockSpec(block(a)[0], lambda q, core_ref, z=block(a)[1]: (q,) + z) for a in by_device]
    small_spec = pl.BlockSpec(small.shape, lambda q, core_ref: (0,) * small.ndim)
    return pl.pallas_call(
        body, name="grad_pair_sum",
        grid_spec=pltpu.PrefetchScalarGridSpec(
            num_scalar_prefetch=1, grid=(N_CHIPS,),
            in_specs=own_specs + sib_specs + [small_spec, small_spec],
            out_specs=tuple(sib_specs) + (small_spec,)),
        out_shape=tuple(jax.ShapeDtypeStruct(a.shape, BF16) for a in from_sibling)
        + (jax.ShapeDtypeStruct(small.shape, F32),),
        compiler_params=_params(1),
    )(core, *by_device, *from_sibling, small, small_from_sibling)


def _params(n_axes):
    return pltpu.CompilerParams(dimension_semantics=("arbitrary",) * n_axes, vmem_limit_bytes=VMEM_LIMIT)


def _whole(shape):
    return pl.BlockSpec(shape, lambda *_: (0,) * len(shape))


VMEM_SPEC = pl.BlockSpec(memory_space=pltpu.VMEM)


def _inproj_fwd(x2, seq_len, norm_g, wat, wqt, wzt, wgt, sgu_weights, conv_w, later_shards):
    t = x2.shape[0]
    tm = min(512, seq_len)
    tiles_per_seq = seq_len // tm
    steps = t // tm
    ns = len(later_shards)

    def body(x_ref, g_ref, wa_ref, wq_ref, wz_ref, wg_ref, lg_ref, lb_ref, ws_ref, bt_ref, cw_ref, *rest):
        shard_refs, rest = rest[:ns], rest[ns:]
        a_ref, q_ref, z_ref, l_ref, sgu_ref, c_ref = rest[:6]
        gathered_refs, (xpad_ref, send_sems, recv_sems, local_sems) = rest[6:6 + ns], rest[6 + ns:]
        start_gather, wait_gather = _direct_exchange(shard_refs, gathered_refs, send_sems, recv_sems, local_sems, True)
        pl.when(pl.program_id(0) == 0)(start_gather)
        n, _ = _rms(x_ref[...])
        xn = (n * g_ref[...]).astype(BF16)
        for w_ref, o_ref in ((wa_ref, a_ref), (wq_ref, q_ref), (wz_ref, z_ref), (wg_ref, l_ref)):
            width = w_ref.shape[0]
            for c0 in range(0, width, 512):
                c1 = min(c0 + 512, width)
                o_ref[:, c0:c1] = lax.dot_general(xn, w_ref[c0:c1, :], (((1,), (1,)), ((), ())),
                                                  preferred_element_type=F32)
        for row0 in range(0, tm, SGU_CHUNK):
            for grp in range(SGU_GROUPS):
                args = _sgu_pieces(a_ref, lg_ref, lb_ref, ws_ref, bt_ref, row0, grp)
                sgu_ref[pl.ds(row0, SGU_CHUNK), pl.ds(grp * 128, 128)] = _sgu_core(*args).astype(sgu_ref.dtype)

        @pl.when(pl.program_id(0) % tiles_per_seq == 0)
        def _():
            xpad_ref[0:CONV_HALO, :] = jnp.zeros((CONV_HALO, xpad_ref.shape[1]), F32)

        xpad_ref[CONV_HALO:, :] = q_ref[...]
        acc = None
        for j in range(CONV_K):
            term = cw_ref[j:j + 1, :] * xpad_ref[pl.ds(CONV_HALO - CONV_K + 1 + j, tm), :]
            acc = term if acc is None else acc + term
        c_ref[...] = acc
        xpad_ref[0:CONV_HALO, :] = xpad_ref[tm:tm + CONV_HALO, :]
        pl.when(pl.program_id(0) == steps - 1)(wait_gather)

    widths = (wat.shape[0], wqt.shape[0], wzt.shape[0], wgt.shape[0])
    tile = lambda w: pl.BlockSpec((tm, w), lambda i: (i, 0))
    sgu_shapes = ((1, SGU_WIDTH), (1, SGU_WIDTH), (SGU_GROUPS, SGU_CHUNK, SGU_CHUNK), (SGU_CHUNK, SGU_GROUPS))
    return pl.pallas_call(
        body, name="inproj_sgu_conv_fwd", grid=(steps,),
        out_shape=tuple(jax.ShapeDtypeStruct((t, w), F32) for w in widths)
        + (jax.ShapeDtypeStruct((t, SGU_WIDTH), BF16), jax.ShapeDtypeStruct((t, widths[1]), F32))
        + tuple(jax.ShapeDtypeStruct((N_DEV,) + a.shape, a.dtype) for a in later_shards),
        in_specs=[tile(D_MODEL), _whole((1, D_MODEL)), VMEM_SPEC, VMEM_SPEC, VMEM_SPEC, VMEM_SPEC]
        + [_whole(s) for s in sgu_shapes] + [_whole((CONV_K, widths[1]))] + [HBM_SPEC] * ns,
        out_specs=tuple(tile(w) for w in widths) + (tile(SGU_WIDTH), tile(widths[1])) + (HBM_SPEC,) * ns,
        scratch_shapes=[pltpu.VMEM((CONV_HALO + tm, widths[1]), F32)] + _exchange_scratch(ns),
        compiler_params=_params(1),
    )(x2, norm_g, wat, wqt, wzt, wgt, *sgu_weights, conv_w, *later_shards)


def _sgu_pieces(uvz_ref, lg_ref, lb_ref, ws_ref, bt_ref, row0, grp):
    rows = pl.ds(row0, SGU_CHUNK)
    lanes = pl.ds(grp * 128, 128)
    u = uvz_ref[rows, pl.ds(grp * 128, 128)]
    v = uvz_ref[rows, pl.ds(SGU_WIDTH + grp * 128, 128)]
    z = uvz_ref[rows, pl.ds(2 * SGU_WIDTH + grp * 128, 128)]
    bcol = jnp.sum(bt_ref[...] * _onehot_row(grp, SGU_GROUPS), axis=-1, keepdims=True)
    return u, v, z, lg_ref[:, lanes], lb_ref[:, lanes], ws_ref[grp], bcol


def _sgu_bwd_tile(uvz_ref, do_ref, sgu_refs, duvz_ref, grad_refs):
    lg_ref, lb_ref, ws_ref, bt_ref = sgu_refs
    dlg_ref, dlb_ref, dws_ref, dbt_ref = grad_refs
    for row0 in range(0, uvz_ref.shape[0], SGU_CHUNK):
        rows = pl.ds(row0, SGU_CHUNK)
        for grp in range(SGU_GROUPS):
            lanes = pl.ds(grp * 128, 128)
            args = _sgu_pieces(uvz_ref, lg_ref, lb_ref, ws_ref, bt_ref, row0, grp)
            _, pull = jax.vjp(_sgu_core, *args)
            du, dv, dz, dlg, dlb, dws, dbcol = pull(do_ref[rows, lanes])
            duvz_ref[rows, pl.ds(grp * 128, 128)] = du.astype(duvz_ref.dtype)
            duvz_ref[rows, pl.ds(SGU_WIDTH + grp * 128, 128)] = dv.astype(duvz_ref.dtype)
            duvz_ref[rows, pl.ds(2 * SGU_WIDTH + grp * 128, 128)] = dz.astype(duvz_ref.dtype)
            dlg_ref[:, lanes] += dlg
            dlb_ref[:, lanes] += dlb
            dws_ref[grp] += dws
            dbt_ref[...] += dbcol * _onehot_row(grp, SGU_GROUPS)


def _dn_pairs(nb):
    return [(b, h) for b in range(nb) for h in range(DN_HEADS)]


def _dn_batch_args(c_ref, z_ref):
    pairs = _dn_pairs(c_ref.shape[0])
    pick = lambda ref, b, col: ref[b, :, pl.ds(col, DN_HEAD_DIM)]
    cq = jnp.stack([pick(c_ref, b, h * DN_HEAD_DIM) for b, h in pairs])
    ck = jnp.stack([pick(c_ref, b, DN_WIDTH + h * DN_HEAD_DIM) for b, h in pairs])
    cv = jnp.stack([pick(c_ref, b, 2 * DN_WIDTH + h * DN_HEAD_DIM) for b, h in pairs])
    z = jnp.stack([pick(z_ref, b, h * DN_HEAD_DIM) for b, h in pairs])
    return cq, ck, cv, z


def _dn_weight_specs():
    return [_whole((CONV_K, 3 * DN_WIDTH)), _whole((1, GATE_PAD)), _whole((1, GATE_PAD)), _whole((1, DN_HEAD_DIM))]


def _dn_fwd(conv_out, zg, logits, alog, dtb, og):
    nb, s, _ = conv_out.shape
    nc = s // DN_CHUNK
    pairs = _dn_pairs(nb)
    gn = len(pairs)
    chunk = lambda w: pl.BlockSpec((nb, DN_CHUNK, w), lambda n: (0, n, 0))

    def body(c_ref, z_ref, l_ref, alog_ref, dtb_ref, og_ref, out_ref, st_ref, inv_ref, state_ref):
        n = pl.program_id(0)

        @pl.when(n == 0)
        def _():
            state_ref[...] = jnp.zeros_like(state_ref)

        cq, ck, cv, z = _dn_batch_args(c_ref, z_ref)
        state = state_ref[...]
        st_ref[...] = state
        out, new_state, t = _dn_core(cq, ck, cv, z, l_ref[...], state, alog_ref[...], dtb_ref[...], og_ref[...])
        state_ref[...] = new_state
        inv_ref[...] = t.astype(inv_ref.dtype)
        for i, (b, h) in enumerate(pairs):
            out_ref[b, :, pl.ds(h * DN_HEAD_DIM, DN_HEAD_DIM)] = out[i].astype(out_ref.dtype)

    per_chunk = pl.BlockSpec((None, gn, DN_HEAD_DIM, DN_HEAD_DIM), lambda n: (n, 0, 0, 0))
    return pl.pallas_call(
        body, name="deltanet_fwd", grid=(nc,),
        out_shape=(jax.ShapeDtypeStruct((nb, s, DN_WIDTH), BF16),
                   jax.ShapeDtypeStruct((nc, gn, DN_HEAD_DIM, DN_HEAD_DIM), F32),
                   jax.ShapeDtypeStruct((nc, gn, DN_CHUNK, DN_CHUNK), BF16)),
        in_specs=[chunk(3 * DN_WIDTH), chunk(DN_WIDTH), chunk(GATE_PAD)] + _dn_weight_specs()[1:],
        out_specs=(chunk(DN_WIDTH), per_chunk, pl.BlockSpec((None, gn, DN_CHUNK, DN_CHUNK), lambda n: (n, 0, 0, 0))),
        scratch_shapes=[pltpu.VMEM((gn, DN_HEAD_DIM, DN_HEAD_DIM), F32)],
        compiler_params=_params(1),
    )(conv_out, zg, logits, alog, dtb, og)


def _dn_bwd(qkv, conv_out, zg, logits, conv_w, alog, dtb, og, states, inverses, d_out, head_grads):
    nb, s, _ = qkv.shape
    nc = s // DN_CHUNK
    rev = lambda n: nc - 1 - n
    pairs = _dn_pairs(nb)
    gn = len(pairs)
    ng = len(head_grads)

    def body(cur_ref, c_ref, z_ref, l_ref, w_ref, alog_ref, dtb_ref, og_ref, st_ref, inv_ref, do_ref, *rest):
        grad_refs, rest = rest[:ng], rest[ng:]
        dqkv_ref, dz_ref, dl_ref, dw_ref, dalog_ref, ddtb_ref, dog_ref = rest[:7]
        recv_refs, (dstate_ref, dcpad_ref, send_sems, recv_sems, local_sems) = rest[7:7 + ng], rest[7 + ng:]
        n = pl.program_id(0)
        start_exchange, wait_exchange = _direct_exchange(grad_refs, recv_refs, send_sems, recv_sems, local_sems, False)
        pl.when(n == 0)(start_exchange)

        @pl.when(n == 0)
        def _():
            dw_ref[...] = jnp.zeros_like(dw_ref)
            dalog_ref[...] = jnp.zeros_like(dalog_ref)
            ddtb_ref[...] = jnp.zeros_like(ddtb_ref)
            dog_ref[...] = jnp.zeros_like(dog_ref)
            dstate_ref[...] = jnp.zeros_like(dstate_ref)
            dcpad_ref[:, DN_CHUNK:, :] = jnp.zeros((nb, CONV_HALO, 3 * DN_WIDTH), F32)

        cq, ck, cv, z = _dn_batch_args(c_ref, z_ref)
        d_out_g = jnp.stack([do_ref[b, :, pl.ds(h * DN_HEAD_DIM, DN_HEAD_DIM)] for b, h in pairs])
        t_known = inv_ref[...].astype(F32)
        core = lambda *args: _dn_core(*args, t_known=t_known)[:2]
        _, pull = jax.vjp(core, cq, ck, cv, z, l_ref[...], st_ref[...], alog_ref[...], dtb_ref[...], og_ref[...])
        dcq, dck, dcv, dz, dlog, dstate, dalog, ddtb, dog = pull((d_out_g, dstate_ref[...]))
        dstate_ref[...] = dstate
        dl_ref[...] = dlog.astype(dl_ref.dtype)
        dalog_ref[...] += dalog
        ddtb_ref[...] += ddtb
        dog_ref[...] += dog
        for i, (b, h) in enumerate(pairs):
            dcpad_ref[b, 0:DN_CHUNK, pl.ds(h * DN_HEAD_DIM, DN_HEAD_DIM)] = dcq[i]
            dcpad_ref[b, 0:DN_CHUNK, pl.ds(DN_WIDTH + h * DN_HEAD_DIM, DN_HEAD_DIM)] = dck[i]
            dcpad_ref[b, 0:DN_CHUNK, pl.ds(2 * DN_WIDTH + h * DN_HEAD_DIM, DN_HEAD_DIM)] = dcv[i]
            dz_ref[b, :, pl.ds(h * DN_HEAD_DIM, DN_HEAD_DIM)] = dz[i].astype(dz_ref.dtype)
        for b in range(nb):
            xb = cur_ref[b]
            dx = None
            for j in range(CONV_K):
                shifted = dcpad_ref[b, pl.ds(CONV_K - 1 - j, DN_CHUNK), :]
                term = w_ref[j:j + 1, :] * shifted
                dx = term if dx is None else dx + term
                dw_ref[j:j + 1, :] += _rowsum(shifted * xb)
            dqkv_ref[b] = dx.astype(dqkv_ref.dtype)
            dcpad_ref[b, DN_CHUNK:, :] = dcpad_ref[b, 0:CONV_HALO, :]
        pl.when(n == nc - 1)(wait_exchange)

    chunk = lambda w: pl.BlockSpec((nb, DN_CHUNK, w), lambda n: (0, rev(n), 0))
    return pl.pallas_call(
        body, name="deltanet_bwd", grid=(nc,),
        out_shape=(jax.ShapeDtypeStruct((nb, s, 3 * DN_WIDTH), BF16), jax.ShapeDtypeStruct((nb, s, DN_WIDTH), BF16),
                   jax.ShapeDtypeStruct((nb, s, GATE_PAD), BF16), jax.ShapeDtypeStruct((CONV_K, 3 * DN_WIDTH), F32),
                   jax.ShapeDtypeStruct((1, GATE_PAD), F32), jax.ShapeDtypeStruct((1, GATE_PAD), F32),
                   jax.ShapeDtypeStruct((1, DN_HEAD_DIM), F32))
        + tuple(jax.ShapeDtypeStruct(a.shape, a.dtype) for a in head_grads),
        in_specs=[chunk(3 * DN_WIDTH), chunk(3 * DN_WIDTH), chunk(DN_WIDTH), chunk(GATE_PAD)] + _dn_weight_specs() + [
            pl.BlockSpec((None, gn, DN_HEAD_DIM, DN_HEAD_DIM), lambda n: (rev(n), 0, 0, 0)),
            pl.BlockSpec((None, gn, DN_CHUNK, DN_CHUNK), lambda n: (rev(n), 0, 0, 0)),
            chunk(DN_WIDTH)] + [HBM_SPEC] * ng,
        out_specs=(chunk(3 * DN_WIDTH), chunk(DN_WIDTH), chunk(GATE_PAD), _whole((CONV_K, 3 * DN_WIDTH)),
                   _whole((1, GATE_PAD)), _whole((1, GATE_PAD)), _whole((1, DN_HEAD_DIM))) + (HBM_SPEC,) * ng,
        scratch_shapes=[pltpu.VMEM((gn, DN_HEAD_DIM, DN_HEAD_DIM), F32),
                        pltpu.VMEM((nb, DN_CHUNK + CONV_HALO, 3 * DN_WIDTH), F32)] + _exchange_scratch(ng),
        compiler_params=_params(1),
    )(qkv, conv_out, zg, logits, conv_w, alog, dtb, og, states, inverses, d_out, *head_grads)


def _head(a_out, b_out, x2, p2, target, w_out, w_out_t, w_gate, w_gate_t, w_proj, ple_g, fin_g):
    t = x2.shape[0]
    tm = min(512, t)
    steps = t // tm

    def body(a_ref, b_ref, x_ref, p_ref, y_ref, wo_ref, wot_ref, wg_ref, wgt_ref, wp_ref, pg_ref, fg_ref,
             da_ref, db_ref, dh_ref, dwo_hbm, dwg_hbm, dwp_hbm, dpg_ref, dfg_ref, loss_ref,
             dwo_acc, dwg_acc, dwp_acc, rows_stage, cols_stage):
        i = pl.program_id(0)

        @pl.when(i == 0)
        def _():
            dwo_acc[...] = jnp.zeros_like(dwo_acc)
            dwg_acc[...] = jnp.zeros_like(dwg_acc)
            dwp_acc[...] = jnp.zeros_like(dwp_acc)
            dpg_ref[...] = jnp.zeros_like(dpg_ref)
            dfg_ref[...] = jnp.zeros_like(dfg_ref)
            loss_ref[...] = jnp.zeros_like(loss_ref)

        a = a_ref[...]
        bb = b_ref[...]
        pb = p_ref[...].astype(BF16)
        pg = pg_ref[...]
        fg = fg_ref[...]
        h1 = (x_ref[...] + jnp.dot(a, wo_ref[0:SGU_WIDTH, :], preferred_element_type=F32)
              + jnp.dot(bb, wo_ref[SGU_WIDTH:, :], preferred_element_type=F32))
        n1, r1 = _rms(h1)
        rn = (n1 * pg).astype(BF16)
        gate = _sigmoid(jnp.dot(rn, wg_ref[...], preferred_element_type=F32))
        pp = jnp.dot(pb, wp_ref[...], preferred_element_type=F32)
        h2 = h1 + gate * pp
        n2, r2 = _rms(h2)
        err = n2 * fg - y_ref[...]
        loss_ref[...] += jnp.broadcast_to(_rowsum(jnp.sum(err * err, axis=-1, keepdims=True)), loss_ref.shape)

        dy = err * (1.0 / D_MODEL)
        dfg_ref[...] += _rowsum(dy * n2)
        dh2 = _rms_bwd(dy * fg, n2, r2)
        dpp = (dh2 * gate).astype(BF16)
        dgl = (dh2 * pp * gate * (1.0 - gate)).astype(BF16)
        dwp_acc[...] += lax.dot_general(pb, dpp, (((0,), (0,)), ((), ())), preferred_element_type=F32)
        dwg_acc[...] += lax.dot_general(rn, dgl, (((0,), (0,)), ((), ())), preferred_element_type=F32)
        drn = jnp.dot(dgl, wgt_ref[...], preferred_element_type=F32)
        dpg_ref[...] += _rowsum(drn * n1)
        dh1 = dh2 + _rms_bwd(drn * pg, n1, r1)
        dh_ref[...] = dh1
        dhb = dh1.astype(BF16)
        da_ref[...] = jnp.dot(dhb, wot_ref[:, 0:SGU_WIDTH], preferred_element_type=F32)
        db_ref[...] = jnp.dot(dhb, wot_ref[:, SGU_WIDTH:], preferred_element_type=F32)
        dwo_acc[0:SGU_WIDTH, :] += lax.dot_general(a, dhb, (((0,), (0,)), ((), ())), preferred_element_type=F32)
        dwo_acc[SGU_WIDTH:, :] += lax.dot_general(bb, dhb, (((0,), (0,)), ((), ())), preferred_element_type=F32)

        @pl.when(i == steps - 1)
        def _():
            for j in range(N_DEV):
                for acc, hbm in ((dwo_acc, dwo_hbm), (dwg_acc, dwg_hbm)):
                    rows_stage[...] = acc[j * LANES:(j + 1) * LANES, :].astype(BF16)
                    pltpu.sync_copy(rows_stage, hbm.at[j])
                cols_stage[...] = dwp_acc[:, j * LANES:(j + 1) * LANES].astype(BF16)
                pltpu.sync_copy(cols_stage, dwp_hbm.at[j])

    tile = lambda w: pl.BlockSpec((tm, w), lambda i: (i, 0))
    return pl.pallas_call(
        body, name="head_fwd_bwd", grid=(steps,),
        out_shape=(jax.ShapeDtypeStruct((t, SGU_WIDTH), F32), jax.ShapeDtypeStruct((t, DN_WIDTH), F32),
                   jax.ShapeDtypeStruct((t, D_MODEL), F32), jax.ShapeDtypeStruct((N_DEV, LANES, D_MODEL), BF16),
                   jax.ShapeDtypeStruct((N_DEV, LANES, D_MODEL), BF16), jax.ShapeDtypeStruct((N_DEV, PLE_DIM, LANES), BF16),
                   jax.ShapeDtypeStruct((1, D_MODEL), F32), jax.ShapeDtypeStruct((1, D_MODEL), F32),
                   jax.ShapeDtypeStruct((8, LANES), F32)),
        in_specs=[tile(SGU_WIDTH), tile(DN_WIDTH), tile(D_MODEL), tile(PLE_DIM), tile(D_MODEL),
                  VMEM_SPEC, VMEM_SPEC, VMEM_SPEC, VMEM_SPEC, VMEM_SPEC, _whole((1, D_MODEL)), _whole((1, D_MODEL))],
        out_specs=(tile(SGU_WIDTH), tile(DN_WIDTH), tile(D_MODEL), HBM_SPEC, HBM_SPEC, HBM_SPEC,
                   _whole((1, D_MODEL)), _whole((1, D_MODEL)), _whole((8, LANES))),
        scratch_shapes=[pltpu.VMEM((D_MODEL, D_MODEL), F32), pltpu.VMEM((D_MODEL, D_MODEL), F32),
                        pltpu.VMEM((PLE_DIM, D_MODEL), F32), pltpu.VMEM((LANES, D_MODEL), BF16),
                        pltpu.VMEM((PLE_DIM, LANES), BF16)],
        compiler_params=_params(1),
    )(a_out, b_out, x2, p2, target, w_out, w_out_t, w_gate, w_gate_t, w_proj, ple_g, fin_g)


def _inproj_bwd(x2, dh1, a_uvz, d_sgu, d_q, d_z, d_l, norm_g, sgu_weights, wat, wqt, wzt, wgt):
    t = x2.shape[0]
    tm = min(256, t)
    steps = t // tm

    widths = (a_uvz.shape[1], d_q.shape[1], d_z.shape[1], d_l.shape[1])
    starts = (0, widths[0], widths[0] + widths[1], widths[0] + widths[1] + widths[2])

    def body(x_ref, dh_ref, uvz_ref, dsgu_ref, dq_ref, dz_ref, dl_ref, g_ref, lg_ref, lb_ref, ws_ref, bt_ref,
             wat_ref, wqt_ref, wzt_ref, wgt_ref,
             dx_ref, dw_hbm, dg_ref, dlg_ref, dlb_ref, dws_ref, dbt_ref, dw_acc, stage_ref, da_ref):
        i = pl.program_id(0)

        @pl.when(i == 0)
        def _():
            dw_acc[...] = jnp.zeros_like(dw_acc)
            for ref in (dg_ref, dlg_ref, dlb_ref, dws_ref, dbt_ref):
                ref[...] = jnp.zeros_like(ref)

        _sgu_bwd_tile(uvz_ref, dsgu_ref, (lg_ref, lb_ref, ws_ref, bt_ref), da_ref, (dlg_ref, dlb_ref, dws_ref, dbt_ref))
        g = g_ref[...]
        n, r = _rms(x_ref[...])
        xn = (n * g).astype(BF16)
        dxn = None
        for d_ref, wt_ref, col0 in zip((da_ref, dq_ref, dz_ref, dl_ref), (wat_ref, wqt_ref, wzt_ref, wgt_ref), starts):
            term = jnp.dot(d_ref[...], wt_ref[...], preferred_element_type=F32)
            dxn = term if dxn is None else dxn + term
            width = d_ref.shape[1]
            for c0 in range(0, width, 512):
                c1 = min(c0 + 512, width)
                dw_acc[col0 + c0:col0 + c1, :] += lax.dot_general(d_ref[:, c0:c1], xn, (((0,), (0,)), ((), ())),
                                                                  preferred_element_type=F32)
        dg_ref[...] += _rowsum(dxn * n)
        dx_ref[...] = dh_ref[...] + _rms_bwd(dxn * g, n, r)

        @pl.when(i == steps - 1)
        def _():
            for j in range(N_DEV):
                stage_ref[...] = dw_acc[j * IN_SHARD:(j + 1) * IN_SHARD, :]
                pltpu.sync_copy(stage_ref, dw_hbm.at[j])

    tile = lambda w: pl.BlockSpec((tm, w), lambda i: (i, 0))
    sgu_shapes = ((1, SGU_WIDTH), (1, SGU_WIDTH), (SGU_GROUPS, SGU_CHUNK, SGU_CHUNK), (SGU_CHUNK, SGU_GROUPS))
    return pl.pallas_call(
        body, name="inproj_sgu_bwd", grid=(steps,),
        out_shape=(jax.ShapeDtypeStruct((t, D_MODEL), F32), jax.ShapeDtypeStruct((N_DEV, IN_SHARD, D_MODEL), F32),
                   jax.ShapeDtypeStruct((1, D_MODEL), F32)) + tuple(jax.ShapeDtypeStruct(s, F32) for s in sgu_shapes),
        in_specs=[tile(D_MODEL), tile(D_MODEL), tile(widths[0]), tile(SGU_WIDTH)] + [tile(w) for w in widths[1:]]
        + [_whole((1, D_MODEL))] + [_whole(s) for s in sgu_shapes] + [VMEM_SPEC] * 4,
        out_specs=(tile(D_MODEL), HBM_SPEC, _whole((1, D_MODEL))) + tuple(_whole(s) for s in sgu_shapes),
        scratch_shapes=[pltpu.VMEM((sum(widths), D_MODEL), F32), pltpu.VMEM((IN_SHARD, D_MODEL), F32),
                        pltpu.VMEM((tm, widths[0]), BF16)],
        compiler_params=_params(1),
    )(x2, dh1, a_uvz, d_sgu, d_q, d_z, d_l, norm_g, *sgu_weights, wat, wqt, wzt, wgt)


def _reduce_adamw(recv, w, m, v, name, col_block=None):
    n, rows, cols = recv.shape
    cb = col_block or cols
    lead = w.ndim - 2

    def body(r_ref, w_ref, m_ref, v_ref, g_ref, d_ref, nm_ref, nv_ref):
        g = r_ref[0].astype(F32)
        for i in range(1, n):
            g = g + r_ref[i].astype(F32)
        m_new = ADAM_B1 * m_ref[...] + (1.0 - ADAM_B1) * g
        v_new = ADAM_B2 * v_ref[...] + (1.0 - ADAM_B2) * jnp.square(g)
        m_hat = m_new / (1.0 - ADAM_B1 ** ADAM_STEP)
        v_hat = v_new / (1.0 - ADAM_B2 ** ADAM_STEP)
        g_ref[...] = g
        d_ref[...] = -ADAM_LR * (m_hat / (jnp.sqrt(v_hat) + ADAM_EPS) + ADAM_WD * w_ref[...])
        nm_ref[...] = m_new
        nv_ref[...] = v_new

    blk = pl.BlockSpec((None,) * lead + (rows, cb), lambda i: (0,) * lead + (0, i))
    return pl.pallas_call(
        body, name=name, grid=(cols // cb,),
        out_shape=tuple(jax.ShapeDtypeStruct(w.shape, F32) for _ in range(4)),
        in_specs=[pl.BlockSpec((n, rows, cb), lambda i: (0, 0, i)), blk, blk, blk],
        out_specs=(blk, blk, blk, blk),
        compiler_params=_params(1),
    )(recv, w, m, v)


def _adamw_replicated(received, ws, ms, vs):
    nw = len(ws)
    starts = [sum(SMALL_PIECE_ROWS[:i]) for i in range(len(SMALL_PIECE_ROWS))]

    def natural(g_ref, row0, shape):
        cols, rows = shape[-1], _size(shape[:-1])
        if cols == LANES:
            return g_ref[row0:row0 + rows, :].reshape(shape)
        if cols < LANES:
            return g_ref[row0:row0 + 1, 0:cols].reshape(shape)
        per = cols // LANES
        return jnp.concatenate(
            [jnp.concatenate([g_ref[row0 + r * per + k:row0 + r * per + k + 1, :] for k in range(per)], axis=1)
             for r in range(rows)], axis=0).reshape(shape)

    def body(r_ref, *refs):
        w_refs, m_refs, v_refs = refs[:nw], refs[nw:2 * nw], refs[2 * nw:3 * nw]
        conv_ref, loss_ref = refs[3 * nw], refs[3 * nw + 1]
        out_refs, g_ref = refs[3 * nw + 2:-1], refs[-1]
        g = r_ref[0]
        for q in range(1, N_CHIPS):
            g = g + r_ref[q]
        g_ref[...] = g
        conv_ref[...] = natural(g_ref, starts[0], (CONV_K, 3 * DN_WIDTH))
        loss_ref[...] = natural(g_ref, starts[-1], (1, 1))
        for i in range(nw):
            gi = natural(g_ref, starts[1 + i], w_refs[i].shape)
            m_new = ADAM_B1 * m_refs[i][...] + (1.0 - ADAM_B1) * gi
            v_new = ADAM_B2 * v_refs[i][...] + (1.0 - ADAM_B2) * jnp.square(gi)
            m_hat = m_new / (1.0 - ADAM_B1 ** ADAM_STEP)
            v_hat = v_new / (1.0 - ADAM_B2 ** ADAM_STEP)
            out_refs[4 * i][...] = gi
            out_refs[4 * i + 1][...] = -ADAM_LR * (m_hat / (jnp.sqrt(v_hat) + ADAM_EPS) + ADAM_WD * w_refs[i][...])
            out_refs[4 * i + 2][...] = m_new
            out_refs[4 * i + 3][...] = v_new

    def spec(a):
        lead = max(a.ndim - 3, 0)
        return pl.BlockSpec((None,) * lead + a.shape[lead:], lambda: (0,) * a.ndim)

    weight_specs = [spec(a) for a in ws]
    return pl.pallas_call(
        body, name="adamw_replicated",
        out_shape=(jax.ShapeDtypeStruct((CONV_K, 3 * DN_WIDTH), F32), jax.ShapeDtypeStruct((1, 1), F32))
        + tuple(jax.ShapeDtypeStruct(a.shape, F32) for a in ws for _ in range(4)),
        in_specs=[pl.BlockSpec(received.shape, lambda: (0, 0, 0))] + weight_specs * 3,
        out_specs=(pl.BlockSpec((CONV_K, 3 * DN_WIDTH), lambda: (0, 0)), pl.BlockSpec((1, 1), lambda: (0, 0)))
        + tuple(s for s in weight_specs for _ in range(4)),
        scratch_shapes=[pltpu.VMEM(received.shape[1:], F32)],
        compiler_params=pltpu.CompilerParams(vmem_limit_bytes=VMEM_LIMIT),
    )(received, *ws, *ms, *vs)


def _pack_rows(pieces, rows):
    padded = [jnp.pad(jnp.ravel(p), (0, -p.size % LANES)) for p in pieces]
    flat = jnp.concatenate(padded)
    return jnp.pad(flat, (0, rows * LANES - flat.shape[0])).reshape(rows, LANES)


def kernel(x, p, norm_g, w_in, sgu_ln_g, sgu_ln_b, sgu_w_s, sgu_b_s, dn_conv_w, dn_a_log, dn_dt_bias, dn_o_norm_g, w_out, ple_norm_g, ple_gate_w, ple_proj_w, final_norm_g, loss_target, m_norm_g, m_w_in, m_sgu_ln_g, m_sgu_ln_b, m_sgu_w_s, m_sgu_b_s, m_dn_conv_w, m_dn_a_log, m_dn_dt_bias, m_dn_o_norm_g, m_w_out, m_ple_norm_g, m_ple_gate_w, m_ple_proj_w, m_final_norm_g, v_norm_g, v_w_in, v_sgu_ln_g, v_sgu_ln_b, v_sgu_w_s, v_sgu_b_s, v_dn_conv_w, v_dn_a_log, v_dn_dt_bias, v_dn_o_norm_g, v_w_out, v_ple_norm_g, v_ple_gate_w, v_ple_proj_w, v_final_norm_g):
    weights = dict(norm_g=norm_g, w_in=w_in, sgu_ln_g=sgu_ln_g, sgu_ln_b=sgu_ln_b, sgu_w_s=sgu_w_s, sgu_b_s=sgu_b_s,
                   dn_conv_w=dn_conv_w, dn_a_log=dn_a_log, dn_dt_bias=dn_dt_bias, dn_o_norm_g=dn_o_norm_g, w_out=w_out,
                   ple_norm_g=ple_norm_g, ple_gate_w=ple_gate_w, ple_proj_w=ple_proj_w, final_norm_g=final_norm_g)
    mom1 = dict(norm_g=m_norm_g, w_in=m_w_in, sgu_ln_g=m_sgu_ln_g, sgu_ln_b=m_sgu_ln_b, sgu_w_s=m_sgu_w_s,
                sgu_b_s=m_sgu_b_s, dn_conv_w=m_dn_conv_w, dn_a_log=m_dn_a_log, dn_dt_bias=m_dn_dt_bias,
                dn_o_norm_g=m_dn_o_norm_g, w_out=m_w_out, ple_norm_g=m_ple_norm_g, ple_gate_w=m_ple_gate_w,
                ple_proj_w=m_ple_proj_w, final_norm_g=m_final_norm_g)
    mom2 = dict(norm_g=v_norm_g, w_in=v_w_in, sgu_ln_g=v_sgu_ln_g, sgu_ln_b=v_sgu_ln_b, sgu_w_s=v_sgu_w_s,
                sgu_b_s=v_sgu_b_s, dn_conv_w=v_dn_conv_w, dn_a_log=v_dn_a_log, dn_dt_bias=v_dn_dt_bias,
                dn_o_norm_g=v_dn_o_norm_g, w_out=v_w_out, ple_norm_g=v_ple_norm_g, ple_gate_w=v_ple_gate_w,
                ple_proj_w=v_ple_proj_w, final_norm_g=v_final_norm_g)
    nb, s, _ = x.shape
    t = nb * s

    transposed = lambda a: jnp.transpose(a, (2, 0, 1)).reshape(IN_SHARD, D_MODEL)
    w_in_t, m_in_t, v_in_t = transposed(w_in), transposed(m_w_in), transposed(v_w_in)
    w_in_blocks, conv_blocks = _all_gather([w_in_t.astype(BF16), dn_conv_w[0]])
    w_in_full_t = w_in_blocks.reshape(IN_COLS, D_MODEL)
    wat = w_in_full_t[:3 * SGU_WIDTH]
    wqt = w_in_full_t[3 * SGU_WIDTH:3 * SGU_WIDTH + 3 * DN_WIDTH]
    wzt = w_in_full_t[3 * SGU_WIDTH + 3 * DN_WIDTH:3 * SGU_WIDTH + 4 * DN_WIDTH]
    wgt = jnp.pad(w_in_full_t[3 * SGU_WIDTH + 4 * DN_WIDTH:], ((0, GATE_PAD - 2 * DN_HEADS), (0, 0)))
    conv_full = jnp.moveaxis(conv_blocks, 0, 1).reshape(CONV_K, 3 * DN_WIDTH)
    later_shards = [w_out[0].astype(BF16), ple_gate_w[0].astype(BF16), ple_proj_w[0].astype(BF16)]

    pad_row = lambda a: jnp.pad(a.reshape(1, -1), ((0, 0), (DN_HEADS, GATE_PAD - DN_HEADS - a.size)))
    alog, dtb = pad_row(dn_a_log), pad_row(dn_dt_bias)
    og = dn_o_norm_g.reshape(1, DN_HEAD_DIM)
    ws = sgu_w_s.reshape(SGU_GROUPS, SGU_CHUNK, SGU_CHUNK)
    b_t = sgu_b_s.reshape(SGU_GROUPS, SGU_CHUNK).T
    fin_g = final_norm_g.reshape(1, D_MODEL)

    x2 = x.reshape(t, D_MODEL)
    sgu_weights = (sgu_ln_g, sgu_ln_b, ws, b_t)
    a_uvz, b_qkv, b_z, b_l, a_out, conv_out, w_out_blocks, w_gate_blocks, w_proj_blocks = _inproj_fwd(
        x2, s, norm_g, wat, wqt, wzt, wgt, sgu_weights, conv_full, later_shards)
    w_out_full = w_out_blocks.reshape(D_MODEL, D_MODEL)
    w_gate_full = w_gate_blocks.reshape(D_MODEL, D_MODEL)
    w_proj_full = jnp.moveaxis(w_proj_blocks, 0, 1).reshape(PLE_DIM, D_MODEL)
    qkv3 = b_qkv.reshape(nb, s, 3 * DN_WIDTH)
    conv_out = conv_out.reshape(nb, s, 3 * DN_WIDTH)
    z3 = b_z.reshape(nb, s, DN_WIDTH)
    l3 = b_l.reshape(nb, s, GATE_PAD)
    b_out, states, inverses = _dn_fwd(conv_out, z3, l3, alog, dtb, og)

    d_a, d_b, dh1, g_w_out, g_gate, g_proj, g_ple_g, g_fin_g, loss_tile = _head(
        a_out, b_out.reshape(t, DN_WIDTH), x2, p.reshape(t, PLE_DIM), loss_target.reshape(t, D_MODEL),
        w_out_full, w_out_full.T, w_gate_full, w_gate_full.T, w_proj_full, ple_norm_g, fin_g)
    d_qkv, d_z, d_l, g_conv, g_alog, g_dtb, g_og, *head_received = _dn_bwd(
        qkv3, conv_out, z3, l3, conv_full, alog, dtb, og, states, inverses, d_b.reshape(nb, s, DN_WIDTH),
        [g_w_out, g_gate, g_proj])
    grad_x, g_w_in, g_norm, g_ln_g, g_ln_b, g_ws, g_bt = _inproj_bwd(
        x2, dh1, a_uvz, d_a, d_qkv.reshape(t, 3 * DN_WIDTH), d_z.reshape(t, DN_WIDTH), d_l.reshape(t, GATE_PAD),
        norm_g, sgu_weights, wat, wqt, wzt, wgt)

    by_device = [g_w_in]
    small = _pack_rows([g_conv, g_norm, g_ln_g, g_ln_b, g_ws, g_bt.T, g_alog[:, DN_HEADS:2 * DN_HEADS], g_dtb[:, DN_HEADS:2 * DN_HEADS], g_og,
                        g_ple_g, g_fin_g, (0.5 / D_MODEL) * loss_tile[0:1, 0:1]], SMALL_ROWS)
    *from_sibling, small_sibling = _sibling_exchange(by_device, small)
    core = lax.axis_index("c").astype(jnp.int32).reshape(1)
    *chip_sums, small_sum = _pair_sum(core, by_device, from_sibling, small, small_sibling)
    *received, small_received = _chip_exchange(chip_sums, small_sum)

    results = {}
    outs = _reduce_adamw(received[0], w_in_t, m_in_t, v_in_t, "adamw_w_in", LANES)
    results["w_in"] = [jnp.transpose(a.reshape(IN_SHARD, 1, D_MODEL), (1, 2, 0)) for a in outs]
    for name, recv in zip(("w_out", "ple_gate_w", "ple_proj_w"), head_received):
        results[name] = _reduce_adamw(recv, weights[name], mom1[name], mom2[name], "adamw_" + name)
    names = [name for name, _ in REPLICATED]
    two_d = lambda a: a.reshape(1, -1) if a.ndim == 1 else a
    g_conv_sum, loss_sum, *flat_outs = _adamw_replicated(
        small_received, *[[two_d(src[k]) for k in names] for src in (weights, mom1, mom2)])
    for i, k in enumerate(names):
        results[k] = [a.reshape(weights[k].shape) for a in flat_outs[4 * i:4 * i + 4]]
    loss = loss_sum[0, 0]
    me = 4 * lax.axis_index("x") + 2 * lax.axis_index("y") + lax.axis_index("c")
    conv_mine = lax.dynamic_slice(g_conv_sum, (0, me * 192), (CONV_K, 192))
    results["dn_conv_w"] = _reduce_adamw(conv_mine[None], dn_conv_w, m_dn_conv_w, v_dn_conv_w, "adamw_dn_conv_w")

    return (loss, grad_x.reshape(nb, s, D_MODEL), *[results[k][0] for k in WEIGHT_ORDER],
            *[results[k][1] for k in WEIGHT_ORDER], *[results[k][2] for k in WEIGHT_ORDER],
            *[results[k][3] for k in WEIGHT_ORDER])
```

```python
import jax
import jax.numpy as jnp
from jax import lax
from jax.experimental import pallas as pl
from jax.experimental.pallas import tpu as pltpu

F32 = jnp.float32
BF16 = jnp.bfloat16

N_DEV = 8
D_MODEL = 1024
SGU_WIDTH = 512
SGU_GROUPS = 4
SGU_CHUNK = 128
DN_WIDTH = 512
DN_HEADS = 4
DN_HEAD_DIM = 128
DN_CHUNK = 128
CONV_K = 4
CONV_HALO = 8
PLE_DIM = 256
EPS = 1e-6
IN_COLS = 3592
IN_SHARD = IN_COLS // N_DEV
GATE_PAD = 128

ADAM_LR = 0.001
ADAM_B1 = 0.9
ADAM_B2 = 0.999
ADAM_EPS = 1e-08
ADAM_WD = 0.01
ADAM_STEP = 10

LANES = 128
VMEM_LIMIT = 56 * 1024 * 1024
MESH = pl.DeviceIdType.MESH

REPLICATED = (("norm_g", (1, D_MODEL)), ("sgu_ln_g", (1, SGU_WIDTH)), ("sgu_ln_b", (1, SGU_WIDTH)),
              ("sgu_w_s", (1, SGU_GROUPS, SGU_CHUNK, SGU_CHUNK)), ("sgu_b_s", (1, SGU_GROUPS, SGU_CHUNK)),
              ("dn_a_log", (1, DN_HEADS)), ("dn_dt_bias", (1, DN_HEADS)), ("dn_o_norm_g", (1, DN_HEAD_DIM)),
              ("ple_norm_g", (1, D_MODEL)), ("final_norm_g", (D_MODEL,)))
WEIGHT_ORDER = ("norm_g", "w_in", "sgu_ln_g", "sgu_ln_b", "sgu_w_s", "sgu_b_s", "dn_conv_w", "dn_a_log",
                "dn_dt_bias", "dn_o_norm_g", "w_out", "ple_norm_g", "ple_gate_w", "ple_proj_w", "final_norm_g")


def _size(shape):
    n = 1
    for s in shape:
        n *= s
    return n


SMALL_LAYOUT = (("conv", (CONV_K, 3 * DN_WIDTH)),) + REPLICATED + (("loss", (1,)),)
SMALL_PIECE_ROWS = tuple(-(-_size(s) // LANES) for _, s in SMALL_LAYOUT)
SMALL_ROWS = -(-sum(SMALL_PIECE_ROWS) // 8) * 8


def _bdot(a, b):
    return jnp.dot(a.astype(BF16), b.astype(BF16), preferred_element_type=F32)


def _sigmoid(x):
    return 0.5 * jnp.tanh(0.5 * x) + 0.5


@jax.custom_vjp
def _silu(x):
    return x * _sigmoid(x)


def _silu_fwd(x):
    s = _sigmoid(x)
    return x * s, (x, s)


def _silu_bwd(res, ct):
    x, s = res
    return (ct * (s * (1.0 + x * (1.0 - s))),)


_silu.defvjp(_silu_fwd, _silu_bwd)


def _normal_cdf(x):
    return 0.5 + 0.5 * lax.erf(x * (0.5 ** 0.5))


@jax.custom_vjp
def _gelu(x):
    return x * _normal_cdf(x)


def _gelu_fwd(x):
    cdf = _normal_cdf(x)
    return x * cdf, (x, cdf)


def _gelu_bwd(res, ct):
    x, cdf = res
    pdf = jnp.exp(-0.5 * x * x) * ((2.0 * jnp.pi) ** -0.5)
    return (ct * (cdf + x * pdf),)


_gelu.defvjp(_gelu_fwd, _gelu_bwd)


def _softplus(x):
    return jnp.maximum(x, 0.0) + jnp.log1p(jnp.exp(-jnp.abs(x)))


def _l2n(x):
    return x * lax.rsqrt(jnp.sum(x * x, axis=-1, keepdims=True) + EPS)


def _rms(x):
    r = lax.rsqrt(jnp.mean(x * x, axis=-1, keepdims=True) + EPS)
    return x * r, r


def _rms_bwd(dn, n, r):
    return r * (dn - n * jnp.mean(dn * n, axis=-1, keepdims=True))


def _onehot_row(idx, width):
    return (lax.broadcasted_iota(jnp.int32, (1, width), 1) == idx).astype(F32)


def _rowsum(x):
    return jnp.sum(x, axis=0, keepdims=True)


def _iota2(n):
    return lax.broadcasted_iota(jnp.int32, (n, n), 0), lax.broadcasted_iota(jnp.int32, (n, n), 1)


def _bmm(a, b):
    return lax.dot_general(a.astype(BF16), b.astype(BF16), (((2,), (1,)), ((0,), (0,))), preferred_element_type=F32)


def _bmm_nt(a, b):
    return lax.dot_general(a.astype(BF16), b.astype(BF16), (((2,), (2,)), ((0,), (0,))), preferred_element_type=F32)


def _bmm_tn(a, b):
    return lax.dot_general(a.astype(BF16), b.astype(BF16), (((1,), (1,)), ((0,), (0,))), preferred_element_type=F32)


def _tri_inv_impl(a):
    n = a.shape[-1]
    r, c = _iota2(n)
    x = r ^ c
    eye = (r == c).astype(F32)
    ad = jnp.where(x < 16, a, 0.0)
    p2 = _bmm(ad, ad)
    e = p2 - ad - _bmm(ad, p2)
    p4 = _bmm(p2, p2)
    e = e + p4 + _bmm(e, p4)
    p8 = _bmm(p4, p4)
    e = e + p8 + _bmm(e, p8)
    size = 16
    while size < n:
        m = jnp.where(jnp.logical_and(x < 2 * size, x >= size), a, 0.0)
        f = m + _bmm(m, e)
        e = e - f - _bmm(e, f)
        size *= 2
    return e + eye


@jax.custom_vjp
def _tri_inv(a, known):
    return _tri_inv_impl(a) if known is None else known


def _tri_inv_fwd(a, known):
    t = _tri_inv(a, known)
    return t, (t, known)


def _tri_inv_bwd(res, dt):
    t, known = res
    return -_bmm_tn(t, _bmm_nt(dt, t)), None if known is None else jnp.zeros_like(known)


_tri_inv.defvjp(_tri_inv_fwd, _tri_inv_bwd)


def _sgu_core(u, v, z, lg, lb, ws, bcol):
    n = ws.shape[0]
    r, c = _iota2(n)
    wm = jnp.where(r >= c, ws, 0.0)
    gu = _gelu(u)
    gv = _gelu(v)
    xc = gv - jnp.mean(gv, axis=-1, keepdims=True)
    ln = xc * lax.rsqrt(jnp.mean(xc * xc, axis=-1, keepdims=True) + EPS) * lg + lb
    s = _bdot(wm, ln) + bcol
    return gu * s * _silu(z)


def _lanes_of(x):
    return jnp.concatenate([x[i] for i in range(x.shape[0])], axis=1)


def _batch_of(x, width):
    return jnp.concatenate([x[None, :, i * width:(i + 1) * width] for i in range(x.shape[1] // width)], axis=0)


def _mask_dot(mask, x):
    hi = x.astype(BF16)
    lo = (x - hi.astype(F32)).astype(BF16)
    m = mask.astype(BF16)
    return jnp.dot(m, hi, preferred_element_type=F32) + jnp.dot(m, lo, preferred_element_type=F32)


def _tri_mask(n, upper):
    r, c = _iota2(n)
    return (r <= c) if upper else (r >= c)


@jax.custom_vjp
def _cumsum_rows(x):
    return _mask_dot(_tri_mask(x.shape[0], False), x)


def _cumsum_rows_fwd(x):
    return _cumsum_rows(x), None


def _cumsum_rows_bwd(_, ct):
    return (_mask_dot(_tri_mask(ct.shape[0], True), ct),)


_cumsum_rows.defvjp(_cumsum_rows_fwd, _cumsum_rows_bwd)


@jax.custom_vjp
def _colsum_all_rows(x):
    return _mask_dot(jnp.ones((x.shape[0], x.shape[0]), jnp.bool_), x)


def _colsum_all_rows_fwd(x):
    return _colsum_all_rows(x), None


def _colsum_all_rows_bwd(_, ct):
    return (_mask_dot(jnp.ones((ct.shape[0], ct.shape[0]), jnp.bool_), ct),)


_colsum_all_rows.defvjp(_colsum_all_rows_fwd, _colsum_all_rows_bwd)


def _dn_core(cq, ck, cv, z, logits, state, alog, dtb, og, t_known=None):
    gn, cn, dh = cq.shape
    heads = gn // logits.shape[0]
    q = _l2n(_silu(cq)) * (dh ** -0.5)
    k = _l2n(_silu(ck))
    v = _silu(cv)
    beta_lanes = _sigmoid(logits)
    g_lanes = -jnp.exp(alog) * _softplus(logits + dtb)
    column = lambda rows, lane: jnp.sum(rows * _onehot_row(lane, rows.shape[-1]), axis=-1, keepdims=True)[None]
    beta = jnp.concatenate([column(beta_lanes[i // heads], i % heads) for i in range(gn)], axis=0)
    g = jnp.concatenate([column(g_lanes[i // heads], heads + i % heads) for i in range(gn)], axis=0)
    r, c = _iota2(cn)
    tril = r >= c
    rw = lax.broadcasted_iota(jnp.int32, (cn, dh), 0)
    cw = lax.broadcasted_iota(jnp.int32, (cn, dh), 1)
    upper_wide = (rw <= cw).astype(F32)
    g_wide = jnp.broadcast_to(g, (gn, cn, dh))
    gc_wide = _batch_of(_cumsum_rows(_lanes_of(g_wide)), dh)
    gc_cols = _batch_of(_colsum_all_rows(_lanes_of(g_wide * upper_wide)), dh)[:, :, :cn]
    decay = jnp.exp(jnp.where(tril, gc_wide[:, :, :cn] - gc_cols, -1e30))
    kb = k * beta
    kk = _bmm_nt(kb, k) * decay
    t = _tri_inv(jnp.where(r > c, kk, 0.0), t_known)
    eg = jnp.exp(gc_wide)
    sol = _bmm(t, jnp.concatenate([v * beta, kb * eg], axis=-1))
    u_val, w_dec = sol[:, :, :dh], sol[:, :, dh:]
    qk = _bmm_nt(q, k) * decay
    g_last = jnp.sum(g_wide, axis=1, keepdims=True)
    k_dec = k * jnp.exp(g_last - gc_wide)
    ws = _bmm(jnp.concatenate([w_dec, q * eg], axis=1), state)
    v_new = u_val - ws[:, :cn]
    o = ws[:, cn:] + _bmm(qk, v_new)
    new_state = state * jnp.exp(g_last) + _bmm_tn(k_dec, v_new)
    on, _ = _rms(o)
    return on * og * _silu(z), new_state, t


N_CHIPS = 4
HBM_SPEC = pl.BlockSpec(memory_space=pl.ANY)


def _place():
    return lax.axis_index("x"), lax.axis_index("y"), lax.axis_index("c")


def _other_chip(k):
    x, y, _ = _place()
    px = 1 - x if k & 2 else x
    py = 1 - y if k & 1 else y
    return px, py, 2 * px + py


def _remote(src, dst, send_sem, recv_sem, device):
    return pltpu.make_async_remote_copy(src_ref=src, dst_ref=dst, send_sem=send_sem, recv_sem=recv_sem,
                                        device_id=device, device_id_type=MESH)


def _other_device(k):
    x, y, c = _place()
    px = 1 - x if k & 4 else x
    py = 1 - y if k & 2 else y
    pc = 1 - c if k & 1 else c
    return (px, py, pc), 4 * px + 2 * py + pc


def _direct_exchange(srcs, outs, send_sems, recv_sems, local_sems, gather):
    x, y, c = _place()
    me = 4 * x + 2 * y + c

    def copies(arriving):
        out_list = []
        for a, (src, out) in enumerate(zip(srcs, outs)):
            for k in range(1, N_DEV):
                peer, index = _other_device(k)
                mine = src if gather else src.at[index]
                out_list.append(_remote(mine, out.at[index if arriving else me], send_sems.at[a, k - 1],
                                        recv_sems.at[a, k - 1], peer))
        return out_list

    def local_copies():
        return [pltpu.make_async_copy(src if gather else src.at[me], out.at[me], local_sems.at[a])
                for a, (src, out) in enumerate(zip(srcs, outs))]

    def start():
        for cp in local_copies() + copies(False):
            cp.start()

    def wait():
        for cp in copies(True):
            cp.wait_recv()
        for cp in copies(False):
            cp.wait_send()
        for cp in local_copies():
            cp.wait()

    return start, wait


def _exchange_scratch(n):
    return [pltpu.SemaphoreType.DMA((n, N_DEV - 1)), pltpu.SemaphoreType.DMA((n, N_DEV - 1)), pltpu.SemaphoreType.DMA((n,))]


def _all_gather_and_norm(shards, x2, norm_g):
    n = len(shards)
    t = x2.shape[0]
    tm = min(512, t)
    steps = t // tm

    def body(x_ref, g_ref, *refs):
        srcs, xn_ref, outs = refs[:n], refs[n], refs[n + 1:2 * n + 1]
        send_sems, recv_sems, local_sems = refs[2 * n + 1:]
        i = pl.program_id(0)
        x, y, c = _place()
        me = 4 * x + 2 * y + c
        sibling = (x, y, 1 - c)

        def first_copies():
            local = [pltpu.make_async_copy(srcs[a], outs[a].at[me], local_sems.at[a]) for a in range(n)]
            sends = [_remote(srcs[a], outs[a].at[me], send_sems.at[a, 0], recv_sems.at[a, 0], sibling) for a in range(n)]
            for k in range(1, N_CHIPS):
                px, py, _ = _other_chip(k)
                sends += [_remote(srcs[a], outs[a].at[me], send_sems.at[a, k], recv_sems.at[a, k], (px, py, c))
                          for a in range(n)]
            return local, sends

        def passed_copies():
            out_list = []
            for k in range(1, N_CHIPS):
                px, py, _ = _other_chip(k)
                blk = 4 * px + 2 * py + c
                out_list += [_remote(outs[a].at[blk], outs[a].at[blk], send_sems.at[a, 3 + k], recv_sems.at[a, 3 + k],
                                     sibling) for a in range(n)]
            return out_list

        @pl.when(i == 0)
        def _():
            local, sends = first_copies()
            for cp in local + sends:
                cp.start()

        n_x, _ = _rms(x_ref[...])
        xn_ref[...] = (n_x * g_ref[...]).astype(xn_ref.dtype)

        @pl.when(i == steps // 2)
        def _():
            for k in range(1, N_CHIPS):
                px, py, _ = _other_chip(k)
                blk = 4 * px + 2 * py + c
                for a in range(n):
                    _remote(srcs[a], outs[a].at[blk], send_sems.at[a, k], recv_sems.at[a, k], (px, py, c)).wait_recv()
            for cp in passed_copies():
                cp.start()

        @pl.when(i == steps - 1)
        def _():
            for a in range(n):
                _remote(srcs[a], outs[a].at[me + 1 - 2 * c], send_sems.at[a, 0], recv_sems.at[a, 0], sibling).wait_recv()
            for k in range(1, N_CHIPS):
                px, py, _ = _other_chip(k)
                blk = 4 * px + 2 * py + 1 - c
                for a in range(n):
                    _remote(srcs[a], outs[a].at[blk], send_sems.at[a, 3 + k], recv_sems.at[a, 3 + k], sibling).wait_recv()
            local, sends = first_copies()
            for cp in sends + passed_copies():
                cp.wait_send()
            for cp in local:
                cp.wait()

    tile = pl.BlockSpec((tm, D_MODEL), lambda i: (i, 0))
    return pl.pallas_call(
        body, name="all_gather_weights_norm", grid=(steps,),
        out_shape=(jax.ShapeDtypeStruct((t, D_MODEL), BF16),)
        + tuple(jax.ShapeDtypeStruct((N_DEV,) + a.shape, a.dtype) for a in shards),
        in_specs=[tile, _whole((1, D_MODEL))] + [HBM_SPEC] * n, out_specs=(tile,) + (HBM_SPEC,) * n,
        scratch_shapes=[pltpu.SemaphoreType.DMA((n, N_DEV - 1)), pltpu.SemaphoreType.DMA((n, N_DEV - 1)),
                        pltpu.SemaphoreType.DMA((n,))],
        compiler_params=_params(1),
    )(x2, norm_g, *shards)


def _sibling_exchange(by_device, small):
    n = len(by_device)

    def body(*refs):
        srcs, small_src = refs[:n], refs[n]
        outs, small_out = refs[n + 1:2 * n + 1], refs[2 * n + 1]
        send_sems, recv_sems = refs[2 * n + 2:]
        x, y, c = _place()
        sibling = (x, y, 1 - c)
        copies = [_remote(small_src, small_out, send_sems.at[n, 0], recv_sems.at[n, 0], sibling)]
        for a in range(n):
            for q in range(N_CHIPS):
                copies.append(_remote(srcs[a].at[2 * q + 1 - c], outs[a].at[q], send_sems.at[a, q], recv_sems.at[a, q],
                                      sibling))
        for cp in copies:
            cp.start()
        for cp in copies:
            cp.wait_recv()
        for cp in copies:
            cp.wait_send()

    return pl.pallas_call(
        body, name="grad_sibling_exchange",
        out_shape=tuple(jax.ShapeDtypeStruct((N_CHIPS,) + a.shape[1:], a.dtype) for a in by_device)
        + (jax.ShapeDtypeStruct(small.shape, small.dtype),),
        in_specs=[HBM_SPEC] * (n + 1), out_specs=(HBM_SPEC,) * (n + 1),
        scratch_shapes=[pltpu.SemaphoreType.DMA((n + 1, N_CHIPS)), pltpu.SemaphoreType.DMA((n + 1, N_CHIPS))],
    )(*by_device, small)


def _chip_exchange(chip_sums, small):
    n = len(chip_sums)

    def body(*refs):
        srcs, small_src = refs[:n], refs[n]
        outs, small_out = refs[n + 1:2 * n + 1], refs[2 * n + 1]
        send_sems, recv_sems, local_sems = refs[2 * n + 2:]
        x, y, c = _place()
        mine = 2 * x + y
        local = [pltpu.make_async_copy(srcs[a].at[mine], outs[a].at[mine], local_sems.at[a]) for a in range(n)]
        local.append(pltpu.make_async_copy(small_src, small_out.at[mine], local_sems.at[n]))
        for cp in local:
            cp.start()
        sends = []
        for k in range(1, N_CHIPS):
            px, py, chip = _other_chip(k)
            for a in range(n):
                sends.append(_remote(srcs[a].at[chip], outs[a].at[mine], send_sems.at[a, k - 1], recv_sems.at[a, k - 1],
                                     (px, py, c)))
            sends.append(_remote(small_src, small_out.at[mine], send_sems.at[n, k - 1], recv_sems.at[n, k - 1], (px, py, c)))
        for cp in sends:
            cp.start()
        for k in range(1, N_CHIPS):
            px, py, chip = _other_chip(k)
            for a in range(n):
                _remote(srcs[a].at[chip], outs[a].at[chip], send_sems.at[a, k - 1], recv_sems.at[a, k - 1],
                        (px, py, c)).wait_recv()
            _remote(small_src, small_out.at[chip], send_sems.at[n, k - 1], recv_sems.at[n, k - 1], (px, py, c)).wait_recv()
        for cp in sends:
            cp.wait_send()
        for cp in local:
            cp.wait()

    return pl.pallas_call(
        body, name="grad_chip_exchange",
        out_shape=tuple(jax.ShapeDtypeStruct(a.shape, a.dtype) for a in chip_sums)
        + (jax.ShapeDtypeStruct((N_CHIPS,) + small.shape, small.dtype),),
        in_specs=[HBM_SPEC] * (n + 1), out_specs=(HBM_SPEC,) * (n + 1),
        scratch_shapes=[pltpu.SemaphoreType.DMA((n + 1, N_CHIPS - 1)), pltpu.SemaphoreType.DMA((n + 1, N_CHIPS - 1)),
                        pltpu.SemaphoreType.DMA((n + 1,))],
    )(*chip_sums, small)


def _pair_sum(core, by_device, from_sibling, small, small_from_sibling):
    n = len(by_device)

    def body(core_ref, *refs):
        own, sib = refs[:n], refs[n:2 * n]
        small_own, small_sib = refs[2 * n], refs[2 * n + 1]
        outs, small_out = refs[2 * n + 2:3 * n + 2], refs[3 * n + 2]
        for a in range(n):
            outs[a][...] = (own[a][...] + sib[a][...]).astype(outs[a].dtype)
        small_out[...] = small_own[...] + small_sib[...]

    def block(a):
        return (None,) + a.shape[1:], (0,) * (a.ndim - 1)

    own_specs = [pl.BlockSpec(block(a)[0], lambda q, core_ref, z=block(a)[1]: (2 * q + core_ref[0],) + z) for a in by_device]
    sib_specs = [pl.BlockSpec(block(a)[0], lambda q, core_ref, z=block(a)[1]: (q,) + z) for a in by_device]
    small_spec = pl.BlockSpec(small.shape, lambda q, core_ref: (0,) * small.ndim)
    return pl.pallas_call(
        body, name="grad_pair_sum",
        grid_spec=pltpu.PrefetchScalarGridSpec(
            num_scalar_prefetch=1, grid=(N_CHIPS,),
            in_specs=own_specs + sib_specs + [small_spec, small_spec],
            out_specs=tuple(sib_specs) + (small_spec,)),
        out_shape=tuple(jax.ShapeDtypeStruct(a.shape, BF16) for a in from_sibling)
        + (jax.ShapeDtypeStruct(small.shape, F32),),
        compiler_params=_params(1),
    )(core, *by_device, *from_sibling, small, small_from_sibling)


def _params(n_axes):
    return pltpu.CompilerParams(dimension_semantics=("arbitrary",) * n_axes, vmem_limit_bytes=VMEM_LIMIT)


def _whole(shape):
    return pl.BlockSpec(shape, lambda *_: (0,) * len(shape))


VMEM_SPEC = pl.BlockSpec(memory_space=pltpu.VMEM)


def _inproj_fwd(xn2, seq_len, wat, wqt, wzt, wgt, sgu_weights, conv_w, later_shards):
    t = xn2.shape[0]
    tm = min(512, seq_len)
    tiles_per_seq = seq_len // tm
    steps = t // tm
    ns = len(later_shards)

    def body(x_ref, wa_ref, wq_ref, wz_ref, wg_ref, lg_ref, lb_ref, ws_ref, bt_ref, cw_ref, *rest):
        shard_refs, rest = rest[:ns], rest[ns:]
        a_ref, q_ref, z_ref, l_ref, sgu_ref, c_ref = rest[:6]
        gathered_refs, (xpad_ref, send_sems, recv_sems, local_sems) = rest[6:6 + ns], rest[6 + ns:]
        start_gather, wait_gather = _direct_exchange(shard_refs, gathered_refs, send_sems, recv_sems, local_sems, True)
        pl.when(pl.program_id(0) == 0)(start_gather)
        xn = x_ref[...]
        for w_ref, o_ref in ((wa_ref, a_ref), (wq_ref, q_ref), (wz_ref, z_ref), (wg_ref, l_ref)):
            width = w_ref.shape[0]
            for c0 in range(0, width, 512):
                c1 = min(c0 + 512, width)
                o_ref[:, c0:c1] = lax.dot_general(xn, w_ref[c0:c1, :], (((1,), (1,)), ((), ())),
                                                  preferred_element_type=F32)
        for row0 in range(0, tm, SGU_CHUNK):
            for grp in range(SGU_GROUPS):
                args = _sgu_pieces(a_ref, lg_ref, lb_ref, ws_ref, bt_ref, row0, grp)
                sgu_ref[pl.ds(row0, SGU_CHUNK), pl.ds(grp * 128, 128)] = _sgu_core(*args).astype(sgu_ref.dtype)

        @pl.when(pl.program_id(0) % tiles_per_seq == 0)
        def _():
            xpad_ref[0:CONV_HALO, :] = jnp.zeros((CONV_HALO, xpad_ref.shape[1]), F32)

        xpad_ref[CONV_HALO:, :] = q_ref[...]
        acc = None
        for j in range(CONV_K):
            term = cw_ref[j:j + 1, :] * xpad_ref[pl.ds(CONV_HALO - CONV_K + 1 + j, tm), :]
            acc = term if acc is None else acc + term
        c_ref[...] = acc
        xpad_ref[0:CONV_HALO, :] = xpad_ref[tm:tm + CONV_HALO, :]
        pl.when(pl.program_id(0) == steps - 1)(wait_gather)

    widths = (wat.shape[0], wqt.shape[0], wzt.shape[0], wgt.shape[0])
    tile = lambda w: pl.BlockSpec((tm, w), lambda i: (i, 0))
    sgu_shapes = ((1, SGU_WIDTH), (1, SGU_WIDTH), (SGU_GROUPS, SGU_CHUNK, SGU_CHUNK), (SGU_CHUNK, SGU_GROUPS))
    return pl.pallas_call(
        body, name="inproj_sgu_conv_fwd", grid=(steps,),
        out_shape=tuple(jax.ShapeDtypeStruct((t, w), F32) for w in widths)
        + (jax.ShapeDtypeStruct((t, SGU_WIDTH), BF16), jax.ShapeDtypeStruct((t, widths[1]), F32))
        + tuple(jax.ShapeDtypeStruct((N_DEV,) + a.shape, a.dtype) for a in later_shards),
        in_specs=[tile(D_MODEL), VMEM_SPEC, VMEM_SPEC, VMEM_SPEC, VMEM_SPEC]
        + [_whole(s) for s in sgu_shapes] + [_whole((CONV_K, widths[1]))] + [HBM_SPEC] * ns,
        out_specs=tuple(tile(w) for w in widths) + (tile(SGU_WIDTH), tile(widths[1])) + (HBM_SPEC,) * ns,
        scratch_shapes=[pltpu.VMEM((CONV_HALO + tm, widths[1]), F32)] + _exchange_scratch(ns),
        compiler_params=_params(1),
    )(xn2, wat, wqt, wzt, wgt, *sgu_weights, conv_w, *later_shards)


def _sgu_pieces(uvz_ref, lg_ref, lb_ref, ws_ref, bt_ref, row0, grp):
    rows = pl.ds(row0, SGU_CHUNK)
    lanes = pl.ds(grp * 128, 128)
    u = uvz_ref[rows, pl.ds(grp * 128, 128)]
    v = uvz_ref[rows, pl.ds(SGU_WIDTH + grp * 128, 128)]
    z = uvz_ref[rows, pl.ds(2 * SGU_WIDTH + grp * 128, 128)]
    bcol = jnp.sum(bt_ref[...] * _onehot_row(grp, SGU_GROUPS), axis=-1, keepdims=True)
    return u, v, z, lg_ref[:, lanes], lb_ref[:, lanes], ws_ref[grp], bcol


def _sgu_bwd_tile(uvz_ref, do_ref, sgu_refs, duvz_ref, grad_refs):
    lg_ref, lb_ref, ws_ref, bt_ref = sgu_refs
    dlg_ref, dlb_ref, dws_ref, dbt_ref = grad_refs
    for row0 in range(0, uvz_ref.shape[0], SGU_CHUNK):
        rows = pl.ds(row0, SGU_CHUNK)
        for grp in range(SGU_GROUPS):
            lanes = pl.ds(grp * 128, 128)
            args = _sgu_pieces(uvz_ref, lg_ref, lb_ref, ws_ref, bt_ref, row0, grp)
            _, pull = jax.vjp(_sgu_core, *args)
            du, dv, dz, dlg, dlb, dws, dbcol = pull(do_ref[rows, lanes])
            duvz_ref[rows, pl.ds(grp * 128, 128)] = du.astype(duvz_ref.dtype)
            duvz_ref[rows, pl.ds(SGU_WIDTH + grp * 128, 128)] = dv.astype(duvz_ref.dtype)
            duvz_ref[rows, pl.ds(2 * SGU_WIDTH + grp * 128, 128)] = dz.astype(duvz_ref.dtype)
            dlg_ref[:, lanes] += dlg
            dlb_ref[:, lanes] += dlb
            dws_ref[grp] += dws
            dbt_ref[...] += dbcol * _onehot_row(grp, SGU_GROUPS)


def _dn_pairs(nb):
    return [(b, h) for b in range(nb) for h in range(DN_HEADS)]


def _dn_batch_args(c_ref, z_ref):
    pairs = _dn_pairs(c_ref.shape[0])
    pick = lambda ref, b, col: ref[b, :, pl.ds(col, DN_HEAD_DIM)]
    cq = jnp.stack([pick(c_ref, b, h * DN_HEAD_DIM) for b, h in pairs])
    ck = jnp.stack([pick(c_ref, b, DN_WIDTH + h * DN_HEAD_DIM) for b, h in pairs])
    cv = jnp.stack([pick(c_ref, b, 2 * DN_WIDTH + h * DN_HEAD_DIM) for b, h in pairs])
    z = jnp.stack([pick(z_ref, b, h * DN_HEAD_DIM) for b, h in pairs])
    return cq, ck, cv, z


def _dn_weight_specs():
    return [_whole((CONV_K, 3 * DN_WIDTH)), _whole((1, GATE_PAD)), _whole((1, GATE_PAD)), _whole((1, DN_HEAD_DIM))]


def _dn_fwd(conv_out, zg, logits, alog, dtb, og):
    nb, s, _ = conv_out.shape
    nc = s // DN_CHUNK
    pairs = _dn_pairs(nb)
    gn = len(pairs)
    chunk = lambda w: pl.BlockSpec((nb, DN_CHUNK, w), lambda n: (0, n, 0))

    def body(c_ref, z_ref, l_ref, alog_ref, dtb_ref, og_ref, out_ref, st_ref, inv_ref, state_ref):
        n = pl.program_id(0)

        @pl.when(n == 0)
        def _():
            state_ref[...] = jnp.zeros_like(state_ref)

        cq, ck, cv, z = _dn_batch_args(c_ref, z_ref)
        state = state_ref[...]
        st_ref[...] = state
        out, new_state, t = _dn_core(cq, ck, cv, z, l_ref[...], state, alog_ref[...], dtb_ref[...], og_ref[...])
        state_ref[...] = new_state
        inv_ref[...] = t.astype(inv_ref.dtype)
        for i, (b, h) in enumerate(pairs):
            out_ref[b, :, pl.ds(h * DN_HEAD_DIM, DN_HEAD_DIM)] = out[i].astype(out_ref.dtype)

    per_chunk = pl.BlockSpec((None, gn, DN_HEAD_DIM, DN_HEAD_DIM), lambda n: (n, 0, 0, 0))
    return pl.pallas_call(
        body, name="deltanet_fwd", grid=(nc,),
        out_shape=(jax.ShapeDtypeStruct((nb, s, DN_WIDTH), BF16),
                   jax.ShapeDtypeStruct((nc, gn, DN_HEAD_DIM, DN_HEAD_DIM), F32),
                   jax.ShapeDtypeStruct((nc, gn, DN_CHUNK, DN_CHUNK), BF16)),
        in_specs=[chunk(3 * DN_WIDTH), chunk(DN_WIDTH), chunk(GATE_PAD)] + _dn_weight_specs()[1:],
        out_specs=(chunk(DN_WIDTH), per_chunk, pl.BlockSpec((None, gn, DN_CHUNK, DN_CHUNK), lambda n: (n, 0, 0, 0))),
        scratch_shapes=[pltpu.VMEM((gn, DN_HEAD_DIM, DN_HEAD_DIM), F32)],
        compiler_params=_params(1),
    )(conv_out, zg, logits, alog, dtb, og)


def _dn_bwd(qkv, conv_out, zg, logits, conv_w, alog, dtb, og, states, inverses, d_out, head_grads):
    nb, s, _ = qkv.shape
    nc = s // DN_CHUNK
    rev = lambda n: nc - 1 - n
    pairs = _dn_pairs(nb)
    gn = len(pairs)
    ng = len(head_grads)

    def body(cur_ref, c_ref, z_ref, l_ref, w_ref, alog_ref, dtb_ref, og_ref, st_ref, inv_ref, do_ref, *rest):
        grad_refs, rest = rest[:ng], rest[ng:]
        dqkv_ref, dz_ref, dl_ref, dw_ref, dalog_ref, ddtb_ref, dog_ref = rest[:7]
        recv_refs, (dstate_ref, dcpad_ref, send_sems, recv_sems, local_sems) = rest[7:7 + ng], rest[7 + ng:]
        n = pl.program_id(0)
        start_exchange, wait_exchange = _direct_exchange(grad_refs, recv_refs, send_sems, recv_sems, local_sems, False)
        pl.when(n == 0)(start_exchange)

        @pl.when(n == 0)
        def _():
            dw_ref[...] = jnp.zeros_like(dw_ref)
            dalog_ref[...] = jnp.zeros_like(dalog_ref)
            ddtb_ref[...] = jnp.zeros_like(ddtb_ref)
            dog_ref[...] = jnp.zeros_like(dog_ref)
            dstate_ref[...] = jnp.zeros_like(dstate_ref)
            dcpad_ref[:, DN_CHUNK:, :] = jnp.zeros((nb, CONV_HALO, 3 * DN_WIDTH), F32)

        cq, ck, cv, z = _dn_batch_args(c_ref, z_ref)
        d_out_g = jnp.stack([do_ref[b, :, pl.ds(h * DN_HEAD_DIM, DN_HEAD_DIM)] for b, h in pairs])
        t_known = inv_ref[...].astype(F32)
        core = lambda *args: _dn_core(*args, t_known=t_known)[:2]
        _, pull = jax.vjp(core, cq, ck, cv, z, l_ref[...], st_ref[...], alog_ref[...], dtb_ref[...], og_ref[...])
        dcq, dck, dcv, dz, dlog, dstate, dalog, ddtb, dog = pull((d_out_g, dstate_ref[...]))
        dstate_ref[...] = dstate
        dl_ref[...] = dlog.astype(dl_ref.dtype)
        dalog_ref[...] += dalog
        ddtb_ref[...] += ddtb
        dog_ref[...] += dog
        for i, (b, h) in enumerate(pairs):
            dcpad_ref[b, 0:DN_CHUNK, pl.ds(h * DN_HEAD_DIM, DN_HEAD_DIM)] = dcq[i]
            dcpad_ref[b, 0:DN_CHUNK, pl.ds(DN_WIDTH + h * DN_HEAD_DIM, DN_HEAD_DIM)] = dck[i]
            dcpad_ref[b, 0:DN_CHUNK, pl.ds(2 * DN_WIDTH + h * DN_HEAD_DIM, DN_HEAD_DIM)] = dcv[i]
            dz_ref[b, :, pl.ds(h * DN_HEAD_DIM, DN_HEAD_DIM)] = dz[i].astype(dz_ref.dtype)
        for b in range(nb):
            xb = cur_ref[b]
            dx = None
            for j in range(CONV_K):
                shifted = dcpad_ref[b, pl.ds(CONV_K - 1 - j, DN_CHUNK), :]
                term = w_ref[j:j + 1, :] * shifted
                dx = term if dx is None else dx + term
                dw_ref[j:j + 1, :] += _rowsum(shifted * xb)
            dqkv_ref[b] = dx.astype(dqkv_ref.dtype)
            dcpad_ref[b, DN_CHUNK:, :] = dcpad_ref[b, 0:CONV_HALO, :]
        pl.when(n == nc - 1)(wait_exchange)

    chunk = lambda w: pl.BlockSpec((nb, DN_CHUNK, w), lambda n: (0, rev(n), 0))
    return pl.pallas_call(
        body, name="deltanet_bwd", grid=(nc,),
        out_shape=(jax.ShapeDtypeStruct((nb, s, 3 * DN_WIDTH), BF16), jax.ShapeDtypeStruct((nb, s, DN_WIDTH), BF16),
                   jax.ShapeDtypeStruct((nb, s, GATE_PAD), BF16), jax.ShapeDtypeStruct((CONV_K, 3 * DN_WIDTH), F32),
                   jax.ShapeDtypeStruct((1, GATE_PAD), F32), jax.ShapeDtypeStruct((1, GATE_PAD), F32),
                   jax.ShapeDtypeStruct((1, DN_HEAD_DIM), F32))
        + tuple(jax.ShapeDtypeStruct(a.shape, a.dtype) for a in head_grads),
        in_specs=[chunk(3 * DN_WIDTH), chunk(3 * DN_WIDTH), chunk(DN_WIDTH), chunk(GATE_PAD)] + _dn_weight_specs() + [
            pl.BlockSpec((None, gn, DN_HEAD_DIM, DN_HEAD_DIM), lambda n: (rev(n), 0, 0, 0)),
            pl.BlockSpec((None, gn, DN_CHUNK, DN_CHUNK), lambda n: (rev(n), 0, 0, 0)),
            chunk(DN_WIDTH)] + [HBM_SPEC] * ng,
        out_specs=(chunk(3 * DN_WIDTH), chunk(DN_WIDTH), chunk(GATE_PAD), _whole((CONV_K, 3 * DN_WIDTH)),
                   _whole((1, GATE_PAD)), _whole((1, GATE_PAD)), _whole((1, DN_HEAD_DIM))) + (HBM_SPEC,) * ng,
        scratch_shapes=[pltpu.VMEM((gn, DN_HEAD_DIM, DN_HEAD_DIM), F32),
                        pltpu.VMEM((nb, DN_CHUNK + CONV_HALO, 3 * DN_WIDTH), F32)] + _exchange_scratch(ng),
        compiler_params=_params(1),
    )(qkv, conv_out, zg, logits, conv_w, alog, dtb, og, states, inverses, d_out, *head_grads)


def _head(a_out, b_out, x2, p2, target, w_out, w_out_t, w_gate, w_gate_t, w_proj, ple_g, fin_g):
    t = x2.shape[0]
    tm = min(512, t)
    steps = t // tm

    def body(a_ref, b_ref, x_ref, p_ref, y_ref, wo_ref, wot_ref, wg_ref, wgt_ref, wp_ref, pg_ref, fg_ref,
             da_ref, db_ref, dh_ref, dwo_hbm, dwg_hbm, dwp_hbm, dpg_ref, dfg_ref, loss_ref,
             dwo_acc, dwg_acc, dwp_acc, rows_stage, cols_stage):
        i = pl.program_id(0)

        @pl.when(i == 0)
        def _():
            dwo_acc[...] = jnp.zeros_like(dwo_acc)
            dwg_acc[...] = jnp.zeros_like(dwg_acc)
            dwp_acc[...] = jnp.zeros_like(dwp_acc)
            dpg_ref[...] = jnp.zeros_like(dpg_ref)
            dfg_ref[...] = jnp.zeros_like(dfg_ref)
            loss_ref[...] = jnp.zeros_like(loss_ref)

        a = a_ref[...]
        bb = b_ref[...]
        pb = p_ref[...].astype(BF16)
        pg = pg_ref[...]
        fg = fg_ref[...]
        h1 = (x_ref[...] + jnp.dot(a, wo_ref[0:SGU_WIDTH, :], preferred_element_type=F32)
              + jnp.dot(bb, wo_ref[SGU_WIDTH:, :], preferred_element_type=F32))
        n1, r1 = _rms(h1)
        rn = (n1 * pg).astype(BF16)
        gate = _sigmoid(jnp.dot(rn, wg_ref[...], preferred_element_type=F32))
        pp = jnp.dot(pb, wp_ref[...], preferred_element_type=F32)
        h2 = h1 + gate * pp
        n2, r2 = _rms(h2)
        err = n2 * fg - y_ref[...]
        loss_ref[...] += jnp.broadcast_to(_rowsum(jnp.sum(err * err, axis=-1, keepdims=True)), loss_ref.shape)

        dy = err * (1.0 / D_MODEL)
        dfg_ref[...] += _rowsum(dy * n2)
        dh2 = _rms_bwd(dy * fg, n2, r2)
        dpp = (dh2 * gate).astype(BF16)
        dgl = (dh2 * pp * gate * (1.0 - gate)).astype(BF16)
        dwp_acc[...] += lax.dot_general(pb, dpp, (((0,), (0,)), ((), ())), preferred_element_type=F32)
        dwg_acc[...] += lax.dot_general(rn, dgl, (((0,), (0,)), ((), ())), preferred_element_type=F32)
        drn = jnp.dot(dgl, wgt_ref[...], preferred_element_type=F32)
        dpg_ref[...] += _rowsum(drn * n1)
        dh1 = dh2 + _rms_bwd(drn * pg, n1, r1)
        dh_ref[...] = dh1
        dhb = dh1.astype(BF16)
        da_ref[...] = jnp.dot(dhb, wot_ref[:, 0:SGU_WIDTH], preferred_element_type=F32)
        db_ref[...] = jnp.dot(dhb, wot_ref[:, SGU_WIDTH:], preferred_element_type=F32)
        dwo_acc[0:SGU_WIDTH, :] += lax.dot_general(a, dhb, (((0,), (0,)), ((), ())), preferred_element_type=F32)
        dwo_acc[SGU_WIDTH:, :] += lax.dot_general(bb, dhb, (((0,), (0,)), ((), ())), preferred_element_type=F32)

        @pl.when(i == steps - 1)
        def _():
            for j in range(N_DEV):
                for acc, hbm in ((dwo_acc, dwo_hbm), (dwg_acc, dwg_hbm)):
                    rows_stage[...] = acc[j * LANES:(j + 1) * LANES, :].astype(BF16)
                    pltpu.sync_copy(rows_stage, hbm.at[j])
                cols_stage[...] = dwp_acc[:, j * LANES:(j + 1) * LANES].astype(BF16)
                pltpu.sync_copy(cols_stage, dwp_hbm.at[j])

    tile = lambda w: pl.BlockSpec((tm, w), lambda i: (i, 0))
    return pl.pallas_call(
        body, name="head_fwd_bwd", grid=(steps,),
        out_shape=(jax.ShapeDtypeStruct((t, SGU_WIDTH), F32), jax.ShapeDtypeStruct((t, DN_WIDTH), F32),
                   jax.ShapeDtypeStruct((t, D_MODEL), F32), jax.ShapeDtypeStruct((N_DEV, LANES, D_MODEL), BF16),
                   jax.ShapeDtypeStruct((N_DEV, LANES, D_MODEL), BF16), jax.ShapeDtypeStruct((N_DEV, PLE_DIM, LANES), BF16),
                   jax.ShapeDtypeStruct((1, D_MODEL), F32), jax.ShapeDtypeStruct((1, D_MODEL), F32),
                   jax.ShapeDtypeStruct((8, LANES), F32)),
        in_specs=[tile(SGU_WIDTH), tile(DN_WIDTH), tile(D_MODEL), tile(PLE_DIM), tile(D_MODEL),
                  VMEM_SPEC, VMEM_SPEC, VMEM_SPEC, VMEM_SPEC, VMEM_SPEC, _whole((1, D_MODEL)), _whole((1, D_MODEL))],
        out_specs=(tile(SGU_WIDTH), tile(DN_WIDTH), tile(D_MODEL), HBM_SPEC, HBM_SPEC, HBM_SPEC,
                   _whole((1, D_MODEL)), _whole((1, D_MODEL)), _whole((8, LANES))),
        scratch_shapes=[pltpu.VMEM((D_MODEL, D_MODEL), F32), pltpu.VMEM((D_MODEL, D_MODEL), F32),
                        pltpu.VMEM((PLE_DIM, D_MODEL), F32), pltpu.VMEM((LANES, D_MODEL), BF16),
                        pltpu.VMEM((PLE_DIM, LANES), BF16)],
        compiler_params=_params(1),
    )(a_out, b_out, x2, p2, target, w_out, w_out_t, w_gate, w_gate_t, w_proj, ple_g, fin_g)


def _inproj_bwd(x2, dh1, a_uvz, d_sgu, d_q, d_z, d_l, norm_g, sgu_weights, wat, wqt, wzt, wgt):
    t = x2.shape[0]
    tm = min(256, t)
    steps = t // tm

    widths = (a_uvz.shape[1], d_q.shape[1], d_z.shape[1], d_l.shape[1])
    starts = (0, widths[0], widths[0] + widths[1], widths[0] + widths[1] + widths[2])

    def body(x_ref, dh_ref, uvz_ref, dsgu_ref, dq_ref, dz_ref, dl_ref, g_ref, lg_ref, lb_ref, ws_ref, bt_ref,
             wat_ref, wqt_ref, wzt_ref, wgt_ref,
             dx_ref, dw_hbm, dg_ref, dlg_ref, dlb_ref, dws_ref, dbt_ref, dw_acc, stage_ref, da_ref):
        i = pl.program_id(0)

        @pl.when(i == 0)
        def _():
            dw_acc[...] = jnp.zeros_like(dw_acc)
            for ref in (dg_ref, dlg_ref, dlb_ref, dws_ref, dbt_ref):
                ref[...] = jnp.zeros_like(ref)

        _sgu_bwd_tile(uvz_ref, dsgu_ref, (lg_ref, lb_ref, ws_ref, bt_ref), da_ref, (dlg_ref, dlb_ref, dws_ref, dbt_ref))
        g = g_ref[...]
        n, r = _rms(x_ref[...])
        xn = (n * g).astype(BF16)
        dxn = None
        for d_ref, wt_ref, col0 in zip((da_ref, dq_ref, dz_ref, dl_ref), (wat_ref, wqt_ref, wzt_ref, wgt_ref), starts):
            term = jnp.dot(d_ref[...], wt_ref[...], preferred_element_type=F32)
            dxn = term if dxn is None else dxn + term
            width = d_ref.shape[1]
            for c0 in range(0, width, 512):
                c1 = min(c0 + 512, width)
                dw_acc[col0 + c0:col0 + c1, :] += lax.dot_general(d_ref[:, c0:c1], xn, (((0,), (0,)), ((), ())),
                                                                  preferred_element_type=F32)
        dg_ref[...] += _rowsum(dxn * n)
        dx_ref[...] = dh_ref[...] + _rms_bwd(dxn * g, n, r)

        @pl.when(i == steps - 1)
        def _():
            for j in range(N_DEV):
                stage_ref[...] = dw_acc[j * IN_SHARD:(j + 1) * IN_SHARD, :]
                pltpu.sync_copy(stage_ref, dw_hbm.at[j])

    tile = lambda w: pl.BlockSpec((tm, w), lambda i: (i, 0))
    sgu_shapes = ((1, SGU_WIDTH), (1, SGU_WIDTH), (SGU_GROUPS, SGU_CHUNK, SGU_CHUNK), (SGU_CHUNK, SGU_GROUPS))
    return pl.pallas_call(
        body, name="inproj_sgu_bwd", grid=(steps,),
        out_shape=(jax.ShapeDtypeStruct((t, D_MODEL), F32), jax.ShapeDtypeStruct((N_DEV, IN_SHARD, D_MODEL), F32),
                   jax.ShapeDtypeStruct((1, D_MODEL), F32)) + tuple(jax.ShapeDtypeStruct(s, F32) for s in sgu_shapes),
        in_specs=[tile(D_MODEL), tile(D_MODEL), tile(widths[0]), tile(SGU_WIDTH)] + [tile(w) for w in widths[1:]]
        + [_whole((1, D_MODEL))] + [_whole(s) for s in sgu_shapes] + [VMEM_SPEC] * 4,
        out_specs=(tile(D_MODEL), HBM_SPEC, _whole((1, D_MODEL))) + tuple(_whole(s) for s in sgu_shapes),
        scratch_shapes=[pltpu.VMEM((sum(widths), D_MODEL), F32), pltpu.VMEM((IN_SHARD, D_MODEL), F32),
                        pltpu.VMEM((tm, widths[0]), BF16)],
        compiler_params=_params(1),
    )(x2, dh1, a_uvz, d_sgu, d_q, d_z, d_l, norm_g, *sgu_weights, wat, wqt, wzt, wgt)


def _reduce_adamw(recv, w, m, v, name, col_block=None):
    n, rows, cols = recv.shape
    cb = col_block or cols
    lead = w.ndim - 2

    def body(r_ref, w_ref, m_ref, v_ref, g_ref, d_ref, nm_ref, nv_ref):
        g = r_ref[0].astype(F32)
        for i in range(1, n):
            g = g + r_ref[i].astype(F32)
        m_new = ADAM_B1 * m_ref[...] + (1.0 - ADAM_B1) * g
        v_new = ADAM_B2 * v_ref[...] + (1.0 - ADAM_B2) * jnp.square(g)
        m_hat = m_new / (1.0 - ADAM_B1 ** ADAM_STEP)
        v_hat = v_new / (1.0 - ADAM_B2 ** ADAM_STEP)
        g_ref[...] = g
        d_ref[...] = -ADAM_LR * (m_hat / (jnp.sqrt(v_hat) + ADAM_EPS) + ADAM_WD * w_ref[...])
        nm_ref[...] = m_new
        nv_ref[...] = v_new

    blk = pl.BlockSpec((None,) * lead + (rows, cb), lambda i: (0,) * lead + (0, i))
    return pl.pallas_call(
        body, name=name, grid=(cols // cb,),
        out_shape=tuple(jax.ShapeDtypeStruct(w.shape, F32) for _ in range(4)),
        in_specs=[pl.BlockSpec((n, rows, cb), lambda i: (0, 0, i)), blk, blk, blk],
        out_specs=(blk, blk, blk, blk),
        compiler_params=_params(1),
    )(recv, w, m, v)


def _adamw_replicated(received, ws, ms, vs):
    nw = len(ws)
    starts = [sum(SMALL_PIECE_ROWS[:i]) for i in range(len(SMALL_PIECE_ROWS))]

    def natural(g_ref, row0, shape):
        cols, rows = shape[-1], _size(shape[:-1])
        if cols == LANES:
            return g_ref[row0:row0 + rows, :].reshape(shape)
        if cols < LANES:
            return g_ref[row0:row0 + 1, 0:cols].reshape(shape)
        per = cols // LANES
        return jnp.concatenate(
            [jnp.concatenate([g_ref[row0 + r * per + k:row0 + r * per + k + 1, :] for k in range(per)], axis=1)
             for r in range(rows)], axis=0).reshape(shape)

    def body(r_ref, *refs):
        w_refs, m_refs, v_refs = refs[:nw], refs[nw:2 * nw], refs[2 * nw:3 * nw]
        conv_ref, loss_ref = refs[3 * nw], refs[3 * nw + 1]
        out_refs, g_ref = refs[3 * nw + 2:-1], refs[-1]
        g = r_ref[0]
        for q in range(1, N_CHIPS):
            g = g + r_ref[q]
        g_ref[...] = g
        conv_ref[...] = natural(g_ref, starts[0], (CONV_K, 3 * DN_WIDTH))
        loss_ref[...] = natural(g_ref, starts[-1], (1, 1))
        for i in range(nw):
            gi = natural(g_ref, starts[1 + i], w_refs[i].shape)
            m_new = ADAM_B1 * m_refs[i][...] + (1.0 - ADAM_B1) * gi
            v_new = ADAM_B2 * v_refs[i][...] + (1.0 - ADAM_B2) * jnp.square(gi)
            m_hat = m_new / (1.0 - ADAM_B1 ** ADAM_STEP)
            v_hat = v_new / (1.0 - ADAM_B2 ** ADAM_STEP)
            out_refs[4 * i][...] = gi
            out_refs[4 * i + 1][...] = -ADAM_LR * (m_hat / (jnp.sqrt(v_hat) + ADAM_EPS) + ADAM_WD * w_refs[i][...])
            out_refs[4 * i + 2][...] = m_new
            out_refs[4 * i + 3][...] = v_new

    def spec(a):
        lead = max(a.ndim - 3, 0)
        return pl.BlockSpec((None,) * lead + a.shape[lead:], lambda: (0,) * a.ndim)

    weight_specs = [spec(a) for a in ws]
    return pl.pallas_call(
        body, name="adamw_replicated",
        out_shape=(jax.ShapeDtypeStruct((CONV_K, 3 * DN_WIDTH), F32), jax.ShapeDtypeStruct((1, 1), F32))
        + tuple(jax.ShapeDtypeStruct(a.shape, F32) for a in ws for _ in range(4)),
        in_specs=[pl.BlockSpec(received.shape, lambda: (0, 0, 0))] + weight_specs * 3,
        out_specs=(pl.BlockSpec((CONV_K, 3 * DN_WIDTH), lambda: (0, 0)), pl.BlockSpec((1, 1), lambda: (0, 0)))
        + tuple(s for s in weight_specs for _ in range(4)),
        scratch_shapes=[pltpu.VMEM(received.shape[1:], F32)],
        compiler_params=pltpu.CompilerParams(vmem_limit_bytes=VMEM_LIMIT),
    )(received, *ws, *ms, *vs)


def _pack_rows(pieces, rows):
    padded = [jnp.pad(jnp.ravel(p), (0, -p.size % LANES)) for p in pieces]
    flat = jnp.concatenate(padded)
    return jnp.pad(flat, (0, rows * LANES - flat.shape[0])).reshape(rows, LANES)


def kernel(x, p, norm_g, w_in, sgu_ln_g, sgu_ln_b, sgu_w_s, sgu_b_s, dn_conv_w, dn_a_log, dn_dt_bias, dn_o_norm_g, w_out, ple_norm_g, ple_gate_w, ple_proj_w, final_norm_g, loss_target, m_norm_g, m_w_in, m_sgu_ln_g, m_sgu_ln_b, m_sgu_w_s, m_sgu_b_s, m_dn_conv_w, m_dn_a_log, m_dn_dt_bias, m_dn_o_norm_g, m_w_out, m_ple_norm_g, m_ple_gate_w, m_ple_proj_w, m_final_norm_g, v_norm_g, v_w_in, v_sgu_ln_g, v_sgu_ln_b, v_sgu_w_s, v_sgu_b_s, v_dn_conv_w, v_dn_a_log, v_dn_dt_bias, v_dn_o_norm_g, v_w_out, v_ple_norm_g, v_ple_gate_w, v_ple_proj_w, v_final_norm_g):
    weights = dict(norm_g=norm_g, w_in=w_in, sgu_ln_g=sgu_ln_g, sgu_ln_b=sgu_ln_b, sgu_w_s=sgu_w_s, sgu_b_s=sgu_b_s,
                   dn_conv_w=dn_conv_w, dn_a_log=dn_a_log, dn_dt_bias=dn_dt_bias, dn_o_norm_g=dn_o_norm_g, w_out=w_out,
                   ple_norm_g=ple_norm_g, ple_gate_w=ple_gate_w, ple_proj_w=ple_proj_w, final_norm_g=final_norm_g)
    mom1 = dict(norm_g=m_norm_g, w_in=m_w_in, sgu_ln_g=m_sgu_ln_g, sgu_ln_b=m_sgu_ln_b, sgu_w_s=m_sgu_w_s,
                sgu_b_s=m_sgu_b_s, dn_conv_w=m_dn_conv_w, dn_a_log=m_dn_a_log, dn_dt_bias=m_dn_dt_bias,
                dn_o_norm_g=m_dn_o_norm_g, w_out=m_w_out, ple_norm_g=m_ple_norm_g, ple_gate_w=m_ple_gate_w,
                ple_proj_w=m_ple_proj_w, final_norm_g=m_final_norm_g)
    mom2 = dict(norm_g=v_norm_g, w_in=v_w_in, sgu_ln_g=v_sgu_ln_g, sgu_ln_b=v_sgu_ln_b, sgu_w_s=v_sgu_w_s,
                sgu_b_s=v_sgu_b_s, dn_conv_w=v_dn_conv_w, dn_a_log=v_dn_a_log, dn_dt_bias=v_dn_dt_bias,
                dn_o_norm_g=v_dn_o_norm_g, w_out=v_w_out, ple_norm_g=v_ple_norm_g, ple_gate_w=v_ple_gate_w,
                ple_proj_w=v_ple_proj_w, final_norm_g=v_final_norm_g)
    nb, s, _ = x.shape
    t = nb * s

    transposed = lambda a: jnp.transpose(a, (2, 0, 1)).reshape(IN_SHARD, D_MODEL)
    w_in_t, m_in_t, v_in_t = transposed(w_in), transposed(m_w_in), transposed(v_w_in)
    x2 = x.reshape(t, D_MODEL)
    xn2, w_in_blocks, conv_blocks = _all_gather_and_norm([w_in_t.astype(BF16), dn_conv_w[0]], x2, norm_g)
    w_in_full_t = w_in_blocks.reshape(IN_COLS, D_MODEL)
    wat = w_in_full_t[:3 * SGU_WIDTH]
    wqt = w_in_full_t[3 * SGU_WIDTH:3 * SGU_WIDTH + 3 * DN_WIDTH]
    wzt = w_in_full_t[3 * SGU_WIDTH + 3 * DN_WIDTH:3 * SGU_WIDTH + 4 * DN_WIDTH]
    wgt = jnp.pad(w_in_full_t[3 * SGU_WIDTH + 4 * DN_WIDTH:], ((0, GATE_PAD - 2 * DN_HEADS), (0, 0)))
    conv_full = jnp.moveaxis(conv_blocks, 0, 1).reshape(CONV_K, 3 * DN_WIDTH)
    later_shards = [w_out[0].astype(BF16), ple_gate_w[0].astype(BF16), ple_proj_w[0].astype(BF16)]

    pad_row = lambda a: jnp.pad(a.reshape(1, -1), ((0, 0), (DN_HEADS, GATE_PAD - DN_HEADS - a.size)))
    alog, dtb = pad_row(dn_a_log), pad_row(dn_dt_bias)
    og = dn_o_norm_g.reshape(1, DN_HEAD_DIM)
    ws = sgu_w_s.reshape(SGU_GROUPS, SGU_CHUNK, SGU_CHUNK)
    b_t = sgu_b_s.reshape(SGU_GROUPS, SGU_CHUNK).T
    fin_g = final_norm_g.reshape(1, D_MODEL)

    sgu_weights = (sgu_ln_g, sgu_ln_b, ws, b_t)
    a_uvz, b_qkv, b_z, b_l, a_out, conv_out, w_out_blocks, w_gate_blocks, w_proj_blocks = _inproj_fwd(
        xn2, s, wat, wqt, wzt, wgt, sgu_weights, conv_full, later_shards)
    w_out_full = w_out_blocks.reshape(D_MODEL, D_MODEL)
    w_gate_full = w_gate_blocks.reshape(D_MODEL, D_MODEL)
    w_proj_full = jnp.moveaxis(w_proj_blocks, 0, 1).reshape(PLE_DIM, D_MODEL)
    qkv3 = b_qkv.reshape(nb, s, 3 * DN_WIDTH)
    conv_out = conv_out.reshape(nb, s, 3 * DN_WIDTH)
    z3 = b_z.reshape(nb, s, DN_WIDTH)
    l3 = b_l.reshape(nb, s, GATE_PAD)
    b_out, states, inverses = _dn_fwd(conv_out, z3, l3, alog, dtb, og)

    d_a, d_b, dh1, g_w_out, g_gate, g_proj, g_ple_g, g_fin_g, loss_tile = _head(
        a_out, b_out.reshape(t, DN_WIDTH), x2, p.reshape(t, PLE_DIM), loss_target.reshape(t, D_MODEL),
        w_out_full, w_out_full.T, w_gate_full, w_gate_full.T, w_proj_full, ple_norm_g, fin_g)
    d_qkv, d_z, d_l, g_conv, g_alog, g_dtb, g_og, *head_received = _dn_bwd(
        qkv3, conv_out, z3, l3, conv_full, alog, dtb, og, states, inverses, d_b.reshape(nb, s, DN_WIDTH),
        [g_w_out, g_gate, g_proj])
    grad_x, g_w_in, g_norm, g_ln_g, g_ln_b, g_ws, g_bt = _inproj_bwd(
        x2, dh1, a_uvz, d_a, d_qkv.reshape(t, 3 * DN_WIDTH), d_z.reshape(t, DN_WIDTH), d_l.reshape(t, GATE_PAD),
        norm_g, sgu_weights, wat, wqt, wzt, wgt)

    by_device = [g_w_in]
    small = _pack_rows([g_conv, g_norm, g_ln_g, g_ln_b, g_ws, g_bt.T, g_alog[:, DN_HEADS:2 * DN_HEADS], g_dtb[:, DN_HEADS:2 * DN_HEADS], g_og,
                        g_ple_g, g_fin_g, (0.5 / D_MODEL) * loss_tile[0:1, 0:1]], SMALL_ROWS)
    *from_sibling, small_sibling = _sibling_exchange(by_device, small)
    core = lax.axis_index("c").astype(jnp.int32).reshape(1)
    *chip_sums, small_sum = _pair_sum(core, by_device, from_sibling, small, small_sibling)
    *received, small_received = _chip_exchange(chip_sums, small_sum)

    results = {}
    outs = _reduce_adamw(received[0], w_in_t, m_in_t, v_in_t, "adamw_w_in", LANES)
    results["w_in"] = [jnp.transpose(a.reshape(IN_SHARD, 1, D_MODEL), (1, 2, 0)) for a in outs]
    for name, recv in zip(("w_out", "ple_gate_w", "ple_proj_w"), head_received):
        results[name] = _reduce_adamw(recv, weights[name], mom1[name], mom2[name], "adamw_" + name)
    names = [name for name, _ in REPLICATED]
    two_d = lambda a: a.reshape(1, -1) if a.ndim == 1 else a
    g_conv_sum, loss_sum, *flat_outs = _adamw_replicated(
        small_received, *[[two_d(src[k]) for k in names] for src in (weights, mom1, mom2)])
    for i, k in enumerate(names):
        results[k] = [a.reshape(weights[k].shape) for a in flat_outs[4 * i:4 * i + 4]]
    loss = loss_sum[0, 0]
    me = 4 * lax.axis_index("x") + 2 * lax.axis_index("y") + lax.axis_index("c")
    conv_mine = lax.dynamic_slice(g_conv_sum, (0, me * 192), (CONV_K, 192))
    results["dn_conv_w"] = _reduce_adamw(conv_mine[None], dn_conv_w, m_dn_conv_w, v_dn_conv_w, "adamw_dn_conv_w")

    return (loss, grad_x.reshape(nb, s, D_MODEL), *[results[k][0] for k in WEIGHT_ORDER],
            *[results[k][1] for k in WEIGHT_ORDER], *[results[k][2] for k in WEIGHT_ORDER],
            *[results[k][3] for k in WEIGHT_ORDER])
```

```python
import jax
import jax.numpy as jnp
from jax import lax
from jax.experimental import pallas as pl
from jax.experimental.pallas import tpu as pltpu

F32 = jnp.float32
BF16 = jnp.bfloat16

N_DEV = 8
D_MODEL = 1024
SGU_WIDTH = 512
SGU_GROUPS = 4
SGU_CHUNK = 128
DN_WIDTH = 512
DN_HEADS = 4
DN_HEAD_DIM = 128
DN_CHUNK = 128
CONV_K = 4
CONV_HALO = 8
PLE_DIM = 256
EPS = 1e-6
IN_COLS = 3592
IN_SHARD = IN_COLS // N_DEV
GATE_PAD = 128

ADAM_LR = 0.001
ADAM_B1 = 0.9
ADAM_B2 = 0.999
ADAM_EPS = 1e-08
ADAM_WD = 0.01
ADAM_STEP = 10

LANES = 128
VMEM_LIMIT = 56 * 1024 * 1024
MESH = pl.DeviceIdType.MESH

REPLICATED = (("norm_g", (1, D_MODEL)), ("sgu_ln_g", (1, SGU_WIDTH)), ("sgu_ln_b", (1, SGU_WIDTH)),
              ("sgu_w_s", (1, SGU_GROUPS, SGU_CHUNK, SGU_CHUNK)), ("sgu_b_s", (1, SGU_GROUPS, SGU_CHUNK)),
              ("dn_a_log", (1, DN_HEADS)), ("dn_dt_bias", (1, DN_HEADS)), ("dn_o_norm_g", (1, DN_HEAD_DIM)),
              ("ple_norm_g", (1, D_MODEL)), ("final_norm_g", (D_MODEL,)))
WEIGHT_ORDER = ("norm_g", "w_in", "sgu_ln_g", "sgu_ln_b", "sgu_w_s", "sgu_b_s", "dn_conv_w", "dn_a_log",
                "dn_dt_bias", "dn_o_norm_g", "w_out", "ple_norm_g", "ple_gate_w", "ple_proj_w", "final_norm_g")


def _size(shape):
    n = 1
    for s in shape:
        n *= s
    return n


SMALL_LAYOUT = (("conv", (CONV_K, 3 * DN_WIDTH)),) + REPLICATED + (("loss", (1,)),)
SMALL_PIECE_ROWS = tuple(-(-_size(s) // LANES) for _, s in SMALL_LAYOUT)
SMALL_ROWS = -(-sum(SMALL_PIECE_ROWS) // 8) * 8


def _bdot(a, b):
    return jnp.dot(a.astype(BF16), b.astype(BF16), preferred_element_type=F32)


def _sigmoid(x):
    return 0.5 * jnp.tanh(0.5 * x) + 0.5


@jax.custom_vjp
def _silu(x):
    return x * _sigmoid(x)


def _silu_fwd(x):
    s = _sigmoid(x)
    return x * s, (x, s)


def _silu_bwd(res, ct):
    x, s = res
    return (ct * (s * (1.0 + x * (1.0 - s))),)


_silu.defvjp(_silu_fwd, _silu_bwd)


def _normal_cdf(x):
    return 0.5 + 0.5 * lax.erf(x * (0.5 ** 0.5))


@jax.custom_vjp
def _gelu(x):
    return x * _normal_cdf(x)


def _gelu_fwd(x):
    cdf = _normal_cdf(x)
    return x * cdf, (x, cdf)


def _gelu_bwd(res, ct):
    x, cdf = res
    pdf = jnp.exp(-0.5 * x * x) * ((2.0 * jnp.pi) ** -0.5)
    return (ct * (cdf + x * pdf),)


_gelu.defvjp(_gelu_fwd, _gelu_bwd)


def _softplus(x):
    return jnp.maximum(x, 0.0) + jnp.log1p(jnp.exp(-jnp.abs(x)))


def _l2n(x):
    return x * lax.rsqrt(jnp.sum(x * x, axis=-1, keepdims=True) + EPS)


def _rms(x):
    r = lax.rsqrt(jnp.mean(x * x, axis=-1, keepdims=True) + EPS)
    return x * r, r


def _rms_bwd(dn, n, r):
    return r * (dn - n * jnp.mean(dn * n, axis=-1, keepdims=True))


def _onehot_row(idx, width):
    return (lax.broadcasted_iota(jnp.int32, (1, width), 1) == idx).astype(F32)


def _rowsum(x):
    return jnp.sum(x, axis=0, keepdims=True)


def _iota2(n):
    return lax.broadcasted_iota(jnp.int32, (n, n), 0), lax.broadcasted_iota(jnp.int32, (n, n), 1)


def _bmm(a, b):
    return lax.dot_general(a.astype(BF16), b.astype(BF16), (((2,), (1,)), ((0,), (0,))), preferred_element_type=F32)


def _bmm_nt(a, b):
    return lax.dot_general(a.astype(BF16), b.astype(BF16), (((2,), (2,)), ((0,), (0,))), preferred_element_type=F32)


def _bmm_tn(a, b):
    return lax.dot_general(a.astype(BF16), b.astype(BF16), (((1,), (1,)), ((0,), (0,))), preferred_element_type=F32)


def _tri_inv_impl(a):
    n = a.shape[-1]
    r, c = _iota2(n)
    x = r ^ c
    eye = (r == c).astype(F32)
    ad = jnp.where(x < 16, a, 0.0)
    p2 = _bmm(ad, ad)
    e = p2 - ad - _bmm(ad, p2)
    p4 = _bmm(p2, p2)
    e = e + p4 + _bmm(e, p4)
    p8 = _bmm(p4, p4)
    e = e + p8 + _bmm(e, p8)
    size = 16
    while size < n:
        m = jnp.where(jnp.logical_and(x < 2 * size, x >= size), a, 0.0)
        f = m + _bmm(m, e)
        e = e - f - _bmm(e, f)
        size *= 2
    return e + eye


@jax.custom_vjp
def _tri_inv(a, known):
    return _tri_inv_impl(a) if known is None else known


def _tri_inv_fwd(a, known):
    t = _tri_inv(a, known)
    return t, (t, known)


def _tri_inv_bwd(res, dt):
    t, known = res
    return -_bmm_tn(t, _bmm_nt(dt, t)), None if known is None else jnp.zeros_like(known)


_tri_inv.defvjp(_tri_inv_fwd, _tri_inv_bwd)


def _sgu_core(u, v, z, lg, lb, ws, bcol):
    n = ws.shape[0]
    r, c = _iota2(n)
    wm = jnp.where(r >= c, ws, 0.0)
    gu = _gelu(u)
    gv = _gelu(v)
    xc = gv - jnp.mean(gv, axis=-1, keepdims=True)
    ln = xc * lax.rsqrt(jnp.mean(xc * xc, axis=-1, keepdims=True) + EPS) * lg + lb
    s = _bdot(wm, ln) + bcol
    return gu * s * _silu(z)


def _lanes_of(x):
    return jnp.concatenate([x[i] for i in range(x.shape[0])], axis=1)


def _batch_of(x, width):
    return jnp.concatenate([x[None, :, i * width:(i + 1) * width] for i in range(x.shape[1] // width)], axis=0)


def _mask_dot(mask, x):
    hi = x.astype(BF16)
    lo = (x - hi.astype(F32)).astype(BF16)
    m = mask.astype(BF16)
    return jnp.dot(m, hi, preferred_element_type=F32) + jnp.dot(m, lo, preferred_element_type=F32)


def _tri_mask(n, upper):
    r, c = _iota2(n)
    return (r <= c) if upper else (r >= c)


@jax.custom_vjp
def _cumsum_rows(x):
    return _mask_dot(_tri_mask(x.shape[0], False), x)


def _cumsum_rows_fwd(x):
    return _cumsum_rows(x), None


def _cumsum_rows_bwd(_, ct):
    return (_mask_dot(_tri_mask(ct.shape[0], True), ct),)


_cumsum_rows.defvjp(_cumsum_rows_fwd, _cumsum_rows_bwd)


@jax.custom_vjp
def _colsum_all_rows(x):
    return _mask_dot(jnp.ones((x.shape[0], x.shape[0]), jnp.bool_), x)


def _colsum_all_rows_fwd(x):
    return _colsum_all_rows(x), None


def _colsum_all_rows_bwd(_, ct):
    return (_mask_dot(jnp.ones((ct.shape[0], ct.shape[0]), jnp.bool_), ct),)


_colsum_all_rows.defvjp(_colsum_all_rows_fwd, _colsum_all_rows_bwd)


def _dn_core(cq, ck, cv, z, logits, state, alog, dtb, og, t_known=None):
    gn, cn, dh = cq.shape
    heads = gn // logits.shape[0]
    q = _l2n(_silu(cq)) * (dh ** -0.5)
    k = _l2n(_silu(ck))
    v = _silu(cv)
    beta_lanes = _sigmoid(logits)
    g_lanes = -jnp.exp(alog) * _softplus(logits + dtb)
    column = lambda rows, lane: jnp.sum(rows * _onehot_row(lane, rows.shape[-1]), axis=-1, keepdims=True)[None]
    beta = jnp.concatenate([column(beta_lanes[i // heads], i % heads) for i in range(gn)], axis=0)
    g = jnp.concatenate([column(g_lanes[i // heads], heads + i % heads) for i in range(gn)], axis=0)
    r, c = _iota2(cn)
    tril = r >= c
    rw = lax.broadcasted_iota(jnp.int32, (cn, dh), 0)
    cw = lax.broadcasted_iota(jnp.int32, (cn, dh), 1)
    upper_wide = (rw <= cw).astype(F32)
    g_wide = jnp.broadcast_to(g, (gn, cn, dh))
    gc_wide = _batch_of(_cumsum_rows(_lanes_of(g_wide)), dh)
    gc_cols = _batch_of(_colsum_all_rows(_lanes_of(g_wide * upper_wide)), dh)[:, :, :cn]
    decay = jnp.exp(jnp.where(tril, gc_wide[:, :, :cn] - gc_cols, -1e30))
    kb = k * beta
    kk = _bmm_nt(kb, k) * decay
    t = _tri_inv(jnp.where(r > c, kk, 0.0), t_known)
    eg = jnp.exp(gc_wide)
    sol = _bmm(t, jnp.concatenate([v * beta, kb * eg], axis=-1))
    u_val, w_dec = sol[:, :, :dh], sol[:, :, dh:]
    qk = _bmm_nt(q, k) * decay
    g_last = jnp.sum(g_wide, axis=1, keepdims=True)
    k_dec = k * jnp.exp(g_last - gc_wide)
    ws = _bmm(jnp.concatenate([w_dec, q * eg], axis=1), state)
    v_new = u_val - ws[:, :cn]
    o = ws[:, cn:] + _bmm(qk, v_new)
    new_state = state * jnp.exp(g_last) + _bmm_tn(k_dec, v_new)
    on, _ = _rms(o)
    return on * og * _silu(z), new_state, t


N_CHIPS = 4
HBM_SPEC = pl.BlockSpec(memory_space=pl.ANY)


def _place():
    return lax.axis_index("x"), lax.axis_index("y"), lax.axis_index("c")


def _other_chip(k):
    x, y, _ = _place()
    px = 1 - x if k & 2 else x
    py = 1 - y if k & 1 else y
    return px, py, 2 * px + py


def _remote(src, dst, send_sem, recv_sem, device):
    return pltpu.make_async_remote_copy(src_ref=src, dst_ref=dst, send_sem=send_sem, recv_sem=recv_sem,
                                        device_id=device, device_id_type=MESH)


def _other_device(k):
    x, y, c = _place()
    px = 1 - x if k & 4 else x
    py = 1 - y if k & 2 else y
    pc = 1 - c if k & 1 else c
    return (px, py, pc), 4 * px + 2 * py + pc


def _direct_exchange(srcs, outs, send_sems, recv_sems, local_sems, gather):
    x, y, c = _place()
    me = 4 * x + 2 * y + c

    def copies(arriving):
        out_list = []
        for a, (src, out) in enumerate(zip(srcs, outs)):
            for k in range(1, N_DEV):
                peer, index = _other_device(k)
                mine = src if gather else src.at[index]
                out_list.append(_remote(mine, out.at[index if arriving else me], send_sems.at[a, k - 1],
                                        recv_sems.at[a, k - 1], peer))
        return out_list

    def local_copies():
        return [pltpu.make_async_copy(src if gather else src.at[me], out.at[me], local_sems.at[a])
                for a, (src, out) in enumerate(zip(srcs, outs))]

    def start():
        for cp in local_copies() + copies(False):
            cp.start()

    def wait():
        for cp in copies(True):
            cp.wait_recv()
        for cp in copies(False):
            cp.wait_send()
        for cp in local_copies():
            cp.wait()

    return start, wait


def _exchange_scratch(n):
    return [pltpu.SemaphoreType.DMA((n, N_DEV - 1)), pltpu.SemaphoreType.DMA((n, N_DEV - 1)), pltpu.SemaphoreType.DMA((n,))]


def _all_gather(shards):
    n = len(shards)

    def body(*refs):
        srcs, outs = refs[:n], refs[n:2 * n]
        send_sems, recv_sems, local_sems = refs[2 * n:]
        x, y, c = _place()
        me = 4 * x + 2 * y + c
        sibling = (x, y, 1 - c)
        local = [pltpu.make_async_copy(srcs[a], outs[a].at[me], local_sems.at[a]) for a in range(n)]
        for cp in local:
            cp.start()
        sends = []
        for a in range(n):
            sends.append(_remote(srcs[a], outs[a].at[me], send_sems.at[a, 0], recv_sems.at[a, 0], sibling))
        for k in range(1, N_CHIPS):
            px, py, _ = _other_chip(k)
            for a in range(n):
                sends.append(_remote(srcs[a], outs[a].at[me], send_sems.at[a, k], recv_sems.at[a, k], (px, py, c)))
        for cp in sends:
            cp.start()
        passed = []
        for k in range(1, N_CHIPS):
            px, py, _ = _other_chip(k)
            blk = 4 * px + 2 * py + c
            for a in range(n):
                _remote(srcs[a], outs[a].at[blk], send_sems.at[a, k], recv_sems.at[a, k], (px, py, c)).wait_recv()
            for a in range(n):
                cp = _remote(outs[a].at[blk], outs[a].at[blk], send_sems.at[a, 3 + k], recv_sems.at[a, 3 + k], sibling)
                cp.start()
                passed.append(cp)
        for a in range(n):
            _remote(srcs[a], outs[a].at[me + 1 - 2 * c], send_sems.at[a, 0], recv_sems.at[a, 0], sibling).wait_recv()
        for k in range(1, N_CHIPS):
            px, py, _ = _other_chip(k)
            blk = 4 * px + 2 * py + 1 - c
            for a in range(n):
                _remote(srcs[a], outs[a].at[blk], send_sems.at[a, 3 + k], recv_sems.at[a, 3 + k], sibling).wait_recv()
        for cp in sends + passed:
            cp.wait_send()
        for cp in local:
            cp.wait()

    return pl.pallas_call(
        body, name="all_gather_weights",
        out_shape=tuple(jax.ShapeDtypeStruct((N_DEV,) + a.shape, a.dtype) for a in shards),
        in_specs=[HBM_SPEC] * n, out_specs=(HBM_SPEC,) * n,
        scratch_shapes=[pltpu.SemaphoreType.DMA((n, N_DEV - 1)), pltpu.SemaphoreType.DMA((n, N_DEV - 1)),
                        pltpu.SemaphoreType.DMA((n,))],
    )(*shards)


def _reduce_exchange(by_device, small):
    _, rows, cols = by_device.shape

    def body(g_ref, small_ref, out_ref, small_out_ref, from_sibling, small_from_sibling, stage, sums, small_own, small_sum,
             pair_send, pair_recv, chip_send, chip_recv, local_sems):
        x, y, c = _place()
        mine = 2 * x + y
        sibling = (x, y, 1 - c)
        chips = [(x, y, mine)] + [_other_chip(k) for k in range(1, N_CHIPS)]
        to_sibling = [_remote(g_ref.at[2 * chips[k][2] + 1 - c], from_sibling.at[k], pair_send.at[k], pair_recv.at[k], sibling)
                      for k in range(N_CHIPS)]
        to_sibling.append(_remote(small_ref, small_from_sibling, pair_send.at[N_CHIPS], pair_recv.at[N_CHIPS], sibling))
        for cp in to_sibling:
            cp.start()
        small_mine = pltpu.make_async_copy(small_ref, small_own, local_sems.at[0])
        small_mine.start()
        to_chips = []
        for k in (1, 2, 3, 0):
            px, py, chip = chips[k]
            mine_k = pltpu.make_async_copy(g_ref.at[2 * chip + c], stage, local_sems.at[1])
            mine_k.start()
            to_sibling[k].wait_recv()
            mine_k.wait()
            sums[k] = (stage[...] + from_sibling[k]).astype(sums.dtype)
            if k:
                cp = _remote(sums.at[k], out_ref.at[mine], chip_send.at[0, k - 1], chip_recv.at[0, k - 1], (px, py, c))
                cp.start()
                to_chips.append(cp)
        own_block = pltpu.make_async_copy(sums.at[0], out_ref.at[mine], local_sems.at[2])
        own_block.start()
        to_sibling[N_CHIPS].wait_recv()
        small_mine.wait()
        small_sum[...] = small_own[...] + small_from_sibling[...]
        for k in range(1, N_CHIPS):
            px, py, _ = chips[k]
            cp = _remote(small_sum, small_out_ref.at[mine], chip_send.at[1, k - 1], chip_recv.at[1, k - 1], (px, py, c))
            cp.start()
            to_chips.append(cp)
        own_small = pltpu.make_async_copy(small_sum, small_out_ref.at[mine], local_sems.at[3])
        own_small.start()
        for k in range(1, N_CHIPS):
            px, py, chip = chips[k]
            _remote(sums.at[k], out_ref.at[chip], chip_send.at[0, k - 1], chip_recv.at[0, k - 1], (px, py, c)).wait_recv()
            _remote(small_sum, small_out_ref.at[chip], chip_send.at[1, k - 1], chip_recv.at[1, k - 1], (px, py, c)).wait_recv()
        for cp in to_sibling + to_chips:
            cp.wait_send()
        own_block.wait()
        own_small.wait()

    return pl.pallas_call(
        body, name="grad_reduce_exchange",
        out_shape=(jax.ShapeDtypeStruct((N_CHIPS, rows, cols), BF16), jax.ShapeDtypeStruct((N_CHIPS,) + small.shape, F32)),
        in_specs=[HBM_SPEC, HBM_SPEC], out_specs=(HBM_SPEC, HBM_SPEC),
        scratch_shapes=[pltpu.VMEM((N_CHIPS, rows, cols), F32), pltpu.VMEM(small.shape, F32), pltpu.VMEM((rows, cols), F32),
                        pltpu.VMEM((N_CHIPS, rows, cols), BF16), pltpu.VMEM(small.shape, F32), pltpu.VMEM(small.shape, F32),
                        pltpu.SemaphoreType.DMA((N_CHIPS + 1,)), pltpu.SemaphoreType.DMA((N_CHIPS + 1,)),
                        pltpu.SemaphoreType.DMA((2, N_CHIPS - 1)), pltpu.SemaphoreType.DMA((2, N_CHIPS - 1)),
                        pltpu.SemaphoreType.DMA((4,))],
        compiler_params=pltpu.CompilerParams(vmem_limit_bytes=VMEM_LIMIT),
    )(by_device, small)


def _params(n_axes):
    return pltpu.CompilerParams(dimension_semantics=("arbitrary",) * n_axes, vmem_limit_bytes=VMEM_LIMIT)


def _whole(shape):
    return pl.BlockSpec(shape, lambda *_: (0,) * len(shape))


VMEM_SPEC = pl.BlockSpec(memory_space=pltpu.VMEM)


def _inproj_fwd(x2, seq_len, norm_g, wat, wqt, wzt, wgt, sgu_weights, conv_w, later_shards):
    t = x2.shape[0]
    tm = min(512, seq_len)
    tiles_per_seq = seq_len // tm
    steps = t // tm
    ns = len(later_shards)

    def body(x_ref, g_ref, wa_ref, wq_ref, wz_ref, wg_ref, lg_ref, lb_ref, ws_ref, bt_ref, cw_ref, *rest):
        shard_refs, rest = rest[:ns], rest[ns:]
        a_ref, q_ref, z_ref, l_ref, sgu_ref, c_ref = rest[:6]
        gathered_refs, (xpad_ref, send_sems, recv_sems, local_sems) = rest[6:6 + ns], rest[6 + ns:]
        start_gather, wait_gather = _direct_exchange(shard_refs, gathered_refs, send_sems, recv_sems, local_sems, True)
        pl.when(pl.program_id(0) == 0)(start_gather)
        n, _ = _rms(x_ref[...])
        xn = (n * g_ref[...]).astype(BF16)
        for w_ref, o_ref in ((wa_ref, a_ref), (wq_ref, q_ref), (wz_ref, z_ref), (wg_ref, l_ref)):
            width = w_ref.shape[0]
            for c0 in range(0, width, 512):
                c1 = min(c0 + 512, width)
                o_ref[:, c0:c1] = lax.dot_general(xn, w_ref[c0:c1, :], (((1,), (1,)), ((), ())),
                                                  preferred_element_type=F32)
        for row0 in range(0, tm, SGU_CHUNK):
            for grp in range(SGU_GROUPS):
                args = _sgu_pieces(a_ref, lg_ref, lb_ref, ws_ref, bt_ref, row0, grp)
                sgu_ref[pl.ds(row0, SGU_CHUNK), pl.ds(grp * 128, 128)] = _sgu_core(*args).astype(sgu_ref.dtype)

        @pl.when(pl.program_id(0) % tiles_per_seq == 0)
        def _():
            xpad_ref[0:CONV_HALO, :] = jnp.zeros((CONV_HALO, xpad_ref.shape[1]), F32)

        xpad_ref[CONV_HALO:, :] = q_ref[...]
        acc = None
        for j in range(CONV_K):
            term = cw_ref[j:j + 1, :] * xpad_ref[pl.ds(CONV_HALO - CONV_K + 1 + j, tm), :]
            acc = term if acc is None else acc + term
        c_ref[...] = acc
        xpad_ref[0:CONV_HALO, :] = xpad_ref[tm:tm + CONV_HALO, :]
        pl.when(pl.program_id(0) == steps - 1)(wait_gather)

    widths = (wat.shape[0], wqt.shape[0], wzt.shape[0], wgt.shape[0])
    tile = lambda w: pl.BlockSpec((tm, w), lambda i: (i, 0))
    sgu_shapes = ((1, SGU_WIDTH), (1, SGU_WIDTH), (SGU_GROUPS, SGU_CHUNK, SGU_CHUNK), (SGU_CHUNK, SGU_GROUPS))
    return pl.pallas_call(
        body, name="inproj_sgu_conv_fwd", grid=(steps,),
        out_shape=tuple(jax.ShapeDtypeStruct((t, w), F32) for w in widths)
        + (jax.ShapeDtypeStruct((t, SGU_WIDTH), BF16), jax.ShapeDtypeStruct((t, widths[1]), F32))
        + tuple(jax.ShapeDtypeStruct((N_DEV,) + a.shape, a.dtype) for a in later_shards),
        in_specs=[tile(D_MODEL), _whole((1, D_MODEL)), VMEM_SPEC, VMEM_SPEC, VMEM_SPEC, VMEM_SPEC]
        + [_whole(s) for s in sgu_shapes] + [_whole((CONV_K, widths[1]))] + [HBM_SPEC] * ns,
        out_specs=tuple(tile(w) for w in widths) + (tile(SGU_WIDTH), tile(widths[1])) + (HBM_SPEC,) * ns,
        scratch_shapes=[pltpu.VMEM((CONV_HALO + tm, widths[1]), F32)] + _exchange_scratch(ns),
        compiler_params=_params(1),
    )(x2, norm_g, wat, wqt, wzt, wgt, *sgu_weights, conv_w, *later_shards)


def _sgu_pieces(uvz_ref, lg_ref, lb_ref, ws_ref, bt_ref, row0, grp):
    rows = pl.ds(row0, SGU_CHUNK)
    lanes = pl.ds(grp * 128, 128)
    u = uvz_ref[rows, pl.ds(grp * 128, 128)]
    v = uvz_ref[rows, pl.ds(SGU_WIDTH + grp * 128, 128)]
    z = uvz_ref[rows, pl.ds(2 * SGU_WIDTH + grp * 128, 128)]
    bcol = jnp.sum(bt_ref[...] * _onehot_row(grp, SGU_GROUPS), axis=-1, keepdims=True)
    return u, v, z, lg_ref[:, lanes], lb_ref[:, lanes], ws_ref[grp], bcol


def _sgu_bwd_tile(uvz_ref, do_ref, sgu_refs, duvz_ref, grad_refs):
    lg_ref, lb_ref, ws_ref, bt_ref = sgu_refs
    dlg_ref, dlb_ref, dws_ref, dbt_ref = grad_refs
    for row0 in range(0, uvz_ref.shape[0], SGU_CHUNK):
        rows = pl.ds(row0, SGU_CHUNK)
        for grp in range(SGU_GROUPS):
            lanes = pl.ds(grp * 128, 128)
            args = _sgu_pieces(uvz_ref, lg_ref, lb_ref, ws_ref, bt_ref, row0, grp)
            _, pull = jax.vjp(_sgu_core, *args)
            du, dv, dz, dlg, dlb, dws, dbcol = pull(do_ref[rows, lanes])
            duvz_ref[rows, pl.ds(grp * 128, 128)] = du.astype(duvz_ref.dtype)
            duvz_ref[rows, pl.ds(SGU_WIDTH + grp * 128, 128)] = dv.astype(duvz_ref.dtype)
            duvz_ref[rows, pl.ds(2 * SGU_WIDTH + grp * 128, 128)] = dz.astype(duvz_ref.dtype)
            dlg_ref[:, lanes] += dlg
            dlb_ref[:, lanes] += dlb
            dws_ref[grp] += dws
            dbt_ref[...] += dbcol * _onehot_row(grp, SGU_GROUPS)


def _dn_pairs(nb):
    return [(b, h) for b in range(nb) for h in range(DN_HEADS)]


def _dn_batch_args(c_ref, z_ref):
    pairs = _dn_pairs(c_ref.shape[0])
    pick = lambda ref, b, col: ref[b, :, pl.ds(col, DN_HEAD_DIM)]
    cq = jnp.stack([pick(c_ref, b, h * DN_HEAD_DIM) for b, h in pairs])
    ck = jnp.stack([pick(c_ref, b, DN_WIDTH + h * DN_HEAD_DIM) for b, h in pairs])
    cv = jnp.stack([pick(c_ref, b, 2 * DN_WIDTH + h * DN_HEAD_DIM) for b, h in pairs])
    z = jnp.stack([pick(z_ref, b, h * DN_HEAD_DIM) for b, h in pairs])
    return cq, ck, cv, z


def _dn_weight_specs():
    return [_whole((CONV_K, 3 * DN_WIDTH)), _whole((1, GATE_PAD)), _whole((1, GATE_PAD)), _whole((1, DN_HEAD_DIM))]


def _dn_fwd(conv_out, zg, logits, alog, dtb, og):
    nb, s, _ = conv_out.shape
    nc = s // DN_CHUNK
    pairs = _dn_pairs(nb)
    gn = len(pairs)
    chunk = lambda w: pl.BlockSpec((nb, DN_CHUNK, w), lambda n: (0, n, 0))

    def body(c_ref, z_ref, l_ref, alog_ref, dtb_ref, og_ref, out_ref, st_ref, inv_ref, state_ref):
        n = pl.program_id(0)

        @pl.when(n == 0)
        def _():
            state_ref[...] = jnp.zeros_like(state_ref)

        cq, ck, cv, z = _dn_batch_args(c_ref, z_ref)
        state = state_ref[...]
        st_ref[...] = state
        out, new_state, t = _dn_core(cq, ck, cv, z, l_ref[...], state, alog_ref[...], dtb_ref[...], og_ref[...])
        state_ref[...] = new_state
        inv_ref[...] = t.astype(inv_ref.dtype)
        for i, (b, h) in enumerate(pairs):
            out_ref[b, :, pl.ds(h * DN_HEAD_DIM, DN_HEAD_DIM)] = out[i].astype(out_ref.dtype)

    per_chunk = pl.BlockSpec((None, gn, DN_HEAD_DIM, DN_HEAD_DIM), lambda n: (n, 0, 0, 0))
    return pl.pallas_call(
        body, name="deltanet_fwd", grid=(nc,),
        out_shape=(jax.ShapeDtypeStruct((nb, s, DN_WIDTH), BF16),
                   jax.ShapeDtypeStruct((nc, gn, DN_HEAD_DIM, DN_HEAD_DIM), F32),
                   jax.ShapeDtypeStruct((nc, gn, DN_CHUNK, DN_CHUNK), BF16)),
        in_specs=[chunk(3 * DN_WIDTH), chunk(DN_WIDTH), chunk(GATE_PAD)] + _dn_weight_specs()[1:],
        out_specs=(chunk(DN_WIDTH), per_chunk, pl.BlockSpec((None, gn, DN_CHUNK, DN_CHUNK), lambda n: (n, 0, 0, 0))),
        scratch_shapes=[pltpu.VMEM((gn, DN_HEAD_DIM, DN_HEAD_DIM), F32)],
        compiler_params=_params(1),
    )(conv_out, zg, logits, alog, dtb, og)


def _dn_bwd(qkv, conv_out, zg, logits, conv_w, alog, dtb, og, states, inverses, d_out, head_grads):
    nb, s, _ = qkv.shape
    nc = s // DN_CHUNK
    rev = lambda n: nc - 1 - n
    pairs = _dn_pairs(nb)
    gn = len(pairs)
    ng = len(head_grads)

    def body(cur_ref, c_ref, z_ref, l_ref, w_ref, alog_ref, dtb_ref, og_ref, st_ref, inv_ref, do_ref, *rest):
        grad_refs, rest = rest[:ng], rest[ng:]
        dqkv_ref, dz_ref, dl_ref, dw_ref, dalog_ref, ddtb_ref, dog_ref = rest[:7]
        recv_refs, (dstate_ref, dcpad_ref, send_sems, recv_sems, local_sems) = rest[7:7 + ng], rest[7 + ng:]
        n = pl.program_id(0)
        start_exchange, wait_exchange = _direct_exchange(grad_refs, recv_refs, send_sems, recv_sems, local_sems, False)
        pl.when(n == 0)(start_exchange)

        @pl.when(n == 0)
        def _():
            dw_ref[...] = jnp.zeros_like(dw_ref)
            dalog_ref[...] = jnp.zeros_like(dalog_ref)
            ddtb_ref[...] = jnp.zeros_like(ddtb_ref)
            dog_ref[...] = jnp.zeros_like(dog_ref)
            dstate_ref[...] = jnp.zeros_like(dstate_ref)
            dcpad_ref[:, DN_CHUNK:, :] = jnp.zeros((nb, CONV_HALO, 3 * DN_WIDTH), F32)

        cq, ck, cv, z = _dn_batch_args(c_ref, z_ref)
        d_out_g = jnp.stack([do_ref[b, :, pl.ds(h * DN_HEAD_DIM, DN_HEAD_DIM)] for b, h in pairs])
        t_known = inv_ref[...].astype(F32)
        core = lambda *args: _dn_core(*args, t_known=t_known)[:2]
        _, pull = jax.vjp(core, cq, ck, cv, z, l_ref[...], st_ref[...], alog_ref[...], dtb_ref[...], og_ref[...])
        dcq, dck, dcv, dz, dlog, dstate, dalog, ddtb, dog = pull((d_out_g, dstate_ref[...]))
        dstate_ref[...] = dstate
        dl_ref[...] = dlog.astype(dl_ref.dtype)
        dalog_ref[...] += dalog
        ddtb_ref[...] += ddtb
        dog_ref[...] += dog
        for i, (b, h) in enumerate(pairs):
            dcpad_ref[b, 0:DN_CHUNK, pl.ds(h * DN_HEAD_DIM, DN_HEAD_DIM)] = dcq[i]
            dcpad_ref[b, 0:DN_CHUNK, pl.ds(DN_WIDTH + h * DN_HEAD_DIM, DN_HEAD_DIM)] = dck[i]
            dcpad_ref[b, 0:DN_CHUNK, pl.ds(2 * DN_WIDTH + h * DN_HEAD_DIM, DN_HEAD_DIM)] = dcv[i]
            dz_ref[b, :, pl.ds(h * DN_HEAD_DIM, DN_HEAD_DIM)] = dz[i].astype(dz_ref.dtype)
        for b in range(nb):
            xb = cur_ref[b]
            dx = None
            for j in range(CONV_K):
                shifted = dcpad_ref[b, pl.ds(CONV_K - 1 - j, DN_CHUNK), :]
                term = w_ref[j:j + 1, :] * shifted
                dx = term if dx is None else dx + term
                dw_ref[j:j + 1, :] += _rowsum(shifted * xb)
            dqkv_ref[b] = dx.astype(dqkv_ref.dtype)
            dcpad_ref[b, DN_CHUNK:, :] = dcpad_ref[b, 0:CONV_HALO, :]
        pl.when(n == nc - 1)(wait_exchange)

    chunk = lambda w: pl.BlockSpec((nb, DN_CHUNK, w), lambda n: (0, rev(n), 0))
    return pl.pallas_call(
        body, name="deltanet_bwd", grid=(nc,),
        out_shape=(jax.ShapeDtypeStruct((nb, s, 3 * DN_WIDTH), BF16), jax.ShapeDtypeStruct((nb, s, DN_WIDTH), BF16),
                   jax.ShapeDtypeStruct((nb, s, GATE_PAD), BF16), jax.ShapeDtypeStruct((CONV_K, 3 * DN_WIDTH), F32),
                   jax.ShapeDtypeStruct((1, GATE_PAD), F32), jax.ShapeDtypeStruct((1, GATE_PAD), F32),
                   jax.ShapeDtypeStruct((1, DN_HEAD_DIM), F32))
        + tuple(jax.ShapeDtypeStruct(a.shape, a.dtype) for a in head_grads),
        in_specs=[chunk(3 * DN_WIDTH), chunk(3 * DN_WIDTH), chunk(DN_WIDTH), chunk(GATE_PAD)] + _dn_weight_specs() + [
            pl.BlockSpec((None, gn, DN_HEAD_DIM, DN_HEAD_DIM), lambda n: (rev(n), 0, 0, 0)),
            pl.BlockSpec((None, gn, DN_CHUNK, DN_CHUNK), lambda n: (rev(n), 0, 0, 0)),
            chunk(DN_WIDTH)] + [HBM_SPEC] * ng,
        out_specs=(chunk(3 * DN_WIDTH), chunk(DN_WIDTH), chunk(GATE_PAD), _whole((CONV_K, 3 * DN_WIDTH)),
                   _whole((1, GATE_PAD)), _whole((1, GATE_PAD)), _whole((1, DN_HEAD_DIM))) + (HBM_SPEC,) * ng,
        scratch_shapes=[pltpu.VMEM((gn, DN_HEAD_DIM, DN_HEAD_DIM), F32),
                        pltpu.VMEM((nb, DN_CHUNK + CONV_HALO, 3 * DN_WIDTH), F32)] + _exchange_scratch(ng),
        compiler_params=_params(1),
    )(qkv, conv_out, zg, logits, conv_w, alog, dtb, og, states, inverses, d_out, *head_grads)


def _head(a_out, b_out, x2, p2, target, w_out, w_out_t, w_gate, w_gate_t, w_proj, ple_g, fin_g):
    t = x2.shape[0]
    tm = min(512, t)
    steps = t // tm

    def body(a_ref, b_ref, x_ref, p_ref, y_ref, wo_ref, wot_ref, wg_ref, wgt_ref, wp_ref, pg_ref, fg_ref,
             da_ref, db_ref, dh_ref, dwo_hbm, dwg_hbm, dwp_hbm, dpg_ref, dfg_ref, loss_ref,
             dwo_acc, dwg_acc, dwp_acc, rows_stage, cols_stage):
        i = pl.program_id(0)

        @pl.when(i == 0)
        def _():
            dwo_acc[...] = jnp.zeros_like(dwo_acc)
            dwg_acc[...] = jnp.zeros_like(dwg_acc)
            dwp_acc[...] = jnp.zeros_like(dwp_acc)
            dpg_ref[...] = jnp.zeros_like(dpg_ref)
            dfg_ref[...] = jnp.zeros_like(dfg_ref)
            loss_ref[...] = jnp.zeros_like(loss_ref)

        a = a_ref[...]
        bb = b_ref[...]
        pb = p_ref[...].astype(BF16)
        pg = pg_ref[...]
        fg = fg_ref[...]
        h1 = (x_ref[...] + jnp.dot(a, wo_ref[0:SGU_WIDTH, :], preferred_element_type=F32)
              + jnp.dot(bb, wo_ref[SGU_WIDTH:, :], preferred_element_type=F32))
        n1, r1 = _rms(h1)
        rn = (n1 * pg).astype(BF16)
        gate = _sigmoid(jnp.dot(rn, wg_ref[...], preferred_element_type=F32))
        pp = jnp.dot(pb, wp_ref[...], preferred_element_type=F32)
        h2 = h1 + gate * pp
        n2, r2 = _rms(h2)
        err = n2 * fg - y_ref[...]
        loss_ref[...] += jnp.broadcast_to(_rowsum(jnp.sum(err * err, axis=-1, keepdims=True)), loss_ref.shape)

        dy = err * (1.0 / D_MODEL)
        dfg_ref[...] += _rowsum(dy * n2)
        dh2 = _rms_bwd(dy * fg, n2, r2)
        dpp = (dh2 * gate).astype(BF16)
        dgl = (dh2 * pp * gate * (1.0 - gate)).astype(BF16)
        dwp_acc[...] += lax.dot_general(pb, dpp, (((0,), (0,)), ((), ())), preferred_element_type=F32)
        dwg_acc[...] += lax.dot_general(rn, dgl, (((0,), (0,)), ((), ())), preferred_element_type=F32)
        drn = jnp.dot(dgl, wgt_ref[...], preferred_element_type=F32)
        dpg_ref[...] += _rowsum(drn * n1)
        dh1 = dh2 + _rms_bwd(drn * pg, n1, r1)
        dh_ref[...] = dh1
        dhb = dh1.astype(BF16)
        da_ref[...] = jnp.dot(dhb, wot_ref[:, 0:SGU_WIDTH], preferred_element_type=F32)
        db_ref[...] = jnp.dot(dhb, wot_ref[:, SGU_WIDTH:], preferred_element_type=F32)
        dwo_acc[0:SGU_WIDTH, :] += lax.dot_general(a, dhb, (((0,), (0,)), ((), ())), preferred_element_type=F32)
        dwo_acc[SGU_WIDTH:, :] += lax.dot_general(bb, dhb, (((0,), (0,)), ((), ())), preferred_element_type=F32)

        @pl.when(i == steps - 1)
        def _():
            for j in range(N_DEV):
                for acc, hbm in ((dwo_acc, dwo_hbm), (dwg_acc, dwg_hbm)):
                    rows_stage[...] = acc[j * LANES:(j + 1) * LANES, :].astype(BF16)
                    pltpu.sync_copy(rows_stage, hbm.at[j])
                cols_stage[...] = dwp_acc[:, j * LANES:(j + 1) * LANES].astype(BF16)
                pltpu.sync_copy(cols_stage, dwp_hbm.at[j])

    tile = lambda w: pl.BlockSpec((tm, w), lambda i: (i, 0))
    return pl.pallas_call(
        body, name="head_fwd_bwd", grid=(steps,),
        out_shape=(jax.ShapeDtypeStruct((t, SGU_WIDTH), F32), jax.ShapeDtypeStruct((t, DN_WIDTH), F32),
                   jax.ShapeDtypeStruct((t, D_MODEL), F32), jax.ShapeDtypeStruct((N_DEV, LANES, D_MODEL), BF16),
                   jax.ShapeDtypeStruct((N_DEV, LANES, D_MODEL), BF16), jax.ShapeDtypeStruct((N_DEV, PLE_DIM, LANES), BF16),
                   jax.ShapeDtypeStruct((1, D_MODEL), F32), jax.ShapeDtypeStruct((1, D_MODEL), F32),
                   jax.ShapeDtypeStruct((8, LANES), F32)),
        in_specs=[tile(SGU_WIDTH), tile(DN_WIDTH), tile(D_MODEL), tile(PLE_DIM), tile(D_MODEL),
                  VMEM_SPEC, VMEM_SPEC, VMEM_SPEC, VMEM_SPEC, VMEM_SPEC, _whole((1, D_MODEL)), _whole((1, D_MODEL))],
        out_specs=(tile(SGU_WIDTH), tile(DN_WIDTH), tile(D_MODEL), HBM_SPEC, HBM_SPEC, HBM_SPEC,
                   _whole((1, D_MODEL)), _whole((1, D_MODEL)), _whole((8, LANES))),
        scratch_shapes=[pltpu.VMEM((D_MODEL, D_MODEL), F32), pltpu.VMEM((D_MODEL, D_MODEL), F32),
                        pltpu.VMEM((PLE_DIM, D_MODEL), F32), pltpu.VMEM((LANES, D_MODEL), BF16),
                        pltpu.VMEM((PLE_DIM, LANES), BF16)],
        compiler_params=_params(1),
    )(a_out, b_out, x2, p2, target, w_out, w_out_t, w_gate, w_gate_t, w_proj, ple_g, fin_g)


def _inproj_bwd(x2, dh1, a_uvz, d_sgu, d_q, d_z, d_l, norm_g, sgu_weights, wat, wqt, wzt, wgt):
    t = x2.shape[0]
    tm = min(256, t)
    steps = t // tm

    widths = (a_uvz.shape[1], d_q.shape[1], d_z.shape[1], d_l.shape[1])
    starts = (0, widths[0], widths[0] + widths[1], widths[0] + widths[1] + widths[2])

    def body(x_ref, dh_ref, uvz_ref, dsgu_ref, dq_ref, dz_ref, dl_ref, g_ref, lg_ref, lb_ref, ws_ref, bt_ref,
             wat_ref, wqt_ref, wzt_ref, wgt_ref,
             dx_ref, dw_hbm, dg_ref, dlg_ref, dlb_ref, dws_ref, dbt_ref, dw_acc, stage_ref, da_ref):
        i = pl.program_id(0)

        @pl.when(i == 0)
        def _():
            dw_acc[...] = jnp.zeros_like(dw_acc)
            for ref in (dg_ref, dlg_ref, dlb_ref, dws_ref, dbt_ref):
                ref[...] = jnp.zeros_like(ref)

        _sgu_bwd_tile(uvz_ref, dsgu_ref, (lg_ref, lb_ref, ws_ref, bt_ref), da_ref, (dlg_ref, dlb_ref, dws_ref, dbt_ref))
        g = g_ref[...]
        n, r = _rms(x_ref[...])
        xn = (n * g).astype(BF16)
        dxn = None
        for d_ref, wt_ref, col0 in zip((da_ref, dq_ref, dz_ref, dl_ref), (wat_ref, wqt_ref, wzt_ref, wgt_ref), starts):
            term = jnp.dot(d_ref[...], wt_ref[...], preferred_element_type=F32)
            dxn = term if dxn is None else dxn + term
            width = d_ref.shape[1]
            for c0 in range(0, width, 512):
                c1 = min(c0 + 512, width)
                dw_acc[col0 + c0:col0 + c1, :] += lax.dot_general(d_ref[:, c0:c1], xn, (((0,), (0,)), ((), ())),
                                                                  preferred_element_type=F32)
        dg_ref[...] += _rowsum(dxn * n)
        dx_ref[...] = dh_ref[...] + _rms_bwd(dxn * g, n, r)

        @pl.when(i == steps - 1)
        def _():
            for j in range(N_DEV):
                stage_ref[...] = dw_acc[j * IN_SHARD:(j + 1) * IN_SHARD, :]
                pltpu.sync_copy(stage_ref, dw_hbm.at[j])

    tile = lambda w: pl.BlockSpec((tm, w), lambda i: (i, 0))
    sgu_shapes = ((1, SGU_WIDTH), (1, SGU_WIDTH), (SGU_GROUPS, SGU_CHUNK, SGU_CHUNK), (SGU_CHUNK, SGU_GROUPS))
    return pl.pallas_call(
        body, name="inproj_sgu_bwd", grid=(steps,),
        out_shape=(jax.ShapeDtypeStruct((t, D_MODEL), F32), jax.ShapeDtypeStruct((N_DEV, IN_SHARD, D_MODEL), F32),
                   jax.ShapeDtypeStruct((1, D_MODEL), F32)) + tuple(jax.ShapeDtypeStruct(s, F32) for s in sgu_shapes),
        in_specs=[tile(D_MODEL), tile(D_MODEL), tile(widths[0]), tile(SGU_WIDTH)] + [tile(w) for w in widths[1:]]
        + [_whole((1, D_MODEL))] + [_whole(s) for s in sgu_shapes] + [VMEM_SPEC] * 4,
        out_specs=(tile(D_MODEL), HBM_SPEC, _whole((1, D_MODEL))) + tuple(_whole(s) for s in sgu_shapes),
        scratch_shapes=[pltpu.VMEM((sum(widths), D_MODEL), F32), pltpu.VMEM((IN_SHARD, D_MODEL), F32),
                        pltpu.VMEM((tm, widths[0]), BF16)],
        compiler_params=_params(1),
    )(x2, dh1, a_uvz, d_sgu, d_q, d_z, d_l, norm_g, *sgu_weights, wat, wqt, wzt, wgt)


def _reduce_adamw(recv, w, m, v, name, col_block=None):
    n, rows, cols = recv.shape
    cb = col_block or cols
    lead = w.ndim - 2

    def body(r_ref, w_ref, m_ref, v_ref, g_ref, d_ref, nm_ref, nv_ref):
        g = r_ref[0].astype(F32)
        for i in range(1, n):
            g = g + r_ref[i].astype(F32)
        m_new = ADAM_B1 * m_ref[...] + (1.0 - ADAM_B1) * g
        v_new = ADAM_B2 * v_ref[...] + (1.0 - ADAM_B2) * jnp.square(g)
        m_hat = m_new / (1.0 - ADAM_B1 ** ADAM_STEP)
        v_hat = v_new / (1.0 - ADAM_B2 ** ADAM_STEP)
        g_ref[...] = g
        d_ref[...] = -ADAM_LR * (m_hat / (jnp.sqrt(v_hat) + ADAM_EPS) + ADAM_WD * w_ref[...])
        nm_ref[...] = m_new
        nv_ref[...] = v_new

    blk = pl.BlockSpec((None,) * lead + (rows, cb), lambda i: (0,) * lead + (0, i))
    return pl.pallas_call(
        body, name=name, grid=(cols // cb,),
        out_shape=tuple(jax.ShapeDtypeStruct(w.shape, F32) for _ in range(4)),
        in_specs=[pl.BlockSpec((n, rows, cb), lambda i: (0, 0, i)), blk, blk, blk],
        out_specs=(blk, blk, blk, blk),
        compiler_params=_params(1),
    )(recv, w, m, v)


def _adamw_replicated(received, ws, ms, vs):
    nw = len(ws)
    starts = [sum(SMALL_PIECE_ROWS[:i]) for i in range(len(SMALL_PIECE_ROWS))]

    def natural(g_ref, row0, shape):
        cols, rows = shape[-1], _size(shape[:-1])
        if cols == LANES:
            return g_ref[row0:row0 + rows, :].reshape(shape)
        if cols < LANES:
            return g_ref[row0:row0 + 1, 0:cols].reshape(shape)
        per = cols // LANES
        return jnp.concatenate(
            [jnp.concatenate([g_ref[row0 + r * per + k:row0 + r * per + k + 1, :] for k in range(per)], axis=1)
             for r in range(rows)], axis=0).reshape(shape)

    def body(r_ref, *refs):
        w_refs, m_refs, v_refs = refs[:nw], refs[nw:2 * nw], refs[2 * nw:3 * nw]
        conv_ref, loss_ref = refs[3 * nw], refs[3 * nw + 1]
        out_refs, g_ref = refs[3 * nw + 2:-1], refs[-1]
        g = r_ref[0]
        for q in range(1, N_CHIPS):
            g = g + r_ref[q]
        g_ref[...] = g
        conv_ref[...] = natural(g_ref, starts[0], (CONV_K, 3 * DN_WIDTH))
        loss_ref[...] = natural(g_ref, starts[-1], (1, 1))
        for i in range(nw):
            gi = natural(g_ref, starts[1 + i], w_refs[i].shape)
            m_new = ADAM_B1 * m_refs[i][...] + (1.0 - ADAM_B1) * gi
            v_new = ADAM_B2 * v_refs[i][...] + (1.0 - ADAM_B2) * jnp.square(gi)
            m_hat = m_new / (1.0 - ADAM_B1 ** ADAM_STEP)
            v_hat = v_new / (1.0 - ADAM_B2 ** ADAM_STEP)
            out_refs[4 * i][...] = gi
            out_refs[4 * i + 1][...] = -ADAM_LR * (m_hat / (jnp.sqrt(v_hat) + ADAM_EPS) + ADAM_WD * w_refs[i][...])
            out_refs[4 * i + 2][...] = m_new
            out_refs[4 * i + 3][...] = v_new

    def spec(a):
        lead = max(a.ndim - 3, 0)
        return pl.BlockSpec((None,) * lead + a.shape[lead:], lambda: (0,) * a.ndim)

    weight_specs = [spec(a) for a in ws]
    return pl.pallas_call(
        body, name="adamw_replicated",
        out_shape=(jax.ShapeDtypeStruct((CONV_K, 3 * DN_WIDTH), F32), jax.ShapeDtypeStruct((1, 1), F32))
        + tuple(jax.ShapeDtypeStruct(a.shape, F32) for a in ws for _ in range(4)),
        in_specs=[pl.BlockSpec(received.shape, lambda: (0, 0, 0))] + weight_specs * 3,
        out_specs=(pl.BlockSpec((CONV_K, 3 * DN_WIDTH), lambda: (0, 0)), pl.BlockSpec((1, 1), lambda: (0, 0)))
        + tuple(s for s in weight_specs for _ in range(4)),
        scratch_shapes=[pltpu.VMEM(received.shape[1:], F32)],
        compiler_params=pltpu.CompilerParams(vmem_limit_bytes=VMEM_LIMIT),
    )(received, *ws, *ms, *vs)


def _pack_rows(pieces, rows):
    padded = [jnp.pad(jnp.ravel(p), (0, -p.size % LANES)) for p in pieces]
    flat = jnp.concatenate(padded)
    return jnp.pad(flat, (0, rows * LANES - flat.shape[0])).reshape(rows, LANES)


def kernel(x, p, norm_g, w_in, sgu_ln_g, sgu_ln_b, sgu_w_s, sgu_b_s, dn_conv_w, dn_a_log, dn_dt_bias, dn_o_norm_g, w_out, ple_norm_g, ple_gate_w, ple_proj_w, final_norm_g, loss_target, m_norm_g, m_w_in, m_sgu_ln_g, m_sgu_ln_b, m_sgu_w_s, m_sgu_b_s, m_dn_conv_w, m_dn_a_log, m_dn_dt_bias, m_dn_o_norm_g, m_w_out, m_ple_norm_g, m_ple_gate_w, m_ple_proj_w, m_final_norm_g, v_norm_g, v_w_in, v_sgu_ln_g, v_sgu_ln_b, v_sgu_w_s, v_sgu_b_s, v_dn_conv_w, v_dn_a_log, v_dn_dt_bias, v_dn_o_norm_g, v_w_out, v_ple_norm_g, v_ple_gate_w, v_ple_proj_w, v_final_norm_g):
    weights = dict(norm_g=norm_g, w_in=w_in, sgu_ln_g=sgu_ln_g, sgu_ln_b=sgu_ln_b, sgu_w_s=sgu_w_s, sgu_b_s=sgu_b_s,
                   dn_conv_w=dn_conv_w, dn_a_log=dn_a_log, dn_dt_bias=dn_dt_bias, dn_o_norm_g=dn_o_norm_g, w_out=w_out,
                   ple_norm_g=ple_norm_g, ple_gate_w=ple_gate_w, ple_proj_w=ple_proj_w, final_norm_g=final_norm_g)
    mom1 = dict(norm_g=m_norm_g, w_in=m_w_in, sgu_ln_g=m_sgu_ln_g, sgu_ln_b=m_sgu_ln_b, sgu_w_s=m_sgu_w_s,
                sgu_b_s=m_sgu_b_s, dn_conv_w=m_dn_conv_w, dn_a_log=m_dn_a_log, dn_dt_bias=m_dn_dt_bias,
                dn_o_norm_g=m_dn_o_norm_g, w_out=m_w_out, ple_norm_g=m_ple_norm_g, ple_gate_w=m_ple_gate_w,
                ple_proj_w=m_ple_proj_w, final_norm_g=m_final_norm_g)
    mom2 = dict(norm_g=v_norm_g, w_in=v_w_in, sgu_ln_g=v_sgu_ln_g, sgu_ln_b=v_sgu_ln_b, sgu_w_s=v_sgu_w_s,
                sgu_b_s=v_sgu_b_s, dn_conv_w=v_dn_conv_w, dn_a_log=v_dn_a_log, dn_dt_bias=v_dn_dt_bias,
                dn_o_norm_g=v_dn_o_norm_g, w_out=v_w_out, ple_norm_g=v_ple_norm_g, ple_gate_w=v_ple_gate_w,
                ple_proj_w=v_ple_proj_w, final_norm_g=v_final_norm_g)
    nb, s, _ = x.shape
    t = nb * s

    transposed = lambda a: jnp.transpose(a, (2, 0, 1)).reshape(IN_SHARD, D_MODEL)
    w_in_t, m_in_t, v_in_t = transposed(w_in), transposed(m_w_in), transposed(v_w_in)
    w_in_blocks, conv_blocks = _all_gather([w_in_t.astype(BF16), dn_conv_w[0]])
    w_in_full_t = w_in_blocks.reshape(IN_COLS, D_MODEL)
    wat = w_in_full_t[:3 * SGU_WIDTH]
    wqt = w_in_full_t[3 * SGU_WIDTH:3 * SGU_WIDTH + 3 * DN_WIDTH]
    wzt = w_in_full_t[3 * SGU_WIDTH + 3 * DN_WIDTH:3 * SGU_WIDTH + 4 * DN_WIDTH]
    wgt = jnp.pad(w_in_full_t[3 * SGU_WIDTH + 4 * DN_WIDTH:], ((0, GATE_PAD - 2 * DN_HEADS), (0, 0)))
    conv_full = jnp.moveaxis(conv_blocks, 0, 1).reshape(CONV_K, 3 * DN_WIDTH)
    later_shards = [w_out[0].astype(BF16), ple_gate_w[0].astype(BF16), ple_proj_w[0].astype(BF16)]

    pad_row = lambda a: jnp.pad(a.reshape(1, -1), ((0, 0), (DN_HEADS, GATE_PAD - DN_HEADS - a.size)))
    alog, dtb = pad_row(dn_a_log), pad_row(dn_dt_bias)
    og = dn_o_norm_g.reshape(1, DN_HEAD_DIM)
    ws = sgu_w_s.reshape(SGU_GROUPS, SGU_CHUNK, SGU_CHUNK)
    b_t = sgu_b_s.reshape(SGU_GROUPS, SGU_CHUNK).T
    fin_g = final_norm_g.reshape(1, D_MODEL)

    x2 = x.reshape(t, D_MODEL)
    sgu_weights = (sgu_ln_g, sgu_ln_b, ws, b_t)
    a_uvz, b_qkv, b_z, b_l, a_out, conv_out, w_out_blocks, w_gate_blocks, w_proj_blocks = _inproj_fwd(
        x2, s, norm_g, wat, wqt, wzt, wgt, sgu_weights, conv_full, later_shards)
    w_out_full = w_out_blocks.reshape(D_MODEL, D_MODEL)
    w_gate_full = w_gate_blocks.reshape(D_MODEL, D_MODEL)
    w_proj_full = jnp.moveaxis(w_proj_blocks, 0, 1).reshape(PLE_DIM, D_MODEL)
    qkv3 = b_qkv.reshape(nb, s, 3 * DN_WIDTH)
    conv_out = conv_out.reshape(nb, s, 3 * DN_WIDTH)
    z3 = b_z.reshape(nb, s, DN_WIDTH)
    l3 = b_l.reshape(nb, s, GATE_PAD)
    b_out, states, inverses = _dn_fwd(conv_out, z3, l3, alog, dtb, og)

    d_a, d_b, dh1, g_w_out, g_gate, g_proj, g_ple_g, g_fin_g, loss_tile = _head(
        a_out, b_out.reshape(t, DN_WIDTH), x2, p.reshape(t, PLE_DIM), loss_target.reshape(t, D_MODEL),
        w_out_full, w_out_full.T, w_gate_full, w_gate_full.T, w_proj_full, ple_norm_g, fin_g)
    d_qkv, d_z, d_l, g_conv, g_alog, g_dtb, g_og, *head_received = _dn_bwd(
        qkv3, conv_out, z3, l3, conv_full, alog, dtb, og, states, inverses, d_b.reshape(nb, s, DN_WIDTH),
        [g_w_out, g_gate, g_proj])
    grad_x, g_w_in, g_norm, g_ln_g, g_ln_b, g_ws, g_bt = _inproj_bwd(
        x2, dh1, a_uvz, d_a, d_qkv.reshape(t, 3 * DN_WIDTH), d_z.reshape(t, DN_WIDTH), d_l.reshape(t, GATE_PAD),
        norm_g, sgu_weights, wat, wqt, wzt, wgt)

    small = _pack_rows([g_conv, g_norm, g_ln_g, g_ln_b, g_ws, g_bt.T, g_alog[:, DN_HEADS:2 * DN_HEADS], g_dtb[:, DN_HEADS:2 * DN_HEADS], g_og,
                        g_ple_g, g_fin_g, (0.5 / D_MODEL) * loss_tile[0:1, 0:1]], SMALL_ROWS)
    w_in_received, small_received = _reduce_exchange(g_w_in, small)

    results = {}
    outs = _reduce_adamw(w_in_received, w_in_t, m_in_t, v_in_t, "adamw_w_in", LANES)
    results["w_in"] = [jnp.transpose(a.reshape(IN_SHARD, 1, D_MODEL), (1, 2, 0)) for a in outs]
    for name, recv in zip(("w_out", "ple_gate_w", "ple_proj_w"), head_received):
        results[name] = _reduce_adamw(recv, weights[name], mom1[name], mom2[name], "adamw_" + name)
    names = [name for name, _ in REPLICATED]
    two_d = lambda a: a.reshape(1, -1) if a.ndim == 1 else a
    g_conv_sum, loss_sum, *flat_outs = _adamw_replicated(
        small_received, *[[two_d(src[k]) for k in names] for src in (weights, mom1, mom2)])
    for i, k in enumerate(names):
        results[k] = [a.reshape(weights[k].shape) for a in flat_outs[4 * i:4 * i + 4]]
    loss = loss_sum[0, 0]
    me = 4 * lax.axis_index("x") + 2 * lax.axis_index("y") + lax.axis_index("c")
    conv_mine = lax.dynamic_slice(g_conv_sum, (0, me * 192), (CONV_K, 192))
    results["dn_conv_w"] = _reduce_adamw(conv_mine[None], dn_conv_w, m_dn_conv_w, v_dn_conv_w, "adamw_dn_conv_w")

    return (loss, grad_x.reshape(nb, s, D_MODEL), *[results[k][0] for k in WEIGHT_ORDER],
            *[results[k][1] for k in WEIGHT_ORDER], *[results[k][2] for k in WEIGHT_ORDER],
            *[results[k][3] for k in WEIGHT_ORDER])
```

```python
import jax
import jax.numpy as jnp
from jax import lax
from jax.experimental import pallas as pl
from jax.experimental.pallas import tpu as pltpu

F32 = jnp.float32
BF16 = jnp.bfloat16

N_DEV = 8
D_MODEL = 1024
SGU_WIDTH = 512
SGU_GROUPS = 4
SGU_CHUNK = 128
DN_WIDTH = 512
DN_HEADS = 4
DN_HEAD_DIM = 128
DN_CHUNK = 128
CONV_K = 4
CONV_HALO = 8
PLE_DIM = 256
EPS = 1e-6
IN_COLS = 3592
IN_SHARD = IN_COLS // N_DEV
GATE_PAD = 128

ADAM_LR = 0.001
ADAM_B1 = 0.9
ADAM_B2 = 0.999
ADAM_EPS = 1e-08
ADAM_WD = 0.01
ADAM_STEP = 10

LANES = 128
VMEM_LIMIT = 56 * 1024 * 1024
MESH = pl.DeviceIdType.MESH

REPLICATED = (("norm_g", (1, D_MODEL)), ("sgu_ln_g", (1, SGU_WIDTH)), ("sgu_ln_b", (1, SGU_WIDTH)),
              ("sgu_w_s", (1, SGU_GROUPS, SGU_CHUNK, SGU_CHUNK)), ("sgu_b_s", (1, SGU_GROUPS, SGU_CHUNK)),
              ("dn_a_log", (1, DN_HEADS)), ("dn_dt_bias", (1, DN_HEADS)), ("dn_o_norm_g", (1, DN_HEAD_DIM)),
              ("ple_norm_g", (1, D_MODEL)), ("final_norm_g", (D_MODEL,)))
WEIGHT_ORDER = ("norm_g", "w_in", "sgu_ln_g", "sgu_ln_b", "sgu_w_s", "sgu_b_s", "dn_conv_w", "dn_a_log",
                "dn_dt_bias", "dn_o_norm_g", "w_out", "ple_norm_g", "ple_gate_w", "ple_proj_w", "final_norm_g")


def _size(shape):
    n = 1
    for s in shape:
        n *= s
    return n


SMALL_LAYOUT = (("conv", (CONV_K, 3 * DN_WIDTH)),) + REPLICATED + (("loss", (1,)),)
SMALL_PIECE_ROWS = tuple(-(-_size(s) // LANES) for _, s in SMALL_LAYOUT)
SMALL_ROWS = -(-sum(SMALL_PIECE_ROWS) // 8) * 8


def _bdot(a, b):
    return jnp.dot(a.astype(BF16), b.astype(BF16), preferred_element_type=F32)


def _sigmoid(x):
    return 0.5 * jnp.tanh(0.5 * x) + 0.5


@jax.custom_vjp
def _silu(x):
    return x * _sigmoid(x)


def _silu_fwd(x):
    s = _sigmoid(x)
    return x * s, (x, s)


def _silu_bwd(res, ct):
    x, s = res
    return (ct * (s * (1.0 + x * (1.0 - s))),)


_silu.defvjp(_silu_fwd, _silu_bwd)


def _normal_cdf(x):
    return 0.5 + 0.5 * lax.erf(x * (0.5 ** 0.5))


@jax.custom_vjp
def _gelu(x):
    return x * _normal_cdf(x)


def _gelu_fwd(x):
    cdf = _normal_cdf(x)
    return x * cdf, (x, cdf)


def _gelu_bwd(res, ct):
    x, cdf = res
    pdf = jnp.exp(-0.5 * x * x) * ((2.0 * jnp.pi) ** -0.5)
    return (ct * (cdf + x * pdf),)


_gelu.defvjp(_gelu_fwd, _gelu_bwd)


def _softplus(x):
    return jnp.maximum(x, 0.0) + jnp.log1p(jnp.exp(-jnp.abs(x)))


def _l2n(x):
    return x * lax.rsqrt(jnp.sum(x * x, axis=-1, keepdims=True) + EPS)


def _rms(x):
    r = lax.rsqrt(jnp.mean(x * x, axis=-1, keepdims=True) + EPS)
    return x * r, r


def _rms_bwd(dn, n, r):
    return r * (dn - n * jnp.mean(dn * n, axis=-1, keepdims=True))


def _onehot_row(idx, width):
    return (lax.broadcasted_iota(jnp.int32, (1, width), 1) == idx).astype(F32)


def _rowsum(x):
    return jnp.sum(x, axis=0, keepdims=True)


def _iota2(n):
    return lax.broadcasted_iota(jnp.int32, (n, n), 0), lax.broadcasted_iota(jnp.int32, (n, n), 1)


def _bmm(a, b):
    return lax.dot_general(a.astype(BF16), b.astype(BF16), (((2,), (1,)), ((0,), (0,))), preferred_element_type=F32)


def _bmm_nt(a, b):
    return lax.dot_general(a.astype(BF16), b.astype(BF16), (((2,), (2,)), ((0,), (0,))), preferred_element_type=F32)


def _bmm_tn(a, b):
    return lax.dot_general(a.astype(BF16), b.astype(BF16), (((1,), (1,)), ((0,), (0,))), preferred_element_type=F32)


def _tri_inv_impl(a):
    n = a.shape[-1]
    r, c = _iota2(n)
    x = r ^ c
    eye = (r == c).astype(F32)
    ad = jnp.where(x < 16, a, 0.0)
    p2 = _bmm(ad, ad)
    e = p2 - ad - _bmm(ad, p2)
    p4 = _bmm(p2, p2)
    e = e + p4 + _bmm(e, p4)
    p8 = _bmm(p4, p4)
    e = e + p8 + _bmm(e, p8)
    size = 16
    while size < n:
        m = jnp.where(jnp.logical_and(x < 2 * size, x >= size), a, 0.0)
        f = m + _bmm(m, e)
        e = e - f - _bmm(e, f)
        size *= 2
    return e + eye


@jax.custom_vjp
def _tri_inv(a, known):
    return _tri_inv_impl(a) if known is None else known


def _tri_inv_fwd(a, known):
    t = _tri_inv(a, known)
    return t, (t, known)


def _tri_inv_bwd(res, dt):
    t, known = res
    return -_bmm_tn(t, _bmm_nt(dt, t)), None if known is None else jnp.zeros_like(known)


_tri_inv.defvjp(_tri_inv_fwd, _tri_inv_bwd)


def _sgu_core(u, v, z, lg, lb, ws, bcol):
    n = ws.shape[0]
    r, c = _iota2(n)
    wm = jnp.where(r >= c, ws, 0.0)
    gu = _gelu(u)
    gv = _gelu(v)
    xc = gv - jnp.mean(gv, axis=-1, keepdims=True)
    ln = xc * lax.rsqrt(jnp.mean(xc * xc, axis=-1, keepdims=True) + EPS) * lg + lb
    s = _bdot(wm, ln) + bcol
    return gu * s * _silu(z)


def _lanes_of(x):
    return jnp.concatenate([x[i] for i in range(x.shape[0])], axis=1)


def _batch_of(x, width):
    return jnp.concatenate([x[None, :, i * width:(i + 1) * width] for i in range(x.shape[1] // width)], axis=0)


def _mask_dot(mask, x):
    hi = x.astype(BF16)
    lo = (x - hi.astype(F32)).astype(BF16)
    m = mask.astype(BF16)
    return jnp.dot(m, hi, preferred_element_type=F32) + jnp.dot(m, lo, preferred_element_type=F32)


def _tri_mask(n, upper):
    r, c = _iota2(n)
    return (r <= c) if upper else (r >= c)


@jax.custom_vjp
def _cumsum_rows(x):
    return _mask_dot(_tri_mask(x.shape[0], False), x)


def _cumsum_rows_fwd(x):
    return _cumsum_rows(x), None


def _cumsum_rows_bwd(_, ct):
    return (_mask_dot(_tri_mask(ct.shape[0], True), ct),)


_cumsum_rows.defvjp(_cumsum_rows_fwd, _cumsum_rows_bwd)


@jax.custom_vjp
def _colsum_all_rows(x):
    return _mask_dot(jnp.ones((x.shape[0], x.shape[0]), jnp.bool_), x)


def _colsum_all_rows_fwd(x):
    return _colsum_all_rows(x), None


def _colsum_all_rows_bwd(_, ct):
    return (_mask_dot(jnp.ones((ct.shape[0], ct.shape[0]), jnp.bool_), ct),)


_colsum_all_rows.defvjp(_colsum_all_rows_fwd, _colsum_all_rows_bwd)


def _dn_core(cq, ck, cv, z, logits, state, alog, dtb, og, t_known=None):
    gn, cn, dh = cq.shape
    heads = gn // logits.shape[0]
    q = _l2n(_silu(cq)) * (dh ** -0.5)
    k = _l2n(_silu(ck))
    v = _silu(cv)
    beta_lanes = _sigmoid(logits)
    g_lanes = -jnp.exp(alog) * _softplus(logits + dtb)
    column = lambda rows, lane: jnp.sum(rows * _onehot_row(lane, rows.shape[-1]), axis=-1, keepdims=True)[None]
    beta = jnp.concatenate([column(beta_lanes[i // heads], i % heads) for i in range(gn)], axis=0)
    g = jnp.concatenate([column(g_lanes[i // heads], heads + i % heads) for i in range(gn)], axis=0)
    r, c = _iota2(cn)
    tril = r >= c
    rw = lax.broadcasted_iota(jnp.int32, (cn, dh), 0)
    cw = lax.broadcasted_iota(jnp.int32, (cn, dh), 1)
    upper_wide = (rw <= cw).astype(F32)
    g_wide = jnp.broadcast_to(g, (gn, cn, dh))
    gc_wide = _batch_of(_cumsum_rows(_lanes_of(g_wide)), dh)
    gc_cols = _batch_of(_colsum_all_rows(_lanes_of(g_wide * upper_wide)), dh)[:, :, :cn]
    decay = jnp.exp(jnp.where(tril, gc_wide[:, :, :cn] - gc_cols, -1e30))
    kb = k * beta
    kk = _bmm_nt(kb, k) * decay
    t = _tri_inv(jnp.where(r > c, kk, 0.0), t_known)
    eg = jnp.exp(gc_wide)
    sol = _bmm(t, jnp.concatenate([v * beta, kb * eg], axis=-1))
    u_val, w_dec = sol[:, :, :dh], sol[:, :, dh:]
    qk = _bmm_nt(q, k) * decay
    g_last = jnp.sum(g_wide, axis=1, keepdims=True)
    k_dec = k * jnp.exp(g_last - gc_wide)
    ws = _bmm(jnp.concatenate([w_dec, q * eg], axis=1), state)
    v_new = u_val - ws[:, :cn]
    o = ws[:, cn:] + _bmm(qk, v_new)
    new_state = state * jnp.exp(g_last) + _bmm_tn(k_dec, v_new)
    on, _ = _rms(o)
    return on * og * _silu(z), new_state, t


N_CHIPS = 4
HBM_SPEC = pl.BlockSpec(memory_space=pl.ANY)


def _place():
    return lax.axis_index("x"), lax.axis_index("y"), lax.axis_index("c")


def _other_chip(k):
    x, y, _ = _place()
    px = 1 - x if k & 2 else x
    py = 1 - y if k & 1 else y
    return px, py, 2 * px + py


def _remote(src, dst, send_sem, recv_sem, device):
    return pltpu.make_async_remote_copy(src_ref=src, dst_ref=dst, send_sem=send_sem, recv_sem=recv_sem,
                                        device_id=device, device_id_type=MESH)


def _other_device(k):
    x, y, c = _place()
    px = 1 - x if k & 4 else x
    py = 1 - y if k & 2 else y
    pc = 1 - c if k & 1 else c
    return (px, py, pc), 4 * px + 2 * py + pc


def _direct_exchange(srcs, outs, send_sems, recv_sems, local_sems, gather):
    x, y, c = _place()
    me = 4 * x + 2 * y + c

    def copies(arriving):
        out_list = []
        for a, (src, out) in enumerate(zip(srcs, outs)):
            for k in range(1, N_DEV):
                peer, index = _other_device(k)
                mine = src if gather else src.at[index]
                out_list.append(_remote(mine, out.at[index if arriving else me], send_sems.at[a, k - 1],
                                        recv_sems.at[a, k - 1], peer))
        return out_list

    def local_copies():
        return [pltpu.make_async_copy(src if gather else src.at[me], out.at[me], local_sems.at[a])
                for a, (src, out) in enumerate(zip(srcs, outs))]

    def start():
        for cp in local_copies() + copies(False):
            cp.start()

    def wait():
        for cp in copies(True):
            cp.wait_recv()
        for cp in copies(False):
            cp.wait_send()
        for cp in local_copies():
            cp.wait()

    return start, wait


def _exchange_scratch(n):
    return [pltpu.SemaphoreType.DMA((n, N_DEV - 1)), pltpu.SemaphoreType.DMA((n, N_DEV - 1)), pltpu.SemaphoreType.DMA((n,))]


def _all_gather(shards):
    n = len(shards)

    def body(*refs):
        srcs, outs = refs[:n], refs[n:2 * n]
        send_sems, recv_sems, local_sems = refs[2 * n:]
        x, y, c = _place()
        me = 4 * x + 2 * y + c
        sibling = (x, y, 1 - c)
        local = [pltpu.make_async_copy(srcs[a], outs[a].at[me], local_sems.at[a]) for a in range(n)]
        for cp in local:
            cp.start()
        sends = []
        for a in range(n):
            sends.append(_remote(srcs[a], outs[a].at[me], send_sems.at[a, 0], recv_sems.at[a, 0], sibling))
        for k in range(1, N_CHIPS):
            px, py, _ = _other_chip(k)
            for a in range(n):
                sends.append(_remote(srcs[a], outs[a].at[me], send_sems.at[a, k], recv_sems.at[a, k], (px, py, c)))
        for cp in sends:
            cp.start()
        passed = []
        for k in range(1, N_CHIPS):
            px, py, _ = _other_chip(k)
            blk = 4 * px + 2 * py + c
            for a in range(n):
                _remote(srcs[a], outs[a].at[blk], send_sems.at[a, k], recv_sems.at[a, k], (px, py, c)).wait_recv()
            for a in range(n):
                cp = _remote(outs[a].at[blk], outs[a].at[blk], send_sems.at[a, 3 + k], recv_sems.at[a, 3 + k], sibling)
                cp.start()
                passed.append(cp)
        for a in range(n):
            _remote(srcs[a], outs[a].at[me + 1 - 2 * c], send_sems.at[a, 0], recv_sems.at[a, 0], sibling).wait_recv()
        for k in range(1, N_CHIPS):
            px, py, _ = _other_chip(k)
            blk = 4 * px + 2 * py + 1 - c
            for a in range(n):
                _remote(srcs[a], outs[a].at[blk], send_sems.at[a, 3 + k], recv_sems.at[a, 3 + k], sibling).wait_recv()
        for cp in sends + passed:
            cp.wait_send()
        for cp in local:
            cp.wait()

    return pl.pallas_call(
        body, name="all_gather_weights",
        out_shape=tuple(jax.ShapeDtypeStruct((N_DEV,) + a.shape, a.dtype) for a in shards),
        in_specs=[HBM_SPEC] * n, out_specs=(HBM_SPEC,) * n,
        scratch_shapes=[pltpu.SemaphoreType.DMA((n, N_DEV - 1)), pltpu.SemaphoreType.DMA((n, N_DEV - 1)),
                        pltpu.SemaphoreType.DMA((n,))],
    )(*shards)


def _adamw(g, w, m, v):
    m_new = ADAM_B1 * m + (1.0 - ADAM_B1) * g
    v_new = ADAM_B2 * v + (1.0 - ADAM_B2) * jnp.square(g)
    m_hat = m_new / (1.0 - ADAM_B1 ** ADAM_STEP)
    v_hat = v_new / (1.0 - ADAM_B2 ** ADAM_STEP)
    return -ADAM_LR * (m_hat / (jnp.sqrt(v_hat) + ADAM_EPS) + ADAM_WD * w), m_new, v_new


def _reduce_exchange(by_device, small, updates):
    _, rows, cols = by_device.shape
    nu = len(updates)

    def body(g_ref, small_ref, *refs):
        update_in, refs = refs[:4 * nu], refs[4 * nu:]
        out_ref, small_out_ref = refs[:2]
        update_out, refs = refs[2:2 + 4 * nu], refs[2 + 4 * nu:]
        from_sibling, small_from_sibling, stage, sums, small_own, small_sum = refs[:6]
        update_vmem, refs = refs[6:6 + 8 * nu], refs[6 + 8 * nu:]
        pair_send, pair_recv, chip_send, chip_recv, local_sems, update_sems = refs
        x, y, c = _place()
        mine = 2 * x + y
        sibling = (x, y, 1 - c)
        chips = [(x, y, mine)] + [_other_chip(k) for k in range(1, N_CHIPS)]
        to_sibling = [_remote(g_ref.at[2 * chips[k][2] + 1 - c], from_sibling.at[k], pair_send.at[k], pair_recv.at[k], sibling)
                      for k in range(N_CHIPS)]
        to_sibling.append(_remote(small_ref, small_from_sibling, pair_send.at[N_CHIPS], pair_recv.at[N_CHIPS], sibling))
        for cp in to_sibling:
            cp.start()
        small_mine = pltpu.make_async_copy(small_ref, small_own, local_sems.at[0])
        small_mine.start()
        to_chips = []
        for k in (1, 2, 3, 0):
            px, py, chip = chips[k]
            mine_k = pltpu.make_async_copy(g_ref.at[2 * chip + c], stage, local_sems.at[1])
            mine_k.start()
            to_sibling[k].wait_recv()
            mine_k.wait()
            sums[k] = (stage[...] + from_sibling[k]).astype(sums.dtype)
            if k:
                cp = _remote(sums.at[k], out_ref.at[mine], chip_send.at[0, k - 1], chip_recv.at[0, k - 1], (px, py, c))
                cp.start()
                to_chips.append(cp)
        own_block = pltpu.make_async_copy(sums.at[0], out_ref.at[mine], local_sems.at[2])
        own_block.start()
        to_sibling[N_CHIPS].wait_recv()
        small_mine.wait()
        small_sum[...] = small_own[...] + small_from_sibling[...]
        for k in range(1, N_CHIPS):
            px, py, _ = chips[k]
            cp = _remote(small_sum, small_out_ref.at[mine], chip_send.at[1, k - 1], chip_recv.at[1, k - 1], (px, py, c))
            cp.start()
            to_chips.append(cp)
        own_small = pltpu.make_async_copy(small_sum, small_out_ref.at[mine], local_sems.at[3])
        own_small.start()
        loads, stores = [], []
        for u in range(nu):
            srcs = (update_in[4 * u],) + tuple(r.at[0] for r in update_in[4 * u + 1:4 * u + 4])
            loads.append([pltpu.make_async_copy(src, update_vmem[8 * u + j], update_sems.at[u, j])
                          for j, src in enumerate(srcs)])
            for cp in loads[u]:
                cp.start()
        for u in range(nu):
            for cp in loads[u]:
                cp.wait()
            parts, w_v, m_v, v_v = update_vmem[8 * u:8 * u + 4]
            g = parts[0].astype(F32)
            for j in range(1, parts.shape[0]):
                g = g + parts[j].astype(F32)
            results = (g,) + _adamw(g, w_v[...], m_v[...], v_v[...])
            for j, value in enumerate(results):
                update_vmem[8 * u + 4 + j][...] = value
                cp = pltpu.make_async_copy(update_vmem[8 * u + 4 + j], update_out[4 * u + j].at[0], update_sems.at[u, 4 + j])
                cp.start()
                stores.append(cp)
        for k in range(1, N_CHIPS):
            px, py, chip = chips[k]
            _remote(sums.at[k], out_ref.at[chip], chip_send.at[0, k - 1], chip_recv.at[0, k - 1], (px, py, c)).wait_recv()
            _remote(small_sum, small_out_ref.at[chip], chip_send.at[1, k - 1], chip_recv.at[1, k - 1], (px, py, c)).wait_recv()
        for cp in to_sibling + to_chips:
            cp.wait_send()
        own_block.wait()
        own_small.wait()
        for cp in stores:
            cp.wait()

    update_scratch = []
    for parts, w, _, _ in updates:
        plane = pltpu.VMEM(w.shape[1:], F32)
        update_scratch += [pltpu.VMEM(parts.shape, parts.dtype)] + [plane] * 7
    return pl.pallas_call(
        body, name="grad_reduce_exchange",
        out_shape=(jax.ShapeDtypeStruct((N_CHIPS, rows, cols), BF16), jax.ShapeDtypeStruct((N_CHIPS,) + small.shape, F32))
        + tuple(jax.ShapeDtypeStruct(w.shape, F32) for _, w, _, _ in updates for _ in range(4)),
        in_specs=[HBM_SPEC] * (2 + 4 * nu), out_specs=(HBM_SPEC,) * (2 + 4 * nu),
        scratch_shapes=[pltpu.VMEM((N_CHIPS, rows, cols), F32), pltpu.VMEM(small.shape, F32), pltpu.VMEM((rows, cols), F32),
                        pltpu.VMEM((N_CHIPS, rows, cols), BF16), pltpu.VMEM(small.shape, F32), pltpu.VMEM(small.shape, F32)]
        + update_scratch
        + [pltpu.SemaphoreType.DMA((N_CHIPS + 1,)), pltpu.SemaphoreType.DMA((N_CHIPS + 1,)),
           pltpu.SemaphoreType.DMA((2, N_CHIPS - 1)), pltpu.SemaphoreType.DMA((2, N_CHIPS - 1)),
           pltpu.SemaphoreType.DMA((4,)), pltpu.SemaphoreType.DMA((max(nu, 1), 8))],
        compiler_params=pltpu.CompilerParams(vmem_limit_bytes=VMEM_LIMIT),
    )(by_device, small, *[a for entry in updates for a in entry])


def _params(n_axes):
    return pltpu.CompilerParams(dimension_semantics=("arbitrary",) * n_axes, vmem_limit_bytes=VMEM_LIMIT)


def _whole(shape):
    return pl.BlockSpec(shape, lambda *_: (0,) * len(shape))


VMEM_SPEC = pl.BlockSpec(memory_space=pltpu.VMEM)


def _inproj_fwd(x2, seq_len, norm_g, wat, wqt, wzt, wgt, sgu_weights, conv_w, later_shards):
    t = x2.shape[0]
    tm = min(512, seq_len)
    tiles_per_seq = seq_len // tm
    steps = t // tm
    ns = len(later_shards)

    def body(x_ref, g_ref, wa_ref, wq_ref, wz_ref, wg_ref, lg_ref, lb_ref, ws_ref, bt_ref, cw_ref, *rest):
        shard_refs, rest = rest[:ns], rest[ns:]
        a_ref, q_ref, z_ref, l_ref, sgu_ref, c_ref = rest[:6]
        gathered_refs, (xpad_ref, send_sems, recv_sems, local_sems) = rest[6:6 + ns], rest[6 + ns:]
        start_gather, wait_gather = _direct_exchange(shard_refs, gathered_refs, send_sems, recv_sems, local_sems, True)
        pl.when(pl.program_id(0) == 0)(start_gather)
        n, _ = _rms(x_ref[...])
        xn = (n * g_ref[...]).astype(BF16)
        for w_ref, o_ref in ((wa_ref, a_ref), (wq_ref, q_ref), (wz_ref, z_ref), (wg_ref, l_ref)):
            width = w_ref.shape[0]
            for c0 in range(0, width, 512):
                c1 = min(c0 + 512, width)
                o_ref[:, c0:c1] = lax.dot_general(xn, w_ref[c0:c1, :], (((1,), (1,)), ((), ())),
                                                  preferred_element_type=F32)
        for row0 in range(0, tm, SGU_CHUNK):
            for grp in range(SGU_GROUPS):
                args = _sgu_pieces(a_ref, lg_ref, lb_ref, ws_ref, bt_ref, row0, grp)
                sgu_ref[pl.ds(row0, SGU_CHUNK), pl.ds(grp * 128, 128)] = _sgu_core(*args).astype(sgu_ref.dtype)

        @pl.when(pl.program_id(0) % tiles_per_seq == 0)
        def _():
            xpad_ref[0:CONV_HALO, :] = jnp.zeros((CONV_HALO, xpad_ref.shape[1]), F32)

        xpad_ref[CONV_HALO:, :] = q_ref[...]
        acc = None
        for j in range(CONV_K):
            term = cw_ref[j:j + 1, :] * xpad_ref[pl.ds(CONV_HALO - CONV_K + 1 + j, tm), :]
            acc = term if acc is None else acc + term
        c_ref[...] = acc
        xpad_ref[0:CONV_HALO, :] = xpad_ref[tm:tm + CONV_HALO, :]
        pl.when(pl.program_id(0) == steps - 1)(wait_gather)

    widths = (wat.shape[0], wqt.shape[0], wzt.shape[0], wgt.shape[0])
    tile = lambda w: pl.BlockSpec((tm, w), lambda i: (i, 0))
    sgu_shapes = ((1, SGU_WIDTH), (1, SGU_WIDTH), (SGU_GROUPS, SGU_CHUNK, SGU_CHUNK), (SGU_CHUNK, SGU_GROUPS))
    return pl.pallas_call(
        body, name="inproj_sgu_conv_fwd", grid=(steps,),
        out_shape=tuple(jax.ShapeDtypeStruct((t, w), F32) for w in widths)
        + (jax.ShapeDtypeStruct((t, SGU_WIDTH), BF16), jax.ShapeDtypeStruct((t, widths[1]), F32))
        + tuple(jax.ShapeDtypeStruct((N_DEV,) + a.shape, a.dtype) for a in later_shards),
        in_specs=[tile(D_MODEL), _whole((1, D_MODEL)), VMEM_SPEC, VMEM_SPEC, VMEM_SPEC, VMEM_SPEC]
        + [_whole(s) for s in sgu_shapes] + [_whole((CONV_K, widths[1]))] + [HBM_SPEC] * ns,
        out_specs=tuple(tile(w) for w in widths) + (tile(SGU_WIDTH), tile(widths[1])) + (HBM_SPEC,) * ns,
        scratch_shapes=[pltpu.VMEM((CONV_HALO + tm, widths[1]), F32)] + _exchange_scratch(ns),
        compiler_params=_params(1),
    )(x2, norm_g, wat, wqt, wzt, wgt, *sgu_weights, conv_w, *later_shards)


def _sgu_pieces(uvz_ref, lg_ref, lb_ref, ws_ref, bt_ref, row0, grp):
    rows = pl.ds(row0, SGU_CHUNK)
    lanes = pl.ds(grp * 128, 128)
    u = uvz_ref[rows, pl.ds(grp * 128, 128)]
    v = uvz_ref[rows, pl.ds(SGU_WIDTH + grp * 128, 128)]
    z = uvz_ref[rows, pl.ds(2 * SGU_WIDTH + grp * 128, 128)]
    bcol = jnp.sum(bt_ref[...] * _onehot_row(grp, SGU_GROUPS), axis=-1, keepdims=True)
    return u, v, z, lg_ref[:, lanes], lb_ref[:, lanes], ws_ref[grp], bcol


def _sgu_bwd_tile(uvz_ref, do_ref, sgu_refs, duvz_ref, grad_refs):
    lg_ref, lb_ref, ws_ref, bt_ref = sgu_refs
    dlg_ref, dlb_ref, dws_ref, dbt_ref = grad_refs
    for row0 in range(0, uvz_ref.shape[0], SGU_CHUNK):
        rows = pl.ds(row0, SGU_CHUNK)
        for grp in range(SGU_GROUPS):
            lanes = pl.ds(grp * 128, 128)
            args = _sgu_pieces(uvz_ref, lg_ref, lb_ref, ws_ref, bt_ref, row0, grp)
            _, pull = jax.vjp(_sgu_core, *args)
            du, dv, dz, dlg, dlb, dws, dbcol = pull(do_ref[rows, lanes])
            duvz_ref[rows, pl.ds(grp * 128, 128)] = du.astype(duvz_ref.dtype)
            duvz_ref[rows, pl.ds(SGU_WIDTH + grp * 128, 128)] = dv.astype(duvz_ref.dtype)
            duvz_ref[rows, pl.ds(2 * SGU_WIDTH + grp * 128, 128)] = dz.astype(duvz_ref.dtype)
            dlg_ref[:, lanes] += dlg
            dlb_ref[:, lanes] += dlb
            dws_ref[grp] += dws
            dbt_ref[...] += dbcol * _onehot_row(grp, SGU_GROUPS)


def _dn_pairs(nb):
    return [(b, h) for b in range(nb) for h in range(DN_HEADS)]


def _dn_batch_args(c_ref, z_ref):
    pairs = _dn_pairs(c_ref.shape[0])
    pick = lambda ref, b, col: ref[b, :, pl.ds(col, DN_HEAD_DIM)]
    cq = jnp.stack([pick(c_ref, b, h * DN_HEAD_DIM) for b, h in pairs])
    ck = jnp.stack([pick(c_ref, b, DN_WIDTH + h * DN_HEAD_DIM) for b, h in pairs])
    cv = jnp.stack([pick(c_ref, b, 2 * DN_WIDTH + h * DN_HEAD_DIM) for b, h in pairs])
    z = jnp.stack([pick(z_ref, b, h * DN_HEAD_DIM) for b, h in pairs])
    return cq, ck, cv, z


def _dn_weight_specs():
    return [_whole((CONV_K, 3 * DN_WIDTH)), _whole((1, GATE_PAD)), _whole((1, GATE_PAD)), _whole((1, DN_HEAD_DIM))]


def _dn_fwd(conv_out, zg, logits, alog, dtb, og):
    nb, s, _ = conv_out.shape
    nc = s // DN_CHUNK
    pairs = _dn_pairs(nb)
    gn = len(pairs)
    chunk = lambda w: pl.BlockSpec((nb, DN_CHUNK, w), lambda n: (0, n, 0))

    def body(c_ref, z_ref, l_ref, alog_ref, dtb_ref, og_ref, out_ref, st_ref, inv_ref, state_ref):
        n = pl.program_id(0)

        @pl.when(n == 0)
        def _():
            state_ref[...] = jnp.zeros_like(state_ref)

        cq, ck, cv, z = _dn_batch_args(c_ref, z_ref)
        state = state_ref[...]
        st_ref[...] = state
        out, new_state, t = _dn_core(cq, ck, cv, z, l_ref[...], state, alog_ref[...], dtb_ref[...], og_ref[...])
        state_ref[...] = new_state
        inv_ref[...] = t.astype(inv_ref.dtype)
        for i, (b, h) in enumerate(pairs):
            out_ref[b, :, pl.ds(h * DN_HEAD_DIM, DN_HEAD_DIM)] = out[i].astype(out_ref.dtype)

    per_chunk = pl.BlockSpec((None, gn, DN_HEAD_DIM, DN_HEAD_DIM), lambda n: (n, 0, 0, 0))
    return pl.pallas_call(
        body, name="deltanet_fwd", grid=(nc,),
        out_shape=(jax.ShapeDtypeStruct((nb, s, DN_WIDTH), BF16),
                   jax.ShapeDtypeStruct((nc, gn, DN_HEAD_DIM, DN_HEAD_DIM), F32),
                   jax.ShapeDtypeStruct((nc, gn, DN_CHUNK, DN_CHUNK), BF16)),
        in_specs=[chunk(3 * DN_WIDTH), chunk(DN_WIDTH), chunk(GATE_PAD)] + _dn_weight_specs()[1:],
        out_specs=(chunk(DN_WIDTH), per_chunk, pl.BlockSpec((None, gn, DN_CHUNK, DN_CHUNK), lambda n: (n, 0, 0, 0))),
        scratch_shapes=[pltpu.VMEM((gn, DN_HEAD_DIM, DN_HEAD_DIM), F32)],
        compiler_params=_params(1),
    )(conv_out, zg, logits, alog, dtb, og)


def _dn_bwd(qkv, conv_out, zg, logits, conv_w, alog, dtb, og, states, inverses, d_out, head_grads):
    nb, s, _ = qkv.shape
    nc = s // DN_CHUNK
    rev = lambda n: nc - 1 - n
    pairs = _dn_pairs(nb)
    gn = len(pairs)
    ng = len(head_grads)

    def body(cur_ref, c_ref, z_ref, l_ref, w_ref, alog_ref, dtb_ref, og_ref, st_ref, inv_ref, do_ref, *rest):
        grad_refs, rest = rest[:ng], rest[ng:]
        dqkv_ref, dz_ref, dl_ref, dw_ref, dalog_ref, ddtb_ref, dog_ref = rest[:7]
        recv_refs, (dstate_ref, dcpad_ref, send_sems, recv_sems, local_sems) = rest[7:7 + ng], rest[7 + ng:]
        n = pl.program_id(0)
        start_exchange, wait_exchange = _direct_exchange(grad_refs, recv_refs, send_sems, recv_sems, local_sems, False)
        pl.when(n == 0)(start_exchange)

        @pl.when(n == 0)
        def _():
            dw_ref[...] = jnp.zeros_like(dw_ref)
            dalog_ref[...] = jnp.zeros_like(dalog_ref)
            ddtb_ref[...] = jnp.zeros_like(ddtb_ref)
            dog_ref[...] = jnp.zeros_like(dog_ref)
            dstate_ref[...] = jnp.zeros_like(dstate_ref)
            dcpad_ref[:, DN_CHUNK:, :] = jnp.zeros((nb, CONV_HALO, 3 * DN_WIDTH), F32)

        cq, ck, cv, z = _dn_batch_args(c_ref, z_ref)
        d_out_g = jnp.stack([do_ref[b, :, pl.ds(h * DN_HEAD_DIM, DN_HEAD_DIM)] for b, h in pairs])
        t_known = inv_ref[...].astype(F32)
        core = lambda *args: _dn_core(*args, t_known=t_known)[:2]
        _, pull = jax.vjp(core, cq, ck, cv, z, l_ref[...], st_ref[...], alog_ref[...], dtb_ref[...], og_ref[...])
        dcq, dck, dcv, dz, dlog, dstate, dalog, ddtb, dog = pull((d_out_g, dstate_ref[...]))
        dstate_ref[...] = dstate
        dl_ref[...] = dlog.astype(dl_ref.dtype)
        dalog_ref[...] += dalog
        ddtb_ref[...] += ddtb
        dog_ref[...] += dog
        for i, (b, h) in enumerate(pairs):
            dcpad_ref[b, 0:DN_CHUNK, pl.ds(h * DN_HEAD_DIM, DN_HEAD_DIM)] = dcq[i]
            dcpad_ref[b, 0:DN_CHUNK, pl.ds(DN_WIDTH + h * DN_HEAD_DIM, DN_HEAD_DIM)] = dck[i]
            dcpad_ref[b, 0:DN_CHUNK, pl.ds(2 * DN_WIDTH + h * DN_HEAD_DIM, DN_HEAD_DIM)] = dcv[i]
            dz_ref[b, :, pl.ds(h * DN_HEAD_DIM, DN_HEAD_DIM)] = dz[i].astype(dz_ref.dtype)
        for b in range(nb):
            xb = cur_ref[b]
            dx = None
            for j in range(CONV_K):
                shifted = dcpad_ref[b, pl.ds(CONV_K - 1 - j, DN_CHUNK), :]
                term = w_ref[j:j + 1, :] * shifted
                dx = term if dx is None else dx + term
                dw_ref[j:j + 1, :] += _rowsum(shifted * xb)
            dqkv_ref[b] = dx.astype(dqkv_ref.dtype)
            dcpad_ref[b, DN_CHUNK:, :] = dcpad_ref[b, 0:CONV_HALO, :]
        pl.when(n == nc - 1)(wait_exchange)

    chunk = lambda w: pl.BlockSpec((nb, DN_CHUNK, w), lambda n: (0, rev(n), 0))
    return pl.pallas_call(
        body, name="deltanet_bwd", grid=(nc,),
        out_shape=(jax.ShapeDtypeStruct((nb, s, 3 * DN_WIDTH), BF16), jax.ShapeDtypeStruct((nb, s, DN_WIDTH), BF16),
                   jax.ShapeDtypeStruct((nb, s, GATE_PAD), BF16), jax.ShapeDtypeStruct((CONV_K, 3 * DN_WIDTH), F32),
                   jax.ShapeDtypeStruct((1, GATE_PAD), F32), jax.ShapeDtypeStruct((1, GATE_PAD), F32),
                   jax.ShapeDtypeStruct((1, DN_HEAD_DIM), F32))
        + tuple(jax.ShapeDtypeStruct(a.shape, a.dtype) for a in head_grads),
        in_specs=[chunk(3 * DN_WIDTH), chunk(3 * DN_WIDTH), chunk(DN_WIDTH), chunk(GATE_PAD)] + _dn_weight_specs() + [
            pl.BlockSpec((None, gn, DN_HEAD_DIM, DN_HEAD_DIM), lambda n: (rev(n), 0, 0, 0)),
            pl.BlockSpec((None, gn, DN_CHUNK, DN_CHUNK), lambda n: (rev(n), 0, 0, 0)),
            chunk(DN_WIDTH)] + [HBM_SPEC] * ng,
        out_specs=(chunk(3 * DN_WIDTH), chunk(DN_WIDTH), chunk(GATE_PAD), _whole((CONV_K, 3 * DN_WIDTH)),
                   _whole((1, GATE_PAD)), _whole((1, GATE_PAD)), _whole((1, DN_HEAD_DIM))) + (HBM_SPEC,) * ng,
        scratch_shapes=[pltpu.VMEM((gn, DN_HEAD_DIM, DN_HEAD_DIM), F32),
                        pltpu.VMEM((nb, DN_CHUNK + CONV_HALO, 3 * DN_WIDTH), F32)] + _exchange_scratch(ng),
        compiler_params=_params(1),
    )(qkv, conv_out, zg, logits, conv_w, alog, dtb, og, states, inverses, d_out, *head_grads)


def _head(a_out, b_out, x2, p2, target, w_out, w_out_t, w_gate, w_gate_t, w_proj, ple_g, fin_g):
    t = x2.shape[0]
    tm = min(512, t)
    steps = t // tm

    def body(a_ref, b_ref, x_ref, p_ref, y_ref, wo_ref, wot_ref, wg_ref, wgt_ref, wp_ref, pg_ref, fg_ref,
             da_ref, db_ref, dh_ref, dwo_hbm, dwg_hbm, dwp_hbm, dpg_ref, dfg_ref, loss_ref,
             dwo_acc, dwg_acc, dwp_acc, rows_stage, cols_stage):
        i = pl.program_id(0)

        @pl.when(i == 0)
        def _():
            dwo_acc[...] = jnp.zeros_like(dwo_acc)
            dwg_acc[...] = jnp.zeros_like(dwg_acc)
            dwp_acc[...] = jnp.zeros_like(dwp_acc)
            dpg_ref[...] = jnp.zeros_like(dpg_ref)
            dfg_ref[...] = jnp.zeros_like(dfg_ref)
            loss_ref[...] = jnp.zeros_like(loss_ref)

        a = a_ref[...]
        bb = b_ref[...]
        pb = p_ref[...].astype(BF16)
        pg = pg_ref[...]
        fg = fg_ref[...]
        h1 = (x_ref[...] + jnp.dot(a, wo_ref[0:SGU_WIDTH, :], preferred_element_type=F32)
              + jnp.dot(bb, wo_ref[SGU_WIDTH:, :], preferred_element_type=F32))
        n1, r1 = _rms(h1)
        rn = (n1 * pg).astype(BF16)
        gate = _sigmoid(jnp.dot(rn, wg_ref[...], preferred_element_type=F32))
        pp = jnp.dot(pb, wp_ref[...], preferred_element_type=F32)
        h2 = h1 + gate * pp
        n2, r2 = _rms(h2)
        err = n2 * fg - y_ref[...]
        loss_ref[...] += jnp.broadcast_to(_rowsum(jnp.sum(err * err, axis=-1, keepdims=True)), loss_ref.shape)

        dy = err * (1.0 / D_MODEL)
        dfg_ref[...] += _rowsum(dy * n2)
        dh2 = _rms_bwd(dy * fg, n2, r2)
        dpp = (dh2 * gate).astype(BF16)
        dgl = (dh2 * pp * gate * (1.0 - gate)).astype(BF16)
        dwp_acc[...] += lax.dot_general(pb, dpp, (((0,), (0,)), ((), ())), preferred_element_type=F32)
        dwg_acc[...] += lax.dot_general(rn, dgl, (((0,), (0,)), ((), ())), preferred_element_type=F32)
        drn = jnp.dot(dgl, wgt_ref[...], preferred_element_type=F32)
        dpg_ref[...] += _rowsum(drn * n1)
        dh1 = dh2 + _rms_bwd(drn * pg, n1, r1)
        dh_ref[...] = dh1
        dhb = dh1.astype(BF16)
        da_ref[...] = jnp.dot(dhb, wot_ref[:, 0:SGU_WIDTH], preferred_element_type=F32)
        db_ref[...] = jnp.dot(dhb, wot_ref[:, SGU_WIDTH:], preferred_element_type=F32)
        dwo_acc[0:SGU_WIDTH, :] += lax.dot_general(a, dhb, (((0,), (0,)), ((), ())), preferred_element_type=F32)
        dwo_acc[SGU_WIDTH:, :] += lax.dot_general(bb, dhb, (((0,), (0,)), ((), ())), preferred_element_type=F32)

        @pl.when(i == steps - 1)
        def _():
            for j in range(N_DEV):
                for acc, hbm in ((dwo_acc, dwo_hbm), (dwg_acc, dwg_hbm)):
                    rows_stage[...] = acc[j * LANES:(j + 1) * LANES, :].astype(BF16)
                    pltpu.sync_copy(rows_stage, hbm.at[j])
                cols_stage[...] = dwp_acc[:, j * LANES:(j + 1) * LANES].astype(BF16)
                pltpu.sync_copy(cols_stage, dwp_hbm.at[j])

    tile = lambda w: pl.BlockSpec((tm, w), lambda i: (i, 0))
    return pl.pallas_call(
        body, name="head_fwd_bwd", grid=(steps,),
        out_shape=(jax.ShapeDtypeStruct((t, SGU_WIDTH), F32), jax.ShapeDtypeStruct((t, DN_WIDTH), F32),
                   jax.ShapeDtypeStruct((t, D_MODEL), F32), jax.ShapeDtypeStruct((N_DEV, LANES, D_MODEL), BF16),
                   jax.ShapeDtypeStruct((N_DEV, LANES, D_MODEL), BF16), jax.ShapeDtypeStruct((N_DEV, PLE_DIM, LANES), BF16),
                   jax.ShapeDtypeStruct((1, D_MODEL), F32), jax.ShapeDtypeStruct((1, D_MODEL), F32),
                   jax.ShapeDtypeStruct((8, LANES), F32)),
        in_specs=[tile(SGU_WIDTH), tile(DN_WIDTH), tile(D_MODEL), tile(PLE_DIM), tile(D_MODEL),
                  VMEM_SPEC, VMEM_SPEC, VMEM_SPEC, VMEM_SPEC, VMEM_SPEC, _whole((1, D_MODEL)), _whole((1, D_MODEL))],
        out_specs=(tile(SGU_WIDTH), tile(DN_WIDTH), tile(D_MODEL), HBM_SPEC, HBM_SPEC, HBM_SPEC,
                   _whole((1, D_MODEL)), _whole((1, D_MODEL)), _whole((8, LANES))),
        scratch_shapes=[pltpu.VMEM((D_MODEL, D_MODEL), F32), pltpu.VMEM((D_MODEL, D_MODEL), F32),
                        pltpu.VMEM((PLE_DIM, D_MODEL), F32), pltpu.VMEM((LANES, D_MODEL), BF16),
                        pltpu.VMEM((PLE_DIM, LANES), BF16)],
        compiler_params=_params(1),
    )(a_out, b_out, x2, p2, target, w_out, w_out_t, w_gate, w_gate_t, w_proj, ple_g, fin_g)


def _inproj_bwd(x2, dh1, a_uvz, d_sgu, d_q, d_z, d_l, norm_g, sgu_weights, wat, wqt, wzt, wgt):
    t = x2.shape[0]
    tm = min(256, t)
    steps = t // tm

    widths = (a_uvz.shape[1], d_q.shape[1], d_z.shape[1], d_l.shape[1])
    starts = (0, widths[0], widths[0] + widths[1], widths[0] + widths[1] + widths[2])

    def body(x_ref, dh_ref, uvz_ref, dsgu_ref, dq_ref, dz_ref, dl_ref, g_ref, lg_ref, lb_ref, ws_ref, bt_ref,
             wat_ref, wqt_ref, wzt_ref, wgt_ref,
             dx_ref, dw_hbm, dg_ref, dlg_ref, dlb_ref, dws_ref, dbt_ref, dw_acc, stage_ref, da_ref):
        i = pl.program_id(0)

        @pl.when(i == 0)
        def _():
            dw_acc[...] = jnp.zeros_like(dw_acc)
            for ref in (dg_ref, dlg_ref, dlb_ref, dws_ref, dbt_ref):
                ref[...] = jnp.zeros_like(ref)

        _sgu_bwd_tile(uvz_ref, dsgu_ref, (lg_ref, lb_ref, ws_ref, bt_ref), da_ref, (dlg_ref, dlb_ref, dws_ref, dbt_ref))
        g = g_ref[...]
        n, r = _rms(x_ref[...])
        xn = (n * g).astype(BF16)
        dxn = None
        for d_ref, wt_ref, col0 in zip((da_ref, dq_ref, dz_ref, dl_ref), (wat_ref, wqt_ref, wzt_ref, wgt_ref), starts):
            term = jnp.dot(d_ref[...], wt_ref[...], preferred_element_type=F32)
            dxn = term if dxn is None else dxn + term
            width = d_ref.shape[1]
            for c0 in range(0, width, 512):
                c1 = min(c0 + 512, width)
                dw_acc[col0 + c0:col0 + c1, :] += lax.dot_general(d_ref[:, c0:c1], xn, (((0,), (0,)), ((), ())),
                                                                  preferred_element_type=F32)
        dg_ref[...] += _rowsum(dxn * n)
        dx_ref[...] = dh_ref[...] + _rms_bwd(dxn * g, n, r)

        @pl.when(i == steps - 1)
        def _():
            for j in range(N_DEV):
                stage_ref[...] = dw_acc[j * IN_SHARD:(j + 1) * IN_SHARD, :]
                pltpu.sync_copy(stage_ref, dw_hbm.at[j])

    tile = lambda w: pl.BlockSpec((tm, w), lambda i: (i, 0))
    sgu_shapes = ((1, SGU_WIDTH), (1, SGU_WIDTH), (SGU_GROUPS, SGU_CHUNK, SGU_CHUNK), (SGU_CHUNK, SGU_GROUPS))
    return pl.pallas_call(
        body, name="inproj_sgu_bwd", grid=(steps,),
        out_shape=(jax.ShapeDtypeStruct((t, D_MODEL), F32), jax.ShapeDtypeStruct((N_DEV, IN_SHARD, D_MODEL), F32),
                   jax.ShapeDtypeStruct((1, D_MODEL), F32)) + tuple(jax.ShapeDtypeStruct(s, F32) for s in sgu_shapes),
        in_specs=[tile(D_MODEL), tile(D_MODEL), tile(widths[0]), tile(SGU_WIDTH)] + [tile(w) for w in widths[1:]]
        + [_whole((1, D_MODEL))] + [_whole(s) for s in sgu_shapes] + [VMEM_SPEC] * 4,
        out_specs=(tile(D_MODEL), HBM_SPEC, _whole((1, D_MODEL))) + tuple(_whole(s) for s in sgu_shapes),
        scratch_shapes=[pltpu.VMEM((sum(widths), D_MODEL), F32), pltpu.VMEM((IN_SHARD, D_MODEL), F32),
                        pltpu.VMEM((tm, widths[0]), BF16)],
        compiler_params=_params(1),
    )(x2, dh1, a_uvz, d_sgu, d_q, d_z, d_l, norm_g, *sgu_weights, wat, wqt, wzt, wgt)


def _reduce_adamw(recv, w, m, v, name, col_block=None):
    n, rows, cols = recv.shape
    cb = col_block or cols
    lead = w.ndim - 2

    def body(r_ref, w_ref, m_ref, v_ref, g_ref, d_ref, nm_ref, nv_ref):
        g = r_ref[0].astype(F32)
        for i in range(1, n):
            g = g + r_ref[i].astype(F32)
        g_ref[...] = g
        d_ref[...], nm_ref[...], nv_ref[...] = _adamw(g, w_ref[...], m_ref[...], v_ref[...])

    blk = pl.BlockSpec((None,) * lead + (rows, cb), lambda i: (0,) * lead + (0, i))
    return pl.pallas_call(
        body, name=name, grid=(cols // cb,),
        out_shape=tuple(jax.ShapeDtypeStruct(w.shape, F32) for _ in range(4)),
        in_specs=[pl.BlockSpec((n, rows, cb), lambda i: (0, 0, i)), blk, blk, blk],
        out_specs=(blk, blk, blk, blk),
        compiler_params=_params(1),
    )(recv, w, m, v)


def _adamw_replicated(received, ws, ms, vs):
    nw = len(ws)
    starts = [sum(SMALL_PIECE_ROWS[:i]) for i in range(len(SMALL_PIECE_ROWS))]

    def natural(g_ref, row0, shape):
        cols, rows = shape[-1], _size(shape[:-1])
        if cols == LANES:
            return g_ref[row0:row0 + rows, :].reshape(shape)
        if cols < LANES:
            return g_ref[row0:row0 + 1, 0:cols].reshape(shape)
        per = cols // LANES
        return jnp.concatenate(
            [jnp.concatenate([g_ref[row0 + r * per + k:row0 + r * per + k + 1, :] for k in range(per)], axis=1)
             for r in range(rows)], axis=0).reshape(shape)

    def body(r_ref, *refs):
        w_refs, m_refs, v_refs = refs[:nw], refs[nw:2 * nw], refs[2 * nw:3 * nw]
        conv_ref, loss_ref = refs[3 * nw], refs[3 * nw + 1]
        out_refs, g_ref = refs[3 * nw + 2:-1], refs[-1]
        g = r_ref[0]
        for q in range(1, N_CHIPS):
            g = g + r_ref[q]
        g_ref[...] = g
        conv_ref[...] = natural(g_ref, starts[0], (CONV_K, 3 * DN_WIDTH))
        loss_ref[...] = natural(g_ref, starts[-1], (1, 1))
        for i in range(nw):
            gi = natural(g_ref, starts[1 + i], w_refs[i].shape)
            out_refs[4 * i][...] = gi
            out_refs[4 * i + 1][...], out_refs[4 * i + 2][...], out_refs[4 * i + 3][...] = _adamw(
                gi, w_refs[i][...], m_refs[i][...], v_refs[i][...])

    def spec(a):
        lead = max(a.ndim - 3, 0)
        return pl.BlockSpec((None,) * lead + a.shape[lead:], lambda: (0,) * a.ndim)

    weight_specs = [spec(a) for a in ws]
    return pl.pallas_call(
        body, name="adamw_replicated",
        out_shape=(jax.ShapeDtypeStruct((CONV_K, 3 * DN_WIDTH), F32), jax.ShapeDtypeStruct((1, 1), F32))
        + tuple(jax.ShapeDtypeStruct(a.shape, F32) for a in ws for _ in range(4)),
        in_specs=[pl.BlockSpec(received.shape, lambda: (0, 0, 0))] + weight_specs * 3,
        out_specs=(pl.BlockSpec((CONV_K, 3 * DN_WIDTH), lambda: (0, 0)), pl.BlockSpec((1, 1), lambda: (0, 0)))
        + tuple(s for s in weight_specs for _ in range(4)),
        scratch_shapes=[pltpu.VMEM(received.shape[1:], F32)],
        compiler_params=pltpu.CompilerParams(vmem_limit_bytes=VMEM_LIMIT),
    )(received, *ws, *ms, *vs)


def _pack_rows(pieces, rows):
    padded = [jnp.pad(jnp.ravel(p), (0, -p.size % LANES)) for p in pieces]
    flat = jnp.concatenate(padded)
    return jnp.pad(flat, (0, rows * LANES - flat.shape[0])).reshape(rows, LANES)


def kernel(x, p, norm_g, w_in, sgu_ln_g, sgu_ln_b, sgu_w_s, sgu_b_s, dn_conv_w, dn_a_log, dn_dt_bias, dn_o_norm_g, w_out, ple_norm_g, ple_gate_w, ple_proj_w, final_norm_g, loss_target, m_norm_g, m_w_in, m_sgu_ln_g, m_sgu_ln_b, m_sgu_w_s, m_sgu_b_s, m_dn_conv_w, m_dn_a_log, m_dn_dt_bias, m_dn_o_norm_g, m_w_out, m_ple_norm_g, m_ple_gate_w, m_ple_proj_w, m_final_norm_g, v_norm_g, v_w_in, v_sgu_ln_g, v_sgu_ln_b, v_sgu_w_s, v_sgu_b_s, v_dn_conv_w, v_dn_a_log, v_dn_dt_bias, v_dn_o_norm_g, v_w_out, v_ple_norm_g, v_ple_gate_w, v_ple_proj_w, v_final_norm_g):
    weights = dict(norm_g=norm_g, w_in=w_in, sgu_ln_g=sgu_ln_g, sgu_ln_b=sgu_ln_b, sgu_w_s=sgu_w_s, sgu_b_s=sgu_b_s,
                   dn_conv_w=dn_conv_w, dn_a_log=dn_a_log, dn_dt_bias=dn_dt_bias, dn_o_norm_g=dn_o_norm_g, w_out=w_out,
                   ple_norm_g=ple_norm_g, ple_gate_w=ple_gate_w, ple_proj_w=ple_proj_w, final_norm_g=final_norm_g)
    mom1 = dict(norm_g=m_norm_g, w_in=m_w_in, sgu_ln_g=m_sgu_ln_g, sgu_ln_b=m_sgu_ln_b, sgu_w_s=m_sgu_w_s,
                sgu_b_s=m_sgu_b_s, dn_conv_w=m_dn_conv_w, dn_a_log=m_dn_a_log, dn_dt_bias=m_dn_dt_bias,
                dn_o_norm_g=m_dn_o_norm_g, w_out=m_w_out, ple_norm_g=m_ple_norm_g, ple_gate_w=m_ple_gate_w,
                ple_proj_w=m_ple_proj_w, final_norm_g=m_final_norm_g)
    mom2 = dict(norm_g=v_norm_g, w_in=v_w_in, sgu_ln_g=v_sgu_ln_g, sgu_ln_b=v_sgu_ln_b, sgu_w_s=v_sgu_w_s,
                sgu_b_s=v_sgu_b_s, dn_conv_w=v_dn_conv_w, dn_a_log=v_dn_a_log, dn_dt_bias=v_dn_dt_bias,
                dn_o_norm_g=v_dn_o_norm_g, w_out=v_w_out, ple_norm_g=v_ple_norm_g, ple_gate_w=v_ple_gate_w,
                ple_proj_w=v_ple_proj_w, final_norm_g=v_final_norm_g)
    nb, s, _ = x.shape
    t = nb * s

    transposed = lambda a: jnp.transpose(a, (2, 0, 1)).reshape(IN_SHARD, D_MODEL)
    w_in_t, m_in_t, v_in_t = transposed(w_in), transposed(m_w_in), transposed(v_w_in)
    w_in_blocks, conv_blocks = _all_gather([w_in_t.astype(BF16), dn_conv_w[0]])
    w_in_full_t = w_in_blocks.reshape(IN_COLS, D_MODEL)
    wat = w_in_full_t[:3 * SGU_WIDTH]
    wqt = w_in_full_t[3 * SGU_WIDTH:3 * SGU_WIDTH + 3 * DN_WIDTH]
    wzt = w_in_full_t[3 * SGU_WIDTH + 3 * DN_WIDTH:3 * SGU_WIDTH + 4 * DN_WIDTH]
    wgt = jnp.pad(w_in_full_t[3 * SGU_WIDTH + 4 * DN_WIDTH:], ((0, GATE_PAD - 2 * DN_HEADS), (0, 0)))
    conv_full = jnp.moveaxis(conv_blocks, 0, 1).reshape(CONV_K, 3 * DN_WIDTH)
    later_shards = [w_out[0].astype(BF16), ple_gate_w[0].astype(BF16), ple_proj_w[0].astype(BF16)]

    pad_row = lambda a: jnp.pad(a.reshape(1, -1), ((0, 0), (DN_HEADS, GATE_PAD - DN_HEADS - a.size)))
    alog, dtb = pad_row(dn_a_log), pad_row(dn_dt_bias)
    og = dn_o_norm_g.reshape(1, DN_HEAD_DIM)
    ws = sgu_w_s.reshape(SGU_GROUPS, SGU_CHUNK, SGU_CHUNK)
    b_t = sgu_b_s.reshape(SGU_GROUPS, SGU_CHUNK).T
    fin_g = final_norm_g.reshape(1, D_MODEL)

    x2 = x.reshape(t, D_MODEL)
    sgu_weights = (sgu_ln_g, sgu_ln_b, ws, b_t)
    a_uvz, b_qkv, b_z, b_l, a_out, conv_out, w_out_blocks, w_gate_blocks, w_proj_blocks = _inproj_fwd(
        x2, s, norm_g, wat, wqt, wzt, wgt, sgu_weights, conv_full, later_shards)
    w_out_full = w_out_blocks.reshape(D_MODEL, D_MODEL)
    w_gate_full = w_gate_blocks.reshape(D_MODEL, D_MODEL)
    w_proj_full = jnp.moveaxis(w_proj_blocks, 0, 1).reshape(PLE_DIM, D_MODEL)
    qkv3 = b_qkv.reshape(nb, s, 3 * DN_WIDTH)
    conv_out = conv_out.reshape(nb, s, 3 * DN_WIDTH)
    z3 = b_z.reshape(nb, s, DN_WIDTH)
    l3 = b_l.reshape(nb, s, GATE_PAD)
    b_out, states, inverses = _dn_fwd(conv_out, z3, l3, alog, dtb, og)

    d_a, d_b, dh1, g_w_out, g_gate, g_proj, g_ple_g, g_fin_g, loss_tile = _head(
        a_out, b_out.reshape(t, DN_WIDTH), x2, p.reshape(t, PLE_DIM), loss_target.reshape(t, D_MODEL),
        w_out_full, w_out_full.T, w_gate_full, w_gate_full.T, w_proj_full, ple_norm_g, fin_g)
    d_qkv, d_z, d_l, g_conv, g_alog, g_dtb, g_og, *head_received = _dn_bwd(
        qkv3, conv_out, z3, l3, conv_full, alog, dtb, og, states, inverses, d_b.reshape(nb, s, DN_WIDTH),
        [g_w_out, g_gate, g_proj])
    grad_x, g_w_in, g_norm, g_ln_g, g_ln_b, g_ws, g_bt = _inproj_bwd(
        x2, dh1, a_uvz, d_a, d_qkv.reshape(t, 3 * DN_WIDTH), d_z.reshape(t, DN_WIDTH), d_l.reshape(t, GATE_PAD),
        norm_g, sgu_weights, wat, wqt, wzt, wgt)

    small = _pack_rows([g_conv, g_norm, g_ln_g, g_ln_b, g_ws, g_bt.T, g_alog[:, DN_HEADS:2 * DN_HEADS], g_dtb[:, DN_HEADS:2 * DN_HEADS], g_og,
                        g_ple_g, g_fin_g, (0.5 / D_MODEL) * loss_tile[0:1, 0:1]], SMALL_ROWS)
    head_names = ("w_out", "ple_gate_w", "ple_proj_w")
    w_in_received, small_received, *head_updates = _reduce_exchange(
        g_w_in, small, [(recv, weights[k], mom1[k], mom2[k]) for k, recv in zip(head_names, head_received)])

    results = {k: head_updates[4 * i:4 * i + 4] for i, k in enumerate(head_names)}
    outs = _reduce_adamw(w_in_received, w_in_t, m_in_t, v_in_t, "adamw_w_in", LANES)
    results["w_in"] = [jnp.transpose(a.reshape(IN_SHARD, 1, D_MODEL), (1, 2, 0)) for a in outs]
    names = [name for name, _ in REPLICATED]
    two_d = lambda a: a.reshape(1, -1) if a.ndim == 1 else a
    g_conv_sum, loss_sum, *flat_outs = _adamw_replicated(
        small_received, *[[two_d(src[k]) for k in names] for src in (weights, mom1, mom2)])
    for i, k in enumerate(names):
        results[k] = [a.reshape(weights[k].shape) for a in flat_outs[4 * i:4 * i + 4]]
    loss = loss_sum[0, 0]
    me = 4 * lax.axis_index("x") + 2 * lax.axis_index("y") + lax.axis_index("c")
    conv_mine = lax.dynamic_slice(g_conv_sum, (0, me * 192), (CONV_K, 192))
    results["dn_conv_w"] = _reduce_adamw(conv_mine[None], dn_conv_w, m_dn_conv_w, v_dn_conv_w, "adamw_dn_conv_w")

    return (loss, grad_x.reshape(nb, s, D_MODEL), *[results[k][0] for k in WEIGHT_ORDER],
            *[results[k][1] for k in WEIGHT_ORDER], *[results[k][2] for k in WEIGHT_ORDER],
            *[results[k][3] for k in WEIGHT_ORDER])
```

```python
import jax
import jax.numpy as jnp
from jax import lax
from jax.experimental import pallas as pl
from jax.experimental.pallas import tpu as pltpu

F32 = jnp.float32
BF16 = jnp.bfloat16

N_DEV = 8
D_MODEL = 1024
SGU_WIDTH = 512
SGU_GROUPS = 4
SGU_CHUNK = 128
DN_WIDTH = 512
DN_HEADS = 4
DN_HEAD_DIM = 128
DN_CHUNK = 128
CONV_K = 4
CONV_HALO = 8
PLE_DIM = 256
EPS = 1e-6
IN_COLS = 3592
IN_SHARD = IN_COLS // N_DEV
GATE_PAD = 128

ADAM_LR = 0.001
ADAM_B1 = 0.9
ADAM_B2 = 0.999
ADAM_EPS = 1e-08
ADAM_WD = 0.01
ADAM_STEP = 10

LANES = 128
VMEM_LIMIT = 56 * 1024 * 1024
MESH = pl.DeviceIdType.MESH

REPLICATED = (("norm_g", (1, D_MODEL)), ("sgu_ln_g", (1, SGU_WIDTH)), ("sgu_ln_b", (1, SGU_WIDTH)),
              ("sgu_w_s", (1, SGU_GROUPS, SGU_CHUNK, SGU_CHUNK)), ("sgu_b_s", (1, SGU_GROUPS, SGU_CHUNK)),
              ("dn_a_log", (1, DN_HEADS)), ("dn_dt_bias", (1, DN_HEADS)), ("dn_o_norm_g", (1, DN_HEAD_DIM)),
              ("ple_norm_g", (1, D_MODEL)), ("final_norm_g", (D_MODEL,)))
WEIGHT_ORDER = ("norm_g", "w_in", "sgu_ln_g", "sgu_ln_b", "sgu_w_s", "sgu_b_s", "dn_conv_w", "dn_a_log",
                "dn_dt_bias", "dn_o_norm_g", "w_out", "ple_norm_g", "ple_gate_w", "ple_proj_w", "final_norm_g")


def _size(shape):
    n = 1
    for s in shape:
        n *= s
    return n


SMALL_LAYOUT = (("conv", (CONV_K, 3 * DN_WIDTH)),) + REPLICATED + (("loss", (1,)),)
SMALL_PIECE_ROWS = tuple(-(-_size(s) // LANES) for _, s in SMALL_LAYOUT)
SMALL_ROWS = -(-sum(SMALL_PIECE_ROWS) // 8) * 8


def _bdot(a, b):
    return jnp.dot(a.astype(BF16), b.astype(BF16), preferred_element_type=F32)


def _sigmoid(x):
    return 0.5 * jnp.tanh(0.5 * x) + 0.5


@jax.custom_vjp
def _silu(x):
    return x * _sigmoid(x)


def _silu_fwd(x):
    s = _sigmoid(x)
    return x * s, (x, s)


def _silu_bwd(res, ct):
    x, s = res
    return (ct * (s * (1.0 + x * (1.0 - s))),)


_silu.defvjp(_silu_fwd, _silu_bwd)


def _normal_cdf(x):
    return 0.5 + 0.5 * lax.erf(x * (0.5 ** 0.5))


@jax.custom_vjp
def _gelu(x):
    return x * _normal_cdf(x)


def _gelu_fwd(x):
    cdf = _normal_cdf(x)
    return x * cdf, (x, cdf)


def _gelu_bwd(res, ct):
    x, cdf = res
    pdf = jnp.exp(-0.5 * x * x) * ((2.0 * jnp.pi) ** -0.5)
    return (ct * (cdf + x * pdf),)


_gelu.defvjp(_gelu_fwd, _gelu_bwd)


def _softplus(x):
    return jnp.maximum(x, 0.0) + jnp.log1p(jnp.exp(-jnp.abs(x)))


@jax.custom_vjp
def _l2n(x):
    return x * lax.rsqrt(jnp.sum(x * x, axis=-1, keepdims=True) + EPS)


def _l2n_fwd(x):
    r = lax.rsqrt(jnp.sum(x * x, axis=-1, keepdims=True) + EPS)
    n = x * r
    return n, (n, r)


def _l2n_bwd(res, ct):
    n, r = res
    return (r * (ct - n * jnp.sum(ct * n, axis=-1, keepdims=True)),)


_l2n.defvjp(_l2n_fwd, _l2n_bwd)


def _rms(x):
    r = lax.rsqrt(jnp.mean(x * x, axis=-1, keepdims=True) + EPS)
    return x * r, r


def _rms_bwd(dn, n, r):
    return r * (dn - n * jnp.mean(dn * n, axis=-1, keepdims=True))


@jax.custom_vjp
def _rms_normed(x):
    return _rms(x)[0]


def _rms_normed_fwd(x):
    n, r = _rms(x)
    return n, (n, r)


def _rms_normed_bwd(res, ct):
    return (_rms_bwd(ct, *res),)


_rms_normed.defvjp(_rms_normed_fwd, _rms_normed_bwd)


def _onehot_row(idx, width):
    return (lax.broadcasted_iota(jnp.int32, (1, width), 1) == idx).astype(F32)


def _rowsum(x):
    return jnp.sum(x, axis=0, keepdims=True)


def _iota2(n):
    return lax.broadcasted_iota(jnp.int32, (n, n), 0), lax.broadcasted_iota(jnp.int32, (n, n), 1)


def _bmm(a, b):
    return lax.dot_general(a.astype(BF16), b.astype(BF16), (((2,), (1,)), ((0,), (0,))), preferred_element_type=F32)


def _bmm_nt(a, b):
    return lax.dot_general(a.astype(BF16), b.astype(BF16), (((2,), (2,)), ((0,), (0,))), preferred_element_type=F32)


def _bmm_tn(a, b):
    return lax.dot_general(a.astype(BF16), b.astype(BF16), (((1,), (1,)), ((0,), (0,))), preferred_element_type=F32)


def _tri_inv_impl(a):
    n = a.shape[-1]
    r, c = _iota2(n)
    x = r ^ c
    eye = (r == c).astype(F32)
    ad = jnp.where(x < 16, a, 0.0)
    p2 = _bmm(ad, ad)
    e = p2 - ad - _bmm(ad, p2)
    p4 = _bmm(p2, p2)
    e = e + p4 + _bmm(e, p4)
    p8 = _bmm(p4, p4)
    e = e + p8 + _bmm(e, p8)
    size = 16
    while size < n:
        m = jnp.where(jnp.logical_and(x < 2 * size, x >= size), a, 0.0)
        f = m + _bmm(m, e)
        e = e - f - _bmm(e, f)
        size *= 2
    return e + eye


@jax.custom_vjp
def _tri_inv(a, known):
    return _tri_inv_impl(a) if known is None else known


def _tri_inv_fwd(a, known):
    t = _tri_inv(a, known)
    return t, (t, known)


def _tri_inv_bwd(res, dt):
    t, known = res
    return -_bmm_tn(t, _bmm_nt(dt, t)), None if known is None else jnp.zeros_like(known)


_tri_inv.defvjp(_tri_inv_fwd, _tri_inv_bwd)


@jax.custom_vjp
def _standardized(x):
    xc = x - jnp.mean(x, axis=-1, keepdims=True)
    return xc * lax.rsqrt(jnp.mean(xc * xc, axis=-1, keepdims=True) + EPS)


def _standardized_fwd(x):
    xc = x - jnp.mean(x, axis=-1, keepdims=True)
    rstd = lax.rsqrt(jnp.mean(xc * xc, axis=-1, keepdims=True) + EPS)
    y = xc * rstd
    return y, (y, rstd)


def _standardized_bwd(res, ct):
    y, rstd = res
    return (rstd * (ct - jnp.mean(ct, axis=-1, keepdims=True) - y * jnp.mean(ct * y, axis=-1, keepdims=True)),)


_standardized.defvjp(_standardized_fwd, _standardized_bwd)


def _sgu_core(u, v, z, lg, lb, ws, bcol):
    n = ws.shape[0]
    r, c = _iota2(n)
    wm = jnp.where(r >= c, ws, 0.0)
    gu = _gelu(u)
    gv = _gelu(v)
    ln = _standardized(gv) * lg + lb
    s = _bdot(wm, ln) + bcol
    return gu * s * _silu(z)


def _lanes_of(x):
    return jnp.concatenate([x[i] for i in range(x.shape[0])], axis=1)


def _batch_of(x, width):
    return jnp.concatenate([x[None, :, i * width:(i + 1) * width] for i in range(x.shape[1] // width)], axis=0)


def _mask_dot(mask, x):
    hi = x.astype(BF16)
    lo = (x - hi.astype(F32)).astype(BF16)
    m = mask.astype(BF16)
    return jnp.dot(m, hi, preferred_element_type=F32) + jnp.dot(m, lo, preferred_element_type=F32)


def _tri_mask(n, upper):
    r, c = _iota2(n)
    return (r <= c) if upper else (r >= c)


@jax.custom_vjp
def _cumsum_rows(x):
    return _mask_dot(_tri_mask(x.shape[0], False), x)


def _cumsum_rows_fwd(x):
    return _cumsum_rows(x), None


def _cumsum_rows_bwd(_, ct):
    return (_mask_dot(_tri_mask(ct.shape[0], True), ct),)


_cumsum_rows.defvjp(_cumsum_rows_fwd, _cumsum_rows_bwd)


@jax.custom_vjp
def _colsum_all_rows(x):
    return _mask_dot(jnp.ones((x.shape[0], x.shape[0]), jnp.bool_), x)


def _colsum_all_rows_fwd(x):
    return _colsum_all_rows(x), None


def _colsum_all_rows_bwd(_, ct):
    return (_mask_dot(jnp.ones((ct.shape[0], ct.shape[0]), jnp.bool_), ct),)


_colsum_all_rows.defvjp(_colsum_all_rows_fwd, _colsum_all_rows_bwd)


def _dn_core(cq, ck, cv, z, logits, state, alog, dtb, og, t_known=None):
    gn, cn, dh = cq.shape
    heads = gn // logits.shape[0]
    q = _l2n(_silu(cq)) * (dh ** -0.5)
    k = _l2n(_silu(ck))
    v = _silu(cv)
    beta_lanes = _sigmoid(logits)
    g_lanes = -jnp.exp(alog) * _softplus(logits + dtb)
    column = lambda rows, lane: jnp.sum(rows * _onehot_row(lane, rows.shape[-1]), axis=-1, keepdims=True)[None]
    beta = jnp.concatenate([column(beta_lanes[i // heads], i % heads) for i in range(gn)], axis=0)
    g = jnp.concatenate([column(g_lanes[i // heads], heads + i % heads) for i in range(gn)], axis=0)
    r, c = _iota2(cn)
    tril = r >= c
    rw = lax.broadcasted_iota(jnp.int32, (cn, dh), 0)
    cw = lax.broadcasted_iota(jnp.int32, (cn, dh), 1)
    upper_wide = (rw <= cw).astype(F32)
    g_wide = jnp.broadcast_to(g, (gn, cn, dh))
    gc_wide = _batch_of(_cumsum_rows(_lanes_of(g_wide)), dh)
    gc_cols = _batch_of(_colsum_all_rows(_lanes_of(g_wide * upper_wide)), dh)[:, :, :cn]
    decay = jnp.exp(jnp.where(tril, gc_wide[:, :, :cn] - gc_cols, -1e30))
    kb = k * beta
    kk = _bmm_nt(kb, k) * decay
    t = _tri_inv(jnp.where(r > c, kk, 0.0), t_known)
    eg = jnp.exp(gc_wide)
    sol = _bmm(t, jnp.concatenate([v * beta, kb * eg], axis=-1))
    u_val, w_dec = sol[:, :, :dh], sol[:, :, dh:]
    qk = _bmm_nt(q, k) * decay
    g_last = jnp.sum(g_wide, axis=1, keepdims=True)
    k_dec = k * jnp.exp(g_last - gc_wide)
    ws = _bmm(jnp.concatenate([w_dec, q * eg], axis=1), state)
    v_new = u_val - ws[:, :cn]
    o = ws[:, cn:] + _bmm(qk, v_new)
    new_state = state * jnp.exp(g_last) + _bmm_tn(k_dec, v_new)
    return _rms_normed(o) * og * _silu(z), new_state, t


N_CHIPS = 4
HBM_SPEC = pl.BlockSpec(memory_space=pl.ANY)


def _place():
    return lax.axis_index("x"), lax.axis_index("y"), lax.axis_index("c")


def _other_chip(k):
    x, y, _ = _place()
    px = 1 - x if k & 2 else x
    py = 1 - y if k & 1 else y
    return px, py, 2 * px + py


def _remote(src, dst, send_sem, recv_sem, device):
    return pltpu.make_async_remote_copy(src_ref=src, dst_ref=dst, send_sem=send_sem, recv_sem=recv_sem,
                                        device_id=device, device_id_type=MESH)


def _other_device(k):
    x, y, c = _place()
    px = 1 - x if k & 4 else x
    py = 1 - y if k & 2 else y
    pc = 1 - c if k & 1 else c
    return (px, py, pc), 4 * px + 2 * py + pc


def _direct_exchange(srcs, outs, send_sems, recv_sems, local_sems, gather):
    x, y, c = _place()
    me = 4 * x + 2 * y + c

    def copies(arriving):
        out_list = []
        for a, (src, out) in enumerate(zip(srcs, outs)):
            for k in range(1, N_DEV):
                peer, index = _other_device(k)
                mine = src if gather else src.at[index]
                out_list.append(_remote(mine, out.at[index if arriving else me], send_sems.at[a, k - 1],
                                        recv_sems.at[a, k - 1], peer))
        return out_list

    def local_copies():
        return [pltpu.make_async_copy(src if gather else src.at[me], out.at[me], local_sems.at[a])
                for a, (src, out) in enumerate(zip(srcs, outs))]

    def start():
        for cp in local_copies() + copies(False):
            cp.start()

    def wait():
        for cp in copies(True):
            cp.wait_recv()
        for cp in copies(False):
            cp.wait_send()
        for cp in local_copies():
            cp.wait()

    return start, wait


def _exchange_scratch(n):
    return [pltpu.SemaphoreType.DMA((n, N_DEV - 1)), pltpu.SemaphoreType.DMA((n, N_DEV - 1)), pltpu.SemaphoreType.DMA((n,))]


def _all_gather(shards):
    n = len(shards)

    def body(*refs):
        srcs, outs = refs[:n], refs[n:2 * n]
        send_sems, recv_sems, local_sems = refs[2 * n:]
        x, y, c = _place()
        me = 4 * x + 2 * y + c
        sibling = (x, y, 1 - c)
        local = [pltpu.make_async_copy(srcs[a], outs[a].at[me], local_sems.at[a]) for a in range(n)]
        for cp in local:
            cp.start()
        sends = []
        for a in range(n):
            sends.append(_remote(srcs[a], outs[a].at[me], send_sems.at[a, 0], recv_sems.at[a, 0], sibling))
        for k in range(1, N_CHIPS):
            px, py, _ = _other_chip(k)
            for a in range(n):
                sends.append(_remote(srcs[a], outs[a].at[me], send_sems.at[a, k], recv_sems.at[a, k], (px, py, c)))
        for cp in sends:
            cp.start()
        passed = []
        for k in range(1, N_CHIPS):
            px, py, _ = _other_chip(k)
            blk = 4 * px + 2 * py + c
            for a in range(n):
                _remote(srcs[a], outs[a].at[blk], send_sems.at[a, k], recv_sems.at[a, k], (px, py, c)).wait_recv()
            for a in range(n):
                cp = _remote(outs[a].at[blk], outs[a].at[blk], send_sems.at[a, 3 + k], recv_sems.at[a, 3 + k], sibling)
                cp.start()
                passed.append(cp)
        for a in range(n):
            _remote(srcs[a], outs[a].at[me + 1 - 2 * c], send_sems.at[a, 0], recv_sems.at[a, 0], sibling).wait_recv()
        for k in range(1, N_CHIPS):
            px, py, _ = _other_chip(k)
            blk = 4 * px + 2 * py + 1 - c
            for a in range(n):
                _remote(srcs[a], outs[a].at[blk], send_sems.at[a, 3 + k], recv_sems.at[a, 3 + k], sibling).wait_recv()
        for cp in sends + passed:
            cp.wait_send()
        for cp in local:
            cp.wait()

    return pl.pallas_call(
        body, name="all_gather_weights",
        out_shape=tuple(jax.ShapeDtypeStruct((N_DEV,) + a.shape, a.dtype) for a in shards),
        in_specs=[HBM_SPEC] * n, out_specs=(HBM_SPEC,) * n,
        scratch_shapes=[pltpu.SemaphoreType.DMA((n, N_DEV - 1)), pltpu.SemaphoreType.DMA((n, N_DEV - 1)),
                        pltpu.SemaphoreType.DMA((n,))],
    )(*shards)


def _reduce_exchange(by_device, small):
    _, rows, cols = by_device.shape

    def body(g_ref, small_ref, out_ref, small_out_ref, from_sibling, small_from_sibling, stage, sums, small_own, small_sum,
             pair_send, pair_recv, chip_send, chip_recv, local_sems):
        x, y, c = _place()
        mine = 2 * x + y
        sibling = (x, y, 1 - c)
        chips = [(x, y, mine)] + [_other_chip(k) for k in range(1, N_CHIPS)]
        to_sibling = [_remote(g_ref.at[2 * chips[k][2] + 1 - c], from_sibling.at[k], pair_send.at[k], pair_recv.at[k], sibling)
                      for k in range(N_CHIPS)]
        to_sibling.append(_remote(small_ref, small_from_sibling, pair_send.at[N_CHIPS], pair_recv.at[N_CHIPS], sibling))
        for cp in to_sibling:
            cp.start()
        small_mine = pltpu.make_async_copy(small_ref, small_own, local_sems.at[0])
        small_mine.start()
        to_chips = []
        for k in (1, 2, 3, 0):
            px, py, chip = chips[k]
            mine_k = pltpu.make_async_copy(g_ref.at[2 * chip + c], stage, local_sems.at[1])
            mine_k.start()
            to_sibling[k].wait_recv()
            mine_k.wait()
            sums[k] = (stage[...] + from_sibling[k]).astype(sums.dtype)
            if k:
                cp = _remote(sums.at[k], out_ref.at[mine], chip_send.at[0, k - 1], chip_recv.at[0, k - 1], (px, py, c))
                cp.start()
                to_chips.append(cp)
        own_block = pltpu.make_async_copy(sums.at[0], out_ref.at[mine], local_sems.at[2])
        own_block.start()
        to_sibling[N_CHIPS].wait_recv()
        small_mine.wait()
        small_sum[...] = small_own[...] + small_from_sibling[...]
        for k in range(1, N_CHIPS):
            px, py, _ = chips[k]
            cp = _remote(small_sum, small_out_ref.at[mine], chip_send.at[1, k - 1], chip_recv.at[1, k - 1], (px, py, c))
            cp.start()
            to_chips.append(cp)
        own_small = pltpu.make_async_copy(small_sum, small_out_ref.at[mine], local_sems.at[3])
        own_small.start()
        for k in range(1, N_CHIPS):
            px, py, chip = chips[k]
            _remote(sums.at[k], out_ref.at[chip], chip_send.at[0, k - 1], chip_recv.at[0, k - 1], (px, py, c)).wait_recv()
            _remote(small_sum, small_out_ref.at[chip], chip_send.at[1, k - 1], chip_recv.at[1, k - 1], (px, py, c)).wait_recv()
        for cp in to_sibling + to_chips:
            cp.wait_send()
        own_block.wait()
        own_small.wait()

    return pl.pallas_call(
        body, name="grad_reduce_exchange",
        out_shape=(jax.ShapeDtypeStruct((N_CHIPS, rows, cols), BF16), jax.ShapeDtypeStruct((N_CHIPS,) + small.shape, F32)),
        in_specs=[HBM_SPEC, HBM_SPEC], out_specs=(HBM_SPEC, HBM_SPEC),
        scratch_shapes=[pltpu.VMEM((N_CHIPS, rows, cols), F32), pltpu.VMEM(small.shape, F32), pltpu.VMEM((rows, cols), F32),
                        pltpu.VMEM((N_CHIPS, rows, cols), BF16), pltpu.VMEM(small.shape, F32), pltpu.VMEM(small.shape, F32),
                        pltpu.SemaphoreType.DMA((N_CHIPS + 1,)), pltpu.SemaphoreType.DMA((N_CHIPS + 1,)),
                        pltpu.SemaphoreType.DMA((2, N_CHIPS - 1)), pltpu.SemaphoreType.DMA((2, N_CHIPS - 1)),
                        pltpu.SemaphoreType.DMA((4,))],
        compiler_params=pltpu.CompilerParams(vmem_limit_bytes=VMEM_LIMIT),
    )(by_device, small)


def _params(n_axes):
    return pltpu.CompilerParams(dimension_semantics=("arbitrary",) * n_axes, vmem_limit_bytes=VMEM_LIMIT)


def _whole(shape):
    return pl.BlockSpec(shape, lambda *_: (0,) * len(shape))


VMEM_SPEC = pl.BlockSpec(memory_space=pltpu.VMEM)


def _inproj_fwd(x2, seq_len, norm_g, wat, wqt, wzt, wgt, sgu_weights, conv_w, later_shards):
    t = x2.shape[0]
    tm = min(512, seq_len)
    tiles_per_seq = seq_len // tm
    steps = t // tm
    ns = len(later_shards)

    def body(x_ref, g_ref, wa_ref, wq_ref, wz_ref, wg_ref, lg_ref, lb_ref, ws_ref, bt_ref, cw_ref, *rest):
        shard_refs, rest = rest[:ns], rest[ns:]
        a_ref, q_ref, z_ref, l_ref, sgu_ref, c_ref = rest[:6]
        gathered_refs, (xpad_ref, send_sems, recv_sems, local_sems) = rest[6:6 + ns], rest[6 + ns:]
        start_gather, wait_gather = _direct_exchange(shard_refs, gathered_refs, send_sems, recv_sems, local_sems, True)
        pl.when(pl.program_id(0) == 0)(start_gather)
        n, _ = _rms(x_ref[...])
        xn = (n * g_ref[...]).astype(BF16)
        for w_ref, o_ref in ((wa_ref, a_ref), (wq_ref, q_ref), (wz_ref, z_ref), (wg_ref, l_ref)):
            width = w_ref.shape[0]
            for c0 in range(0, width, 512):
                c1 = min(c0 + 512, width)
                o_ref[:, c0:c1] = lax.dot_general(xn, w_ref[c0:c1, :], (((1,), (1,)), ((), ())),
                                                  preferred_element_type=F32)
        for row0 in range(0, tm, SGU_CHUNK):
            for grp in range(SGU_GROUPS):
                args = _sgu_pieces(a_ref, lg_ref, lb_ref, ws_ref, bt_ref, row0, grp)
                sgu_ref[pl.ds(row0, SGU_CHUNK), pl.ds(grp * 128, 128)] = _sgu_core(*args).astype(sgu_ref.dtype)

        @pl.when(pl.program_id(0) % tiles_per_seq == 0)
        def _():
            xpad_ref[0:CONV_HALO, :] = jnp.zeros((CONV_HALO, xpad_ref.shape[1]), F32)

        xpad_ref[CONV_HALO:, :] = q_ref[...]
        acc = None
        for j in range(CONV_K):
            term = cw_ref[j:j + 1, :] * xpad_ref[pl.ds(CONV_HALO - CONV_K + 1 + j, tm), :]
            acc = term if acc is None else acc + term
        c_ref[...] = acc
        xpad_ref[0:CONV_HALO, :] = xpad_ref[tm:tm + CONV_HALO, :]
        pl.when(pl.program_id(0) == steps - 1)(wait_gather)

    widths = (wat.shape[0], wqt.shape[0], wzt.shape[0], wgt.shape[0])
    tile = lambda w: pl.BlockSpec((tm, w), lambda i: (i, 0))
    sgu_shapes = ((1, SGU_WIDTH), (1, SGU_WIDTH), (SGU_GROUPS, SGU_CHUNK, SGU_CHUNK), (SGU_CHUNK, SGU_GROUPS))
    return pl.pallas_call(
        body, name="inproj_sgu_conv_fwd", grid=(steps,),
        out_shape=tuple(jax.ShapeDtypeStruct((t, w), F32) for w in widths)
        + (jax.ShapeDtypeStruct((t, SGU_WIDTH), BF16), jax.ShapeDtypeStruct((t, widths[1]), F32))
        + tuple(jax.ShapeDtypeStruct((N_DEV,) + a.shape, a.dtype) for a in later_shards),
        in_specs=[tile(D_MODEL), _whole((1, D_MODEL)), VMEM_SPEC, VMEM_SPEC, VMEM_SPEC, VMEM_SPEC]
        + [_whole(s) for s in sgu_shapes] + [_whole((CONV_K, widths[1]))] + [HBM_SPEC] * ns,
        out_specs=tuple(tile(w) for w in widths) + (tile(SGU_WIDTH), tile(widths[1])) + (HBM_SPEC,) * ns,
        scratch_shapes=[pltpu.VMEM((CONV_HALO + tm, widths[1]), F32)] + _exchange_scratch(ns),
        compiler_params=_params(1),
    )(x2, norm_g, wat, wqt, wzt, wgt, *sgu_weights, conv_w, *later_shards)


def _sgu_pieces(uvz_ref, lg_ref, lb_ref, ws_ref, bt_ref, row0, grp):
    rows = pl.ds(row0, SGU_CHUNK)
    lanes = pl.ds(grp * 128, 128)
    u = uvz_ref[rows, pl.ds(grp * 128, 128)]
    v = uvz_ref[rows, pl.ds(SGU_WIDTH + grp * 128, 128)]
    z = uvz_ref[rows, pl.ds(2 * SGU_WIDTH + grp * 128, 128)]
    bcol = jnp.sum(bt_ref[...] * _onehot_row(grp, SGU_GROUPS), axis=-1, keepdims=True)
    return u, v, z, lg_ref[:, lanes], lb_ref[:, lanes], ws_ref[grp], bcol


def _sgu_bwd_tile(uvz_ref, do_ref, sgu_refs, duvz_ref, grad_refs):
    lg_ref, lb_ref, ws_ref, bt_ref = sgu_refs
    dlg_ref, dlb_ref, dws_ref, dbt_ref = grad_refs
    for row0 in range(0, uvz_ref.shape[0], SGU_CHUNK):
        rows = pl.ds(row0, SGU_CHUNK)
        for grp in range(SGU_GROUPS):
            lanes = pl.ds(grp * 128, 128)
            args = _sgu_pieces(uvz_ref, lg_ref, lb_ref, ws_ref, bt_ref, row0, grp)
            _, pull = jax.vjp(_sgu_core, *args)
            du, dv, dz, dlg, dlb, dws, dbcol = pull(do_ref[rows, lanes])
            duvz_ref[rows, pl.ds(grp * 128, 128)] = du.astype(duvz_ref.dtype)
            duvz_ref[rows, pl.ds(SGU_WIDTH + grp * 128, 128)] = dv.astype(duvz_ref.dtype)
            duvz_ref[rows, pl.ds(2 * SGU_WIDTH + grp * 128, 128)] = dz.astype(duvz_ref.dtype)
            dlg_ref[:, lanes] += dlg
            dlb_ref[:, lanes] += dlb
            dws_ref[grp] += dws
            dbt_ref[...] += dbcol * _onehot_row(grp, SGU_GROUPS)


def _dn_pairs(nb):
    return [(b, h) for b in range(nb) for h in range(DN_HEADS)]


def _dn_batch_args(c_ref, z_ref):
    pairs = _dn_pairs(c_ref.shape[0])
    pick = lambda ref, b, col: ref[b, :, pl.ds(col, DN_HEAD_DIM)]
    cq = jnp.stack([pick(c_ref, b, h * DN_HEAD_DIM) for b, h in pairs])
    ck = jnp.stack([pick(c_ref, b, DN_WIDTH + h * DN_HEAD_DIM) for b, h in pairs])
    cv = jnp.stack([pick(c_ref, b, 2 * DN_WIDTH + h * DN_HEAD_DIM) for b, h in pairs])
    z = jnp.stack([pick(z_ref, b, h * DN_HEAD_DIM) for b, h in pairs])
    return cq, ck, cv, z


def _dn_weight_specs():
    return [_whole((CONV_K, 3 * DN_WIDTH)), _whole((1, GATE_PAD)), _whole((1, GATE_PAD)), _whole((1, DN_HEAD_DIM))]


def _dn_fwd(conv_out, zg, logits, alog, dtb, og):
    nb, s, _ = conv_out.shape
    nc = s // DN_CHUNK
    pairs = _dn_pairs(nb)
    gn = len(pairs)
    chunk = lambda w: pl.BlockSpec((nb, DN_CHUNK, w), lambda n: (0, n, 0))

    def body(c_ref, z_ref, l_ref, alog_ref, dtb_ref, og_ref, out_ref, st_ref, inv_ref, state_ref):
        n = pl.program_id(0)

        @pl.when(n == 0)
        def _():
            state_ref[...] = jnp.zeros_like(state_ref)

        cq, ck, cv, z = _dn_batch_args(c_ref, z_ref)
        state = state_ref[...]
        st_ref[...] = state
        out, new_state, t = _dn_core(cq, ck, cv, z, l_ref[...], state, alog_ref[...], dtb_ref[...], og_ref[...])
        state_ref[...] = new_state
        inv_ref[...] = t.astype(inv_ref.dtype)
        for i, (b, h) in enumerate(pairs):
            out_ref[b, :, pl.ds(h * DN_HEAD_DIM, DN_HEAD_DIM)] = out[i].astype(out_ref.dtype)

    per_chunk = pl.BlockSpec((None, gn, DN_HEAD_DIM, DN_HEAD_DIM), lambda n: (n, 0, 0, 0))
    return pl.pallas_call(
        body, name="deltanet_fwd", grid=(nc,),
        out_shape=(jax.ShapeDtypeStruct((nb, s, DN_WIDTH), BF16),
                   jax.ShapeDtypeStruct((nc, gn, DN_HEAD_DIM, DN_HEAD_DIM), F32),
                   jax.ShapeDtypeStruct((nc, gn, DN_CHUNK, DN_CHUNK), BF16)),
        in_specs=[chunk(3 * DN_WIDTH), chunk(DN_WIDTH), chunk(GATE_PAD)] + _dn_weight_specs()[1:],
        out_specs=(chunk(DN_WIDTH), per_chunk, pl.BlockSpec((None, gn, DN_CHUNK, DN_CHUNK), lambda n: (n, 0, 0, 0))),
        scratch_shapes=[pltpu.VMEM((gn, DN_HEAD_DIM, DN_HEAD_DIM), F32)],
        compiler_params=_params(1),
    )(conv_out, zg, logits, alog, dtb, og)


def _dn_bwd(qkv, conv_out, zg, logits, conv_w, alog, dtb, og, states, inverses, d_out, head_grads):
    nb, s, _ = qkv.shape
    nc = s // DN_CHUNK
    rev = lambda n: nc - 1 - n
    pairs = _dn_pairs(nb)
    gn = len(pairs)
    ng = len(head_grads)

    def body(cur_ref, c_ref, z_ref, l_ref, w_ref, alog_ref, dtb_ref, og_ref, st_ref, inv_ref, do_ref, *rest):
        grad_refs, rest = rest[:ng], rest[ng:]
        dqkv_ref, dz_ref, dl_ref, dw_ref, dalog_ref, ddtb_ref, dog_ref = rest[:7]
        recv_refs, (dstate_ref, dcpad_ref, send_sems, recv_sems, local_sems) = rest[7:7 + ng], rest[7 + ng:]
        n = pl.program_id(0)
        start_exchange, wait_exchange = _direct_exchange(grad_refs, recv_refs, send_sems, recv_sems, local_sems, False)
        pl.when(n == 0)(start_exchange)

        @pl.when(n == 0)
        def _():
            dw_ref[...] = jnp.zeros_like(dw_ref)
            dalog_ref[...] = jnp.zeros_like(dalog_ref)
            ddtb_ref[...] = jnp.zeros_like(ddtb_ref)
            dog_ref[...] = jnp.zeros_like(dog_ref)
            dstate_ref[...] = jnp.zeros_like(dstate_ref)
            dcpad_ref[:, DN_CHUNK:, :] = jnp.zeros((nb, CONV_HALO, 3 * DN_WIDTH), F32)

        cq, ck, cv, z = _dn_batch_args(c_ref, z_ref)
        d_out_g = jnp.stack([do_ref[b, :, pl.ds(h * DN_HEAD_DIM, DN_HEAD_DIM)] for b, h in pairs])
        t_known = inv_ref[...].astype(F32)
        core = lambda *args: _dn_core(*args, t_known=t_known)[:2]
        _, pull = jax.vjp(core, cq, ck, cv, z, l_ref[...], st_ref[...], alog_ref[...], dtb_ref[...], og_ref[...])
        dcq, dck, dcv, dz, dlog, dstate, dalog, ddtb, dog = pull((d_out_g, dstate_ref[...]))
        dstate_ref[...] = dstate
        dl_ref[...] = dlog.astype(dl_ref.dtype)
        dalog_ref[...] += dalog
        ddtb_ref[...] += ddtb
        dog_ref[...] += dog
        for i, (b, h) in enumerate(pairs):
            dcpad_ref[b, 0:DN_CHUNK, pl.ds(h * DN_HEAD_DIM, DN_HEAD_DIM)] = dcq[i]
            dcpad_ref[b, 0:DN_CHUNK, pl.ds(DN_WIDTH + h * DN_HEAD_DIM, DN_HEAD_DIM)] = dck[i]
            dcpad_ref[b, 0:DN_CHUNK, pl.ds(2 * DN_WIDTH + h * DN_HEAD_DIM, DN_HEAD_DIM)] = dcv[i]
            dz_ref[b, :, pl.ds(h * DN_HEAD_DIM, DN_HEAD_DIM)] = dz[i].astype(dz_ref.dtype)
        for b in range(nb):
            xb = cur_ref[b]
            dx = None
            for j in range(CONV_K):
                shifted = dcpad_ref[b, pl.ds(CONV_K - 1 - j, DN_CHUNK), :]
                term = w_ref[j:j + 1, :] * shifted
                dx = term if dx is None else dx + term
                dw_ref[j:j + 1, :] += _rowsum(shifted * xb)
            dqkv_ref[b] = dx.astype(dqkv_ref.dtype)
            dcpad_ref[b, DN_CHUNK:, :] = dcpad_ref[b, 0:CONV_HALO, :]
        pl.when(n == nc - 1)(wait_exchange)

    chunk = lambda w: pl.BlockSpec((nb, DN_CHUNK, w), lambda n: (0, rev(n), 0))
    return pl.pallas_call(
        body, name="deltanet_bwd", grid=(nc,),
        out_shape=(jax.ShapeDtypeStruct((nb, s, 3 * DN_WIDTH), BF16), jax.ShapeDtypeStruct((nb, s, DN_WIDTH), BF16),
                   jax.ShapeDtypeStruct((nb, s, GATE_PAD), BF16), jax.ShapeDtypeStruct((CONV_K, 3 * DN_WIDTH), F32),
                   jax.ShapeDtypeStruct((1, GATE_PAD), F32), jax.ShapeDtypeStruct((1, GATE_PAD), F32),
                   jax.ShapeDtypeStruct((1, DN_HEAD_DIM), F32))
        + tuple(jax.ShapeDtypeStruct(a.shape, a.dtype) for a in head_grads),
        in_specs=[chunk(3 * DN_WIDTH), chunk(3 * DN_WIDTH), chunk(DN_WIDTH), chunk(GATE_PAD)] + _dn_weight_specs() + [
            pl.BlockSpec((None, gn, DN_HEAD_DIM, DN_HEAD_DIM), lambda n: (rev(n), 0, 0, 0)),
            pl.BlockSpec((None, gn, DN_CHUNK, DN_CHUNK), lambda n: (rev(n), 0, 0, 0)),
            chunk(DN_WIDTH)] + [HBM_SPEC] * ng,
        out_specs=(chunk(3 * DN_WIDTH), chunk(DN_WIDTH), chunk(GATE_PAD), _whole((CONV_K, 3 * DN_WIDTH)),
                   _whole((1, GATE_PAD)), _whole((1, GATE_PAD)), _whole((1, DN_HEAD_DIM))) + (HBM_SPEC,) * ng,
        scratch_shapes=[pltpu.VMEM((gn, DN_HEAD_DIM, DN_HEAD_DIM), F32),
                        pltpu.VMEM((nb, DN_CHUNK + CONV_HALO, 3 * DN_WIDTH), F32)] + _exchange_scratch(ng),
        compiler_params=_params(1),
    )(qkv, conv_out, zg, logits, conv_w, alog, dtb, og, states, inverses, d_out, *head_grads)


def _head(a_out, b_out, x2, p2, target, w_out, w_out_t, w_gate, w_gate_t, w_proj, ple_g, fin_g):
    t = x2.shape[0]
    tm = min(512, t)
    steps = t // tm

    def body(a_ref, b_ref, x_ref, p_ref, y_ref, wo_ref, wot_ref, wg_ref, wgt_ref, wp_ref, pg_ref, fg_ref,
             da_ref, db_ref, dh_ref, dwo_hbm, dwg_hbm, dwp_hbm, dpg_ref, dfg_ref, loss_ref,
             dwo_acc, dwg_acc, dwp_acc, rows_stage, cols_stage):
        i = pl.program_id(0)

        @pl.when(i == 0)
        def _():
            dwo_acc[...] = jnp.zeros_like(dwo_acc)
            dwg_acc[...] = jnp.zeros_like(dwg_acc)
            dwp_acc[...] = jnp.zeros_like(dwp_acc)
            dpg_ref[...] = jnp.zeros_like(dpg_ref)
            dfg_ref[...] = jnp.zeros_like(dfg_ref)
            loss_ref[...] = jnp.zeros_like(loss_ref)

        a = a_ref[...]
        bb = b_ref[...]
        pb = p_ref[...].astype(BF16)
        pg = pg_ref[...]
        fg = fg_ref[...]
        h1 = (x_ref[...] + jnp.dot(a, wo_ref[0:SGU_WIDTH, :], preferred_element_type=F32)
              + jnp.dot(bb, wo_ref[SGU_WIDTH:, :], preferred_element_type=F32))
        n1, r1 = _rms(h1)
        rn = (n1 * pg).astype(BF16)
        gate = _sigmoid(jnp.dot(rn, wg_ref[...], preferred_element_type=F32))
        pp = jnp.dot(pb, wp_ref[...], preferred_element_type=F32)
        h2 = h1 + gate * pp
        n2, r2 = _rms(h2)
        err = n2 * fg - y_ref[...]
        loss_ref[...] += jnp.broadcast_to(_rowsum(jnp.sum(err * err, axis=-1, keepdims=True)), loss_ref.shape)

        dy = err * (1.0 / D_MODEL)
        dfg_ref[...] += _rowsum(dy * n2)
        dh2 = _rms_bwd(dy * fg, n2, r2)
        dpp = (dh2 * gate).astype(BF16)
        dgl = (dh2 * pp * gate * (1.0 - gate)).astype(BF16)
        dwp_acc[...] += lax.dot_general(pb, dpp, (((0,), (0,)), ((), ())), preferred_element_type=F32)
        dwg_acc[...] += lax.dot_general(rn, dgl, (((0,), (0,)), ((), ())), preferred_element_type=F32)
        drn = jnp.dot(dgl, wgt_ref[...], preferred_element_type=F32)
        dpg_ref[...] += _rowsum(drn * n1)
        dh1 = dh2 + _rms_bwd(drn * pg, n1, r1)
        dh_ref[...] = dh1
        dhb = dh1.astype(BF16)
        da_ref[...] = jnp.dot(dhb, wot_ref[:, 0:SGU_WIDTH], preferred_element_type=F32)
        db_ref[...] = jnp.dot(dhb, wot_ref[:, SGU_WIDTH:], preferred_element_type=F32)
        dwo_acc[0:SGU_WIDTH, :] += lax.dot_general(a, dhb, (((0,), (0,)), ((), ())), preferred_element_type=F32)
        dwo_acc[SGU_WIDTH:, :] += lax.dot_general(bb, dhb, (((0,), (0,)), ((), ())), preferred_element_type=F32)

        @pl.when(i == steps - 1)
        def _():
            for j in range(N_DEV):
                for acc, hbm in ((dwo_acc, dwo_hbm), (dwg_acc, dwg_hbm)):
                    rows_stage[...] = acc[j * LANES:(j + 1) * LANES, :].astype(BF16)
                    pltpu.sync_copy(rows_stage, hbm.at[j])
                cols_stage[...] = dwp_acc[:, j * LANES:(j + 1) * LANES].astype(BF16)
                pltpu.sync_copy(cols_stage, dwp_hbm.at[j])

    tile = lambda w: pl.BlockSpec((tm, w), lambda i: (i, 0))
    return pl.pallas_call(
        body, name="head_fwd_bwd", grid=(steps,),
        out_shape=(jax.ShapeDtypeStruct((t, SGU_WIDTH), F32), jax.ShapeDtypeStruct((t, DN_WIDTH), F32),
                   jax.ShapeDtypeStruct((t, D_MODEL), F32), jax.ShapeDtypeStruct((N_DEV, LANES, D_MODEL), BF16),
                   jax.ShapeDtypeStruct((N_DEV, LANES, D_MODEL), BF16), jax.ShapeDtypeStruct((N_DEV, PLE_DIM, LANES), BF16),
                   jax.ShapeDtypeStruct((1, D_MODEL), F32), jax.ShapeDtypeStruct((1, D_MODEL), F32),
                   jax.ShapeDtypeStruct((8, LANES), F32)),
        in_specs=[tile(SGU_WIDTH), tile(DN_WIDTH), tile(D_MODEL), tile(PLE_DIM), tile(D_MODEL),
                  VMEM_SPEC, VMEM_SPEC, VMEM_SPEC, VMEM_SPEC, VMEM_SPEC, _whole((1, D_MODEL)), _whole((1, D_MODEL))],
        out_specs=(tile(SGU_WIDTH), tile(DN_WIDTH), tile(D_MODEL), HBM_SPEC, HBM_SPEC, HBM_SPEC,
                   _whole((1, D_MODEL)), _whole((1, D_MODEL)), _whole((8, LANES))),
        scratch_shapes=[pltpu.VMEM((D_MODEL, D_MODEL), F32), pltpu.VMEM((D_MODEL, D_MODEL), F32),
                        pltpu.VMEM((PLE_DIM, D_MODEL), F32), pltpu.VMEM((LANES, D_MODEL), BF16),
                        pltpu.VMEM((PLE_DIM, LANES), BF16)],
        compiler_params=_params(1),
    )(a_out, b_out, x2, p2, target, w_out, w_out_t, w_gate, w_gate_t, w_proj, ple_g, fin_g)


def _inproj_bwd(x2, dh1, a_uvz, d_sgu, d_q, d_z, d_l, norm_g, sgu_weights, wat, wqt, wzt, wgt):
    t = x2.shape[0]
    tm = min(256, t)
    steps = t // tm

    widths = (a_uvz.shape[1], d_q.shape[1], d_z.shape[1], d_l.shape[1])
    starts = (0, widths[0], widths[0] + widths[1], widths[0] + widths[1] + widths[2])

    def body(x_ref, dh_ref, uvz_ref, dsgu_ref, dq_ref, dz_ref, dl_ref, g_ref, lg_ref, lb_ref, ws_ref, bt_ref,
             wat_ref, wqt_ref, wzt_ref, wgt_ref,
             dx_ref, dw_hbm, dg_ref, dlg_ref, dlb_ref, dws_ref, dbt_ref, dw_acc, stage_ref, da_ref):
        i = pl.program_id(0)

        @pl.when(i == 0)
        def _():
            dw_acc[...] = jnp.zeros_like(dw_acc)
            for ref in (dg_ref, dlg_ref, dlb_ref, dws_ref, dbt_ref):
                ref[...] = jnp.zeros_like(ref)

        _sgu_bwd_tile(uvz_ref, dsgu_ref, (lg_ref, lb_ref, ws_ref, bt_ref), da_ref, (dlg_ref, dlb_ref, dws_ref, dbt_ref))
        g = g_ref[...]
        n, r = _rms(x_ref[...])
        xn = (n * g).astype(BF16)
        dxn = None
        for d_ref, wt_ref, col0 in zip((da_ref, dq_ref, dz_ref, dl_ref), (wat_ref, wqt_ref, wzt_ref, wgt_ref), starts):
            term = jnp.dot(d_ref[...], wt_ref[...], preferred_element_type=F32)
            dxn = term if dxn is None else dxn + term
            width = d_ref.shape[1]
            for c0 in range(0, width, 512):
                c1 = min(c0 + 512, width)
                dw_acc[col0 + c0:col0 + c1, :] += lax.dot_general(d_ref[:, c0:c1], xn, (((0,), (0,)), ((), ())),
                                                                  preferred_element_type=F32)
        dg_ref[...] += _rowsum(dxn * n)
        dx_ref[...] = dh_ref[...] + _rms_bwd(dxn * g, n, r)

        @pl.when(i == steps - 1)
        def _():
            for j in range(N_DEV):
                stage_ref[...] = dw_acc[j * IN_SHARD:(j + 1) * IN_SHARD, :]
                pltpu.sync_copy(stage_ref, dw_hbm.at[j])

    tile = lambda w: pl.BlockSpec((tm, w), lambda i: (i, 0))
    sgu_shapes = ((1, SGU_WIDTH), (1, SGU_WIDTH), (SGU_GROUPS, SGU_CHUNK, SGU_CHUNK), (SGU_CHUNK, SGU_GROUPS))
    return pl.pallas_call(
        body, name="inproj_sgu_bwd", grid=(steps,),
        out_shape=(jax.ShapeDtypeStruct((t, D_MODEL), F32), jax.ShapeDtypeStruct((N_DEV, IN_SHARD, D_MODEL), F32),
                   jax.ShapeDtypeStruct((1, D_MODEL), F32)) + tuple(jax.ShapeDtypeStruct(s, F32) for s in sgu_shapes),
        in_specs=[tile(D_MODEL), tile(D_MODEL), tile(widths[0]), tile(SGU_WIDTH)] + [tile(w) for w in widths[1:]]
        + [_whole((1, D_MODEL))] + [_whole(s) for s in sgu_shapes] + [VMEM_SPEC] * 4,
        out_specs=(tile(D_MODEL), HBM_SPEC, _whole((1, D_MODEL))) + tuple(_whole(s) for s in sgu_shapes),
        scratch_shapes=[pltpu.VMEM((sum(widths), D_MODEL), F32), pltpu.VMEM((IN_SHARD, D_MODEL), F32),
                        pltpu.VMEM((tm, widths[0]), BF16)],
        compiler_params=_params(1),
    )(x2, dh1, a_uvz, d_sgu, d_q, d_z, d_l, norm_g, *sgu_weights, wat, wqt, wzt, wgt)


def _reduce_adamw(recv, w, m, v, name, col_block=None):
    n, rows, cols = recv.shape
    cb = col_block or cols
    lead = w.ndim - 2

    def body(r_ref, w_ref, m_ref, v_ref, g_ref, d_ref, nm_ref, nv_ref):
        g = r_ref[0].astype(F32)
        for i in range(1, n):
            g = g + r_ref[i].astype(F32)
        m_new = ADAM_B1 * m_ref[...] + (1.0 - ADAM_B1) * g
        v_new = ADAM_B2 * v_ref[...] + (1.0 - ADAM_B2) * jnp.square(g)
        m_hat = m_new / (1.0 - ADAM_B1 ** ADAM_STEP)
        v_hat = v_new / (1.0 - ADAM_B2 ** ADAM_STEP)
        g_ref[...] = g
        d_ref[...] = -ADAM_LR * (m_hat / (jnp.sqrt(v_hat) + ADAM_EPS) + ADAM_WD * w_ref[...])
        nm_ref[...] = m_new
        nv_ref[...] = v_new

    blk = pl.BlockSpec((None,) * lead + (rows, cb), lambda i: (0,) * lead + (0, i))
    return pl.pallas_call(
        body, name=name, grid=(cols // cb,),
        out_shape=tuple(jax.ShapeDtypeStruct(w.shape, F32) for _ in range(4)),
        in_specs=[pl.BlockSpec((n, rows, cb), lambda i: (0, 0, i)), blk, blk, blk],
        out_specs=(blk, blk, blk, blk),
        compiler_params=_params(1),
    )(recv, w, m, v)


def _adamw_replicated(received, ws, ms, vs):
    nw = len(ws)
    starts = [sum(SMALL_PIECE_ROWS[:i]) for i in range(len(SMALL_PIECE_ROWS))]

    def natural(g_ref, row0, shape):
        cols, rows = shape[-1], _size(shape[:-1])
        if cols == LANES:
            return g_ref[row0:row0 + rows, :].reshape(shape)
        if cols < LANES:
            return g_ref[row0:row0 + 1, 0:cols].reshape(shape)
        per = cols // LANES
        return jnp.concatenate(
            [jnp.concatenate([g_ref[row0 + r * per + k:row0 + r * per + k + 1, :] for k in range(per)], axis=1)
             for r in range(rows)], axis=0).reshape(shape)

    def body(r_ref, *refs):
        w_refs, m_refs, v_refs = refs[:nw], refs[nw:2 * nw], refs[2 * nw:3 * nw]
        conv_ref, loss_ref = refs[3 * nw], refs[3 * nw + 1]
        out_refs, g_ref = refs[3 * nw + 2:-1], refs[-1]
        g = r_ref[0]
        for q in range(1, N_CHIPS):
            g = g + r_ref[q]
        g_ref[...] = g
        conv_ref[...] = natural(g_ref, starts[0], (CONV_K, 3 * DN_WIDTH))
        loss_ref[...] = natural(g_ref, starts[-1], (1, 1))
        for i in range(nw):
            gi = natural(g_ref, starts[1 + i], w_refs[i].shape)
            m_new = ADAM_B1 * m_refs[i][...] + (1.0 - ADAM_B1) * gi
            v_new = ADAM_B2 * v_refs[i][...] + (1.0 - ADAM_B2) * jnp.square(gi)
            m_hat = m_new / (1.0 - ADAM_B1 ** ADAM_STEP)
            v_hat = v_new / (1.0 - ADAM_B2 ** ADAM_STEP)
            out_refs[4 * i][...] = gi
            out_refs[4 * i + 1][...] = -ADAM_LR * (m_hat / (jnp.sqrt(v_hat) + ADAM_EPS) + ADAM_WD * w_refs[i][...])
            out_refs[4 * i + 2][...] = m_new
            out_refs[4 * i + 3][...] = v_new

    def spec(a):
        lead = max(a.ndim - 3, 0)
        return pl.BlockSpec((None,) * lead + a.shape[lead:], lambda: (0,) * a.ndim)

    weight_specs = [spec(a) for a in ws]
    return pl.pallas_call(
        body, name="adamw_replicated",
        out_shape=(jax.ShapeDtypeStruct((CONV_K, 3 * DN_WIDTH), F32), jax.ShapeDtypeStruct((1, 1), F32))
        + tuple(jax.ShapeDtypeStruct(a.shape, F32) for a in ws for _ in range(4)),
        in_specs=[pl.BlockSpec(received.shape, lambda: (0, 0, 0))] + weight_specs * 3,
        out_specs=(pl.BlockSpec((CONV_K, 3 * DN_WIDTH), lambda: (0, 0)), pl.BlockSpec((1, 1), lambda: (0, 0)))
        + tuple(s for s in weight_specs for _ in range(4)),
        scratch_shapes=[pltpu.VMEM(received.shape[1:], F32)],
        compiler_params=pltpu.CompilerParams(vmem_limit_bytes=VMEM_LIMIT),
    )(received, *ws, *ms, *vs)


def _pack_rows(pieces, rows):
    padded = [jnp.pad(jnp.ravel(p), (0, -p.size % LANES)) for p in pieces]
    flat = jnp.concatenate(padded)
    return jnp.pad(flat, (0, rows * LANES - flat.shape[0])).reshape(rows, LANES)


def kernel(x, p, norm_g, w_in, sgu_ln_g, sgu_ln_b, sgu_w_s, sgu_b_s, dn_conv_w, dn_a_log, dn_dt_bias, dn_o_norm_g, w_out, ple_norm_g, ple_gate_w, ple_proj_w, final_norm_g, loss_target, m_norm_g, m_w_in, m_sgu_ln_g, m_sgu_ln_b, m_sgu_w_s, m_sgu_b_s, m_dn_conv_w, m_dn_a_log, m_dn_dt_bias, m_dn_o_norm_g, m_w_out, m_ple_norm_g, m_ple_gate_w, m_ple_proj_w, m_final_norm_g, v_norm_g, v_w_in, v_sgu_ln_g, v_sgu_ln_b, v_sgu_w_s, v_sgu_b_s, v_dn_conv_w, v_dn_a_log, v_dn_dt_bias, v_dn_o_norm_g, v_w_out, v_ple_norm_g, v_ple_gate_w, v_ple_proj_w, v_final_norm_g):
    weights = dict(norm_g=norm_g, w_in=w_in, sgu_ln_g=sgu_ln_g, sgu_ln_b=sgu_ln_b, sgu_w_s=sgu_w_s, sgu_b_s=sgu_b_s,
                   dn_conv_w=dn_conv_w, dn_a_log=dn_a_log, dn_dt_bias=dn_dt_bias, dn_o_norm_g=dn_o_norm_g, w_out=w_out,
                   ple_norm_g=ple_norm_g, ple_gate_w=ple_gate_w, ple_proj_w=ple_proj_w, final_norm_g=final_norm_g)
    mom1 = dict(norm_g=m_norm_g, w_in=m_w_in, sgu_ln_g=m_sgu_ln_g, sgu_ln_b=m_sgu_ln_b, sgu_w_s=m_sgu_w_s,
                sgu_b_s=m_sgu_b_s, dn_conv_w=m_dn_conv_w, dn_a_log=m_dn_a_log, dn_dt_bias=m_dn_dt_bias,
                dn_o_norm_g=m_dn_o_norm_g, w_out=m_w_out, ple_norm_g=m_ple_norm_g, ple_gate_w=m_ple_gate_w,
                ple_proj_w=m_ple_proj_w, final_norm_g=m_final_norm_g)
    mom2 = dict(norm_g=v_norm_g, w_in=v_w_in, sgu_ln_g=v_sgu_ln_g, sgu_ln_b=v_sgu_ln_b, sgu_w_s=v_sgu_w_s,
                sgu_b_s=v_sgu_b_s, dn_conv_w=v_dn_conv_w, dn_a_log=v_dn_a_log, dn_dt_bias=v_dn_dt_bias,
                dn_o_norm_g=v_dn_o_norm_g, w_out=v_w_out, ple_norm_g=v_ple_norm_g, ple_gate_w=v_ple_gate_w,
                ple_proj_w=v_ple_proj_w, final_norm_g=v_final_norm_g)
    nb, s, _ = x.shape
    t = nb * s

    transposed = lambda a: jnp.transpose(a, (2, 0, 1)).reshape(IN_SHARD, D_MODEL)
    w_in_t, m_in_t, v_in_t = transposed(w_in), transposed(m_w_in), transposed(v_w_in)
    w_in_blocks, conv_blocks = _all_gather([w_in_t.astype(BF16), dn_conv_w[0]])
    w_in_full_t = w_in_blocks.reshape(IN_COLS, D_MODEL)
    wat = w_in_full_t[:3 * SGU_WIDTH]
    wqt = w_in_full_t[3 * SGU_WIDTH:3 * SGU_WIDTH + 3 * DN_WIDTH]
    wzt = w_in_full_t[3 * SGU_WIDTH + 3 * DN_WIDTH:3 * SGU_WIDTH + 4 * DN_WIDTH]
    wgt = jnp.pad(w_in_full_t[3 * SGU_WIDTH + 4 * DN_WIDTH:], ((0, GATE_PAD - 2 * DN_HEADS), (0, 0)))
    conv_full = jnp.moveaxis(conv_blocks, 0, 1).reshape(CONV_K, 3 * DN_WIDTH)
    later_shards = [w_out[0].astype(BF16), ple_gate_w[0].astype(BF16), ple_proj_w[0].astype(BF16)]

    pad_row = lambda a: jnp.pad(a.reshape(1, -1), ((0, 0), (DN_HEADS, GATE_PAD - DN_HEADS - a.size)))
    alog, dtb = pad_row(dn_a_log), pad_row(dn_dt_bias)
    og = dn_o_norm_g.reshape(1, DN_HEAD_DIM)
    ws = sgu_w_s.reshape(SGU_GROUPS, SGU_CHUNK, SGU_CHUNK)
    b_t = sgu_b_s.reshape(SGU_GROUPS, SGU_CHUNK).T
    fin_g = final_norm_g.reshape(1, D_MODEL)

    x2 = x.reshape(t, D_MODEL)
    sgu_weights = (sgu_ln_g, sgu_ln_b, ws, b_t)
    a_uvz, b_qkv, b_z, b_l, a_out, conv_out, w_out_blocks, w_gate_blocks, w_proj_blocks = _inproj_fwd(
        x2, s, norm_g, wat, wqt, wzt, wgt, sgu_weights, conv_full, later_shards)
    w_out_full = w_out_blocks.reshape(D_MODEL, D_MODEL)
    w_gate_full = w_gate_blocks.reshape(D_MODEL, D_MODEL)
    w_proj_full = jnp.moveaxis(w_proj_blocks, 0, 1).reshape(PLE_DIM, D_MODEL)
    qkv3 = b_qkv.reshape(nb, s, 3 * DN_WIDTH)
    conv_out = conv_out.reshape(nb, s, 3 * DN_WIDTH)
    z3 = b_z.reshape(nb, s, DN_WIDTH)
    l3 = b_l.reshape(nb, s, GATE_PAD)
    b_out, states, inverses = _dn_fwd(conv_out, z3, l3, alog, dtb, og)

    d_a, d_b, dh1, g_w_out, g_gate, g_proj, g_ple_g, g_fin_g, loss_tile = _head(
        a_out, b_out.reshape(t, DN_WIDTH), x2, p.reshape(t, PLE_DIM), loss_target.reshape(t, D_MODEL),
        w_out_full, w_out_full.T, w_gate_full, w_gate_full.T, w_proj_full, ple_norm_g, fin_g)
    d_qkv, d_z, d_l, g_conv, g_alog, g_dtb, g_og, *head_received = _dn_bwd(
        qkv3, conv_out, z3, l3, conv_full, alog, dtb, og, states, inverses, d_b.reshape(nb, s, DN_WIDTH),
        [g_w_out, g_gate, g_proj])
    grad_x, g_w_in, g_norm, g_ln_g, g_ln_b, g_ws, g_bt = _inproj_bwd(
        x2, dh1, a_uvz, d_a, d_qkv.reshape(t, 3 * DN_WIDTH), d_z.reshape(t, DN_WIDTH), d_l.reshape(t, GATE_PAD),
        norm_g, sgu_weights, wat, wqt, wzt, wgt)

    small = _pack_rows([g_conv, g_norm, g_ln_g, g_ln_b, g_ws, g_bt.T, g_alog[:, DN_HEADS:2 * DN_HEADS], g_dtb[:, DN_HEADS:2 * DN_HEADS], g_og,
                        g_ple_g, g_fin_g, (0.5 / D_MODEL) * loss_tile[0:1, 0:1]], SMALL_ROWS)
    w_in_received, small_received = _reduce_exchange(g_w_in, small)

    results = {}
    outs = _reduce_adamw(w_in_received, w_in_t, m_in_t, v_in_t, "adamw_w_in", LANES)
    results["w_in"] = [jnp.transpose(a.reshape(IN_SHARD, 1, D_MODEL), (1, 2, 0)) for a in outs]
    for name, recv in zip(("w_out", "ple_gate_w", "ple_proj_w"), head_received):
        results[name] = _reduce_adamw(recv, weights[name], mom1[name], mom2[name], "adamw_" + name)
    names = [name for name, _ in REPLICATED]
    two_d = lambda a: a.reshape(1, -1) if a.ndim == 1 else a
    g_conv_sum, loss_sum, *flat_outs = _adamw_replicated(
        small_received, *[[two_d(src[k]) for k in names] for src in (weights, mom1, mom2)])
    for i, k in enumerate(names):
        results[k] = [a.reshape(weights[k].shape) for a in flat_outs[4 * i:4 * i + 4]]
    loss = loss_sum[0, 0]
    me = 4 * lax.axis_index("x") + 2 * lax.axis_index("y") + lax.axis_index("c")
    conv_mine = lax.dynamic_slice(g_conv_sum, (0, me * 192), (CONV_K, 192))
    results["dn_conv_w"] = _reduce_adamw(conv_mine[None], dn_conv_w, m_dn_conv_w, v_dn_conv_w, "adamw_dn_conv_w")

    return (loss, grad_x.reshape(nb, s, D_MODEL), *[results[k][0] for k in WEIGHT_ORDER],
            *[results[k][1] for k in WEIGHT_ORDER], *[results[k][2] for k in WEIGHT_ORDER],
            *[results[k][3] for k in WEIGHT_ORDER])
```

```python
import jax
import jax.numpy as jnp
from jax import lax
from jax.experimental import pallas as pl
from jax.experimental.pallas import tpu as pltpu

F32 = jnp.float32
BF16 = jnp.bfloat16

N_DEV = 8
D_MODEL = 1024
SGU_WIDTH = 512
SGU_GROUPS = 4
SGU_CHUNK = 128
DN_WIDTH = 512
DN_HEADS = 4
DN_HEAD_DIM = 128
DN_CHUNK = 128
CONV_K = 4
CONV_HALO = 8
PLE_DIM = 256
EPS = 1e-6
IN_COLS = 3592
IN_SHARD = IN_COLS // N_DEV
GATE_PAD = 128

ADAM_LR = 0.001
ADAM_B1 = 0.9
ADAM_B2 = 0.999
ADAM_EPS = 1e-08
ADAM_WD = 0.01
ADAM_STEP = 10

LANES = 128
VMEM_LIMIT = 56 * 1024 * 1024
MESH = pl.DeviceIdType.MESH

REPLICATED = (("norm_g", (1, D_MODEL)), ("sgu_ln_g", (1, SGU_WIDTH)), ("sgu_ln_b", (1, SGU_WIDTH)),
              ("sgu_w_s", (1, SGU_GROUPS, SGU_CHUNK, SGU_CHUNK)), ("sgu_b_s", (1, SGU_GROUPS, SGU_CHUNK)),
              ("dn_a_log", (1, DN_HEADS)), ("dn_dt_bias", (1, DN_HEADS)), ("dn_o_norm_g", (1, DN_HEAD_DIM)),
              ("ple_norm_g", (1, D_MODEL)), ("final_norm_g", (D_MODEL,)))
WEIGHT_ORDER = ("norm_g", "w_in", "sgu_ln_g", "sgu_ln_b", "sgu_w_s", "sgu_b_s", "dn_conv_w", "dn_a_log",
                "dn_dt_bias", "dn_o_norm_g", "w_out", "ple_norm_g", "ple_gate_w", "ple_proj_w", "final_norm_g")


def _size(shape):
    n = 1
    for s in shape:
        n *= s
    return n


SMALL_LAYOUT = (("conv", (CONV_K, 3 * DN_WIDTH)),) + REPLICATED + (("loss", (1,)),)
SMALL_PIECE_ROWS = tuple(-(-_size(s) // LANES) for _, s in SMALL_LAYOUT)
SMALL_ROWS = -(-sum(SMALL_PIECE_ROWS) // 8) * 8


def _bdot(a, b):
    return jnp.dot(a.astype(BF16), b.astype(BF16), preferred_element_type=F32)


def _sigmoid(x):
    return 0.5 * jnp.tanh(0.5 * x) + 0.5


@jax.custom_vjp
def _silu(x):
    return x * _sigmoid(x)


def _silu_fwd(x):
    s = _sigmoid(x)
    return x * s, (x, s)


def _silu_bwd(res, ct):
    x, s = res
    return (ct * (s * (1.0 + x * (1.0 - s))),)


_silu.defvjp(_silu_fwd, _silu_bwd)


def _normal_cdf(x):
    return 0.5 + 0.5 * lax.erf(x * (0.5 ** 0.5))


@jax.custom_vjp
def _gelu(x):
    return x * _normal_cdf(x)


def _gelu_fwd(x):
    cdf = _normal_cdf(x)
    return x * cdf, (x, cdf)


def _gelu_bwd(res, ct):
    x, cdf = res
    pdf = jnp.exp(-0.5 * x * x) * ((2.0 * jnp.pi) ** -0.5)
    return (ct * (cdf + x * pdf),)


_gelu.defvjp(_gelu_fwd, _gelu_bwd)


def _softplus(x):
    return jnp.maximum(x, 0.0) + jnp.log1p(jnp.exp(-jnp.abs(x)))


@jax.custom_vjp
def _l2n(x):
    return x * lax.rsqrt(jnp.sum(x * x, axis=-1, keepdims=True) + EPS)


def _l2n_fwd(x):
    r = lax.rsqrt(jnp.sum(x * x, axis=-1, keepdims=True) + EPS)
    n = x * r
    return n, (n, r)


def _l2n_bwd(res, ct):
    n, r = res
    return (r * (ct - n * jnp.sum(ct * n, axis=-1, keepdims=True)),)


_l2n.defvjp(_l2n_fwd, _l2n_bwd)


def _rms(x):
    r = lax.rsqrt(jnp.mean(x * x, axis=-1, keepdims=True) + EPS)
    return x * r, r


def _rms_bwd(dn, n, r):
    return r * (dn - n * jnp.mean(dn * n, axis=-1, keepdims=True))


@jax.custom_vjp
def _rms_normed(x):
    return _rms(x)[0]


def _rms_normed_fwd(x):
    n, r = _rms(x)
    return n, (n, r)


def _rms_normed_bwd(res, ct):
    return (_rms_bwd(ct, *res),)


_rms_normed.defvjp(_rms_normed_fwd, _rms_normed_bwd)


def _onehot_row(idx, width):
    return (lax.broadcasted_iota(jnp.int32, (1, width), 1) == idx).astype(F32)


def _rowsum(x):
    return jnp.sum(x, axis=0, keepdims=True)


def _iota2(n):
    return lax.broadcasted_iota(jnp.int32, (n, n), 0), lax.broadcasted_iota(jnp.int32, (n, n), 1)


def _bmm(a, b):
    return lax.dot_general(a.astype(BF16), b.astype(BF16), (((2,), (1,)), ((0,), (0,))), preferred_element_type=F32)


def _bmm_nt(a, b):
    return lax.dot_general(a.astype(BF16), b.astype(BF16), (((2,), (2,)), ((0,), (0,))), preferred_element_type=F32)


def _bmm_tn(a, b):
    return lax.dot_general(a.astype(BF16), b.astype(BF16), (((1,), (1,)), ((0,), (0,))), preferred_element_type=F32)


def _tri_inv_impl(a):
    n = a.shape[-1]
    r, c = _iota2(n)
    x = r ^ c
    eye = (r == c).astype(F32)
    ad = jnp.where(x < 16, a, 0.0)
    p2 = _bmm(ad, ad)
    e = p2 - ad - _bmm(ad, p2)
    p4 = _bmm(p2, p2)
    e = e + p4 + _bmm(e, p4)
    p8 = _bmm(p4, p4)
    e = e + p8 + _bmm(e, p8)
    size = 16
    while size < n:
        m = jnp.where(jnp.logical_and(x < 2 * size, x >= size), a, 0.0)
        f = m + _bmm(m, e)
        e = e - f - _bmm(e, f)
        size *= 2
    return e + eye


@jax.custom_vjp
def _tri_inv(a, known):
    return _tri_inv_impl(a) if known is None else known


def _tri_inv_fwd(a, known):
    t = _tri_inv(a, known)
    return t, (t, known)


def _tri_inv_bwd(res, dt):
    t, known = res
    return -_bmm_tn(t, _bmm_nt(dt, t)), None if known is None else jnp.zeros_like(known)


_tri_inv.defvjp(_tri_inv_fwd, _tri_inv_bwd)


@jax.custom_vjp
def _standardized(x):
    xc = x - jnp.mean(x, axis=-1, keepdims=True)
    return xc * lax.rsqrt(jnp.mean(xc * xc, axis=-1, keepdims=True) + EPS)


def _standardized_fwd(x):
    xc = x - jnp.mean(x, axis=-1, keepdims=True)
    rstd = lax.rsqrt(jnp.mean(xc * xc, axis=-1, keepdims=True) + EPS)
    y = xc * rstd
    return y, (y, rstd)


def _standardized_bwd(res, ct):
    y, rstd = res
    return (rstd * (ct - jnp.mean(ct, axis=-1, keepdims=True) - y * jnp.mean(ct * y, axis=-1, keepdims=True)),)


_standardized.defvjp(_standardized_fwd, _standardized_bwd)


def _sgu_core(u, v, z, lg, lb, ws, bcol):
    n = ws.shape[0]
    r, c = _iota2(n)
    wm = jnp.where(r >= c, ws, 0.0)
    gu = _gelu(u)
    gv = _gelu(v)
    ln = _standardized(gv) * lg + lb
    s = _bdot(wm, ln) + bcol
    return gu * s * _silu(z)


def _lanes_of(x):
    return jnp.concatenate([x[i] for i in range(x.shape[0])], axis=1)


def _batch_of(x, width):
    return jnp.concatenate([x[None, :, i * width:(i + 1) * width] for i in range(x.shape[1] // width)], axis=0)


def _mask_dot(mask, x):
    hi = x.astype(BF16)
    lo = (x - hi.astype(F32)).astype(BF16)
    m = mask.astype(BF16)
    return jnp.dot(m, hi, preferred_element_type=F32) + jnp.dot(m, lo, preferred_element_type=F32)


def _tri_mask(n, upper):
    r, c = _iota2(n)
    return (r <= c) if upper else (r >= c)


@jax.custom_vjp
def _cumsum_rows(x):
    return _mask_dot(_tri_mask(x.shape[0], False), x)


def _cumsum_rows_fwd(x):
    return _cumsum_rows(x), None


def _cumsum_rows_bwd(_, ct):
    return (_mask_dot(_tri_mask(ct.shape[0], True), ct),)


_cumsum_rows.defvjp(_cumsum_rows_fwd, _cumsum_rows_bwd)


@jax.custom_vjp
def _colsum_all_rows(x):
    return _mask_dot(jnp.ones((x.shape[0], x.shape[0]), jnp.bool_), x)


def _colsum_all_rows_fwd(x):
    return _colsum_all_rows(x), None


def _colsum_all_rows_bwd(_, ct):
    return (_mask_dot(jnp.ones((ct.shape[0], ct.shape[0]), jnp.bool_), ct),)


_colsum_all_rows.defvjp(_colsum_all_rows_fwd, _colsum_all_rows_bwd)


def _dn_core(cq, ck, cv, z, logits, state, alog, dtb, og, t_known=None):
    gn, cn, dh = cq.shape
    heads = gn // logits.shape[0]
    q = _l2n(_silu(cq)) * (dh ** -0.5)
    k = _l2n(_silu(ck))
    v = _silu(cv)
    beta_lanes = _sigmoid(logits)
    g_lanes = -jnp.exp(alog) * _softplus(logits + dtb)
    column = lambda rows, lane: jnp.sum(rows * _onehot_row(lane, rows.shape[-1]), axis=-1, keepdims=True)[None]
    beta = jnp.concatenate([column(beta_lanes[i // heads], i % heads) for i in range(gn)], axis=0)
    g = jnp.concatenate([column(g_lanes[i // heads], heads + i % heads) for i in range(gn)], axis=0)
    r, c = _iota2(cn)
    tril = r >= c
    rw = lax.broadcasted_iota(jnp.int32, (cn, dh), 0)
    cw = lax.broadcasted_iota(jnp.int32, (cn, dh), 1)
    upper_wide = (rw <= cw).astype(F32)
    g_wide = jnp.broadcast_to(g, (gn, cn, dh))
    gc_wide = _batch_of(_cumsum_rows(_lanes_of(g_wide)), dh)
    gc_cols = _batch_of(_colsum_all_rows(_lanes_of(g_wide * upper_wide)), dh)[:, :, :cn]
    decay = jnp.exp(jnp.where(tril, gc_wide[:, :, :cn] - gc_cols, -1e30))
    kb = k * beta
    kk = _bmm_nt(kb, k) * decay
    t = _tri_inv(jnp.where(r > c, kk, 0.0), t_known)
    eg = jnp.exp(gc_wide)
    sol = _bmm(t, jnp.concatenate([v * beta, kb * eg], axis=-1))
    u_val, w_dec = sol[:, :, :dh], sol[:, :, dh:]
    qk = _bmm_nt(q, k) * decay
    g_last = jnp.sum(g_wide, axis=1, keepdims=True)
    k_dec = k * jnp.exp(g_last - gc_wide)
    ws = _bmm(jnp.concatenate([w_dec, q * eg], axis=1), state)
    v_new = u_val - ws[:, :cn]
    o = ws[:, cn:] + _bmm(qk, v_new)
    new_state = state * jnp.exp(g_last) + _bmm_tn(k_dec, v_new)
    return _rms_normed(o) * og * _silu(z), new_state, t


N_CHIPS = 4
HBM_SPEC = pl.BlockSpec(memory_space=pl.ANY)


def _place():
    return lax.axis_index("x"), lax.axis_index("y"), lax.axis_index("c")


def _other_chip(k):
    x, y, _ = _place()
    px = 1 - x if k & 2 else x
    py = 1 - y if k & 1 else y
    return px, py, 2 * px + py


def _remote(src, dst, send_sem, recv_sem, device):
    return pltpu.make_async_remote_copy(src_ref=src, dst_ref=dst, send_sem=send_sem, recv_sem=recv_sem,
                                        device_id=device, device_id_type=MESH)


def _other_device(k):
    x, y, c = _place()
    px = 1 - x if k & 4 else x
    py = 1 - y if k & 2 else y
    pc = 1 - c if k & 1 else c
    return (px, py, pc), 4 * px + 2 * py + pc


def _direct_exchange(srcs, outs, send_sems, recv_sems, local_sems, gather):
    x, y, c = _place()
    me = 4 * x + 2 * y + c

    def copies(arriving):
        out_list = []
        for a, (src, out) in enumerate(zip(srcs, outs)):
            for k in range(1, N_DEV):
                peer, index = _other_device(k)
                mine = src if gather else src.at[index]
                out_list.append(_remote(mine, out.at[index if arriving else me], send_sems.at[a, k - 1],
                                        recv_sems.at[a, k - 1], peer))
        return out_list

    def local_copies():
        return [pltpu.make_async_copy(src if gather else src.at[me], out.at[me], local_sems.at[a])
                for a, (src, out) in enumerate(zip(srcs, outs))]

    def start():
        for cp in local_copies() + copies(False):
            cp.start()

    def wait():
        for cp in copies(True):
            cp.wait_recv()
        for cp in copies(False):
            cp.wait_send()
        for cp in local_copies():
            cp.wait()

    return start, wait


def _exchange_scratch(n):
    return [pltpu.SemaphoreType.DMA((n, N_DEV - 1)), pltpu.SemaphoreType.DMA((n, N_DEV - 1)), pltpu.SemaphoreType.DMA((n,))]


def _all_gather(shards):
    n = len(shards)

    def body(*refs):
        srcs, outs = refs[:n], refs[n:2 * n]
        send_sems, recv_sems, local_sems = refs[2 * n:]
        x, y, c = _place()
        me = 4 * x + 2 * y + c
        sibling = (x, y, 1 - c)
        local = [pltpu.make_async_copy(srcs[a], outs[a].at[me], local_sems.at[a]) for a in range(n)]
        for cp in local:
            cp.start()
        sends = []
        for a in range(n):
            sends.append(_remote(srcs[a], outs[a].at[me], send_sems.at[a, 0], recv_sems.at[a, 0], sibling))
        for k in range(1, N_CHIPS):
            px, py, _ = _other_chip(k)
            for a in range(n):
                sends.append(_remote(srcs[a], outs[a].at[me], send_sems.at[a, k], recv_sems.at[a, k], (px, py, c)))
        for cp in sends:
            cp.start()
        passed = []
        for k in range(1, N_CHIPS):
            px, py, _ = _other_chip(k)
            blk = 4 * px + 2 * py + c
            for a in range(n):
                _remote(srcs[a], outs[a].at[blk], send_sems.at[a, k], recv_sems.at[a, k], (px, py, c)).wait_recv()
            for a in range(n):
                cp = _remote(outs[a].at[blk], outs[a].at[blk], send_sems.at[a, 3 + k], recv_sems.at[a, 3 + k], sibling)
                cp.start()
                passed.append(cp)
        for a in range(n):
            _remote(srcs[a], outs[a].at[me + 1 - 2 * c], send_sems.at[a, 0], recv_sems.at[a, 0], sibling).wait_recv()
        for k in range(1, N_CHIPS):
            px, py, _ = _other_chip(k)
            blk = 4 * px + 2 * py + 1 - c
            for a in range(n):
                _remote(srcs[a], outs[a].at[blk], send_sems.at[a, 3 + k], recv_sems.at[a, 3 + k], sibling).wait_recv()
        for cp in sends + passed:
            cp.wait_send()
        for cp in local:
            cp.wait()

    return pl.pallas_call(
        body, name="all_gather_weights",
        out_shape=tuple(jax.ShapeDtypeStruct((N_DEV,) + a.shape, a.dtype) for a in shards),
        in_specs=[HBM_SPEC] * n, out_specs=(HBM_SPEC,) * n,
        scratch_shapes=[pltpu.SemaphoreType.DMA((n, N_DEV - 1)), pltpu.SemaphoreType.DMA((n, N_DEV - 1)),
                        pltpu.SemaphoreType.DMA((n,))],
    )(*shards)


def _reduce_exchange(by_device, small):
    _, rows, cols = by_device.shape

    def body(g_ref, small_ref, out_ref, small_out_ref, from_sibling, small_from_sibling, stage, sums, small_own, small_sum,
             pair_send, pair_recv, chip_send, chip_recv, local_sems):
        x, y, c = _place()
        mine = 2 * x + y
        sibling = (x, y, 1 - c)
        chips = [(x, y, mine)] + [_other_chip(k) for k in range(1, N_CHIPS)]
        to_sibling = [_remote(g_ref.at[2 * chips[k][2] + 1 - c], from_sibling.at[k], pair_send.at[k], pair_recv.at[k], sibling)
                      for k in range(N_CHIPS)]
        to_sibling.append(_remote(small_ref, small_from_sibling, pair_send.at[N_CHIPS], pair_recv.at[N_CHIPS], sibling))
        for cp in to_sibling:
            cp.start()
        small_mine = pltpu.make_async_copy(small_ref, small_own, local_sems.at[0])
        small_mine.start()
        to_chips = []
        for k in (1, 2, 3, 0):
            px, py, chip = chips[k]
            mine_k = pltpu.make_async_copy(g_ref.at[2 * chip + c], stage, local_sems.at[1])
            mine_k.start()
            to_sibling[k].wait_recv()
            mine_k.wait()
            sums[k] = (stage[...] + from_sibling[k]).astype(sums.dtype)
            if k:
                cp = _remote(sums.at[k], out_ref.at[mine], chip_send.at[0, k - 1], chip_recv.at[0, k - 1], (px, py, c))
                cp.start()
                to_chips.append(cp)
        own_block = pltpu.make_async_copy(sums.at[0], out_ref.at[mine], local_sems.at[2])
        own_block.start()
        to_sibling[N_CHIPS].wait_recv()
        small_mine.wait()
        small_sum[...] = small_own[...] + small_from_sibling[...]
        for k in range(1, N_CHIPS):
            px, py, _ = chips[k]
            cp = _remote(small_sum, small_out_ref.at[mine], chip_send.at[1, k - 1], chip_recv.at[1, k - 1], (px, py, c))
            cp.start()
            to_chips.append(cp)
        own_small = pltpu.make_async_copy(small_sum, small_out_ref.at[mine], local_sems.at[3])
        own_small.start()
        for k in range(1, N_CHIPS):
            px, py, chip = chips[k]
            _remote(sums.at[k], out_ref.at[chip], chip_send.at[0, k - 1], chip_recv.at[0, k - 1], (px, py, c)).wait_recv()
            _remote(small_sum, small_out_ref.at[chip], chip_send.at[1, k - 1], chip_recv.at[1, k - 1], (px, py, c)).wait_recv()
        for cp in to_sibling + to_chips:
            cp.wait_send()
        own_block.wait()
        own_small.wait()

    return pl.pallas_call(
        body, name="grad_reduce_exchange",
        out_shape=(jax.ShapeDtypeStruct((N_CHIPS, rows, cols), BF16), jax.ShapeDtypeStruct((N_CHIPS,) + small.shape, F32)),
        in_specs=[HBM_SPEC, HBM_SPEC], out_specs=(HBM_SPEC, HBM_SPEC),
        scratch_shapes=[pltpu.VMEM((N_CHIPS, rows, cols), F32), pltpu.VMEM(small.shape, F32), pltpu.VMEM((rows, cols), F32),
                        pltpu.VMEM((N_CHIPS, rows, cols), BF16), pltpu.VMEM(small.shape, F32), pltpu.VMEM(small.shape, F32),
                        pltpu.SemaphoreType.DMA((N_CHIPS + 1,)), pltpu.SemaphoreType.DMA((N_CHIPS + 1,)),
                        pltpu.SemaphoreType.DMA((2, N_CHIPS - 1)), pltpu.SemaphoreType.DMA((2, N_CHIPS - 1)),
                        pltpu.SemaphoreType.DMA((4,))],
        compiler_params=pltpu.CompilerParams(vmem_limit_bytes=VMEM_LIMIT),
    )(by_device, small)


def _params(n_axes):
    return pltpu.CompilerParams(dimension_semantics=("arbitrary",) * n_axes, vmem_limit_bytes=VMEM_LIMIT)


def _whole(shape):
    return pl.BlockSpec(shape, lambda *_: (0,) * len(shape))


VMEM_SPEC = pl.BlockSpec(memory_space=pltpu.VMEM)


def _inproj_fwd(x2, seq_len, norm_g, wat, wqt, wzt, wgt, sgu_weights, conv_w, later_shards):
    t = x2.shape[0]
    tm = min(512, seq_len)
    tiles_per_seq = seq_len // tm
    steps = t // tm
    ns = len(later_shards)

    def body(x_ref, g_ref, wa_ref, wq_ref, wz_ref, wg_ref, lg_ref, lb_ref, ws_ref, bt_ref, cw_ref, *rest):
        shard_refs, rest = rest[:ns], rest[ns:]
        a_ref, q_ref, z_ref, l_ref, sgu_ref, c_ref = rest[:6]
        gathered_refs, (xpad_ref, send_sems, recv_sems, local_sems) = rest[6:6 + ns], rest[6 + ns:]
        start_gather, wait_gather = _direct_exchange(shard_refs, gathered_refs, send_sems, recv_sems, local_sems, True)
        pl.when(pl.program_id(0) == 0)(start_gather)
        n, _ = _rms(x_ref[...])
        xn = (n * g_ref[...]).astype(BF16)
        for w_ref, o_ref in ((wa_ref, a_ref), (wq_ref, q_ref), (wz_ref, z_ref), (wg_ref, l_ref)):
            width = w_ref.shape[0]
            for c0 in range(0, width, 512):
                c1 = min(c0 + 512, width)
                o_ref[:, c0:c1] = lax.dot_general(xn, w_ref[c0:c1, :], (((1,), (1,)), ((), ())),
                                                  preferred_element_type=F32)
        for row0 in range(0, tm, SGU_CHUNK):
            for grp in range(SGU_GROUPS):
                args = _sgu_pieces(a_ref, lg_ref, lb_ref, ws_ref, bt_ref, row0, grp)
                sgu_ref[pl.ds(row0, SGU_CHUNK), pl.ds(grp * 128, 128)] = _sgu_core(*args).astype(sgu_ref.dtype)

        @pl.when(pl.program_id(0) % tiles_per_seq == 0)
        def _():
            xpad_ref[0:CONV_HALO, :] = jnp.zeros((CONV_HALO, xpad_ref.shape[1]), F32)

        xpad_ref[CONV_HALO:, :] = q_ref[...]
        acc = None
        for j in range(CONV_K):
            term = cw_ref[j:j + 1, :] * xpad_ref[pl.ds(CONV_HALO - CONV_K + 1 + j, tm), :]
            acc = term if acc is None else acc + term
        c_ref[...] = acc
        xpad_ref[0:CONV_HALO, :] = xpad_ref[tm:tm + CONV_HALO, :]
        pl.when(pl.program_id(0) == steps - 1)(wait_gather)

    widths = (wat.shape[0], wqt.shape[0], wzt.shape[0], wgt.shape[0])
    tile = lambda w: pl.BlockSpec((tm, w), lambda i: (i, 0))
    sgu_shapes = ((1, SGU_WIDTH), (1, SGU_WIDTH), (SGU_GROUPS, SGU_CHUNK, SGU_CHUNK), (SGU_CHUNK, SGU_GROUPS))
    return pl.pallas_call(
        body, name="inproj_sgu_conv_fwd", grid=(steps,),
        out_shape=tuple(jax.ShapeDtypeStruct((t, w), F32) for w in widths)
        + (jax.ShapeDtypeStruct((t, SGU_WIDTH), BF16), jax.ShapeDtypeStruct((t, widths[1]), F32))
        + tuple(jax.ShapeDtypeStruct((N_DEV,) + a.shape, a.dtype) for a in later_shards),
        in_specs=[tile(D_MODEL), _whole((1, D_MODEL)), VMEM_SPEC, VMEM_SPEC, VMEM_SPEC, VMEM_SPEC]
        + [_whole(s) for s in sgu_shapes] + [_whole((CONV_K, widths[1]))] + [HBM_SPEC] * ns,
        out_specs=tuple(tile(w) for w in widths) + (tile(SGU_WIDTH), tile(widths[1])) + (HBM_SPEC,) * ns,
        scratch_shapes=[pltpu.VMEM((CONV_HALO + tm, widths[1]), F32)] + _exchange_scratch(ns),
        compiler_params=_params(1),
    )(x2, norm_g, wat, wqt, wzt, wgt, *sgu_weights, conv_w, *later_shards)


def _sgu_pieces(uvz_ref, lg_ref, lb_ref, ws_ref, bt_ref, row0, grp):
    rows = pl.ds(row0, SGU_CHUNK)
    lanes = pl.ds(grp * 128, 128)
    u = uvz_ref[rows, pl.ds(grp * 128, 128)]
    v = uvz_ref[rows, pl.ds(SGU_WIDTH + grp * 128, 128)]
    z = uvz_ref[rows, pl.ds(2 * SGU_WIDTH + grp * 128, 128)]
    bcol = jnp.sum(bt_ref[...] * _onehot_row(grp, SGU_GROUPS), axis=-1, keepdims=True)
    return u, v, z, lg_ref[:, lanes], lb_ref[:, lanes], ws_ref[grp], bcol


def _sgu_bwd_tile(uvz_ref, do_ref, sgu_refs, duvz_ref, grad_refs):
    lg_ref, lb_ref, ws_ref, bt_ref = sgu_refs
    dlg_ref, dlb_ref, dws_ref, dbt_ref = grad_refs
    for row0 in range(0, uvz_ref.shape[0], SGU_CHUNK):
        rows = pl.ds(row0, SGU_CHUNK)
        for grp in range(SGU_GROUPS):
            lanes = pl.ds(grp * 128, 128)
            args = _sgu_pieces(uvz_ref, lg_ref, lb_ref, ws_ref, bt_ref, row0, grp)
            _, pull = jax.vjp(_sgu_core, *args)
            du, dv, dz, dlg, dlb, dws, dbcol = pull(do_ref[rows, lanes])
            duvz_ref[rows, pl.ds(grp * 128, 128)] = du.astype(duvz_ref.dtype)
            duvz_ref[rows, pl.ds(SGU_WIDTH + grp * 128, 128)] = dv.astype(duvz_ref.dtype)
            duvz_ref[rows, pl.ds(2 * SGU_WIDTH + grp * 128, 128)] = dz.astype(duvz_ref.dtype)
            dlg_ref[:, lanes] += dlg
            dlb_ref[:, lanes] += dlb
            dws_ref[grp] += dws
            dbt_ref[...] += dbcol * _onehot_row(grp, SGU_GROUPS)


def _dn_pairs(nb):
    return [(b, h) for b in range(nb) for h in range(DN_HEADS)]


def _dn_batch_args(c_ref, z_ref):
    pairs = _dn_pairs(c_ref.shape[0])
    pick = lambda ref, b, col: ref[b, :, pl.ds(col, DN_HEAD_DIM)]
    cq = jnp.stack([pick(c_ref, b, h * DN_HEAD_DIM) for b, h in pairs])
    ck = jnp.stack([pick(c_ref, b, DN_WIDTH + h * DN_HEAD_DIM) for b, h in pairs])
    cv = jnp.stack([pick(c_ref, b, 2 * DN_WIDTH + h * DN_HEAD_DIM) for b, h in pairs])
    z = jnp.stack([pick(z_ref, b, h * DN_HEAD_DIM) for b, h in pairs])
    return cq, ck, cv, z


def _dn_weight_specs():
    return [_whole((CONV_K, 3 * DN_WIDTH)), _whole((1, GATE_PAD)), _whole((1, GATE_PAD)), _whole((1, DN_HEAD_DIM))]


def _dn_fwd(conv_out, zg, logits, alog, dtb, og):
    nb, s, _ = conv_out.shape
    nc = s // DN_CHUNK
    pairs = _dn_pairs(nb)
    gn = len(pairs)
    chunk = lambda w: pl.BlockSpec((nb, DN_CHUNK, w), lambda n: (0, n, 0))

    def body(c_ref, z_ref, l_ref, alog_ref, dtb_ref, og_ref, out_ref, st_ref, inv_ref, state_ref):
        n = pl.program_id(0)

        @pl.when(n == 0)
        def _():
            state_ref[...] = jnp.zeros_like(state_ref)

        cq, ck, cv, z = _dn_batch_args(c_ref, z_ref)
        state = state_ref[...]
        st_ref[...] = state
        out, new_state, t = _dn_core(cq, ck, cv, z, l_ref[...], state, alog_ref[...], dtb_ref[...], og_ref[...])
        state_ref[...] = new_state
        inv_ref[...] = t.astype(inv_ref.dtype)
        for i, (b, h) in enumerate(pairs):
            out_ref[b, :, pl.ds(h * DN_HEAD_DIM, DN_HEAD_DIM)] = out[i].astype(out_ref.dtype)

    per_chunk = pl.BlockSpec((None, gn, DN_HEAD_DIM, DN_HEAD_DIM), lambda n: (n, 0, 0, 0))
    return pl.pallas_call(
        body, name="deltanet_fwd", grid=(nc,),
        out_shape=(jax.ShapeDtypeStruct((nb, s, DN_WIDTH), BF16),
                   jax.ShapeDtypeStruct((nc, gn, DN_HEAD_DIM, DN_HEAD_DIM), F32),
                   jax.ShapeDtypeStruct((nc, gn, DN_CHUNK, DN_CHUNK), BF16)),
        in_specs=[chunk(3 * DN_WIDTH), chunk(DN_WIDTH), chunk(GATE_PAD)] + _dn_weight_specs()[1:],
        out_specs=(chunk(DN_WIDTH), per_chunk, pl.BlockSpec((None, gn, DN_CHUNK, DN_CHUNK), lambda n: (n, 0, 0, 0))),
        scratch_shapes=[pltpu.VMEM((gn, DN_HEAD_DIM, DN_HEAD_DIM), F32)],
        compiler_params=_params(1),
    )(conv_out, zg, logits, alog, dtb, og)


def _dn_bwd(qkv, conv_out, zg, logits, conv_w, alog, dtb, og, states, inverses, d_out, head_grads):
    nb, s, _ = qkv.shape
    nc = s // DN_CHUNK
    rev = lambda n: nc - 1 - n
    pairs = _dn_pairs(nb)
    gn = len(pairs)
    ng = len(head_grads)

    def body(cur_ref, c_ref, z_ref, l_ref, w_ref, alog_ref, dtb_ref, og_ref, st_ref, inv_ref, do_ref, *rest):
        grad_refs, rest = rest[:ng], rest[ng:]
        dqkv_ref, dz_ref, dl_ref, dw_ref, dalog_ref, ddtb_ref, dog_ref = rest[:7]
        recv_refs, (dstate_ref, dcpad_ref, dw_part_ref, send_sems, recv_sems, local_sems) = rest[7:7 + ng], rest[7 + ng:]
        n = pl.program_id(0)
        start_exchange, wait_exchange = _direct_exchange(grad_refs, recv_refs, send_sems, recv_sems, local_sems, False)
        pl.when(n == 0)(start_exchange)

        @pl.when(n == 0)
        def _():
            dw_part_ref[...] = jnp.zeros_like(dw_part_ref)
            dalog_ref[...] = jnp.zeros_like(dalog_ref)
            ddtb_ref[...] = jnp.zeros_like(ddtb_ref)
            dog_ref[...] = jnp.zeros_like(dog_ref)
            dstate_ref[...] = jnp.zeros_like(dstate_ref)
            dcpad_ref[:, DN_CHUNK:, :] = jnp.zeros((nb, CONV_HALO, 3 * DN_WIDTH), F32)

        cq, ck, cv, z = _dn_batch_args(c_ref, z_ref)
        d_out_g = jnp.stack([do_ref[b, :, pl.ds(h * DN_HEAD_DIM, DN_HEAD_DIM)] for b, h in pairs])
        t_known = inv_ref[...].astype(F32)
        core = lambda *args: _dn_core(*args, t_known=t_known)[:2]
        _, pull = jax.vjp(core, cq, ck, cv, z, l_ref[...], st_ref[...], alog_ref[...], dtb_ref[...], og_ref[...])
        dcq, dck, dcv, dz, dlog, dstate, dalog, ddtb, dog = pull((d_out_g, dstate_ref[...]))
        dstate_ref[...] = dstate
        dl_ref[...] = dlog.astype(dl_ref.dtype)
        dalog_ref[...] += dalog
        ddtb_ref[...] += ddtb
        dog_ref[...] += dog
        for i, (b, h) in enumerate(pairs):
            dcpad_ref[b, 0:DN_CHUNK, pl.ds(h * DN_HEAD_DIM, DN_HEAD_DIM)] = dcq[i]
            dcpad_ref[b, 0:DN_CHUNK, pl.ds(DN_WIDTH + h * DN_HEAD_DIM, DN_HEAD_DIM)] = dck[i]
            dcpad_ref[b, 0:DN_CHUNK, pl.ds(2 * DN_WIDTH + h * DN_HEAD_DIM, DN_HEAD_DIM)] = dcv[i]
            dz_ref[b, :, pl.ds(h * DN_HEAD_DIM, DN_HEAD_DIM)] = dz[i].astype(dz_ref.dtype)
        for b in range(nb):
            xb = cur_ref[b]
            dx = None
            for j in range(CONV_K):
                shifted = dcpad_ref[b, pl.ds(CONV_K - 1 - j, DN_CHUNK), :]
                term = w_ref[j:j + 1, :] * shifted
                dx = term if dx is None else dx + term
                dw_part_ref[j] += jnp.sum((shifted * xb).reshape(DN_CHUNK // 8, 8, 3 * DN_WIDTH), axis=0)
            dqkv_ref[b] = dx.astype(dqkv_ref.dtype)
            dcpad_ref[b, DN_CHUNK:, :] = dcpad_ref[b, 0:CONV_HALO, :]

        @pl.when(n == nc - 1)
        def _():
            dw_ref[...] = jnp.sum(dw_part_ref[...], axis=1)

        pl.when(n == nc - 1)(wait_exchange)

    chunk = lambda w: pl.BlockSpec((nb, DN_CHUNK, w), lambda n: (0, rev(n), 0))
    return pl.pallas_call(
        body, name="deltanet_bwd", grid=(nc,),
        out_shape=(jax.ShapeDtypeStruct((nb, s, 3 * DN_WIDTH), BF16), jax.ShapeDtypeStruct((nb, s, DN_WIDTH), BF16),
                   jax.ShapeDtypeStruct((nb, s, GATE_PAD), BF16), jax.ShapeDtypeStruct((CONV_K, 3 * DN_WIDTH), F32),
                   jax.ShapeDtypeStruct((1, GATE_PAD), F32), jax.ShapeDtypeStruct((1, GATE_PAD), F32),
                   jax.ShapeDtypeStruct((1, DN_HEAD_DIM), F32))
        + tuple(jax.ShapeDtypeStruct(a.shape, a.dtype) for a in head_grads),
        in_specs=[chunk(3 * DN_WIDTH), chunk(3 * DN_WIDTH), chunk(DN_WIDTH), chunk(GATE_PAD)] + _dn_weight_specs() + [
            pl.BlockSpec((None, gn, DN_HEAD_DIM, DN_HEAD_DIM), lambda n: (rev(n), 0, 0, 0)),
            pl.BlockSpec((None, gn, DN_CHUNK, DN_CHUNK), lambda n: (rev(n), 0, 0, 0)),
            chunk(DN_WIDTH)] + [HBM_SPEC] * ng,
        out_specs=(chunk(3 * DN_WIDTH), chunk(DN_WIDTH), chunk(GATE_PAD), _whole((CONV_K, 3 * DN_WIDTH)),
                   _whole((1, GATE_PAD)), _whole((1, GATE_PAD)), _whole((1, DN_HEAD_DIM))) + (HBM_SPEC,) * ng,
        scratch_shapes=[pltpu.VMEM((gn, DN_HEAD_DIM, DN_HEAD_DIM), F32),
                        pltpu.VMEM((nb, DN_CHUNK + CONV_HALO, 3 * DN_WIDTH), F32),
                        pltpu.VMEM((CONV_K, 8, 3 * DN_WIDTH), F32)] + _exchange_scratch(ng),
        compiler_params=_params(1),
    )(qkv, conv_out, zg, logits, conv_w, alog, dtb, og, states, inverses, d_out, *head_grads)


def _head(a_out, b_out, x2, p2, target, w_out, w_out_t, w_gate, w_gate_t, w_proj, ple_g, fin_g):
    t = x2.shape[0]
    tm = min(512, t)
    steps = t // tm

    def body(a_ref, b_ref, x_ref, p_ref, y_ref, wo_ref, wot_ref, wg_ref, wgt_ref, wp_ref, pg_ref, fg_ref,
             da_ref, db_ref, dh_ref, dwo_hbm, dwg_hbm, dwp_hbm, dpg_ref, dfg_ref, loss_ref,
             dwo_acc, dwg_acc, dwp_acc, rows_stage, cols_stage):
        i = pl.program_id(0)

        @pl.when(i == 0)
        def _():
            dwo_acc[...] = jnp.zeros_like(dwo_acc)
            dwg_acc[...] = jnp.zeros_like(dwg_acc)
            dwp_acc[...] = jnp.zeros_like(dwp_acc)
            dpg_ref[...] = jnp.zeros_like(dpg_ref)
            dfg_ref[...] = jnp.zeros_like(dfg_ref)
            loss_ref[...] = jnp.zeros_like(loss_ref)

        a = a_ref[...]
        bb = b_ref[...]
        pb = p_ref[...].astype(BF16)
        pg = pg_ref[...]
        fg = fg_ref[...]
        h1 = (x_ref[...] + jnp.dot(a, wo_ref[0:SGU_WIDTH, :], preferred_element_type=F32)
              + jnp.dot(bb, wo_ref[SGU_WIDTH:, :], preferred_element_type=F32))
        n1, r1 = _rms(h1)
        rn = (n1 * pg).astype(BF16)
        gate = _sigmoid(jnp.dot(rn, wg_ref[...], preferred_element_type=F32))
        pp = jnp.dot(pb, wp_ref[...], preferred_element_type=F32)
        h2 = h1 + gate * pp
        n2, r2 = _rms(h2)
        err = n2 * fg - y_ref[...]
        loss_ref[...] += jnp.broadcast_to(_rowsum(jnp.sum(err * err, axis=-1, keepdims=True)), loss_ref.shape)

        dy = err * (1.0 / D_MODEL)
        dfg_ref[...] += _rowsum(dy * n2)
        dh2 = _rms_bwd(dy * fg, n2, r2)
        dpp = (dh2 * gate).astype(BF16)
        dgl = (dh2 * pp * gate * (1.0 - gate)).astype(BF16)
        dwp_acc[...] += lax.dot_general(pb, dpp, (((0,), (0,)), ((), ())), preferred_element_type=F32)
        dwg_acc[...] += lax.dot_general(rn, dgl, (((0,), (0,)), ((), ())), preferred_element_type=F32)
        drn = jnp.dot(dgl, wgt_ref[...], preferred_element_type=F32)
        dpg_ref[...] += _rowsum(drn * n1)
        dh1 = dh2 + _rms_bwd(drn * pg, n1, r1)
        dh_ref[...] = dh1
        dhb = dh1.astype(BF16)
        da_ref[...] = jnp.dot(dhb, wot_ref[:, 0:SGU_WIDTH], preferred_element_type=F32)
        db_ref[...] = jnp.dot(dhb, wot_ref[:, SGU_WIDTH:], preferred_element_type=F32)
        dwo_acc[0:SGU_WIDTH, :] += lax.dot_general(a, dhb, (((0,), (0,)), ((), ())), preferred_element_type=F32)
        dwo_acc[SGU_WIDTH:, :] += lax.dot_general(bb, dhb, (((0,), (0,)), ((), ())), preferred_element_type=F32)

        @pl.when(i == steps - 1)
        def _():
            for j in range(N_DEV):
                for acc, hbm in ((dwo_acc, dwo_hbm), (dwg_acc, dwg_hbm)):
                    rows_stage[...] = acc[j * LANES:(j + 1) * LANES, :].astype(BF16)
                    pltpu.sync_copy(rows_stage, hbm.at[j])
                cols_stage[...] = dwp_acc[:, j * LANES:(j + 1) * LANES].astype(BF16)
                pltpu.sync_copy(cols_stage, dwp_hbm.at[j])

    tile = lambda w: pl.BlockSpec((tm, w), lambda i: (i, 0))
    return pl.pallas_call(
        body, name="head_fwd_bwd", grid=(steps,),
        out_shape=(jax.ShapeDtypeStruct((t, SGU_WIDTH), F32), jax.ShapeDtypeStruct((t, DN_WIDTH), F32),
                   jax.ShapeDtypeStruct((t, D_MODEL), F32), jax.ShapeDtypeStruct((N_DEV, LANES, D_MODEL), BF16),
                   jax.ShapeDtypeStruct((N_DEV, LANES, D_MODEL), BF16), jax.ShapeDtypeStruct((N_DEV, PLE_DIM, LANES), BF16),
                   jax.ShapeDtypeStruct((1, D_MODEL), F32), jax.ShapeDtypeStruct((1, D_MODEL), F32),
                   jax.ShapeDtypeStruct((8, LANES), F32)),
        in_specs=[tile(SGU_WIDTH), tile(DN_WIDTH), tile(D_MODEL), tile(PLE_DIM), tile(D_MODEL),
                  VMEM_SPEC, VMEM_SPEC, VMEM_SPEC, VMEM_SPEC, VMEM_SPEC, _whole((1, D_MODEL)), _whole((1, D_MODEL))],
        out_specs=(tile(SGU_WIDTH), tile(DN_WIDTH), tile(D_MODEL), HBM_SPEC, HBM_SPEC, HBM_SPEC,
                   _whole((1, D_MODEL)), _whole((1, D_MODEL)), _whole((8, LANES))),
        scratch_shapes=[pltpu.VMEM((D_MODEL, D_MODEL), F32), pltpu.VMEM((D_MODEL, D_MODEL), F32),
                        pltpu.VMEM((PLE_DIM, D_MODEL), F32), pltpu.VMEM((LANES, D_MODEL), BF16),
                        pltpu.VMEM((PLE_DIM, LANES), BF16)],
        compiler_params=_params(1),
    )(a_out, b_out, x2, p2, target, w_out, w_out_t, w_gate, w_gate_t, w_proj, ple_g, fin_g)


def _inproj_bwd(x2, dh1, a_uvz, d_sgu, d_q, d_z, d_l, norm_g, sgu_weights, wat, wqt, wzt, wgt):
    t = x2.shape[0]
    tm = min(256, t)
    steps = t // tm

    widths = (a_uvz.shape[1], d_q.shape[1], d_z.shape[1], d_l.shape[1])
    starts = (0, widths[0], widths[0] + widths[1], widths[0] + widths[1] + widths[2])

    def body(x_ref, dh_ref, uvz_ref, dsgu_ref, dq_ref, dz_ref, dl_ref, g_ref, lg_ref, lb_ref, ws_ref, bt_ref,
             wat_ref, wqt_ref, wzt_ref, wgt_ref,
             dx_ref, dw_hbm, dg_ref, dlg_ref, dlb_ref, dws_ref, dbt_ref, dw_acc, stage_ref, da_ref):
        i = pl.program_id(0)

        @pl.when(i == 0)
        def _():
            dw_acc[...] = jnp.zeros_like(dw_acc)
            for ref in (dg_ref, dlg_ref, dlb_ref, dws_ref, dbt_ref):
                ref[...] = jnp.zeros_like(ref)

        _sgu_bwd_tile(uvz_ref, dsgu_ref, (lg_ref, lb_ref, ws_ref, bt_ref), da_ref, (dlg_ref, dlb_ref, dws_ref, dbt_ref))
        g = g_ref[...]
        n, r = _rms(x_ref[...])
        xn = (n * g).astype(BF16)
        dxn = None
        for d_ref, wt_ref, col0 in zip((da_ref, dq_ref, dz_ref, dl_ref), (wat_ref, wqt_ref, wzt_ref, wgt_ref), starts):
            term = jnp.dot(d_ref[...], wt_ref[...], preferred_element_type=F32)
            dxn = term if dxn is None else dxn + term
            width = d_ref.shape[1]
            for c0 in range(0, width, 512):
                c1 = min(c0 + 512, width)
                dw_acc[col0 + c0:col0 + c1, :] += lax.dot_general(d_ref[:, c0:c1], xn, (((0,), (0,)), ((), ())),
                                                                  preferred_element_type=F32)
        dg_ref[...] += _rowsum(dxn * n)
        dx_ref[...] = dh_ref[...] + _rms_bwd(dxn * g, n, r)

        @pl.when(i == steps - 1)
        def _():
            for j in range(N_DEV):
                stage_ref[...] = dw_acc[j * IN_SHARD:(j + 1) * IN_SHARD, :]
                pltpu.sync_copy(stage_ref, dw_hbm.at[j])

    tile = lambda w: pl.BlockSpec((tm, w), lambda i: (i, 0))
    sgu_shapes = ((1, SGU_WIDTH), (1, SGU_WIDTH), (SGU_GROUPS, SGU_CHUNK, SGU_CHUNK), (SGU_CHUNK, SGU_GROUPS))
    return pl.pallas_call(
        body, name="inproj_sgu_bwd", grid=(steps,),
        out_shape=(jax.ShapeDtypeStruct((t, D_MODEL), F32), jax.ShapeDtypeStruct((N_DEV, IN_SHARD, D_MODEL), F32),
                   jax.ShapeDtypeStruct((1, D_MODEL), F32)) + tuple(jax.ShapeDtypeStruct(s, F32) for s in sgu_shapes),
        in_specs=[tile(D_MODEL), tile(D_MODEL), tile(widths[0]), tile(SGU_WIDTH)] + [tile(w) for w in widths[1:]]
        + [_whole((1, D_MODEL))] + [_whole(s) for s in sgu_shapes] + [VMEM_SPEC] * 4,
        out_specs=(tile(D_MODEL), HBM_SPEC, _whole((1, D_MODEL))) + tuple(_whole(s) for s in sgu_shapes),
        scratch_shapes=[pltpu.VMEM((sum(widths), D_MODEL), F32), pltpu.VMEM((IN_SHARD, D_MODEL), F32),
                        pltpu.VMEM((tm, widths[0]), BF16)],
        compiler_params=_params(1),
    )(x2, dh1, a_uvz, d_sgu, d_q, d_z, d_l, norm_g, *sgu_weights, wat, wqt, wzt, wgt)


def _reduce_adamw(recv, w, m, v, name, col_block=None):
    n, rows, cols = recv.shape
    cb = col_block or cols
    lead = w.ndim - 2

    def body(r_ref, w_ref, m_ref, v_ref, g_ref, d_ref, nm_ref, nv_ref):
        g = r_ref[0].astype(F32)
        for i in range(1, n):
            g = g + r_ref[i].astype(F32)
        m_new = ADAM_B1 * m_ref[...] + (1.0 - ADAM_B1) * g
        v_new = ADAM_B2 * v_ref[...] + (1.0 - ADAM_B2) * jnp.square(g)
        m_hat = m_new / (1.0 - ADAM_B1 ** ADAM_STEP)
        v_hat = v_new / (1.0 - ADAM_B2 ** ADAM_STEP)
        g_ref[...] = g
        d_ref[...] = -ADAM_LR * (m_hat / (jnp.sqrt(v_hat) + ADAM_EPS) + ADAM_WD * w_ref[...])
        nm_ref[...] = m_new
        nv_ref[...] = v_new

    blk = pl.BlockSpec((None,) * lead + (rows, cb), lambda i: (0,) * lead + (0, i))
    return pl.pallas_call(
        body, name=name, grid=(cols // cb,),
        out_shape=tuple(jax.ShapeDtypeStruct(w.shape, F32) for _ in range(4)),
        in_specs=[pl.BlockSpec((n, rows, cb), lambda i: (0, 0, i)), blk, blk, blk],
        out_specs=(blk, blk, blk, blk),
        compiler_params=_params(1),
    )(recv, w, m, v)


def _adamw_replicated(received, ws, ms, vs):
    nw = len(ws)
    starts = [sum(SMALL_PIECE_ROWS[:i]) for i in range(len(SMALL_PIECE_ROWS))]

    def natural(g_ref, row0, shape):
        cols, rows = shape[-1], _size(shape[:-1])
        if cols == LANES:
            return g_ref[row0:row0 + rows, :].reshape(shape)
        if cols < LANES:
            return g_ref[row0:row0 + 1, 0:cols].reshape(shape)
        per = cols // LANES
        return jnp.concatenate(
            [jnp.concatenate([g_ref[row0 + r * per + k:row0 + r * per + k + 1, :] for k in range(per)], axis=1)
             for r in range(rows)], axis=0).reshape(shape)

    def body(r_ref, *refs):
        w_refs, m_refs, v_refs = refs[:nw], refs[nw:2 * nw], refs[2 * nw:3 * nw]
        conv_ref, loss_ref = refs[3 * nw], refs[3 * nw + 1]
        out_refs, g_ref = refs[3 * nw + 2:-1], refs[-1]
        g = r_ref[0]
        for q in range(1, N_CHIPS):
            g = g + r_ref[q]
        g_ref[...] = g
        conv_ref[...] = natural(g_ref, starts[0], (CONV_K, 3 * DN_WIDTH))
        loss_ref[...] = natural(g_ref, starts[-1], (1, 1))
        for i in range(nw):
            gi = natural(g_ref, starts[1 + i], w_refs[i].shape)
            m_new = ADAM_B1 * m_refs[i][...] + (1.0 - ADAM_B1) * gi
            v_new = ADAM_B2 * v_refs[i][...] + (1.0 - ADAM_B2) * jnp.square(gi)
            m_hat = m_new / (1.0 - ADAM_B1 ** ADAM_STEP)
            v_hat = v_new / (1.0 - ADAM_B2 ** ADAM_STEP)
            out_refs[4 * i][...] = gi
            out_refs[4 * i + 1][...] = -ADAM_LR * (m_hat / (jnp.sqrt(v_hat) + ADAM_EPS) + ADAM_WD * w_refs[i][...])
            out_refs[4 * i + 2][...] = m_new
            out_refs[4 * i + 3][...] = v_new

    def spec(a):
        lead = max(a.ndim - 3, 0)
        return pl.BlockSpec((None,) * lead + a.shape[lead:], lambda: (0,) * a.ndim)

    weight_specs = [spec(a) for a in ws]
    return pl.pallas_call(
        body, name="adamw_replicated",
        out_shape=(jax.ShapeDtypeStruct((CONV_K, 3 * DN_WIDTH), F32), jax.ShapeDtypeStruct((1, 1), F32))
        + tuple(jax.ShapeDtypeStruct(a.shape, F32) for a in ws for _ in range(4)),
        in_specs=[pl.BlockSpec(received.shape, lambda: (0, 0, 0))] + weight_specs * 3,
        out_specs=(pl.BlockSpec((CONV_K, 3 * DN_WIDTH), lambda: (0, 0)), pl.BlockSpec((1, 1), lambda: (0, 0)))
        + tuple(s for s in weight_specs for _ in range(4)),
        scratch_shapes=[pltpu.VMEM(received.shape[1:], F32)],
        compiler_params=pltpu.CompilerParams(vmem_limit_bytes=VMEM_LIMIT),
    )(received, *ws, *ms, *vs)


def _pack_rows(pieces, rows):
    padded = [jnp.pad(jnp.ravel(p), (0, -p.size % LANES)) for p in pieces]
    flat = jnp.concatenate(padded)
    return jnp.pad(flat, (0, rows * LANES - flat.shape[0])).reshape(rows, LANES)


def kernel(x, p, norm_g, w_in, sgu_ln_g, sgu_ln_b, sgu_w_s, sgu_b_s, dn_conv_w, dn_a_log, dn_dt_bias, dn_o_norm_g, w_out, ple_norm_g, ple_gate_w, ple_proj_w, final_norm_g, loss_target, m_norm_g, m_w_in, m_sgu_ln_g, m_sgu_ln_b, m_sgu_w_s, m_sgu_b_s, m_dn_conv_w, m_dn_a_log, m_dn_dt_bias, m_dn_o_norm_g, m_w_out, m_ple_norm_g, m_ple_gate_w, m_ple_proj_w, m_final_norm_g, v_norm_g, v_w_in, v_sgu_ln_g, v_sgu_ln_b, v_sgu_w_s, v_sgu_b_s, v_dn_conv_w, v_dn_a_log, v_dn_dt_bias, v_dn_o_norm_g, v_w_out, v_ple_norm_g, v_ple_gate_w, v_ple_proj_w, v_final_norm_g):
    weights = dict(norm_g=norm_g, w_in=w_in, sgu_ln_g=sgu_ln_g, sgu_ln_b=sgu_ln_b, sgu_w_s=sgu_w_s, sgu_b_s=sgu_b_s,
                   dn_conv_w=dn_conv_w, dn_a_log=dn_a_log, dn_dt_bias=dn_dt_bias, dn_o_norm_g=dn_o_norm_g, w_out=w_out,
                   ple_norm_g=ple_norm_g, ple_gate_w=ple_gate_w, ple_proj_w=ple_proj_w, final_norm_g=final_norm_g)
    mom1 = dict(norm_g=m_norm_g, w_in=m_w_in, sgu_ln_g=m_sgu_ln_g, sgu_ln_b=m_sgu_ln_b, sgu_w_s=m_sgu_w_s,
                sgu_b_s=m_sgu_b_s, dn_conv_w=m_dn_conv_w, dn_a_log=m_dn_a_log, dn_dt_bias=m_dn_dt_bias,
                dn_o_norm_g=m_dn_o_norm_g, w_out=m_w_out, ple_norm_g=m_ple_norm_g, ple_gate_w=m_ple_gate_w,
                ple_proj_w=m_ple_proj_w, final_norm_g=m_final_norm_g)
    mom2 = dict(norm_g=v_norm_g, w_in=v_w_in, sgu_ln_g=v_sgu_ln_g, sgu_ln_b=v_sgu_ln_b, sgu_w_s=v_sgu_w_s,
                sgu_b_s=v_sgu_b_s, dn_conv_w=v_dn_conv_w, dn_a_log=v_dn_a_log, dn_dt_bias=v_dn_dt_bias,
                dn_o_norm_g=v_dn_o_norm_g, w_out=v_w_out, ple_norm_g=v_ple_norm_g, ple_gate_w=v_ple_gate_w,
                ple_proj_w=v_ple_proj_w, final_norm_g=v_final_norm_g)
    nb, s, _ = x.shape
    t = nb * s

    transposed = lambda a: jnp.transpose(a, (2, 0, 1)).reshape(IN_SHARD, D_MODEL)
    w_in_t, m_in_t, v_in_t = transposed(w_in), transposed(m_w_in), transposed(v_w_in)
    w_in_blocks, conv_blocks = _all_gather([w_in_t.astype(BF16), dn_conv_w[0]])
    w_in_full_t = w_in_blocks.reshape(IN_COLS, D_MODEL)
    wat = w_in_full_t[:3 * SGU_WIDTH]
    wqt = w_in_full_t[3 * SGU_WIDTH:3 * SGU_WIDTH + 3 * DN_WIDTH]
    wzt = w_in_full_t[3 * SGU_WIDTH + 3 * DN_WIDTH:3 * SGU_WIDTH + 4 * DN_WIDTH]
    wgt = jnp.pad(w_in_full_t[3 * SGU_WIDTH + 4 * DN_WIDTH:], ((0, GATE_PAD - 2 * DN_HEADS), (0, 0)))
    conv_full = jnp.moveaxis(conv_blocks, 0, 1).reshape(CONV_K, 3 * DN_WIDTH)
    later_shards = [w_out[0].astype(BF16), ple_gate_w[0].astype(BF16), ple_proj_w[0].astype(BF16)]

    pad_row = lambda a: jnp.pad(a.reshape(1, -1), ((0, 0), (DN_HEADS, GATE_PAD - DN_HEADS - a.size)))
    alog, dtb = pad_row(dn_a_log), pad_row(dn_dt_bias)
    og = dn_o_norm_g.reshape(1, DN_HEAD_DIM)
    ws = sgu_w_s.reshape(SGU_GROUPS, SGU_CHUNK, SGU_CHUNK)
    b_t = sgu_b_s.reshape(SGU_GROUPS, SGU_CHUNK).T
    fin_g = final_norm_g.reshape(1, D_MODEL)

    x2 = x.reshape(t, D_MODEL)
    sgu_weights = (sgu_ln_g, sgu_ln_b, ws, b_t)
    a_uvz, b_qkv, b_z, b_l, a_out, conv_out, w_out_blocks, w_gate_blocks, w_proj_blocks = _inproj_fwd(
        x2, s, norm_g, wat, wqt, wzt, wgt, sgu_weights, conv_full, later_shards)
    w_out_full = w_out_blocks.reshape(D_MODEL, D_MODEL)
    w_gate_full = w_gate_blocks.reshape(D_MODEL, D_MODEL)
    w_proj_full = jnp.moveaxis(w_proj_blocks, 0, 1).reshape(PLE_DIM, D_MODEL)
    qkv3 = b_qkv.reshape(nb, s, 3 * DN_WIDTH)
    conv_out = conv_out.reshape(nb, s, 3 * DN_WIDTH)
    z3 = b_z.reshape(nb, s, DN_WIDTH)
    l3 = b_l.reshape(nb, s, GATE_PAD)
    b_out, states, inverses = _dn_fwd(conv_out, z3, l3, alog, dtb, og)

    d_a, d_b, dh1, g_w_out, g_gate, g_proj, g_ple_g, g_fin_g, loss_tile = _head(
        a_out, b_out.reshape(t, DN_WIDTH), x2, p.reshape(t, PLE_DIM), loss_target.reshape(t, D_MODEL),
        w_out_full, w_out_full.T, w_gate_full, w_gate_full.T, w_proj_full, ple_norm_g, fin_g)
    d_qkv, d_z, d_l, g_conv, g_alog, g_dtb, g_og, *head_received = _dn_bwd(
        qkv3, conv_out, z3, l3, conv_full, alog, dtb, og, states, inverses, d_b.reshape(nb, s, DN_WIDTH),
        [g_w_out, g_gate, g_proj])
    grad_x, g_w_in, g_norm, g_ln_g, g_ln_b, g_ws, g_bt = _inproj_bwd(
        x2, dh1, a_uvz, d_a, d_qkv.reshape(t, 3 * DN_WIDTH), d_z.reshape(t, DN_WIDTH), d_l.reshape(t, GATE_PAD),
        norm_g, sgu_weights, wat, wqt, wzt, wgt)

    small = _pack_rows([g_conv, g_norm, g_ln_g, g_ln_b, g_ws, g_bt.T, g_alog[:, DN_HEADS:2 * DN_HEADS], g_dtb[:, DN_HEADS:2 * DN_HEADS], g_og,
                        g_ple_g, g_fin_g, (0.5 / D_MODEL) * loss_tile[0:1, 0:1]], SMALL_ROWS)
    w_in_received, small_received = _reduce_exchange(g_w_in, small)

    results = {}
    outs = _reduce_adamw(w_in_received, w_in_t, m_in_t, v_in_t, "adamw_w_in", 4 * LANES)
    results["w_in"] = [jnp.transpose(a.reshape(IN_SHARD, 1, D_MODEL), (1, 2, 0)) for a in outs]
    for name, recv in zip(("w_out", "ple_gate_w", "ple_proj_w"), head_received):
        results[name] = _reduce_adamw(recv, weights[name], mom1[name], mom2[name], "adamw_" + name)
    names = [name for name, _ in REPLICATED]
    two_d = lambda a: a.reshape(1, -1) if a.ndim == 1 else a
    g_conv_sum, loss_sum, *flat_outs = _adamw_replicated(
        small_received, *[[two_d(src[k]) for k in names] for src in (weights, mom1, mom2)])
    for i, k in enumerate(names):
        results[k] = [a.reshape(weights[k].shape) for a in flat_outs[4 * i:4 * i + 4]]
    loss = loss_sum[0, 0]
    me = 4 * lax.axis_index("x") + 2 * lax.axis_index("y") + lax.axis_index("c")
    conv_mine = lax.dynamic_slice(g_conv_sum, (0, me * 192), (CONV_K, 192))
    results["dn_conv_w"] = _reduce_adamw(conv_mine[None], dn_conv_w, m_dn_conv_w, v_dn_conv_w, "adamw_dn_conv_w")

    return (loss, grad_x.reshape(nb, s, D_MODEL), *[results[k][0] for k in WEIGHT_ORDER],
            *[results[k][1] for k in WEIGHT_ORDER], *[results[k][2] for k in WEIGHT_ORDER],
            *[results[k][3] for k in WEIGHT_ORDER])
```

```python
import jax
import jax.numpy as jnp
from jax import lax
from jax.experimental import pallas as pl
from jax.experimental.pallas import tpu as pltpu

F32 = jnp.float32
BF16 = jnp.bfloat16

N_DEV = 8
D_MODEL = 1024
SGU_WIDTH = 512
SGU_GROUPS = 4
SGU_CHUNK = 128
DN_WIDTH = 512
DN_HEADS = 4
DN_HEAD_DIM = 128
DN_CHUNK = 128
CONV_K = 4
CONV_HALO = 8
PLE_DIM = 256
EPS = 1e-6
IN_COLS = 3592
IN_SHARD = IN_COLS // N_DEV
GATE_PAD = 128
IN_GROUPS = (3 * SGU_WIDTH, 3 * DN_WIDTH, DN_WIDTH)

ADAM_LR = 0.001
ADAM_B1 = 0.9
ADAM_B2 = 0.999
ADAM_EPS = 1e-08
ADAM_WD = 0.01
ADAM_STEP = 10

LANES = 128
VMEM_LIMIT = 56 * 1024 * 1024
MESH = pl.DeviceIdType.MESH

REPLICATED = (("norm_g", (1, D_MODEL)), ("sgu_ln_g", (1, SGU_WIDTH)), ("sgu_ln_b", (1, SGU_WIDTH)),
              ("sgu_w_s", (1, SGU_GROUPS, SGU_CHUNK, SGU_CHUNK)), ("sgu_b_s", (1, SGU_GROUPS, SGU_CHUNK)),
              ("dn_a_log", (1, DN_HEADS)), ("dn_dt_bias", (1, DN_HEADS)), ("dn_o_norm_g", (1, DN_HEAD_DIM)),
              ("ple_norm_g", (1, D_MODEL)), ("final_norm_g", (D_MODEL,)))
WEIGHT_ORDER = ("norm_g", "w_in", "sgu_ln_g", "sgu_ln_b", "sgu_w_s", "sgu_b_s", "dn_conv_w", "dn_a_log",
                "dn_dt_bias", "dn_o_norm_g", "w_out", "ple_norm_g", "ple_gate_w", "ple_proj_w", "final_norm_g")


def _size(shape):
    n = 1
    for s in shape:
        n *= s
    return n


SMALL_LAYOUT = (("conv", (CONV_K, 3 * DN_WIDTH)),) + REPLICATED + (("loss", (1,)),)
SMALL_PIECE_ROWS = tuple(-(-_size(s) // LANES) for _, s in SMALL_LAYOUT)
SMALL_ROWS = -(-sum(SMALL_PIECE_ROWS) // 8) * 8


def _bdot(a, b):
    return jnp.dot(a.astype(BF16), b.astype(BF16), preferred_element_type=F32)


def _sigmoid(x):
    return 0.5 * jnp.tanh(0.5 * x) + 0.5


@jax.custom_vjp
def _silu(x):
    return x * _sigmoid(x)


def _silu_fwd(x):
    s = _sigmoid(x)
    return x * s, (x, s)


def _silu_bwd(res, ct):
    x, s = res
    return (ct * (s * (1.0 + x * (1.0 - s))),)


_silu.defvjp(_silu_fwd, _silu_bwd)


def _normal_cdf(x):
    return 0.5 + 0.5 * lax.erf(x * (0.5 ** 0.5))


@jax.custom_vjp
def _gelu(x):
    return x * _normal_cdf(x)


def _gelu_fwd(x):
    cdf = _normal_cdf(x)
    return x * cdf, (x, cdf)


def _gelu_bwd(res, ct):
    x, cdf = res
    pdf = jnp.exp(-0.5 * x * x) * ((2.0 * jnp.pi) ** -0.5)
    return (ct * (cdf + x * pdf),)


_gelu.defvjp(_gelu_fwd, _gelu_bwd)


def _softplus(x):
    return jnp.maximum(x, 0.0) + jnp.log1p(jnp.exp(-jnp.abs(x)))


@jax.custom_vjp
def _l2n(x):
    return x * lax.rsqrt(jnp.sum(x * x, axis=-1, keepdims=True) + EPS)


def _l2n_fwd(x):
    r = lax.rsqrt(jnp.sum(x * x, axis=-1, keepdims=True) + EPS)
    n = x * r
    return n, (n, r)


def _l2n_bwd(res, ct):
    n, r = res
    return (r * (ct - n * jnp.sum(ct * n, axis=-1, keepdims=True)),)


_l2n.defvjp(_l2n_fwd, _l2n_bwd)


def _rms(x):
    r = lax.rsqrt(jnp.mean(x * x, axis=-1, keepdims=True) + EPS)
    return x * r, r


def _rms_bwd(dn, n, r):
    return r * (dn - n * jnp.mean(dn * n, axis=-1, keepdims=True))


@jax.custom_vjp
def _rms_normed(x):
    return _rms(x)[0]


def _rms_normed_fwd(x):
    n, r = _rms(x)
    return n, (n, r)


def _rms_normed_bwd(res, ct):
    return (_rms_bwd(ct, *res),)


_rms_normed.defvjp(_rms_normed_fwd, _rms_normed_bwd)


def _onehot_row(idx, width):
    return (lax.broadcasted_iota(jnp.int32, (1, width), 1) == idx).astype(F32)


def _rowsum(x):
    return jnp.sum(x, axis=0, keepdims=True)


def _iota2(n):
    return lax.broadcasted_iota(jnp.int32, (n, n), 0), lax.broadcasted_iota(jnp.int32, (n, n), 1)


def _bmm(a, b):
    return lax.dot_general(a.astype(BF16), b.astype(BF16), (((2,), (1,)), ((0,), (0,))), preferred_element_type=F32)


def _bmm_nt(a, b):
    return lax.dot_general(a.astype(BF16), b.astype(BF16), (((2,), (2,)), ((0,), (0,))), preferred_element_type=F32)


def _bmm_tn(a, b):
    return lax.dot_general(a.astype(BF16), b.astype(BF16), (((1,), (1,)), ((0,), (0,))), preferred_element_type=F32)


def _tri_inv_impl(a):
    n = a.shape[-1]
    r, c = _iota2(n)
    x = r ^ c
    eye = (r == c).astype(F32)
    ad = jnp.where(x < 16, a, 0.0)
    p2 = _bmm(ad, ad)
    e = p2 - ad - _bmm(ad, p2)
    p4 = _bmm(p2, p2)
    e = e + p4 + _bmm(e, p4)
    p8 = _bmm(p4, p4)
    e = e + p8 + _bmm(e, p8)
    size = 16
    while size < n:
        m = jnp.where(jnp.logical_and(x < 2 * size, x >= size), a, 0.0)
        f = m + _bmm(m, e)
        e = e - f - _bmm(e, f)
        size *= 2
    return e + eye


@jax.custom_vjp
def _tri_inv(a, known):
    return _tri_inv_impl(a) if known is None else known


def _tri_inv_fwd(a, known):
    t = _tri_inv(a, known)
    return t, (t, known)


def _tri_inv_bwd(res, dt):
    t, known = res
    return -_bmm_tn(t, _bmm_nt(dt, t)), None if known is None else jnp.zeros_like(known)


_tri_inv.defvjp(_tri_inv_fwd, _tri_inv_bwd)


@jax.custom_vjp
def _standardized(x):
    xc = x - jnp.mean(x, axis=-1, keepdims=True)
    return xc * lax.rsqrt(jnp.mean(xc * xc, axis=-1, keepdims=True) + EPS)


def _standardized_fwd(x):
    xc = x - jnp.mean(x, axis=-1, keepdims=True)
    rstd = lax.rsqrt(jnp.mean(xc * xc, axis=-1, keepdims=True) + EPS)
    y = xc * rstd
    return y, (y, rstd)


def _standardized_bwd(res, ct):
    y, rstd = res
    return (rstd * (ct - jnp.mean(ct, axis=-1, keepdims=True) - y * jnp.mean(ct * y, axis=-1, keepdims=True)),)


_standardized.defvjp(_standardized_fwd, _standardized_bwd)


def _sgu_core(u, v, z, lg, lb, ws, bcol):
    n = ws.shape[0]
    r, c = _iota2(n)
    wm = jnp.where(r >= c, ws, 0.0)
    gu = _gelu(u)
    gv = _gelu(v)
    ln = _standardized(gv) * lg + lb
    s = _bdot(wm, ln) + bcol
    return gu * s * _silu(z)


def _lanes_of(x):
    return jnp.concatenate([x[i] for i in range(x.shape[0])], axis=1)


def _batch_of(x, width):
    return jnp.concatenate([x[None, :, i * width:(i + 1) * width] for i in range(x.shape[1] // width)], axis=0)


def _mask_dot(mask, x):
    hi = x.astype(BF16)
    lo = (x - hi.astype(F32)).astype(BF16)
    m = mask.astype(BF16)
    return jnp.dot(m, hi, preferred_element_type=F32) + jnp.dot(m, lo, preferred_element_type=F32)


def _tri_mask(n, upper):
    r, c = _iota2(n)
    return (r <= c) if upper else (r >= c)


@jax.custom_vjp
def _cumsum_rows(x):
    return _mask_dot(_tri_mask(x.shape[0], False), x)


def _cumsum_rows_fwd(x):
    return _cumsum_rows(x), None


def _cumsum_rows_bwd(_, ct):
    return (_mask_dot(_tri_mask(ct.shape[0], True), ct),)


_cumsum_rows.defvjp(_cumsum_rows_fwd, _cumsum_rows_bwd)


@jax.custom_vjp
def _colsum_all_rows(x):
    return _mask_dot(jnp.ones((x.shape[0], x.shape[0]), jnp.bool_), x)


def _colsum_all_rows_fwd(x):
    return _colsum_all_rows(x), None


def _colsum_all_rows_bwd(_, ct):
    return (_mask_dot(jnp.ones((ct.shape[0], ct.shape[0]), jnp.bool_), ct),)


_colsum_all_rows.defvjp(_colsum_all_rows_fwd, _colsum_all_rows_bwd)


def _dn_core(cq, ck, cv, z, logits, state, alog, dtb, og, t_known=None):
    gn, cn, dh = cq.shape
    heads = gn // logits.shape[0]
    q = _l2n(_silu(cq)) * (dh ** -0.5)
    k = _l2n(_silu(ck))
    v = _silu(cv)
    beta_lanes = _sigmoid(logits)
    g_lanes = -jnp.exp(alog) * _softplus(logits + dtb)
    column = lambda rows, lane: jnp.sum(rows * _onehot_row(lane, rows.shape[-1]), axis=-1, keepdims=True)[None]
    beta = jnp.concatenate([column(beta_lanes[i // heads], i % heads) for i in range(gn)], axis=0)
    g = jnp.concatenate([column(g_lanes[i // heads], heads + i % heads) for i in range(gn)], axis=0)
    r, c = _iota2(cn)
    tril = r >= c
    rw = lax.broadcasted_iota(jnp.int32, (cn, dh), 0)
    cw = lax.broadcasted_iota(jnp.int32, (cn, dh), 1)
    upper_wide = (rw <= cw).astype(F32)
    g_wide = jnp.broadcast_to(g, (gn, cn, dh))
    gc_wide = _batch_of(_cumsum_rows(_lanes_of(g_wide)), dh)
    gc_cols = _batch_of(_colsum_all_rows(_lanes_of(g_wide * upper_wide)), dh)[:, :, :cn]
    decay = jnp.exp(jnp.where(tril, gc_wide[:, :, :cn] - gc_cols, -1e30))
    kb = k * beta
    kk = _bmm_nt(kb, k) * decay
    t = _tri_inv(jnp.where(r > c, kk, 0.0), t_known)
    eg = jnp.exp(gc_wide)
    sol = _bmm(t, jnp.concatenate([v * beta, kb * eg], axis=-1))
    u_val, w_dec = sol[:, :, :dh], sol[:, :, dh:]
    qk = _bmm_nt(q, k) * decay
    g_last = jnp.sum(g_wide, axis=1, keepdims=True)
    k_dec = k * jnp.exp(g_last - gc_wide)
    ws = _bmm(jnp.concatenate([w_dec, q * eg], axis=1), state)
    v_new = u_val - ws[:, :cn]
    o = ws[:, cn:] + _bmm(qk, v_new)
    new_state = state * jnp.exp(g_last) + _bmm_tn(k_dec, v_new)
    return _rms_normed(o) * og * _silu(z), new_state, t


N_CHIPS = 4
HBM_SPEC = pl.BlockSpec(memory_space=pl.ANY)


def _place():
    return lax.axis_index("x"), lax.axis_index("y"), lax.axis_index("c")


def _other_chip(k):
    x, y, _ = _place()
    px = 1 - x if k & 2 else x
    py = 1 - y if k & 1 else y
    return px, py, 2 * px + py


def _remote(src, dst, send_sem, recv_sem, device):
    return pltpu.make_async_remote_copy(src_ref=src, dst_ref=dst, send_sem=send_sem, recv_sem=recv_sem,
                                        device_id=device, device_id_type=MESH)


def _other_device(k):
    x, y, c = _place()
    px = 1 - x if k & 4 else x
    py = 1 - y if k & 2 else y
    pc = 1 - c if k & 1 else c
    return (px, py, pc), 4 * px + 2 * py + pc


def _direct_exchange(srcs, outs, send_sems, recv_sems, local_sems, gather):
    x, y, c = _place()
    me = 4 * x + 2 * y + c

    def copies(arriving):
        out_list = []
        for a, (src, out) in enumerate(zip(srcs, outs)):
            for k in range(1, N_DEV):
                peer, index = _other_device(k)
                mine = src if gather else src.at[index]
                out_list.append(_remote(mine, out.at[index if arriving else me], send_sems.at[a, k - 1],
                                        recv_sems.at[a, k - 1], peer))
        return out_list

    def local_copies():
        return [pltpu.make_async_copy(src if gather else src.at[me], out.at[me], local_sems.at[a])
                for a, (src, out) in enumerate(zip(srcs, outs))]

    def start():
        for cp in local_copies() + copies(False):
            cp.start()

    def wait():
        for cp in copies(True):
            cp.wait_recv()
        for cp in copies(False):
            cp.wait_send()
        for cp in local_copies():
            cp.wait()

    return start, wait


def _exchange_scratch(n):
    return [pltpu.SemaphoreType.DMA((n, N_DEV - 1)), pltpu.SemaphoreType.DMA((n, N_DEV - 1)), pltpu.SemaphoreType.DMA((n,))]


def _all_gather(shards):
    n = len(shards)

    def body(*refs):
        srcs, outs = refs[:n], refs[n:2 * n]
        send_sems, recv_sems, local_sems = refs[2 * n:]
        x, y, c = _place()
        me = 4 * x + 2 * y + c
        sibling = (x, y, 1 - c)
        local = [pltpu.make_async_copy(srcs[a], outs[a].at[me], local_sems.at[a]) for a in range(n)]
        for cp in local:
            cp.start()
        sends = []
        for a in range(n):
            sends.append(_remote(srcs[a], outs[a].at[me], send_sems.at[a, 0], recv_sems.at[a, 0], sibling))
        for k in range(1, N_CHIPS):
            px, py, _ = _other_chip(k)
            for a in range(n):
                sends.append(_remote(srcs[a], outs[a].at[me], send_sems.at[a, k], recv_sems.at[a, k], (px, py, c)))
        for cp in sends:
            cp.start()
        passed = []
        for k in range(1, N_CHIPS):
            px, py, _ = _other_chip(k)
            blk = 4 * px + 2 * py + c
            for a in range(n):
                _remote(srcs[a], outs[a].at[blk], send_sems.at[a, k], recv_sems.at[a, k], (px, py, c)).wait_recv()
            for a in range(n):
                cp = _remote(outs[a].at[blk], outs[a].at[blk], send_sems.at[a, 3 + k], recv_sems.at[a, 3 + k], sibling)
                cp.start()
                passed.append(cp)
        for a in range(n):
            _remote(srcs[a], outs[a].at[me + 1 - 2 * c], send_sems.at[a, 0], recv_sems.at[a, 0], sibling).wait_recv()
        for k in range(1, N_CHIPS):
            px, py, _ = _other_chip(k)
            blk = 4 * px + 2 * py + 1 - c
            for a in range(n):
                _remote(srcs[a], outs[a].at[blk], send_sems.at[a, 3 + k], recv_sems.at[a, 3 + k], sibling).wait_recv()
        for cp in sends + passed:
            cp.wait_send()
        for cp in local:
            cp.wait()

    return pl.pallas_call(
        body, name="all_gather_weights",
        out_shape=tuple(jax.ShapeDtypeStruct((N_DEV,) + a.shape, a.dtype) for a in shards),
        in_specs=[HBM_SPEC] * n, out_specs=(HBM_SPEC,) * n,
        scratch_shapes=[pltpu.SemaphoreType.DMA((n, N_DEV - 1)), pltpu.SemaphoreType.DMA((n, N_DEV - 1)),
                        pltpu.SemaphoreType.DMA((n,))],
    )(*shards)


def _reduce_exchange(by_device, small):
    _, rows, cols = by_device.shape

    def body(g_ref, small_ref, out_ref, small_out_ref, from_sibling, small_from_sibling, stage, sums, small_own, small_sum,
             pair_send, pair_recv, chip_send, chip_recv, local_sems):
        x, y, c = _place()
        mine = 2 * x + y
        sibling = (x, y, 1 - c)
        chips = [(x, y, mine)] + [_other_chip(k) for k in range(1, N_CHIPS)]
        to_sibling = [_remote(g_ref.at[2 * chips[k][2] + 1 - c], from_sibling.at[k], pair_send.at[k], pair_recv.at[k], sibling)
                      for k in range(N_CHIPS)]
        to_sibling.append(_remote(small_ref, small_from_sibling, pair_send.at[N_CHIPS], pair_recv.at[N_CHIPS], sibling))
        for cp in to_sibling:
            cp.start()
        small_mine = pltpu.make_async_copy(small_ref, small_own, local_sems.at[0])
        small_mine.start()
        to_chips = []
        for k in (1, 2, 3, 0):
            px, py, chip = chips[k]
            mine_k = pltpu.make_async_copy(g_ref.at[2 * chip + c], stage, local_sems.at[1])
            mine_k.start()
            to_sibling[k].wait_recv()
            mine_k.wait()
            sums[k] = (stage[...] + from_sibling[k]).astype(sums.dtype)
            if k:
                cp = _remote(sums.at[k], out_ref.at[mine], chip_send.at[0, k - 1], chip_recv.at[0, k - 1], (px, py, c))
                cp.start()
                to_chips.append(cp)
        own_block = pltpu.make_async_copy(sums.at[0], out_ref.at[mine], local_sems.at[2])
        own_block.start()
        to_sibling[N_CHIPS].wait_recv()
        small_mine.wait()
        small_sum[...] = small_own[...] + small_from_sibling[...]
        for k in range(1, N_CHIPS):
            px, py, _ = chips[k]
            cp = _remote(small_sum, small_out_ref.at[mine], chip_send.at[1, k - 1], chip_recv.at[1, k - 1], (px, py, c))
            cp.start()
            to_chips.append(cp)
        own_small = pltpu.make_async_copy(small_sum, small_out_ref.at[mine], local_sems.at[3])
        own_small.start()
        for k in range(1, N_CHIPS):
            px, py, chip = chips[k]
            _remote(sums.at[k], out_ref.at[chip], chip_send.at[0, k - 1], chip_recv.at[0, k - 1], (px, py, c)).wait_recv()
            _remote(small_sum, small_out_ref.at[chip], chip_send.at[1, k - 1], chip_recv.at[1, k - 1], (px, py, c)).wait_recv()
        for cp in to_sibling + to_chips:
            cp.wait_send()
        own_block.wait()
        own_small.wait()

    return pl.pallas_call(
        body, name="grad_reduce_exchange",
        out_shape=(jax.ShapeDtypeStruct((N_CHIPS, rows, cols), BF16), jax.ShapeDtypeStruct((N_CHIPS,) + small.shape, F32)),
        in_specs=[HBM_SPEC, HBM_SPEC], out_specs=(HBM_SPEC, HBM_SPEC),
        scratch_shapes=[pltpu.VMEM((N_CHIPS, rows, cols), F32), pltpu.VMEM(small.shape, F32), pltpu.VMEM((rows, cols), F32),
                        pltpu.VMEM((N_CHIPS, rows, cols), BF16), pltpu.VMEM(small.shape, F32), pltpu.VMEM(small.shape, F32),
                        pltpu.SemaphoreType.DMA((N_CHIPS + 1,)), pltpu.SemaphoreType.DMA((N_CHIPS + 1,)),
                        pltpu.SemaphoreType.DMA((2, N_CHIPS - 1)), pltpu.SemaphoreType.DMA((2, N_CHIPS - 1)),
                        pltpu.SemaphoreType.DMA((4,))],
        compiler_params=pltpu.CompilerParams(vmem_limit_bytes=VMEM_LIMIT),
    )(by_device, small)


def _params(n_axes):
    return pltpu.CompilerParams(dimension_semantics=("arbitrary",) * n_axes, vmem_limit_bytes=VMEM_LIMIT)


def _whole(shape):
    return pl.BlockSpec(shape, lambda *_: (0,) * len(shape))


VMEM_SPEC = pl.BlockSpec(memory_space=pltpu.VMEM)


def _inproj_fwd(x2, seq_len, norm_g, wt, wgt, sgu_weights, conv_w, later_shards):
    t = x2.shape[0]
    tm = min(512, seq_len)
    tiles_per_seq = seq_len // tm
    steps = t // tm
    ns = len(later_shards)

    widths = IN_GROUPS + (wgt.shape[0],)
    starts = (0, IN_GROUPS[0], IN_GROUPS[0] + IN_GROUPS[1], 0)

    def body(x_ref, g_ref, wt_ref, wg_ref, lg_ref, lb_ref, ws_ref, bt_ref, cw_ref, *rest):
        shard_refs, rest = rest[:ns], rest[ns:]
        a_ref, q_ref, z_ref, l_ref, sgu_ref, c_ref = rest[:6]
        gathered_refs, (xpad_ref, send_sems, recv_sems, local_sems) = rest[6:6 + ns], rest[6 + ns:]
        start_gather, wait_gather = _direct_exchange(shard_refs, gathered_refs, send_sems, recv_sems, local_sems, True)
        pl.when(pl.program_id(0) == 0)(start_gather)
        n, _ = _rms(x_ref[...])
        xn = (n * g_ref[...]).astype(BF16)
        for w_ref, row0, width, o_ref in zip((wt_ref, wt_ref, wt_ref, wg_ref), starts, widths, (a_ref, q_ref, z_ref, l_ref)):
            for c0 in range(0, width, 512):
                c1 = min(c0 + 512, width)
                o_ref[:, c0:c1] = lax.dot_general(xn, w_ref[row0 + c0:row0 + c1, :], (((1,), (1,)), ((), ())),
                                                  preferred_element_type=F32)
        for row0 in range(0, tm, SGU_CHUNK):
            for grp in range(SGU_GROUPS):
                args = _sgu_pieces(a_ref, lg_ref, lb_ref, ws_ref, bt_ref, row0, grp)
                sgu_ref[pl.ds(row0, SGU_CHUNK), pl.ds(grp * 128, 128)] = _sgu_core(*args).astype(sgu_ref.dtype)

        @pl.when(pl.program_id(0) % tiles_per_seq == 0)
        def _():
            xpad_ref[0:CONV_HALO, :] = jnp.zeros((CONV_HALO, xpad_ref.shape[1]), F32)

        xpad_ref[CONV_HALO:, :] = q_ref[...]
        acc = None
        for j in range(CONV_K):
            term = cw_ref[j:j + 1, :] * xpad_ref[pl.ds(CONV_HALO - CONV_K + 1 + j, tm), :]
            acc = term if acc is None else acc + term
        c_ref[...] = acc
        xpad_ref[0:CONV_HALO, :] = xpad_ref[tm:tm + CONV_HALO, :]
        pl.when(pl.program_id(0) == steps - 1)(wait_gather)

    tile = lambda w: pl.BlockSpec((tm, w), lambda i: (i, 0))
    sgu_shapes = ((1, SGU_WIDTH), (1, SGU_WIDTH), (SGU_GROUPS, SGU_CHUNK, SGU_CHUNK), (SGU_CHUNK, SGU_GROUPS))
    return pl.pallas_call(
        body, name="inproj_sgu_conv_fwd", grid=(steps,),
        out_shape=tuple(jax.ShapeDtypeStruct((t, w), F32) for w in widths)
        + (jax.ShapeDtypeStruct((t, SGU_WIDTH), BF16), jax.ShapeDtypeStruct((t, widths[1]), F32))
        + tuple(jax.ShapeDtypeStruct((N_DEV,) + a.shape, a.dtype) for a in later_shards),
        in_specs=[tile(D_MODEL), _whole((1, D_MODEL)), VMEM_SPEC, VMEM_SPEC]
        + [_whole(s) for s in sgu_shapes] + [_whole((CONV_K, widths[1]))] + [HBM_SPEC] * ns,
        out_specs=tuple(tile(w) for w in widths) + (tile(SGU_WIDTH), tile(widths[1])) + (HBM_SPEC,) * ns,
        scratch_shapes=[pltpu.VMEM((CONV_HALO + tm, widths[1]), F32)] + _exchange_scratch(ns),
        compiler_params=_params(1),
    )(x2, norm_g, wt, wgt, *sgu_weights, conv_w, *later_shards)


def _sgu_pieces(uvz_ref, lg_ref, lb_ref, ws_ref, bt_ref, row0, grp):
    rows = pl.ds(row0, SGU_CHUNK)
    lanes = pl.ds(grp * 128, 128)
    u = uvz_ref[rows, pl.ds(grp * 128, 128)]
    v = uvz_ref[rows, pl.ds(SGU_WIDTH + grp * 128, 128)]
    z = uvz_ref[rows, pl.ds(2 * SGU_WIDTH + grp * 128, 128)]
    bcol = jnp.sum(bt_ref[...] * _onehot_row(grp, SGU_GROUPS), axis=-1, keepdims=True)
    return u, v, z, lg_ref[:, lanes], lb_ref[:, lanes], ws_ref[grp], bcol


def _sgu_bwd_tile(uvz_ref, do_ref, sgu_refs, duvz_ref, grad_refs):
    lg_ref, lb_ref, ws_ref, bt_ref = sgu_refs
    dlg_ref, dlb_ref, dws_ref, dbt_ref = grad_refs
    for row0 in range(0, uvz_ref.shape[0], SGU_CHUNK):
        rows = pl.ds(row0, SGU_CHUNK)
        for grp in range(SGU_GROUPS):
            lanes = pl.ds(grp * 128, 128)
            args = _sgu_pieces(uvz_ref, lg_ref, lb_ref, ws_ref, bt_ref, row0, grp)
            _, pull = jax.vjp(_sgu_core, *args)
            du, dv, dz, dlg, dlb, dws, dbcol = pull(do_ref[rows, lanes])
            duvz_ref[rows, pl.ds(grp * 128, 128)] = du.astype(duvz_ref.dtype)
            duvz_ref[rows, pl.ds(SGU_WIDTH + grp * 128, 128)] = dv.astype(duvz_ref.dtype)
            duvz_ref[rows, pl.ds(2 * SGU_WIDTH + grp * 128, 128)] = dz.astype(duvz_ref.dtype)
            dlg_ref[:, lanes] += dlg
            dlb_ref[:, lanes] += dlb
            dws_ref[grp] += dws
            dbt_ref[...] += dbcol * _onehot_row(grp, SGU_GROUPS)


def _dn_pairs(nb):
    return [(b, h) for b in range(nb) for h in range(DN_HEADS)]


def _dn_batch_args(c_ref, z_ref):
    pairs = _dn_pairs(c_ref.shape[0])
    pick = lambda ref, b, col: ref[b, :, pl.ds(col, DN_HEAD_DIM)]
    cq = jnp.stack([pick(c_ref, b, h * DN_HEAD_DIM) for b, h in pairs])
    ck = jnp.stack([pick(c_ref, b, DN_WIDTH + h * DN_HEAD_DIM) for b, h in pairs])
    cv = jnp.stack([pick(c_ref, b, 2 * DN_WIDTH + h * DN_HEAD_DIM) for b, h in pairs])
    z = jnp.stack([pick(z_ref, b, h * DN_HEAD_DIM) for b, h in pairs])
    return cq, ck, cv, z


def _dn_weight_specs():
    return [_whole((CONV_K, 3 * DN_WIDTH)), _whole((1, GATE_PAD)), _whole((1, GATE_PAD)), _whole((1, DN_HEAD_DIM))]


def _dn_fwd(conv_out, zg, logits, alog, dtb, og):
    nb, s, _ = conv_out.shape
    nc = s // DN_CHUNK
    pairs = _dn_pairs(nb)
    gn = len(pairs)
    chunk = lambda w: pl.BlockSpec((nb, DN_CHUNK, w), lambda n: (0, n, 0))

    def body(c_ref, z_ref, l_ref, alog_ref, dtb_ref, og_ref, out_ref, st_ref, inv_ref, state_ref):
        n = pl.program_id(0)

        @pl.when(n == 0)
        def _():
            state_ref[...] = jnp.zeros_like(state_ref)

        cq, ck, cv, z = _dn_batch_args(c_ref, z_ref)
        state = state_ref[...]
        st_ref[...] = state
        out, new_state, t = _dn_core(cq, ck, cv, z, l_ref[...], state, alog_ref[...], dtb_ref[...], og_ref[...])
        state_ref[...] = new_state
        inv_ref[...] = t.astype(inv_ref.dtype)
        for i, (b, h) in enumerate(pairs):
            out_ref[b, :, pl.ds(h * DN_HEAD_DIM, DN_HEAD_DIM)] = out[i].astype(out_ref.dtype)

    per_chunk = pl.BlockSpec((None, gn, DN_HEAD_DIM, DN_HEAD_DIM), lambda n: (n, 0, 0, 0))
    return pl.pallas_call(
        body, name="deltanet_fwd", grid=(nc,),
        out_shape=(jax.ShapeDtypeStruct((nb, s, DN_WIDTH), BF16),
                   jax.ShapeDtypeStruct((nc, gn, DN_HEAD_DIM, DN_HEAD_DIM), F32),
                   jax.ShapeDtypeStruct((nc, gn, DN_CHUNK, DN_CHUNK), BF16)),
        in_specs=[chunk(3 * DN_WIDTH), chunk(DN_WIDTH), chunk(GATE_PAD)] + _dn_weight_specs()[1:],
        out_specs=(chunk(DN_WIDTH), per_chunk, pl.BlockSpec((None, gn, DN_CHUNK, DN_CHUNK), lambda n: (n, 0, 0, 0))),
        scratch_shapes=[pltpu.VMEM((gn, DN_HEAD_DIM, DN_HEAD_DIM), F32)],
        compiler_params=_params(1),
    )(conv_out, zg, logits, alog, dtb, og)


def _dn_bwd(qkv, conv_out, zg, logits, conv_w, alog, dtb, og, states, inverses, d_out, head_grads):
    nb, s, _ = qkv.shape
    nc = s // DN_CHUNK
    rev = lambda n: nc - 1 - n
    pairs = _dn_pairs(nb)
    gn = len(pairs)
    ng = len(head_grads)

    def body(cur_ref, c_ref, z_ref, l_ref, w_ref, alog_ref, dtb_ref, og_ref, st_ref, inv_ref, do_ref, *rest):
        grad_refs, rest = rest[:ng], rest[ng:]
        dqkv_ref, dz_ref, dl_ref, dw_ref, dalog_ref, ddtb_ref, dog_ref = rest[:7]
        recv_refs, (dstate_ref, dcpad_ref, dw_part_ref, send_sems, recv_sems, local_sems) = rest[7:7 + ng], rest[7 + ng:]
        n = pl.program_id(0)
        start_exchange, wait_exchange = _direct_exchange(grad_refs, recv_refs, send_sems, recv_sems, local_sems, False)
        pl.when(n == 0)(start_exchange)

        @pl.when(n == 0)
        def _():
            dw_part_ref[...] = jnp.zeros_like(dw_part_ref)
            dalog_ref[...] = jnp.zeros_like(dalog_ref)
            ddtb_ref[...] = jnp.zeros_like(ddtb_ref)
            dog_ref[...] = jnp.zeros_like(dog_ref)
            dstate_ref[...] = jnp.zeros_like(dstate_ref)
            dcpad_ref[:, DN_CHUNK:, :] = jnp.zeros((nb, CONV_HALO, 3 * DN_WIDTH), F32)

        cq, ck, cv, z = _dn_batch_args(c_ref, z_ref)
        d_out_g = jnp.stack([do_ref[b, :, pl.ds(h * DN_HEAD_DIM, DN_HEAD_DIM)] for b, h in pairs])
        t_known = inv_ref[...].astype(F32)
        core = lambda *args: _dn_core(*args, t_known=t_known)[:2]
        _, pull = jax.vjp(core, cq, ck, cv, z, l_ref[...], st_ref[...], alog_ref[...], dtb_ref[...], og_ref[...])
        dcq, dck, dcv, dz, dlog, dstate, dalog, ddtb, dog = pull((d_out_g, dstate_ref[...]))
        dstate_ref[...] = dstate
        dl_ref[...] = dlog.astype(dl_ref.dtype)
        dalog_ref[...] += dalog
        ddtb_ref[...] += ddtb
        dog_ref[...] += dog
        for i, (b, h) in enumerate(pairs):
            dcpad_ref[b, 0:DN_CHUNK, pl.ds(h * DN_HEAD_DIM, DN_HEAD_DIM)] = dcq[i]
            dcpad_ref[b, 0:DN_CHUNK, pl.ds(DN_WIDTH + h * DN_HEAD_DIM, DN_HEAD_DIM)] = dck[i]
            dcpad_ref[b, 0:DN_CHUNK, pl.ds(2 * DN_WIDTH + h * DN_HEAD_DIM, DN_HEAD_DIM)] = dcv[i]
            dz_ref[b, :, pl.ds(h * DN_HEAD_DIM, DN_HEAD_DIM)] = dz[i].astype(dz_ref.dtype)
        for b in range(nb):
            xb = cur_ref[b]
            dx = None
            for j in range(CONV_K):
                shifted = dcpad_ref[b, pl.ds(CONV_K - 1 - j, DN_CHUNK), :]
                term = w_ref[j:j + 1, :] * shifted
                dx = term if dx is None else dx + term
                dw_part_ref[j] += jnp.sum((shifted * xb).reshape(DN_CHUNK // 8, 8, 3 * DN_WIDTH), axis=0)
            dqkv_ref[b] = dx.astype(dqkv_ref.dtype)
            dcpad_ref[b, DN_CHUNK:, :] = dcpad_ref[b, 0:CONV_HALO, :]

        @pl.when(n == nc - 1)
        def _():
            dw_ref[...] = jnp.sum(dw_part_ref[...], axis=1)

        pl.when(n == nc - 1)(wait_exchange)

    chunk = lambda w: pl.BlockSpec((nb, DN_CHUNK, w), lambda n: (0, rev(n), 0))
    return pl.pallas_call(
        body, name="deltanet_bwd", grid=(nc,),
        out_shape=(jax.ShapeDtypeStruct((nb, s, 3 * DN_WIDTH), BF16), jax.ShapeDtypeStruct((nb, s, DN_WIDTH), BF16),
                   jax.ShapeDtypeStruct((nb, s, GATE_PAD), BF16), jax.ShapeDtypeStruct((CONV_K, 3 * DN_WIDTH), F32),
                   jax.ShapeDtypeStruct((1, GATE_PAD), F32), jax.ShapeDtypeStruct((1, GATE_PAD), F32),
                   jax.ShapeDtypeStruct((1, DN_HEAD_DIM), F32))
        + tuple(jax.ShapeDtypeStruct(a.shape, a.dtype) for a in head_grads),
        in_specs=[chunk(3 * DN_WIDTH), chunk(3 * DN_WIDTH), chunk(DN_WIDTH), chunk(GATE_PAD)] + _dn_weight_specs() + [
            pl.BlockSpec((None, gn, DN_HEAD_DIM, DN_HEAD_DIM), lambda n: (rev(n), 0, 0, 0)),
            pl.BlockSpec((None, gn, DN_CHUNK, DN_CHUNK), lambda n: (rev(n), 0, 0, 0)),
            chunk(DN_WIDTH)] + [HBM_SPEC] * ng,
        out_specs=(chunk(3 * DN_WIDTH), chunk(DN_WIDTH), chunk(GATE_PAD), _whole((CONV_K, 3 * DN_WIDTH)),
                   _whole((1, GATE_PAD)), _whole((1, GATE_PAD)), _whole((1, DN_HEAD_DIM))) + (HBM_SPEC,) * ng,
        scratch_shapes=[pltpu.VMEM((gn, DN_HEAD_DIM, DN_HEAD_DIM), F32),
                        pltpu.VMEM((nb, DN_CHUNK + CONV_HALO, 3 * DN_WIDTH), F32),
                        pltpu.VMEM((CONV_K, 8, 3 * DN_WIDTH), F32)] + _exchange_scratch(ng),
        compiler_params=_params(1),
    )(qkv, conv_out, zg, logits, conv_w, alog, dtb, og, states, inverses, d_out, *head_grads)


def _head(a_out, b_out, x2, p2, target, w_out, w_gate, w_proj, ple_g, fin_g):
    t = x2.shape[0]
    tm = min(512, t)
    steps = t // tm

    def body(a_ref, b_ref, x_ref, p_ref, y_ref, wo_ref, wg_ref, wp_ref, pg_ref, fg_ref,
             da_ref, db_ref, dh_ref, dwo_hbm, dwg_hbm, dwp_hbm, dpg_ref, dfg_ref, loss_ref,
             dwo_acc, dwg_acc, dwp_acc, rows_stage, cols_stage):
        i = pl.program_id(0)

        @pl.when(i == 0)
        def _():
            dwo_acc[...] = jnp.zeros_like(dwo_acc)
            dwg_acc[...] = jnp.zeros_like(dwg_acc)
            dwp_acc[...] = jnp.zeros_like(dwp_acc)
            dpg_ref[...] = jnp.zeros_like(dpg_ref)
            dfg_ref[...] = jnp.zeros_like(dfg_ref)
            loss_ref[...] = jnp.zeros_like(loss_ref)

        a = a_ref[...]
        bb = b_ref[...]
        pb = p_ref[...].astype(BF16)
        pg = pg_ref[...]
        fg = fg_ref[...]
        h1 = (x_ref[...] + jnp.dot(a, wo_ref[0:SGU_WIDTH, :], preferred_element_type=F32)
              + jnp.dot(bb, wo_ref[SGU_WIDTH:, :], preferred_element_type=F32))
        n1, r1 = _rms(h1)
        rn = (n1 * pg).astype(BF16)
        gate = _sigmoid(jnp.dot(rn, wg_ref[...], preferred_element_type=F32))
        pp = jnp.dot(pb, wp_ref[...], preferred_element_type=F32)
        h2 = h1 + gate * pp
        n2, r2 = _rms(h2)
        err = n2 * fg - y_ref[...]
        loss_ref[...] += jnp.broadcast_to(_rowsum(jnp.sum(err * err, axis=-1, keepdims=True)), loss_ref.shape)

        dy = err * (1.0 / D_MODEL)
        dfg_ref[...] += _rowsum(dy * n2)
        dh2 = _rms_bwd(dy * fg, n2, r2)
        dpp = (dh2 * gate).astype(BF16)
        dgl = (dh2 * pp * gate * (1.0 - gate)).astype(BF16)
        dwp_acc[...] += lax.dot_general(pb, dpp, (((0,), (0,)), ((), ())), preferred_element_type=F32)
        dwg_acc[...] += lax.dot_general(rn, dgl, (((0,), (0,)), ((), ())), preferred_element_type=F32)
        nt = (((1,), (1,)), ((), ()))
        drn = lax.dot_general(dgl, wg_ref[...], nt, preferred_element_type=F32)
        dpg_ref[...] += _rowsum(drn * n1)
        dh1 = dh2 + _rms_bwd(drn * pg, n1, r1)
        dh_ref[...] = dh1
        dhb = dh1.astype(BF16)
        da_ref[...] = lax.dot_general(dhb, wo_ref[0:SGU_WIDTH, :], nt, preferred_element_type=F32)
        db_ref[...] = lax.dot_general(dhb, wo_ref[SGU_WIDTH:, :], nt, preferred_element_type=F32)
        dwo_acc[0:SGU_WIDTH, :] += lax.dot_general(a, dhb, (((0,), (0,)), ((), ())), preferred_element_type=F32)
        dwo_acc[SGU_WIDTH:, :] += lax.dot_general(bb, dhb, (((0,), (0,)), ((), ())), preferred_element_type=F32)

        @pl.when(i == steps - 1)
        def _():
            for j in range(N_DEV):
                for acc, hbm in ((dwo_acc, dwo_hbm), (dwg_acc, dwg_hbm)):
                    rows_stage[...] = acc[j * LANES:(j + 1) * LANES, :].astype(BF16)
                    pltpu.sync_copy(rows_stage, hbm.at[j])
                cols_stage[...] = dwp_acc[:, j * LANES:(j + 1) * LANES].astype(BF16)
                pltpu.sync_copy(cols_stage, dwp_hbm.at[j])

    tile = lambda w: pl.BlockSpec((tm, w), lambda i: (i, 0))
    return pl.pallas_call(
        body, name="head_fwd_bwd", grid=(steps,),
        out_shape=(jax.ShapeDtypeStruct((t, SGU_WIDTH), F32), jax.ShapeDtypeStruct((t, DN_WIDTH), F32),
                   jax.ShapeDtypeStruct((t, D_MODEL), F32), jax.ShapeDtypeStruct((N_DEV, LANES, D_MODEL), BF16),
                   jax.ShapeDtypeStruct((N_DEV, LANES, D_MODEL), BF16), jax.ShapeDtypeStruct((N_DEV, PLE_DIM, LANES), BF16),
                   jax.ShapeDtypeStruct((1, D_MODEL), F32), jax.ShapeDtypeStruct((1, D_MODEL), F32),
                   jax.ShapeDtypeStruct((8, LANES), F32)),
        in_specs=[tile(SGU_WIDTH), tile(DN_WIDTH), tile(D_MODEL), tile(PLE_DIM), tile(D_MODEL),
                  VMEM_SPEC, VMEM_SPEC, VMEM_SPEC, _whole((1, D_MODEL)), _whole((1, D_MODEL))],
        out_specs=(tile(SGU_WIDTH), tile(DN_WIDTH), tile(D_MODEL), HBM_SPEC, HBM_SPEC, HBM_SPEC,
                   _whole((1, D_MODEL)), _whole((1, D_MODEL)), _whole((8, LANES))),
        scratch_shapes=[pltpu.VMEM((D_MODEL, D_MODEL), F32), pltpu.VMEM((D_MODEL, D_MODEL), F32),
                        pltpu.VMEM((PLE_DIM, D_MODEL), F32), pltpu.VMEM((LANES, D_MODEL), BF16),
                        pltpu.VMEM((PLE_DIM, LANES), BF16)],
        compiler_params=_params(1),
    )(a_out, b_out, x2, p2, target, w_out, w_gate, w_proj, ple_g, fin_g)


def _inproj_bwd(x2, dh1, a_uvz, d_sgu, d_q, d_z, d_l, norm_g, sgu_weights, wt, wgt):
    t = x2.shape[0]
    tm = min(256, t)
    steps = t // tm

    widths = (a_uvz.shape[1], d_q.shape[1], d_z.shape[1], d_l.shape[1])
    starts = (0, widths[0], widths[0] + widths[1], widths[0] + widths[1] + widths[2])

    def body(x_ref, dh_ref, uvz_ref, dsgu_ref, dq_ref, dz_ref, dl_ref, g_ref, lg_ref, lb_ref, ws_ref, bt_ref,
             wt_ref, wgt_ref,
             dx_ref, dw_hbm, dg_ref, dlg_ref, dlb_ref, dws_ref, dbt_ref, dw_acc, stage_ref, da_ref):
        i = pl.program_id(0)

        @pl.when(i == 0)
        def _():
            dw_acc[...] = jnp.zeros_like(dw_acc)
            for ref in (dg_ref, dlg_ref, dlb_ref, dws_ref, dbt_ref):
                ref[...] = jnp.zeros_like(ref)

        _sgu_bwd_tile(uvz_ref, dsgu_ref, (lg_ref, lb_ref, ws_ref, bt_ref), da_ref, (dlg_ref, dlb_ref, dws_ref, dbt_ref))
        g = g_ref[...]
        n, r = _rms(x_ref[...])
        xn = (n * g).astype(BF16)
        dxn = None
        for d_ref, col0 in zip((da_ref, dq_ref, dz_ref, dl_ref), starts):
            width = d_ref.shape[1]
            rows = wgt_ref[...] if d_ref is dl_ref else wt_ref[col0:col0 + width, :]
            term = jnp.dot(d_ref[...], rows, preferred_element_type=F32)
            dxn = term if dxn is None else dxn + term
            for c0 in range(0, width, 512):
                c1 = min(c0 + 512, width)
                dw_acc[col0 + c0:col0 + c1, :] += lax.dot_general(d_ref[:, c0:c1], xn, (((0,), (0,)), ((), ())),
                                                                  preferred_element_type=F32)
        dg_ref[...] += _rowsum(dxn * n)
        dx_ref[...] = dh_ref[...] + _rms_bwd(dxn * g, n, r)

        @pl.when(i == steps - 1)
        def _():
            for j in range(N_DEV):
                stage_ref[...] = dw_acc[j * IN_SHARD:(j + 1) * IN_SHARD, :]
                pltpu.sync_copy(stage_ref, dw_hbm.at[j])

    tile = lambda w: pl.BlockSpec((tm, w), lambda i: (i, 0))
    sgu_shapes = ((1, SGU_WIDTH), (1, SGU_WIDTH), (SGU_GROUPS, SGU_CHUNK, SGU_CHUNK), (SGU_CHUNK, SGU_GROUPS))
    return pl.pallas_call(
        body, name="inproj_sgu_bwd", grid=(steps,),
        out_shape=(jax.ShapeDtypeStruct((t, D_MODEL), F32), jax.ShapeDtypeStruct((N_DEV, IN_SHARD, D_MODEL), F32),
                   jax.ShapeDtypeStruct((1, D_MODEL), F32)) + tuple(jax.ShapeDtypeStruct(s, F32) for s in sgu_shapes),
        in_specs=[tile(D_MODEL), tile(D_MODEL), tile(widths[0]), tile(SGU_WIDTH)] + [tile(w) for w in widths[1:]]
        + [_whole((1, D_MODEL))] + [_whole(s) for s in sgu_shapes] + [VMEM_SPEC] * 2,
        out_specs=(tile(D_MODEL), HBM_SPEC, _whole((1, D_MODEL))) + tuple(_whole(s) for s in sgu_shapes),
        scratch_shapes=[pltpu.VMEM((sum(widths), D_MODEL), F32), pltpu.VMEM((IN_SHARD, D_MODEL), F32),
                        pltpu.VMEM((tm, widths[0]), BF16)],
        compiler_params=_params(1),
    )(x2, dh1, a_uvz, d_sgu, d_q, d_z, d_l, norm_g, *sgu_weights, wt, wgt)


def _reduce_adamw(recv, w, m, v, name, col_block=None):
    n, rows, cols = recv.shape
    cb = col_block or cols
    lead = w.ndim - 2

    def body(r_ref, w_ref, m_ref, v_ref, g_ref, d_ref, nm_ref, nv_ref):
        g = r_ref[0].astype(F32)
        for i in range(1, n):
            g = g + r_ref[i].astype(F32)
        m_new = ADAM_B1 * m_ref[...] + (1.0 - ADAM_B1) * g
        v_new = ADAM_B2 * v_ref[...] + (1.0 - ADAM_B2) * jnp.square(g)
        m_hat = m_new / (1.0 - ADAM_B1 ** ADAM_STEP)
        v_hat = v_new / (1.0 - ADAM_B2 ** ADAM_STEP)
        g_ref[...] = g
        d_ref[...] = -ADAM_LR * (m_hat / (jnp.sqrt(v_hat) + ADAM_EPS) + ADAM_WD * w_ref[...])
        nm_ref[...] = m_new
        nv_ref[...] = v_new

    blk = pl.BlockSpec((None,) * lead + (rows, cb), lambda i: (0,) * lead + (0, i))
    return pl.pallas_call(
        body, name=name, grid=(cols // cb,),
        out_shape=tuple(jax.ShapeDtypeStruct(w.shape, F32) for _ in range(4)),
        in_specs=[pl.BlockSpec((n, rows, cb), lambda i: (0, 0, i)), blk, blk, blk],
        out_specs=(blk, blk, blk, blk),
        compiler_params=_params(1),
    )(recv, w, m, v)


def _adamw_replicated(received, ws, ms, vs):
    nw = len(ws)
    starts = [sum(SMALL_PIECE_ROWS[:i]) for i in range(len(SMALL_PIECE_ROWS))]

    def natural(g_ref, row0, shape):
        cols, rows = shape[-1], _size(shape[:-1])
        if cols == LANES:
            return g_ref[row0:row0 + rows, :].reshape(shape)
        if cols < LANES:
            return g_ref[row0:row0 + 1, 0:cols].reshape(shape)
        per = cols // LANES
        return jnp.concatenate(
            [jnp.concatenate([g_ref[row0 + r * per + k:row0 + r * per + k + 1, :] for k in range(per)], axis=1)
             for r in range(rows)], axis=0).reshape(shape)

    def body(r_ref, *refs):
        w_refs, m_refs, v_refs = refs[:nw], refs[nw:2 * nw], refs[2 * nw:3 * nw]
        conv_ref, loss_ref = refs[3 * nw], refs[3 * nw + 1]
        out_refs, g_ref = refs[3 * nw + 2:-1], refs[-1]
        g = r_ref[0]
        for q in range(1, N_CHIPS):
            g = g + r_ref[q]
        g_ref[...] = g
        conv_ref[...] = natural(g_ref, starts[0], (CONV_K, 3 * DN_WIDTH))
        loss_ref[...] = natural(g_ref, starts[-1], (1, 1))
        for i in range(nw):
            gi = natural(g_ref, starts[1 + i], w_refs[i].shape)
            m_new = ADAM_B1 * m_refs[i][...] + (1.0 - ADAM_B1) * gi
            v_new = ADAM_B2 * v_refs[i][...] + (1.0 - ADAM_B2) * jnp.square(gi)
            m_hat = m_new / (1.0 - ADAM_B1 ** ADAM_STEP)
            v_hat = v_new / (1.0 - ADAM_B2 ** ADAM_STEP)
            out_refs[4 * i][...] = gi
            out_refs[4 * i + 1][...] = -ADAM_LR * (m_hat / (jnp.sqrt(v_hat) + ADAM_EPS) + ADAM_WD * w_refs[i][...])
            out_refs[4 * i + 2][...] = m_new
            out_refs[4 * i + 3][...] = v_new

    def spec(a):
        lead = max(a.ndim - 3, 0)
        return pl.BlockSpec((None,) * lead + a.shape[lead:], lambda: (0,) * a.ndim)

    weight_specs = [spec(a) for a in ws]
    return pl.pallas_call(
        body, name="adamw_replicated",
        out_shape=(jax.ShapeDtypeStruct((CONV_K, 3 * DN_WIDTH), F32), jax.ShapeDtypeStruct((1, 1), F32))
        + tuple(jax.ShapeDtypeStruct(a.shape, F32) for a in ws for _ in range(4)),
        in_specs=[pl.BlockSpec(received.shape, lambda: (0, 0, 0))] + weight_specs * 3,
        out_specs=(pl.BlockSpec((CONV_K, 3 * DN_WIDTH), lambda: (0, 0)), pl.BlockSpec((1, 1), lambda: (0, 0)))
        + tuple(s for s in weight_specs for _ in range(4)),
        scratch_shapes=[pltpu.VMEM(received.shape[1:], F32)],
        compiler_params=pltpu.CompilerParams(vmem_limit_bytes=VMEM_LIMIT),
    )(received, *ws, *ms, *vs)


def _pack_rows(pieces, rows):
    padded = [jnp.pad(jnp.ravel(p), (0, -p.size % LANES)) for p in pieces]
    flat = jnp.concatenate(padded)
    return jnp.pad(flat, (0, rows * LANES - flat.shape[0])).reshape(rows, LANES)


def kernel(x, p, norm_g, w_in, sgu_ln_g, sgu_ln_b, sgu_w_s, sgu_b_s, dn_conv_w, dn_a_log, dn_dt_bias, dn_o_norm_g, w_out, ple_norm_g, ple_gate_w, ple_proj_w, final_norm_g, loss_target, m_norm_g, m_w_in, m_sgu_ln_g, m_sgu_ln_b, m_sgu_w_s, m_sgu_b_s, m_dn_conv_w, m_dn_a_log, m_dn_dt_bias, m_dn_o_norm_g, m_w_out, m_ple_norm_g, m_ple_gate_w, m_ple_proj_w, m_final_norm_g, v_norm_g, v_w_in, v_sgu_ln_g, v_sgu_ln_b, v_sgu_w_s, v_sgu_b_s, v_dn_conv_w, v_dn_a_log, v_dn_dt_bias, v_dn_o_norm_g, v_w_out, v_ple_norm_g, v_ple_gate_w, v_ple_proj_w, v_final_norm_g):
    weights = dict(norm_g=norm_g, w_in=w_in, sgu_ln_g=sgu_ln_g, sgu_ln_b=sgu_ln_b, sgu_w_s=sgu_w_s, sgu_b_s=sgu_b_s,
                   dn_conv_w=dn_conv_w, dn_a_log=dn_a_log, dn_dt_bias=dn_dt_bias, dn_o_norm_g=dn_o_norm_g, w_out=w_out,
                   ple_norm_g=ple_norm_g, ple_gate_w=ple_gate_w, ple_proj_w=ple_proj_w, final_norm_g=final_norm_g)
    mom1 = dict(norm_g=m_norm_g, w_in=m_w_in, sgu_ln_g=m_sgu_ln_g, sgu_ln_b=m_sgu_ln_b, sgu_w_s=m_sgu_w_s,
                sgu_b_s=m_sgu_b_s, dn_conv_w=m_dn_conv_w, dn_a_log=m_dn_a_log, dn_dt_bias=m_dn_dt_bias,
                dn_o_norm_g=m_dn_o_norm_g, w_out=m_w_out, ple_norm_g=m_ple_norm_g, ple_gate_w=m_ple_gate_w,
                ple_proj_w=m_ple_proj_w, final_norm_g=m_final_norm_g)
    mom2 = dict(norm_g=v_norm_g, w_in=v_w_in, sgu_ln_g=v_sgu_ln_g, sgu_ln_b=v_sgu_ln_b, sgu_w_s=v_sgu_w_s,
                sgu_b_s=v_sgu_b_s, dn_conv_w=v_dn_conv_w, dn_a_log=v_dn_a_log, dn_dt_bias=v_dn_dt_bias,
                dn_o_norm_g=v_dn_o_norm_g, w_out=v_w_out, ple_norm_g=v_ple_norm_g, ple_gate_w=v_ple_gate_w,
                ple_proj_w=v_ple_proj_w, final_norm_g=v_final_norm_g)
    nb, s, _ = x.shape
    t = nb * s

    transposed = lambda a: jnp.transpose(a, (2, 0, 1)).reshape(IN_SHARD, D_MODEL)
    w_in_t, m_in_t, v_in_t = transposed(w_in), transposed(m_w_in), transposed(v_w_in)
    w_in_blocks, conv_blocks = _all_gather([w_in_t.astype(BF16), dn_conv_w[0]])
    w_in_full_t = w_in_blocks.reshape(IN_COLS, D_MODEL)
    wgt = jnp.pad(w_in_full_t[sum(IN_GROUPS):], ((0, GATE_PAD - 2 * DN_HEADS), (0, 0)))
    conv_full = jnp.moveaxis(conv_blocks, 0, 1).reshape(CONV_K, 3 * DN_WIDTH)
    later_shards = [w_out[0].astype(BF16), ple_gate_w[0].astype(BF16), ple_proj_w[0].astype(BF16)]

    pad_row = lambda a: jnp.pad(a.reshape(1, -1), ((0, 0), (DN_HEADS, GATE_PAD - DN_HEADS - a.size)))
    alog, dtb = pad_row(dn_a_log), pad_row(dn_dt_bias)
    og = dn_o_norm_g.reshape(1, DN_HEAD_DIM)
    ws = sgu_w_s.reshape(SGU_GROUPS, SGU_CHUNK, SGU_CHUNK)
    b_t = sgu_b_s.reshape(SGU_GROUPS, SGU_CHUNK).T
    fin_g = final_norm_g.reshape(1, D_MODEL)

    x2 = x.reshape(t, D_MODEL)
    sgu_weights = (sgu_ln_g, sgu_ln_b, ws, b_t)
    a_uvz, b_qkv, b_z, b_l, a_out, conv_out, w_out_blocks, w_gate_blocks, w_proj_blocks = _inproj_fwd(
        x2, s, norm_g, w_in_full_t, wgt, sgu_weights, conv_full, later_shards)
    w_out_full = w_out_blocks.reshape(D_MODEL, D_MODEL)
    w_gate_full = w_gate_blocks.reshape(D_MODEL, D_MODEL)
    w_proj_full = jnp.moveaxis(w_proj_blocks, 0, 1).reshape(PLE_DIM, D_MODEL)
    qkv3 = b_qkv.reshape(nb, s, 3 * DN_WIDTH)
    conv_out = conv_out.reshape(nb, s, 3 * DN_WIDTH)
    z3 = b_z.reshape(nb, s, DN_WIDTH)
    l3 = b_l.reshape(nb, s, GATE_PAD)
    b_out, states, inverses = _dn_fwd(conv_out, z3, l3, alog, dtb, og)

    d_a, d_b, dh1, g_w_out, g_gate, g_proj, g_ple_g, g_fin_g, loss_tile = _head(
        a_out, b_out.reshape(t, DN_WIDTH), x2, p.reshape(t, PLE_DIM), loss_target.reshape(t, D_MODEL),
        w_out_full, w_gate_full, w_proj_full, ple_norm_g, fin_g)
    d_qkv, d_z, d_l, g_conv, g_alog, g_dtb, g_og, *head_received = _dn_bwd(
        qkv3, conv_out, z3, l3, conv_full, alog, dtb, og, states, inverses, d_b.reshape(nb, s, DN_WIDTH),
        [g_w_out, g_gate, g_proj])
    grad_x, g_w_in, g_norm, g_ln_g, g_ln_b, g_ws, g_bt = _inproj_bwd(
        x2, dh1, a_uvz, d_a, d_qkv.reshape(t, 3 * DN_WIDTH), d_z.reshape(t, DN_WIDTH), d_l.reshape(t, GATE_PAD),
        norm_g, sgu_weights, w_in_full_t, wgt)

    small = _pack_rows([g_conv, g_norm, g_ln_g, g_ln_b, g_ws, g_bt.T, g_alog[:, DN_HEADS:2 * DN_HEADS], g_dtb[:, DN_HEADS:2 * DN_HEADS], g_og,
                        g_ple_g, g_fin_g, (0.5 / D_MODEL) * loss_tile[0:1, 0:1]], SMALL_ROWS)
    w_in_received, small_received = _reduce_exchange(g_w_in, small)

    results = {}
    outs = _reduce_adamw(w_in_received, w_in_t, m_in_t, v_in_t, "adamw_w_in", 4 * LANES)
    results["w_in"] = [jnp.transpose(a.reshape(IN_SHARD, 1, D_MODEL), (1, 2, 0)) for a in outs]
    for name, recv in zip(("w_out", "ple_gate_w", "ple_proj_w"), head_received):
        results[name] = _reduce_adamw(recv, weights[name], mom1[name], mom2[name], "adamw_" + name)
    names = [name for name, _ in REPLICATED]
    two_d = lambda a: a.reshape(1, -1) if a.ndim == 1 else a
    g_conv_sum, loss_sum, *flat_outs = _adamw_replicated(
        small_received, *[[two_d(src[k]) for k in names] for src in (weights, mom1, mom2)])
    for i, k in enumerate(names):
        results[k] = [a.reshape(weights[k].shape) for a in flat_outs[4 * i:4 * i + 4]]
    loss = loss_sum[0, 0]
    me = 4 * lax.axis_index("x") + 2 * lax.axis_index("y") + lax.axis_index("c")
    conv_mine = lax.dynamic_slice(g_conv_sum, (0, me * 192), (CONV_K, 192))
    results["dn_conv_w"] = _reduce_adamw(conv_mine[None], dn_conv_w, m_dn_conv_w, v_dn_conv_w, "adamw_dn_conv_w")

    return (loss, grad_x.reshape(nb, s, D_MODEL), *[results[k][0] for k in WEIGHT_ORDER],
            *[results[k][1] for k in WEIGHT_ORDER], *[results[k][2] for k in WEIGHT_ORDER],
            *[results[k][3] for k in WEIGHT_ORDER])
```

```python
import functools

import jax
import jax.numpy as jnp
from jax import lax
from jax.experimental import pallas as pl
from jax.experimental.pallas import tpu as pltpu

F32 = jnp.float32
BF16 = jnp.bfloat16

N_DEV = 8
D_MODEL = 1024
SGU_WIDTH = 512
SGU_GROUPS = 4
SGU_CHUNK = 128
DN_WIDTH = 512
DN_HEADS = 4
DN_HEAD_DIM = 128
DN_CHUNK = 128
CONV_K = 4
CONV_HALO = 8
PLE_DIM = 256
EPS = 1e-6
IN_COLS = 3592
IN_SHARD = IN_COLS // N_DEV
GATE_PAD = 128
IN_GROUPS = (3 * SGU_WIDTH, 3 * DN_WIDTH, DN_WIDTH)

ADAM_LR = 0.001
ADAM_B1 = 0.9
ADAM_B2 = 0.999
ADAM_EPS = 1e-08
ADAM_WD = 0.01
ADAM_STEP = 10

LANES = 128
VMEM_LIMIT = 56 * 1024 * 1024
MESH = pl.DeviceIdType.MESH

REPLICATED = (("norm_g", (1, D_MODEL)), ("sgu_ln_g", (1, SGU_WIDTH)), ("sgu_ln_b", (1, SGU_WIDTH)),
              ("sgu_w_s", (1, SGU_GROUPS, SGU_CHUNK, SGU_CHUNK)), ("sgu_b_s", (1, SGU_GROUPS, SGU_CHUNK)),
              ("dn_a_log", (1, DN_HEADS)), ("dn_dt_bias", (1, DN_HEADS)), ("dn_o_norm_g", (1, DN_HEAD_DIM)),
              ("ple_norm_g", (1, D_MODEL)), ("final_norm_g", (D_MODEL,)))
WEIGHT_ORDER = ("norm_g", "w_in", "sgu_ln_g", "sgu_ln_b", "sgu_w_s", "sgu_b_s", "dn_conv_w", "dn_a_log",
                "dn_dt_bias", "dn_o_norm_g", "w_out", "ple_norm_g", "ple_gate_w", "ple_proj_w", "final_norm_g")


def _size(shape):
    n = 1
    for s in shape:
        n *= s
    return n


SMALL_LAYOUT = (("conv", (CONV_K, 3 * DN_WIDTH)),) + REPLICATED + (("loss", (1,)),)
SMALL_PIECE_ROWS = tuple(-(-_size(s) // LANES) for _, s in SMALL_LAYOUT)
SMALL_ROWS = -(-sum(SMALL_PIECE_ROWS) // 8) * 8


def _bdot(a, b):
    return jnp.dot(a.astype(BF16), b.astype(BF16), preferred_element_type=F32)


def _sigmoid(x):
    return 0.5 * jnp.tanh(0.5 * x) + 0.5


@jax.custom_vjp
def _silu(x):
    return x * _sigmoid(x)


def _silu_fwd(x):
    s = _sigmoid(x)
    return x * s, (x, s)


def _silu_bwd(res, ct):
    x, s = res
    return (ct * (s * (1.0 + x * (1.0 - s))),)


_silu.defvjp(_silu_fwd, _silu_bwd)


def _normal_cdf(x):
    return 0.5 + 0.5 * lax.erf(x * (0.5 ** 0.5))


@jax.custom_vjp
def _gelu(x):
    return x * _normal_cdf(x)


def _gelu_fwd(x):
    cdf = _normal_cdf(x)
    return x * cdf, (x, cdf)


def _gelu_bwd(res, ct):
    x, cdf = res
    pdf = jnp.exp(-0.5 * x * x) * ((2.0 * jnp.pi) ** -0.5)
    return (ct * (cdf + x * pdf),)


_gelu.defvjp(_gelu_fwd, _gelu_bwd)


def _softplus(x):
    return jnp.maximum(x, 0.0) + jnp.log1p(jnp.exp(-jnp.abs(x)))


@jax.custom_vjp
def _l2n(x):
    return x * lax.rsqrt(jnp.sum(x * x, axis=-1, keepdims=True) + EPS)


def _l2n_fwd(x):
    r = lax.rsqrt(jnp.sum(x * x, axis=-1, keepdims=True) + EPS)
    n = x * r
    return n, (n, r)


def _l2n_bwd(res, ct):
    n, r = res
    return (r * (ct - n * jnp.sum(ct * n, axis=-1, keepdims=True)),)


_l2n.defvjp(_l2n_fwd, _l2n_bwd)


def _rms(x):
    r = lax.rsqrt(jnp.mean(x * x, axis=-1, keepdims=True) + EPS)
    return x * r, r


def _rms_bwd(dn, n, r):
    return r * (dn - n * jnp.mean(dn * n, axis=-1, keepdims=True))


@jax.custom_vjp
def _rms_normed(x):
    return _rms(x)[0]


def _rms_normed_fwd(x):
    n, r = _rms(x)
    return n, (n, r)


def _rms_normed_bwd(res, ct):
    return (_rms_bwd(ct, *res),)


_rms_normed.defvjp(_rms_normed_fwd, _rms_normed_bwd)


def _onehot_row(idx, width):
    return (lax.broadcasted_iota(jnp.int32, (1, width), 1) == idx).astype(F32)


def _rowsum(x):
    return jnp.sum(x, axis=0, keepdims=True)


def _iota2(n):
    return lax.broadcasted_iota(jnp.int32, (n, n), 0), lax.broadcasted_iota(jnp.int32, (n, n), 1)


def _bmm(a, b):
    return lax.dot_general(a.astype(BF16), b.astype(BF16), (((2,), (1,)), ((0,), (0,))), preferred_element_type=F32)


def _bmm_nt(a, b):
    return lax.dot_general(a.astype(BF16), b.astype(BF16), (((2,), (2,)), ((0,), (0,))), preferred_element_type=F32)


def _bmm_tn(a, b):
    return lax.dot_general(a.astype(BF16), b.astype(BF16), (((1,), (1,)), ((0,), (0,))), preferred_element_type=F32)


def _tri_inv_impl(a):
    n = a.shape[-1]
    r, c = _iota2(n)
    x = r ^ c
    eye = (r == c).astype(F32)
    ad = jnp.where(x < 16, a, 0.0)
    p2 = _bmm(ad, ad)
    e = p2 - ad - _bmm(ad, p2)
    p4 = _bmm(p2, p2)
    e = e + p4 + _bmm(e, p4)
    p8 = _bmm(p4, p4)
    e = e + p8 + _bmm(e, p8)
    size = 16
    while size < n:
        m = jnp.where(jnp.logical_and(x < 2 * size, x >= size), a, 0.0)
        f = m + _bmm(m, e)
        e = e - f - _bmm(e, f)
        size *= 2
    return e + eye


@jax.custom_vjp
def _tri_inv(a, known):
    return _tri_inv_impl(a) if known is None else known


def _tri_inv_fwd(a, known):
    t = _tri_inv(a, known)
    return t, (t, known)


def _tri_inv_bwd(res, dt):
    t, known = res
    return -_bmm_tn(t, _bmm_nt(dt, t)), None if known is None else jnp.zeros_like(known)


_tri_inv.defvjp(_tri_inv_fwd, _tri_inv_bwd)


@jax.custom_vjp
def _standardized(x):
    xc = x - jnp.mean(x, axis=-1, keepdims=True)
    return xc * lax.rsqrt(jnp.mean(xc * xc, axis=-1, keepdims=True) + EPS)


def _standardized_fwd(x):
    xc = x - jnp.mean(x, axis=-1, keepdims=True)
    rstd = lax.rsqrt(jnp.mean(xc * xc, axis=-1, keepdims=True) + EPS)
    y = xc * rstd
    return y, (y, rstd)


def _standardized_bwd(res, ct):
    y, rstd = res
    return (rstd * (ct - jnp.mean(ct, axis=-1, keepdims=True) - y * jnp.mean(ct * y, axis=-1, keepdims=True)),)


_standardized.defvjp(_standardized_fwd, _standardized_bwd)


def _sgu_core(u, v, z, lg, lb, ws, bcol):
    n = ws.shape[0]
    r, c = _iota2(n)
    wm = jnp.where(r >= c, ws, 0.0)
    gu = _gelu(u)
    gv = _gelu(v)
    ln = _standardized(gv) * lg + lb
    s = _bdot(wm, ln) + bcol
    return gu * s * _silu(z)


def _lanes_of(x):
    return jnp.concatenate([x[i] for i in range(x.shape[0])], axis=1)


def _batch_of(x, width):
    return jnp.concatenate([x[None, :, i * width:(i + 1) * width] for i in range(x.shape[1] // width)], axis=0)


def _mask_dot(mask, x):
    hi = x.astype(BF16)
    lo = (x - hi.astype(F32)).astype(BF16)
    m = mask.astype(BF16)
    return jnp.dot(m, hi, preferred_element_type=F32) + jnp.dot(m, lo, preferred_element_type=F32)


def _split_dot(x, mask, dims):
    hi = x.astype(BF16)
    lo = (x - hi.astype(F32)).astype(BF16)
    m = mask.astype(BF16)
    return (lax.dot_general(hi, m, dims, preferred_element_type=F32)
            + lax.dot_general(lo, m, dims, preferred_element_type=F32))


def _lane_select(lanes, blocks, first_lane):
    src = lax.broadcasted_iota(jnp.int32, (lanes, blocks * LANES), 0)
    dst = lax.broadcasted_iota(jnp.int32, (lanes, blocks * LANES), 1) // LANES
    return src == dst + first_lane


def _pick_lanes(x, blocks, first_lane):
    return _pick_lanes_vjp(blocks, first_lane, x)


@functools.partial(jax.custom_vjp, nondiff_argnums=(0, 1))
def _pick_lanes_vjp(blocks, first_lane, x):
    return _split_dot(x, _lane_select(x.shape[-1], blocks, first_lane), (((1,), (0,)), ((), ())))


def _pick_lanes_fwd(blocks, first_lane, x):
    return _pick_lanes_vjp(blocks, first_lane, x), x.shape[-1]


def _pick_lanes_bwd(blocks, first_lane, lanes, ct):
    return (sum(jnp.sum(ct[:, h * LANES:(h + 1) * LANES], axis=-1, keepdims=True) * _onehot_row(first_lane + h, lanes)
                for h in range(blocks)),)


_pick_lanes_vjp.defvjp(_pick_lanes_fwd, _pick_lanes_bwd)


def _tri_mask(n, upper):
    r, c = _iota2(n)
    return (r <= c) if upper else (r >= c)


@jax.custom_vjp
def _cumsum_rows(x):
    return _mask_dot(_tri_mask(x.shape[0], False), x)


def _cumsum_rows_fwd(x):
    return _cumsum_rows(x), None


def _cumsum_rows_bwd(_, ct):
    return (_mask_dot(_tri_mask(ct.shape[0], True), ct),)


_cumsum_rows.defvjp(_cumsum_rows_fwd, _cumsum_rows_bwd)


@jax.custom_vjp
def _colsum_all_rows(x):
    return _mask_dot(jnp.ones((x.shape[0], x.shape[0]), jnp.bool_), x)


def _colsum_all_rows_fwd(x):
    return _colsum_all_rows(x), None


def _colsum_all_rows_bwd(_, ct):
    return (_mask_dot(jnp.ones((ct.shape[0], ct.shape[0]), jnp.bool_), ct),)


_colsum_all_rows.defvjp(_colsum_all_rows_fwd, _colsum_all_rows_bwd)


def _dn_core(cq, ck, cv, z, logits, state, alog, dtb, og, t_known=None):
    gn, cn, dh = cq.shape
    heads = gn // logits.shape[0]
    q = _l2n(_silu(cq)) * (dh ** -0.5)
    k = _l2n(_silu(ck))
    v = _silu(cv)
    beta_lanes = _sigmoid(logits)
    g_lanes = -jnp.exp(alog) * _softplus(logits + dtb)
    beta_all = jnp.concatenate([_pick_lanes(beta_lanes[b], heads, 0) for b in range(logits.shape[0])], axis=1)
    g_all = jnp.concatenate([_pick_lanes(g_lanes[b], heads, heads) for b in range(logits.shape[0])], axis=1)
    beta = _batch_of(beta_all, dh)
    g_wide = _batch_of(g_all, dh)
    r, c = _iota2(cn)
    tril = r >= c
    rw = lax.broadcasted_iota(jnp.int32, (cn, dh), 0)
    cw = lax.broadcasted_iota(jnp.int32, (cn, dh), 1)
    upper_wide = (rw <= cw).astype(F32)
    gc_wide = _batch_of(_cumsum_rows(g_all), dh)
    gc_cols = _batch_of(_colsum_all_rows(_lanes_of(g_wide * upper_wide)), dh)[:, :, :cn]
    decay = jnp.exp(jnp.where(tril, gc_wide[:, :, :cn] - gc_cols, -1e30))
    kb = k * beta
    kk = _bmm_nt(kb, k) * decay
    t = _tri_inv(jnp.where(r > c, kk, 0.0), t_known)
    eg = jnp.exp(gc_wide)
    sol = _bmm(t, jnp.concatenate([v * beta, kb * eg], axis=-1))
    u_val, w_dec = sol[:, :, :dh], sol[:, :, dh:]
    qk = _bmm_nt(q, k) * decay
    g_last = jnp.sum(g_wide, axis=1, keepdims=True)
    k_dec = k * jnp.exp(g_last - gc_wide)
    ws = _bmm(jnp.concatenate([w_dec, q * eg], axis=1), state)
    v_new = u_val - ws[:, :cn]
    o = ws[:, cn:] + _bmm(qk, v_new)
    new_state = state * jnp.exp(g_last) + _bmm_tn(k_dec, v_new)
    return _rms_normed(o) * og * _silu(z), new_state, t


N_CHIPS = 4
HBM_SPEC = pl.BlockSpec(memory_space=pl.ANY)


def _place():
    return lax.axis_index("x"), lax.axis_index("y"), lax.axis_index("c")


def _other_chip(k):
    x, y, _ = _place()
    px = 1 - x if k & 2 else x
    py = 1 - y if k & 1 else y
    return px, py, 2 * px + py


def _remote(src, dst, send_sem, recv_sem, device):
    return pltpu.make_async_remote_copy(src_ref=src, dst_ref=dst, send_sem=send_sem, recv_sem=recv_sem,
                                        device_id=device, device_id_type=MESH)


def _other_device(k):
    x, y, c = _place()
    px = 1 - x if k & 4 else x
    py = 1 - y if k & 2 else y
    pc = 1 - c if k & 1 else c
    return (px, py, pc), 4 * px + 2 * py + pc


def _direct_exchange(srcs, outs, send_sems, recv_sems, local_sems, gather):
    x, y, c = _place()
    me = 4 * x + 2 * y + c

    def copies(arriving):
        out_list = []
        for a, (src, out) in enumerate(zip(srcs, outs)):
            for k in range(1, N_DEV):
                peer, index = _other_device(k)
                mine = src if gather else src.at[index]
                out_list.append(_remote(mine, out.at[index if arriving else me], send_sems.at[a, k - 1],
                                        recv_sems.at[a, k - 1], peer))
        return out_list

    def local_copies():
        return [pltpu.make_async_copy(src if gather else src.at[me], out.at[me], local_sems.at[a])
                for a, (src, out) in enumerate(zip(srcs, outs))]

    def start():
        for cp in local_copies() + copies(False):
            cp.start()

    def wait():
        for cp in copies(True):
            cp.wait_recv()
        for cp in copies(False):
            cp.wait_send()
        for cp in local_copies():
            cp.wait()

    return start, wait


def _exchange_scratch(n):
    return [pltpu.SemaphoreType.DMA((n, N_DEV - 1)), pltpu.SemaphoreType.DMA((n, N_DEV - 1)), pltpu.SemaphoreType.DMA((n,))]


def _all_gather(shards):
    n = len(shards)

    def body(*refs):
        srcs, outs = refs[:n], refs[n:2 * n]
        send_sems, recv_sems, local_sems = refs[2 * n:]
        x, y, c = _place()
        me = 4 * x + 2 * y + c
        sibling = (x, y, 1 - c)
        local = [pltpu.make_async_copy(srcs[a], outs[a].at[me], local_sems.at[a]) for a in range(n)]
        for cp in local:
            cp.start()
        sends = []
        for a in range(n):
            sends.append(_remote(srcs[a], outs[a].at[me], send_sems.at[a, 0], recv_sems.at[a, 0], sibling))
        for k in range(1, N_CHIPS):
            px, py, _ = _other_chip(k)
            for a in range(n):
                sends.append(_remote(srcs[a], outs[a].at[me], send_sems.at[a, k], recv_sems.at[a, k], (px, py, c)))
        for cp in sends:
            cp.start()
        passed = []
        for k in range(1, N_CHIPS):
            px, py, _ = _other_chip(k)
            blk = 4 * px + 2 * py + c
            for a in range(n):
                _remote(srcs[a], outs[a].at[blk], send_sems.at[a, k], recv_sems.at[a, k], (px, py, c)).wait_recv()
            for a in range(n):
                cp = _remote(outs[a].at[blk], outs[a].at[blk], send_sems.at[a, 3 + k], recv_sems.at[a, 3 + k], sibling)
                cp.start()
                passed.append(cp)
        for a in range(n):
            _remote(srcs[a], outs[a].at[me + 1 - 2 * c], send_sems.at[a, 0], recv_sems.at[a, 0], sibling).wait_recv()
        for k in range(1, N_CHIPS):
            px, py, _ = _other_chip(k)
            blk = 4 * px + 2 * py + 1 - c
            for a in range(n):
                _remote(srcs[a], outs[a].at[blk], send_sems.at[a, 3 + k], recv_sems.at[a, 3 + k], sibling).wait_recv()
        for cp in sends + passed:
            cp.wait_send()
        for cp in local:
            cp.wait()

    return pl.pallas_call(
        body, name="all_gather_weights",
        out_shape=tuple(jax.ShapeDtypeStruct((N_DEV,) + a.shape, a.dtype) for a in shards),
        in_specs=[HBM_SPEC] * n, out_specs=(HBM_SPEC,) * n,
        scratch_shapes=[pltpu.SemaphoreType.DMA((n, N_DEV - 1)), pltpu.SemaphoreType.DMA((n, N_DEV - 1)),
                        pltpu.SemaphoreType.DMA((n,))],
    )(*shards)


def _reduce_exchange(by_device, small):
    _, rows, cols = by_device.shape

    def body(g_ref, small_ref, out_ref, small_out_ref, from_sibling, small_from_sibling, stage, sums, small_own, small_sum,
             pair_send, pair_recv, chip_send, chip_recv, local_sems):
        x, y, c = _place()
        mine = 2 * x + y
        sibling = (x, y, 1 - c)
        chips = [(x, y, mine)] + [_other_chip(k) for k in range(1, N_CHIPS)]
        to_sibling = [_remote(g_ref.at[2 * chips[k][2] + 1 - c], from_sibling.at[k], pair_send.at[k], pair_recv.at[k], sibling)
                      for k in range(N_CHIPS)]
        to_sibling.append(_remote(small_ref, small_from_sibling, pair_send.at[N_CHIPS], pair_recv.at[N_CHIPS], sibling))
        for cp in to_sibling:
            cp.start()
        small_mine = pltpu.make_async_copy(small_ref, small_own, local_sems.at[0])
        small_mine.start()
        to_chips = []
        for k in (1, 2, 3, 0):
            px, py, chip = chips[k]
            mine_k = pltpu.make_async_copy(g_ref.at[2 * chip + c], stage, local_sems.at[1])
            mine_k.start()
            to_sibling[k].wait_recv()
            mine_k.wait()
            sums[k] = (stage[...] + from_sibling[k]).astype(sums.dtype)
            if k:
                cp = _remote(sums.at[k], out_ref.at[mine], chip_send.at[0, k - 1], chip_recv.at[0, k - 1], (px, py, c))
                cp.start()
                to_chips.append(cp)
        own_block = pltpu.make_async_copy(sums.at[0], out_ref.at[mine], local_sems.at[2])
        own_block.start()
        to_sibling[N_CHIPS].wait_recv()
        small_mine.wait()
        small_sum[...] = small_own[...] + small_from_sibling[...]
        for k in range(1, N_CHIPS):
            px, py, _ = chips[k]
            cp = _remote(small_sum, small_out_ref.at[mine], chip_send.at[1, k - 1], chip_recv.at[1, k - 1], (px, py, c))
            cp.start()
            to_chips.append(cp)
        own_small = pltpu.make_async_copy(small_sum, small_out_ref.at[mine], local_sems.at[3])
        own_small.start()
        for k in range(1, N_CHIPS):
            px, py, chip = chips[k]
            _remote(sums.at[k], out_ref.at[chip], chip_send.at[0, k - 1], chip_recv.at[0, k - 1], (px, py, c)).wait_recv()
            _remote(small_sum, small_out_ref.at[chip], chip_send.at[1, k - 1], chip_recv.at[1, k - 1], (px, py, c)).wait_recv()
        for cp in to_sibling + to_chips:
            cp.wait_send()
        own_block.wait()
        own_small.wait()

    return pl.pallas_call(
        body, name="grad_reduce_exchange",
        out_shape=(jax.ShapeDtypeStruct((N_CHIPS, rows, cols), BF16), jax.ShapeDtypeStruct((N_CHIPS,) + small.shape, F32)),
        in_specs=[HBM_SPEC, HBM_SPEC], out_specs=(HBM_SPEC, HBM_SPEC),
        scratch_shapes=[pltpu.VMEM((N_CHIPS, rows, cols), F32), pltpu.VMEM(small.shape, F32), pltpu.VMEM((rows, cols), F32),
                        pltpu.VMEM((N_CHIPS, rows, cols), BF16), pltpu.VMEM(small.shape, F32), pltpu.VMEM(small.shape, F32),
                        pltpu.SemaphoreType.DMA((N_CHIPS + 1,)), pltpu.SemaphoreType.DMA((N_CHIPS + 1,)),
                        pltpu.SemaphoreType.DMA((2, N_CHIPS - 1)), pltpu.SemaphoreType.DMA((2, N_CHIPS - 1)),
                        pltpu.SemaphoreType.DMA((4,))],
        compiler_params=pltpu.CompilerParams(vmem_limit_bytes=VMEM_LIMIT),
    )(by_device, small)


def _params(n_axes):
    return pltpu.CompilerParams(dimension_semantics=("arbitrary",) * n_axes, vmem_limit_bytes=VMEM_LIMIT)


def _whole(shape):
    return pl.BlockSpec(shape, lambda *_: (0,) * len(shape))


VMEM_SPEC = pl.BlockSpec(memory_space=pltpu.VMEM)


def _inproj_fwd(x2, seq_len, norm_g, wt, wgt, sgu_weights, conv_w, later_shards):
    t = x2.shape[0]
    tm = min(512, seq_len)
    tiles_per_seq = seq_len // tm
    steps = t // tm
    ns = len(later_shards)

    widths = IN_GROUPS + (wgt.shape[0],)
    starts = (0, IN_GROUPS[0], IN_GROUPS[0] + IN_GROUPS[1], 0)

    def body(x_ref, g_ref, wt_ref, wg_ref, lg_ref, lb_ref, ws_ref, bt_ref, cw_ref, *rest):
        shard_refs, rest = rest[:ns], rest[ns:]
        a_ref, q_ref, z_ref, l_ref, sgu_ref, c_ref = rest[:6]
        gathered_refs, (xpad_ref, send_sems, recv_sems, local_sems) = rest[6:6 + ns], rest[6 + ns:]
        start_gather, wait_gather = _direct_exchange(shard_refs, gathered_refs, send_sems, recv_sems, local_sems, True)
        pl.when(pl.program_id(0) == 0)(start_gather)
        n, _ = _rms(x_ref[...])
        xn = (n * g_ref[...]).astype(BF16)
        for w_ref, row0, width, o_ref in zip((wt_ref, wt_ref, wt_ref, wg_ref), starts, widths, (a_ref, q_ref, z_ref, l_ref)):
            for c0 in range(0, width, 512):
                c1 = min(c0 + 512, width)
                o_ref[:, c0:c1] = lax.dot_general(xn, w_ref[row0 + c0:row0 + c1, :], (((1,), (1,)), ((), ())),
                                                  preferred_element_type=F32)
        for row0 in range(0, tm, SGU_CHUNK):
            for grp in range(SGU_GROUPS):
                args = _sgu_pieces(a_ref, lg_ref, lb_ref, ws_ref, bt_ref, row0, grp)
                sgu_ref[pl.ds(row0, SGU_CHUNK), pl.ds(grp * 128, 128)] = _sgu_core(*args).astype(sgu_ref.dtype)

        @pl.when(pl.program_id(0) % tiles_per_seq == 0)
        def _():
            xpad_ref[0:CONV_HALO, :] = jnp.zeros((CONV_HALO, xpad_ref.shape[1]), F32)

        xpad_ref[CONV_HALO:, :] = q_ref[...]
        acc = None
        for j in range(CONV_K):
            term = cw_ref[j:j + 1, :] * xpad_ref[pl.ds(CONV_HALO - CONV_K + 1 + j, tm), :]
            acc = term if acc is None else acc + term
        c_ref[...] = acc
        xpad_ref[0:CONV_HALO, :] = xpad_ref[tm:tm + CONV_HALO, :]
        pl.when(pl.program_id(0) == steps - 1)(wait_gather)

    tile = lambda w: pl.BlockSpec((tm, w), lambda i: (i, 0))
    sgu_shapes = ((1, SGU_WIDTH), (1, SGU_WIDTH), (SGU_GROUPS, SGU_CHUNK, SGU_CHUNK), (SGU_CHUNK, SGU_GROUPS))
    return pl.pallas_call(
        body, name="inproj_sgu_conv_fwd", grid=(steps,),
        out_shape=tuple(jax.ShapeDtypeStruct((t, w), F32) for w in widths)
        + (jax.ShapeDtypeStruct((t, SGU_WIDTH), BF16), jax.ShapeDtypeStruct((t, widths[1]), F32))
        + tuple(jax.ShapeDtypeStruct((N_DEV,) + a.shape, a.dtype) for a in later_shards),
        in_specs=[tile(D_MODEL), _whole((1, D_MODEL)), VMEM_SPEC, VMEM_SPEC]
        + [_whole(s) for s in sgu_shapes] + [_whole((CONV_K, widths[1]))] + [HBM_SPEC] * ns,
        out_specs=tuple(tile(w) for w in widths) + (tile(SGU_WIDTH), tile(widths[1])) + (HBM_SPEC,) * ns,
        scratch_shapes=[pltpu.VMEM((CONV_HALO + tm, widths[1]), F32)] + _exchange_scratch(ns),
        compiler_params=_params(1),
    )(x2, norm_g, wt, wgt, *sgu_weights, conv_w, *later_shards)


def _sgu_pieces(uvz_ref, lg_ref, lb_ref, ws_ref, bt_ref, row0, grp):
    rows = pl.ds(row0, SGU_CHUNK)
    lanes = pl.ds(grp * 128, 128)
    u = uvz_ref[rows, pl.ds(grp * 128, 128)]
    v = uvz_ref[rows, pl.ds(SGU_WIDTH + grp * 128, 128)]
    z = uvz_ref[rows, pl.ds(2 * SGU_WIDTH + grp * 128, 128)]
    bcol = jnp.sum(bt_ref[...] * _onehot_row(grp, SGU_GROUPS), axis=-1, keepdims=True)
    return u, v, z, lg_ref[:, lanes], lb_ref[:, lanes], ws_ref[grp], bcol


def _sgu_bwd_tile(uvz_ref, do_ref, sgu_refs, duvz_ref, grad_refs):
    lg_ref, lb_ref, ws_ref, bt_ref = sgu_refs
    dlg_ref, dlb_ref, dws_ref, dbt_ref = grad_refs
    for row0 in range(0, uvz_ref.shape[0], SGU_CHUNK):
        rows = pl.ds(row0, SGU_CHUNK)
        for grp in range(SGU_GROUPS):
            lanes = pl.ds(grp * 128, 128)
            args = _sgu_pieces(uvz_ref, lg_ref, lb_ref, ws_ref, bt_ref, row0, grp)
            _, pull = jax.vjp(_sgu_core, *args)
            du, dv, dz, dlg, dlb, dws, dbcol = pull(do_ref[rows, lanes])
            duvz_ref[rows, pl.ds(grp * 128, 128)] = du.astype(duvz_ref.dtype)
            duvz_ref[rows, pl.ds(SGU_WIDTH + grp * 128, 128)] = dv.astype(duvz_ref.dtype)
            duvz_ref[rows, pl.ds(2 * SGU_WIDTH + grp * 128, 128)] = dz.astype(duvz_ref.dtype)
            dlg_ref[:, lanes] += dlg
            dlb_ref[:, lanes] += dlb
            dws_ref[grp] += dws
            dbt_ref[...] += dbcol * _onehot_row(grp, SGU_GROUPS)


def _dn_pairs(nb):
    return [(b, h) for b in range(nb) for h in range(DN_HEADS)]


def _dn_batch_args(c_ref, z_ref):
    pairs = _dn_pairs(c_ref.shape[0])
    pick = lambda ref, b, col: ref[b, :, pl.ds(col, DN_HEAD_DIM)]
    cq = jnp.stack([pick(c_ref, b, h * DN_HEAD_DIM) for b, h in pairs])
    ck = jnp.stack([pick(c_ref, b, DN_WIDTH + h * DN_HEAD_DIM) for b, h in pairs])
    cv = jnp.stack([pick(c_ref, b, 2 * DN_WIDTH + h * DN_HEAD_DIM) for b, h in pairs])
    z = jnp.stack([pick(z_ref, b, h * DN_HEAD_DIM) for b, h in pairs])
    return cq, ck, cv, z


def _dn_weight_specs():
    return [_whole((CONV_K, 3 * DN_WIDTH)), _whole((1, GATE_PAD)), _whole((1, GATE_PAD)), _whole((1, DN_HEAD_DIM))]


def _dn_fwd(conv_out, zg, logits, alog, dtb, og):
    nb, s, _ = conv_out.shape
    nc = s // DN_CHUNK
    pairs = _dn_pairs(nb)
    gn = len(pairs)
    chunk = lambda w: pl.BlockSpec((nb, DN_CHUNK, w), lambda n: (0, n, 0))

    def body(c_ref, z_ref, l_ref, alog_ref, dtb_ref, og_ref, out_ref, st_ref, inv_ref, state_ref):
        n = pl.program_id(0)

        @pl.when(n == 0)
        def _():
            state_ref[...] = jnp.zeros_like(state_ref)

        cq, ck, cv, z = _dn_batch_args(c_ref, z_ref)
        state = state_ref[...]
        st_ref[...] = state
        out, new_state, t = _dn_core(cq, ck, cv, z, l_ref[...], state, alog_ref[...], dtb_ref[...], og_ref[...])
        state_ref[...] = new_state
        inv_ref[...] = t.astype(inv_ref.dtype)
        for i, (b, h) in enumerate(pairs):
            out_ref[b, :, pl.ds(h * DN_HEAD_DIM, DN_HEAD_DIM)] = out[i].astype(out_ref.dtype)

    per_chunk = pl.BlockSpec((None, gn, DN_HEAD_DIM, DN_HEAD_DIM), lambda n: (n, 0, 0, 0))
    return pl.pallas_call(
        body, name="deltanet_fwd", grid=(nc,),
        out_shape=(jax.ShapeDtypeStruct((nb, s, DN_WIDTH), BF16),
                   jax.ShapeDtypeStruct((nc, gn, DN_HEAD_DIM, DN_HEAD_DIM), F32),
                   jax.ShapeDtypeStruct((nc, gn, DN_CHUNK, DN_CHUNK), BF16)),
        in_specs=[chunk(3 * DN_WIDTH), chunk(DN_WIDTH), chunk(GATE_PAD)] + _dn_weight_specs()[1:],
        out_specs=(chunk(DN_WIDTH), per_chunk, pl.BlockSpec((None, gn, DN_CHUNK, DN_CHUNK), lambda n: (n, 0, 0, 0))),
        scratch_shapes=[pltpu.VMEM((gn, DN_HEAD_DIM, DN_HEAD_DIM), F32)],
        compiler_params=_params(1),
    )(conv_out, zg, logits, alog, dtb, og)


def _dn_bwd(qkv, conv_out, zg, logits, conv_w, alog, dtb, og, states, inverses, d_out, head_grads):
    nb, s, _ = qkv.shape
    nc = s // DN_CHUNK
    rev = lambda n: nc - 1 - n
    pairs = _dn_pairs(nb)
    gn = len(pairs)
    ng = len(head_grads)

    def body(cur_ref, c_ref, z_ref, l_ref, w_ref, alog_ref, dtb_ref, og_ref, st_ref, inv_ref, do_ref, *rest):
        grad_refs, rest = rest[:ng], rest[ng:]
        dqkv_ref, dz_ref, dl_ref, dw_ref, dalog_ref, ddtb_ref, dog_ref = rest[:7]
        recv_refs, (dstate_ref, dcpad_ref, dw_part_ref, send_sems, recv_sems, local_sems) = rest[7:7 + ng], rest[7 + ng:]
        n = pl.program_id(0)
        start_exchange, wait_exchange = _direct_exchange(grad_refs, recv_refs, send_sems, recv_sems, local_sems, False)
        pl.when(n == 0)(start_exchange)

        @pl.when(n == 0)
        def _():
            dw_part_ref[...] = jnp.zeros_like(dw_part_ref)
            dalog_ref[...] = jnp.zeros_like(dalog_ref)
            ddtb_ref[...] = jnp.zeros_like(ddtb_ref)
            dog_ref[...] = jnp.zeros_like(dog_ref)
            dstate_ref[...] = jnp.zeros_like(dstate_ref)
            dcpad_ref[:, DN_CHUNK:, :] = jnp.zeros((nb, CONV_HALO, 3 * DN_WIDTH), F32)

        cq, ck, cv, z = _dn_batch_args(c_ref, z_ref)
        d_out_g = jnp.stack([do_ref[b, :, pl.ds(h * DN_HEAD_DIM, DN_HEAD_DIM)] for b, h in pairs])
        t_known = inv_ref[...].astype(F32)
        core = lambda *args: _dn_core(*args, t_known=t_known)[:2]
        _, pull = jax.vjp(core, cq, ck, cv, z, l_ref[...], st_ref[...], alog_ref[...], dtb_ref[...], og_ref[...])
        dcq, dck, dcv, dz, dlog, dstate, dalog, ddtb, dog = pull((d_out_g, dstate_ref[...]))
        dstate_ref[...] = dstate
        dl_ref[...] = dlog.astype(dl_ref.dtype)
        dalog_ref[...] += dalog
        ddtb_ref[...] += ddtb
        dog_ref[...] += dog
        for i, (b, h) in enumerate(pairs):
            dcpad_ref[b, 0:DN_CHUNK, pl.ds(h * DN_HEAD_DIM, DN_HEAD_DIM)] = dcq[i]
            dcpad_ref[b, 0:DN_CHUNK, pl.ds(DN_WIDTH + h * DN_HEAD_DIM, DN_HEAD_DIM)] = dck[i]
            dcpad_ref[b, 0:DN_CHUNK, pl.ds(2 * DN_WIDTH + h * DN_HEAD_DIM, DN_HEAD_DIM)] = dcv[i]
            dz_ref[b, :, pl.ds(h * DN_HEAD_DIM, DN_HEAD_DIM)] = dz[i].astype(dz_ref.dtype)
        for b in range(nb):
            xb = cur_ref[b]
            dx = None
            for j in range(CONV_K):
                shifted = dcpad_ref[b, pl.ds(CONV_K - 1 - j, DN_CHUNK), :]
                term = w_ref[j:j + 1, :] * shifted
                dx = term if dx is None else dx + term
                dw_part_ref[j] += jnp.sum((shifted * xb).reshape(DN_CHUNK // 8, 8, 3 * DN_WIDTH), axis=0)
            dqkv_ref[b] = dx.astype(dqkv_ref.dtype)
            dcpad_ref[b, DN_CHUNK:, :] = dcpad_ref[b, 0:CONV_HALO, :]

        @pl.when(n == nc - 1)
        def _():
            dw_ref[...] = jnp.sum(dw_part_ref[...], axis=1)

        pl.when(n == nc - 1)(wait_exchange)

    chunk = lambda w: pl.BlockSpec((nb, DN_CHUNK, w), lambda n: (0, rev(n), 0))
    return pl.pallas_call(
        body, name="deltanet_bwd", grid=(nc,),
        out_shape=(jax.ShapeDtypeStruct((nb, s, 3 * DN_WIDTH), BF16), jax.ShapeDtypeStruct((nb, s, DN_WIDTH), BF16),
                   jax.ShapeDtypeStruct((nb, s, GATE_PAD), BF16), jax.ShapeDtypeStruct((CONV_K, 3 * DN_WIDTH), F32),
                   jax.ShapeDtypeStruct((1, GATE_PAD), F32), jax.ShapeDtypeStruct((1, GATE_PAD), F32),
                   jax.ShapeDtypeStruct((1, DN_HEAD_DIM), F32))
        + tuple(jax.ShapeDtypeStruct(a.shape, a.dtype) for a in head_grads),
        in_specs=[chunk(3 * DN_WIDTH), chunk(3 * DN_WIDTH), chunk(DN_WIDTH), chunk(GATE_PAD)] + _dn_weight_specs() + [
            pl.BlockSpec((None, gn, DN_HEAD_DIM, DN_HEAD_DIM), lambda n: (rev(n), 0, 0, 0)),
            pl.BlockSpec((None, gn, DN_CHUNK, DN_CHUNK), lambda n: (rev(n), 0, 0, 0)),
            chunk(DN_WIDTH)] + [HBM_SPEC] * ng,
        out_specs=(chunk(3 * DN_WIDTH), chunk(DN_WIDTH), chunk(GATE_PAD), _whole((CONV_K, 3 * DN_WIDTH)),
                   _whole((1, GATE_PAD)), _whole((1, GATE_PAD)), _whole((1, DN_HEAD_DIM))) + (HBM_SPEC,) * ng,
        scratch_shapes=[pltpu.VMEM((gn, DN_HEAD_DIM, DN_HEAD_DIM), F32),
                        pltpu.VMEM((nb, DN_CHUNK + CONV_HALO, 3 * DN_WIDTH), F32),
                        pltpu.VMEM((CONV_K, 8, 3 * DN_WIDTH), F32)] + _exchange_scratch(ng),
        compiler_params=_params(1),
    )(qkv, conv_out, zg, logits, conv_w, alog, dtb, og, states, inverses, d_out, *head_grads)


def _head(a_out, b_out, x2, p2, target, w_out, w_gate, w_proj, ple_g, fin_g):
    t = x2.shape[0]
    tm = min(512, t)
    steps = t // tm

    def body(a_ref, b_ref, x_ref, p_ref, y_ref, wo_ref, wg_ref, wp_ref, pg_ref, fg_ref,
             da_ref, db_ref, dh_ref, dwo_hbm, dwg_hbm, dwp_hbm, dpg_ref, dfg_ref, loss_ref,
             dwo_acc, dwg_acc, dwp_acc, rows_stage, cols_stage):
        i = pl.program_id(0)

        @pl.when(i == 0)
        def _():
            dwo_acc[...] = jnp.zeros_like(dwo_acc)
            dwg_acc[...] = jnp.zeros_like(dwg_acc)
            dwp_acc[...] = jnp.zeros_like(dwp_acc)
            dpg_ref[...] = jnp.zeros_like(dpg_ref)
            dfg_ref[...] = jnp.zeros_like(dfg_ref)
            loss_ref[...] = jnp.zeros_like(loss_ref)

        a = a_ref[...]
        bb = b_ref[...]
        pb = p_ref[...].astype(BF16)
        pg = pg_ref[...]
        fg = fg_ref[...]
        h1 = (x_ref[...] + jnp.dot(a, wo_ref[0:SGU_WIDTH, :], preferred_element_type=F32)
              + jnp.dot(bb, wo_ref[SGU_WIDTH:, :], preferred_element_type=F32))
        n1, r1 = _rms(h1)
        rn = (n1 * pg).astype(BF16)
        gate = _sigmoid(jnp.dot(rn, wg_ref[...], preferred_element_type=F32))
        pp = jnp.dot(pb, wp_ref[...], preferred_element_type=F32)
        h2 = h1 + gate * pp
        n2, r2 = _rms(h2)
        err = n2 * fg - y_ref[...]
        loss_ref[...] += jnp.broadcast_to(_rowsum(jnp.sum(err * err, axis=-1, keepdims=True)), loss_ref.shape)

        dy = err * (1.0 / D_MODEL)
        dfg_ref[...] += _rowsum(dy * n2)
        dh2 = _rms_bwd(dy * fg, n2, r2)
        dpp = (dh2 * gate).astype(BF16)
        dgl = (dh2 * pp * gate * (1.0 - gate)).astype(BF16)
        dwp_acc[...] += lax.dot_general(pb, dpp, (((0,), (0,)), ((), ())), preferred_element_type=F32)
        dwg_acc[...] += lax.dot_general(rn, dgl, (((0,), (0,)), ((), ())), preferred_element_type=F32)
        nt = (((1,), (1,)), ((), ()))
        drn = lax.dot_general(dgl, wg_ref[...], nt, preferred_element_type=F32)
        dpg_ref[...] += _rowsum(drn * n1)
        dh1 = dh2 + _rms_bwd(drn * pg, n1, r1)
        dh_ref[...] = dh1
        dhb = dh1.astype(BF16)
        da_ref[...] = lax.dot_general(dhb, wo_ref[0:SGU_WIDTH, :], nt, preferred_element_type=F32)
        db_ref[...] = lax.dot_general(dhb, wo_ref[SGU_WIDTH:, :], nt, preferred_element_type=F32)
        dwo_acc[0:SGU_WIDTH, :] += lax.dot_general(a, dhb, (((0,), (0,)), ((), ())), preferred_element_type=F32)
        dwo_acc[SGU_WIDTH:, :] += lax.dot_general(bb, dhb, (((0,), (0,)), ((), ())), preferred_element_type=F32)

        @pl.when(i == steps - 1)
        def _():
            for j in range(N_DEV):
                for acc, hbm in ((dwo_acc, dwo_hbm), (dwg_acc, dwg_hbm)):
                    rows_stage[...] = acc[j * LANES:(j + 1) * LANES, :].astype(BF16)
                    pltpu.sync_copy(rows_stage, hbm.at[j])
                cols_stage[...] = dwp_acc[:, j * LANES:(j + 1) * LANES].astype(BF16)
                pltpu.sync_copy(cols_stage, dwp_hbm.at[j])

    tile = lambda w: pl.BlockSpec((tm, w), lambda i: (i, 0))
    return pl.pallas_call(
        body, name="head_fwd_bwd", grid=(steps,),
        out_shape=(jax.ShapeDtypeStruct((t, SGU_WIDTH), F32), jax.ShapeDtypeStruct((t, DN_WIDTH), F32),
                   jax.ShapeDtypeStruct((t, D_MODEL), F32), jax.ShapeDtypeStruct((N_DEV, LANES, D_MODEL), BF16),
                   jax.ShapeDtypeStruct((N_DEV, LANES, D_MODEL), BF16), jax.ShapeDtypeStruct((N_DEV, PLE_DIM, LANES), BF16),
                   jax.ShapeDtypeStruct((1, D_MODEL), F32), jax.ShapeDtypeStruct((1, D_MODEL), F32),
                   jax.ShapeDtypeStruct((8, LANES), F32)),
        in_specs=[tile(SGU_WIDTH), tile(DN_WIDTH), tile(D_MODEL), tile(PLE_DIM), tile(D_MODEL),
                  VMEM_SPEC, VMEM_SPEC, VMEM_SPEC, _whole((1, D_MODEL)), _whole((1, D_MODEL))],
        out_specs=(tile(SGU_WIDTH), tile(DN_WIDTH), tile(D_MODEL), HBM_SPEC, HBM_SPEC, HBM_SPEC,
                   _whole((1, D_MODEL)), _whole((1, D_MODEL)), _whole((8, LANES))),
        scratch_shapes=[pltpu.VMEM((D_MODEL, D_MODEL), F32), pltpu.VMEM((D_MODEL, D_MODEL), F32),
                        pltpu.VMEM((PLE_DIM, D_MODEL), F32), pltpu.VMEM((LANES, D_MODEL), BF16),
                        pltpu.VMEM((PLE_DIM, LANES), BF16)],
        compiler_params=_params(1),
    )(a_out, b_out, x2, p2, target, w_out, w_gate, w_proj, ple_g, fin_g)


def _inproj_bwd(x2, dh1, a_uvz, d_sgu, d_q, d_z, d_l, norm_g, sgu_weights, wt, wgt):
    t = x2.shape[0]
    tm = min(256, t)
    steps = t // tm

    widths = (a_uvz.shape[1], d_q.shape[1], d_z.shape[1], d_l.shape[1])
    starts = (0, widths[0], widths[0] + widths[1], widths[0] + widths[1] + widths[2])

    def body(x_ref, dh_ref, uvz_ref, dsgu_ref, dq_ref, dz_ref, dl_ref, g_ref, lg_ref, lb_ref, ws_ref, bt_ref,
             wt_ref, wgt_ref,
             dx_ref, dw_hbm, dg_ref, dlg_ref, dlb_ref, dws_ref, dbt_ref, dw_acc, stage_ref, da_ref):
        i = pl.program_id(0)

        @pl.when(i == 0)
        def _():
            dw_acc[...] = jnp.zeros_like(dw_acc)
            for ref in (dg_ref, dlg_ref, dlb_ref, dws_ref, dbt_ref):
                ref[...] = jnp.zeros_like(ref)

        _sgu_bwd_tile(uvz_ref, dsgu_ref, (lg_ref, lb_ref, ws_ref, bt_ref), da_ref, (dlg_ref, dlb_ref, dws_ref, dbt_ref))
        g = g_ref[...]
        n, r = _rms(x_ref[...])
        xn = (n * g).astype(BF16)
        dxn = None
        for d_ref, col0 in zip((da_ref, dq_ref, dz_ref, dl_ref), starts):
            width = d_ref.shape[1]
            rows = wgt_ref[...] if d_ref is dl_ref else wt_ref[col0:col0 + width, :]
            term = jnp.dot(d_ref[...], rows, preferred_element_type=F32)
            dxn = term if dxn is None else dxn + term
            for c0 in range(0, width, 512):
                c1 = min(c0 + 512, width)
                dw_acc[col0 + c0:col0 + c1, :] += lax.dot_general(d_ref[:, c0:c1], xn, (((0,), (0,)), ((), ())),
                                                                  preferred_element_type=F32)
        dg_ref[...] += _rowsum(dxn * n)
        dx_ref[...] = dh_ref[...] + _rms_bwd(dxn * g, n, r)

        @pl.when(i == steps - 1)
        def _():
            for j in range(N_DEV):
                stage_ref[...] = dw_acc[j * IN_SHARD:(j + 1) * IN_SHARD, :]
                pltpu.sync_copy(stage_ref, dw_hbm.at[j])

    tile = lambda w: pl.BlockSpec((tm, w), lambda i: (i, 0))
    sgu_shapes = ((1, SGU_WIDTH), (1, SGU_WIDTH), (SGU_GROUPS, SGU_CHUNK, SGU_CHUNK), (SGU_CHUNK, SGU_GROUPS))
    return pl.pallas_call(
        body, name="inproj_sgu_bwd", grid=(steps,),
        out_shape=(jax.ShapeDtypeStruct((t, D_MODEL), F32), jax.ShapeDtypeStruct((N_DEV, IN_SHARD, D_MODEL), F32),
                   jax.ShapeDtypeStruct((1, D_MODEL), F32)) + tuple(jax.ShapeDtypeStruct(s, F32) for s in sgu_shapes),
        in_specs=[tile(D_MODEL), tile(D_MODEL), tile(widths[0]), tile(SGU_WIDTH)] + [tile(w) for w in widths[1:]]
        + [_whole((1, D_MODEL))] + [_whole(s) for s in sgu_shapes] + [VMEM_SPEC] * 2,
        out_specs=(tile(D_MODEL), HBM_SPEC, _whole((1, D_MODEL))) + tuple(_whole(s) for s in sgu_shapes),
        scratch_shapes=[pltpu.VMEM((sum(widths), D_MODEL), F32), pltpu.VMEM((IN_SHARD, D_MODEL), F32),
                        pltpu.VMEM((tm, widths[0]), BF16)],
        compiler_params=_params(1),
    )(x2, dh1, a_uvz, d_sgu, d_q, d_z, d_l, norm_g, *sgu_weights, wt, wgt)


def _reduce_adamw(recv, w, m, v, name, col_block=None):
    n, rows, cols = recv.shape
    cb = col_block or cols
    lead = w.ndim - 2

    def body(r_ref, w_ref, m_ref, v_ref, g_ref, d_ref, nm_ref, nv_ref):
        g = r_ref[0].astype(F32)
        for i in range(1, n):
            g = g + r_ref[i].astype(F32)
        m_new = ADAM_B1 * m_ref[...] + (1.0 - ADAM_B1) * g
        v_new = ADAM_B2 * v_ref[...] + (1.0 - ADAM_B2) * jnp.square(g)
        m_hat = m_new / (1.0 - ADAM_B1 ** ADAM_STEP)
        v_hat = v_new / (1.0 - ADAM_B2 ** ADAM_STEP)
        g_ref[...] = g
        d_ref[...] = -ADAM_LR * (m_hat / (jnp.sqrt(v_hat) + ADAM_EPS) + ADAM_WD * w_ref[...])
        nm_ref[...] = m_new
        nv_ref[...] = v_new

    blk = pl.BlockSpec((None,) * lead + (rows, cb), lambda i: (0,) * lead + (0, i))
    return pl.pallas_call(
        body, name=name, grid=(cols // cb,),
        out_shape=tuple(jax.ShapeDtypeStruct(w.shape, F32) for _ in range(4)),
        in_specs=[pl.BlockSpec((n, rows, cb), lambda i: (0, 0, i)), blk, blk, blk],
        out_specs=(blk, blk, blk, blk),
        compiler_params=_params(1),
    )(recv, w, m, v)


def _adamw_replicated(received, ws, ms, vs):
    nw = len(ws)
    starts = [sum(SMALL_PIECE_ROWS[:i]) for i in range(len(SMALL_PIECE_ROWS))]

    def natural(g_ref, row0, shape):
        cols, rows = shape[-1], _size(shape[:-1])
        if cols == LANES:
            return g_ref[row0:row0 + rows, :].reshape(shape)
        if cols < LANES:
            return g_ref[row0:row0 + 1, 0:cols].reshape(shape)
        per = cols // LANES
        return jnp.concatenate(
            [jnp.concatenate([g_ref[row0 + r * per + k:row0 + r * per + k + 1, :] for k in range(per)], axis=1)
             for r in range(rows)], axis=0).reshape(shape)

    def body(r_ref, *refs):
        w_refs, m_refs, v_refs = refs[:nw], refs[nw:2 * nw], refs[2 * nw:3 * nw]
        conv_ref, loss_ref = refs[3 * nw], refs[3 * nw + 1]
        out_refs, g_ref = refs[3 * nw + 2:-1], refs[-1]
        g = r_ref[0]
        for q in range(1, N_CHIPS):
            g = g + r_ref[q]
        g_ref[...] = g
        conv_ref[...] = natural(g_ref, starts[0], (CONV_K, 3 * DN_WIDTH))
        loss_ref[...] = natural(g_ref, starts[-1], (1, 1))
        for i in range(nw):
            gi = natural(g_ref, starts[1 + i], w_refs[i].shape)
            m_new = ADAM_B1 * m_refs[i][...] + (1.0 - ADAM_B1) * gi
            v_new = ADAM_B2 * v_refs[i][...] + (1.0 - ADAM_B2) * jnp.square(gi)
            m_hat = m_new / (1.0 - ADAM_B1 ** ADAM_STEP)
            v_hat = v_new / (1.0 - ADAM_B2 ** ADAM_STEP)
            out_refs[4 * i][...] = gi
            out_refs[4 * i + 1][...] = -ADAM_LR * (m_hat / (jnp.sqrt(v_hat) + ADAM_EPS) + ADAM_WD * w_refs[i][...])
            out_refs[4 * i + 2][...] = m_new
            out_refs[4 * i + 3][...] = v_new

    def spec(a):
        lead = max(a.ndim - 3, 0)
        return pl.BlockSpec((None,) * lead + a.shape[lead:], lambda: (0,) * a.ndim)

    weight_specs = [spec(a) for a in ws]
    return pl.pallas_call(
        body, name="adamw_replicated",
        out_shape=(jax.ShapeDtypeStruct((CONV_K, 3 * DN_WIDTH), F32), jax.ShapeDtypeStruct((1, 1), F32))
        + tuple(jax.ShapeDtypeStruct(a.shape, F32) for a in ws for _ in range(4)),
        in_specs=[pl.BlockSpec(received.shape, lambda: (0, 0, 0))] + weight_specs * 3,
        out_specs=(pl.BlockSpec((CONV_K, 3 * DN_WIDTH), lambda: (0, 0)), pl.BlockSpec((1, 1), lambda: (0, 0)))
        + tuple(s for s in weight_specs for _ in range(4)),
        scratch_shapes=[pltpu.VMEM(received.shape[1:], F32)],
        compiler_params=pltpu.CompilerParams(vmem_limit_bytes=VMEM_LIMIT),
    )(received, *ws, *ms, *vs)


def _pack_rows(pieces, rows):
    padded = [jnp.pad(jnp.ravel(p), (0, -p.size % LANES)) for p in pieces]
    flat = jnp.concatenate(padded)
    return jnp.pad(flat, (0, rows * LANES - flat.shape[0])).reshape(rows, LANES)


def kernel(x, p, norm_g, w_in, sgu_ln_g, sgu_ln_b, sgu_w_s, sgu_b_s, dn_conv_w, dn_a_log, dn_dt_bias, dn_o_norm_g, w_out, ple_norm_g, ple_gate_w, ple_proj_w, final_norm_g, loss_target, m_norm_g, m_w_in, m_sgu_ln_g, m_sgu_ln_b, m_sgu_w_s, m_sgu_b_s, m_dn_conv_w, m_dn_a_log, m_dn_dt_bias, m_dn_o_norm_g, m_w_out, m_ple_norm_g, m_ple_gate_w, m_ple_proj_w, m_final_norm_g, v_norm_g, v_w_in, v_sgu_ln_g, v_sgu_ln_b, v_sgu_w_s, v_sgu_b_s, v_dn_conv_w, v_dn_a_log, v_dn_dt_bias, v_dn_o_norm_g, v_w_out, v_ple_norm_g, v_ple_gate_w, v_ple_proj_w, v_final_norm_g):
    weights = dict(norm_g=norm_g, w_in=w_in, sgu_ln_g=sgu_ln_g, sgu_ln_b=sgu_ln_b, sgu_w_s=sgu_w_s, sgu_b_s=sgu_b_s,
                   dn_conv_w=dn_conv_w, dn_a_log=dn_a_log, dn_dt_bias=dn_dt_bias, dn_o_norm_g=dn_o_norm_g, w_out=w_out,
                   ple_norm_g=ple_norm_g, ple_gate_w=ple_gate_w, ple_proj_w=ple_proj_w, final_norm_g=final_norm_g)
    mom1 = dict(norm_g=m_norm_g, w_in=m_w_in, sgu_ln_g=m_sgu_ln_g, sgu_ln_b=m_sgu_ln_b, sgu_w_s=m_sgu_w_s,
                sgu_b_s=m_sgu_b_s, dn_conv_w=m_dn_conv_w, dn_a_log=m_dn_a_log, dn_dt_bias=m_dn_dt_bias,
                dn_o_norm_g=m_dn_o_norm_g, w_out=m_w_out, ple_norm_g=m_ple_norm_g, ple_gate_w=m_ple_gate_w,
                ple_proj_w=m_ple_proj_w, final_norm_g=m_final_norm_g)
    mom2 = dict(norm_g=v_norm_g, w_in=v_w_in, sgu_ln_g=v_sgu_ln_g, sgu_ln_b=v_sgu_ln_b, sgu_w_s=v_sgu_w_s,
                sgu_b_s=v_sgu_b_s, dn_conv_w=v_dn_conv_w, dn_a_log=v_dn_a_log, dn_dt_bias=v_dn_dt_bias,
                dn_o_norm_g=v_dn_o_norm_g, w_out=v_w_out, ple_norm_g=v_ple_norm_g, ple_gate_w=v_ple_gate_w,
                ple_proj_w=v_ple_proj_w, final_norm_g=v_final_norm_g)
    nb, s, _ = x.shape
    t = nb * s

    transposed = lambda a: jnp.transpose(a, (2, 0, 1)).reshape(IN_SHARD, D_MODEL)
    w_in_t, m_in_t, v_in_t = transposed(w_in), transposed(m_w_in), transposed(v_w_in)
    w_in_blocks, conv_blocks = _all_gather([w_in_t.astype(BF16), dn_conv_w[0]])
    w_in_full_t = w_in_blocks.reshape(IN_COLS, D_MODEL)
    wgt = jnp.pad(w_in_full_t[sum(IN_GROUPS):], ((0, GATE_PAD - 2 * DN_HEADS), (0, 0)))
    conv_full = jnp.moveaxis(conv_blocks, 0, 1).reshape(CONV_K, 3 * DN_WIDTH)
    later_shards = [w_out[0].astype(BF16), ple_gate_w[0].astype(BF16), ple_proj_w[0].astype(BF16)]

    pad_row = lambda a: jnp.pad(a.reshape(1, -1), ((0, 0), (DN_HEADS, GATE_PAD - DN_HEADS - a.size)))
    alog, dtb = pad_row(dn_a_log), pad_row(dn_dt_bias)
    og = dn_o_norm_g.reshape(1, DN_HEAD_DIM)
    ws = sgu_w_s.reshape(SGU_GROUPS, SGU_CHUNK, SGU_CHUNK)
    b_t = sgu_b_s.reshape(SGU_GROUPS, SGU_CHUNK).T
    fin_g = final_norm_g.reshape(1, D_MODEL)

    x2 = x.reshape(t, D_MODEL)
    sgu_weights = (sgu_ln_g, sgu_ln_b, ws, b_t)
    a_uvz, b_qkv, b_z, b_l, a_out, conv_out, w_out_blocks, w_gate_blocks, w_proj_blocks = _inproj_fwd(
        x2, s, norm_g, w_in_full_t, wgt, sgu_weights, conv_full, later_shards)
    w_out_full = w_out_blocks.reshape(D_MODEL, D_MODEL)
    w_gate_full = w_gate_blocks.reshape(D_MODEL, D_MODEL)
    w_proj_full = jnp.moveaxis(w_proj_blocks, 0, 1).reshape(PLE_DIM, D_MODEL)
    qkv3 = b_qkv.reshape(nb, s, 3 * DN_WIDTH)
    conv_out = conv_out.reshape(nb, s, 3 * DN_WIDTH)
    z3 = b_z.reshape(nb, s, DN_WIDTH)
    l3 = b_l.reshape(nb, s, GATE_PAD)
    b_out, states, inverses = _dn_fwd(conv_out, z3, l3, alog, dtb, og)

    d_a, d_b, dh1, g_w_out, g_gate, g_proj, g_ple_g, g_fin_g, loss_tile = _head(
        a_out, b_out.reshape(t, DN_WIDTH), x2, p.reshape(t, PLE_DIM), loss_target.reshape(t, D_MODEL),
        w_out_full, w_gate_full, w_proj_full, ple_norm_g, fin_g)
    d_qkv, d_z, d_l, g_conv, g_alog, g_dtb, g_og, *head_received = _dn_bwd(
        qkv3, conv_out, z3, l3, conv_full, alog, dtb, og, states, inverses, d_b.reshape(nb, s, DN_WIDTH),
        [g_w_out, g_gate, g_proj])
    grad_x, g_w_in, g_norm, g_ln_g, g_ln_b, g_ws, g_bt = _inproj_bwd(
        x2, dh1, a_uvz, d_a, d_qkv.reshape(t, 3 * DN_WIDTH), d_z.reshape(t, DN_WIDTH), d_l.reshape(t, GATE_PAD),
        norm_g, sgu_weights, w_in_full_t, wgt)

    small = _pack_rows([g_conv, g_norm, g_ln_g, g_ln_b, g_ws, g_bt.T, g_alog[:, DN_HEADS:2 * DN_HEADS], g_dtb[:, DN_HEADS:2 * DN_HEADS], g_og,
                        g_ple_g, g_fin_g, (0.5 / D_MODEL) * loss_tile[0:1, 0:1]], SMALL_ROWS)
    w_in_received, small_received = _reduce_exchange(g_w_in, small)

    results = {}
    outs = _reduce_adamw(w_in_received, w_in_t, m_in_t, v_in_t, "adamw_w_in", 4 * LANES)
    results["w_in"] = [jnp.transpose(a.reshape(IN_SHARD, 1, D_MODEL), (1, 2, 0)) for a in outs]
    for name, recv in zip(("w_out", "ple_gate_w", "ple_proj_w"), head_received):
        results[name] = _reduce_adamw(recv, weights[name], mom1[name], mom2[name], "adamw_" + name)
    names = [name for name, _ in REPLICATED]
    two_d = lambda a: a.reshape(1, -1) if a.ndim == 1 else a
    g_conv_sum, loss_sum, *flat_outs = _adamw_replicated(
        small_received, *[[two_d(src[k]) for k in names] for src in (weights, mom1, mom2)])
    for i, k in enumerate(names):
        results[k] = [a.reshape(weights[k].shape) for a in flat_outs[4 * i:4 * i + 4]]
    loss = loss_sum[0, 0]
    me = 4 * lax.axis_index("x") + 2 * lax.axis_index("y") + lax.axis_index("c")
    conv_mine = lax.dynamic_slice(g_conv_sum, (0, me * 192), (CONV_K, 192))
    results["dn_conv_w"] = _reduce_adamw(conv_mine[None], dn_conv_w, m_dn_conv_w, v_dn_conv_w, "adamw_dn_conv_w")

    return (loss, grad_x.reshape(nb, s, D_MODEL), *[results[k][0] for k in WEIGHT_ORDER],
            *[results[k][1] for k in WEIGHT_ORDER], *[results[k][2] for k in WEIGHT_ORDER],
            *[results[k][3] for k in WEIGHT_ORDER])
```

```python
import functools

import jax
import jax.numpy as jnp
from jax import lax
from jax.experimental import pallas as pl
from jax.experimental.pallas import tpu as pltpu

F32 = jnp.float32
BF16 = jnp.bfloat16

N_DEV = 8
D_MODEL = 1024
SGU_WIDTH = 512
SGU_GROUPS = 4
SGU_CHUNK = 128
DN_WIDTH = 512
DN_HEADS = 4
DN_HEAD_DIM = 128
DN_CHUNK = 128
CONV_K = 4
CONV_HALO = 8
PLE_DIM = 256
EPS = 1e-6
IN_COLS = 3592
IN_SHARD = IN_COLS // N_DEV
GATE_PAD = 128
IN_GROUPS = (3 * SGU_WIDTH, 3 * DN_WIDTH, DN_WIDTH)

ADAM_LR = 0.001
ADAM_B1 = 0.9
ADAM_B2 = 0.999
ADAM_EPS = 1e-08
ADAM_WD = 0.01
ADAM_STEP = 10

LANES = 128
VMEM_LIMIT = 56 * 1024 * 1024
MESH = pl.DeviceIdType.MESH

REPLICATED = (("norm_g", (1, D_MODEL)), ("sgu_ln_g", (1, SGU_WIDTH)), ("sgu_ln_b", (1, SGU_WIDTH)),
              ("sgu_w_s", (1, SGU_GROUPS, SGU_CHUNK, SGU_CHUNK)), ("sgu_b_s", (1, SGU_GROUPS, SGU_CHUNK)),
              ("dn_a_log", (1, DN_HEADS)), ("dn_dt_bias", (1, DN_HEADS)), ("dn_o_norm_g", (1, DN_HEAD_DIM)),
              ("ple_norm_g", (1, D_MODEL)), ("final_norm_g", (D_MODEL,)))
WEIGHT_ORDER = ("norm_g", "w_in", "sgu_ln_g", "sgu_ln_b", "sgu_w_s", "sgu_b_s", "dn_conv_w", "dn_a_log",
                "dn_dt_bias", "dn_o_norm_g", "w_out", "ple_norm_g", "ple_gate_w", "ple_proj_w", "final_norm_g")


def _size(shape):
    n = 1
    for s in shape:
        n *= s
    return n


SMALL_LAYOUT = (("conv", (CONV_K, 3 * DN_WIDTH)),) + REPLICATED + (("loss", (1,)),)
SMALL_PIECE_ROWS = tuple(-(-_size(s) // LANES) for _, s in SMALL_LAYOUT)
SMALL_ROWS = -(-sum(SMALL_PIECE_ROWS) // 8) * 8


def _bdot(a, b):
    return jnp.dot(a.astype(BF16), b.astype(BF16), preferred_element_type=F32)


def _sigmoid(x):
    return 0.5 * jnp.tanh(0.5 * x) + 0.5


@jax.custom_vjp
def _silu(x):
    return x * _sigmoid(x)


def _silu_fwd(x):
    s = _sigmoid(x)
    return x * s, (x, s)


def _silu_bwd(res, ct):
    x, s = res
    return (ct * (s * (1.0 + x * (1.0 - s))),)


_silu.defvjp(_silu_fwd, _silu_bwd)


def _normal_cdf(x):
    return 0.5 + 0.5 * lax.erf(x * (0.5 ** 0.5))


@jax.custom_vjp
def _gelu(x):
    return x * _normal_cdf(x)


def _gelu_fwd(x):
    cdf = _normal_cdf(x)
    return x * cdf, (x, cdf)


def _gelu_bwd(res, ct):
    x, cdf = res
    pdf = jnp.exp(-0.5 * x * x) * ((2.0 * jnp.pi) ** -0.5)
    return (ct * (cdf + x * pdf),)


_gelu.defvjp(_gelu_fwd, _gelu_bwd)


def _softplus(x):
    return jnp.maximum(x, 0.0) + jnp.log1p(jnp.exp(-jnp.abs(x)))


@jax.custom_vjp
def _l2n(x):
    return x * lax.rsqrt(jnp.sum(x * x, axis=-1, keepdims=True) + EPS)


def _l2n_fwd(x):
    r = lax.rsqrt(jnp.sum(x * x, axis=-1, keepdims=True) + EPS)
    n = x * r
    return n, (n, r)


def _l2n_bwd(res, ct):
    n, r = res
    return (r * (ct - n * jnp.sum(ct * n, axis=-1, keepdims=True)),)


_l2n.defvjp(_l2n_fwd, _l2n_bwd)


def _rms(x):
    r = lax.rsqrt(jnp.mean(x * x, axis=-1, keepdims=True) + EPS)
    return x * r, r


def _rms_bwd(dn, n, r):
    return r * (dn - n * jnp.mean(dn * n, axis=-1, keepdims=True))


@jax.custom_vjp
def _rms_normed(x):
    return _rms(x)[0]


def _rms_normed_fwd(x):
    n, r = _rms(x)
    return n, (n, r)


def _rms_normed_bwd(res, ct):
    return (_rms_bwd(ct, *res),)


_rms_normed.defvjp(_rms_normed_fwd, _rms_normed_bwd)


def _onehot_row(idx, width):
    return (lax.broadcasted_iota(jnp.int32, (1, width), 1) == idx).astype(F32)


def _rowsum(x):
    return jnp.sum(x, axis=0, keepdims=True)


def _iota2(n):
    return lax.broadcasted_iota(jnp.int32, (n, n), 0), lax.broadcasted_iota(jnp.int32, (n, n), 1)


def _bmm(a, b):
    return lax.dot_general(a.astype(BF16), b.astype(BF16), (((2,), (1,)), ((0,), (0,))), preferred_element_type=F32)


def _bmm_nt(a, b):
    return lax.dot_general(a.astype(BF16), b.astype(BF16), (((2,), (2,)), ((0,), (0,))), preferred_element_type=F32)


def _bmm_tn(a, b):
    return lax.dot_general(a.astype(BF16), b.astype(BF16), (((1,), (1,)), ((0,), (0,))), preferred_element_type=F32)


def _tri_inv_impl(a):
    n = a.shape[-1]
    r, c = _iota2(n)
    x = r ^ c
    eye = (r == c).astype(F32)
    ad = jnp.where(x < 16, a, 0.0)
    p2 = _bmm(ad, ad)
    e = p2 - ad - _bmm(ad, p2)
    p4 = _bmm(p2, p2)
    e = e + p4 + _bmm(e, p4)
    p8 = _bmm(p4, p4)
    e = e + p8 + _bmm(e, p8)
    size = 16
    while size < n:
        m = jnp.where(jnp.logical_and(x < 2 * size, x >= size), a, 0.0)
        f = m + _bmm(m, e)
        e = e - f - _bmm(e, f)
        size *= 2
    return e + eye


@jax.custom_vjp
def _tri_inv(a, known):
    return _tri_inv_impl(a) if known is None else known


def _tri_inv_fwd(a, known):
    t = _tri_inv(a, known)
    return t, (t, known)


def _tri_inv_bwd(res, dt):
    t, known = res
    return -_bmm_tn(t, _bmm_nt(dt, t)), None if known is None else jnp.zeros_like(known)


_tri_inv.defvjp(_tri_inv_fwd, _tri_inv_bwd)


@jax.custom_vjp
def _standardized(x):
    xc = x - jnp.mean(x, axis=-1, keepdims=True)
    return xc * lax.rsqrt(jnp.mean(xc * xc, axis=-1, keepdims=True) + EPS)


def _standardized_fwd(x):
    xc = x - jnp.mean(x, axis=-1, keepdims=True)
    rstd = lax.rsqrt(jnp.mean(xc * xc, axis=-1, keepdims=True) + EPS)
    y = xc * rstd
    return y, (y, rstd)


def _standardized_bwd(res, ct):
    y, rstd = res
    return (rstd * (ct - jnp.mean(ct, axis=-1, keepdims=True) - y * jnp.mean(ct * y, axis=-1, keepdims=True)),)


_standardized.defvjp(_standardized_fwd, _standardized_bwd)


def _sgu_core(u, v, z, lg, lb, ws, bcol):
    n = ws.shape[0]
    r, c = _iota2(n)
    wm = jnp.where(r >= c, ws, 0.0)
    gu = _gelu(u)
    gv = _gelu(v)
    ln = _standardized(gv) * lg + lb
    s = _bdot(wm, ln) + bcol
    return gu * s * _silu(z)


def _lanes_of(x):
    return jnp.concatenate([x[i] for i in range(x.shape[0])], axis=1)


def _batch_of(x, width):
    return jnp.concatenate([x[None, :, i * width:(i + 1) * width] for i in range(x.shape[1] // width)], axis=0)


def _mask_dot(mask, x):
    hi = x.astype(BF16)
    lo = (x - hi.astype(F32)).astype(BF16)
    m = mask.astype(BF16)
    return jnp.dot(m, hi, preferred_element_type=F32) + jnp.dot(m, lo, preferred_element_type=F32)


def _split_dot(x, mask, dims):
    hi = x.astype(BF16)
    lo = (x - hi.astype(F32)).astype(BF16)
    m = mask.astype(BF16)
    return (lax.dot_general(hi, m, dims, preferred_element_type=F32)
            + lax.dot_general(lo, m, dims, preferred_element_type=F32))


def _lane_select(lanes, blocks, first_lane):
    src = lax.broadcasted_iota(jnp.int32, (lanes, blocks * LANES), 0)
    dst = lax.broadcasted_iota(jnp.int32, (lanes, blocks * LANES), 1) // LANES
    return src == dst + first_lane


def _pick_lanes(x, blocks, first_lane):
    return _pick_lanes_vjp(blocks, first_lane, x)


@functools.partial(jax.custom_vjp, nondiff_argnums=(0, 1))
def _pick_lanes_vjp(blocks, first_lane, x):
    return _split_dot(x, _lane_select(x.shape[-1], blocks, first_lane), (((1,), (0,)), ((), ())))


def _pick_lanes_fwd(blocks, first_lane, x):
    return _pick_lanes_vjp(blocks, first_lane, x), x.shape[-1]


def _pick_lanes_bwd(blocks, first_lane, lanes, ct):
    return (sum(jnp.sum(ct[:, h * LANES:(h + 1) * LANES], axis=-1, keepdims=True) * _onehot_row(first_lane + h, lanes)
                for h in range(blocks)),)


_pick_lanes_vjp.defvjp(_pick_lanes_fwd, _pick_lanes_bwd)


def _tri_mask(n, upper):
    r, c = _iota2(n)
    return (r <= c) if upper else (r >= c)


@jax.custom_vjp
def _cumsum_rows(x):
    return _mask_dot(_tri_mask(x.shape[0], False), x)


def _cumsum_rows_fwd(x):
    return _cumsum_rows(x), None


def _cumsum_rows_bwd(_, ct):
    return (_mask_dot(_tri_mask(ct.shape[0], True), ct),)


_cumsum_rows.defvjp(_cumsum_rows_fwd, _cumsum_rows_bwd)


@jax.custom_vjp
def _colsum_all_rows(x):
    return _mask_dot(jnp.ones((x.shape[0], x.shape[0]), jnp.bool_), x)


def _colsum_all_rows_fwd(x):
    return _colsum_all_rows(x), None


def _colsum_all_rows_bwd(_, ct):
    return (_mask_dot(jnp.ones((ct.shape[0], ct.shape[0]), jnp.bool_), ct),)


_colsum_all_rows.defvjp(_colsum_all_rows_fwd, _colsum_all_rows_bwd)


def _dn_core(cq, ck, cv, z, logits, state, alog, dtb, og, t_known=None):
    gn, cn, dh = cq.shape
    heads = gn // logits.shape[0]
    q = _l2n(_silu(cq)) * (dh ** -0.5)
    k = _l2n(_silu(ck))
    v = _silu(cv)
    beta_lanes = _sigmoid(logits)
    g_lanes = -jnp.exp(alog) * _softplus(logits + dtb)
    beta_all = jnp.concatenate([_pick_lanes(beta_lanes[b], heads, 0) for b in range(logits.shape[0])], axis=1)
    g_all = jnp.concatenate([_pick_lanes(g_lanes[b], heads, heads) for b in range(logits.shape[0])], axis=1)
    beta = _batch_of(beta_all, dh)
    g_wide = _batch_of(g_all, dh)
    r, c = _iota2(cn)
    tril = r >= c
    rw = lax.broadcasted_iota(jnp.int32, (cn, dh), 0)
    cw = lax.broadcasted_iota(jnp.int32, (cn, dh), 1)
    upper_wide = (rw <= cw).astype(F32)
    gc_wide = _batch_of(_cumsum_rows(g_all), dh)
    gc_cols = _batch_of(_colsum_all_rows(_lanes_of(g_wide * upper_wide)), dh)[:, :, :cn]
    decay = jnp.exp(jnp.where(tril, gc_wide[:, :, :cn] - gc_cols, -1e30))
    kb = k * beta
    kk = _bmm_nt(kb, k) * decay
    t = _tri_inv(jnp.where(r > c, kk, 0.0), t_known)
    eg = jnp.exp(gc_wide)
    sol = _bmm(t, jnp.concatenate([v * beta, kb * eg], axis=-1))
    u_val, w_dec = sol[:, :, :dh], sol[:, :, dh:]
    qk = _bmm_nt(q, k) * decay
    g_last = jnp.sum(g_wide, axis=1, keepdims=True)
    k_dec = k * jnp.exp(g_last - gc_wide)
    ws = _bmm(jnp.concatenate([w_dec, q * eg], axis=1), state)
    v_new = u_val - ws[:, :cn]
    o = ws[:, cn:] + _bmm(qk, v_new)
    new_state = state * jnp.exp(g_last) + _bmm_tn(k_dec, v_new)
    return _rms_normed(o) * og * _silu(z), new_state, t


N_CHIPS = 4
HBM_SPEC = pl.BlockSpec(memory_space=pl.ANY)


def _place():
    return lax.axis_index("x"), lax.axis_index("y"), lax.axis_index("c")


def _other_chip(k):
    x, y, _ = _place()
    px = 1 - x if k & 2 else x
    py = 1 - y if k & 1 else y
    return px, py, 2 * px + py


def _remote(src, dst, send_sem, recv_sem, device):
    return pltpu.make_async_remote_copy(src_ref=src, dst_ref=dst, send_sem=send_sem, recv_sem=recv_sem,
                                        device_id=device, device_id_type=MESH)


def _other_device(k):
    x, y, c = _place()
    px = 1 - x if k & 4 else x
    py = 1 - y if k & 2 else y
    pc = 1 - c if k & 1 else c
    return (px, py, pc), 4 * px + 2 * py + pc


def _direct_exchange(srcs, outs, send_sems, recv_sems, local_sems, gather):
    x, y, c = _place()
    me = 4 * x + 2 * y + c

    def copies(arriving):
        out_list = []
        for a, (src, out) in enumerate(zip(srcs, outs)):
            for k in range(1, N_DEV):
                peer, index = _other_device(k)
                mine = src if gather else src.at[index]
                out_list.append(_remote(mine, out.at[index if arriving else me], send_sems.at[a, k - 1],
                                        recv_sems.at[a, k - 1], peer))
        return out_list

    def local_copies():
        return [pltpu.make_async_copy(src if gather else src.at[me], out.at[me], local_sems.at[a])
                for a, (src, out) in enumerate(zip(srcs, outs))]

    def start():
        for cp in local_copies() + copies(False):
            cp.start()

    def wait():
        for cp in copies(True):
            cp.wait_recv()
        for cp in copies(False):
            cp.wait_send()
        for cp in local_copies():
            cp.wait()

    return start, wait


def _exchange_scratch(n):
    return [pltpu.SemaphoreType.DMA((n, N_DEV - 1)), pltpu.SemaphoreType.DMA((n, N_DEV - 1)), pltpu.SemaphoreType.DMA((n,))]


def _all_gather(shards):
    n = len(shards)

    def body(*refs):
        srcs, outs = refs[:n], refs[n:2 * n]
        send_sems, recv_sems, local_sems = refs[2 * n:]
        x, y, c = _place()
        me = 4 * x + 2 * y + c
        sibling = (x, y, 1 - c)
        local = [pltpu.make_async_copy(srcs[a], outs[a].at[me], local_sems.at[a]) for a in range(n)]
        for cp in local:
            cp.start()
        sends = []
        for a in range(n):
            sends.append(_remote(srcs[a], outs[a].at[me], send_sems.at[a, 0], recv_sems.at[a, 0], sibling))
        for k in range(1, N_CHIPS):
            px, py, _ = _other_chip(k)
            for a in range(n):
                sends.append(_remote(srcs[a], outs[a].at[me], send_sems.at[a, k], recv_sems.at[a, k], (px, py, c)))
        for cp in sends:
            cp.start()
        passed = []
        for k in range(1, N_CHIPS):
            px, py, _ = _other_chip(k)
            blk = 4 * px + 2 * py + c
            for a in range(n):
                _remote(srcs[a], outs[a].at[blk], send_sems.at[a, k], recv_sems.at[a, k], (px, py, c)).wait_recv()
            for a in range(n):
                cp = _remote(outs[a].at[blk], outs[a].at[blk], send_sems.at[a, 3 + k], recv_sems.at[a, 3 + k], sibling)
                cp.start()
                passed.append(cp)
        for a in range(n):
            _remote(srcs[a], outs[a].at[me + 1 - 2 * c], send_sems.at[a, 0], recv_sems.at[a, 0], sibling).wait_recv()
        for k in range(1, N_CHIPS):
            px, py, _ = _other_chip(k)
            blk = 4 * px + 2 * py + 1 - c
            for a in range(n):
                _remote(srcs[a], outs[a].at[blk], send_sems.at[a, 3 + k], recv_sems.at[a, 3 + k], sibling).wait_recv()
        for cp in sends + passed:
            cp.wait_send()
        for cp in local:
            cp.wait()

    return pl.pallas_call(
        body, name="all_gather_weights",
        out_shape=tuple(jax.ShapeDtypeStruct((N_DEV,) + a.shape, a.dtype) for a in shards),
        in_specs=[HBM_SPEC] * n, out_specs=(HBM_SPEC,) * n,
        scratch_shapes=[pltpu.SemaphoreType.DMA((n, N_DEV - 1)), pltpu.SemaphoreType.DMA((n, N_DEV - 1)),
                        pltpu.SemaphoreType.DMA((n,))],
    )(*shards)


def _reduce_exchange(by_device, small):
    _, rows, cols = by_device.shape

    def body(g_ref, small_ref, out_ref, small_out_ref, from_sibling, small_from_sibling, stage, sums, small_own, small_sum,
             pair_send, pair_recv, chip_send, chip_recv, local_sems):
        x, y, c = _place()
        mine = 2 * x + y
        sibling = (x, y, 1 - c)
        chips = [(x, y, mine)] + [_other_chip(k) for k in range(1, N_CHIPS)]
        to_sibling = [_remote(g_ref.at[2 * chips[k][2] + 1 - c], from_sibling.at[k], pair_send.at[k], pair_recv.at[k], sibling)
                      for k in range(N_CHIPS)]
        to_sibling.append(_remote(small_ref, small_from_sibling, pair_send.at[N_CHIPS], pair_recv.at[N_CHIPS], sibling))
        for cp in to_sibling:
            cp.start()
        small_mine = pltpu.make_async_copy(small_ref, small_own, local_sems.at[0])
        small_mine.start()
        to_chips = []
        for k in (1, 2, 3, 0):
            px, py, chip = chips[k]
            mine_k = pltpu.make_async_copy(g_ref.at[2 * chip + c], stage, local_sems.at[1])
            mine_k.start()
            to_sibling[k].wait_recv()
            mine_k.wait()
            sums[k] = (stage[...] + from_sibling[k]).astype(sums.dtype)
            if k:
                cp = _remote(sums.at[k], out_ref.at[mine], chip_send.at[0, k - 1], chip_recv.at[0, k - 1], (px, py, c))
                cp.start()
                to_chips.append(cp)
        own_block = pltpu.make_async_copy(sums.at[0], out_ref.at[mine], local_sems.at[2])
        own_block.start()
        to_sibling[N_CHIPS].wait_recv()
        small_mine.wait()
        small_sum[...] = small_own[...] + small_from_sibling[...]
        for k in range(1, N_CHIPS):
            px, py, _ = chips[k]
            cp = _remote(small_sum, small_out_ref.at[mine], chip_send.at[1, k - 1], chip_recv.at[1, k - 1], (px, py, c))
            cp.start()
            to_chips.append(cp)
        own_small = pltpu.make_async_copy(small_sum, small_out_ref.at[mine], local_sems.at[3])
        own_small.start()
        for k in range(1, N_CHIPS):
            px, py, chip = chips[k]
            _remote(sums.at[k], out_ref.at[chip], chip_send.at[0, k - 1], chip_recv.at[0, k - 1], (px, py, c)).wait_recv()
            _remote(small_sum, small_out_ref.at[chip], chip_send.at[1, k - 1], chip_recv.at[1, k - 1], (px, py, c)).wait_recv()
        for cp in to_sibling + to_chips:
            cp.wait_send()
        own_block.wait()
        own_small.wait()

    return pl.pallas_call(
        body, name="grad_reduce_exchange",
        out_shape=(jax.ShapeDtypeStruct((N_CHIPS, rows, cols), BF16), jax.ShapeDtypeStruct((N_CHIPS,) + small.shape, F32)),
        in_specs=[HBM_SPEC, HBM_SPEC], out_specs=(HBM_SPEC, HBM_SPEC),
        scratch_shapes=[pltpu.VMEM((N_CHIPS, rows, cols), F32), pltpu.VMEM(small.shape, F32), pltpu.VMEM((rows, cols), F32),
                        pltpu.VMEM((N_CHIPS, rows, cols), BF16), pltpu.VMEM(small.shape, F32), pltpu.VMEM(small.shape, F32),
                        pltpu.SemaphoreType.DMA((N_CHIPS + 1,)), pltpu.SemaphoreType.DMA((N_CHIPS + 1,)),
                        pltpu.SemaphoreType.DMA((2, N_CHIPS - 1)), pltpu.SemaphoreType.DMA((2, N_CHIPS - 1)),
                        pltpu.SemaphoreType.DMA((4,))],
        compiler_params=pltpu.CompilerParams(vmem_limit_bytes=VMEM_LIMIT),
    )(by_device, small)


def _params(n_axes):
    return pltpu.CompilerParams(dimension_semantics=("arbitrary",) * n_axes, vmem_limit_bytes=VMEM_LIMIT)


def _whole(shape):
    return pl.BlockSpec(shape, lambda *_: (0,) * len(shape))


VMEM_SPEC = pl.BlockSpec(memory_space=pltpu.VMEM)


def _inproj_fwd(x2, seq_len, norm_g, wt, wgt, sgu_weights, conv_w, later_shards):
    t = x2.shape[0]
    tm = min(512, seq_len)
    tiles_per_seq = seq_len // tm
    steps = t // tm
    ns = len(later_shards)

    widths = IN_GROUPS + (wgt.shape[0],)
    starts = (0, IN_GROUPS[0], IN_GROUPS[0] + IN_GROUPS[1], 0)

    def body(x_ref, g_ref, wt_ref, wg_ref, lg_ref, lb_ref, ws_ref, bt_ref, cw_ref, *rest):
        shard_refs, rest = rest[:ns], rest[ns:]
        a_ref, q_ref, z_ref, l_ref, sgu_ref, c_ref = rest[:6]
        gathered_refs, (xpad_ref, send_sems, recv_sems, local_sems) = rest[6:6 + ns], rest[6 + ns:]
        start_gather, wait_gather = _direct_exchange(shard_refs, gathered_refs, send_sems, recv_sems, local_sems, True)
        pl.when(pl.program_id(0) == 0)(start_gather)

        @pl.when(pl.program_id(0) % tiles_per_seq == 0)
        def _():
            xpad_ref[0:CONV_HALO, :] = jnp.zeros((CONV_HALO, xpad_ref.shape[1]), F32)

        n, _ = _rms(x_ref[...])
        xn = (n * g_ref[...]).astype(BF16)

        def project(w_ref, row0, width, o_ref):
            for c0 in range(0, width, 512):
                c1 = min(c0 + 512, width)
                o_ref[:, c0:c1] = lax.dot_general(xn, w_ref[row0 + c0:row0 + c1, :], (((1,), (1,)), ((), ())),
                                                  preferred_element_type=F32)

        def sgu_rows(row0):
            for grp in range(SGU_GROUPS):
                args = _sgu_pieces(a_ref, lg_ref, lb_ref, ws_ref, bt_ref, row0, grp)
                sgu_ref[pl.ds(row0, SGU_CHUNK), pl.ds(grp * 128, 128)] = _sgu_core(*args).astype(sgu_ref.dtype)

        def conv():
            xpad_ref[CONV_HALO:, :] = q_ref[...]
            acc = None
            for j in range(CONV_K):
                term = cw_ref[j:j + 1, :] * xpad_ref[pl.ds(CONV_HALO - CONV_K + 1 + j, tm), :]
                acc = term if acc is None else acc + term
            c_ref[...] = acc
            xpad_ref[0:CONV_HALO, :] = xpad_ref[tm:tm + CONV_HALO, :]

        groups = tuple(zip((wt_ref, wt_ref, wt_ref, wg_ref), starts, widths, (a_ref, q_ref, z_ref, l_ref)))
        row_chunks = list(range(0, tm, SGU_CHUNK))
        project(*groups[0])
        for row0 in row_chunks[:len(row_chunks) // 2]:
            sgu_rows(row0)
        project(*groups[1])
        for row0 in row_chunks[len(row_chunks) // 2:]:
            sgu_rows(row0)
        project(*groups[3])
        conv()
        project(*groups[2])
        pl.when(pl.program_id(0) == steps - 1)(wait_gather)

    tile = lambda w: pl.BlockSpec((tm, w), lambda i: (i, 0))
    sgu_shapes = ((1, SGU_WIDTH), (1, SGU_WIDTH), (SGU_GROUPS, SGU_CHUNK, SGU_CHUNK), (SGU_CHUNK, SGU_GROUPS))
    return pl.pallas_call(
        body, name="inproj_sgu_conv_fwd", grid=(steps,),
        out_shape=tuple(jax.ShapeDtypeStruct((t, w), F32) for w in widths)
        + (jax.ShapeDtypeStruct((t, SGU_WIDTH), BF16), jax.ShapeDtypeStruct((t, widths[1]), F32))
        + tuple(jax.ShapeDtypeStruct((N_DEV,) + a.shape, a.dtype) for a in later_shards),
        in_specs=[tile(D_MODEL), _whole((1, D_MODEL)), VMEM_SPEC, VMEM_SPEC]
        + [_whole(s) for s in sgu_shapes] + [_whole((CONV_K, widths[1]))] + [HBM_SPEC] * ns,
        out_specs=tuple(tile(w) for w in widths) + (tile(SGU_WIDTH), tile(widths[1])) + (HBM_SPEC,) * ns,
        scratch_shapes=[pltpu.VMEM((CONV_HALO + tm, widths[1]), F32)] + _exchange_scratch(ns),
        compiler_params=_params(1),
    )(x2, norm_g, wt, wgt, *sgu_weights, conv_w, *later_shards)


def _sgu_pieces(uvz_ref, lg_ref, lb_ref, ws_ref, bt_ref, row0, grp):
    rows = pl.ds(row0, SGU_CHUNK)
    lanes = pl.ds(grp * 128, 128)
    u = uvz_ref[rows, pl.ds(grp * 128, 128)]
    v = uvz_ref[rows, pl.ds(SGU_WIDTH + grp * 128, 128)]
    z = uvz_ref[rows, pl.ds(2 * SGU_WIDTH + grp * 128, 128)]
    bcol = jnp.sum(bt_ref[...] * _onehot_row(grp, SGU_GROUPS), axis=-1, keepdims=True)
    return u, v, z, lg_ref[:, lanes], lb_ref[:, lanes], ws_ref[grp], bcol


def _sgu_bwd_tile(uvz_ref, do_ref, sgu_refs, duvz_ref, grad_refs, pieces):
    lg_ref, lb_ref, ws_ref, bt_ref = sgu_refs
    dlg_ref, dlb_ref, dws_ref, dbt_ref = grad_refs
    for piece in pieces:
        row0, grp = piece // SGU_GROUPS * SGU_CHUNK, piece % SGU_GROUPS
        rows = pl.ds(row0, SGU_CHUNK)
        lanes = pl.ds(grp * 128, 128)
        args = _sgu_pieces(uvz_ref, lg_ref, lb_ref, ws_ref, bt_ref, row0, grp)
        _, pull = jax.vjp(_sgu_core, *args)
        du, dv, dz, dlg, dlb, dws, dbcol = pull(do_ref[rows, lanes])
        duvz_ref[rows, pl.ds(grp * 128, 128)] = du.astype(duvz_ref.dtype)
        duvz_ref[rows, pl.ds(SGU_WIDTH + grp * 128, 128)] = dv.astype(duvz_ref.dtype)
        duvz_ref[rows, pl.ds(2 * SGU_WIDTH + grp * 128, 128)] = dz.astype(duvz_ref.dtype)
        dlg_ref[:, lanes] += dlg
        dlb_ref[:, lanes] += dlb
        dws_ref[grp] += dws
        dbt_ref[...] += dbcol * _onehot_row(grp, SGU_GROUPS)


def _dn_pairs(nb):
    return [(b, h) for b in range(nb) for h in range(DN_HEADS)]


def _dn_batch_args(c_ref, z_ref):
    pairs = _dn_pairs(c_ref.shape[0])
    pick = lambda ref, b, col: ref[b, :, pl.ds(col, DN_HEAD_DIM)]
    cq = jnp.stack([pick(c_ref, b, h * DN_HEAD_DIM) for b, h in pairs])
    ck = jnp.stack([pick(c_ref, b, DN_WIDTH + h * DN_HEAD_DIM) for b, h in pairs])
    cv = jnp.stack([pick(c_ref, b, 2 * DN_WIDTH + h * DN_HEAD_DIM) for b, h in pairs])
    z = jnp.stack([pick(z_ref, b, h * DN_HEAD_DIM) for b, h in pairs])
    return cq, ck, cv, z


def _dn_weight_specs():
    return [_whole((CONV_K, 3 * DN_WIDTH)), _whole((1, GATE_PAD)), _whole((1, GATE_PAD)), _whole((1, DN_HEAD_DIM))]


def _dn_fwd(conv_out, zg, logits, alog, dtb, og):
    nb, s, _ = conv_out.shape
    nc = s // DN_CHUNK
    pairs = _dn_pairs(nb)
    gn = len(pairs)
    chunk = lambda w: pl.BlockSpec((nb, DN_CHUNK, w), lambda n: (0, n, 0))

    def body(c_ref, z_ref, l_ref, alog_ref, dtb_ref, og_ref, out_ref, st_ref, inv_ref, state_ref):
        n = pl.program_id(0)

        @pl.when(n == 0)
        def _():
            state_ref[...] = jnp.zeros_like(state_ref)

        cq, ck, cv, z = _dn_batch_args(c_ref, z_ref)
        state = state_ref[...]
        st_ref[...] = state
        out, new_state, t = _dn_core(cq, ck, cv, z, l_ref[...], state, alog_ref[...], dtb_ref[...], og_ref[...])
        state_ref[...] = new_state
        inv_ref[...] = t.astype(inv_ref.dtype)
        for i, (b, h) in enumerate(pairs):
            out_ref[b, :, pl.ds(h * DN_HEAD_DIM, DN_HEAD_DIM)] = out[i].astype(out_ref.dtype)

    per_chunk = pl.BlockSpec((None, gn, DN_HEAD_DIM, DN_HEAD_DIM), lambda n: (n, 0, 0, 0))
    return pl.pallas_call(
        body, name="deltanet_fwd", grid=(nc,),
        out_shape=(jax.ShapeDtypeStruct((nb, s, DN_WIDTH), BF16),
                   jax.ShapeDtypeStruct((nc, gn, DN_HEAD_DIM, DN_HEAD_DIM), F32),
                   jax.ShapeDtypeStruct((nc, gn, DN_CHUNK, DN_CHUNK), BF16)),
        in_specs=[chunk(3 * DN_WIDTH), chunk(DN_WIDTH), chunk(GATE_PAD)] + _dn_weight_specs()[1:],
        out_specs=(chunk(DN_WIDTH), per_chunk, pl.BlockSpec((None, gn, DN_CHUNK, DN_CHUNK), lambda n: (n, 0, 0, 0))),
        scratch_shapes=[pltpu.VMEM((gn, DN_HEAD_DIM, DN_HEAD_DIM), F32)],
        compiler_params=_params(1),
    )(conv_out, zg, logits, alog, dtb, og)


def _dn_bwd(qkv, conv_out, zg, logits, conv_w, alog, dtb, og, states, inverses, d_out, head_grads):
    nb, s, _ = qkv.shape
    nc = s // DN_CHUNK
    rev = lambda n: nc - 1 - n
    pairs = _dn_pairs(nb)
    gn = len(pairs)
    ng = len(head_grads)

    def body(cur_ref, c_ref, z_ref, l_ref, w_ref, alog_ref, dtb_ref, og_ref, st_ref, inv_ref, do_ref, *rest):
        grad_refs, rest = rest[:ng], rest[ng:]
        dqkv_ref, dz_ref, dl_ref, dw_ref, dalog_ref, ddtb_ref, dog_ref = rest[:7]
        recv_refs, (dstate_ref, dcpad_ref, dw_part_ref, send_sems, recv_sems, local_sems) = rest[7:7 + ng], rest[7 + ng:]
        n = pl.program_id(0)
        start_exchange, wait_exchange = _direct_exchange(grad_refs, recv_refs, send_sems, recv_sems, local_sems, False)
        pl.when(n == 0)(start_exchange)

        @pl.when(n == 0)
        def _():
            dw_part_ref[...] = jnp.zeros_like(dw_part_ref)
            dalog_ref[...] = jnp.zeros_like(dalog_ref)
            ddtb_ref[...] = jnp.zeros_like(ddtb_ref)
            dog_ref[...] = jnp.zeros_like(dog_ref)
            dstate_ref[...] = jnp.zeros_like(dstate_ref)
            dcpad_ref[:, DN_CHUNK:, :] = jnp.zeros((nb, CONV_HALO, 3 * DN_WIDTH), F32)

        cq, ck, cv, z = _dn_batch_args(c_ref, z_ref)
        d_out_g = jnp.stack([do_ref[b, :, pl.ds(h * DN_HEAD_DIM, DN_HEAD_DIM)] for b, h in pairs])
        t_known = inv_ref[...].astype(F32)
        core = lambda *args: _dn_core(*args, t_known=t_known)[:2]
        _, pull = jax.vjp(core, cq, ck, cv, z, l_ref[...], st_ref[...], alog_ref[...], dtb_ref[...], og_ref[...])
        dcq, dck, dcv, dz, dlog, dstate, dalog, ddtb, dog = pull((d_out_g, dstate_ref[...]))
        dstate_ref[...] = dstate
        dl_ref[...] = dlog.astype(dl_ref.dtype)
        dalog_ref[...] += dalog
        ddtb_ref[...] += ddtb
        dog_ref[...] += dog
        for i, (b, h) in enumerate(pairs):
            dcpad_ref[b, 0:DN_CHUNK, pl.ds(h * DN_HEAD_DIM, DN_HEAD_DIM)] = dcq[i]
            dcpad_ref[b, 0:DN_CHUNK, pl.ds(DN_WIDTH + h * DN_HEAD_DIM, DN_HEAD_DIM)] = dck[i]
            dcpad_ref[b, 0:DN_CHUNK, pl.ds(2 * DN_WIDTH + h * DN_HEAD_DIM, DN_HEAD_DIM)] = dcv[i]
            dz_ref[b, :, pl.ds(h * DN_HEAD_DIM, DN_HEAD_DIM)] = dz[i].astype(dz_ref.dtype)
        for b in range(nb):
            xb = cur_ref[b]
            dx = None
            for j in range(CONV_K):
                shifted = dcpad_ref[b, pl.ds(CONV_K - 1 - j, DN_CHUNK), :]
                term = w_ref[j:j + 1, :] * shifted
                dx = term if dx is None else dx + term
                dw_part_ref[j] += jnp.sum((shifted * xb).reshape(DN_CHUNK // 8, 8, 3 * DN_WIDTH), axis=0)
            dqkv_ref[b] = dx.astype(dqkv_ref.dtype)
            dcpad_ref[b, DN_CHUNK:, :] = dcpad_ref[b, 0:CONV_HALO, :]

        @pl.when(n == nc - 1)
        def _():
            dw_ref[...] = jnp.sum(dw_part_ref[...], axis=1)

        pl.when(n == nc - 1)(wait_exchange)

    chunk = lambda w: pl.BlockSpec((nb, DN_CHUNK, w), lambda n: (0, rev(n), 0))
    return pl.pallas_call(
        body, name="deltanet_bwd", grid=(nc,),
        out_shape=(jax.ShapeDtypeStruct((nb, s, 3 * DN_WIDTH), BF16), jax.ShapeDtypeStruct((nb, s, DN_WIDTH), BF16),
                   jax.ShapeDtypeStruct((nb, s, GATE_PAD), BF16), jax.ShapeDtypeStruct((CONV_K, 3 * DN_WIDTH), F32),
                   jax.ShapeDtypeStruct((1, GATE_PAD), F32), jax.ShapeDtypeStruct((1, GATE_PAD), F32),
                   jax.ShapeDtypeStruct((1, DN_HEAD_DIM), F32))
        + tuple(jax.ShapeDtypeStruct(a.shape, a.dtype) for a in head_grads),
        in_specs=[chunk(3 * DN_WIDTH), chunk(3 * DN_WIDTH), chunk(DN_WIDTH), chunk(GATE_PAD)] + _dn_weight_specs() + [
            pl.BlockSpec((None, gn, DN_HEAD_DIM, DN_HEAD_DIM), lambda n: (rev(n), 0, 0, 0)),
            pl.BlockSpec((None, gn, DN_CHUNK, DN_CHUNK), lambda n: (rev(n), 0, 0, 0)),
            chunk(DN_WIDTH)] + [HBM_SPEC] * ng,
        out_specs=(chunk(3 * DN_WIDTH), chunk(DN_WIDTH), chunk(GATE_PAD), _whole((CONV_K, 3 * DN_WIDTH)),
                   _whole((1, GATE_PAD)), _whole((1, GATE_PAD)), _whole((1, DN_HEAD_DIM))) + (HBM_SPEC,) * ng,
        scratch_shapes=[pltpu.VMEM((gn, DN_HEAD_DIM, DN_HEAD_DIM), F32),
                        pltpu.VMEM((nb, DN_CHUNK + CONV_HALO, 3 * DN_WIDTH), F32),
                        pltpu.VMEM((CONV_K, 8, 3 * DN_WIDTH), F32)] + _exchange_scratch(ng),
        compiler_params=_params(1),
    )(qkv, conv_out, zg, logits, conv_w, alog, dtb, og, states, inverses, d_out, *head_grads)


def _head(a_out, b_out, x2, p2, target, w_out, w_gate, w_proj, ple_g, fin_g):
    t = x2.shape[0]
    tm = min(512, t)
    steps = t // tm

    def body(a_ref, b_ref, x_ref, p_ref, y_ref, wo_ref, wg_ref, wp_ref, pg_ref, fg_ref,
             da_ref, db_ref, dh_ref, dwo_hbm, dwg_hbm, dwp_hbm, dpg_ref, dfg_ref, loss_ref,
             dwo_acc, dwg_acc, dwp_acc, rows_stage, cols_stage):
        i = pl.program_id(0)

        @pl.when(i == 0)
        def _():
            dwo_acc[...] = jnp.zeros_like(dwo_acc)
            dwg_acc[...] = jnp.zeros_like(dwg_acc)
            dwp_acc[...] = jnp.zeros_like(dwp_acc)
            dpg_ref[...] = jnp.zeros_like(dpg_ref)
            dfg_ref[...] = jnp.zeros_like(dfg_ref)
            loss_ref[...] = jnp.zeros_like(loss_ref)

        a = a_ref[...]
        bb = b_ref[...]
        pb = p_ref[...].astype(BF16)
        pg = pg_ref[...]
        fg = fg_ref[...]
        h1 = (x_ref[...] + jnp.dot(a, wo_ref[0:SGU_WIDTH, :], preferred_element_type=F32)
              + jnp.dot(bb, wo_ref[SGU_WIDTH:, :], preferred_element_type=F32))
        n1, r1 = _rms(h1)
        rn = (n1 * pg).astype(BF16)
        gate = _sigmoid(jnp.dot(rn, wg_ref[...], preferred_element_type=F32))
        pp = jnp.dot(pb, wp_ref[...], preferred_element_type=F32)
        h2 = h1 + gate * pp
        n2, r2 = _rms(h2)
        err = n2 * fg - y_ref[...]
        loss_ref[...] += jnp.broadcast_to(_rowsum(jnp.sum(err * err, axis=-1, keepdims=True)), loss_ref.shape)

        dy = err * (1.0 / D_MODEL)
        dfg_ref[...] += _rowsum(dy * n2)
        dh2 = _rms_bwd(dy * fg, n2, r2)
        dpp = (dh2 * gate).astype(BF16)
        dgl = (dh2 * pp * gate * (1.0 - gate)).astype(BF16)
        dwp_acc[...] += lax.dot_general(pb, dpp, (((0,), (0,)), ((), ())), preferred_element_type=F32)
        dwg_acc[...] += lax.dot_general(rn, dgl, (((0,), (0,)), ((), ())), preferred_element_type=F32)
        nt = (((1,), (1,)), ((), ()))
        drn = lax.dot_general(dgl, wg_ref[...], nt, preferred_element_type=F32)
        dpg_ref[...] += _rowsum(drn * n1)
        dh1 = dh2 + _rms_bwd(drn * pg, n1, r1)
        dh_ref[...] = dh1
        dhb = dh1.astype(BF16)
        da_ref[...] = lax.dot_general(dhb, wo_ref[0:SGU_WIDTH, :], nt, preferred_element_type=F32)
        db_ref[...] = lax.dot_general(dhb, wo_ref[SGU_WIDTH:, :], nt, preferred_element_type=F32)
        dwo_acc[0:SGU_WIDTH, :] += lax.dot_general(a, dhb, (((0,), (0,)), ((), ())), preferred_element_type=F32)
        dwo_acc[SGU_WIDTH:, :] += lax.dot_general(bb, dhb, (((0,), (0,)), ((), ())), preferred_element_type=F32)

        @pl.when(i == steps - 1)
        def _():
            for j in range(N_DEV):
                for acc, hbm in ((dwo_acc, dwo_hbm), (dwg_acc, dwg_hbm)):
                    rows_stage[...] = acc[j * LANES:(j + 1) * LANES, :].astype(BF16)
                    pltpu.sync_copy(rows_stage, hbm.at[j])
                cols_stage[...] = dwp_acc[:, j * LANES:(j + 1) * LANES].astype(BF16)
                pltpu.sync_copy(cols_stage, dwp_hbm.at[j])

    tile = lambda w: pl.BlockSpec((tm, w), lambda i: (i, 0))
    return pl.pallas_call(
        body, name="head_fwd_bwd", grid=(steps,),
        out_shape=(jax.ShapeDtypeStruct((t, SGU_WIDTH), F32), jax.ShapeDtypeStruct((t, DN_WIDTH), F32),
                   jax.ShapeDtypeStruct((t, D_MODEL), F32), jax.ShapeDtypeStruct((N_DEV, LANES, D_MODEL), BF16),
                   jax.ShapeDtypeStruct((N_DEV, LANES, D_MODEL), BF16), jax.ShapeDtypeStruct((N_DEV, PLE_DIM, LANES), BF16),
                   jax.ShapeDtypeStruct((1, D_MODEL), F32), jax.ShapeDtypeStruct((1, D_MODEL), F32),
                   jax.ShapeDtypeStruct((8, LANES), F32)),
        in_specs=[tile(SGU_WIDTH), tile(DN_WIDTH), tile(D_MODEL), tile(PLE_DIM), tile(D_MODEL),
                  VMEM_SPEC, VMEM_SPEC, VMEM_SPEC, _whole((1, D_MODEL)), _whole((1, D_MODEL))],
        out_specs=(tile(SGU_WIDTH), tile(DN_WIDTH), tile(D_MODEL), HBM_SPEC, HBM_SPEC, HBM_SPEC,
                   _whole((1, D_MODEL)), _whole((1, D_MODEL)), _whole((8, LANES))),
        scratch_shapes=[pltpu.VMEM((D_MODEL, D_MODEL), F32), pltpu.VMEM((D_MODEL, D_MODEL), F32),
                        pltpu.VMEM((PLE_DIM, D_MODEL), F32), pltpu.VMEM((LANES, D_MODEL), BF16),
                        pltpu.VMEM((PLE_DIM, LANES), BF16)],
        compiler_params=_params(1),
    )(a_out, b_out, x2, p2, target, w_out, w_gate, w_proj, ple_g, fin_g)


def _inproj_bwd(x2, dh1, a_uvz, d_sgu, d_q, d_z, d_l, norm_g, sgu_weights, wt, wgt):
    t = x2.shape[0]
    tm = min(256, t)
    steps = t // tm

    widths = (a_uvz.shape[1], d_q.shape[1], d_z.shape[1], d_l.shape[1])
    starts = (0, widths[0], widths[0] + widths[1], widths[0] + widths[1] + widths[2])

    def body(x_ref, dh_ref, uvz_ref, dsgu_ref, dq_ref, dz_ref, dl_ref, g_ref, lg_ref, lb_ref, ws_ref, bt_ref,
             wt_ref, wgt_ref,
             dx_ref, dw_hbm, dg_ref, dlg_ref, dlb_ref, dws_ref, dbt_ref, dw_acc, stage_ref, da_ref):
        i = pl.program_id(0)

        @pl.when(i == 0)
        def _():
            dw_acc[...] = jnp.zeros_like(dw_acc)
            for ref in (dg_ref, dlg_ref, dlb_ref, dws_ref, dbt_ref):
                ref[...] = jnp.zeros_like(ref)

        g = g_ref[...]
        n, r = _rms(x_ref[...])
        xn = (n * g).astype(BF16)
        dxn = None
        sgu_done = 0

        def sgu_pieces(count):
            nonlocal sgu_done
            _sgu_bwd_tile(uvz_ref, dsgu_ref, (lg_ref, lb_ref, ws_ref, bt_ref), da_ref,
                          (dlg_ref, dlb_ref, dws_ref, dbt_ref), range(sgu_done, sgu_done + count))
            sgu_done += count

        sgu_total = tm // SGU_CHUNK * SGU_GROUPS
        before_q, after_q_chunk = sgu_total // 2, (sgu_total // 4, sgu_total // 8, sgu_total // 8)
        for d_ref, col0 in reversed(tuple(zip((da_ref, dq_ref, dz_ref, dl_ref), starts))):
            if d_ref is dq_ref:
                sgu_pieces(before_q)
            if d_ref is da_ref:
                sgu_pieces(sgu_total - sgu_done)
            width = d_ref.shape[1]
            rows = wgt_ref[...] if d_ref is dl_ref else wt_ref[col0:col0 + width, :]
            term = jnp.dot(d_ref[...], rows, preferred_element_type=F32)
            dxn = term if dxn is None else dxn + term
            for c0 in range(0, width, 512):
                c1 = min(c0 + 512, width)
                dw_acc[col0 + c0:col0 + c1, :] += lax.dot_general(d_ref[:, c0:c1], xn, (((0,), (0,)), ((), ())),
                                                                  preferred_element_type=F32)
                if d_ref is dq_ref:
                    sgu_pieces(after_q_chunk[c0 // 512])
        dg_ref[...] += _rowsum(dxn * n)
        dx_ref[...] = dh_ref[...] + _rms_bwd(dxn * g, n, r)

        @pl.when(i == steps - 1)
        def _():
            for j in range(N_DEV):
                stage_ref[...] = dw_acc[j * IN_SHARD:(j + 1) * IN_SHARD, :]
                pltpu.sync_copy(stage_ref, dw_hbm.at[j])

    tile = lambda w: pl.BlockSpec((tm, w), lambda i: (i, 0))
    sgu_shapes = ((1, SGU_WIDTH), (1, SGU_WIDTH), (SGU_GROUPS, SGU_CHUNK, SGU_CHUNK), (SGU_CHUNK, SGU_GROUPS))
    return pl.pallas_call(
        body, name="inproj_sgu_bwd", grid=(steps,),
        out_shape=(jax.ShapeDtypeStruct((t, D_MODEL), F32), jax.ShapeDtypeStruct((N_DEV, IN_SHARD, D_MODEL), F32),
                   jax.ShapeDtypeStruct((1, D_MODEL), F32)) + tuple(jax.ShapeDtypeStruct(s, F32) for s in sgu_shapes),
        in_specs=[tile(D_MODEL), tile(D_MODEL), tile(widths[0]), tile(SGU_WIDTH)] + [tile(w) for w in widths[1:]]
        + [_whole((1, D_MODEL))] + [_whole(s) for s in sgu_shapes] + [VMEM_SPEC] * 2,
        out_specs=(tile(D_MODEL), HBM_SPEC, _whole((1, D_MODEL))) + tuple(_whole(s) for s in sgu_shapes),
        scratch_shapes=[pltpu.VMEM((sum(widths), D_MODEL), F32), pltpu.VMEM((IN_SHARD, D_MODEL), F32),
                        pltpu.VMEM((tm, widths[0]), BF16)],
        compiler_params=_params(1),
    )(x2, dh1, a_uvz, d_sgu, d_q, d_z, d_l, norm_g, *sgu_weights, wt, wgt)


def _reduce_adamw(recv, w, m, v, name, col_block=None):
    n, rows, cols = recv.shape
    cb = col_block or cols
    lead = w.ndim - 2

    def body(r_ref, w_ref, m_ref, v_ref, g_ref, d_ref, nm_ref, nv_ref):
        g = r_ref[0].astype(F32)
        for i in range(1, n):
            g = g + r_ref[i].astype(F32)
        m_new = ADAM_B1 * m_ref[...] + (1.0 - ADAM_B1) * g
        v_new = ADAM_B2 * v_ref[...] + (1.0 - ADAM_B2) * jnp.square(g)
        m_hat = m_new / (1.0 - ADAM_B1 ** ADAM_STEP)
        v_hat = v_new / (1.0 - ADAM_B2 ** ADAM_STEP)
        g_ref[...] = g
        d_ref[...] = -ADAM_LR * (m_hat / (jnp.sqrt(v_hat) + ADAM_EPS) + ADAM_WD * w_ref[...])
        nm_ref[...] = m_new
        nv_ref[...] = v_new

    blk = pl.BlockSpec((None,) * lead + (rows, cb), lambda i: (0,) * lead + (0, i))
    return pl.pallas_call(
        body, name=name, grid=(cols // cb,),
        out_shape=tuple(jax.ShapeDtypeStruct(w.shape, F32) for _ in range(4)),
        in_specs=[pl.BlockSpec((n, rows, cb), lambda i: (0, 0, i)), blk, blk, blk],
        out_specs=(blk, blk, blk, blk),
        compiler_params=_params(1),
    )(recv, w, m, v)


def _adamw_replicated(received, ws, ms, vs):
    nw = len(ws)
    starts = [sum(SMALL_PIECE_ROWS[:i]) for i in range(len(SMALL_PIECE_ROWS))]

    def natural(g_ref, row0, shape):
        cols, rows = shape[-1], _size(shape[:-1])
        if cols == LANES:
            return g_ref[row0:row0 + rows, :].reshape(shape)
        if cols < LANES:
            return g_ref[row0:row0 + 1, 0:cols].reshape(shape)
        per = cols // LANES
        return jnp.concatenate(
            [jnp.concatenate([g_ref[row0 + r * per + k:row0 + r * per + k + 1, :] for k in range(per)], axis=1)
             for r in range(rows)], axis=0).reshape(shape)

    def body(r_ref, *refs):
        w_refs, m_refs, v_refs = refs[:nw], refs[nw:2 * nw], refs[2 * nw:3 * nw]
        conv_ref, loss_ref = refs[3 * nw], refs[3 * nw + 1]
        out_refs, g_ref = refs[3 * nw + 2:-1], refs[-1]
        g = r_ref[0]
        for q in range(1, N_CHIPS):
            g = g + r_ref[q]
        g_ref[...] = g
        conv_ref[...] = natural(g_ref, starts[0], (CONV_K, 3 * DN_WIDTH))
        loss_ref[...] = natural(g_ref, starts[-1], (1, 1))
        for i in range(nw):
            gi = natural(g_ref, starts[1 + i], w_refs[i].shape)
            m_new = ADAM_B1 * m_refs[i][...] + (1.0 - ADAM_B1) * gi
            v_new = ADAM_B2 * v_refs[i][...] + (1.0 - ADAM_B2) * jnp.square(gi)
            m_hat = m_new / (1.0 - ADAM_B1 ** ADAM_STEP)
            v_hat = v_new / (1.0 - ADAM_B2 ** ADAM_STEP)
            out_refs[4 * i][...] = gi
            out_refs[4 * i + 1][...] = -ADAM_LR * (m_hat / (jnp.sqrt(v_hat) + ADAM_EPS) + ADAM_WD * w_refs[i][...])
            out_refs[4 * i + 2][...] = m_new
            out_refs[4 * i + 3][...] = v_new

    def spec(a):
        lead = max(a.ndim - 3, 0)
        return pl.BlockSpec((None,) * lead + a.shape[lead:], lambda: (0,) * a.ndim)

    weight_specs = [spec(a) for a in ws]
    return pl.pallas_call(
        body, name="adamw_replicated",
        out_shape=(jax.ShapeDtypeStruct((CONV_K, 3 * DN_WIDTH), F32), jax.ShapeDtypeStruct((1, 1), F32))
        + tuple(jax.ShapeDtypeStruct(a.shape, F32) for a in ws for _ in range(4)),
        in_specs=[pl.BlockSpec(received.shape, lambda: (0, 0, 0))] + weight_specs * 3,
        out_specs=(pl.BlockSpec((CONV_K, 3 * DN_WIDTH), lambda: (0, 0)), pl.BlockSpec((1, 1), lambda: (0, 0)))
        + tuple(s for s in weight_specs for _ in range(4)),
        scratch_shapes=[pltpu.VMEM(received.shape[1:], F32)],
        compiler_params=pltpu.CompilerParams(vmem_limit_bytes=VMEM_LIMIT),
    )(received, *ws, *ms, *vs)


def _pack_rows(pieces, rows):
    padded = [jnp.pad(jnp.ravel(p), (0, -p.size % LANES)) for p in pieces]
    flat = jnp.concatenate(padded)
    return jnp.pad(flat, (0, rows * LANES - flat.shape[0])).reshape(rows, LANES)


def kernel(x, p, norm_g, w_in, sgu_ln_g, sgu_ln_b, sgu_w_s, sgu_b_s, dn_conv_w, dn_a_log, dn_dt_bias, dn_o_norm_g, w_out, ple_norm_g, ple_gate_w, ple_proj_w, final_norm_g, loss_target, m_norm_g, m_w_in, m_sgu_ln_g, m_sgu_ln_b, m_sgu_w_s, m_sgu_b_s, m_dn_conv_w, m_dn_a_log, m_dn_dt_bias, m_dn_o_norm_g, m_w_out, m_ple_norm_g, m_ple_gate_w, m_ple_proj_w, m_final_norm_g, v_norm_g, v_w_in, v_sgu_ln_g, v_sgu_ln_b, v_sgu_w_s, v_sgu_b_s, v_dn_conv_w, v_dn_a_log, v_dn_dt_bias, v_dn_o_norm_g, v_w_out, v_ple_norm_g, v_ple_gate_w, v_ple_proj_w, v_final_norm_g):
    weights = dict(norm_g=norm_g, w_in=w_in, sgu_ln_g=sgu_ln_g, sgu_ln_b=sgu_ln_b, sgu_w_s=sgu_w_s, sgu_b_s=sgu_b_s,
                   dn_conv_w=dn_conv_w, dn_a_log=dn_a_log, dn_dt_bias=dn_dt_bias, dn_o_norm_g=dn_o_norm_g, w_out=w_out,
                   ple_norm_g=ple_norm_g, ple_gate_w=ple_gate_w, ple_proj_w=ple_proj_w, final_norm_g=final_norm_g)
    mom1 = dict(norm_g=m_norm_g, w_in=m_w_in, sgu_ln_g=m_sgu_ln_g, sgu_ln_b=m_sgu_ln_b, sgu_w_s=m_sgu_w_s,
                sgu_b_s=m_sgu_b_s, dn_conv_w=m_dn_conv_w, dn_a_log=m_dn_a_log, dn_dt_bias=m_dn_dt_bias,
                dn_o_norm_g=m_dn_o_norm_g, w_out=m_w_out, ple_norm_g=m_ple_norm_g, ple_gate_w=m_ple_gate_w,
                ple_proj_w=m_ple_proj_w, final_norm_g=m_final_norm_g)
    mom2 = dict(norm_g=v_norm_g, w_in=v_w_in, sgu_ln_g=v_sgu_ln_g, sgu_ln_b=v_sgu_ln_b, sgu_w_s=v_sgu_w_s,
                sgu_b_s=v_sgu_b_s, dn_conv_w=v_dn_conv_w, dn_a_log=v_dn_a_log, dn_dt_bias=v_dn_dt_bias,
                dn_o_norm_g=v_dn_o_norm_g, w_out=v_w_out, ple_norm_g=v_ple_norm_g, ple_gate_w=v_ple_gate_w,
                ple_proj_w=v_ple_proj_w, final_norm_g=v_final_norm_g)
    nb, s, _ = x.shape
    t = nb * s

    transposed = lambda a: jnp.transpose(a, (2, 0, 1)).reshape(IN_SHARD, D_MODEL)
    w_in_t, m_in_t, v_in_t = transposed(w_in), transposed(m_w_in), transposed(v_w_in)
    w_in_blocks, conv_blocks = _all_gather([w_in_t.astype(BF16), dn_conv_w[0]])
    w_in_full_t = w_in_blocks.reshape(IN_COLS, D_MODEL)
    wgt = jnp.pad(w_in_full_t[sum(IN_GROUPS):], ((0, GATE_PAD - 2 * DN_HEADS), (0, 0)))
    conv_full = jnp.moveaxis(conv_blocks, 0, 1).reshape(CONV_K, 3 * DN_WIDTH)
    later_shards = [w_out[0].astype(BF16), ple_gate_w[0].astype(BF16), ple_proj_w[0].astype(BF16)]

    pad_row = lambda a: jnp.pad(a.reshape(1, -1), ((0, 0), (DN_HEADS, GATE_PAD - DN_HEADS - a.size)))
    alog, dtb = pad_row(dn_a_log), pad_row(dn_dt_bias)
    og = dn_o_norm_g.reshape(1, DN_HEAD_DIM)
    ws = sgu_w_s.reshape(SGU_GROUPS, SGU_CHUNK, SGU_CHUNK)
    b_t = sgu_b_s.reshape(SGU_GROUPS, SGU_CHUNK).T
    fin_g = final_norm_g.reshape(1, D_MODEL)

    x2 = x.reshape(t, D_MODEL)
    sgu_weights = (sgu_ln_g, sgu_ln_b, ws, b_t)
    a_uvz, b_qkv, b_z, b_l, a_out, conv_out, w_out_blocks, w_gate_blocks, w_proj_blocks = _inproj_fwd(
        x2, s, norm_g, w_in_full_t, wgt, sgu_weights, conv_full, later_shards)
    w_out_full = w_out_blocks.reshape(D_MODEL, D_MODEL)
    w_gate_full = w_gate_blocks.reshape(D_MODEL, D_MODEL)
    w_proj_full = jnp.moveaxis(w_proj_blocks, 0, 1).reshape(PLE_DIM, D_MODEL)
    qkv3 = b_qkv.reshape(nb, s, 3 * DN_WIDTH)
    conv_out = conv_out.reshape(nb, s, 3 * DN_WIDTH)
    z3 = b_z.reshape(nb, s, DN_WIDTH)
    l3 = b_l.reshape(nb, s, GATE_PAD)
    b_out, states, inverses = _dn_fwd(conv_out, z3, l3, alog, dtb, og)

    d_a, d_b, dh1, g_w_out, g_gate, g_proj, g_ple_g, g_fin_g, loss_tile = _head(
        a_out, b_out.reshape(t, DN_WIDTH), x2, p.reshape(t, PLE_DIM), loss_target.reshape(t, D_MODEL),
        w_out_full, w_gate_full, w_proj_full, ple_norm_g, fin_g)
    d_qkv, d_z, d_l, g_conv, g_alog, g_dtb, g_og, *head_received = _dn_bwd(
        qkv3, conv_out, z3, l3, conv_full, alog, dtb, og, states, inverses, d_b.reshape(nb, s, DN_WIDTH),
        [g_w_out, g_gate, g_proj])
    grad_x, g_w_in, g_norm, g_ln_g, g_ln_b, g_ws, g_bt = _inproj_bwd(
        x2, dh1, a_uvz, d_a, d_qkv.reshape(t, 3 * DN_WIDTH), d_z.reshape(t, DN_WIDTH), d_l.reshape(t, GATE_PAD),
        norm_g, sgu_weights, w_in_full_t, wgt)

    small = _pack_rows([g_conv, g_norm, g_ln_g, g_ln_b, g_ws, g_bt.T, g_alog[:, DN_HEADS:2 * DN_HEADS], g_dtb[:, DN_HEADS:2 * DN_HEADS], g_og,
                        g_ple_g, g_fin_g, (0.5 / D_MODEL) * loss_tile[0:1, 0:1]], SMALL_ROWS)
    w_in_received, small_received = _reduce_exchange(g_w_in, small)

    results = {}
    outs = _reduce_adamw(w_in_received, w_in_t, m_in_t, v_in_t, "adamw_w_in", 4 * LANES)
    results["w_in"] = [jnp.transpose(a.reshape(IN_SHARD, 1, D_MODEL), (1, 2, 0)) for a in outs]
    for name, recv in zip(("w_out", "ple_gate_w", "ple_proj_w"), head_received):
        results[name] = _reduce_adamw(recv, weights[name], mom1[name], mom2[name], "adamw_" + name)
    names = [name for name, _ in REPLICATED]
    two_d = lambda a: a.reshape(1, -1) if a.ndim == 1 else a
    g_conv_sum, loss_sum, *flat_outs = _adamw_replicated(
        small_received, *[[two_d(src[k]) for k in names] for src in (weights, mom1, mom2)])
    for i, k in enumerate(names):
        results[k] = [a.reshape(weights[k].shape) for a in flat_outs[4 * i:4 * i + 4]]
    loss = loss_sum[0, 0]
    me = 4 * lax.axis_index("x") + 2 * lax.axis_index("y") + lax.axis_index("c")
    conv_mine = lax.dynamic_slice(g_conv_sum, (0, me * 192), (CONV_K, 192))
    results["dn_conv_w"] = _reduce_adamw(conv_mine[None], dn_conv_w, m_dn_conv_w, v_dn_conv_w, "adamw_dn_conv_w")

    return (loss, grad_x.reshape(nb, s, D_MODEL), *[results[k][0] for k in WEIGHT_ORDER],
            *[results[k][1] for k in WEIGHT_ORDER], *[results[k][2] for k in WEIGHT_ORDER],
            *[results[k][3] for k in WEIGHT_ORDER])
```

```python
import functools

import jax
import jax.numpy as jnp
from jax import lax
from jax.experimental import pallas as pl
from jax.experimental.pallas import tpu as pltpu

F32 = jnp.float32
BF16 = jnp.bfloat16

N_DEV = 8
D_MODEL = 1024
SGU_WIDTH = 512
SGU_GROUPS = 4
SGU_CHUNK = 128
DN_WIDTH = 512
DN_HEADS = 4
DN_HEAD_DIM = 128
DN_CHUNK = 128
CONV_K = 4
CONV_HALO = 8
PLE_DIM = 256
EPS = 1e-6
IN_COLS = 3592
IN_SHARD = IN_COLS // N_DEV
GATE_PAD = 128
IN_GROUPS = (3 * SGU_WIDTH, 3 * DN_WIDTH, DN_WIDTH)

ADAM_LR = 0.001
ADAM_B1 = 0.9
ADAM_B2 = 0.999
ADAM_EPS = 1e-08
ADAM_WD = 0.01
ADAM_STEP = 10

LANES = 128
VMEM_LIMIT = 56 * 1024 * 1024
MESH = pl.DeviceIdType.MESH

REPLICATED = (("norm_g", (1, D_MODEL)), ("sgu_ln_g", (1, SGU_WIDTH)), ("sgu_ln_b", (1, SGU_WIDTH)),
              ("sgu_w_s", (1, SGU_GROUPS, SGU_CHUNK, SGU_CHUNK)), ("sgu_b_s", (1, SGU_GROUPS, SGU_CHUNK)),
              ("dn_a_log", (1, DN_HEADS)), ("dn_dt_bias", (1, DN_HEADS)), ("dn_o_norm_g", (1, DN_HEAD_DIM)),
              ("ple_norm_g", (1, D_MODEL)), ("final_norm_g", (D_MODEL,)))
WEIGHT_ORDER = ("norm_g", "w_in", "sgu_ln_g", "sgu_ln_b", "sgu_w_s", "sgu_b_s", "dn_conv_w", "dn_a_log",
                "dn_dt_bias", "dn_o_norm_g", "w_out", "ple_norm_g", "ple_gate_w", "ple_proj_w", "final_norm_g")


def _size(shape):
    n = 1
    for s in shape:
        n *= s
    return n


SMALL_LAYOUT = (("conv", (CONV_K, 3 * DN_WIDTH)),) + REPLICATED + (("loss", (1,)),)
SMALL_PIECE_ROWS = tuple(-(-_size(s) // LANES) for _, s in SMALL_LAYOUT)
SMALL_ROWS = -(-sum(SMALL_PIECE_ROWS) // 8) * 8


def _bdot(a, b):
    return jnp.dot(a.astype(BF16), b.astype(BF16), preferred_element_type=F32)


def _sigmoid(x):
    return 0.5 * jnp.tanh(0.5 * x) + 0.5


@jax.custom_vjp
def _silu(x):
    return x * _sigmoid(x)


def _silu_fwd(x):
    s = _sigmoid(x)
    return x * s, (x, s)


def _silu_bwd(res, ct):
    x, s = res
    return (ct * (s * (1.0 + x * (1.0 - s))),)


_silu.defvjp(_silu_fwd, _silu_bwd)


def _normal_cdf(x):
    return 0.5 + 0.5 * lax.erf(x * (0.5 ** 0.5))


@jax.custom_vjp
def _gelu(x):
    return x * _normal_cdf(x)


def _gelu_fwd(x):
    cdf = _normal_cdf(x)
    return x * cdf, (x, cdf)


def _gelu_bwd(res, ct):
    x, cdf = res
    pdf = jnp.exp(-0.5 * x * x) * ((2.0 * jnp.pi) ** -0.5)
    return (ct * (cdf + x * pdf),)


_gelu.defvjp(_gelu_fwd, _gelu_bwd)


def _softplus(x):
    return jnp.maximum(x, 0.0) + jnp.log1p(jnp.exp(-jnp.abs(x)))


@jax.custom_vjp
def _l2n(x):
    return x * lax.rsqrt(jnp.sum(x * x, axis=-1, keepdims=True) + EPS)


def _l2n_fwd(x):
    r = lax.rsqrt(jnp.sum(x * x, axis=-1, keepdims=True) + EPS)
    n = x * r
    return n, (n, r)


def _l2n_bwd(res, ct):
    n, r = res
    return (r * (ct - n * jnp.sum(ct * n, axis=-1, keepdims=True)),)


_l2n.defvjp(_l2n_fwd, _l2n_bwd)


def _rms(x):
    r = lax.rsqrt(jnp.mean(x * x, axis=-1, keepdims=True) + EPS)
    return x * r, r


def _rms_bwd(dn, n, r):
    return r * (dn - n * jnp.mean(dn * n, axis=-1, keepdims=True))


@jax.custom_vjp
def _rms_normed(x):
    return _rms(x)[0]


def _rms_normed_fwd(x):
    n, r = _rms(x)
    return n, (n, r)


def _rms_normed_bwd(res, ct):
    return (_rms_bwd(ct, *res),)


_rms_normed.defvjp(_rms_normed_fwd, _rms_normed_bwd)


def _onehot_row(idx, width):
    return (lax.broadcasted_iota(jnp.int32, (1, width), 1) == idx).astype(F32)


def _rowsum(x):
    return jnp.sum(x, axis=0, keepdims=True)


def _iota2(n):
    return lax.broadcasted_iota(jnp.int32, (n, n), 0), lax.broadcasted_iota(jnp.int32, (n, n), 1)


def _bmm(a, b):
    return lax.dot_general(a.astype(BF16), b.astype(BF16), (((2,), (1,)), ((0,), (0,))), preferred_element_type=F32)


def _bmm_nt(a, b):
    return lax.dot_general(a.astype(BF16), b.astype(BF16), (((2,), (2,)), ((0,), (0,))), preferred_element_type=F32)


def _bmm_tn(a, b):
    return lax.dot_general(a.astype(BF16), b.astype(BF16), (((1,), (1,)), ((0,), (0,))), preferred_element_type=F32)


def _tri_inv_impl(a):
    n = a.shape[-1]
    r, c = _iota2(n)
    x = r ^ c
    eye = (r == c).astype(F32)
    ad = jnp.where(x < 16, a, 0.0)
    p2 = _bmm(ad, ad)
    e = p2 - ad - _bmm(ad, p2)
    p4 = _bmm(p2, p2)
    e = e + p4 + _bmm(e, p4)
    p8 = _bmm(p4, p4)
    e = e + p8 + _bmm(e, p8)
    size = 16
    while size < n:
        m = jnp.where(jnp.logical_and(x < 2 * size, x >= size), a, 0.0)
        f = m + _bmm(m, e)
        e = e - f - _bmm(e, f)
        size *= 2
    return e + eye


@jax.custom_vjp
def _tri_inv(a, known):
    return _tri_inv_impl(a) if known is None else known


def _tri_inv_fwd(a, known):
    t = _tri_inv(a, known)
    return t, (t, known)


def _tri_inv_bwd(res, dt):
    t, known = res
    return -_bmm_tn(t, _bmm_nt(dt, t)), None if known is None else jnp.zeros_like(known)


_tri_inv.defvjp(_tri_inv_fwd, _tri_inv_bwd)


@jax.custom_vjp
def _standardized(x):
    xc = x - jnp.mean(x, axis=-1, keepdims=True)
    return xc * lax.rsqrt(jnp.mean(xc * xc, axis=-1, keepdims=True) + EPS)


def _standardized_fwd(x):
    xc = x - jnp.mean(x, axis=-1, keepdims=True)
    rstd = lax.rsqrt(jnp.mean(xc * xc, axis=-1, keepdims=True) + EPS)
    y = xc * rstd
    return y, (y, rstd)


def _standardized_bwd(res, ct):
    y, rstd = res
    return (rstd * (ct - jnp.mean(ct, axis=-1, keepdims=True) - y * jnp.mean(ct * y, axis=-1, keepdims=True)),)


_standardized.defvjp(_standardized_fwd, _standardized_bwd)


def _sgu_core(u, v, z, lg, lb, ws, bcol):
    n = ws.shape[0]
    r, c = _iota2(n)
    wm = jnp.where(r >= c, ws, 0.0)
    gu = _gelu(u)
    gv = _gelu(v)
    ln = _standardized(gv) * lg + lb
    s = _bdot(wm, ln) + bcol
    return gu * s * _silu(z)


def _lanes_of(x):
    return jnp.concatenate([x[i] for i in range(x.shape[0])], axis=1)


def _batch_of(x, width):
    return jnp.concatenate([x[None, :, i * width:(i + 1) * width] for i in range(x.shape[1] // width)], axis=0)


def _mask_dot(mask, x):
    hi = x.astype(BF16)
    lo = (x - hi.astype(F32)).astype(BF16)
    m = mask.astype(BF16)
    return jnp.dot(m, hi, preferred_element_type=F32) + jnp.dot(m, lo, preferred_element_type=F32)


def _split_dot(x, mask, dims):
    hi = x.astype(BF16)
    lo = (x - hi.astype(F32)).astype(BF16)
    m = mask.astype(BF16)
    return (lax.dot_general(hi, m, dims, preferred_element_type=F32)
            + lax.dot_general(lo, m, dims, preferred_element_type=F32))


def _lane_select(lanes, blocks, first_lane):
    src = lax.broadcasted_iota(jnp.int32, (lanes, blocks * LANES), 0)
    dst = lax.broadcasted_iota(jnp.int32, (lanes, blocks * LANES), 1) // LANES
    return src == dst + first_lane


def _pick_lanes(x, blocks, first_lane):
    return _pick_lanes_vjp(blocks, first_lane, x)


@functools.partial(jax.custom_vjp, nondiff_argnums=(0, 1))
def _pick_lanes_vjp(blocks, first_lane, x):
    return _split_dot(x, _lane_select(x.shape[-1], blocks, first_lane), (((1,), (0,)), ((), ())))


def _pick_lanes_fwd(blocks, first_lane, x):
    return _pick_lanes_vjp(blocks, first_lane, x), x.shape[-1]


def _pick_lanes_bwd(blocks, first_lane, lanes, ct):
    return (sum(jnp.sum(ct[:, h * LANES:(h + 1) * LANES], axis=-1, keepdims=True) * _onehot_row(first_lane + h, lanes)
                for h in range(blocks)),)


_pick_lanes_vjp.defvjp(_pick_lanes_fwd, _pick_lanes_bwd)


def _tri_mask(n, upper):
    r, c = _iota2(n)
    return (r <= c) if upper else (r >= c)


@jax.custom_vjp
def _cumsum_rows(x):
    return _mask_dot(_tri_mask(x.shape[0], False), x)


def _cumsum_rows_fwd(x):
    return _cumsum_rows(x), None


def _cumsum_rows_bwd(_, ct):
    return (_mask_dot(_tri_mask(ct.shape[0], True), ct),)


_cumsum_rows.defvjp(_cumsum_rows_fwd, _cumsum_rows_bwd)


@jax.custom_vjp
def _colsum_all_rows(x):
    return _mask_dot(jnp.ones((x.shape[0], x.shape[0]), jnp.bool_), x)


def _colsum_all_rows_fwd(x):
    return _colsum_all_rows(x), None


def _colsum_all_rows_bwd(_, ct):
    return (_mask_dot(jnp.ones((ct.shape[0], ct.shape[0]), jnp.bool_), ct),)


_colsum_all_rows.defvjp(_colsum_all_rows_fwd, _colsum_all_rows_bwd)


def _dn_core(cq, ck, cv, z, logits, state, alog, dtb, og, t_known=None):
    gn, cn, dh = cq.shape
    heads = gn // logits.shape[0]
    q = _l2n(_silu(cq)) * (dh ** -0.5)
    k = _l2n(_silu(ck))
    v = _silu(cv)
    beta_lanes = _sigmoid(logits)
    g_lanes = -jnp.exp(alog) * _softplus(logits + dtb)
    beta_all = jnp.concatenate([_pick_lanes(beta_lanes[b], heads, 0) for b in range(logits.shape[0])], axis=1)
    g_all = jnp.concatenate([_pick_lanes(g_lanes[b], heads, heads) for b in range(logits.shape[0])], axis=1)
    beta = _batch_of(beta_all, dh)
    g_wide = _batch_of(g_all, dh)
    r, c = _iota2(cn)
    tril = r >= c
    rw = lax.broadcasted_iota(jnp.int32, (cn, dh), 0)
    cw = lax.broadcasted_iota(jnp.int32, (cn, dh), 1)
    upper_wide = (rw <= cw).astype(F32)
    gc_wide = _batch_of(_cumsum_rows(g_all), dh)
    gc_cols = _batch_of(_colsum_all_rows(_lanes_of(g_wide * upper_wide)), dh)[:, :, :cn]
    decay = jnp.exp(jnp.where(tril, gc_wide[:, :, :cn] - gc_cols, -1e30))
    kb = k * beta
    kk = _bmm_nt(kb, k) * decay
    t = _tri_inv(jnp.where(r > c, kk, 0.0), t_known)
    eg = jnp.exp(gc_wide)
    sol = _bmm(t, jnp.concatenate([v * beta, kb * eg], axis=-1))
    u_val, w_dec = sol[:, :, :dh], sol[:, :, dh:]
    qk = _bmm_nt(q, k) * decay
    g_last = jnp.sum(g_wide, axis=1, keepdims=True)
    k_dec = k * jnp.exp(g_last - gc_wide)
    ws = _bmm(jnp.concatenate([w_dec, q * eg], axis=1), state)
    v_new = u_val - ws[:, :cn]
    o = ws[:, cn:] + _bmm(qk, v_new)
    new_state = state * jnp.exp(g_last) + _bmm_tn(k_dec, v_new)
    return _rms_normed(o) * og * _silu(z), new_state, t


N_CHIPS = 4
HBM_SPEC = pl.BlockSpec(memory_space=pl.ANY)


def _place():
    return lax.axis_index("x"), lax.axis_index("y"), lax.axis_index("c")


def _other_chip(k):
    x, y, _ = _place()
    px = 1 - x if k & 2 else x
    py = 1 - y if k & 1 else y
    return px, py, 2 * px + py


def _remote(src, dst, send_sem, recv_sem, device):
    return pltpu.make_async_remote_copy(src_ref=src, dst_ref=dst, send_sem=send_sem, recv_sem=recv_sem,
                                        device_id=device, device_id_type=MESH)


def _other_device(k):
    x, y, c = _place()
    px = 1 - x if k & 4 else x
    py = 1 - y if k & 2 else y
    pc = 1 - c if k & 1 else c
    return (px, py, pc), 4 * px + 2 * py + pc


def _direct_exchange(srcs, outs, send_sems, recv_sems, local_sems, gather):
    x, y, c = _place()
    me = 4 * x + 2 * y + c

    def copies(arriving):
        out_list = []
        for a, (src, out) in enumerate(zip(srcs, outs)):
            for k in range(1, N_DEV):
                peer, index = _other_device(k)
                mine = src if gather else src.at[index]
                out_list.append(_remote(mine, out.at[index if arriving else me], send_sems.at[a, k - 1],
                                        recv_sems.at[a, k - 1], peer))
        return out_list

    def local_copies():
        return [pltpu.make_async_copy(src if gather else src.at[me], out.at[me], local_sems.at[a])
                for a, (src, out) in enumerate(zip(srcs, outs))]

    def start():
        for cp in local_copies() + copies(False):
            cp.start()

    def wait():
        for cp in copies(True):
            cp.wait_recv()
        for cp in copies(False):
            cp.wait_send()
        for cp in local_copies():
            cp.wait()

    return start, wait


def _exchange_scratch(n):
    return [pltpu.SemaphoreType.DMA((n, N_DEV - 1)), pltpu.SemaphoreType.DMA((n, N_DEV - 1)), pltpu.SemaphoreType.DMA((n,))]


def _all_gather(shards):
    n = len(shards)

    def body(*refs):
        srcs, outs = refs[:n], refs[n:2 * n]
        send_sems, recv_sems, local_sems = refs[2 * n:]
        x, y, c = _place()
        me = 4 * x + 2 * y + c
        sibling = (x, y, 1 - c)
        local = [pltpu.make_async_copy(srcs[a], outs[a].at[me], local_sems.at[a]) for a in range(n)]
        for cp in local:
            cp.start()
        sends = []
        for a in range(n):
            sends.append(_remote(srcs[a], outs[a].at[me], send_sems.at[a, 0], recv_sems.at[a, 0], sibling))
        for k in range(1, N_CHIPS):
            px, py, _ = _other_chip(k)
            for a in range(n):
                sends.append(_remote(srcs[a], outs[a].at[me], send_sems.at[a, k], recv_sems.at[a, k], (px, py, c)))
        for cp in sends:
            cp.start()
        passed = []
        for k in range(1, N_CHIPS):
            px, py, _ = _other_chip(k)
            blk = 4 * px + 2 * py + c
            for a in range(n):
                _remote(srcs[a], outs[a].at[blk], send_sems.at[a, k], recv_sems.at[a, k], (px, py, c)).wait_recv()
            for a in range(n):
                cp = _remote(outs[a].at[blk], outs[a].at[blk], send_sems.at[a, 3 + k], recv_sems.at[a, 3 + k], sibling)
                cp.start()
                passed.append(cp)
        for a in range(n):
            _remote(srcs[a], outs[a].at[me + 1 - 2 * c], send_sems.at[a, 0], recv_sems.at[a, 0], sibling).wait_recv()
        for k in range(1, N_CHIPS):
            px, py, _ = _other_chip(k)
            blk = 4 * px + 2 * py + 1 - c
            for a in range(n):
                _remote(srcs[a], outs[a].at[blk], send_sems.at[a, 3 + k], recv_sems.at[a, 3 + k], sibling).wait_recv()
        for cp in sends + passed:
            cp.wait_send()
        for cp in local:
            cp.wait()

    return pl.pallas_call(
        body, name="all_gather_weights",
        out_shape=tuple(jax.ShapeDtypeStruct((N_DEV,) + a.shape, a.dtype) for a in shards),
        in_specs=[HBM_SPEC] * n, out_specs=(HBM_SPEC,) * n,
        scratch_shapes=[pltpu.SemaphoreType.DMA((n, N_DEV - 1)), pltpu.SemaphoreType.DMA((n, N_DEV - 1)),
                        pltpu.SemaphoreType.DMA((n,))],
    )(*shards)


def _reduce_exchange(by_device, small):
    _, rows, cols = by_device.shape

    def body(g_ref, small_ref, out_ref, small_out_ref, from_sibling, small_from_sibling, stage, sums, small_own, small_sum,
             pair_send, pair_recv, chip_send, chip_recv, local_sems):
        x, y, c = _place()
        mine = 2 * x + y
        sibling = (x, y, 1 - c)
        chips = [(x, y, mine)] + [_other_chip(k) for k in range(1, N_CHIPS)]
        to_sibling = [_remote(g_ref.at[2 * chips[k][2] + 1 - c], from_sibling.at[k], pair_send.at[k], pair_recv.at[k], sibling)
                      for k in range(N_CHIPS)]
        to_sibling.append(_remote(small_ref, small_from_sibling, pair_send.at[N_CHIPS], pair_recv.at[N_CHIPS], sibling))
        for cp in to_sibling:
            cp.start()
        small_mine = pltpu.make_async_copy(small_ref, small_own, local_sems.at[0])
        small_mine.start()
        to_chips = []
        for k in (1, 2, 3, 0):
            px, py, chip = chips[k]
            mine_k = pltpu.make_async_copy(g_ref.at[2 * chip + c], stage, local_sems.at[1])
            mine_k.start()
            to_sibling[k].wait_recv()
            mine_k.wait()
            sums[k] = (stage[...] + from_sibling[k]).astype(sums.dtype)
            if k:
                cp = _remote(sums.at[k], out_ref.at[mine], chip_send.at[0, k - 1], chip_recv.at[0, k - 1], (px, py, c))
                cp.start()
                to_chips.append(cp)
        own_block = pltpu.make_async_copy(sums.at[0], out_ref.at[mine], local_sems.at[2])
        own_block.start()
        to_sibling[N_CHIPS].wait_recv()
        small_mine.wait()
        small_sum[...] = small_own[...] + small_from_sibling[...]
        for k in range(1, N_CHIPS):
            px, py, _ = chips[k]
            cp = _remote(small_sum, small_out_ref.at[mine], chip_send.at[1, k - 1], chip_recv.at[1, k - 1], (px, py, c))
            cp.start()
            to_chips.append(cp)
        own_small = pltpu.make_async_copy(small_sum, small_out_ref.at[mine], local_sems.at[3])
        own_small.start()
        for k in range(1, N_CHIPS):
            px, py, chip = chips[k]
            _remote(sums.at[k], out_ref.at[chip], chip_send.at[0, k - 1], chip_recv.at[0, k - 1], (px, py, c)).wait_recv()
            _remote(small_sum, small_out_ref.at[chip], chip_send.at[1, k - 1], chip_recv.at[1, k - 1], (px, py, c)).wait_recv()
        for cp in to_sibling + to_chips:
            cp.wait_send()
        own_block.wait()
        own_small.wait()

    return pl.pallas_call(
        body, name="grad_reduce_exchange",
        out_shape=(jax.ShapeDtypeStruct((N_CHIPS, rows, cols), BF16), jax.ShapeDtypeStruct((N_CHIPS,) + small.shape, F32)),
        in_specs=[HBM_SPEC, HBM_SPEC], out_specs=(HBM_SPEC, HBM_SPEC),
        scratch_shapes=[pltpu.VMEM((N_CHIPS, rows, cols), F32), pltpu.VMEM(small.shape, F32), pltpu.VMEM((rows, cols), F32),
                        pltpu.VMEM((N_CHIPS, rows, cols), BF16), pltpu.VMEM(small.shape, F32), pltpu.VMEM(small.shape, F32),
                        pltpu.SemaphoreType.DMA((N_CHIPS + 1,)), pltpu.SemaphoreType.DMA((N_CHIPS + 1,)),
                        pltpu.SemaphoreType.DMA((2, N_CHIPS - 1)), pltpu.SemaphoreType.DMA((2, N_CHIPS - 1)),
                        pltpu.SemaphoreType.DMA((4,))],
        compiler_params=pltpu.CompilerParams(vmem_limit_bytes=VMEM_LIMIT),
    )(by_device, small)


def _params(n_axes):
    return pltpu.CompilerParams(dimension_semantics=("arbitrary",) * n_axes, vmem_limit_bytes=VMEM_LIMIT)


def _whole(shape):
    return pl.BlockSpec(shape, lambda *_: (0,) * len(shape))


VMEM_SPEC = pl.BlockSpec(memory_space=pltpu.VMEM)


def _inproj_fwd(x2, seq_len, norm_g, wt, wgt, sgu_weights, conv_w, later_shards):
    t = x2.shape[0]
    tm = min(512, seq_len)
    tiles_per_seq = seq_len // tm
    steps = t // tm
    ns = len(later_shards)

    widths = IN_GROUPS + (wgt.shape[0],)
    starts = (0, IN_GROUPS[0], IN_GROUPS[0] + IN_GROUPS[1], 0)

    def body(x_ref, g_ref, wt_ref, wg_ref, lg_ref, lb_ref, ws_ref, bt_ref, cw_ref, *rest):
        shard_refs, rest = rest[:ns], rest[ns:]
        a_ref, q_ref, z_ref, l_ref, sgu_ref, c_ref = rest[:6]
        gathered_refs, (xpad_ref, send_sems, recv_sems, local_sems) = rest[6:6 + ns], rest[6 + ns:]
        start_gather, wait_gather = _direct_exchange(shard_refs, gathered_refs, send_sems, recv_sems, local_sems, True)
        pl.when(pl.program_id(0) == 0)(start_gather)

        @pl.when(pl.program_id(0) % tiles_per_seq == 0)
        def _():
            xpad_ref[0:CONV_HALO, :] = jnp.zeros((CONV_HALO, xpad_ref.shape[1]), F32)

        n, _ = _rms(x_ref[...])
        xn = (n * g_ref[...]).astype(BF16)

        def project(w_ref, row0, width, o_ref):
            for c0 in range(0, width, 512):
                c1 = min(c0 + 512, width)
                o_ref[:, c0:c1] = lax.dot_general(xn, w_ref[row0 + c0:row0 + c1, :], (((1,), (1,)), ((), ())),
                                                  preferred_element_type=F32)

        def sgu_rows(row0):
            for grp in range(SGU_GROUPS):
                args = _sgu_pieces(a_ref, lg_ref, lb_ref, ws_ref, bt_ref, row0, grp)
                sgu_ref[pl.ds(row0, SGU_CHUNK), pl.ds(grp * 128, 128)] = _sgu_core(*args).astype(sgu_ref.dtype)

        def conv():
            xpad_ref[CONV_HALO:, :] = q_ref[...]
            acc = None
            for j in range(CONV_K):
                term = cw_ref[j:j + 1, :] * xpad_ref[pl.ds(CONV_HALO - CONV_K + 1 + j, tm), :]
                acc = term if acc is None else acc + term
            c_ref[...] = acc
            xpad_ref[0:CONV_HALO, :] = xpad_ref[tm:tm + CONV_HALO, :]

        groups = tuple(zip((wt_ref, wt_ref, wt_ref, wg_ref), starts, widths, (a_ref, q_ref, z_ref, l_ref)))
        row_chunks = list(range(0, tm, SGU_CHUNK))
        project(*groups[0])
        for row0 in row_chunks[:len(row_chunks) // 2]:
            sgu_rows(row0)
        project(*groups[1])
        for row0 in row_chunks[len(row_chunks) // 2:]:
            sgu_rows(row0)
        project(*groups[3])
        conv()
        project(*groups[2])
        pl.when(pl.program_id(0) == steps - 1)(wait_gather)

    tile = lambda w: pl.BlockSpec((tm, w), lambda i: (i, 0))
    sgu_shapes = ((1, SGU_WIDTH), (1, SGU_WIDTH), (SGU_GROUPS, SGU_CHUNK, SGU_CHUNK), (SGU_CHUNK, SGU_GROUPS))
    return pl.pallas_call(
        body, name="inproj_sgu_conv_fwd", grid=(steps,),
        out_shape=tuple(jax.ShapeDtypeStruct((t, w), F32) for w in widths)
        + (jax.ShapeDtypeStruct((t, SGU_WIDTH), BF16), jax.ShapeDtypeStruct((t, widths[1]), F32))
        + tuple(jax.ShapeDtypeStruct((N_DEV,) + a.shape, a.dtype) for a in later_shards),
        in_specs=[tile(D_MODEL), _whole((1, D_MODEL)), VMEM_SPEC, VMEM_SPEC]
        + [_whole(s) for s in sgu_shapes] + [_whole((CONV_K, widths[1]))] + [HBM_SPEC] * ns,
        out_specs=tuple(tile(w) for w in widths) + (tile(SGU_WIDTH), tile(widths[1])) + (HBM_SPEC,) * ns,
        scratch_shapes=[pltpu.VMEM((CONV_HALO + tm, widths[1]), F32)] + _exchange_scratch(ns),
        compiler_params=_params(1),
    )(x2, norm_g, wt, wgt, *sgu_weights, conv_w, *later_shards)


def _sgu_pieces(uvz_ref, lg_ref, lb_ref, ws_ref, bt_ref, row0, grp):
    rows = pl.ds(row0, SGU_CHUNK)
    lanes = pl.ds(grp * 128, 128)
    u = uvz_ref[rows, pl.ds(grp * 128, 128)]
    v = uvz_ref[rows, pl.ds(SGU_WIDTH + grp * 128, 128)]
    z = uvz_ref[rows, pl.ds(2 * SGU_WIDTH + grp * 128, 128)]
    bcol = jnp.sum(bt_ref[...] * _onehot_row(grp, SGU_GROUPS), axis=-1, keepdims=True)
    return u, v, z, lg_ref[:, lanes], lb_ref[:, lanes], ws_ref[grp], bcol


def _sgu_bwd_tile(uvz_ref, do_ref, sgu_refs, duvz_ref, grad_refs, pieces):
    lg_ref, lb_ref, ws_ref, bt_ref = sgu_refs
    dlg_ref, dlb_ref, dws_ref, dbt_ref = grad_refs
    for piece in pieces:
        row0, grp = piece // SGU_GROUPS * SGU_CHUNK, piece % SGU_GROUPS
        rows = pl.ds(row0, SGU_CHUNK)
        lanes = pl.ds(grp * 128, 128)
        args = _sgu_pieces(uvz_ref, lg_ref, lb_ref, ws_ref, bt_ref, row0, grp)
        _, pull = jax.vjp(_sgu_core, *args)
        du, dv, dz, dlg, dlb, dws, dbcol = pull(do_ref[rows, lanes])
        duvz_ref[rows, pl.ds(grp * 128, 128)] = du.astype(duvz_ref.dtype)
        duvz_ref[rows, pl.ds(SGU_WIDTH + grp * 128, 128)] = dv.astype(duvz_ref.dtype)
        duvz_ref[rows, pl.ds(2 * SGU_WIDTH + grp * 128, 128)] = dz.astype(duvz_ref.dtype)
        dlg_ref[:, lanes] += dlg
        dlb_ref[:, lanes] += dlb
        dws_ref[grp] += dws
        dbt_ref[...] += dbcol * _onehot_row(grp, SGU_GROUPS)


def _dn_pairs(nb):
    return [(b, h) for b in range(nb) for h in range(DN_HEADS)]


def _dn_batch_args(c_ref, z_ref):
    pairs = _dn_pairs(c_ref.shape[0])
    pick = lambda ref, b, col: ref[b, :, pl.ds(col, DN_HEAD_DIM)]
    cq = jnp.stack([pick(c_ref, b, h * DN_HEAD_DIM) for b, h in pairs])
    ck = jnp.stack([pick(c_ref, b, DN_WIDTH + h * DN_HEAD_DIM) for b, h in pairs])
    cv = jnp.stack([pick(c_ref, b, 2 * DN_WIDTH + h * DN_HEAD_DIM) for b, h in pairs])
    z = jnp.stack([pick(z_ref, b, h * DN_HEAD_DIM) for b, h in pairs])
    return cq, ck, cv, z


def _dn_weight_specs():
    return [_whole((CONV_K, 3 * DN_WIDTH)), _whole((1, GATE_PAD)), _whole((1, GATE_PAD)), _whole((1, DN_HEAD_DIM))]


def _dn_fwd(conv_out, zg, logits, alog, dtb, og):
    nb, s, _ = conv_out.shape
    nc = s // DN_CHUNK
    pairs = _dn_pairs(nb)
    gn = len(pairs)
    chunk = lambda w: pl.BlockSpec((nb, DN_CHUNK, w), lambda n: (0, n, 0))

    def body(c_ref, z_ref, l_ref, alog_ref, dtb_ref, og_ref, out_ref, st_ref, inv_ref, state_ref):
        n = pl.program_id(0)

        @pl.when(n == 0)
        def _():
            state_ref[...] = jnp.zeros_like(state_ref)

        cq, ck, cv, z = _dn_batch_args(c_ref, z_ref)
        state = state_ref[...]
        st_ref[...] = state
        out, new_state, t = _dn_core(cq, ck, cv, z, l_ref[...], state, alog_ref[...], dtb_ref[...], og_ref[...])
        state_ref[...] = new_state
        inv_ref[...] = t.astype(inv_ref.dtype)
        for i, (b, h) in enumerate(pairs):
            out_ref[b, :, pl.ds(h * DN_HEAD_DIM, DN_HEAD_DIM)] = out[i].astype(out_ref.dtype)

    per_chunk = pl.BlockSpec((None, gn, DN_HEAD_DIM, DN_HEAD_DIM), lambda n: (n, 0, 0, 0))
    return pl.pallas_call(
        body, name="deltanet_fwd", grid=(nc,),
        out_shape=(jax.ShapeDtypeStruct((nb, s, DN_WIDTH), BF16),
                   jax.ShapeDtypeStruct((nc, gn, DN_HEAD_DIM, DN_HEAD_DIM), F32),
                   jax.ShapeDtypeStruct((nc, gn, DN_CHUNK, DN_CHUNK), BF16)),
        in_specs=[chunk(3 * DN_WIDTH), chunk(DN_WIDTH), chunk(GATE_PAD)] + _dn_weight_specs()[1:],
        out_specs=(chunk(DN_WIDTH), per_chunk, pl.BlockSpec((None, gn, DN_CHUNK, DN_CHUNK), lambda n: (n, 0, 0, 0))),
        scratch_shapes=[pltpu.VMEM((gn, DN_HEAD_DIM, DN_HEAD_DIM), F32)],
        compiler_params=_params(1),
    )(conv_out, zg, logits, alog, dtb, og)


def _dn_bwd(qkv, conv_out, zg, logits, conv_w, alog, dtb, og, states, inverses, d_out, head_grads):
    nb, s, _ = qkv.shape
    nc = s // DN_CHUNK
    rev = lambda n: nc - 1 - n
    pairs = _dn_pairs(nb)
    gn = len(pairs)
    ng = len(head_grads)

    def body(cur_ref, c_ref, z_ref, l_ref, w_ref, alog_ref, dtb_ref, og_ref, st_ref, inv_ref, do_ref, *rest):
        grad_refs, rest = rest[:ng], rest[ng:]
        dqkv_ref, dz_ref, dl_ref, dw_ref, dalog_ref, ddtb_ref, dog_ref = rest[:7]
        recv_refs, (dstate_ref, dcpad_ref, dw_part_ref, send_sems, recv_sems, local_sems) = rest[7:7 + ng], rest[7 + ng:]
        n = pl.program_id(0)
        start_exchange, wait_exchange = _direct_exchange(grad_refs, recv_refs, send_sems, recv_sems, local_sems, False)
        pl.when(n == 0)(start_exchange)

        @pl.when(n == 0)
        def _():
            dw_part_ref[...] = jnp.zeros_like(dw_part_ref)
            dalog_ref[...] = jnp.zeros_like(dalog_ref)
            ddtb_ref[...] = jnp.zeros_like(ddtb_ref)
            dog_ref[...] = jnp.zeros_like(dog_ref)
            dstate_ref[...] = jnp.zeros_like(dstate_ref)
            dcpad_ref[:, DN_CHUNK:, :] = jnp.zeros((nb, CONV_HALO, 3 * DN_WIDTH), F32)

        cq, ck, cv, z = _dn_batch_args(c_ref, z_ref)
        d_out_g = jnp.stack([do_ref[b, :, pl.ds(h * DN_HEAD_DIM, DN_HEAD_DIM)] for b, h in pairs])
        t_known = inv_ref[...].astype(F32)
        core = lambda *args: _dn_core(*args, t_known=t_known)[:2]
        _, pull = jax.vjp(core, cq, ck, cv, z, l_ref[...], st_ref[...], alog_ref[...], dtb_ref[...], og_ref[...])
        dcq, dck, dcv, dz, dlog, dstate, dalog, ddtb, dog = pull((d_out_g, dstate_ref[...]))
        dstate_ref[...] = dstate
        dl_ref[...] = dlog.astype(dl_ref.dtype)
        dalog_ref[...] += dalog
        ddtb_ref[...] += ddtb
        dog_ref[...] += dog
        for i, (b, h) in enumerate(pairs):
            dcpad_ref[b, 0:DN_CHUNK, pl.ds(h * DN_HEAD_DIM, DN_HEAD_DIM)] = dcq[i]
            dcpad_ref[b, 0:DN_CHUNK, pl.ds(DN_WIDTH + h * DN_HEAD_DIM, DN_HEAD_DIM)] = dck[i]
            dcpad_ref[b, 0:DN_CHUNK, pl.ds(2 * DN_WIDTH + h * DN_HEAD_DIM, DN_HEAD_DIM)] = dcv[i]
            dz_ref[b, :, pl.ds(h * DN_HEAD_DIM, DN_HEAD_DIM)] = dz[i].astype(dz_ref.dtype)
        for b in range(nb):
            xb = cur_ref[b]
            dx = None
            for j in range(CONV_K):
                shifted = dcpad_ref[b, pl.ds(CONV_K - 1 - j, DN_CHUNK), :]
                term = w_ref[j:j + 1, :] * shifted
                dx = term if dx is None else dx + term
                dw_part_ref[j] += jnp.sum((shifted * xb).reshape(DN_CHUNK // 8, 8, 3 * DN_WIDTH), axis=0)
            dqkv_ref[b] = dx.astype(dqkv_ref.dtype)
            dcpad_ref[b, DN_CHUNK:, :] = dcpad_ref[b, 0:CONV_HALO, :]

        @pl.when(n == nc - 1)
        def _():
            dw_ref[...] = jnp.sum(dw_part_ref[...], axis=1)

        pl.when(n == nc - 1)(wait_exchange)

    chunk = lambda w: pl.BlockSpec((nb, DN_CHUNK, w), lambda n: (0, rev(n), 0))
    return pl.pallas_call(
        body, name="deltanet_bwd", grid=(nc,),
        out_shape=(jax.ShapeDtypeStruct((nb, s, 3 * DN_WIDTH), BF16), jax.ShapeDtypeStruct((nb, s, DN_WIDTH), BF16),
                   jax.ShapeDtypeStruct((nb, s, GATE_PAD), BF16), jax.ShapeDtypeStruct((CONV_K, 3 * DN_WIDTH), F32),
                   jax.ShapeDtypeStruct((1, GATE_PAD), F32), jax.ShapeDtypeStruct((1, GATE_PAD), F32),
                   jax.ShapeDtypeStruct((1, DN_HEAD_DIM), F32))
        + tuple(jax.ShapeDtypeStruct(a.shape, a.dtype) for a in head_grads),
        in_specs=[chunk(3 * DN_WIDTH), chunk(3 * DN_WIDTH), chunk(DN_WIDTH), chunk(GATE_PAD)] + _dn_weight_specs() + [
            pl.BlockSpec((None, gn, DN_HEAD_DIM, DN_HEAD_DIM), lambda n: (rev(n), 0, 0, 0)),
            pl.BlockSpec((None, gn, DN_CHUNK, DN_CHUNK), lambda n: (rev(n), 0, 0, 0)),
            chunk(DN_WIDTH)] + [HBM_SPEC] * ng,
        out_specs=(chunk(3 * DN_WIDTH), chunk(DN_WIDTH), chunk(GATE_PAD), _whole((CONV_K, 3 * DN_WIDTH)),
                   _whole((1, GATE_PAD)), _whole((1, GATE_PAD)), _whole((1, DN_HEAD_DIM))) + (HBM_SPEC,) * ng,
        scratch_shapes=[pltpu.VMEM((gn, DN_HEAD_DIM, DN_HEAD_DIM), F32),
                        pltpu.VMEM((nb, DN_CHUNK + CONV_HALO, 3 * DN_WIDTH), F32),
                        pltpu.VMEM((CONV_K, 8, 3 * DN_WIDTH), F32)] + _exchange_scratch(ng),
        compiler_params=_params(1),
    )(qkv, conv_out, zg, logits, conv_w, alog, dtb, og, states, inverses, d_out, *head_grads)


def _head(a_out, b_out, x2, p2, target, w_out, w_gate, w_proj, ple_g, fin_g):
    t = x2.shape[0]
    tm = min(512, t)
    steps = t // tm

    def body(a_ref, b_ref, x_ref, p_ref, y_ref, wo_ref, wg_ref, wp_ref, pg_ref, fg_ref,
             da_ref, db_ref, dh_ref, dwo_hbm, dwg_hbm, dwp_hbm, dpg_ref, dfg_ref, loss_ref,
             dwo_acc, dwg_acc, dwp_acc, rows_stage, cols_stage):
        i = pl.program_id(0)

        @pl.when(i == 0)
        def _():
            dwo_acc[...] = jnp.zeros_like(dwo_acc)
            dwg_acc[...] = jnp.zeros_like(dwg_acc)
            dwp_acc[...] = jnp.zeros_like(dwp_acc)
            dpg_ref[...] = jnp.zeros_like(dpg_ref)
            dfg_ref[...] = jnp.zeros_like(dfg_ref)
            loss_ref[...] = jnp.zeros_like(loss_ref)

        pg = pg_ref[...]
        fg = fg_ref[...]
        nt = (((1,), (1,)), ((), ()))
        tn = (((0,), (0,)), ((), ()))

        def to_first_norm(rows):
            h1 = (x_ref[rows, :] + jnp.dot(a_ref[rows, :], wo_ref[0:SGU_WIDTH, :], preferred_element_type=F32)
                  + jnp.dot(b_ref[rows, :], wo_ref[SGU_WIDTH:, :], preferred_element_type=F32))
            n1, r1 = _rms(h1)
            pp = jnp.dot(p_ref[rows, :].astype(BF16), wp_ref[...], preferred_element_type=F32)
            return h1, n1, r1, (n1 * pg).astype(BF16), pp

        def to_gate_cotangents(rows, h1, rn, pp):
            gate = _sigmoid(jnp.dot(rn, wg_ref[...], preferred_element_type=F32))
            h2 = h1 + gate * pp
            n2, r2 = _rms(h2)
            err = n2 * fg - y_ref[rows, :]
            loss = _rowsum(jnp.sum(err * err, axis=-1, keepdims=True))
            dy = err * (1.0 / D_MODEL)
            dh2 = _rms_bwd(dy * fg, n2, r2)
            return loss, _rowsum(dy * n2), dh2, (dh2 * gate).astype(BF16), (dh2 * pp * gate * (1.0 - gate)).astype(BF16)

        def to_branch_cotangents(rows, dgl, dh2, n1, r1):
            drn = lax.dot_general(dgl, wg_ref[...], nt, preferred_element_type=F32)
            dh1 = dh2 + _rms_bwd(drn * pg, n1, r1)
            dh_ref[rows, :] = dh1
            dhb = dh1.astype(BF16)
            da_ref[rows, :] = lax.dot_general(dhb, wo_ref[0:SGU_WIDTH, :], nt, preferred_element_type=F32)
            db_ref[rows, :] = lax.dot_general(dhb, wo_ref[SGU_WIDTH:, :], nt, preferred_element_type=F32)
            return _rowsum(drn * n1), dhb

        parts = [pl.ds(k * (tm // 2), tm // 2) for k in range(2)]
        first = [to_first_norm(rows) for rows in parts]
        mid = [to_gate_cotangents(rows, h1, rn, pp) for rows, (h1, _, _, rn, pp) in zip(parts, first)]
        loss_ref[...] += jnp.broadcast_to(mid[0][0] + mid[1][0], loss_ref.shape)
        dfg_ref[...] += mid[0][1] + mid[1][1]
        last = [to_branch_cotangents(rows, m[4], m[2], f[1], f[2]) for rows, m, f in zip(parts, mid, first)]
        rn = jnp.concatenate([f[3] for f in first], axis=0)
        dpp = jnp.concatenate([m[3] for m in mid], axis=0)
        dgl = jnp.concatenate([m[4] for m in mid], axis=0)
        dwp_acc[...] += lax.dot_general(p_ref[...].astype(BF16), dpp, tn, preferred_element_type=F32)
        dwg_acc[...] += lax.dot_general(rn, dgl, tn, preferred_element_type=F32)
        dpg_ref[...] += last[0][0] + last[1][0]
        dhb = jnp.concatenate([l[1] for l in last], axis=0)
        dwo_acc[0:SGU_WIDTH, :] += lax.dot_general(a_ref[...], dhb, tn, preferred_element_type=F32)
        dwo_acc[SGU_WIDTH:, :] += lax.dot_general(b_ref[...], dhb, tn, preferred_element_type=F32)

        @pl.when(i == steps - 1)
        def _():
            for j in range(N_DEV):
                for acc, hbm in ((dwo_acc, dwo_hbm), (dwg_acc, dwg_hbm)):
                    rows_stage[...] = acc[j * LANES:(j + 1) * LANES, :].astype(BF16)
                    pltpu.sync_copy(rows_stage, hbm.at[j])
                cols_stage[...] = dwp_acc[:, j * LANES:(j + 1) * LANES].astype(BF16)
                pltpu.sync_copy(cols_stage, dwp_hbm.at[j])

    tile = lambda w: pl.BlockSpec((tm, w), lambda i: (i, 0))
    return pl.pallas_call(
        body, name="head_fwd_bwd", grid=(steps,),
        out_shape=(jax.ShapeDtypeStruct((t, SGU_WIDTH), F32), jax.ShapeDtypeStruct((t, DN_WIDTH), F32),
                   jax.ShapeDtypeStruct((t, D_MODEL), F32), jax.ShapeDtypeStruct((N_DEV, LANES, D_MODEL), BF16),
                   jax.ShapeDtypeStruct((N_DEV, LANES, D_MODEL), BF16), jax.ShapeDtypeStruct((N_DEV, PLE_DIM, LANES), BF16),
                   jax.ShapeDtypeStruct((1, D_MODEL), F32), jax.ShapeDtypeStruct((1, D_MODEL), F32),
                   jax.ShapeDtypeStruct((8, LANES), F32)),
        in_specs=[tile(SGU_WIDTH), tile(DN_WIDTH), tile(D_MODEL), tile(PLE_DIM), tile(D_MODEL),
                  VMEM_SPEC, VMEM_SPEC, VMEM_SPEC, _whole((1, D_MODEL)), _whole((1, D_MODEL))],
        out_specs=(tile(SGU_WIDTH), tile(DN_WIDTH), tile(D_MODEL), HBM_SPEC, HBM_SPEC, HBM_SPEC,
                   _whole((1, D_MODEL)), _whole((1, D_MODEL)), _whole((8, LANES))),
        scratch_shapes=[pltpu.VMEM((D_MODEL, D_MODEL), F32), pltpu.VMEM((D_MODEL, D_MODEL), F32),
                        pltpu.VMEM((PLE_DIM, D_MODEL), F32), pltpu.VMEM((LANES, D_MODEL), BF16),
                        pltpu.VMEM((PLE_DIM, LANES), BF16)],
        compiler_params=_params(1),
    )(a_out, b_out, x2, p2, target, w_out, w_gate, w_proj, ple_g, fin_g)


def _inproj_bwd(x2, dh1, a_uvz, d_sgu, d_q, d_z, d_l, norm_g, sgu_weights, wt, wgt):
    t = x2.shape[0]
    tm = min(256, t)
    steps = t // tm

    widths = (a_uvz.shape[1], d_q.shape[1], d_z.shape[1], d_l.shape[1])
    starts = (0, widths[0], widths[0] + widths[1], widths[0] + widths[1] + widths[2])

    def body(x_ref, dh_ref, uvz_ref, dsgu_ref, dq_ref, dz_ref, dl_ref, g_ref, lg_ref, lb_ref, ws_ref, bt_ref,
             wt_ref, wgt_ref,
             dx_ref, dw_hbm, dg_ref, dlg_ref, dlb_ref, dws_ref, dbt_ref, dw_acc, stage_ref, da_ref):
        i = pl.program_id(0)

        @pl.when(i == 0)
        def _():
            dw_acc[...] = jnp.zeros_like(dw_acc)
            for ref in (dg_ref, dlg_ref, dlb_ref, dws_ref, dbt_ref):
                ref[...] = jnp.zeros_like(ref)

        g = g_ref[...]
        n, r = _rms(x_ref[...])
        xn = (n * g).astype(BF16)
        dxn = None
        sgu_done = 0

        def sgu_pieces(count):
            nonlocal sgu_done
            _sgu_bwd_tile(uvz_ref, dsgu_ref, (lg_ref, lb_ref, ws_ref, bt_ref), da_ref,
                          (dlg_ref, dlb_ref, dws_ref, dbt_ref), range(sgu_done, sgu_done + count))
            sgu_done += count

        sgu_total = tm // SGU_CHUNK * SGU_GROUPS
        before_q, after_q_chunk = sgu_total // 2, (sgu_total // 4, sgu_total // 8, sgu_total // 8)
        for d_ref, col0 in reversed(tuple(zip((da_ref, dq_ref, dz_ref, dl_ref), starts))):
            if d_ref is dq_ref:
                sgu_pieces(before_q)
            if d_ref is da_ref:
                sgu_pieces(sgu_total - sgu_done)
            width = d_ref.shape[1]
            rows = wgt_ref[...] if d_ref is dl_ref else wt_ref[col0:col0 + width, :]
            term = jnp.dot(d_ref[...], rows, preferred_element_type=F32)
            dxn = term if dxn is None else dxn + term
            for c0 in range(0, width, 512):
                c1 = min(c0 + 512, width)
                dw_acc[col0 + c0:col0 + c1, :] += lax.dot_general(d_ref[:, c0:c1], xn, (((0,), (0,)), ((), ())),
                                                                  preferred_element_type=F32)
                if d_ref is dq_ref:
                    sgu_pieces(after_q_chunk[c0 // 512])
        dg_ref[...] += _rowsum(dxn * n)
        dx_ref[...] = dh_ref[...] + _rms_bwd(dxn * g, n, r)

        @pl.when(i == steps - 1)
        def _():
            for j in range(N_DEV):
                stage_ref[...] = dw_acc[j * IN_SHARD:(j + 1) * IN_SHARD, :]
                pltpu.sync_copy(stage_ref, dw_hbm.at[j])

    tile = lambda w: pl.BlockSpec((tm, w), lambda i: (i, 0))
    sgu_shapes = ((1, SGU_WIDTH), (1, SGU_WIDTH), (SGU_GROUPS, SGU_CHUNK, SGU_CHUNK), (SGU_CHUNK, SGU_GROUPS))
    return pl.pallas_call(
        body, name="inproj_sgu_bwd", grid=(steps,),
        out_shape=(jax.ShapeDtypeStruct((t, D_MODEL), F32), jax.ShapeDtypeStruct((N_DEV, IN_SHARD, D_MODEL), F32),
                   jax.ShapeDtypeStruct((1, D_MODEL), F32)) + tuple(jax.ShapeDtypeStruct(s, F32) for s in sgu_shapes),
        in_specs=[tile(D_MODEL), tile(D_MODEL), tile(widths[0]), tile(SGU_WIDTH)] + [tile(w) for w in widths[1:]]
        + [_whole((1, D_MODEL))] + [_whole(s) for s in sgu_shapes] + [VMEM_SPEC] * 2,
        out_specs=(tile(D_MODEL), HBM_SPEC, _whole((1, D_MODEL))) + tuple(_whole(s) for s in sgu_shapes),
        scratch_shapes=[pltpu.VMEM((sum(widths), D_MODEL), F32), pltpu.VMEM((IN_SHARD, D_MODEL), F32),
                        pltpu.VMEM((tm, widths[0]), BF16)],
        compiler_params=_params(1),
    )(x2, dh1, a_uvz, d_sgu, d_q, d_z, d_l, norm_g, *sgu_weights, wt, wgt)


def _reduce_adamw(recv, w, m, v, name, col_block=None):
    n, rows, cols = recv.shape
    cb = col_block or cols
    lead = w.ndim - 2

    def body(r_ref, w_ref, m_ref, v_ref, g_ref, d_ref, nm_ref, nv_ref):
        g = r_ref[0].astype(F32)
        for i in range(1, n):
            g = g + r_ref[i].astype(F32)
        m_new = ADAM_B1 * m_ref[...] + (1.0 - ADAM_B1) * g
        v_new = ADAM_B2 * v_ref[...] + (1.0 - ADAM_B2) * jnp.square(g)
        m_hat = m_new / (1.0 - ADAM_B1 ** ADAM_STEP)
        v_hat = v_new / (1.0 - ADAM_B2 ** ADAM_STEP)
        g_ref[...] = g
        d_ref[...] = -ADAM_LR * (m_hat / (jnp.sqrt(v_hat) + ADAM_EPS) + ADAM_WD * w_ref[...])
        nm_ref[...] = m_new
        nv_ref[...] = v_new

    blk = pl.BlockSpec((None,) * lead + (rows, cb), lambda i: (0,) * lead + (0, i))
    return pl.pallas_call(
        body, name=name, grid=(cols // cb,),
        out_shape=tuple(jax.ShapeDtypeStruct(w.shape, F32) for _ in range(4)),
        in_specs=[pl.BlockSpec((n, rows, cb), lambda i: (0, 0, i)), blk, blk, blk],
        out_specs=(blk, blk, blk, blk),
        compiler_params=_params(1),
    )(recv, w, m, v)


def _adamw_replicated(received, ws, ms, vs):
    nw = len(ws)
    starts = [sum(SMALL_PIECE_ROWS[:i]) for i in range(len(SMALL_PIECE_ROWS))]

    def natural(g_ref, row0, shape):
        cols, rows = shape[-1], _size(shape[:-1])
        if cols == LANES:
            return g_ref[row0:row0 + rows, :].reshape(shape)
        if cols < LANES:
            return g_ref[row0:row0 + 1, 0:cols].reshape(shape)
        per = cols // LANES
        return jnp.concatenate(
            [jnp.concatenate([g_ref[row0 + r * per + k:row0 + r * per + k + 1, :] for k in range(per)], axis=1)
             for r in range(rows)], axis=0).reshape(shape)

    def body(r_ref, *refs):
        w_refs, m_refs, v_refs = refs[:nw], refs[nw:2 * nw], refs[2 * nw:3 * nw]
        conv_ref, loss_ref = refs[3 * nw], refs[3 * nw + 1]
        out_refs, g_ref = refs[3 * nw + 2:-1], refs[-1]
        g = r_ref[0]
        for q in range(1, N_CHIPS):
            g = g + r_ref[q]
        g_ref[...] = g
        conv_ref[...] = natural(g_ref, starts[0], (CONV_K, 3 * DN_WIDTH))
        loss_ref[...] = natural(g_ref, starts[-1], (1, 1))
        for i in range(nw):
            gi = natural(g_ref, starts[1 + i], w_refs[i].shape)
            m_new = ADAM_B1 * m_refs[i][...] + (1.0 - ADAM_B1) * gi
            v_new = ADAM_B2 * v_refs[i][...] + (1.0 - ADAM_B2) * jnp.square(gi)
            m_hat = m_new / (1.0 - ADAM_B1 ** ADAM_STEP)
            v_hat = v_new / (1.0 - ADAM_B2 ** ADAM_STEP)
            out_refs[4 * i][...] = gi
            out_refs[4 * i + 1][...] = -ADAM_LR * (m_hat / (jnp.sqrt(v_hat) + ADAM_EPS) + ADAM_WD * w_refs[i][...])
            out_refs[4 * i + 2][...] = m_new
            out_refs[4 * i + 3][...] = v_new

    def spec(a):
        lead = max(a.ndim - 3, 0)
        return pl.BlockSpec((None,) * lead + a.shape[lead:], lambda: (0,) * a.ndim)

    weight_specs = [spec(a) for a in ws]
    return pl.pallas_call(
        body, name="adamw_replicated",
        out_shape=(jax.ShapeDtypeStruct((CONV_K, 3 * DN_WIDTH), F32), jax.ShapeDtypeStruct((1, 1), F32))
        + tuple(jax.ShapeDtypeStruct(a.shape, F32) for a in ws for _ in range(4)),
        in_specs=[pl.BlockSpec(received.shape, lambda: (0, 0, 0))] + weight_specs * 3,
        out_specs=(pl.BlockSpec((CONV_K, 3 * DN_WIDTH), lambda: (0, 0)), pl.BlockSpec((1, 1), lambda: (0, 0)))
        + tuple(s for s in weight_specs for _ in range(4)),
        scratch_shapes=[pltpu.VMEM(received.shape[1:], F32)],
        compiler_params=pltpu.CompilerParams(vmem_limit_bytes=VMEM_LIMIT),
    )(received, *ws, *ms, *vs)


def _pack_rows(pieces, rows):
    padded = [jnp.pad(jnp.ravel(p), (0, -p.size % LANES)) for p in pieces]
    flat = jnp.concatenate(padded)
    return jnp.pad(flat, (0, rows * LANES - flat.shape[0])).reshape(rows, LANES)


def kernel(x, p, norm_g, w_in, sgu_ln_g, sgu_ln_b, sgu_w_s, sgu_b_s, dn_conv_w, dn_a_log, dn_dt_bias, dn_o_norm_g, w_out, ple_norm_g, ple_gate_w, ple_proj_w, final_norm_g, loss_target, m_norm_g, m_w_in, m_sgu_ln_g, m_sgu_ln_b, m_sgu_w_s, m_sgu_b_s, m_dn_conv_w, m_dn_a_log, m_dn_dt_bias, m_dn_o_norm_g, m_w_out, m_ple_norm_g, m_ple_gate_w, m_ple_proj_w, m_final_norm_g, v_norm_g, v_w_in, v_sgu_ln_g, v_sgu_ln_b, v_sgu_w_s, v_sgu_b_s, v_dn_conv_w, v_dn_a_log, v_dn_dt_bias, v_dn_o_norm_g, v_w_out, v_ple_norm_g, v_ple_gate_w, v_ple_proj_w, v_final_norm_g):
    weights = dict(norm_g=norm_g, w_in=w_in, sgu_ln_g=sgu_ln_g, sgu_ln_b=sgu_ln_b, sgu_w_s=sgu_w_s, sgu_b_s=sgu_b_s,
                   dn_conv_w=dn_conv_w, dn_a_log=dn_a_log, dn_dt_bias=dn_dt_bias, dn_o_norm_g=dn_o_norm_g, w_out=w_out,
                   ple_norm_g=ple_norm_g, ple_gate_w=ple_gate_w, ple_proj_w=ple_proj_w, final_norm_g=final_norm_g)
    mom1 = dict(norm_g=m_norm_g, w_in=m_w_in, sgu_ln_g=m_sgu_ln_g, sgu_ln_b=m_sgu_ln_b, sgu_w_s=m_sgu_w_s,
                sgu_b_s=m_sgu_b_s, dn_conv_w=m_dn_conv_w, dn_a_log=m_dn_a_log, dn_dt_bias=m_dn_dt_bias,
                dn_o_norm_g=m_dn_o_norm_g, w_out=m_w_out, ple_norm_g=m_ple_norm_g, ple_gate_w=m_ple_gate_w,
                ple_proj_w=m_ple_proj_w, final_norm_g=m_final_norm_g)
    mom2 = dict(norm_g=v_norm_g, w_in=v_w_in, sgu_ln_g=v_sgu_ln_g, sgu_ln_b=v_sgu_ln_b, sgu_w_s=v_sgu_w_s,
                sgu_b_s=v_sgu_b_s, dn_conv_w=v_dn_conv_w, dn_a_log=v_dn_a_log, dn_dt_bias=v_dn_dt_bias,
                dn_o_norm_g=v_dn_o_norm_g, w_out=v_w_out, ple_norm_g=v_ple_norm_g, ple_gate_w=v_ple_gate_w,
                ple_proj_w=v_ple_proj_w, final_norm_g=v_final_norm_g)
    nb, s, _ = x.shape
    t = nb * s

    transposed = lambda a: jnp.transpose(a, (2, 0, 1)).reshape(IN_SHARD, D_MODEL)
    w_in_t, m_in_t, v_in_t = transposed(w_in), transposed(m_w_in), transposed(v_w_in)
    w_in_blocks, conv_blocks = _all_gather([w_in_t.astype(BF16), dn_conv_w[0]])
    w_in_full_t = w_in_blocks.reshape(IN_COLS, D_MODEL)
    wgt = jnp.pad(w_in_full_t[sum(IN_GROUPS):], ((0, GATE_PAD - 2 * DN_HEADS), (0, 0)))
    conv_full = jnp.moveaxis(conv_blocks, 0, 1).reshape(CONV_K, 3 * DN_WIDTH)
    later_shards = [w_out[0].astype(BF16), ple_gate_w[0].astype(BF16), ple_proj_w[0].astype(BF16)]

    pad_row = lambda a: jnp.pad(a.reshape(1, -1), ((0, 0), (DN_HEADS, GATE_PAD - DN_HEADS - a.size)))
    alog, dtb = pad_row(dn_a_log), pad_row(dn_dt_bias)
    og = dn_o_norm_g.reshape(1, DN_HEAD_DIM)
    ws = sgu_w_s.reshape(SGU_GROUPS, SGU_CHUNK, SGU_CHUNK)
    b_t = sgu_b_s.reshape(SGU_GROUPS, SGU_CHUNK).T
    fin_g = final_norm_g.reshape(1, D_MODEL)

    x2 = x.reshape(t, D_MODEL)
    sgu_weights = (sgu_ln_g, sgu_ln_b, ws, b_t)
    a_uvz, b_qkv, b_z, b_l, a_out, conv_out, w_out_blocks, w_gate_blocks, w_proj_blocks = _inproj_fwd(
        x2, s, norm_g, w_in_full_t, wgt, sgu_weights, conv_full, later_shards)
    w_out_full = w_out_blocks.reshape(D_MODEL, D_MODEL)
    w_gate_full = w_gate_blocks.reshape(D_MODEL, D_MODEL)
    w_proj_full = jnp.moveaxis(w_proj_blocks, 0, 1).reshape(PLE_DIM, D_MODEL)
    qkv3 = b_qkv.reshape(nb, s, 3 * DN_WIDTH)
    conv_out = conv_out.reshape(nb, s, 3 * DN_WIDTH)
    z3 = b_z.reshape(nb, s, DN_WIDTH)
    l3 = b_l.reshape(nb, s, GATE_PAD)
    b_out, states, inverses = _dn_fwd(conv_out, z3, l3, alog, dtb, og)

    d_a, d_b, dh1, g_w_out, g_gate, g_proj, g_ple_g, g_fin_g, loss_tile = _head(
        a_out, b_out.reshape(t, DN_WIDTH), x2, p.reshape(t, PLE_DIM), loss_target.reshape(t, D_MODEL),
        w_out_full, w_gate_full, w_proj_full, ple_norm_g, fin_g)
    d_qkv, d_z, d_l, g_conv, g_alog, g_dtb, g_og, *head_received = _dn_bwd(
        qkv3, conv_out, z3, l3, conv_full, alog, dtb, og, states, inverses, d_b.reshape(nb, s, DN_WIDTH),
        [g_w_out, g_gate, g_proj])
    grad_x, g_w_in, g_norm, g_ln_g, g_ln_b, g_ws, g_bt = _inproj_bwd(
        x2, dh1, a_uvz, d_a, d_qkv.reshape(t, 3 * DN_WIDTH), d_z.reshape(t, DN_WIDTH), d_l.reshape(t, GATE_PAD),
        norm_g, sgu_weights, w_in_full_t, wgt)

    small = _pack_rows([g_conv, g_norm, g_ln_g, g_ln_b, g_ws, g_bt.T, g_alog[:, DN_HEADS:2 * DN_HEADS], g_dtb[:, DN_HEADS:2 * DN_HEADS], g_og,
                        g_ple_g, g_fin_g, (0.5 / D_MODEL) * loss_tile[0:1, 0:1]], SMALL_ROWS)
    w_in_received, small_received = _reduce_exchange(g_w_in, small)

    results = {}
    outs = _reduce_adamw(w_in_received, w_in_t, m_in_t, v_in_t, "adamw_w_in", 4 * LANES)
    results["w_in"] = [jnp.transpose(a.reshape(IN_SHARD, 1, D_MODEL), (1, 2, 0)) for a in outs]
    for name, recv in zip(("w_out", "ple_gate_w", "ple_proj_w"), head_received):
        results[name] = _reduce_adamw(recv, weights[name], mom1[name], mom2[name], "adamw_" + name)
    names = [name for name, _ in REPLICATED]
    two_d = lambda a: a.reshape(1, -1) if a.ndim == 1 else a
    g_conv_sum, loss_sum, *flat_outs = _adamw_replicated(
        small_received, *[[two_d(src[k]) for k in names] for src in (weights, mom1, mom2)])
    for i, k in enumerate(names):
        results[k] = [a.reshape(weights[k].shape) for a in flat_outs[4 * i:4 * i + 4]]
    loss = loss_sum[0, 0]
    me = 4 * lax.axis_index("x") + 2 * lax.axis_index("y") + lax.axis_index("c")
    conv_mine = lax.dynamic_slice(g_conv_sum, (0, me * 192), (CONV_K, 192))
    results["dn_conv_w"] = _reduce_adamw(conv_mine[None], dn_conv_w, m_dn_conv_w, v_dn_conv_w, "adamw_dn_conv_w")

    return (loss, grad_x.reshape(nb, s, D_MODEL), *[results[k][0] for k in WEIGHT_ORDER],
            *[results[k][1] for k in WEIGHT_ORDER], *[results[k][2] for k in WEIGHT_ORDER],
            *[results[k][3] for k in WEIGHT_ORDER])
```

```python
import functools

import jax
import jax.numpy as jnp
from jax import lax
from jax.experimental import pallas as pl
from jax.experimental.pallas import tpu as pltpu

F32 = jnp.float32
BF16 = jnp.bfloat16

N_DEV = 8
D_MODEL = 1024
SGU_WIDTH = 512
SGU_GROUPS = 4
SGU_CHUNK = 128
DN_WIDTH = 512
DN_HEADS = 4
DN_HEAD_DIM = 128
DN_CHUNK = 128
CONV_K = 4
CONV_HALO = 8
PLE_DIM = 256
EPS = 1e-6
IN_COLS = 3592
IN_SHARD = IN_COLS // N_DEV
GATE_PAD = 128
IN_GROUPS = (3 * SGU_WIDTH, 3 * DN_WIDTH, DN_WIDTH)

ADAM_LR = 0.001
ADAM_B1 = 0.9
ADAM_B2 = 0.999
ADAM_EPS = 1e-08
ADAM_WD = 0.01
ADAM_STEP = 10

LANES = 128
VMEM_LIMIT = 56 * 1024 * 1024
MESH = pl.DeviceIdType.MESH

REPLICATED = (("norm_g", (1, D_MODEL)), ("sgu_ln_g", (1, SGU_WIDTH)), ("sgu_ln_b", (1, SGU_WIDTH)),
              ("sgu_w_s", (1, SGU_GROUPS, SGU_CHUNK, SGU_CHUNK)), ("sgu_b_s", (1, SGU_GROUPS, SGU_CHUNK)),
              ("dn_a_log", (1, DN_HEADS)), ("dn_dt_bias", (1, DN_HEADS)), ("dn_o_norm_g", (1, DN_HEAD_DIM)),
              ("ple_norm_g", (1, D_MODEL)), ("final_norm_g", (D_MODEL,)))
WEIGHT_ORDER = ("norm_g", "w_in", "sgu_ln_g", "sgu_ln_b", "sgu_w_s", "sgu_b_s", "dn_conv_w", "dn_a_log",
                "dn_dt_bias", "dn_o_norm_g", "w_out", "ple_norm_g", "ple_gate_w", "ple_proj_w", "final_norm_g")


def _size(shape):
    n = 1
    for s in shape:
        n *= s
    return n


SMALL_LAYOUT = (("conv", (CONV_K, 3 * DN_WIDTH)),) + REPLICATED + (("loss", (1,)),)
SMALL_PIECE_ROWS = tuple(-(-_size(s) // LANES) for _, s in SMALL_LAYOUT)
SMALL_ROWS = -(-sum(SMALL_PIECE_ROWS) // 8) * 8


def _bdot(a, b):
    return jnp.dot(a.astype(BF16), b.astype(BF16), preferred_element_type=F32)


def _sigmoid(x):
    return 0.5 * jnp.tanh(0.5 * x) + 0.5


@jax.custom_vjp
def _silu(x):
    return x * _sigmoid(x)


def _silu_fwd(x):
    s = _sigmoid(x)
    return x * s, (x, s)


def _silu_bwd(res, ct):
    x, s = res
    return (ct * (s * (1.0 + x * (1.0 - s))),)


_silu.defvjp(_silu_fwd, _silu_bwd)


def _normal_cdf(x):
    return 0.5 + 0.5 * lax.erf(x * (0.5 ** 0.5))


@jax.custom_vjp
def _gelu(x):
    return x * _normal_cdf(x)


def _gelu_fwd(x):
    cdf = _normal_cdf(x)
    return x * cdf, (x, cdf)


def _gelu_bwd(res, ct):
    x, cdf = res
    pdf = jnp.exp(-0.5 * x * x) * ((2.0 * jnp.pi) ** -0.5)
    return (ct * (cdf + x * pdf),)


_gelu.defvjp(_gelu_fwd, _gelu_bwd)


def _softplus(x):
    return jnp.maximum(x, 0.0) + jnp.log1p(jnp.exp(-jnp.abs(x)))


@jax.custom_vjp
def _l2n(x):
    return x * lax.rsqrt(jnp.sum(x * x, axis=-1, keepdims=True) + EPS)


def _l2n_fwd(x):
    r = lax.rsqrt(jnp.sum(x * x, axis=-1, keepdims=True) + EPS)
    n = x * r
    return n, (n, r)


def _l2n_bwd(res, ct):
    n, r = res
    return (r * (ct - n * jnp.sum(ct * n, axis=-1, keepdims=True)),)


_l2n.defvjp(_l2n_fwd, _l2n_bwd)


def _rms(x):
    r = lax.rsqrt(jnp.mean(x * x, axis=-1, keepdims=True) + EPS)
    return x * r, r


def _rms_bwd(dn, n, r):
    return r * (dn - n * jnp.mean(dn * n, axis=-1, keepdims=True))


@jax.custom_vjp
def _rms_normed(x):
    return _rms(x)[0]


def _rms_normed_fwd(x):
    n, r = _rms(x)
    return n, (n, r)


def _rms_normed_bwd(res, ct):
    return (_rms_bwd(ct, *res),)


_rms_normed.defvjp(_rms_normed_fwd, _rms_normed_bwd)


def _onehot_row(idx, width):
    return (lax.broadcasted_iota(jnp.int32, (1, width), 1) == idx).astype(F32)


def _rowsum(x):
    return jnp.sum(x, axis=0, keepdims=True)


def _iota2(n):
    return lax.broadcasted_iota(jnp.int32, (n, n), 0), lax.broadcasted_iota(jnp.int32, (n, n), 1)


def _bmm(a, b):
    return lax.dot_general(a.astype(BF16), b.astype(BF16), (((2,), (1,)), ((0,), (0,))), preferred_element_type=F32)


def _bmm_nt(a, b):
    return lax.dot_general(a.astype(BF16), b.astype(BF16), (((2,), (2,)), ((0,), (0,))), preferred_element_type=F32)


def _bmm_tn(a, b):
    return lax.dot_general(a.astype(BF16), b.astype(BF16), (((1,), (1,)), ((0,), (0,))), preferred_element_type=F32)


def _tri_inv_impl(a):
    n = a.shape[-1]
    r, c = _iota2(n)
    x = r ^ c
    eye = (r == c).astype(F32)
    ad = jnp.where(x < 16, a, 0.0)
    p2 = _bmm(ad, ad)
    e = p2 - ad - _bmm(ad, p2)
    p4 = _bmm(p2, p2)
    e = e + p4 + _bmm(e, p4)
    p8 = _bmm(p4, p4)
    e = e + p8 + _bmm(e, p8)
    size = 16
    while size < n:
        m = jnp.where(jnp.logical_and(x < 2 * size, x >= size), a, 0.0)
        f = m + _bmm(m, e)
        e = e - f - _bmm(e, f)
        size *= 2
    return e + eye


@jax.custom_vjp
def _tri_inv(a, known):
    return _tri_inv_impl(a) if known is None else known


def _tri_inv_fwd(a, known):
    t = _tri_inv(a, known)
    return t, (t, known)


def _tri_inv_bwd(res, dt):
    t, known = res
    return -_bmm_tn(t, _bmm_nt(dt, t)), None if known is None else jnp.zeros_like(known)


_tri_inv.defvjp(_tri_inv_fwd, _tri_inv_bwd)


@jax.custom_vjp
def _standardized(x):
    xc = x - jnp.mean(x, axis=-1, keepdims=True)
    return xc * lax.rsqrt(jnp.mean(xc * xc, axis=-1, keepdims=True) + EPS)


def _standardized_fwd(x):
    xc = x - jnp.mean(x, axis=-1, keepdims=True)
    rstd = lax.rsqrt(jnp.mean(xc * xc, axis=-1, keepdims=True) + EPS)
    y = xc * rstd
    return y, (y, rstd)


def _standardized_bwd(res, ct):
    y, rstd = res
    return (rstd * (ct - jnp.mean(ct, axis=-1, keepdims=True) - y * jnp.mean(ct * y, axis=-1, keepdims=True)),)


_standardized.defvjp(_standardized_fwd, _standardized_bwd)


def _sgu_core(u, v, z, lg, lb, ws, bcol):
    n = ws.shape[0]
    r, c = _iota2(n)
    wm = jnp.where(r >= c, ws, 0.0)
    gu = _gelu(u)
    gv = _gelu(v)
    ln = _standardized(gv) * lg + lb
    s = _bdot(wm, ln) + bcol
    return gu * s * _silu(z)


def _lanes_of(x):
    return jnp.concatenate([x[i] for i in range(x.shape[0])], axis=1)


def _batch_of(x, width):
    return jnp.concatenate([x[None, :, i * width:(i + 1) * width] for i in range(x.shape[1] // width)], axis=0)


def _mask_dot(mask, x):
    hi = x.astype(BF16)
    lo = (x - hi.astype(F32)).astype(BF16)
    m = mask.astype(BF16)
    return jnp.dot(m, hi, preferred_element_type=F32) + jnp.dot(m, lo, preferred_element_type=F32)


def _split_dot(x, mask, dims):
    hi = x.astype(BF16)
    lo = (x - hi.astype(F32)).astype(BF16)
    m = mask.astype(BF16)
    return (lax.dot_general(hi, m, dims, preferred_element_type=F32)
            + lax.dot_general(lo, m, dims, preferred_element_type=F32))


def _lane_select(lanes, blocks, first_lane):
    src = lax.broadcasted_iota(jnp.int32, (lanes, blocks * LANES), 0)
    dst = lax.broadcasted_iota(jnp.int32, (lanes, blocks * LANES), 1) // LANES
    return src == dst + first_lane


def _pick_lanes(x, blocks, first_lane):
    return _pick_lanes_vjp(blocks, first_lane, x)


@functools.partial(jax.custom_vjp, nondiff_argnums=(0, 1))
def _pick_lanes_vjp(blocks, first_lane, x):
    return _split_dot(x, _lane_select(x.shape[-1], blocks, first_lane), (((1,), (0,)), ((), ())))


def _pick_lanes_fwd(blocks, first_lane, x):
    return _pick_lanes_vjp(blocks, first_lane, x), x.shape[-1]


def _pick_lanes_bwd(blocks, first_lane, lanes, ct):
    return (sum(jnp.sum(ct[:, h * LANES:(h + 1) * LANES], axis=-1, keepdims=True) * _onehot_row(first_lane + h, lanes)
                for h in range(blocks)),)


_pick_lanes_vjp.defvjp(_pick_lanes_fwd, _pick_lanes_bwd)


def _tri_mask(n, upper):
    r, c = _iota2(n)
    return (r <= c) if upper else (r >= c)


@jax.custom_vjp
def _cumsum_rows(x):
    return _mask_dot(_tri_mask(x.shape[0], False), x)


def _cumsum_rows_fwd(x):
    return _cumsum_rows(x), None


def _cumsum_rows_bwd(_, ct):
    return (_mask_dot(_tri_mask(ct.shape[0], True), ct),)


_cumsum_rows.defvjp(_cumsum_rows_fwd, _cumsum_rows_bwd)


@jax.custom_vjp
def _colsum_all_rows(x):
    return _mask_dot(jnp.ones((x.shape[0], x.shape[0]), jnp.bool_), x)


def _colsum_all_rows_fwd(x):
    return _colsum_all_rows(x), None


def _colsum_all_rows_bwd(_, ct):
    return (_mask_dot(jnp.ones((ct.shape[0], ct.shape[0]), jnp.bool_), ct),)


_colsum_all_rows.defvjp(_colsum_all_rows_fwd, _colsum_all_rows_bwd)


def _dn_core(cq, ck, cv, z, logits, state, alog, dtb, og, t_known=None):
    gn, cn, dh = cq.shape
    heads = gn // logits.shape[0]
    q = _l2n(_silu(cq)) * (dh ** -0.5)
    k = _l2n(_silu(ck))
    v = _silu(cv)
    beta_lanes = _sigmoid(logits)
    g_lanes = -jnp.exp(alog) * _softplus(logits + dtb)
    beta_all = jnp.concatenate([_pick_lanes(beta_lanes[b], heads, 0) for b in range(logits.shape[0])], axis=1)
    g_all = jnp.concatenate([_pick_lanes(g_lanes[b], heads, heads) for b in range(logits.shape[0])], axis=1)
    beta = _batch_of(beta_all, dh)
    g_wide = _batch_of(g_all, dh)
    r, c = _iota2(cn)
    tril = r >= c
    rw = lax.broadcasted_iota(jnp.int32, (cn, dh), 0)
    cw = lax.broadcasted_iota(jnp.int32, (cn, dh), 1)
    upper_wide = (rw <= cw).astype(F32)
    gc_wide = _batch_of(_cumsum_rows(g_all), dh)
    gc_cols = _batch_of(_colsum_all_rows(_lanes_of(g_wide * upper_wide)), dh)[:, :, :cn]
    decay = jnp.exp(jnp.where(tril, gc_wide[:, :, :cn] - gc_cols, -1e30))
    kb = k * beta
    kk = _bmm_nt(kb, k) * decay
    t = _tri_inv(jnp.where(r > c, kk, 0.0), t_known)
    eg = jnp.exp(gc_wide)
    sol = _bmm(t, jnp.concatenate([v * beta, kb * eg], axis=-1))
    u_val, w_dec = sol[:, :, :dh], sol[:, :, dh:]
    qk = _bmm_nt(q, k) * decay
    g_last = jnp.sum(g_wide, axis=1, keepdims=True)
    k_dec = k * jnp.exp(g_last - gc_wide)
    ws = _bmm(jnp.concatenate([w_dec, q * eg], axis=1), state)
    v_new = u_val - ws[:, :cn]
    o = ws[:, cn:] + _bmm(qk, v_new)
    new_state = state * jnp.exp(g_last) + _bmm_tn(k_dec, v_new)
    return _rms_normed(o) * og * _silu(z), new_state, t


N_CHIPS = 4
HBM_SPEC = pl.BlockSpec(memory_space=pl.ANY)


def _place():
    return lax.axis_index("x"), lax.axis_index("y"), lax.axis_index("c")


def _other_chip(k):
    x, y, _ = _place()
    px = 1 - x if k & 2 else x
    py = 1 - y if k & 1 else y
    return px, py, 2 * px + py


def _remote(src, dst, send_sem, recv_sem, device):
    return pltpu.make_async_remote_copy(src_ref=src, dst_ref=dst, send_sem=send_sem, recv_sem=recv_sem,
                                        device_id=device, device_id_type=MESH)


def _other_device(k):
    x, y, c = _place()
    px = 1 - x if k & 4 else x
    py = 1 - y if k & 2 else y
    pc = 1 - c if k & 1 else c
    return (px, py, pc), 4 * px + 2 * py + pc


def _direct_exchange(srcs, outs, send_sems, recv_sems, local_sems, gather):
    x, y, c = _place()
    me = 4 * x + 2 * y + c

    def copies(arriving):
        out_list = []
        for a, (src, out) in enumerate(zip(srcs, outs)):
            for k in range(1, N_DEV):
                peer, index = _other_device(k)
                mine = src if gather else src.at[index]
                out_list.append(_remote(mine, out.at[index if arriving else me], send_sems.at[a, k - 1],
                                        recv_sems.at[a, k - 1], peer))
        return out_list

    def local_copies():
        return [pltpu.make_async_copy(src if gather else src.at[me], out.at[me], local_sems.at[a])
                for a, (src, out) in enumerate(zip(srcs, outs))]

    def start():
        for cp in local_copies() + copies(False):
            cp.start()

    def wait():
        for cp in copies(True):
            cp.wait_recv()
        for cp in copies(False):
            cp.wait_send()
        for cp in local_copies():
            cp.wait()

    return start, wait


def _exchange_scratch(n):
    return [pltpu.SemaphoreType.DMA((n, N_DEV - 1)), pltpu.SemaphoreType.DMA((n, N_DEV - 1)), pltpu.SemaphoreType.DMA((n,))]


def _all_gather(shards):
    n = len(shards)

    def body(*refs):
        srcs, outs = refs[:n], refs[n:2 * n]
        send_sems, recv_sems, local_sems = refs[2 * n:]
        x, y, c = _place()
        me = 4 * x + 2 * y + c
        sibling = (x, y, 1 - c)
        local = [pltpu.make_async_copy(srcs[a], outs[a].at[me], local_sems.at[a]) for a in range(n)]
        for cp in local:
            cp.start()
        sends = []
        for a in range(n):
            sends.append(_remote(srcs[a], outs[a].at[me], send_sems.at[a, 0], recv_sems.at[a, 0], sibling))
        for k in range(1, N_CHIPS):
            px, py, _ = _other_chip(k)
            for a in range(n):
                sends.append(_remote(srcs[a], outs[a].at[me], send_sems.at[a, k], recv_sems.at[a, k], (px, py, c)))
        for cp in sends:
            cp.start()
        passed = []
        for k in range(1, N_CHIPS):
            px, py, _ = _other_chip(k)
            blk = 4 * px + 2 * py + c
            for a in range(n):
                _remote(srcs[a], outs[a].at[blk], send_sems.at[a, k], recv_sems.at[a, k], (px, py, c)).wait_recv()
            for a in range(n):
                cp = _remote(outs[a].at[blk], outs[a].at[blk], send_sems.at[a, 3 + k], recv_sems.at[a, 3 + k], sibling)
                cp.start()
                passed.append(cp)
        for a in range(n):
            _remote(srcs[a], outs[a].at[me + 1 - 2 * c], send_sems.at[a, 0], recv_sems.at[a, 0], sibling).wait_recv()
        for k in range(1, N_CHIPS):
            px, py, _ = _other_chip(k)
            blk = 4 * px + 2 * py + 1 - c
            for a in range(n):
                _remote(srcs[a], outs[a].at[blk], send_sems.at[a, 3 + k], recv_sems.at[a, 3 + k], sibling).wait_recv()
        for cp in sends + passed:
            cp.wait_send()
        for cp in local:
            cp.wait()

    return pl.pallas_call(
        body, name="all_gather_weights",
        out_shape=tuple(jax.ShapeDtypeStruct((N_DEV,) + a.shape, a.dtype) for a in shards),
        in_specs=[HBM_SPEC] * n, out_specs=(HBM_SPEC,) * n,
        scratch_shapes=[pltpu.SemaphoreType.DMA((n, N_DEV - 1)), pltpu.SemaphoreType.DMA((n, N_DEV - 1)),
                        pltpu.SemaphoreType.DMA((n,))],
    )(*shards)


def _reduce_exchange(by_device, small):
    _, rows, cols = by_device.shape

    def body(g_ref, small_ref, out_ref, small_out_ref, from_sibling, small_from_sibling, stage, sums, small_own, small_sum,
             pair_send, pair_recv, chip_send, chip_recv, local_sems):
        x, y, c = _place()
        mine = 2 * x + y
        sibling = (x, y, 1 - c)
        chips = [(x, y, mine)] + [_other_chip(k) for k in range(1, N_CHIPS)]
        to_sibling = [_remote(g_ref.at[2 * chips[k][2] + 1 - c], from_sibling.at[k], pair_send.at[k], pair_recv.at[k], sibling)
                      for k in range(N_CHIPS)]
        to_sibling.append(_remote(small_ref, small_from_sibling, pair_send.at[N_CHIPS], pair_recv.at[N_CHIPS], sibling))
        for cp in to_sibling:
            cp.start()
        small_mine = pltpu.make_async_copy(small_ref, small_own, local_sems.at[0])
        small_mine.start()
        to_chips = []
        for k in (1, 2, 3, 0):
            px, py, chip = chips[k]
            mine_k = pltpu.make_async_copy(g_ref.at[2 * chip + c], stage, local_sems.at[1])
            mine_k.start()
            to_sibling[k].wait_recv()
            mine_k.wait()
            sums[k] = (stage[...] + from_sibling[k]).astype(sums.dtype)
            if k:
                cp = _remote(sums.at[k], out_ref.at[mine], chip_send.at[0, k - 1], chip_recv.at[0, k - 1], (px, py, c))
                cp.start()
                to_chips.append(cp)
        own_block = pltpu.make_async_copy(sums.at[0], out_ref.at[mine], local_sems.at[2])
        own_block.start()
        to_sibling[N_CHIPS].wait_recv()
        small_mine.wait()
        small_sum[...] = small_own[...] + small_from_sibling[...]
        for k in range(1, N_CHIPS):
            px, py, _ = chips[k]
            cp = _remote(small_sum, small_out_ref.at[mine], chip_send.at[1, k - 1], chip_recv.at[1, k - 1], (px, py, c))
            cp.start()
            to_chips.append(cp)
        own_small = pltpu.make_async_copy(small_sum, small_out_ref.at[mine], local_sems.at[3])
        own_small.start()
        for k in range(1, N_CHIPS):
            px, py, chip = chips[k]
            _remote(sums.at[k], out_ref.at[chip], chip_send.at[0, k - 1], chip_recv.at[0, k - 1], (px, py, c)).wait_recv()
            _remote(small_sum, small_out_ref.at[chip], chip_send.at[1, k - 1], chip_recv.at[1, k - 1], (px, py, c)).wait_recv()
        for cp in to_sibling + to_chips:
            cp.wait_send()
        own_block.wait()
        own_small.wait()

    return pl.pallas_call(
        body, name="grad_reduce_exchange",
        out_shape=(jax.ShapeDtypeStruct((N_CHIPS, rows, cols), BF16), jax.ShapeDtypeStruct((N_CHIPS,) + small.shape, F32)),
        in_specs=[HBM_SPEC, HBM_SPEC], out_specs=(HBM_SPEC, HBM_SPEC),
        scratch_shapes=[pltpu.VMEM((N_CHIPS, rows, cols), F32), pltpu.VMEM(small.shape, F32), pltpu.VMEM((rows, cols), F32),
                        pltpu.VMEM((N_CHIPS, rows, cols), BF16), pltpu.VMEM(small.shape, F32), pltpu.VMEM(small.shape, F32),
                        pltpu.SemaphoreType.DMA((N_CHIPS + 1,)), pltpu.SemaphoreType.DMA((N_CHIPS + 1,)),
                        pltpu.SemaphoreType.DMA((2, N_CHIPS - 1)), pltpu.SemaphoreType.DMA((2, N_CHIPS - 1)),
                        pltpu.SemaphoreType.DMA((4,))],
        compiler_params=pltpu.CompilerParams(vmem_limit_bytes=VMEM_LIMIT),
    )(by_device, small)


def _params(n_axes):
    return pltpu.CompilerParams(dimension_semantics=("arbitrary",) * n_axes, vmem_limit_bytes=VMEM_LIMIT)


def _whole(shape):
    return pl.BlockSpec(shape, lambda *_: (0,) * len(shape))


VMEM_SPEC = pl.BlockSpec(memory_space=pltpu.VMEM)


def _inproj_fwd(x2, seq_len, norm_g, wt, wgt, sgu_weights, conv_w, later_shards):
    t = x2.shape[0]
    tm = min(512, seq_len)
    tiles_per_seq = seq_len // tm
    steps = t // tm
    ns = len(later_shards)

    widths = IN_GROUPS + (wgt.shape[0],)
    starts = (0, IN_GROUPS[0], IN_GROUPS[0] + IN_GROUPS[1], 0)

    def body(x_ref, g_ref, wt_ref, wg_ref, lg_ref, lb_ref, ws_ref, bt_ref, cw_ref, *rest):
        shard_refs, rest = rest[:ns], rest[ns:]
        a_ref, q_ref, z_ref, l_ref, sgu_ref, c_ref = rest[:6]
        gathered_refs, (xpad_ref, send_sems, recv_sems, local_sems) = rest[6:6 + ns], rest[6 + ns:]
        start_gather, wait_gather = _direct_exchange(shard_refs, gathered_refs, send_sems, recv_sems, local_sems, True)
        pl.when(pl.program_id(0) == 0)(start_gather)

        @pl.when(pl.program_id(0) % tiles_per_seq == 0)
        def _():
            xpad_ref[0:CONV_HALO, :] = jnp.zeros((CONV_HALO, xpad_ref.shape[1]), F32)

        n, _ = _rms(x_ref[...])
        xn = (n * g_ref[...]).astype(BF16)

        def project(w_ref, row0, width, o_ref):
            for c0 in range(0, width, 512):
                c1 = min(c0 + 512, width)
                o_ref[:, c0:c1] = lax.dot_general(xn, w_ref[row0 + c0:row0 + c1, :], (((1,), (1,)), ((), ())),
                                                  preferred_element_type=F32)

        def sgu_rows(row0):
            for grp in range(SGU_GROUPS):
                args = _sgu_pieces(a_ref, lg_ref, lb_ref, ws_ref, bt_ref, row0, grp)
                sgu_ref[pl.ds(row0, SGU_CHUNK), pl.ds(grp * 128, 128)] = _sgu_core(*args).astype(sgu_ref.dtype)

        def conv():
            xpad_ref[CONV_HALO:, :] = q_ref[...]
            acc = None
            for j in range(CONV_K):
                term = cw_ref[j:j + 1, :] * xpad_ref[pl.ds(CONV_HALO - CONV_K + 1 + j, tm), :]
                acc = term if acc is None else acc + term
            c_ref[...] = acc
            xpad_ref[0:CONV_HALO, :] = xpad_ref[tm:tm + CONV_HALO, :]

        groups = tuple(zip((wt_ref, wt_ref, wt_ref, wg_ref), starts, widths, (a_ref, q_ref, z_ref, l_ref)))
        row_chunks = list(range(0, tm, SGU_CHUNK))
        project(*groups[0])
        for row0 in row_chunks[:len(row_chunks) // 2]:
            sgu_rows(row0)
        project(*groups[1])
        for row0 in row_chunks[len(row_chunks) // 2:]:
            sgu_rows(row0)
        project(*groups[3])
        conv()
        project(*groups[2])
        pl.when(pl.program_id(0) == steps - 1)(wait_gather)

    tile = lambda w: pl.BlockSpec((tm, w), lambda i: (i, 0))
    sgu_shapes = ((1, SGU_WIDTH), (1, SGU_WIDTH), (SGU_GROUPS, SGU_CHUNK, SGU_CHUNK), (SGU_CHUNK, SGU_GROUPS))
    return pl.pallas_call(
        body, name="inproj_sgu_conv_fwd", grid=(steps,),
        out_shape=tuple(jax.ShapeDtypeStruct((t, w), F32) for w in widths)
        + (jax.ShapeDtypeStruct((t, SGU_WIDTH), BF16), jax.ShapeDtypeStruct((t, widths[1]), F32))
        + tuple(jax.ShapeDtypeStruct((N_DEV,) + a.shape, a.dtype) for a in later_shards),
        in_specs=[tile(D_MODEL), _whole((1, D_MODEL)), VMEM_SPEC, VMEM_SPEC]
        + [_whole(s) for s in sgu_shapes] + [_whole((CONV_K, widths[1]))] + [HBM_SPEC] * ns,
        out_specs=tuple(tile(w) for w in widths) + (tile(SGU_WIDTH), tile(widths[1])) + (HBM_SPEC,) * ns,
        scratch_shapes=[pltpu.VMEM((CONV_HALO + tm, widths[1]), F32)] + _exchange_scratch(ns),
        compiler_params=_params(1),
    )(x2, norm_g, wt, wgt, *sgu_weights, conv_w, *later_shards)


def _sgu_pieces(uvz_ref, lg_ref, lb_ref, ws_ref, bt_ref, row0, grp):
    rows = pl.ds(row0, SGU_CHUNK)
    lanes = pl.ds(grp * 128, 128)
    u = uvz_ref[rows, pl.ds(grp * 128, 128)]
    v = uvz_ref[rows, pl.ds(SGU_WIDTH + grp * 128, 128)]
    z = uvz_ref[rows, pl.ds(2 * SGU_WIDTH + grp * 128, 128)]
    bcol = jnp.sum(bt_ref[...] * _onehot_row(grp, SGU_GROUPS), axis=-1, keepdims=True)
    return u, v, z, lg_ref[:, lanes], lb_ref[:, lanes], ws_ref[grp], bcol


def _sgu_bwd_tile(uvz_ref, do_ref, sgu_refs, duvz_ref, grad_refs, pieces):
    lg_ref, lb_ref, ws_ref, bt_ref = sgu_refs
    dlg_ref, dlb_ref, dws_ref, dbt_ref = grad_refs
    for piece in pieces:
        row0, grp = piece // SGU_GROUPS * SGU_CHUNK, piece % SGU_GROUPS
        rows = pl.ds(row0, SGU_CHUNK)
        lanes = pl.ds(grp * 128, 128)
        args = _sgu_pieces(uvz_ref, lg_ref, lb_ref, ws_ref, bt_ref, row0, grp)
        _, pull = jax.vjp(_sgu_core, *args)
        du, dv, dz, dlg, dlb, dws, dbcol = pull(do_ref[rows, lanes])
        duvz_ref[rows, pl.ds(grp * 128, 128)] = du.astype(duvz_ref.dtype)
        duvz_ref[rows, pl.ds(SGU_WIDTH + grp * 128, 128)] = dv.astype(duvz_ref.dtype)
        duvz_ref[rows, pl.ds(2 * SGU_WIDTH + grp * 128, 128)] = dz.astype(duvz_ref.dtype)
        dlg_ref[:, lanes] += dlg
        dlb_ref[:, lanes] += dlb
        dws_ref[grp] += dws
        dbt_ref[...] += dbcol * _onehot_row(grp, SGU_GROUPS)


def _dn_pairs(nb):
    return [(b, h) for b in range(nb) for h in range(DN_HEADS)]


def _dn_batch_args(c_ref, z_ref):
    pairs = _dn_pairs(c_ref.shape[0])
    pick = lambda ref, b, col: ref[b, :, pl.ds(col, DN_HEAD_DIM)]
    cq = jnp.stack([pick(c_ref, b, h * DN_HEAD_DIM) for b, h in pairs])
    ck = jnp.stack([pick(c_ref, b, DN_WIDTH + h * DN_HEAD_DIM) for b, h in pairs])
    cv = jnp.stack([pick(c_ref, b, 2 * DN_WIDTH + h * DN_HEAD_DIM) for b, h in pairs])
    z = jnp.stack([pick(z_ref, b, h * DN_HEAD_DIM) for b, h in pairs])
    return cq, ck, cv, z


def _dn_weight_specs():
    return [_whole((CONV_K, 3 * DN_WIDTH)), _whole((1, GATE_PAD)), _whole((1, GATE_PAD)), _whole((1, DN_HEAD_DIM))]


def _dn_fwd(conv_out, zg, logits, alog, dtb, og):
    nb, s, _ = conv_out.shape
    nc = s // DN_CHUNK
    pairs = _dn_pairs(nb)
    gn = len(pairs)
    chunk = lambda w: pl.BlockSpec((nb, DN_CHUNK, w), lambda n: (0, n, 0))

    def body(c_ref, z_ref, l_ref, alog_ref, dtb_ref, og_ref, out_ref, st_ref, inv_ref, state_ref):
        n = pl.program_id(0)

        @pl.when(n == 0)
        def _():
            state_ref[...] = jnp.zeros_like(state_ref)

        cq, ck, cv, z = _dn_batch_args(c_ref, z_ref)
        state = state_ref[...]
        st_ref[...] = state
        out, new_state, t = _dn_core(cq, ck, cv, z, l_ref[...], state, alog_ref[...], dtb_ref[...], og_ref[...])
        state_ref[...] = new_state
        inv_ref[...] = t.astype(inv_ref.dtype)
        for i, (b, h) in enumerate(pairs):
            out_ref[b, :, pl.ds(h * DN_HEAD_DIM, DN_HEAD_DIM)] = out[i].astype(out_ref.dtype)

    per_chunk = pl.BlockSpec((None, gn, DN_HEAD_DIM, DN_HEAD_DIM), lambda n: (n, 0, 0, 0))
    return pl.pallas_call(
        body, name="deltanet_fwd", grid=(nc,),
        out_shape=(jax.ShapeDtypeStruct((nb, s, DN_WIDTH), BF16),
                   jax.ShapeDtypeStruct((nc, gn, DN_HEAD_DIM, DN_HEAD_DIM), F32),
                   jax.ShapeDtypeStruct((nc, gn, DN_CHUNK, DN_CHUNK), BF16)),
        in_specs=[chunk(3 * DN_WIDTH), chunk(DN_WIDTH), chunk(GATE_PAD)] + _dn_weight_specs()[1:],
        out_specs=(chunk(DN_WIDTH), per_chunk, pl.BlockSpec((None, gn, DN_CHUNK, DN_CHUNK), lambda n: (n, 0, 0, 0))),
        scratch_shapes=[pltpu.VMEM((gn, DN_HEAD_DIM, DN_HEAD_DIM), F32)],
        compiler_params=_params(1),
    )(conv_out, zg, logits, alog, dtb, og)


def _dn_bwd(qkv, conv_out, zg, logits, conv_w, alog, dtb, og, states, inverses, d_out, head_grads):
    nb, s, _ = qkv.shape
    nc = s // DN_CHUNK
    rev = lambda n: nc - 1 - n
    pairs = _dn_pairs(nb)
    gn = len(pairs)
    ng = len(head_grads)

    def body(cur_ref, c_ref, z_ref, l_ref, w_ref, alog_ref, dtb_ref, og_ref, st_ref, inv_ref, do_ref, *rest):
        grad_refs, rest = rest[:ng], rest[ng:]
        dqkv_ref, dz_ref, dl_ref, dw_ref, dalog_ref, ddtb_ref, dog_ref = rest[:7]
        recv_refs, (dstate_ref, dcpad_ref, dw_part_ref, send_sems, recv_sems, local_sems) = rest[7:7 + ng], rest[7 + ng:]
        n = pl.program_id(0)
        start_exchange, wait_exchange = _direct_exchange(grad_refs, recv_refs, send_sems, recv_sems, local_sems, False)
        pl.when(n == 0)(start_exchange)

        @pl.when(n == 0)
        def _():
            dw_part_ref[...] = jnp.zeros_like(dw_part_ref)
            dalog_ref[...] = jnp.zeros_like(dalog_ref)
            ddtb_ref[...] = jnp.zeros_like(ddtb_ref)
            dog_ref[...] = jnp.zeros_like(dog_ref)
            dstate_ref[...] = jnp.zeros_like(dstate_ref)
            dcpad_ref[:, DN_CHUNK:, :] = jnp.zeros((nb, CONV_HALO, 3 * DN_WIDTH), F32)

        cq, ck, cv, z = _dn_batch_args(c_ref, z_ref)
        d_out_g = jnp.stack([do_ref[b, :, pl.ds(h * DN_HEAD_DIM, DN_HEAD_DIM)] for b, h in pairs])
        t_known = inv_ref[...].astype(F32)
        core = lambda *args: _dn_core(*args, t_known=t_known)[:2]
        _, pull = jax.vjp(core, cq, ck, cv, z, l_ref[...], st_ref[...], alog_ref[...], dtb_ref[...], og_ref[...])
        dcq, dck, dcv, dz, dlog, dstate, dalog, ddtb, dog = pull((d_out_g, dstate_ref[...]))
        dstate_ref[...] = dstate
        dl_ref[...] = dlog.astype(dl_ref.dtype)
        dalog_ref[...] += dalog
        ddtb_ref[...] += ddtb
        dog_ref[...] += dog
        for i, (b, h) in enumerate(pairs):
            dcpad_ref[b, 0:DN_CHUNK, pl.ds(h * DN_HEAD_DIM, DN_HEAD_DIM)] = dcq[i]
            dcpad_ref[b, 0:DN_CHUNK, pl.ds(DN_WIDTH + h * DN_HEAD_DIM, DN_HEAD_DIM)] = dck[i]
            dcpad_ref[b, 0:DN_CHUNK, pl.ds(2 * DN_WIDTH + h * DN_HEAD_DIM, DN_HEAD_DIM)] = dcv[i]
            dz_ref[b, :, pl.ds(h * DN_HEAD_DIM, DN_HEAD_DIM)] = dz[i].astype(dz_ref.dtype)
        for b in range(nb):
            xb = cur_ref[b]
            dx = None
            for j in range(CONV_K):
                shifted = dcpad_ref[b, pl.ds(CONV_K - 1 - j, DN_CHUNK), :]
                term = w_ref[j:j + 1, :] * shifted
                dx = term if dx is None else dx + term
                dw_part_ref[j] += jnp.sum((shifted * xb).reshape(DN_CHUNK // 8, 8, 3 * DN_WIDTH), axis=0)
            dqkv_ref[b] = dx.astype(dqkv_ref.dtype)
            dcpad_ref[b, DN_CHUNK:, :] = dcpad_ref[b, 0:CONV_HALO, :]

        @pl.when(n == nc - 1)
        def _():
            dw_ref[...] = jnp.sum(dw_part_ref[...], axis=1)

        pl.when(n == nc - 1)(wait_exchange)

    chunk = lambda w: pl.BlockSpec((nb, DN_CHUNK, w), lambda n: (0, rev(n), 0))
    return pl.pallas_call(
        body, name="deltanet_bwd", grid=(nc,),
        out_shape=(jax.ShapeDtypeStruct((nb, s, 3 * DN_WIDTH), BF16), jax.ShapeDtypeStruct((nb, s, DN_WIDTH), BF16),
                   jax.ShapeDtypeStruct((nb, s, GATE_PAD), BF16), jax.ShapeDtypeStruct((CONV_K, 3 * DN_WIDTH), F32),
                   jax.ShapeDtypeStruct((1, GATE_PAD), F32), jax.ShapeDtypeStruct((1, GATE_PAD), F32),
                   jax.ShapeDtypeStruct((1, DN_HEAD_DIM), F32))
        + tuple(jax.ShapeDtypeStruct(a.shape, a.dtype) for a in head_grads),
        in_specs=[chunk(3 * DN_WIDTH), chunk(3 * DN_WIDTH), chunk(DN_WIDTH), chunk(GATE_PAD)] + _dn_weight_specs() + [
            pl.BlockSpec((None, gn, DN_HEAD_DIM, DN_HEAD_DIM), lambda n: (rev(n), 0, 0, 0)),
            pl.BlockSpec((None, gn, DN_CHUNK, DN_CHUNK), lambda n: (rev(n), 0, 0, 0)),
            chunk(DN_WIDTH)] + [HBM_SPEC] * ng,
        out_specs=(chunk(3 * DN_WIDTH), chunk(DN_WIDTH), chunk(GATE_PAD), _whole((CONV_K, 3 * DN_WIDTH)),
                   _whole((1, GATE_PAD)), _whole((1, GATE_PAD)), _whole((1, DN_HEAD_DIM))) + (HBM_SPEC,) * ng,
        scratch_shapes=[pltpu.VMEM((gn, DN_HEAD_DIM, DN_HEAD_DIM), F32),
                        pltpu.VMEM((nb, DN_CHUNK + CONV_HALO, 3 * DN_WIDTH), F32),
                        pltpu.VMEM((CONV_K, 8, 3 * DN_WIDTH), F32)] + _exchange_scratch(ng),
        compiler_params=_params(1),
    )(qkv, conv_out, zg, logits, conv_w, alog, dtb, og, states, inverses, d_out, *head_grads)


def _head(a_out, b_out, x2, p2, target, w_out, w_gate, w_proj, ple_g, fin_g):
    t = x2.shape[0]
    tm = min(512, t)
    steps = t // tm

    def body(a_ref, b_ref, x_ref, p_ref, y_ref, wo_ref, wg_ref, wp_ref, pg_ref, fg_ref,
             da_ref, db_ref, dh_ref, dwo_hbm, dwg_hbm, dwp_hbm, dpg_ref, dfg_ref, loss_ref,
             dwo_acc, dwg_acc, dwp_acc, rows_stage, cols_stage):
        i = pl.program_id(0)

        @pl.when(i == 0)
        def _():
            dwo_acc[...] = jnp.zeros_like(dwo_acc)
            dwg_acc[...] = jnp.zeros_like(dwg_acc)
            dwp_acc[...] = jnp.zeros_like(dwp_acc)
            dpg_ref[...] = jnp.zeros_like(dpg_ref)
            dfg_ref[...] = jnp.zeros_like(dfg_ref)
            loss_ref[...] = jnp.zeros_like(loss_ref)

        pg = pg_ref[...]
        fg = fg_ref[...]
        nt = (((1,), (1,)), ((), ()))
        tn = (((0,), (0,)), ((), ()))

        def to_first_norm(rows):
            h1 = (x_ref[rows, :] + jnp.dot(a_ref[rows, :], wo_ref[0:SGU_WIDTH, :], preferred_element_type=F32)
                  + jnp.dot(b_ref[rows, :], wo_ref[SGU_WIDTH:, :], preferred_element_type=F32))
            n1, r1 = _rms(h1)
            pp = jnp.dot(p_ref[rows, :].astype(BF16), wp_ref[...], preferred_element_type=F32)
            return h1, n1, r1, (n1 * pg).astype(BF16), pp

        def to_gate_cotangents(rows, h1, rn, pp):
            gate = _sigmoid(jnp.dot(rn, wg_ref[...], preferred_element_type=F32))
            h2 = h1 + gate * pp
            n2, r2 = _rms(h2)
            err = n2 * fg - y_ref[rows, :]
            loss = _rowsum(jnp.sum(err * err, axis=-1, keepdims=True))
            dy = err * (1.0 / D_MODEL)
            dh2 = _rms_bwd(dy * fg, n2, r2)
            return loss, _rowsum(dy * n2), dh2, (dh2 * gate).astype(BF16), (dh2 * pp * gate * (1.0 - gate)).astype(BF16)

        def to_branch_cotangents(rows, dgl, dh2, n1, r1):
            drn = lax.dot_general(dgl, wg_ref[...], nt, preferred_element_type=F32)
            dh1 = dh2 + _rms_bwd(drn * pg, n1, r1)
            dh_ref[rows, :] = dh1
            dhb = dh1.astype(BF16)
            da_ref[rows, :] = lax.dot_general(dhb, wo_ref[0:SGU_WIDTH, :], nt, preferred_element_type=F32)
            db_ref[rows, :] = lax.dot_general(dhb, wo_ref[SGU_WIDTH:, :], nt, preferred_element_type=F32)
            return _rowsum(drn * n1), dhb

        parts = [pl.ds(k * (tm // 2), tm // 2) for k in range(2)]
        first = [to_first_norm(rows) for rows in parts]
        mid = [to_gate_cotangents(rows, h1, rn, pp) for rows, (h1, _, _, rn, pp) in zip(parts, first)]
        loss_ref[...] += jnp.broadcast_to(mid[0][0] + mid[1][0], loss_ref.shape)
        dfg_ref[...] += mid[0][1] + mid[1][1]
        last = [to_branch_cotangents(rows, m[4], m[2], f[1], f[2]) for rows, m, f in zip(parts, mid, first)]
        rn = jnp.concatenate([f[3] for f in first], axis=0)
        dpp = jnp.concatenate([m[3] for m in mid], axis=0)
        dgl = jnp.concatenate([m[4] for m in mid], axis=0)
        dwp_acc[...] += lax.dot_general(p_ref[...].astype(BF16), dpp, tn, preferred_element_type=F32)
        dwg_acc[...] += lax.dot_general(rn, dgl, tn, preferred_element_type=F32)
        dpg_ref[...] += last[0][0] + last[1][0]
        dhb = jnp.concatenate([l[1] for l in last], axis=0)
        dwo_acc[0:SGU_WIDTH, :] += lax.dot_general(a_ref[...], dhb, tn, preferred_element_type=F32)
        dwo_acc[SGU_WIDTH:, :] += lax.dot_general(b_ref[...], dhb, tn, preferred_element_type=F32)

        @pl.when(i == steps - 1)
        def _():
            for j in range(N_DEV):
                for acc, hbm in ((dwo_acc, dwo_hbm), (dwg_acc, dwg_hbm)):
                    rows_stage[...] = acc[j * LANES:(j + 1) * LANES, :].astype(BF16)
                    pltpu.sync_copy(rows_stage, hbm.at[j])
                cols_stage[...] = dwp_acc[:, j * LANES:(j + 1) * LANES].astype(BF16)
                pltpu.sync_copy(cols_stage, dwp_hbm.at[j])

    tile = lambda w: pl.BlockSpec((tm, w), lambda i: (i, 0))
    return pl.pallas_call(
        body, name="head_fwd_bwd", grid=(steps,),
        out_shape=(jax.ShapeDtypeStruct((t, SGU_WIDTH), F32), jax.ShapeDtypeStruct((t, DN_WIDTH), F32),
                   jax.ShapeDtypeStruct((t, D_MODEL), F32), jax.ShapeDtypeStruct((N_DEV, LANES, D_MODEL), BF16),
                   jax.ShapeDtypeStruct((N_DEV, LANES, D_MODEL), BF16), jax.ShapeDtypeStruct((N_DEV, PLE_DIM, LANES), BF16),
                   jax.ShapeDtypeStruct((1, D_MODEL), F32), jax.ShapeDtypeStruct((1, D_MODEL), F32),
                   jax.ShapeDtypeStruct((8, LANES), F32)),
        in_specs=[tile(SGU_WIDTH), tile(DN_WIDTH), tile(D_MODEL), tile(PLE_DIM), tile(D_MODEL),
                  VMEM_SPEC, VMEM_SPEC, VMEM_SPEC, _whole((1, D_MODEL)), _whole((1, D_MODEL))],
        out_specs=(tile(SGU_WIDTH), tile(DN_WIDTH), tile(D_MODEL), HBM_SPEC, HBM_SPEC, HBM_SPEC,
                   _whole((1, D_MODEL)), _whole((1, D_MODEL)), _whole((8, LANES))),
        scratch_shapes=[pltpu.VMEM((D_MODEL, D_MODEL), F32), pltpu.VMEM((D_MODEL, D_MODEL), F32),
                        pltpu.VMEM((PLE_DIM, D_MODEL), F32), pltpu.VMEM((LANES, D_MODEL), BF16),
                        pltpu.VMEM((PLE_DIM, LANES), BF16)],
        compiler_params=_params(1),
    )(a_out, b_out, x2, p2, target, w_out, w_gate, w_proj, ple_g, fin_g)


def _inproj_bwd(x2, dh1, a_uvz, d_sgu, d_q, d_z, d_l, norm_g, sgu_weights, wt, wgt):
    t = x2.shape[0]
    tm = min(256, t)
    steps = t // tm

    widths = (a_uvz.shape[1], d_q.shape[1], d_z.shape[1], d_l.shape[1])
    starts = (0, widths[0], widths[0] + widths[1], widths[0] + widths[1] + widths[2])

    def body(x_ref, dh_ref, uvz_ref, dsgu_ref, dq_ref, dz_ref, dl_ref, g_ref, lg_ref, lb_ref, ws_ref, bt_ref,
             wt_ref, wgt_ref,
             dx_ref, dw_hbm, dg_ref, dlg_ref, dlb_ref, dws_ref, dbt_ref, dw_acc, stage_ref, da_ref):
        i = pl.program_id(0)

        @pl.when(i == 0)
        def _():
            dw_acc[...] = jnp.zeros_like(dw_acc)
            for ref in (dg_ref, dlg_ref, dlb_ref, dws_ref, dbt_ref):
                ref[...] = jnp.zeros_like(ref)

        g = g_ref[...]
        n, r = _rms(x_ref[...])
        xn = (n * g).astype(BF16)
        dxn = None
        sgu_done = 0

        def sgu_pieces(count):
            nonlocal sgu_done
            _sgu_bwd_tile(uvz_ref, dsgu_ref, (lg_ref, lb_ref, ws_ref, bt_ref), da_ref,
                          (dlg_ref, dlb_ref, dws_ref, dbt_ref), range(sgu_done, sgu_done + count))
            sgu_done += count

        sgu_total = tm // SGU_CHUNK * SGU_GROUPS
        before_q, after_q_chunk = sgu_total // 2, (sgu_total // 4, sgu_total // 8, sgu_total // 8)
        for d_ref, col0 in reversed(tuple(zip((da_ref, dq_ref, dz_ref, dl_ref), starts))):
            if d_ref is dq_ref:
                sgu_pieces(before_q)
            if d_ref is da_ref:
                sgu_pieces(sgu_total - sgu_done)
            width = d_ref.shape[1]
            rows = wgt_ref[...] if d_ref is dl_ref else wt_ref[col0:col0 + width, :]
            term = jnp.dot(d_ref[...], rows, preferred_element_type=F32)
            dxn = term if dxn is None else dxn + term
            for c0 in range(0, width, 512):
                c1 = min(c0 + 512, width)
                dw_acc[col0 + c0:col0 + c1, :] += lax.dot_general(d_ref[:, c0:c1], xn, (((0,), (0,)), ((), ())),
                                                                  preferred_element_type=F32)
                if d_ref is dq_ref:
                    sgu_pieces(after_q_chunk[c0 // 512])
        dg_ref[...] += _rowsum(dxn * n)
        dx_ref[...] = dh_ref[...] + _rms_bwd(dxn * g, n, r)

        @pl.when(i == steps - 1)
        def _():
            for j in range(N_DEV):
                stage_ref[...] = dw_acc[j * IN_SHARD:(j + 1) * IN_SHARD, :]
                pltpu.sync_copy(stage_ref, dw_hbm.at[j])

    tile = lambda w: pl.BlockSpec((tm, w), lambda i: (i, 0))
    sgu_shapes = ((1, SGU_WIDTH), (1, SGU_WIDTH), (SGU_GROUPS, SGU_CHUNK, SGU_CHUNK), (SGU_CHUNK, SGU_GROUPS))
    return pl.pallas_call(
        body, name="inproj_sgu_bwd", grid=(steps,),
        out_shape=(jax.ShapeDtypeStruct((t, D_MODEL), F32), jax.ShapeDtypeStruct((N_DEV, IN_SHARD, D_MODEL), F32),
                   jax.ShapeDtypeStruct((1, D_MODEL), F32)) + tuple(jax.ShapeDtypeStruct(s, F32) for s in sgu_shapes),
        in_specs=[tile(D_MODEL), tile(D_MODEL), tile(widths[0]), tile(SGU_WIDTH)] + [tile(w) for w in widths[1:]]
        + [_whole((1, D_MODEL))] + [_whole(s) for s in sgu_shapes] + [VMEM_SPEC] * 2,
        out_specs=(tile(D_MODEL), HBM_SPEC, _whole((1, D_MODEL))) + tuple(_whole(s) for s in sgu_shapes),
        scratch_shapes=[pltpu.VMEM((sum(widths), D_MODEL), F32), pltpu.VMEM((IN_SHARD, D_MODEL), F32),
                        pltpu.VMEM((tm, widths[0]), BF16)],
        compiler_params=_params(1),
    )(x2, dh1, a_uvz, d_sgu, d_q, d_z, d_l, norm_g, *sgu_weights, wt, wgt)


def _reduce_adamw(recv, w, m, v, name, block=None, recv_transposed=False):
    n = recv.shape[0]
    rows, cols = w.shape[-2:]
    lead = w.ndim - 2
    if recv_transposed:
        rb = block or rows
        steps, padded = rows // rb, -(-cols // LANES) * LANES
        recv_spec = pl.BlockSpec((n, cols, rb), lambda i: (0, 0, i))
        blk = pl.BlockSpec((None,) * lead + (rb, cols), lambda i: (0,) * lead + (i, 0))
        scratch = [pltpu.VMEM((padded, rb), F32)]
    else:
        cb = block or cols
        steps = cols // cb
        recv_spec = pl.BlockSpec((n, rows, cb), lambda i: (0, 0, i))
        blk = pl.BlockSpec((None,) * lead + (rows, cb), lambda i: (0,) * lead + (0, i))
        scratch = []

    def body(r_ref, w_ref, m_ref, v_ref, g_ref, d_ref, nm_ref, nv_ref, *scratch_refs):
        g = r_ref[0].astype(F32)
        for i in range(1, n):
            g = g + r_ref[i].astype(F32)
        if recv_transposed:
            (gt_ref,) = scratch_refs
            whole = cols // 8 * 8
            gt_ref[whole:, :] = jnp.zeros((padded - whole, rb), F32)
            gt_ref[0:cols, :] = g
            g = gt_ref[...].T[:, 0:cols]
        m_new = ADAM_B1 * m_ref[...] + (1.0 - ADAM_B1) * g
        v_new = ADAM_B2 * v_ref[...] + (1.0 - ADAM_B2) * jnp.square(g)
        m_hat = m_new / (1.0 - ADAM_B1 ** ADAM_STEP)
        v_hat = v_new / (1.0 - ADAM_B2 ** ADAM_STEP)
        g_ref[...] = g
        d_ref[...] = -ADAM_LR * (m_hat / (jnp.sqrt(v_hat) + ADAM_EPS) + ADAM_WD * w_ref[...])
        nm_ref[...] = m_new
        nv_ref[...] = v_new

    return pl.pallas_call(
        body, name=name, grid=(steps,),
        out_shape=tuple(jax.ShapeDtypeStruct(w.shape, F32) for _ in range(4)),
        in_specs=[recv_spec, blk, blk, blk],
        out_specs=(blk, blk, blk, blk),
        scratch_shapes=scratch,
        compiler_params=_params(1),
    )(recv, w, m, v)


def _adamw_replicated(received, ws, ms, vs):
    nw = len(ws)
    starts = [sum(SMALL_PIECE_ROWS[:i]) for i in range(len(SMALL_PIECE_ROWS))]

    def natural(g_ref, row0, shape):
        cols, rows = shape[-1], _size(shape[:-1])
        if cols == LANES:
            return g_ref[row0:row0 + rows, :].reshape(shape)
        if cols < LANES:
            return g_ref[row0:row0 + 1, 0:cols].reshape(shape)
        per = cols // LANES
        return jnp.concatenate(
            [jnp.concatenate([g_ref[row0 + r * per + k:row0 + r * per + k + 1, :] for k in range(per)], axis=1)
             for r in range(rows)], axis=0).reshape(shape)

    def body(r_ref, *refs):
        w_refs, m_refs, v_refs = refs[:nw], refs[nw:2 * nw], refs[2 * nw:3 * nw]
        conv_ref, loss_ref = refs[3 * nw], refs[3 * nw + 1]
        out_refs, g_ref = refs[3 * nw + 2:-1], refs[-1]
        g = r_ref[0]
        for q in range(1, N_CHIPS):
            g = g + r_ref[q]
        g_ref[...] = g
        conv_ref[...] = natural(g_ref, starts[0], (CONV_K, 3 * DN_WIDTH))
        loss_ref[...] = natural(g_ref, starts[-1], (1, 1))
        for i in range(nw):
            gi = natural(g_ref, starts[1 + i], w_refs[i].shape)
            m_new = ADAM_B1 * m_refs[i][...] + (1.0 - ADAM_B1) * gi
            v_new = ADAM_B2 * v_refs[i][...] + (1.0 - ADAM_B2) * jnp.square(gi)
            m_hat = m_new / (1.0 - ADAM_B1 ** ADAM_STEP)
            v_hat = v_new / (1.0 - ADAM_B2 ** ADAM_STEP)
            out_refs[4 * i][...] = gi
            out_refs[4 * i + 1][...] = -ADAM_LR * (m_hat / (jnp.sqrt(v_hat) + ADAM_EPS) + ADAM_WD * w_refs[i][...])
            out_refs[4 * i + 2][...] = m_new
            out_refs[4 * i + 3][...] = v_new

    def spec(a):
        lead = max(a.ndim - 3, 0)
        return pl.BlockSpec((None,) * lead + a.shape[lead:], lambda: (0,) * a.ndim)

    weight_specs = [spec(a) for a in ws]
    return pl.pallas_call(
        body, name="adamw_replicated",
        out_shape=(jax.ShapeDtypeStruct((CONV_K, 3 * DN_WIDTH), F32), jax.ShapeDtypeStruct((1, 1), F32))
        + tuple(jax.ShapeDtypeStruct(a.shape, F32) for a in ws for _ in range(4)),
        in_specs=[pl.BlockSpec(received.shape, lambda: (0, 0, 0))] + weight_specs * 3,
        out_specs=(pl.BlockSpec((CONV_K, 3 * DN_WIDTH), lambda: (0, 0)), pl.BlockSpec((1, 1), lambda: (0, 0)))
        + tuple(s for s in weight_specs for _ in range(4)),
        scratch_shapes=[pltpu.VMEM(received.shape[1:], F32)],
        compiler_params=pltpu.CompilerParams(vmem_limit_bytes=VMEM_LIMIT),
    )(received, *ws, *ms, *vs)


def _pack_rows(pieces, rows):
    padded = [jnp.pad(jnp.ravel(p), (0, -p.size % LANES)) for p in pieces]
    flat = jnp.concatenate(padded)
    return jnp.pad(flat, (0, rows * LANES - flat.shape[0])).reshape(rows, LANES)


def kernel(x, p, norm_g, w_in, sgu_ln_g, sgu_ln_b, sgu_w_s, sgu_b_s, dn_conv_w, dn_a_log, dn_dt_bias, dn_o_norm_g, w_out, ple_norm_g, ple_gate_w, ple_proj_w, final_norm_g, loss_target, m_norm_g, m_w_in, m_sgu_ln_g, m_sgu_ln_b, m_sgu_w_s, m_sgu_b_s, m_dn_conv_w, m_dn_a_log, m_dn_dt_bias, m_dn_o_norm_g, m_w_out, m_ple_norm_g, m_ple_gate_w, m_ple_proj_w, m_final_norm_g, v_norm_g, v_w_in, v_sgu_ln_g, v_sgu_ln_b, v_sgu_w_s, v_sgu_b_s, v_dn_conv_w, v_dn_a_log, v_dn_dt_bias, v_dn_o_norm_g, v_w_out, v_ple_norm_g, v_ple_gate_w, v_ple_proj_w, v_final_norm_g):
    weights = dict(norm_g=norm_g, w_in=w_in, sgu_ln_g=sgu_ln_g, sgu_ln_b=sgu_ln_b, sgu_w_s=sgu_w_s, sgu_b_s=sgu_b_s,
                   dn_conv_w=dn_conv_w, dn_a_log=dn_a_log, dn_dt_bias=dn_dt_bias, dn_o_norm_g=dn_o_norm_g, w_out=w_out,
                   ple_norm_g=ple_norm_g, ple_gate_w=ple_gate_w, ple_proj_w=ple_proj_w, final_norm_g=final_norm_g)
    mom1 = dict(norm_g=m_norm_g, w_in=m_w_in, sgu_ln_g=m_sgu_ln_g, sgu_ln_b=m_sgu_ln_b, sgu_w_s=m_sgu_w_s,
                sgu_b_s=m_sgu_b_s, dn_conv_w=m_dn_conv_w, dn_a_log=m_dn_a_log, dn_dt_bias=m_dn_dt_bias,
                dn_o_norm_g=m_dn_o_norm_g, w_out=m_w_out, ple_norm_g=m_ple_norm_g, ple_gate_w=m_ple_gate_w,
                ple_proj_w=m_ple_proj_w, final_norm_g=m_final_norm_g)
    mom2 = dict(norm_g=v_norm_g, w_in=v_w_in, sgu_ln_g=v_sgu_ln_g, sgu_ln_b=v_sgu_ln_b, sgu_w_s=v_sgu_w_s,
                sgu_b_s=v_sgu_b_s, dn_conv_w=v_dn_conv_w, dn_a_log=v_dn_a_log, dn_dt_bias=v_dn_dt_bias,
                dn_o_norm_g=v_dn_o_norm_g, w_out=v_w_out, ple_norm_g=v_ple_norm_g, ple_gate_w=v_ple_gate_w,
                ple_proj_w=v_ple_proj_w, final_norm_g=v_final_norm_g)
    nb, s, _ = x.shape
    t = nb * s

    w_in_t = jnp.transpose(w_in, (2, 0, 1)).reshape(IN_SHARD, D_MODEL)
    w_in_blocks, conv_blocks = _all_gather([w_in_t.astype(BF16), dn_conv_w[0]])
    w_in_full_t = w_in_blocks.reshape(IN_COLS, D_MODEL)
    wgt = jnp.pad(w_in_full_t[sum(IN_GROUPS):], ((0, GATE_PAD - 2 * DN_HEADS), (0, 0)))
    conv_full = jnp.moveaxis(conv_blocks, 0, 1).reshape(CONV_K, 3 * DN_WIDTH)
    later_shards = [w_out[0].astype(BF16), ple_gate_w[0].astype(BF16), ple_proj_w[0].astype(BF16)]

    pad_row = lambda a: jnp.pad(a.reshape(1, -1), ((0, 0), (DN_HEADS, GATE_PAD - DN_HEADS - a.size)))
    alog, dtb = pad_row(dn_a_log), pad_row(dn_dt_bias)
    og = dn_o_norm_g.reshape(1, DN_HEAD_DIM)
    ws = sgu_w_s.reshape(SGU_GROUPS, SGU_CHUNK, SGU_CHUNK)
    b_t = sgu_b_s.reshape(SGU_GROUPS, SGU_CHUNK).T
    fin_g = final_norm_g.reshape(1, D_MODEL)

    x2 = x.reshape(t, D_MODEL)
    sgu_weights = (sgu_ln_g, sgu_ln_b, ws, b_t)
    a_uvz, b_qkv, b_z, b_l, a_out, conv_out, w_out_blocks, w_gate_blocks, w_proj_blocks = _inproj_fwd(
        x2, s, norm_g, w_in_full_t, wgt, sgu_weights, conv_full, later_shards)
    w_out_full = w_out_blocks.reshape(D_MODEL, D_MODEL)
    w_gate_full = w_gate_blocks.reshape(D_MODEL, D_MODEL)
    w_proj_full = jnp.moveaxis(w_proj_blocks, 0, 1).reshape(PLE_DIM, D_MODEL)
    qkv3 = b_qkv.reshape(nb, s, 3 * DN_WIDTH)
    conv_out = conv_out.reshape(nb, s, 3 * DN_WIDTH)
    z3 = b_z.reshape(nb, s, DN_WIDTH)
    l3 = b_l.reshape(nb, s, GATE_PAD)
    b_out, states, inverses = _dn_fwd(conv_out, z3, l3, alog, dtb, og)

    d_a, d_b, dh1, g_w_out, g_gate, g_proj, g_ple_g, g_fin_g, loss_tile = _head(
        a_out, b_out.reshape(t, DN_WIDTH), x2, p.reshape(t, PLE_DIM), loss_target.reshape(t, D_MODEL),
        w_out_full, w_gate_full, w_proj_full, ple_norm_g, fin_g)
    d_qkv, d_z, d_l, g_conv, g_alog, g_dtb, g_og, *head_received = _dn_bwd(
        qkv3, conv_out, z3, l3, conv_full, alog, dtb, og, states, inverses, d_b.reshape(nb, s, DN_WIDTH),
        [g_w_out, g_gate, g_proj])
    grad_x, g_w_in, g_norm, g_ln_g, g_ln_b, g_ws, g_bt = _inproj_bwd(
        x2, dh1, a_uvz, d_a, d_qkv.reshape(t, 3 * DN_WIDTH), d_z.reshape(t, DN_WIDTH), d_l.reshape(t, GATE_PAD),
        norm_g, sgu_weights, w_in_full_t, wgt)

    small = _pack_rows([g_conv, g_norm, g_ln_g, g_ln_b, g_ws, g_bt.T, g_alog[:, DN_HEADS:2 * DN_HEADS], g_dtb[:, DN_HEADS:2 * DN_HEADS], g_og,
                        g_ple_g, g_fin_g, (0.5 / D_MODEL) * loss_tile[0:1, 0:1]], SMALL_ROWS)
    w_in_received, small_received = _reduce_exchange(g_w_in, small)

    results = {}
    results["w_in"] = _reduce_adamw(w_in_received, w_in, m_w_in, v_w_in, "adamw_w_in", 4 * LANES, recv_transposed=True)
    for name, recv in zip(("w_out", "ple_gate_w", "ple_proj_w"), head_received):
        results[name] = _reduce_adamw(recv, weights[name], mom1[name], mom2[name], "adamw_" + name)
    names = [name for name, _ in REPLICATED]
    two_d = lambda a: a.reshape(1, -1) if a.ndim == 1 else a
    g_conv_sum, loss_sum, *flat_outs = _adamw_replicated(
        small_received, *[[two_d(src[k]) for k in names] for src in (weights, mom1, mom2)])
    for i, k in enumerate(names):
        results[k] = [a.reshape(weights[k].shape) for a in flat_outs[4 * i:4 * i + 4]]
    loss = loss_sum[0, 0]
    me = 4 * lax.axis_index("x") + 2 * lax.axis_index("y") + lax.axis_index("c")
    conv_mine = lax.dynamic_slice(g_conv_sum, (0, me * 192), (CONV_K, 192))
    results["dn_conv_w"] = _reduce_adamw(conv_mine[None], dn_conv_w, m_dn_conv_w, v_dn_conv_w, "adamw_dn_conv_w")

    return (loss, grad_x.reshape(nb, s, D_MODEL), *[results[k][0] for k in WEIGHT_ORDER],
            *[results[k][1] for k in WEIGHT_ORDER], *[results[k][2] for k in WEIGHT_ORDER],
            *[results[k][3] for k in WEIGHT_ORDER])
```

```python
import functools

import jax
import jax.numpy as jnp
from jax import lax
from jax.experimental import pallas as pl
from jax.experimental.pallas import tpu as pltpu

F32 = jnp.float32
BF16 = jnp.bfloat16

N_DEV = 8
D_MODEL = 1024
SGU_WIDTH = 512
SGU_GROUPS = 4
SGU_CHUNK = 128
DN_WIDTH = 512
DN_HEADS = 4
DN_HEAD_DIM = 128
DN_CHUNK = 128
CONV_K = 4
CONV_HALO = 8
PLE_DIM = 256
EPS = 1e-6
IN_COLS = 3592
IN_SHARD = IN_COLS // N_DEV
GATE_PAD = 128
IN_GROUPS = (3 * SGU_WIDTH, 3 * DN_WIDTH, DN_WIDTH)

ADAM_LR = 0.001
ADAM_B1 = 0.9
ADAM_B2 = 0.999
ADAM_EPS = 1e-08
ADAM_WD = 0.01
ADAM_STEP = 10

LANES = 128
VMEM_LIMIT = 56 * 1024 * 1024
MESH = pl.DeviceIdType.MESH

REPLICATED = (("norm_g", (1, D_MODEL)), ("sgu_ln_g", (1, SGU_WIDTH)), ("sgu_ln_b", (1, SGU_WIDTH)),
              ("sgu_w_s", (1, SGU_GROUPS, SGU_CHUNK, SGU_CHUNK)), ("sgu_b_s", (1, SGU_GROUPS, SGU_CHUNK)),
              ("dn_a_log", (1, DN_HEADS)), ("dn_dt_bias", (1, DN_HEADS)), ("dn_o_norm_g", (1, DN_HEAD_DIM)),
              ("ple_norm_g", (1, D_MODEL)), ("final_norm_g", (D_MODEL,)))
WEIGHT_ORDER = ("norm_g", "w_in", "sgu_ln_g", "sgu_ln_b", "sgu_w_s", "sgu_b_s", "dn_conv_w", "dn_a_log",
                "dn_dt_bias", "dn_o_norm_g", "w_out", "ple_norm_g", "ple_gate_w", "ple_proj_w", "final_norm_g")


def _size(shape):
    n = 1
    for s in shape:
        n *= s
    return n


SMALL_LAYOUT = (("conv", (CONV_K, 3 * DN_WIDTH)),) + REPLICATED + (("loss", (1,)),)
SMALL_PIECE_ROWS = tuple(-(-_size(s) // LANES) for _, s in SMALL_LAYOUT)
SMALL_ROWS = -(-sum(SMALL_PIECE_ROWS) // 8) * 8


def _bdot(a, b):
    return jnp.dot(a.astype(BF16), b.astype(BF16), preferred_element_type=F32)


def _sigmoid(x):
    return 0.5 * jnp.tanh(0.5 * x) + 0.5


@jax.custom_vjp
def _silu(x):
    return x * _sigmoid(x)


def _silu_fwd(x):
    s = _sigmoid(x)
    return x * s, (x, s)


def _silu_bwd(res, ct):
    x, s = res
    return (ct * (s * (1.0 + x * (1.0 - s))),)


_silu.defvjp(_silu_fwd, _silu_bwd)


def _normal_cdf(x):
    return 0.5 + 0.5 * lax.erf(x * (0.5 ** 0.5))


@jax.custom_vjp
def _gelu(x):
    return x * _normal_cdf(x)


def _gelu_fwd(x):
    cdf = _normal_cdf(x)
    return x * cdf, (x, cdf)


def _gelu_bwd(res, ct):
    x, cdf = res
    pdf = jnp.exp(-0.5 * x * x) * ((2.0 * jnp.pi) ** -0.5)
    return (ct * (cdf + x * pdf),)


_gelu.defvjp(_gelu_fwd, _gelu_bwd)


def _softplus(x):
    return jnp.maximum(x, 0.0) + jnp.log1p(jnp.exp(-jnp.abs(x)))


@jax.custom_vjp
def _l2n(x):
    return x * lax.rsqrt(jnp.sum(x * x, axis=-1, keepdims=True) + EPS)


def _l2n_fwd(x):
    r = lax.rsqrt(jnp.sum(x * x, axis=-1, keepdims=True) + EPS)
    n = x * r
    return n, (n, r)


def _l2n_bwd(res, ct):
    n, r = res
    return (r * (ct - n * jnp.sum(ct * n, axis=-1, keepdims=True)),)


_l2n.defvjp(_l2n_fwd, _l2n_bwd)


def _rms(x):
    r = lax.rsqrt(jnp.mean(x * x, axis=-1, keepdims=True) + EPS)
    return x * r, r


def _rms_bwd(dn, n, r):
    return r * (dn - n * jnp.mean(dn * n, axis=-1, keepdims=True))


@jax.custom_vjp
def _rms_normed(x):
    return _rms(x)[0]


def _rms_normed_fwd(x):
    n, r = _rms(x)
    return n, (n, r)


def _rms_normed_bwd(res, ct):
    return (_rms_bwd(ct, *res),)


_rms_normed.defvjp(_rms_normed_fwd, _rms_normed_bwd)


def _onehot_row(idx, width):
    return (lax.broadcasted_iota(jnp.int32, (1, width), 1) == idx).astype(F32)


def _rowsum(x):
    return jnp.sum(x, axis=0, keepdims=True)


def _iota2(n):
    return lax.broadcasted_iota(jnp.int32, (n, n), 0), lax.broadcasted_iota(jnp.int32, (n, n), 1)


def _bmm(a, b):
    return lax.dot_general(a.astype(BF16), b.astype(BF16), (((2,), (1,)), ((0,), (0,))), preferred_element_type=F32)


def _bmm_nt(a, b):
    return lax.dot_general(a.astype(BF16), b.astype(BF16), (((2,), (2,)), ((0,), (0,))), preferred_element_type=F32)


def _bmm_tn(a, b):
    return lax.dot_general(a.astype(BF16), b.astype(BF16), (((1,), (1,)), ((0,), (0,))), preferred_element_type=F32)


def _tri_inv_impl(a):
    n = a.shape[-1]
    r, c = _iota2(n)
    x = r ^ c
    eye = (r == c).astype(F32)
    ad = jnp.where(x < 16, a, 0.0)
    p2 = _bmm(ad, ad)
    e = p2 - ad - _bmm(ad, p2)
    p4 = _bmm(p2, p2)
    e = e + p4 + _bmm(e, p4)
    p8 = _bmm(p4, p4)
    e = e + p8 + _bmm(e, p8)
    size = 16
    while size < n:
        m = jnp.where(jnp.logical_and(x < 2 * size, x >= size), a, 0.0)
        f = m + _bmm(m, e)
        e = e - f - _bmm(e, f)
        size *= 2
    return e + eye


@jax.custom_vjp
def _tri_inv(a, known):
    return _tri_inv_impl(a) if known is None else known


def _tri_inv_fwd(a, known):
    t = _tri_inv(a, known)
    return t, (t, known)


def _tri_inv_bwd(res, dt):
    t, known = res
    return -_bmm_tn(t, _bmm_nt(dt, t)), None if known is None else jnp.zeros_like(known)


_tri_inv.defvjp(_tri_inv_fwd, _tri_inv_bwd)


@jax.custom_vjp
def _standardized(x):
    xc = x - jnp.mean(x, axis=-1, keepdims=True)
    return xc * lax.rsqrt(jnp.mean(xc * xc, axis=-1, keepdims=True) + EPS)


def _standardized_fwd(x):
    xc = x - jnp.mean(x, axis=-1, keepdims=True)
    rstd = lax.rsqrt(jnp.mean(xc * xc, axis=-1, keepdims=True) + EPS)
    y = xc * rstd
    return y, (y, rstd)


def _standardized_bwd(res, ct):
    y, rstd = res
    return (rstd * (ct - jnp.mean(ct, axis=-1, keepdims=True) - y * jnp.mean(ct * y, axis=-1, keepdims=True)),)


_standardized.defvjp(_standardized_fwd, _standardized_bwd)


def _sgu_core(u, v, z, lg, lb, ws, bcol):
    n = ws.shape[0]
    r, c = _iota2(n)
    wm = jnp.where(r >= c, ws, 0.0)
    gu = _gelu(u)
    gv = _gelu(v)
    ln = _standardized(gv) * lg + lb
    s = _bdot(wm, ln) + bcol
    return gu * s * _silu(z)


def _lanes_of(x):
    return jnp.concatenate([x[i] for i in range(x.shape[0])], axis=1)


def _batch_of(x, width):
    return jnp.concatenate([x[None, :, i * width:(i + 1) * width] for i in range(x.shape[1] // width)], axis=0)


def _mask_dot(mask, x):
    hi = x.astype(BF16)
    lo = (x - hi.astype(F32)).astype(BF16)
    m = mask.astype(BF16)
    return jnp.dot(m, hi, preferred_element_type=F32) + jnp.dot(m, lo, preferred_element_type=F32)


def _split_dot(x, mask, dims):
    hi = x.astype(BF16)
    lo = (x - hi.astype(F32)).astype(BF16)
    m = mask.astype(BF16)
    return (lax.dot_general(hi, m, dims, preferred_element_type=F32)
            + lax.dot_general(lo, m, dims, preferred_element_type=F32))


def _lane_select(lanes, blocks, first_lane):
    src = lax.broadcasted_iota(jnp.int32, (lanes, blocks * LANES), 0)
    dst = lax.broadcasted_iota(jnp.int32, (lanes, blocks * LANES), 1) // LANES
    return src == dst + first_lane


def _pick_lanes(x, blocks, first_lane):
    return _pick_lanes_vjp(blocks, first_lane, x)


@functools.partial(jax.custom_vjp, nondiff_argnums=(0, 1))
def _pick_lanes_vjp(blocks, first_lane, x):
    return _split_dot(x, _lane_select(x.shape[-1], blocks, first_lane), (((1,), (0,)), ((), ())))


def _pick_lanes_fwd(blocks, first_lane, x):
    return _pick_lanes_vjp(blocks, first_lane, x), x.shape[-1]


def _pick_lanes_bwd(blocks, first_lane, lanes, ct):
    return (sum(jnp.sum(ct[:, h * LANES:(h + 1) * LANES], axis=-1, keepdims=True) * _onehot_row(first_lane + h, lanes)
                for h in range(blocks)),)


_pick_lanes_vjp.defvjp(_pick_lanes_fwd, _pick_lanes_bwd)


def _tri_mask(n, upper):
    r, c = _iota2(n)
    return (r <= c) if upper else (r >= c)


@jax.custom_vjp
def _cumsum_rows(x):
    return _mask_dot(_tri_mask(x.shape[0], False), x)


def _cumsum_rows_fwd(x):
    return _cumsum_rows(x), None


def _cumsum_rows_bwd(_, ct):
    return (_mask_dot(_tri_mask(ct.shape[0], True), ct),)


_cumsum_rows.defvjp(_cumsum_rows_fwd, _cumsum_rows_bwd)


@jax.custom_vjp
def _colsum_all_rows(x):
    return _mask_dot(jnp.ones((x.shape[0], x.shape[0]), jnp.bool_), x)


def _colsum_all_rows_fwd(x):
    return _colsum_all_rows(x), None


def _colsum_all_rows_bwd(_, ct):
    return (_mask_dot(jnp.ones((ct.shape[0], ct.shape[0]), jnp.bool_), ct),)


_colsum_all_rows.defvjp(_colsum_all_rows_fwd, _colsum_all_rows_bwd)


def _dn_core(cq, ck, cv, z, logits, state, alog, dtb, og, t_known=None):
    gn, cn, dh = cq.shape
    heads = gn // logits.shape[0]
    q = _l2n(_silu(cq)) * (dh ** -0.5)
    k = _l2n(_silu(ck))
    v = _silu(cv)
    beta_lanes = _sigmoid(logits)
    g_lanes = -jnp.exp(alog) * _softplus(logits + dtb)
    beta_all = jnp.concatenate([_pick_lanes(beta_lanes[b], heads, 0) for b in range(logits.shape[0])], axis=1)
    g_all = jnp.concatenate([_pick_lanes(g_lanes[b], heads, heads) for b in range(logits.shape[0])], axis=1)
    beta = _batch_of(beta_all, dh)
    g_wide = _batch_of(g_all, dh)
    r, c = _iota2(cn)
    tril = r >= c
    rw = lax.broadcasted_iota(jnp.int32, (cn, dh), 0)
    cw = lax.broadcasted_iota(jnp.int32, (cn, dh), 1)
    upper_wide = (rw <= cw).astype(F32)
    gc_wide = _batch_of(_cumsum_rows(g_all), dh)
    gc_cols = _batch_of(_colsum_all_rows(_lanes_of(g_wide * upper_wide)), dh)[:, :, :cn]
    decay = jnp.exp(jnp.where(tril, gc_wide[:, :, :cn] - gc_cols, -1e30))
    kb = k * beta
    kk = _bmm_nt(kb, k) * decay
    t = _tri_inv(jnp.where(r > c, kk, 0.0), t_known)
    eg = jnp.exp(gc_wide)
    sol = _bmm(t, jnp.concatenate([v * beta, kb * eg], axis=-1))
    u_val, w_dec = sol[:, :, :dh], sol[:, :, dh:]
    qk = _bmm_nt(q, k) * decay
    g_last = jnp.sum(g_wide, axis=1, keepdims=True)
    k_dec = k * jnp.exp(g_last - gc_wide)
    ws = _bmm(jnp.concatenate([w_dec, q * eg], axis=1), state)
    v_new = u_val - ws[:, :cn]
    o = ws[:, cn:] + _bmm(qk, v_new)
    new_state = state * jnp.exp(g_last) + _bmm_tn(k_dec, v_new)
    return _rms_normed(o) * og * _silu(z), new_state, t


N_CHIPS = 4
HBM_SPEC = pl.BlockSpec(memory_space=pl.ANY)


def _place():
    return lax.axis_index("x"), lax.axis_index("y"), lax.axis_index("c")


def _other_chip(k):
    x, y, _ = _place()
    px = 1 - x if k & 2 else x
    py = 1 - y if k & 1 else y
    return px, py, 2 * px + py


def _remote(src, dst, send_sem, recv_sem, device):
    return pltpu.make_async_remote_copy(src_ref=src, dst_ref=dst, send_sem=send_sem, recv_sem=recv_sem,
                                        device_id=device, device_id_type=MESH)


def _other_device(k):
    x, y, c = _place()
    px = 1 - x if k & 4 else x
    py = 1 - y if k & 2 else y
    pc = 1 - c if k & 1 else c
    return (px, py, pc), 4 * px + 2 * py + pc


def _direct_exchange(srcs, outs, send_sems, recv_sems, local_sems, gather):
    x, y, c = _place()
    me = 4 * x + 2 * y + c

    def copies(arriving):
        out_list = []
        for a, (src, out) in enumerate(zip(srcs, outs)):
            for k in range(1, N_DEV):
                peer, index = _other_device(k)
                mine = src if gather else src.at[index]
                out_list.append(_remote(mine, out.at[index if arriving else me], send_sems.at[a, k - 1],
                                        recv_sems.at[a, k - 1], peer))
        return out_list

    def local_copies():
        return [pltpu.make_async_copy(src if gather else src.at[me], out.at[me], local_sems.at[a])
                for a, (src, out) in enumerate(zip(srcs, outs))]

    def start():
        for cp in local_copies() + copies(False):
            cp.start()

    def wait():
        for cp in copies(True):
            cp.wait_recv()
        for cp in copies(False):
            cp.wait_send()
        for cp in local_copies():
            cp.wait()

    return start, wait


def _exchange_scratch(n):
    return [pltpu.SemaphoreType.DMA((n, N_DEV - 1)), pltpu.SemaphoreType.DMA((n, N_DEV - 1)), pltpu.SemaphoreType.DMA((n,))]


def _all_gather(shards):
    n = len(shards)

    def body(*refs):
        srcs, outs = refs[:n], refs[n:2 * n]
        send_sems, recv_sems, local_sems = refs[2 * n:]
        x, y, c = _place()
        me = 4 * x + 2 * y + c
        sibling = (x, y, 1 - c)
        local = [pltpu.make_async_copy(srcs[a], outs[a].at[me], local_sems.at[a]) for a in range(n)]
        for cp in local:
            cp.start()
        sends = []
        for a in range(n):
            sends.append(_remote(srcs[a], outs[a].at[me], send_sems.at[a, 0], recv_sems.at[a, 0], sibling))
        for k in range(1, N_CHIPS):
            px, py, _ = _other_chip(k)
            for a in range(n):
                sends.append(_remote(srcs[a], outs[a].at[me], send_sems.at[a, k], recv_sems.at[a, k], (px, py, c)))
        for cp in sends:
            cp.start()
        passed = []
        for k in range(1, N_CHIPS):
            px, py, _ = _other_chip(k)
            blk = 4 * px + 2 * py + c
            for a in range(n):
                _remote(srcs[a], outs[a].at[blk], send_sems.at[a, k], recv_sems.at[a, k], (px, py, c)).wait_recv()
            for a in range(n):
                cp = _remote(outs[a].at[blk], outs[a].at[blk], send_sems.at[a, 3 + k], recv_sems.at[a, 3 + k], sibling)
                cp.start()
                passed.append(cp)
        for a in range(n):
            _remote(srcs[a], outs[a].at[me + 1 - 2 * c], send_sems.at[a, 0], recv_sems.at[a, 0], sibling).wait_recv()
        for k in range(1, N_CHIPS):
            px, py, _ = _other_chip(k)
            blk = 4 * px + 2 * py + 1 - c
            for a in range(n):
                _remote(srcs[a], outs[a].at[blk], send_sems.at[a, 3 + k], recv_sems.at[a, 3 + k], sibling).wait_recv()
        for cp in sends + passed:
            cp.wait_send()
        for cp in local:
            cp.wait()

    return pl.pallas_call(
        body, name="all_gather_weights",
        out_shape=tuple(jax.ShapeDtypeStruct((N_DEV,) + a.shape, a.dtype) for a in shards),
        in_specs=[HBM_SPEC] * n, out_specs=(HBM_SPEC,) * n,
        scratch_shapes=[pltpu.SemaphoreType.DMA((n, N_DEV - 1)), pltpu.SemaphoreType.DMA((n, N_DEV - 1)),
                        pltpu.SemaphoreType.DMA((n,))],
    )(*shards)


def _reduce_exchange(by_device, small):
    _, rows, cols = by_device.shape

    def body(g_ref, small_ref, out_ref, small_out_ref, from_sibling, small_from_sibling, stage, sums, small_own, small_sum,
             pair_send, pair_recv, chip_send, chip_recv, local_sems):
        x, y, c = _place()
        mine = 2 * x + y
        sibling = (x, y, 1 - c)
        chips = [(x, y, mine)] + [_other_chip(k) for k in range(1, N_CHIPS)]
        to_sibling = [_remote(g_ref.at[2 * chips[k][2] + 1 - c], from_sibling.at[k], pair_send.at[k], pair_recv.at[k], sibling)
                      for k in range(N_CHIPS)]
        to_sibling.append(_remote(small_ref, small_from_sibling, pair_send.at[N_CHIPS], pair_recv.at[N_CHIPS], sibling))
        for cp in to_sibling:
            cp.start()
        small_mine = pltpu.make_async_copy(small_ref, small_own, local_sems.at[0])
        small_mine.start()
        to_chips = []
        for k in (3, 1, 2, 0):
            px, py, chip = chips[k]
            mine_k = pltpu.make_async_copy(g_ref.at[2 * chip + c], stage, local_sems.at[1])
            mine_k.start()
            to_sibling[k].wait_recv()
            mine_k.wait()
            sums[k] = (stage[...] + from_sibling[k]).astype(sums.dtype)
            if k:
                cp = _remote(sums.at[k], out_ref.at[mine], chip_send.at[0, k - 1], chip_recv.at[0, k - 1], (px, py, c))
                cp.start()
                to_chips.append(cp)
        own_block = pltpu.make_async_copy(sums.at[0], out_ref.at[mine], local_sems.at[2])
        own_block.start()
        to_sibling[N_CHIPS].wait_recv()
        small_mine.wait()
        small_sum[...] = small_own[...] + small_from_sibling[...]
        for k in range(1, N_CHIPS):
            px, py, _ = chips[k]
            cp = _remote(small_sum, small_out_ref.at[mine], chip_send.at[1, k - 1], chip_recv.at[1, k - 1], (px, py, c))
            cp.start()
            to_chips.append(cp)
        own_small = pltpu.make_async_copy(small_sum, small_out_ref.at[mine], local_sems.at[3])
        own_small.start()
        for k in range(1, N_CHIPS):
            px, py, chip = chips[k]
            _remote(sums.at[k], out_ref.at[chip], chip_send.at[0, k - 1], chip_recv.at[0, k - 1], (px, py, c)).wait_recv()
            _remote(small_sum, small_out_ref.at[chip], chip_send.at[1, k - 1], chip_recv.at[1, k - 1], (px, py, c)).wait_recv()
        for cp in to_sibling + to_chips:
            cp.wait_send()
        own_block.wait()
        own_small.wait()

    return pl.pallas_call(
        body, name="grad_reduce_exchange",
        out_shape=(jax.ShapeDtypeStruct((N_CHIPS, rows, cols), BF16), jax.ShapeDtypeStruct((N_CHIPS,) + small.shape, F32)),
        in_specs=[HBM_SPEC, HBM_SPEC], out_specs=(HBM_SPEC, HBM_SPEC),
        scratch_shapes=[pltpu.VMEM((N_CHIPS, rows, cols), F32), pltpu.VMEM(small.shape, F32), pltpu.VMEM((rows, cols), F32),
                        pltpu.VMEM((N_CHIPS, rows, cols), BF16), pltpu.VMEM(small.shape, F32), pltpu.VMEM(small.shape, F32),
                        pltpu.SemaphoreType.DMA((N_CHIPS + 1,)), pltpu.SemaphoreType.DMA((N_CHIPS + 1,)),
                        pltpu.SemaphoreType.DMA((2, N_CHIPS - 1)), pltpu.SemaphoreType.DMA((2, N_CHIPS - 1)),
                        pltpu.SemaphoreType.DMA((4,))],
        compiler_params=pltpu.CompilerParams(vmem_limit_bytes=VMEM_LIMIT),
    )(by_device, small)


def _params(n_axes):
    return pltpu.CompilerParams(dimension_semantics=("arbitrary",) * n_axes, vmem_limit_bytes=VMEM_LIMIT)


def _whole(shape):
    return pl.BlockSpec(shape, lambda *_: (0,) * len(shape))


VMEM_SPEC = pl.BlockSpec(memory_space=pltpu.VMEM)


def _inproj_fwd(x2, seq_len, norm_g, wt, wgt, sgu_weights, conv_w, later_shards):
    t = x2.shape[0]
    tm = min(512, seq_len)
    tiles_per_seq = seq_len // tm
    steps = t // tm
    ns = len(later_shards)

    widths = IN_GROUPS + (wgt.shape[0],)
    starts = (0, IN_GROUPS[0], IN_GROUPS[0] + IN_GROUPS[1], 0)

    def body(x_ref, g_ref, wt_ref, wg_ref, lg_ref, lb_ref, ws_ref, bt_ref, cw_ref, *rest):
        shard_refs, rest = rest[:ns], rest[ns:]
        a_ref, q_ref, z_ref, l_ref, sgu_ref, c_ref = rest[:6]
        gathered_refs, (xpad_ref, send_sems, recv_sems, local_sems) = rest[6:6 + ns], rest[6 + ns:]
        start_gather, wait_gather = _direct_exchange(shard_refs, gathered_refs, send_sems, recv_sems, local_sems, True)
        pl.when(pl.program_id(0) == 0)(start_gather)

        @pl.when(pl.program_id(0) % tiles_per_seq == 0)
        def _():
            xpad_ref[0:CONV_HALO, :] = jnp.zeros((CONV_HALO, xpad_ref.shape[1]), F32)

        n, _ = _rms(x_ref[...])
        xn = (n * g_ref[...]).astype(BF16)

        def project(w_ref, row0, width, o_ref):
            for c0 in range(0, width, 512):
                c1 = min(c0 + 512, width)
                o_ref[:, c0:c1] = lax.dot_general(xn, w_ref[row0 + c0:row0 + c1, :], (((1,), (1,)), ((), ())),
                                                  preferred_element_type=F32)

        def sgu_rows(row0):
            for grp in range(SGU_GROUPS):
                args = _sgu_pieces(a_ref, lg_ref, lb_ref, ws_ref, bt_ref, row0, grp)
                sgu_ref[pl.ds(row0, SGU_CHUNK), pl.ds(grp * 128, 128)] = _sgu_core(*args).astype(sgu_ref.dtype)

        def conv():
            xpad_ref[CONV_HALO:, :] = q_ref[...]
            acc = None
            for j in range(CONV_K):
                term = cw_ref[j:j + 1, :] * xpad_ref[pl.ds(CONV_HALO - CONV_K + 1 + j, tm), :]
                acc = term if acc is None else acc + term
            c_ref[...] = acc
            xpad_ref[0:CONV_HALO, :] = xpad_ref[tm:tm + CONV_HALO, :]

        groups = tuple(zip((wt_ref, wt_ref, wt_ref, wg_ref), starts, widths, (a_ref, q_ref, z_ref, l_ref)))
        row_chunks = list(range(0, tm, SGU_CHUNK))
        project(*groups[0])
        for row0 in row_chunks[:len(row_chunks) // 2]:
            sgu_rows(row0)
        project(*groups[1])
        for row0 in row_chunks[len(row_chunks) // 2:]:
            sgu_rows(row0)
        project(*groups[3])
        conv()
        project(*groups[2])
        pl.when(pl.program_id(0) == steps - 1)(wait_gather)

    tile = lambda w: pl.BlockSpec((tm, w), lambda i: (i, 0))
    sgu_shapes = ((1, SGU_WIDTH), (1, SGU_WIDTH), (SGU_GROUPS, SGU_CHUNK, SGU_CHUNK), (SGU_CHUNK, SGU_GROUPS))
    return pl.pallas_call(
        body, name="inproj_sgu_conv_fwd", grid=(steps,),
        out_shape=tuple(jax.ShapeDtypeStruct((t, w), F32) for w in widths)
        + (jax.ShapeDtypeStruct((t, SGU_WIDTH), BF16), jax.ShapeDtypeStruct((t, widths[1]), F32))
        + tuple(jax.ShapeDtypeStruct((N_DEV,) + a.shape, a.dtype) for a in later_shards),
        in_specs=[tile(D_MODEL), _whole((1, D_MODEL)), VMEM_SPEC, VMEM_SPEC]
        + [_whole(s) for s in sgu_shapes] + [_whole((CONV_K, widths[1]))] + [HBM_SPEC] * ns,
        out_specs=tuple(tile(w) for w in widths) + (tile(SGU_WIDTH), tile(widths[1])) + (HBM_SPEC,) * ns,
        scratch_shapes=[pltpu.VMEM((CONV_HALO + tm, widths[1]), F32)] + _exchange_scratch(ns),
        compiler_params=_params(1),
    )(x2, norm_g, wt, wgt, *sgu_weights, conv_w, *later_shards)


def _sgu_pieces(uvz_ref, lg_ref, lb_ref, ws_ref, bt_ref, row0, grp):
    rows = pl.ds(row0, SGU_CHUNK)
    lanes = pl.ds(grp * 128, 128)
    u = uvz_ref[rows, pl.ds(grp * 128, 128)]
    v = uvz_ref[rows, pl.ds(SGU_WIDTH + grp * 128, 128)]
    z = uvz_ref[rows, pl.ds(2 * SGU_WIDTH + grp * 128, 128)]
    bcol = jnp.sum(bt_ref[...] * _onehot_row(grp, SGU_GROUPS), axis=-1, keepdims=True)
    return u, v, z, lg_ref[:, lanes], lb_ref[:, lanes], ws_ref[grp], bcol


def _sgu_bwd_tile(uvz_ref, do_ref, sgu_refs, duvz_ref, grad_refs, pieces):
    lg_ref, lb_ref, ws_ref, bt_ref = sgu_refs
    dlg_ref, dlb_ref, dws_ref, dbt_ref = grad_refs
    for piece in pieces:
        row0, grp = piece // SGU_GROUPS * SGU_CHUNK, piece % SGU_GROUPS
        rows = pl.ds(row0, SGU_CHUNK)
        lanes = pl.ds(grp * 128, 128)
        args = _sgu_pieces(uvz_ref, lg_ref, lb_ref, ws_ref, bt_ref, row0, grp)
        _, pull = jax.vjp(_sgu_core, *args)
        du, dv, dz, dlg, dlb, dws, dbcol = pull(do_ref[rows, lanes])
        duvz_ref[rows, pl.ds(grp * 128, 128)] = du.astype(duvz_ref.dtype)
        duvz_ref[rows, pl.ds(SGU_WIDTH + grp * 128, 128)] = dv.astype(duvz_ref.dtype)
        duvz_ref[rows, pl.ds(2 * SGU_WIDTH + grp * 128, 128)] = dz.astype(duvz_ref.dtype)
        dlg_ref[:, lanes] += dlg
        dlb_ref[:, lanes] += dlb
        dws_ref[grp] += dws
        dbt_ref[...] += dbcol * _onehot_row(grp, SGU_GROUPS)


def _dn_pairs(nb):
    return [(b, h) for b in range(nb) for h in range(DN_HEADS)]


def _dn_batch_args(c_ref, z_ref):
    pairs = _dn_pairs(c_ref.shape[0])
    pick = lambda ref, b, col: ref[b, :, pl.ds(col, DN_HEAD_DIM)]
    cq = jnp.stack([pick(c_ref, b, h * DN_HEAD_DIM) for b, h in pairs])
    ck = jnp.stack([pick(c_ref, b, DN_WIDTH + h * DN_HEAD_DIM) for b, h in pairs])
    cv = jnp.stack([pick(c_ref, b, 2 * DN_WIDTH + h * DN_HEAD_DIM) for b, h in pairs])
    z = jnp.stack([pick(z_ref, b, h * DN_HEAD_DIM) for b, h in pairs])
    return cq, ck, cv, z


def _dn_weight_specs():
    return [_whole((CONV_K, 3 * DN_WIDTH)), _whole((1, GATE_PAD)), _whole((1, GATE_PAD)), _whole((1, DN_HEAD_DIM))]


def _dn_fwd(conv_out, zg, logits, alog, dtb, og):
    nb, s, _ = conv_out.shape
    nc = s // DN_CHUNK
    pairs = _dn_pairs(nb)
    gn = len(pairs)
    chunk = lambda w: pl.BlockSpec((nb, DN_CHUNK, w), lambda n: (0, n, 0))

    def body(c_ref, z_ref, l_ref, alog_ref, dtb_ref, og_ref, out_ref, st_ref, inv_ref, state_ref):
        n = pl.program_id(0)

        @pl.when(n == 0)
        def _():
            state_ref[...] = jnp.zeros_like(state_ref)

        cq, ck, cv, z = _dn_batch_args(c_ref, z_ref)
        state = state_ref[...]
        st_ref[...] = state
        out, new_state, t = _dn_core(cq, ck, cv, z, l_ref[...], state, alog_ref[...], dtb_ref[...], og_ref[...])
        state_ref[...] = new_state
        inv_ref[...] = t.astype(inv_ref.dtype)
        for i, (b, h) in enumerate(pairs):
            out_ref[b, :, pl.ds(h * DN_HEAD_DIM, DN_HEAD_DIM)] = out[i].astype(out_ref.dtype)

    per_chunk = pl.BlockSpec((None, gn, DN_HEAD_DIM, DN_HEAD_DIM), lambda n: (n, 0, 0, 0))
    return pl.pallas_call(
        body, name="deltanet_fwd", grid=(nc,),
        out_shape=(jax.ShapeDtypeStruct((nb, s, DN_WIDTH), BF16),
                   jax.ShapeDtypeStruct((nc, gn, DN_HEAD_DIM, DN_HEAD_DIM), F32),
                   jax.ShapeDtypeStruct((nc, gn, DN_CHUNK, DN_CHUNK), BF16)),
        in_specs=[chunk(3 * DN_WIDTH), chunk(DN_WIDTH), chunk(GATE_PAD)] + _dn_weight_specs()[1:],
        out_specs=(chunk(DN_WIDTH), per_chunk, pl.BlockSpec((None, gn, DN_CHUNK, DN_CHUNK), lambda n: (n, 0, 0, 0))),
        scratch_shapes=[pltpu.VMEM((gn, DN_HEAD_DIM, DN_HEAD_DIM), F32)],
        compiler_params=_params(1),
    )(conv_out, zg, logits, alog, dtb, og)


def _dn_bwd(qkv, conv_out, zg, logits, conv_w, alog, dtb, og, states, inverses, d_out, head_grads):
    nb, s, _ = qkv.shape
    nc = s // DN_CHUNK
    rev = lambda n: nc - 1 - n
    pairs = _dn_pairs(nb)
    gn = len(pairs)
    ng = len(head_grads)

    def body(cur_ref, c_ref, z_ref, l_ref, w_ref, alog_ref, dtb_ref, og_ref, st_ref, inv_ref, do_ref, *rest):
        grad_refs, rest = rest[:ng], rest[ng:]
        dqkv_ref, dz_ref, dl_ref, dw_ref, dalog_ref, ddtb_ref, dog_ref = rest[:7]
        recv_refs, (dstate_ref, dcpad_ref, dw_part_ref, send_sems, recv_sems, local_sems) = rest[7:7 + ng], rest[7 + ng:]
        n = pl.program_id(0)
        start_exchange, wait_exchange = _direct_exchange(grad_refs, recv_refs, send_sems, recv_sems, local_sems, False)
        pl.when(n == 0)(start_exchange)

        @pl.when(n == 0)
        def _():
            dw_part_ref[...] = jnp.zeros_like(dw_part_ref)
            dalog_ref[...] = jnp.zeros_like(dalog_ref)
            ddtb_ref[...] = jnp.zeros_like(ddtb_ref)
            dog_ref[...] = jnp.zeros_like(dog_ref)
            dstate_ref[...] = jnp.zeros_like(dstate_ref)
            dcpad_ref[:, DN_CHUNK:, :] = jnp.zeros((nb, CONV_HALO, 3 * DN_WIDTH), F32)

        cq, ck, cv, z = _dn_batch_args(c_ref, z_ref)
        d_out_g = jnp.stack([do_ref[b, :, pl.ds(h * DN_HEAD_DIM, DN_HEAD_DIM)] for b, h in pairs])
        t_known = inv_ref[...].astype(F32)
        core = lambda *args: _dn_core(*args, t_known=t_known)[:2]
        _, pull = jax.vjp(core, cq, ck, cv, z, l_ref[...], st_ref[...], alog_ref[...], dtb_ref[...], og_ref[...])
        dcq, dck, dcv, dz, dlog, dstate, dalog, ddtb, dog = pull((d_out_g, dstate_ref[...]))
        dstate_ref[...] = dstate
        dl_ref[...] = dlog.astype(dl_ref.dtype)
        dalog_ref[...] += dalog
        ddtb_ref[...] += ddtb
        dog_ref[...] += dog
        for i, (b, h) in enumerate(pairs):
            dcpad_ref[b, 0:DN_CHUNK, pl.ds(h * DN_HEAD_DIM, DN_HEAD_DIM)] = dcq[i]
            dcpad_ref[b, 0:DN_CHUNK, pl.ds(DN_WIDTH + h * DN_HEAD_DIM, DN_HEAD_DIM)] = dck[i]
            dcpad_ref[b, 0:DN_CHUNK, pl.ds(2 * DN_WIDTH + h * DN_HEAD_DIM, DN_HEAD_DIM)] = dcv[i]
            dz_ref[b, :, pl.ds(h * DN_HEAD_DIM, DN_HEAD_DIM)] = dz[i].astype(dz_ref.dtype)
        for b in range(nb):
            xb = cur_ref[b]
            dx = None
            for j in range(CONV_K):
                shifted = dcpad_ref[b, pl.ds(CONV_K - 1 - j, DN_CHUNK), :]
                term = w_ref[j:j + 1, :] * shifted
                dx = term if dx is None else dx + term
                dw_part_ref[j] += jnp.sum((shifted * xb).reshape(DN_CHUNK // 8, 8, 3 * DN_WIDTH), axis=0)
            dqkv_ref[b] = dx.astype(dqkv_ref.dtype)
            dcpad_ref[b, DN_CHUNK:, :] = dcpad_ref[b, 0:CONV_HALO, :]

        @pl.when(n == nc - 1)
        def _():
            dw_ref[...] = jnp.sum(dw_part_ref[...], axis=1)

        pl.when(n == nc - 1)(wait_exchange)

    chunk = lambda w: pl.BlockSpec((nb, DN_CHUNK, w), lambda n: (0, rev(n), 0))
    return pl.pallas_call(
        body, name="deltanet_bwd", grid=(nc,),
        out_shape=(jax.ShapeDtypeStruct((nb, s, 3 * DN_WIDTH), BF16), jax.ShapeDtypeStruct((nb, s, DN_WIDTH), BF16),
                   jax.ShapeDtypeStruct((nb, s, GATE_PAD), BF16), jax.ShapeDtypeStruct((CONV_K, 3 * DN_WIDTH), F32),
                   jax.ShapeDtypeStruct((1, GATE_PAD), F32), jax.ShapeDtypeStruct((1, GATE_PAD), F32),
                   jax.ShapeDtypeStruct((1, DN_HEAD_DIM), F32))
        + tuple(jax.ShapeDtypeStruct(a.shape, a.dtype) for a in head_grads),
        in_specs=[chunk(3 * DN_WIDTH), chunk(3 * DN_WIDTH), chunk(DN_WIDTH), chunk(GATE_PAD)] + _dn_weight_specs() + [
            pl.BlockSpec((None, gn, DN_HEAD_DIM, DN_HEAD_DIM), lambda n: (rev(n), 0, 0, 0)),
            pl.BlockSpec((None, gn, DN_CHUNK, DN_CHUNK), lambda n: (rev(n), 0, 0, 0)),
            chunk(DN_WIDTH)] + [HBM_SPEC] * ng,
        out_specs=(chunk(3 * DN_WIDTH), chunk(DN_WIDTH), chunk(GATE_PAD), _whole((CONV_K, 3 * DN_WIDTH)),
                   _whole((1, GATE_PAD)), _whole((1, GATE_PAD)), _whole((1, DN_HEAD_DIM))) + (HBM_SPEC,) * ng,
        scratch_shapes=[pltpu.VMEM((gn, DN_HEAD_DIM, DN_HEAD_DIM), F32),
                        pltpu.VMEM((nb, DN_CHUNK + CONV_HALO, 3 * DN_WIDTH), F32),
                        pltpu.VMEM((CONV_K, 8, 3 * DN_WIDTH), F32)] + _exchange_scratch(ng),
        compiler_params=_params(1),
    )(qkv, conv_out, zg, logits, conv_w, alog, dtb, og, states, inverses, d_out, *head_grads)


def _head(a_out, b_out, x2, p2, target, w_out, w_gate, w_proj, ple_g, fin_g):
    t = x2.shape[0]
    tm = min(512, t)
    steps = t // tm

    def body(a_ref, b_ref, x_ref, p_ref, y_ref, wo_ref, wg_ref, wp_ref, pg_ref, fg_ref,
             da_ref, db_ref, dh_ref, dwo_hbm, dwg_hbm, dwp_hbm, dpg_ref, dfg_ref, loss_ref,
             dwo_acc, dwg_acc, dwp_acc, rows_stage, cols_stage):
        i = pl.program_id(0)

        @pl.when(i == 0)
        def _():
            dwo_acc[...] = jnp.zeros_like(dwo_acc)
            dwg_acc[...] = jnp.zeros_like(dwg_acc)
            dwp_acc[...] = jnp.zeros_like(dwp_acc)
            dpg_ref[...] = jnp.zeros_like(dpg_ref)
            dfg_ref[...] = jnp.zeros_like(dfg_ref)
            loss_ref[...] = jnp.zeros_like(loss_ref)

        pg = pg_ref[...]
        fg = fg_ref[...]
        nt = (((1,), (1,)), ((), ()))
        tn = (((0,), (0,)), ((), ()))

        def to_first_norm(rows):
            h1 = (x_ref[rows, :] + jnp.dot(a_ref[rows, :], wo_ref[0:SGU_WIDTH, :], preferred_element_type=F32)
                  + jnp.dot(b_ref[rows, :], wo_ref[SGU_WIDTH:, :], preferred_element_type=F32))
            n1, r1 = _rms(h1)
            pp = jnp.dot(p_ref[rows, :].astype(BF16), wp_ref[...], preferred_element_type=F32)
            return h1, n1, r1, (n1 * pg).astype(BF16), pp

        def to_gate_cotangents(rows, h1, rn, pp):
            gate = _sigmoid(jnp.dot(rn, wg_ref[...], preferred_element_type=F32))
            h2 = h1 + gate * pp
            n2, r2 = _rms(h2)
            err = n2 * fg - y_ref[rows, :]
            loss = _rowsum(jnp.sum(err * err, axis=-1, keepdims=True))
            dy = err * (1.0 / D_MODEL)
            dh2 = _rms_bwd(dy * fg, n2, r2)
            return loss, _rowsum(dy * n2), dh2, (dh2 * gate).astype(BF16), (dh2 * pp * gate * (1.0 - gate)).astype(BF16)

        def to_branch_cotangents(rows, dgl, dh2, n1, r1):
            drn = lax.dot_general(dgl, wg_ref[...], nt, preferred_element_type=F32)
            dh1 = dh2 + _rms_bwd(drn * pg, n1, r1)
            dh_ref[rows, :] = dh1
            dhb = dh1.astype(BF16)
            da_ref[rows, :] = lax.dot_general(dhb, wo_ref[0:SGU_WIDTH, :], nt, preferred_element_type=F32)
            db_ref[rows, :] = lax.dot_general(dhb, wo_ref[SGU_WIDTH:, :], nt, preferred_element_type=F32)
            return _rowsum(drn * n1), dhb

        parts = [pl.ds(k * (tm // 2), tm // 2) for k in range(2)]
        first = [to_first_norm(rows) for rows in parts]
        mid = [to_gate_cotangents(rows, h1, rn, pp) for rows, (h1, _, _, rn, pp) in zip(parts, first)]
        loss_ref[...] += jnp.broadcast_to(mid[0][0] + mid[1][0], loss_ref.shape)
        dfg_ref[...] += mid[0][1] + mid[1][1]
        last = [to_branch_cotangents(rows, m[4], m[2], f[1], f[2]) for rows, m, f in zip(parts, mid, first)]
        rn = jnp.concatenate([f[3] for f in first], axis=0)
        dpp = jnp.concatenate([m[3] for m in mid], axis=0)
        dgl = jnp.concatenate([m[4] for m in mid], axis=0)
        dwp_acc[...] += lax.dot_general(p_ref[...].astype(BF16), dpp, tn, preferred_element_type=F32)
        dwg_acc[...] += lax.dot_general(rn, dgl, tn, preferred_element_type=F32)
        dpg_ref[...] += last[0][0] + last[1][0]
        dhb = jnp.concatenate([l[1] for l in last], axis=0)
        dwo_acc[0:SGU_WIDTH, :] += lax.dot_general(a_ref[...], dhb, tn, preferred_element_type=F32)
        dwo_acc[SGU_WIDTH:, :] += lax.dot_general(b_ref[...], dhb, tn, preferred_element_type=F32)

        @pl.when(i == steps - 1)
        def _():
            for j in range(N_DEV):
                for acc, hbm in ((dwo_acc, dwo_hbm), (dwg_acc, dwg_hbm)):
                    rows_stage[...] = acc[j * LANES:(j + 1) * LANES, :].astype(BF16)
                    pltpu.sync_copy(rows_stage, hbm.at[j])
                cols_stage[...] = dwp_acc[:, j * LANES:(j + 1) * LANES].astype(BF16)
                pltpu.sync_copy(cols_stage, dwp_hbm.at[j])

    tile = lambda w: pl.BlockSpec((tm, w), lambda i: (i, 0))
    return pl.pallas_call(
        body, name="head_fwd_bwd", grid=(steps,),
        out_shape=(jax.ShapeDtypeStruct((t, SGU_WIDTH), F32), jax.ShapeDtypeStruct((t, DN_WIDTH), F32),
                   jax.ShapeDtypeStruct((t, D_MODEL), F32), jax.ShapeDtypeStruct((N_DEV, LANES, D_MODEL), BF16),
                   jax.ShapeDtypeStruct((N_DEV, LANES, D_MODEL), BF16), jax.ShapeDtypeStruct((N_DEV, PLE_DIM, LANES), BF16),
                   jax.ShapeDtypeStruct((1, D_MODEL), F32), jax.ShapeDtypeStruct((1, D_MODEL), F32),
                   jax.ShapeDtypeStruct((8, LANES), F32)),
        in_specs=[tile(SGU_WIDTH), tile(DN_WIDTH), tile(D_MODEL), tile(PLE_DIM), tile(D_MODEL),
                  VMEM_SPEC, VMEM_SPEC, VMEM_SPEC, _whole((1, D_MODEL)), _whole((1, D_MODEL))],
        out_specs=(tile(SGU_WIDTH), tile(DN_WIDTH), tile(D_MODEL), HBM_SPEC, HBM_SPEC, HBM_SPEC,
                   _whole((1, D_MODEL)), _whole((1, D_MODEL)), _whole((8, LANES))),
        scratch_shapes=[pltpu.VMEM((D_MODEL, D_MODEL), F32), pltpu.VMEM((D_MODEL, D_MODEL), F32),
                        pltpu.VMEM((PLE_DIM, D_MODEL), F32), pltpu.VMEM((LANES, D_MODEL), BF16),
                        pltpu.VMEM((PLE_DIM, LANES), BF16)],
        compiler_params=_params(1),
    )(a_out, b_out, x2, p2, target, w_out, w_gate, w_proj, ple_g, fin_g)


def _inproj_bwd(x2, dh1, a_uvz, d_sgu, d_q, d_z, d_l, norm_g, sgu_weights, wt, wgt):
    t = x2.shape[0]
    tm = min(256, t)
    steps = t // tm

    widths = (a_uvz.shape[1], d_q.shape[1], d_z.shape[1], d_l.shape[1])
    starts = (0, widths[0], widths[0] + widths[1], widths[0] + widths[1] + widths[2])

    def body(x_ref, dh_ref, uvz_ref, dsgu_ref, dq_ref, dz_ref, dl_ref, g_ref, lg_ref, lb_ref, ws_ref, bt_ref,
             wt_ref, wgt_ref,
             dx_ref, dw_hbm, dg_ref, dlg_ref, dlb_ref, dws_ref, dbt_ref, dw_acc, stage_ref, da_ref):
        i = pl.program_id(0)

        @pl.when(i == 0)
        def _():
            dw_acc[...] = jnp.zeros_like(dw_acc)
            for ref in (dg_ref, dlg_ref, dlb_ref, dws_ref, dbt_ref):
                ref[...] = jnp.zeros_like(ref)

        g = g_ref[...]
        n, r = _rms(x_ref[...])
        xn = (n * g).astype(BF16)
        dxn = None
        sgu_done = 0

        def sgu_pieces(count):
            nonlocal sgu_done
            _sgu_bwd_tile(uvz_ref, dsgu_ref, (lg_ref, lb_ref, ws_ref, bt_ref), da_ref,
                          (dlg_ref, dlb_ref, dws_ref, dbt_ref), range(sgu_done, sgu_done + count))
            sgu_done += count

        sgu_total = tm // SGU_CHUNK * SGU_GROUPS
        before_q, after_q_chunk = sgu_total // 2, (sgu_total // 4, sgu_total // 8, sgu_total // 8)
        for d_ref, col0 in reversed(tuple(zip((da_ref, dq_ref, dz_ref, dl_ref), starts))):
            if d_ref is dq_ref:
                sgu_pieces(before_q)
            if d_ref is da_ref:
                sgu_pieces(sgu_total - sgu_done)
            width = d_ref.shape[1]
            rows = wgt_ref[...] if d_ref is dl_ref else wt_ref[col0:col0 + width, :]
            term = jnp.dot(d_ref[...], rows, preferred_element_type=F32)
            dxn = term if dxn is None else dxn + term
            for c0 in range(0, width, 512):
                c1 = min(c0 + 512, width)
                dw_acc[col0 + c0:col0 + c1, :] += lax.dot_general(d_ref[:, c0:c1], xn, (((0,), (0,)), ((), ())),
                                                                  preferred_element_type=F32)
                if d_ref is dq_ref:
                    sgu_pieces(after_q_chunk[c0 // 512])
        dg_ref[...] += _rowsum(dxn * n)
        dx_ref[...] = dh_ref[...] + _rms_bwd(dxn * g, n, r)

        @pl.when(i == steps - 1)
        def _():
            for j in range(N_DEV):
                stage_ref[...] = dw_acc[j * IN_SHARD:(j + 1) * IN_SHARD, :]
                pltpu.sync_copy(stage_ref, dw_hbm.at[j])

    tile = lambda w: pl.BlockSpec((tm, w), lambda i: (i, 0))
    sgu_shapes = ((1, SGU_WIDTH), (1, SGU_WIDTH), (SGU_GROUPS, SGU_CHUNK, SGU_CHUNK), (SGU_CHUNK, SGU_GROUPS))
    return pl.pallas_call(
        body, name="inproj_sgu_bwd", grid=(steps,),
        out_shape=(jax.ShapeDtypeStruct((t, D_MODEL), F32), jax.ShapeDtypeStruct((N_DEV, IN_SHARD, D_MODEL), F32),
                   jax.ShapeDtypeStruct((1, D_MODEL), F32)) + tuple(jax.ShapeDtypeStruct(s, F32) for s in sgu_shapes),
        in_specs=[tile(D_MODEL), tile(D_MODEL), tile(widths[0]), tile(SGU_WIDTH)] + [tile(w) for w in widths[1:]]
        + [_whole((1, D_MODEL))] + [_whole(s) for s in sgu_shapes] + [VMEM_SPEC] * 2,
        out_specs=(tile(D_MODEL), HBM_SPEC, _whole((1, D_MODEL))) + tuple(_whole(s) for s in sgu_shapes),
        scratch_shapes=[pltpu.VMEM((sum(widths), D_MODEL), F32), pltpu.VMEM((IN_SHARD, D_MODEL), F32),
                        pltpu.VMEM((tm, widths[0]), BF16)],
        compiler_params=_params(1),
    )(x2, dh1, a_uvz, d_sgu, d_q, d_z, d_l, norm_g, *sgu_weights, wt, wgt)


def _reduce_adamw(recv, w, m, v, name, col_block=None):
    n, rows, cols = recv.shape
    cb = col_block or cols
    lead = w.ndim - 2

    def body(r_ref, w_ref, m_ref, v_ref, g_ref, d_ref, nm_ref, nv_ref):
        g = r_ref[0].astype(F32)
        for i in range(1, n):
            g = g + r_ref[i].astype(F32)
        m_new = ADAM_B1 * m_ref[...] + (1.0 - ADAM_B1) * g
        v_new = ADAM_B2 * v_ref[...] + (1.0 - ADAM_B2) * jnp.square(g)
        m_hat = m_new / (1.0 - ADAM_B1 ** ADAM_STEP)
        v_hat = v_new / (1.0 - ADAM_B2 ** ADAM_STEP)
        g_ref[...] = g
        d_ref[...] = -ADAM_LR * (m_hat / (jnp.sqrt(v_hat) + ADAM_EPS) + ADAM_WD * w_ref[...])
        nm_ref[...] = m_new
        nv_ref[...] = v_new

    blk = pl.BlockSpec((None,) * lead + (rows, cb), lambda i: (0,) * lead + (0, i))
    return pl.pallas_call(
        body, name=name, grid=(cols // cb,),
        out_shape=tuple(jax.ShapeDtypeStruct(w.shape, F32) for _ in range(4)),
        in_specs=[pl.BlockSpec((n, rows, cb), lambda i: (0, 0, i)), blk, blk, blk],
        out_specs=(blk, blk, blk, blk),
        compiler_params=_params(1),
    )(recv, w, m, v)


def _adamw_replicated(received, ws, ms, vs):
    nw = len(ws)
    starts = [sum(SMALL_PIECE_ROWS[:i]) for i in range(len(SMALL_PIECE_ROWS))]

    def natural(g_ref, row0, shape):
        cols, rows = shape[-1], _size(shape[:-1])
        if cols == LANES:
            return g_ref[row0:row0 + rows, :].reshape(shape)
        if cols < LANES:
            return g_ref[row0:row0 + 1, 0:cols].reshape(shape)
        per = cols // LANES
        return jnp.concatenate(
            [jnp.concatenate([g_ref[row0 + r * per + k:row0 + r * per + k + 1, :] for k in range(per)], axis=1)
             for r in range(rows)], axis=0).reshape(shape)

    def body(r_ref, *refs):
        w_refs, m_refs, v_refs = refs[:nw], refs[nw:2 * nw], refs[2 * nw:3 * nw]
        conv_ref, loss_ref = refs[3 * nw], refs[3 * nw + 1]
        out_refs, g_ref = refs[3 * nw + 2:-1], refs[-1]
        g = r_ref[0]
        for q in range(1, N_CHIPS):
            g = g + r_ref[q]
        g_ref[...] = g
        conv_ref[...] = natural(g_ref, starts[0], (CONV_K, 3 * DN_WIDTH))
        loss_ref[...] = natural(g_ref, starts[-1], (1, 1))
        for i in range(nw):
            gi = natural(g_ref, starts[1 + i], w_refs[i].shape)
            m_new = ADAM_B1 * m_refs[i][...] + (1.0 - ADAM_B1) * gi
            v_new = ADAM_B2 * v_refs[i][...] + (1.0 - ADAM_B2) * jnp.square(gi)
            m_hat = m_new / (1.0 - ADAM_B1 ** ADAM_STEP)
            v_hat = v_new / (1.0 - ADAM_B2 ** ADAM_STEP)
            out_refs[4 * i][...] = gi
            out_refs[4 * i + 1][...] = -ADAM_LR * (m_hat / (jnp.sqrt(v_hat) + ADAM_EPS) + ADAM_WD * w_refs[i][...])
            out_refs[4 * i + 2][...] = m_new
            out_refs[4 * i + 3][...] = v_new

    def spec(a):
        lead = max(a.ndim - 3, 0)
        return pl.BlockSpec((None,) * lead + a.shape[lead:], lambda: (0,) * a.ndim)

    weight_specs = [spec(a) for a in ws]
    return pl.pallas_call(
        body, name="adamw_replicated",
        out_shape=(jax.ShapeDtypeStruct((CONV_K, 3 * DN_WIDTH), F32), jax.ShapeDtypeStruct((1, 1), F32))
        + tuple(jax.ShapeDtypeStruct(a.shape, F32) for a in ws for _ in range(4)),
        in_specs=[pl.BlockSpec(received.shape, lambda: (0, 0, 0))] + weight_specs * 3,
        out_specs=(pl.BlockSpec((CONV_K, 3 * DN_WIDTH), lambda: (0, 0)), pl.BlockSpec((1, 1), lambda: (0, 0)))
        + tuple(s for s in weight_specs for _ in range(4)),
        scratch_shapes=[pltpu.VMEM(received.shape[1:], F32)],
        compiler_params=pltpu.CompilerParams(vmem_limit_bytes=VMEM_LIMIT),
    )(received, *ws, *ms, *vs)


def _pack_rows(pieces, rows):
    padded = [jnp.pad(jnp.ravel(p), (0, -p.size % LANES)) for p in pieces]
    flat = jnp.concatenate(padded)
    return jnp.pad(flat, (0, rows * LANES - flat.shape[0])).reshape(rows, LANES)


def kernel(x, p, norm_g, w_in, sgu_ln_g, sgu_ln_b, sgu_w_s, sgu_b_s, dn_conv_w, dn_a_log, dn_dt_bias, dn_o_norm_g, w_out, ple_norm_g, ple_gate_w, ple_proj_w, final_norm_g, loss_target, m_norm_g, m_w_in, m_sgu_ln_g, m_sgu_ln_b, m_sgu_w_s, m_sgu_b_s, m_dn_conv_w, m_dn_a_log, m_dn_dt_bias, m_dn_o_norm_g, m_w_out, m_ple_norm_g, m_ple_gate_w, m_ple_proj_w, m_final_norm_g, v_norm_g, v_w_in, v_sgu_ln_g, v_sgu_ln_b, v_sgu_w_s, v_sgu_b_s, v_dn_conv_w, v_dn_a_log, v_dn_dt_bias, v_dn_o_norm_g, v_w_out, v_ple_norm_g, v_ple_gate_w, v_ple_proj_w, v_final_norm_g):
    weights = dict(norm_g=norm_g, w_in=w_in, sgu_ln_g=sgu_ln_g, sgu_ln_b=sgu_ln_b, sgu_w_s=sgu_w_s, sgu_b_s=sgu_b_s,
                   dn_conv_w=dn_conv_w, dn_a_log=dn_a_log, dn_dt_bias=dn_dt_bias, dn_o_norm_g=dn_o_norm_g, w_out=w_out,
                   ple_norm_g=ple_norm_g, ple_gate_w=ple_gate_w, ple_proj_w=ple_proj_w, final_norm_g=final_norm_g)
    mom1 = dict(norm_g=m_norm_g, w_in=m_w_in, sgu_ln_g=m_sgu_ln_g, sgu_ln_b=m_sgu_ln_b, sgu_w_s=m_sgu_w_s,
                sgu_b_s=m_sgu_b_s, dn_conv_w=m_dn_conv_w, dn_a_log=m_dn_a_log, dn_dt_bias=m_dn_dt_bias,
                dn_o_norm_g=m_dn_o_norm_g, w_out=m_w_out, ple_norm_g=m_ple_norm_g, ple_gate_w=m_ple_gate_w,
                ple_proj_w=m_ple_proj_w, final_norm_g=m_final_norm_g)
    mom2 = dict(norm_g=v_norm_g, w_in=v_w_in, sgu_ln_g=v_sgu_ln_g, sgu_ln_b=v_sgu_ln_b, sgu_w_s=v_sgu_w_s,
                sgu_b_s=v_sgu_b_s, dn_conv_w=v_dn_conv_w, dn_a_log=v_dn_a_log, dn_dt_bias=v_dn_dt_bias,
                dn_o_norm_g=v_dn_o_norm_g, w_out=v_w_out, ple_norm_g=v_ple_norm_g, ple_gate_w=v_ple_gate_w,
                ple_proj_w=v_ple_proj_w, final_norm_g=v_final_norm_g)
    nb, s, _ = x.shape
    t = nb * s

    transposed = lambda a: jnp.transpose(a, (2, 0, 1)).reshape(IN_SHARD, D_MODEL)
    w_in_t, m_in_t, v_in_t = transposed(w_in), transposed(m_w_in), transposed(v_w_in)
    w_in_blocks, conv_blocks = _all_gather([w_in_t.astype(BF16), dn_conv_w[0]])
    w_in_full_t = w_in_blocks.reshape(IN_COLS, D_MODEL)
    wgt = jnp.pad(w_in_full_t[sum(IN_GROUPS):], ((0, GATE_PAD - 2 * DN_HEADS), (0, 0)))
    conv_full = jnp.moveaxis(conv_blocks, 0, 1).reshape(CONV_K, 3 * DN_WIDTH)
    later_shards = [w_out[0].astype(BF16), ple_gate_w[0].astype(BF16), ple_proj_w[0].astype(BF16)]

    pad_row = lambda a: jnp.pad(a.reshape(1, -1), ((0, 0), (DN_HEADS, GATE_PAD - DN_HEADS - a.size)))
    alog, dtb = pad_row(dn_a_log), pad_row(dn_dt_bias)
    og = dn_o_norm_g.reshape(1, DN_HEAD_DIM)
    ws = sgu_w_s.reshape(SGU_GROUPS, SGU_CHUNK, SGU_CHUNK)
    b_t = sgu_b_s.reshape(SGU_GROUPS, SGU_CHUNK).T
    fin_g = final_norm_g.reshape(1, D_MODEL)

    x2 = x.reshape(t, D_MODEL)
    sgu_weights = (sgu_ln_g, sgu_ln_b, ws, b_t)
    a_uvz, b_qkv, b_z, b_l, a_out, conv_out, w_out_blocks, w_gate_blocks, w_proj_blocks = _inproj_fwd(
        x2, s, norm_g, w_in_full_t, wgt, sgu_weights, conv_full, later_shards)
    w_out_full = w_out_blocks.reshape(D_MODEL, D_MODEL)
    w_gate_full = w_gate_blocks.reshape(D_MODEL, D_MODEL)
    w_proj_full = jnp.moveaxis(w_proj_blocks, 0, 1).reshape(PLE_DIM, D_MODEL)
    qkv3 = b_qkv.reshape(nb, s, 3 * DN_WIDTH)
    conv_out = conv_out.reshape(nb, s, 3 * DN_WIDTH)
    z3 = b_z.reshape(nb, s, DN_WIDTH)
    l3 = b_l.reshape(nb, s, GATE_PAD)
    b_out, states, inverses = _dn_fwd(conv_out, z3, l3, alog, dtb, og)

    d_a, d_b, dh1, g_w_out, g_gate, g_proj, g_ple_g, g_fin_g, loss_tile = _head(
        a_out, b_out.reshape(t, DN_WIDTH), x2, p.reshape(t, PLE_DIM), loss_target.reshape(t, D_MODEL),
        w_out_full, w_gate_full, w_proj_full, ple_norm_g, fin_g)
    d_qkv, d_z, d_l, g_conv, g_alog, g_dtb, g_og, *head_received = _dn_bwd(
        qkv3, conv_out, z3, l3, conv_full, alog, dtb, og, states, inverses, d_b.reshape(nb, s, DN_WIDTH),
        [g_w_out, g_gate, g_proj])
    grad_x, g_w_in, g_norm, g_ln_g, g_ln_b, g_ws, g_bt = _inproj_bwd(
        x2, dh1, a_uvz, d_a, d_qkv.reshape(t, 3 * DN_WIDTH), d_z.reshape(t, DN_WIDTH), d_l.reshape(t, GATE_PAD),
        norm_g, sgu_weights, w_in_full_t, wgt)

    small = _pack_rows([g_conv, g_norm, g_ln_g, g_ln_b, g_ws, g_bt.T, g_alog[:, DN_HEADS:2 * DN_HEADS], g_dtb[:, DN_HEADS:2 * DN_HEADS], g_og,
                        g_ple_g, g_fin_g, (0.5 / D_MODEL) * loss_tile[0:1, 0:1]], SMALL_ROWS)
    w_in_received, small_received = _reduce_exchange(g_w_in, small)

    results = {}
    outs = _reduce_adamw(w_in_received, w_in_t, m_in_t, v_in_t, "adamw_w_in", 4 * LANES)
    results["w_in"] = [jnp.transpose(a.reshape(IN_SHARD, 1, D_MODEL), (1, 2, 0)) for a in outs]
    for name, recv in zip(("w_out", "ple_gate_w", "ple_proj_w"), head_received):
        results[name] = _reduce_adamw(recv, weights[name], mom1[name], mom2[name], "adamw_" + name)
    names = [name for name, _ in REPLICATED]
    two_d = lambda a: a.reshape(1, -1) if a.ndim == 1 else a
    g_conv_sum, loss_sum, *flat_outs = _adamw_replicated(
        small_received, *[[two_d(src[k]) for k in names] for src in (weights, mom1, mom2)])
    for i, k in enumerate(names):
        results[k] = [a.reshape(weights[k].shape) for a in flat_outs[4 * i:4 * i + 4]]
    loss = loss_sum[0, 0]
    me = 4 * lax.axis_index("x") + 2 * lax.axis_index("y") + lax.axis_index("c")
    conv_mine = lax.dynamic_slice(g_conv_sum, (0, me * 192), (CONV_K, 192))
    results["dn_conv_w"] = _reduce_adamw(conv_mine[None], dn_conv_w, m_dn_conv_w, v_dn_conv_w, "adamw_dn_conv_w")

    return (loss, grad_x.reshape(nb, s, D_MODEL), *[results[k][0] for k in WEIGHT_ORDER],
            *[results[k][1] for k in WEIGHT_ORDER], *[results[k][2] for k in WEIGHT_ORDER],
            *[results[k][3] for k in WEIGHT_ORDER])
```

```python
import functools

import jax
import jax.numpy as jnp
from jax import lax
from jax.experimental import pallas as pl
from jax.experimental.pallas import tpu as pltpu

F32 = jnp.float32
BF16 = jnp.bfloat16

N_DEV = 8
D_MODEL = 1024
SGU_WIDTH = 512
SGU_GROUPS = 4
SGU_CHUNK = 128
DN_WIDTH = 512
DN_HEADS = 4
DN_HEAD_DIM = 128
DN_CHUNK = 128
CONV_K = 4
CONV_HALO = 8
PLE_DIM = 256
EPS = 1e-6
IN_COLS = 3592
IN_SHARD = IN_COLS // N_DEV
GATE_PAD = 128
IN_GROUPS = (3 * SGU_WIDTH, 3 * DN_WIDTH, DN_WIDTH)

ADAM_LR = 0.001
ADAM_B1 = 0.9
ADAM_B2 = 0.999
ADAM_EPS = 1e-08
ADAM_WD = 0.01
ADAM_STEP = 10

LANES = 128
VMEM_LIMIT = 56 * 1024 * 1024
MESH = pl.DeviceIdType.MESH

REPLICATED = (("norm_g", (1, D_MODEL)), ("sgu_ln_g", (1, SGU_WIDTH)), ("sgu_ln_b", (1, SGU_WIDTH)),
              ("sgu_w_s", (1, SGU_GROUPS, SGU_CHUNK, SGU_CHUNK)), ("sgu_b_s", (1, SGU_GROUPS, SGU_CHUNK)),
              ("dn_a_log", (1, DN_HEADS)), ("dn_dt_bias", (1, DN_HEADS)), ("dn_o_norm_g", (1, DN_HEAD_DIM)),
              ("ple_norm_g", (1, D_MODEL)), ("final_norm_g", (D_MODEL,)))
WEIGHT_ORDER = ("norm_g", "w_in", "sgu_ln_g", "sgu_ln_b", "sgu_w_s", "sgu_b_s", "dn_conv_w", "dn_a_log",
                "dn_dt_bias", "dn_o_norm_g", "w_out", "ple_norm_g", "ple_gate_w", "ple_proj_w", "final_norm_g")


def _size(shape):
    n = 1
    for s in shape:
        n *= s
    return n


SMALL_LAYOUT = (("conv", (CONV_K, 3 * DN_WIDTH)),) + REPLICATED + (("loss", (1,)),)
SMALL_PIECE_ROWS = tuple(-(-_size(s) // LANES) for _, s in SMALL_LAYOUT)
SMALL_ROWS = -(-sum(SMALL_PIECE_ROWS) // 8) * 8


def _bdot(a, b):
    return jnp.dot(a.astype(BF16), b.astype(BF16), preferred_element_type=F32)


def _sigmoid(x):
    return 0.5 * jnp.tanh(0.5 * x) + 0.5


@jax.custom_vjp
def _silu(x):
    return x * _sigmoid(x)


def _silu_fwd(x):
    s = _sigmoid(x)
    return x * s, (x, s)


def _silu_bwd(res, ct):
    x, s = res
    return (ct * (s * (1.0 + x * (1.0 - s))),)


_silu.defvjp(_silu_fwd, _silu_bwd)


def _normal_cdf(x):
    return 0.5 + 0.5 * lax.erf(x * (0.5 ** 0.5))


@jax.custom_vjp
def _gelu(x):
    return x * _normal_cdf(x)


def _gelu_fwd(x):
    cdf = _normal_cdf(x)
    return x * cdf, (x, cdf)


def _gelu_bwd(res, ct):
    x, cdf = res
    pdf = jnp.exp(-0.5 * x * x) * ((2.0 * jnp.pi) ** -0.5)
    return (ct * (cdf + x * pdf),)


_gelu.defvjp(_gelu_fwd, _gelu_bwd)


def _softplus(x):
    return jnp.maximum(x, 0.0) + jnp.log1p(jnp.exp(-jnp.abs(x)))


@jax.custom_vjp
def _l2n(x):
    return x * lax.rsqrt(jnp.sum(x * x, axis=-1, keepdims=True) + EPS)


def _l2n_fwd(x):
    r = lax.rsqrt(jnp.sum(x * x, axis=-1, keepdims=True) + EPS)
    n = x * r
    return n, (n, r)


def _l2n_bwd(res, ct):
    n, r = res
    return (r * (ct - n * jnp.sum(ct * n, axis=-1, keepdims=True)),)


_l2n.defvjp(_l2n_fwd, _l2n_bwd)


def _rms(x):
    r = lax.rsqrt(jnp.mean(x * x, axis=-1, keepdims=True) + EPS)
    return x * r, r


def _rms_bwd(dn, n, r):
    return r * (dn - n * jnp.mean(dn * n, axis=-1, keepdims=True))


@jax.custom_vjp
def _rms_normed(x):
    return _rms(x)[0]


def _rms_normed_fwd(x):
    n, r = _rms(x)
    return n, (n, r)


def _rms_normed_bwd(res, ct):
    return (_rms_bwd(ct, *res),)


_rms_normed.defvjp(_rms_normed_fwd, _rms_normed_bwd)


def _onehot_row(idx, width):
    return (lax.broadcasted_iota(jnp.int32, (1, width), 1) == idx).astype(F32)


def _rowsum(x):
    return jnp.sum(x, axis=0, keepdims=True)


def _iota2(n):
    return lax.broadcasted_iota(jnp.int32, (n, n), 0), lax.broadcasted_iota(jnp.int32, (n, n), 1)


def _bmm(a, b):
    return lax.dot_general(a.astype(BF16), b.astype(BF16), (((2,), (1,)), ((0,), (0,))), preferred_element_type=F32)


def _bmm_nt(a, b):
    return lax.dot_general(a.astype(BF16), b.astype(BF16), (((2,), (2,)), ((0,), (0,))), preferred_element_type=F32)


def _bmm_tn(a, b):
    return lax.dot_general(a.astype(BF16), b.astype(BF16), (((1,), (1,)), ((0,), (0,))), preferred_element_type=F32)


def _tri_inv_impl(a):
    n = a.shape[-1]
    r, c = _iota2(n)
    x = r ^ c
    eye = (r == c).astype(F32)
    ad = jnp.where(x < 16, a, 0.0)
    p2 = _bmm(ad, ad)
    e = p2 - ad - _bmm(ad, p2)
    p4 = _bmm(p2, p2)
    e = e + p4 + _bmm(e, p4)
    p8 = _bmm(p4, p4)
    e = e + p8 + _bmm(e, p8)
    size = 16
    while size < n:
        m = jnp.where(jnp.logical_and(x < 2 * size, x >= size), a, 0.0)
        f = m + _bmm(m, e)
        e = e - f - _bmm(e, f)
        size *= 2
    return e + eye


@jax.custom_vjp
def _tri_inv(a, known):
    return _tri_inv_impl(a) if known is None else known


def _tri_inv_fwd(a, known):
    t = _tri_inv(a, known)
    return t, (t, known)


def _tri_inv_bwd(res, dt):
    t, known = res
    return -_bmm_tn(t, _bmm_nt(dt, t)), None if known is None else jnp.zeros_like(known)


_tri_inv.defvjp(_tri_inv_fwd, _tri_inv_bwd)


@jax.custom_vjp
def _standardized(x):
    xc = x - jnp.mean(x, axis=-1, keepdims=True)
    return xc * lax.rsqrt(jnp.mean(xc * xc, axis=-1, keepdims=True) + EPS)


def _standardized_fwd(x):
    xc = x - jnp.mean(x, axis=-1, keepdims=True)
    rstd = lax.rsqrt(jnp.mean(xc * xc, axis=-1, keepdims=True) + EPS)
    y = xc * rstd
    return y, (y, rstd)


def _standardized_bwd(res, ct):
    y, rstd = res
    return (rstd * (ct - jnp.mean(ct, axis=-1, keepdims=True) - y * jnp.mean(ct * y, axis=-1, keepdims=True)),)


_standardized.defvjp(_standardized_fwd, _standardized_bwd)


def _sgu_core(u, v, z, lg, lb, ws, bcol):
    n = ws.shape[0]
    r, c = _iota2(n)
    wm = jnp.where(r >= c, ws, 0.0)
    gu = _gelu(u)
    gv = _gelu(v)
    ln = _standardized(gv) * lg + lb
    s = _bdot(wm, ln) + bcol
    return gu * s * _silu(z)


def _lanes_of(x):
    return jnp.concatenate([x[i] for i in range(x.shape[0])], axis=1)


def _batch_of(x, width):
    return jnp.concatenate([x[None, :, i * width:(i + 1) * width] for i in range(x.shape[1] // width)], axis=0)


def _mask_dot(mask, x):
    hi = x.astype(BF16)
    lo = (x - hi.astype(F32)).astype(BF16)
    m = mask.astype(BF16)
    return jnp.dot(m, hi, preferred_element_type=F32) + jnp.dot(m, lo, preferred_element_type=F32)


def _split_dot(x, mask, dims):
    hi = x.astype(BF16)
    lo = (x - hi.astype(F32)).astype(BF16)
    m = mask.astype(BF16)
    return (lax.dot_general(hi, m, dims, preferred_element_type=F32)
            + lax.dot_general(lo, m, dims, preferred_element_type=F32))


def _lane_select(lanes, blocks, first_lane):
    src = lax.broadcasted_iota(jnp.int32, (lanes, blocks * LANES), 0)
    dst = lax.broadcasted_iota(jnp.int32, (lanes, blocks * LANES), 1) // LANES
    return src == dst + first_lane


def _pick_lanes(x, blocks, first_lane):
    return _pick_lanes_vjp(blocks, first_lane, x)


@functools.partial(jax.custom_vjp, nondiff_argnums=(0, 1))
def _pick_lanes_vjp(blocks, first_lane, x):
    return _split_dot(x, _lane_select(x.shape[-1], blocks, first_lane), (((1,), (0,)), ((), ())))


def _pick_lanes_fwd(blocks, first_lane, x):
    return _pick_lanes_vjp(blocks, first_lane, x), x.shape[-1]


def _pick_lanes_bwd(blocks, first_lane, lanes, ct):
    return (sum(jnp.sum(ct[:, h * LANES:(h + 1) * LANES], axis=-1, keepdims=True) * _onehot_row(first_lane + h, lanes)
                for h in range(blocks)),)


_pick_lanes_vjp.defvjp(_pick_lanes_fwd, _pick_lanes_bwd)


def _tri_mask(n, upper):
    r, c = _iota2(n)
    return (r <= c) if upper else (r >= c)


@jax.custom_vjp
def _cumsum_rows(x):
    return _mask_dot(_tri_mask(x.shape[0], False), x)


def _cumsum_rows_fwd(x):
    return _cumsum_rows(x), None


def _cumsum_rows_bwd(_, ct):
    return (_mask_dot(_tri_mask(ct.shape[0], True), ct),)


_cumsum_rows.defvjp(_cumsum_rows_fwd, _cumsum_rows_bwd)


@jax.custom_vjp
def _colsum_all_rows(x):
    return _mask_dot(jnp.ones((x.shape[0], x.shape[0]), jnp.bool_), x)


def _colsum_all_rows_fwd(x):
    return _colsum_all_rows(x), None


def _colsum_all_rows_bwd(_, ct):
    return (_mask_dot(jnp.ones((ct.shape[0], ct.shape[0]), jnp.bool_), ct),)


_colsum_all_rows.defvjp(_colsum_all_rows_fwd, _colsum_all_rows_bwd)


def _dn_core(cq, ck, cv, z, logits, state, alog, dtb, og, t_known=None):
    gn, cn, dh = cq.shape
    heads = gn // logits.shape[0]
    q = _l2n(_silu(cq)) * (dh ** -0.5)
    k = _l2n(_silu(ck))
    v = _silu(cv)
    beta_lanes = _sigmoid(logits)
    g_lanes = -jnp.exp(alog) * _softplus(logits + dtb)
    beta_all = jnp.concatenate([_pick_lanes(beta_lanes[b], heads, 0) for b in range(logits.shape[0])], axis=1)
    g_all = jnp.concatenate([_pick_lanes(g_lanes[b], heads, heads) for b in range(logits.shape[0])], axis=1)
    beta = _batch_of(beta_all, dh)
    g_wide = _batch_of(g_all, dh)
    r, c = _iota2(cn)
    tril = r >= c
    rw = lax.broadcasted_iota(jnp.int32, (cn, dh), 0)
    cw = lax.broadcasted_iota(jnp.int32, (cn, dh), 1)
    upper_wide = (rw <= cw).astype(F32)
    gc_wide = _batch_of(_cumsum_rows(g_all), dh)
    gc_cols = _batch_of(_colsum_all_rows(_lanes_of(g_wide * upper_wide)), dh)[:, :, :cn]
    decay = jnp.exp(jnp.where(tril, gc_wide[:, :, :cn] - gc_cols, -1e30))
    kb = k * beta
    kk = _bmm_nt(kb, k) * decay
    t = _tri_inv(jnp.where(r > c, kk, 0.0), t_known)
    eg = jnp.exp(gc_wide)
    sol = _bmm(t, jnp.concatenate([v * beta, kb * eg], axis=-1))
    u_val, w_dec = sol[:, :, :dh], sol[:, :, dh:]
    qk = _bmm_nt(q, k) * decay
    g_last = jnp.sum(g_wide, axis=1, keepdims=True)
    k_dec = k * jnp.exp(g_last - gc_wide)
    ws = _bmm(jnp.concatenate([w_dec, q * eg], axis=1), state)
    v_new = u_val - ws[:, :cn]
    o = ws[:, cn:] + _bmm(qk, v_new)
    new_state = state * jnp.exp(g_last) + _bmm_tn(k_dec, v_new)
    return _rms_normed(o) * og * _silu(z), new_state, t


N_CHIPS = 4
HBM_SPEC = pl.BlockSpec(memory_space=pl.ANY)


def _place():
    return lax.axis_index("x"), lax.axis_index("y"), lax.axis_index("c")


def _other_chip(k):
    x, y, _ = _place()
    px = 1 - x if k & 2 else x
    py = 1 - y if k & 1 else y
    return px, py, 2 * px + py


def _remote(src, dst, send_sem, recv_sem, device):
    return pltpu.make_async_remote_copy(src_ref=src, dst_ref=dst, send_sem=send_sem, recv_sem=recv_sem,
                                        device_id=device, device_id_type=MESH)


def _other_device(k):
    x, y, c = _place()
    px = 1 - x if k & 4 else x
    py = 1 - y if k & 2 else y
    pc = 1 - c if k & 1 else c
    return (px, py, pc), 4 * px + 2 * py + pc


def _direct_exchange(srcs, outs, send_sems, recv_sems, local_sems, gather):
    x, y, c = _place()
    me = 4 * x + 2 * y + c

    def copies(arriving):
        out_list = []
        for a, (src, out) in enumerate(zip(srcs, outs)):
            for k in range(1, N_DEV):
                peer, index = _other_device(k)
                mine = src if gather else src.at[index]
                out_list.append(_remote(mine, out.at[index if arriving else me], send_sems.at[a, k - 1],
                                        recv_sems.at[a, k - 1], peer))
        return out_list

    def local_copies():
        return [pltpu.make_async_copy(src if gather else src.at[me], out.at[me], local_sems.at[a])
                for a, (src, out) in enumerate(zip(srcs, outs))]

    def start():
        for cp in local_copies() + copies(False):
            cp.start()

    def wait():
        for cp in copies(True):
            cp.wait_recv()
        for cp in copies(False):
            cp.wait_send()
        for cp in local_copies():
            cp.wait()

    return start, wait


def _exchange_scratch(n):
    return [pltpu.SemaphoreType.DMA((n, N_DEV - 1)), pltpu.SemaphoreType.DMA((n, N_DEV - 1)), pltpu.SemaphoreType.DMA((n,))]


def _all_gather(shards):
    n = len(shards)

    def body(*refs):
        srcs, outs = refs[:n], refs[n:2 * n]
        send_sems, recv_sems, local_sems = refs[2 * n:]
        x, y, c = _place()
        me = 4 * x + 2 * y + c
        sibling = (x, y, 1 - c)
        local = [pltpu.make_async_copy(srcs[a], outs[a].at[me], local_sems.at[a]) for a in range(n)]
        for cp in local:
            cp.start()
        sends = []
        for a in range(n):
            sends.append(_remote(srcs[a], outs[a].at[me], send_sems.at[a, 0], recv_sems.at[a, 0], sibling))
        for k in range(1, N_CHIPS):
            px, py, _ = _other_chip(k)
            for a in range(n):
                sends.append(_remote(srcs[a], outs[a].at[me], send_sems.at[a, k], recv_sems.at[a, k], (px, py, c)))
        for cp in sends:
            cp.start()
        passed = []
        for k in range(1, N_CHIPS):
            px, py, _ = _other_chip(k)
            blk = 4 * px + 2 * py + c
            for a in range(n):
                _remote(srcs[a], outs[a].at[blk], send_sems.at[a, k], recv_sems.at[a, k], (px, py, c)).wait_recv()
            for a in range(n):
                cp = _remote(outs[a].at[blk], outs[a].at[blk], send_sems.at[a, 3 + k], recv_sems.at[a, 3 + k], sibling)
                cp.start()
                passed.append(cp)
        for a in range(n):
            _remote(srcs[a], outs[a].at[me + 1 - 2 * c], send_sems.at[a, 0], recv_sems.at[a, 0], sibling).wait_recv()
        for k in range(1, N_CHIPS):
            px, py, _ = _other_chip(k)
            blk = 4 * px + 2 * py + 1 - c
            for a in range(n):
                _remote(srcs[a], outs[a].at[blk], send_sems.at[a, 3 + k], recv_sems.at[a, 3 + k], sibling).wait_recv()
        for cp in sends + passed:
            cp.wait_send()
        for cp in local:
            cp.wait()

    return pl.pallas_call(
        body, name="all_gather_weights",
        out_shape=tuple(jax.ShapeDtypeStruct((N_DEV,) + a.shape, a.dtype) for a in shards),
        in_specs=[HBM_SPEC] * n, out_specs=(HBM_SPEC,) * n,
        scratch_shapes=[pltpu.SemaphoreType.DMA((n, N_DEV - 1)), pltpu.SemaphoreType.DMA((n, N_DEV - 1)),
                        pltpu.SemaphoreType.DMA((n,))],
    )(*shards)


def _reduce_exchange(by_device, small):
    _, rows, cols = by_device.shape

    def body(g_ref, small_ref, out_ref, small_out_ref, from_sibling, small_from_sibling, stage, sums, small_own, small_sum,
             pair_send, pair_recv, chip_send, chip_recv, local_sems):
        x, y, c = _place()
        mine = 2 * x + y
        sibling = (x, y, 1 - c)
        chips = [(x, y, mine)] + [_other_chip(k) for k in range(1, N_CHIPS)]
        to_sibling = [_remote(g_ref.at[2 * chips[k][2] + 1 - c], from_sibling.at[k], pair_send.at[k], pair_recv.at[k], sibling)
                      for k in range(N_CHIPS)]
        to_sibling.append(_remote(small_ref, small_from_sibling, pair_send.at[N_CHIPS], pair_recv.at[N_CHIPS], sibling))
        for cp in to_sibling:
            cp.start()
        small_mine = pltpu.make_async_copy(small_ref, small_own, local_sems.at[0])
        small_mine.start()
        to_chips = []
        for k in (1, 2, 3, 0):
            px, py, chip = chips[k]
            mine_k = pltpu.make_async_copy(g_ref.at[2 * chip + c], stage, local_sems.at[1])
            mine_k.start()
            to_sibling[k].wait_recv()
            mine_k.wait()
            sums[k] = (stage[...] + from_sibling[k]).astype(sums.dtype)
            if k:
                cp = _remote(sums.at[k], out_ref.at[mine], chip_send.at[0, k - 1], chip_recv.at[0, k - 1], (px, py, c))
                cp.start()
                to_chips.append(cp)
        own_block = pltpu.make_async_copy(sums.at[0], out_ref.at[mine], local_sems.at[2])
        own_block.start()
        to_sibling[N_CHIPS].wait_recv()
        small_mine.wait()
        small_sum[...] = small_own[...] + small_from_sibling[...]
        for k in range(1, N_CHIPS):
            px, py, _ = chips[k]
            cp = _remote(small_sum, small_out_ref.at[mine], chip_send.at[1, k - 1], chip_recv.at[1, k - 1], (px, py, c))
            cp.start()
            to_chips.append(cp)
        own_small = pltpu.make_async_copy(small_sum, small_out_ref.at[mine], local_sems.at[3])
        own_small.start()
        for k in range(1, N_CHIPS):
            px, py, chip = chips[k]
            _remote(sums.at[k], out_ref.at[chip], chip_send.at[0, k - 1], chip_recv.at[0, k - 1], (px, py, c)).wait_recv()
            _remote(small_sum, small_out_ref.at[chip], chip_send.at[1, k - 1], chip_recv.at[1, k - 1], (px, py, c)).wait_recv()
        for cp in to_sibling + to_chips:
            cp.wait_send()
        own_block.wait()
        own_small.wait()

    return pl.pallas_call(
        body, name="grad_reduce_exchange",
        out_shape=(jax.ShapeDtypeStruct((N_CHIPS, rows, cols), BF16), jax.ShapeDtypeStruct((N_CHIPS,) + small.shape, F32)),
        in_specs=[HBM_SPEC, HBM_SPEC], out_specs=(HBM_SPEC, HBM_SPEC),
        scratch_shapes=[pltpu.VMEM((N_CHIPS, rows, cols), F32), pltpu.VMEM(small.shape, F32), pltpu.VMEM((rows, cols), F32),
                        pltpu.VMEM((N_CHIPS, rows, cols), BF16), pltpu.VMEM(small.shape, F32), pltpu.VMEM(small.shape, F32),
                        pltpu.SemaphoreType.DMA((N_CHIPS + 1,)), pltpu.SemaphoreType.DMA((N_CHIPS + 1,)),
                        pltpu.SemaphoreType.DMA((2, N_CHIPS - 1)), pltpu.SemaphoreType.DMA((2, N_CHIPS - 1)),
                        pltpu.SemaphoreType.DMA((4,))],
        compiler_params=pltpu.CompilerParams(vmem_limit_bytes=VMEM_LIMIT),
    )(by_device, small)


def _params(n_axes):
    return pltpu.CompilerParams(dimension_semantics=("arbitrary",) * n_axes, vmem_limit_bytes=VMEM_LIMIT)


def _whole(shape):
    return pl.BlockSpec(shape, lambda *_: (0,) * len(shape))


VMEM_SPEC = pl.BlockSpec(memory_space=pltpu.VMEM)


def _inproj_fwd(x2, seq_len, norm_g, wt, wgt, sgu_weights, conv_w, later_shards):
    t = x2.shape[0]
    tm = min(512, seq_len)
    tiles_per_seq = seq_len // tm
    steps = t // tm
    ns = len(later_shards)

    widths = IN_GROUPS + (wgt.shape[0],)
    starts = (0, IN_GROUPS[0], IN_GROUPS[0] + IN_GROUPS[1], 0)

    def body(x_ref, g_ref, wt_ref, wg_ref, lg_ref, lb_ref, ws_ref, bt_ref, cw_ref, *rest):
        shard_refs, rest = rest[:ns], rest[ns:]
        a_ref, q_ref, z_ref, l_ref, sgu_ref, c_ref = rest[:6]
        gathered_refs, (xpad_ref, send_sems, recv_sems, local_sems) = rest[6:6 + ns], rest[6 + ns:]
        start_gather, wait_gather = _direct_exchange(shard_refs, gathered_refs, send_sems, recv_sems, local_sems, True)
        pl.when(pl.program_id(0) == 0)(start_gather)

        @pl.when(pl.program_id(0) % tiles_per_seq == 0)
        def _():
            xpad_ref[0:CONV_HALO, :] = jnp.zeros((CONV_HALO, xpad_ref.shape[1]), F32)

        n, _ = _rms(x_ref[...])
        xn = (n * g_ref[...]).astype(BF16)

        def project(w_ref, row0, width, o_ref):
            for c0 in range(0, width, 512):
                c1 = min(c0 + 512, width)
                o_ref[:, c0:c1] = lax.dot_general(xn, w_ref[row0 + c0:row0 + c1, :], (((1,), (1,)), ((), ())),
                                                  preferred_element_type=F32)

        def sgu_rows(row0):
            for grp in range(SGU_GROUPS):
                args = _sgu_pieces(a_ref, lg_ref, lb_ref, ws_ref, bt_ref, row0, grp)
                sgu_ref[pl.ds(row0, SGU_CHUNK), pl.ds(grp * 128, 128)] = _sgu_core(*args).astype(sgu_ref.dtype)

        def conv():
            xpad_ref[CONV_HALO:, :] = q_ref[...]
            acc = None
            padded = xpad_ref[...]
            for j in range(CONV_K):
                rows_up = CONV_HALO - CONV_K + 1 + j
                shifted = (padded[rows_up:rows_up + tm] if rows_up % 8 == 0
                           else pltpu.roll(padded, padded.shape[0] - rows_up, axis=0)[0:tm])
                term = cw_ref[j:j + 1, :] * shifted
                acc = term if acc is None else acc + term
            c_ref[...] = acc
            xpad_ref[0:CONV_HALO, :] = xpad_ref[tm:tm + CONV_HALO, :]

        groups = tuple(zip((wt_ref, wt_ref, wt_ref, wg_ref), starts, widths, (a_ref, q_ref, z_ref, l_ref)))
        row_chunks = list(range(0, tm, SGU_CHUNK))
        project(*groups[0])
        for row0 in row_chunks[:len(row_chunks) // 2]:
            sgu_rows(row0)
        project(*groups[1])
        for row0 in row_chunks[len(row_chunks) // 2:]:
            sgu_rows(row0)
        project(*groups[3])
        conv()
        project(*groups[2])
        pl.when(pl.program_id(0) == steps - 1)(wait_gather)

    tile = lambda w: pl.BlockSpec((tm, w), lambda i: (i, 0))
    sgu_shapes = ((1, SGU_WIDTH), (1, SGU_WIDTH), (SGU_GROUPS, SGU_CHUNK, SGU_CHUNK), (SGU_CHUNK, SGU_GROUPS))
    return pl.pallas_call(
        body, name="inproj_sgu_conv_fwd", grid=(steps,),
        out_shape=tuple(jax.ShapeDtypeStruct((t, w), F32) for w in widths)
        + (jax.ShapeDtypeStruct((t, SGU_WIDTH), BF16), jax.ShapeDtypeStruct((t, widths[1]), F32))
        + tuple(jax.ShapeDtypeStruct((N_DEV,) + a.shape, a.dtype) for a in later_shards),
        in_specs=[tile(D_MODEL), _whole((1, D_MODEL)), VMEM_SPEC, VMEM_SPEC]
        + [_whole(s) for s in sgu_shapes] + [_whole((CONV_K, widths[1]))] + [HBM_SPEC] * ns,
        out_specs=tuple(tile(w) for w in widths) + (tile(SGU_WIDTH), tile(widths[1])) + (HBM_SPEC,) * ns,
        scratch_shapes=[pltpu.VMEM((CONV_HALO + tm, widths[1]), F32)] + _exchange_scratch(ns),
        compiler_params=_params(1),
    )(x2, norm_g, wt, wgt, *sgu_weights, conv_w, *later_shards)


def _sgu_pieces(uvz_ref, lg_ref, lb_ref, ws_ref, bt_ref, row0, grp):
    rows = pl.ds(row0, SGU_CHUNK)
    lanes = pl.ds(grp * 128, 128)
    u = uvz_ref[rows, pl.ds(grp * 128, 128)]
    v = uvz_ref[rows, pl.ds(SGU_WIDTH + grp * 128, 128)]
    z = uvz_ref[rows, pl.ds(2 * SGU_WIDTH + grp * 128, 128)]
    bcol = jnp.sum(bt_ref[...] * _onehot_row(grp, SGU_GROUPS), axis=-1, keepdims=True)
    return u, v, z, lg_ref[:, lanes], lb_ref[:, lanes], ws_ref[grp], bcol


def _sgu_bwd_tile(uvz_ref, do_ref, sgu_refs, duvz_ref, grad_refs, pieces):
    lg_ref, lb_ref, ws_ref, bt_ref = sgu_refs
    dlg_ref, dlb_ref, dws_ref, dbt_ref = grad_refs
    for piece in pieces:
        row0, grp = piece // SGU_GROUPS * SGU_CHUNK, piece % SGU_GROUPS
        rows = pl.ds(row0, SGU_CHUNK)
        lanes = pl.ds(grp * 128, 128)
        args = _sgu_pieces(uvz_ref, lg_ref, lb_ref, ws_ref, bt_ref, row0, grp)
        _, pull = jax.vjp(_sgu_core, *args)
        du, dv, dz, dlg, dlb, dws, dbcol = pull(do_ref[rows, lanes])
        duvz_ref[rows, pl.ds(grp * 128, 128)] = du.astype(duvz_ref.dtype)
        duvz_ref[rows, pl.ds(SGU_WIDTH + grp * 128, 128)] = dv.astype(duvz_ref.dtype)
        duvz_ref[rows, pl.ds(2 * SGU_WIDTH + grp * 128, 128)] = dz.astype(duvz_ref.dtype)
        dlg_ref[:, lanes] += dlg
        dlb_ref[:, lanes] += dlb
        dws_ref[grp] += dws
        dbt_ref[...] += dbcol * _onehot_row(grp, SGU_GROUPS)


def _dn_pairs(nb):
    return [(b, h) for b in range(nb) for h in range(DN_HEADS)]


def _dn_batch_args(c_ref, z_ref):
    pairs = _dn_pairs(c_ref.shape[0])
    pick = lambda ref, b, col: ref[b, :, pl.ds(col, DN_HEAD_DIM)]
    cq = jnp.stack([pick(c_ref, b, h * DN_HEAD_DIM) for b, h in pairs])
    ck = jnp.stack([pick(c_ref, b, DN_WIDTH + h * DN_HEAD_DIM) for b, h in pairs])
    cv = jnp.stack([pick(c_ref, b, 2 * DN_WIDTH + h * DN_HEAD_DIM) for b, h in pairs])
    z = jnp.stack([pick(z_ref, b, h * DN_HEAD_DIM) for b, h in pairs])
    return cq, ck, cv, z


def _dn_weight_specs():
    return [_whole((CONV_K, 3 * DN_WIDTH)), _whole((1, GATE_PAD)), _whole((1, GATE_PAD)), _whole((1, DN_HEAD_DIM))]


def _dn_fwd(conv_out, zg, logits, alog, dtb, og):
    nb, s, _ = conv_out.shape
    nc = s // DN_CHUNK
    pairs = _dn_pairs(nb)
    gn = len(pairs)
    chunk = lambda w: pl.BlockSpec((nb, DN_CHUNK, w), lambda n: (0, n, 0))

    def body(c_ref, z_ref, l_ref, alog_ref, dtb_ref, og_ref, out_ref, st_ref, inv_ref, state_ref):
        n = pl.program_id(0)

        @pl.when(n == 0)
        def _():
            state_ref[...] = jnp.zeros_like(state_ref)

        cq, ck, cv, z = _dn_batch_args(c_ref, z_ref)
        state = state_ref[...]
        st_ref[...] = state
        out, new_state, t = _dn_core(cq, ck, cv, z, l_ref[...], state, alog_ref[...], dtb_ref[...], og_ref[...])
        state_ref[...] = new_state
        inv_ref[...] = t.astype(inv_ref.dtype)
        for i, (b, h) in enumerate(pairs):
            out_ref[b, :, pl.ds(h * DN_HEAD_DIM, DN_HEAD_DIM)] = out[i].astype(out_ref.dtype)

    per_chunk = pl.BlockSpec((None, gn, DN_HEAD_DIM, DN_HEAD_DIM), lambda n: (n, 0, 0, 0))
    return pl.pallas_call(
        body, name="deltanet_fwd", grid=(nc,),
        out_shape=(jax.ShapeDtypeStruct((nb, s, DN_WIDTH), BF16),
                   jax.ShapeDtypeStruct((nc, gn, DN_HEAD_DIM, DN_HEAD_DIM), F32),
                   jax.ShapeDtypeStruct((nc, gn, DN_CHUNK, DN_CHUNK), BF16)),
        in_specs=[chunk(3 * DN_WIDTH), chunk(DN_WIDTH), chunk(GATE_PAD)] + _dn_weight_specs()[1:],
        out_specs=(chunk(DN_WIDTH), per_chunk, pl.BlockSpec((None, gn, DN_CHUNK, DN_CHUNK), lambda n: (n, 0, 0, 0))),
        scratch_shapes=[pltpu.VMEM((gn, DN_HEAD_DIM, DN_HEAD_DIM), F32)],
        compiler_params=_params(1),
    )(conv_out, zg, logits, alog, dtb, og)


def _dn_bwd(qkv, conv_out, zg, logits, conv_w, alog, dtb, og, states, inverses, d_out, head_grads):
    nb, s, _ = qkv.shape
    nc = s // DN_CHUNK
    rev = lambda n: nc - 1 - n
    pairs = _dn_pairs(nb)
    gn = len(pairs)
    ng = len(head_grads)

    def body(cur_ref, c_ref, z_ref, l_ref, w_ref, alog_ref, dtb_ref, og_ref, st_ref, inv_ref, do_ref, *rest):
        grad_refs, rest = rest[:ng], rest[ng:]
        dqkv_ref, dz_ref, dl_ref, dw_ref, dalog_ref, ddtb_ref, dog_ref = rest[:7]
        recv_refs, (dstate_ref, dcpad_ref, dw_part_ref, send_sems, recv_sems, local_sems) = rest[7:7 + ng], rest[7 + ng:]
        n = pl.program_id(0)
        start_exchange, wait_exchange = _direct_exchange(grad_refs, recv_refs, send_sems, recv_sems, local_sems, False)
        pl.when(n == 0)(start_exchange)

        @pl.when(n == 0)
        def _():
            dw_part_ref[...] = jnp.zeros_like(dw_part_ref)
            dalog_ref[...] = jnp.zeros_like(dalog_ref)
            ddtb_ref[...] = jnp.zeros_like(ddtb_ref)
            dog_ref[...] = jnp.zeros_like(dog_ref)
            dstate_ref[...] = jnp.zeros_like(dstate_ref)
            dcpad_ref[:, DN_CHUNK:, :] = jnp.zeros((nb, CONV_HALO, 3 * DN_WIDTH), F32)

        cq, ck, cv, z = _dn_batch_args(c_ref, z_ref)
        d_out_g = jnp.stack([do_ref[b, :, pl.ds(h * DN_HEAD_DIM, DN_HEAD_DIM)] for b, h in pairs])
        t_known = inv_ref[...].astype(F32)
        core = lambda *args: _dn_core(*args, t_known=t_known)[:2]
        _, pull = jax.vjp(core, cq, ck, cv, z, l_ref[...], st_ref[...], alog_ref[...], dtb_ref[...], og_ref[...])
        dcq, dck, dcv, dz, dlog, dstate, dalog, ddtb, dog = pull((d_out_g, dstate_ref[...]))
        dstate_ref[...] = dstate
        dl_ref[...] = dlog.astype(dl_ref.dtype)
        dalog_ref[...] += dalog
        ddtb_ref[...] += ddtb
        dog_ref[...] += dog
        for i, (b, h) in enumerate(pairs):
            dcpad_ref[b, 0:DN_CHUNK, pl.ds(h * DN_HEAD_DIM, DN_HEAD_DIM)] = dcq[i]
            dcpad_ref[b, 0:DN_CHUNK, pl.ds(DN_WIDTH + h * DN_HEAD_DIM, DN_HEAD_DIM)] = dck[i]
            dcpad_ref[b, 0:DN_CHUNK, pl.ds(2 * DN_WIDTH + h * DN_HEAD_DIM, DN_HEAD_DIM)] = dcv[i]
            dz_ref[b, :, pl.ds(h * DN_HEAD_DIM, DN_HEAD_DIM)] = dz[i].astype(dz_ref.dtype)
        for b in range(nb):
            xb = cur_ref[b]
            dx = None
            padded = dcpad_ref[b]
            for j in range(CONV_K):
                rows_up = CONV_K - 1 - j
                shifted = (pltpu.roll(padded, padded.shape[0] - rows_up, axis=0) if rows_up else padded)[0:DN_CHUNK]
                term = w_ref[j:j + 1, :] * shifted
                dx = term if dx is None else dx + term
                dw_part_ref[j] += jnp.sum((shifted * xb).reshape(DN_CHUNK // 8, 8, 3 * DN_WIDTH), axis=0)
            dqkv_ref[b] = dx.astype(dqkv_ref.dtype)
            dcpad_ref[b, DN_CHUNK:, :] = dcpad_ref[b, 0:CONV_HALO, :]

        @pl.when(n == nc - 1)
        def _():
            dw_ref[...] = jnp.sum(dw_part_ref[...], axis=1)

        pl.when(n == nc - 1)(wait_exchange)

    chunk = lambda w: pl.BlockSpec((nb, DN_CHUNK, w), lambda n: (0, rev(n), 0))
    return pl.pallas_call(
        body, name="deltanet_bwd", grid=(nc,),
        out_shape=(jax.ShapeDtypeStruct((nb, s, 3 * DN_WIDTH), BF16), jax.ShapeDtypeStruct((nb, s, DN_WIDTH), BF16),
                   jax.ShapeDtypeStruct((nb, s, GATE_PAD), BF16), jax.ShapeDtypeStruct((CONV_K, 3 * DN_WIDTH), F32),
                   jax.ShapeDtypeStruct((1, GATE_PAD), F32), jax.ShapeDtypeStruct((1, GATE_PAD), F32),
                   jax.ShapeDtypeStruct((1, DN_HEAD_DIM), F32))
        + tuple(jax.ShapeDtypeStruct(a.shape, a.dtype) for a in head_grads),
        in_specs=[chunk(3 * DN_WIDTH), chunk(3 * DN_WIDTH), chunk(DN_WIDTH), chunk(GATE_PAD)] + _dn_weight_specs() + [
            pl.BlockSpec((None, gn, DN_HEAD_DIM, DN_HEAD_DIM), lambda n: (rev(n), 0, 0, 0)),
            pl.BlockSpec((None, gn, DN_CHUNK, DN_CHUNK), lambda n: (rev(n), 0, 0, 0)),
            chunk(DN_WIDTH)] + [HBM_SPEC] * ng,
        out_specs=(chunk(3 * DN_WIDTH), chunk(DN_WIDTH), chunk(GATE_PAD), _whole((CONV_K, 3 * DN_WIDTH)),
                   _whole((1, GATE_PAD)), _whole((1, GATE_PAD)), _whole((1, DN_HEAD_DIM))) + (HBM_SPEC,) * ng,
        scratch_shapes=[pltpu.VMEM((gn, DN_HEAD_DIM, DN_HEAD_DIM), F32),
                        pltpu.VMEM((nb, DN_CHUNK + CONV_HALO, 3 * DN_WIDTH), F32),
                        pltpu.VMEM((CONV_K, 8, 3 * DN_WIDTH), F32)] + _exchange_scratch(ng),
        compiler_params=_params(1),
    )(qkv, conv_out, zg, logits, conv_w, alog, dtb, og, states, inverses, d_out, *head_grads)


def _head(a_out, b_out, x2, p2, target, w_out, w_gate, w_proj, ple_g, fin_g):
    t = x2.shape[0]
    tm = min(512, t)
    steps = t // tm

    def body(a_ref, b_ref, x_ref, p_ref, y_ref, wo_ref, wg_ref, wp_ref, pg_ref, fg_ref,
             da_ref, db_ref, dh_ref, dwo_hbm, dwg_hbm, dwp_hbm, dpg_ref, dfg_ref, loss_ref,
             dwo_acc, dwg_acc, dwp_acc, rows_stage, cols_stage):
        i = pl.program_id(0)

        @pl.when(i == 0)
        def _():
            dwo_acc[...] = jnp.zeros_like(dwo_acc)
            dwg_acc[...] = jnp.zeros_like(dwg_acc)
            dwp_acc[...] = jnp.zeros_like(dwp_acc)
            dpg_ref[...] = jnp.zeros_like(dpg_ref)
            dfg_ref[...] = jnp.zeros_like(dfg_ref)
            loss_ref[...] = jnp.zeros_like(loss_ref)

        pg = pg_ref[...]
        fg = fg_ref[...]
        nt = (((1,), (1,)), ((), ()))
        tn = (((0,), (0,)), ((), ()))

        def to_first_norm(rows):
            h1 = (x_ref[rows, :] + jnp.dot(a_ref[rows, :], wo_ref[0:SGU_WIDTH, :], preferred_element_type=F32)
                  + jnp.dot(b_ref[rows, :], wo_ref[SGU_WIDTH:, :], preferred_element_type=F32))
            n1, r1 = _rms(h1)
            pp = jnp.dot(p_ref[rows, :].astype(BF16), wp_ref[...], preferred_element_type=F32)
            return h1, n1, r1, (n1 * pg).astype(BF16), pp

        def to_gate_cotangents(rows, h1, rn, pp):
            gate = _sigmoid(jnp.dot(rn, wg_ref[...], preferred_element_type=F32))
            h2 = h1 + gate * pp
            n2, r2 = _rms(h2)
            err = n2 * fg - y_ref[rows, :]
            loss = _rowsum(jnp.sum(err * err, axis=-1, keepdims=True))
            dy = err * (1.0 / D_MODEL)
            dh2 = _rms_bwd(dy * fg, n2, r2)
            return loss, _rowsum(dy * n2), dh2, (dh2 * gate).astype(BF16), (dh2 * pp * gate * (1.0 - gate)).astype(BF16)

        def to_branch_cotangents(rows, dgl, dh2, n1, r1):
            drn = lax.dot_general(dgl, wg_ref[...], nt, preferred_element_type=F32)
            dh1 = dh2 + _rms_bwd(drn * pg, n1, r1)
            dh_ref[rows, :] = dh1
            dhb = dh1.astype(BF16)
            da_ref[rows, :] = lax.dot_general(dhb, wo_ref[0:SGU_WIDTH, :], nt, preferred_element_type=F32)
            db_ref[rows, :] = lax.dot_general(dhb, wo_ref[SGU_WIDTH:, :], nt, preferred_element_type=F32)
            return _rowsum(drn * n1), dhb

        parts = [pl.ds(k * (tm // 2), tm // 2) for k in range(2)]
        first = [to_first_norm(rows) for rows in parts]
        mid = [to_gate_cotangents(rows, h1, rn, pp) for rows, (h1, _, _, rn, pp) in zip(parts, first)]
        loss_ref[...] += jnp.broadcast_to(mid[0][0] + mid[1][0], loss_ref.shape)
        dfg_ref[...] += mid[0][1] + mid[1][1]
        last = [to_branch_cotangents(rows, m[4], m[2], f[1], f[2]) for rows, m, f in zip(parts, mid, first)]
        rn = jnp.concatenate([f[3] for f in first], axis=0)
        dpp = jnp.concatenate([m[3] for m in mid], axis=0)
        dgl = jnp.concatenate([m[4] for m in mid], axis=0)
        dwp_acc[...] += lax.dot_general(p_ref[...].astype(BF16), dpp, tn, preferred_element_type=F32)
        dwg_acc[...] += lax.dot_general(rn, dgl, tn, preferred_element_type=F32)
        dpg_ref[...] += last[0][0] + last[1][0]
        dhb = jnp.concatenate([l[1] for l in last], axis=0)
        dwo_acc[0:SGU_WIDTH, :] += lax.dot_general(a_ref[...], dhb, tn, preferred_element_type=F32)
        dwo_acc[SGU_WIDTH:, :] += lax.dot_general(b_ref[...], dhb, tn, preferred_element_type=F32)

        @pl.when(i == steps - 1)
        def _():
            for j in range(N_DEV):
                for acc, hbm in ((dwo_acc, dwo_hbm), (dwg_acc, dwg_hbm)):
                    rows_stage[...] = acc[j * LANES:(j + 1) * LANES, :].astype(BF16)
                    pltpu.sync_copy(rows_stage, hbm.at[j])
                cols_stage[...] = dwp_acc[:, j * LANES:(j + 1) * LANES].astype(BF16)
                pltpu.sync_copy(cols_stage, dwp_hbm.at[j])

    tile = lambda w: pl.BlockSpec((tm, w), lambda i: (i, 0))
    return pl.pallas_call(
        body, name="head_fwd_bwd", grid=(steps,),
        out_shape=(jax.ShapeDtypeStruct((t, SGU_WIDTH), F32), jax.ShapeDtypeStruct((t, DN_WIDTH), F32),
                   jax.ShapeDtypeStruct((t, D_MODEL), F32), jax.ShapeDtypeStruct((N_DEV, LANES, D_MODEL), BF16),
                   jax.ShapeDtypeStruct((N_DEV, LANES, D_MODEL), BF16), jax.ShapeDtypeStruct((N_DEV, PLE_DIM, LANES), BF16),
                   jax.ShapeDtypeStruct((1, D_MODEL), F32), jax.ShapeDtypeStruct((1, D_MODEL), F32),
                   jax.ShapeDtypeStruct((8, LANES), F32)),
        in_specs=[tile(SGU_WIDTH), tile(DN_WIDTH), tile(D_MODEL), tile(PLE_DIM), tile(D_MODEL),
                  VMEM_SPEC, VMEM_SPEC, VMEM_SPEC, _whole((1, D_MODEL)), _whole((1, D_MODEL))],
        out_specs=(tile(SGU_WIDTH), tile(DN_WIDTH), tile(D_MODEL), HBM_SPEC, HBM_SPEC, HBM_SPEC,
                   _whole((1, D_MODEL)), _whole((1, D_MODEL)), _whole((8, LANES))),
        scratch_shapes=[pltpu.VMEM((D_MODEL, D_MODEL), F32), pltpu.VMEM((D_MODEL, D_MODEL), F32),
                        pltpu.VMEM((PLE_DIM, D_MODEL), F32), pltpu.VMEM((LANES, D_MODEL), BF16),
                        pltpu.VMEM((PLE_DIM, LANES), BF16)],
        compiler_params=_params(1),
    )(a_out, b_out, x2, p2, target, w_out, w_gate, w_proj, ple_g, fin_g)


def _inproj_bwd(x2, dh1, a_uvz, d_sgu, d_q, d_z, d_l, norm_g, sgu_weights, wt, wgt):
    t = x2.shape[0]
    tm = min(256, t)
    steps = t // tm

    widths = (a_uvz.shape[1], d_q.shape[1], d_z.shape[1], d_l.shape[1])
    starts = (0, widths[0], widths[0] + widths[1], widths[0] + widths[1] + widths[2])

    def body(x_ref, dh_ref, uvz_ref, dsgu_ref, dq_ref, dz_ref, dl_ref, g_ref, lg_ref, lb_ref, ws_ref, bt_ref,
             wt_ref, wgt_ref,
             dx_ref, dw_hbm, dg_ref, dlg_ref, dlb_ref, dws_ref, dbt_ref, dw_acc, stage_ref, da_ref):
        i = pl.program_id(0)

        @pl.when(i == 0)
        def _():
            dw_acc[...] = jnp.zeros_like(dw_acc)
            for ref in (dg_ref, dlg_ref, dlb_ref, dws_ref, dbt_ref):
                ref[...] = jnp.zeros_like(ref)

        g = g_ref[...]
        n, r = _rms(x_ref[...])
        xn = (n * g).astype(BF16)
        dxn = None
        sgu_done = 0

        def sgu_pieces(count):
            nonlocal sgu_done
            _sgu_bwd_tile(uvz_ref, dsgu_ref, (lg_ref, lb_ref, ws_ref, bt_ref), da_ref,
                          (dlg_ref, dlb_ref, dws_ref, dbt_ref), range(sgu_done, sgu_done + count))
            sgu_done += count

        sgu_total = tm // SGU_CHUNK * SGU_GROUPS
        before_q, after_q_chunk = sgu_total // 2, (sgu_total // 4, sgu_total // 8, sgu_total // 8)
        for d_ref, col0 in reversed(tuple(zip((da_ref, dq_ref, dz_ref, dl_ref), starts))):
            if d_ref is dq_ref:
                sgu_pieces(before_q)
            if d_ref is da_ref:
                sgu_pieces(sgu_total - sgu_done)
            width = d_ref.shape[1]
            rows = wgt_ref[...] if d_ref is dl_ref else wt_ref[col0:col0 + width, :]
            term = jnp.dot(d_ref[...], rows, preferred_element_type=F32)
            dxn = term if dxn is None else dxn + term
            for c0 in range(0, width, 512):
                c1 = min(c0 + 512, width)
                dw_acc[col0 + c0:col0 + c1, :] += lax.dot_general(d_ref[:, c0:c1], xn, (((0,), (0,)), ((), ())),
                                                                  preferred_element_type=F32)
                if d_ref is dq_ref:
                    sgu_pieces(after_q_chunk[c0 // 512])
        dg_ref[...] += _rowsum(dxn * n)
        dx_ref[...] = dh_ref[...] + _rms_bwd(dxn * g, n, r)

        @pl.when(i == steps - 1)
        def _():
            for j in range(N_DEV):
                stage_ref[...] = dw_acc[j * IN_SHARD:(j + 1) * IN_SHARD, :]
                pltpu.sync_copy(stage_ref, dw_hbm.at[j])

    tile = lambda w: pl.BlockSpec((tm, w), lambda i: (i, 0))
    sgu_shapes = ((1, SGU_WIDTH), (1, SGU_WIDTH), (SGU_GROUPS, SGU_CHUNK, SGU_CHUNK), (SGU_CHUNK, SGU_GROUPS))
    return pl.pallas_call(
        body, name="inproj_sgu_bwd", grid=(steps,),
        out_shape=(jax.ShapeDtypeStruct((t, D_MODEL), F32), jax.ShapeDtypeStruct((N_DEV, IN_SHARD, D_MODEL), F32),
                   jax.ShapeDtypeStruct((1, D_MODEL), F32)) + tuple(jax.ShapeDtypeStruct(s, F32) for s in sgu_shapes),
        in_specs=[tile(D_MODEL), tile(D_MODEL), tile(widths[0]), tile(SGU_WIDTH)] + [tile(w) for w in widths[1:]]
        + [_whole((1, D_MODEL))] + [_whole(s) for s in sgu_shapes] + [VMEM_SPEC] * 2,
        out_specs=(tile(D_MODEL), HBM_SPEC, _whole((1, D_MODEL))) + tuple(_whole(s) for s in sgu_shapes),
        scratch_shapes=[pltpu.VMEM((sum(widths), D_MODEL), F32), pltpu.VMEM((IN_SHARD, D_MODEL), F32),
                        pltpu.VMEM((tm, widths[0]), BF16)],
        compiler_params=_params(1),
    )(x2, dh1, a_uvz, d_sgu, d_q, d_z, d_l, norm_g, *sgu_weights, wt, wgt)


def _reduce_adamw(recv, w, m, v, name, col_block=None):
    n, rows, cols = recv.shape
    cb = col_block or cols
    lead = w.ndim - 2

    def body(r_ref, w_ref, m_ref, v_ref, g_ref, d_ref, nm_ref, nv_ref):
        g = r_ref[0].astype(F32)
        for i in range(1, n):
            g = g + r_ref[i].astype(F32)
        m_new = ADAM_B1 * m_ref[...] + (1.0 - ADAM_B1) * g
        v_new = ADAM_B2 * v_ref[...] + (1.0 - ADAM_B2) * jnp.square(g)
        m_hat = m_new / (1.0 - ADAM_B1 ** ADAM_STEP)
        v_hat = v_new / (1.0 - ADAM_B2 ** ADAM_STEP)
        g_ref[...] = g
        d_ref[...] = -ADAM_LR * (m_hat / (jnp.sqrt(v_hat) + ADAM_EPS) + ADAM_WD * w_ref[...])
        nm_ref[...] = m_new
        nv_ref[...] = v_new

    blk = pl.BlockSpec((None,) * lead + (rows, cb), lambda i: (0,) * lead + (0, i))
    return pl.pallas_call(
        body, name=name, grid=(cols // cb,),
        out_shape=tuple(jax.ShapeDtypeStruct(w.shape, F32) for _ in range(4)),
        in_specs=[pl.BlockSpec((n, rows, cb), lambda i: (0, 0, i)), blk, blk, blk],
        out_specs=(blk, blk, blk, blk),
        compiler_params=_params(1),
    )(recv, w, m, v)


def _adamw_replicated(received, ws, ms, vs):
    nw = len(ws)
    starts = [sum(SMALL_PIECE_ROWS[:i]) for i in range(len(SMALL_PIECE_ROWS))]

    def natural(g_ref, row0, shape):
        cols, rows = shape[-1], _size(shape[:-1])
        if cols == LANES:
            return g_ref[row0:row0 + rows, :].reshape(shape)
        if cols < LANES:
            return g_ref[row0:row0 + 1, 0:cols].reshape(shape)
        per = cols // LANES
        return jnp.concatenate(
            [jnp.concatenate([g_ref[row0 + r * per + k:row0 + r * per + k + 1, :] for k in range(per)], axis=1)
             for r in range(rows)], axis=0).reshape(shape)

    def body(r_ref, *refs):
        w_refs, m_refs, v_refs = refs[:nw], refs[nw:2 * nw], refs[2 * nw:3 * nw]
        conv_ref, loss_ref = refs[3 * nw], refs[3 * nw + 1]
        out_refs, g_ref = refs[3 * nw + 2:-1], refs[-1]
        g = r_ref[0]
        for q in range(1, N_CHIPS):
            g = g + r_ref[q]
        g_ref[...] = g
        conv_ref[...] = natural(g_ref, starts[0], (CONV_K, 3 * DN_WIDTH))
        loss_ref[...] = natural(g_ref, starts[-1], (1, 1))
        for i in range(nw):
            gi = natural(g_ref, starts[1 + i], w_refs[i].shape)
            m_new = ADAM_B1 * m_refs[i][...] + (1.0 - ADAM_B1) * gi
            v_new = ADAM_B2 * v_refs[i][...] + (1.0 - ADAM_B2) * jnp.square(gi)
            m_hat = m_new / (1.0 - ADAM_B1 ** ADAM_STEP)
            v_hat = v_new / (1.0 - ADAM_B2 ** ADAM_STEP)
            out_refs[4 * i][...] = gi
            out_refs[4 * i + 1][...] = -ADAM_LR * (m_hat / (jnp.sqrt(v_hat) + ADAM_EPS) + ADAM_WD * w_refs[i][...])
            out_refs[4 * i + 2][...] = m_new
            out_refs[4 * i + 3][...] = v_new

    def spec(a):
        lead = max(a.ndim - 3, 0)
        return pl.BlockSpec((None,) * lead + a.shape[lead:], lambda: (0,) * a.ndim)

    weight_specs = [spec(a) for a in ws]
    return pl.pallas_call(
        body, name="adamw_replicated",
        out_shape=(jax.ShapeDtypeStruct((CONV_K, 3 * DN_WIDTH), F32), jax.ShapeDtypeStruct((1, 1), F32))
        + tuple(jax.ShapeDtypeStruct(a.shape, F32) for a in ws for _ in range(4)),
        in_specs=[pl.BlockSpec(received.shape, lambda: (0, 0, 0))] + weight_specs * 3,
        out_specs=(pl.BlockSpec((CONV_K, 3 * DN_WIDTH), lambda: (0, 0)), pl.BlockSpec((1, 1), lambda: (0, 0)))
        + tuple(s for s in weight_specs for _ in range(4)),
        scratch_shapes=[pltpu.VMEM(received.shape[1:], F32)],
        compiler_params=pltpu.CompilerParams(vmem_limit_bytes=VMEM_LIMIT),
    )(received, *ws, *ms, *vs)


def _pack_rows(pieces, rows):
    padded = [jnp.pad(jnp.ravel(p), (0, -p.size % LANES)) for p in pieces]
    flat = jnp.concatenate(padded)
    return jnp.pad(flat, (0, rows * LANES - flat.shape[0])).reshape(rows, LANES)


def kernel(x, p, norm_g, w_in, sgu_ln_g, sgu_ln_b, sgu_w_s, sgu_b_s, dn_conv_w, dn_a_log, dn_dt_bias, dn_o_norm_g, w_out, ple_norm_g, ple_gate_w, ple_proj_w, final_norm_g, loss_target, m_norm_g, m_w_in, m_sgu_ln_g, m_sgu_ln_b, m_sgu_w_s, m_sgu_b_s, m_dn_conv_w, m_dn_a_log, m_dn_dt_bias, m_dn_o_norm_g, m_w_out, m_ple_norm_g, m_ple_gate_w, m_ple_proj_w, m_final_norm_g, v_norm_g, v_w_in, v_sgu_ln_g, v_sgu_ln_b, v_sgu_w_s, v_sgu_b_s, v_dn_conv_w, v_dn_a_log, v_dn_dt_bias, v_dn_o_norm_g, v_w_out, v_ple_norm_g, v_ple_gate_w, v_ple_proj_w, v_final_norm_g):
    weights = dict(norm_g=norm_g, w_in=w_in, sgu_ln_g=sgu_ln_g, sgu_ln_b=sgu_ln_b, sgu_w_s=sgu_w_s, sgu_b_s=sgu_b_s,
                   dn_conv_w=dn_conv_w, dn_a_log=dn_a_log, dn_dt_bias=dn_dt_bias, dn_o_norm_g=dn_o_norm_g, w_out=w_out,
                   ple_norm_g=ple_norm_g, ple_gate_w=ple_gate_w, ple_proj_w=ple_proj_w, final_norm_g=final_norm_g)
    mom1 = dict(norm_g=m_norm_g, w_in=m_w_in, sgu_ln_g=m_sgu_ln_g, sgu_ln_b=m_sgu_ln_b, sgu_w_s=m_sgu_w_s,
                sgu_b_s=m_sgu_b_s, dn_conv_w=m_dn_conv_w, dn_a_log=m_dn_a_log, dn_dt_bias=m_dn_dt_bias,
                dn_o_norm_g=m_dn_o_norm_g, w_out=m_w_out, ple_norm_g=m_ple_norm_g, ple_gate_w=m_ple_gate_w,
                ple_proj_w=m_ple_proj_w, final_norm_g=m_final_norm_g)
    mom2 = dict(norm_g=v_norm_g, w_in=v_w_in, sgu_ln_g=v_sgu_ln_g, sgu_ln_b=v_sgu_ln_b, sgu_w_s=v_sgu_w_s,
                sgu_b_s=v_sgu_b_s, dn_conv_w=v_dn_conv_w, dn_a_log=v_dn_a_log, dn_dt_bias=v_dn_dt_bias,
                dn_o_norm_g=v_dn_o_norm_g, w_out=v_w_out, ple_norm_g=v_ple_norm_g, ple_gate_w=v_ple_gate_w,
                ple_proj_w=v_ple_proj_w, final_norm_g=v_final_norm_g)
    nb, s, _ = x.shape
    t = nb * s

    transposed = lambda a: jnp.transpose(a, (2, 0, 1)).reshape(IN_SHARD, D_MODEL)
    w_in_t, m_in_t, v_in_t = transposed(w_in), transposed(m_w_in), transposed(v_w_in)
    w_in_blocks, conv_blocks = _all_gather([w_in_t.astype(BF16), dn_conv_w[0]])
    w_in_full_t = w_in_blocks.reshape(IN_COLS, D_MODEL)
    wgt = jnp.pad(w_in_full_t[sum(IN_GROUPS):], ((0, GATE_PAD - 2 * DN_HEADS), (0, 0)))
    conv_full = jnp.moveaxis(conv_blocks, 0, 1).reshape(CONV_K, 3 * DN_WIDTH)
    later_shards = [w_out[0].astype(BF16), ple_gate_w[0].astype(BF16), ple_proj_w[0].astype(BF16)]

    pad_row = lambda a: jnp.pad(a.reshape(1, -1), ((0, 0), (DN_HEADS, GATE_PAD - DN_HEADS - a.size)))
    alog, dtb = pad_row(dn_a_log), pad_row(dn_dt_bias)
    og = dn_o_norm_g.reshape(1, DN_HEAD_DIM)
    ws = sgu_w_s.reshape(SGU_GROUPS, SGU_CHUNK, SGU_CHUNK)
    b_t = sgu_b_s.reshape(SGU_GROUPS, SGU_CHUNK).T
    fin_g = final_norm_g.reshape(1, D_MODEL)

    x2 = x.reshape(t, D_MODEL)
    sgu_weights = (sgu_ln_g, sgu_ln_b, ws, b_t)
    a_uvz, b_qkv, b_z, b_l, a_out, conv_out, w_out_blocks, w_gate_blocks, w_proj_blocks = _inproj_fwd(
        x2, s, norm_g, w_in_full_t, wgt, sgu_weights, conv_full, later_shards)
    w_out_full = w_out_blocks.reshape(D_MODEL, D_MODEL)
    w_gate_full = w_gate_blocks.reshape(D_MODEL, D_MODEL)
    w_proj_full = jnp.moveaxis(w_proj_blocks, 0, 1).reshape(PLE_DIM, D_MODEL)
    qkv3 = b_qkv.reshape(nb, s, 3 * DN_WIDTH)
    conv_out = conv_out.reshape(nb, s, 3 * DN_WIDTH)
    z3 = b_z.reshape(nb, s, DN_WIDTH)
    l3 = b_l.reshape(nb, s, GATE_PAD)
    b_out, states, inverses = _dn_fwd(conv_out, z3, l3, alog, dtb, og)

    d_a, d_b, dh1, g_w_out, g_gate, g_proj, g_ple_g, g_fin_g, loss_tile = _head(
        a_out, b_out.reshape(t, DN_WIDTH), x2, p.reshape(t, PLE_DIM), loss_target.reshape(t, D_MODEL),
        w_out_full, w_gate_full, w_proj_full, ple_norm_g, fin_g)
    d_qkv, d_z, d_l, g_conv, g_alog, g_dtb, g_og, *head_received = _dn_bwd(
        qkv3, conv_out, z3, l3, conv_full, alog, dtb, og, states, inverses, d_b.reshape(nb, s, DN_WIDTH),
        [g_w_out, g_gate, g_proj])
    grad_x, g_w_in, g_norm, g_ln_g, g_ln_b, g_ws, g_bt = _inproj_bwd(
        x2, dh1, a_uvz, d_a, d_qkv.reshape(t, 3 * DN_WIDTH), d_z.reshape(t, DN_WIDTH), d_l.reshape(t, GATE_PAD),
        norm_g, sgu_weights, w_in_full_t, wgt)

    small = _pack_rows([g_conv, g_norm, g_ln_g, g_ln_b, g_ws, g_bt.T, g_alog[:, DN_HEADS:2 * DN_HEADS], g_dtb[:, DN_HEADS:2 * DN_HEADS], g_og,
                        g_ple_g, g_fin_g, (0.5 / D_MODEL) * loss_tile[0:1, 0:1]], SMALL_ROWS)
    w_in_received, small_received = _reduce_exchange(g_w_in, small)

    results = {}
    outs = _reduce_adamw(w_in_received, w_in_t, m_in_t, v_in_t, "adamw_w_in", 4 * LANES)
    results["w_in"] = [jnp.transpose(a.reshape(IN_SHARD, 1, D_MODEL), (1, 2, 0)) for a in outs]
    for name, recv in zip(("w_out", "ple_gate_w", "ple_proj_w"), head_received):
        results[name] = _reduce_adamw(recv, weights[name], mom1[name], mom2[name], "adamw_" + name)
    names = [name for name, _ in REPLICATED]
    two_d = lambda a: a.reshape(1, -1) if a.ndim == 1 else a
    g_conv_sum, loss_sum, *flat_outs = _adamw_replicated(
        small_received, *[[two_d(src[k]) for k in names] for src in (weights, mom1, mom2)])
    for i, k in enumerate(names):
        results[k] = [a.reshape(weights[k].shape) for a in flat_outs[4 * i:4 * i + 4]]
    loss = loss_sum[0, 0]
    me = 4 * lax.axis_index("x") + 2 * lax.axis_index("y") + lax.axis_index("c")
    conv_mine = lax.dynamic_slice(g_conv_sum, (0, me * 192), (CONV_K, 192))
    results["dn_conv_w"] = _reduce_adamw(conv_mine[None], dn_conv_w, m_dn_conv_w, v_dn_conv_w, "adamw_dn_conv_w")

    return (loss, grad_x.reshape(nb, s, D_MODEL), *[results[k][0] for k in WEIGHT_ORDER],
            *[results[k][1] for k in WEIGHT_ORDER], *[results[k][2] for k in WEIGHT_ORDER],
            *[results[k][3] for k in WEIGHT_ORDER])
```

```python
import functools

import jax
import jax.numpy as jnp
from jax import lax
from jax.experimental import pallas as pl
from jax.experimental.pallas import tpu as pltpu

F32 = jnp.float32
BF16 = jnp.bfloat16

N_DEV = 8
D_MODEL = 1024
SGU_WIDTH = 512
SGU_GROUPS = 4
SGU_CHUNK = 128
CONV_STRIP = 256
DN_WIDTH = 512
DN_HEADS = 4
DN_HEAD_DIM = 128
DN_CHUNK = 128
CONV_K = 4
CONV_HALO = 8
PLE_DIM = 256
EPS = 1e-6
IN_COLS = 3592
IN_SHARD = IN_COLS // N_DEV
GATE_PAD = 128
IN_GROUPS = (3 * SGU_WIDTH, 3 * DN_WIDTH, DN_WIDTH)

ADAM_LR = 0.001
ADAM_B1 = 0.9
ADAM_B2 = 0.999
ADAM_EPS = 1e-08
ADAM_WD = 0.01
ADAM_STEP = 10

LANES = 128
VMEM_LIMIT = 56 * 1024 * 1024
MESH = pl.DeviceIdType.MESH

REPLICATED = (("norm_g", (1, D_MODEL)), ("sgu_ln_g", (1, SGU_WIDTH)), ("sgu_ln_b", (1, SGU_WIDTH)),
              ("sgu_w_s", (1, SGU_GROUPS, SGU_CHUNK, SGU_CHUNK)), ("sgu_b_s", (1, SGU_GROUPS, SGU_CHUNK)),
              ("dn_a_log", (1, DN_HEADS)), ("dn_dt_bias", (1, DN_HEADS)), ("dn_o_norm_g", (1, DN_HEAD_DIM)),
              ("ple_norm_g", (1, D_MODEL)), ("final_norm_g", (D_MODEL,)))
WEIGHT_ORDER = ("norm_g", "w_in", "sgu_ln_g", "sgu_ln_b", "sgu_w_s", "sgu_b_s", "dn_conv_w", "dn_a_log",
                "dn_dt_bias", "dn_o_norm_g", "w_out", "ple_norm_g", "ple_gate_w", "ple_proj_w", "final_norm_g")


def _size(shape):
    n = 1
    for s in shape:
        n *= s
    return n


SMALL_LAYOUT = (("conv", (CONV_K, 3 * DN_WIDTH)),) + REPLICATED + (("loss", (1,)),)
SMALL_PIECE_ROWS = tuple(-(-_size(s) // LANES) for _, s in SMALL_LAYOUT)
SMALL_ROWS = -(-sum(SMALL_PIECE_ROWS) // 8) * 8


def _bdot(a, b):
    return jnp.dot(a.astype(BF16), b.astype(BF16), preferred_element_type=F32)


def _sigmoid(x):
    return 0.5 * jnp.tanh(0.5 * x) + 0.5


@jax.custom_vjp
def _silu(x):
    return x * _sigmoid(x)


def _silu_fwd(x):
    s = _sigmoid(x)
    return x * s, (x, s)


def _silu_bwd(res, ct):
    x, s = res
    return (ct * (s * (1.0 + x * (1.0 - s))),)


_silu.defvjp(_silu_fwd, _silu_bwd)


def _normal_cdf(x):
    return 0.5 + 0.5 * lax.erf(x * (0.5 ** 0.5))


@jax.custom_vjp
def _gelu(x):
    return x * _normal_cdf(x)


def _gelu_fwd(x):
    cdf = _normal_cdf(x)
    return x * cdf, (x, cdf)


def _gelu_bwd(res, ct):
    x, cdf = res
    pdf = jnp.exp(-0.5 * x * x) * ((2.0 * jnp.pi) ** -0.5)
    return (ct * (cdf + x * pdf),)


_gelu.defvjp(_gelu_fwd, _gelu_bwd)


def _softplus(x):
    return jnp.maximum(x, 0.0) + jnp.log1p(jnp.exp(-jnp.abs(x)))


@jax.custom_vjp
def _l2n(x):
    return x * lax.rsqrt(jnp.sum(x * x, axis=-1, keepdims=True) + EPS)


def _l2n_fwd(x):
    r = lax.rsqrt(jnp.sum(x * x, axis=-1, keepdims=True) + EPS)
    n = x * r
    return n, (n, r)


def _l2n_bwd(res, ct):
    n, r = res
    return (r * (ct - n * jnp.sum(ct * n, axis=-1, keepdims=True)),)


_l2n.defvjp(_l2n_fwd, _l2n_bwd)


def _rms(x):
    r = lax.rsqrt(jnp.mean(x * x, axis=-1, keepdims=True) + EPS)
    return x * r, r


def _rms_bwd(dn, n, r):
    return r * (dn - n * jnp.mean(dn * n, axis=-1, keepdims=True))


@jax.custom_vjp
def _rms_normed(x):
    return _rms(x)[0]


def _rms_normed_fwd(x):
    n, r = _rms(x)
    return n, (n, r)


def _rms_normed_bwd(res, ct):
    return (_rms_bwd(ct, *res),)


_rms_normed.defvjp(_rms_normed_fwd, _rms_normed_bwd)


def _onehot_row(idx, width):
    return (lax.broadcasted_iota(jnp.int32, (1, width), 1) == idx).astype(F32)


def _rowsum(x):
    return jnp.sum(x, axis=0, keepdims=True)


def _iota2(n):
    return lax.broadcasted_iota(jnp.int32, (n, n), 0), lax.broadcasted_iota(jnp.int32, (n, n), 1)


def _bmm(a, b):
    return lax.dot_general(a.astype(BF16), b.astype(BF16), (((2,), (1,)), ((0,), (0,))), preferred_element_type=F32)


def _bmm_nt(a, b):
    return lax.dot_general(a.astype(BF16), b.astype(BF16), (((2,), (2,)), ((0,), (0,))), preferred_element_type=F32)


def _bmm_tn(a, b):
    return lax.dot_general(a.astype(BF16), b.astype(BF16), (((1,), (1,)), ((0,), (0,))), preferred_element_type=F32)


def _tri_inv_impl(a):
    n = a.shape[-1]
    r, c = _iota2(n)
    x = r ^ c
    eye = (r == c).astype(F32)
    ad = jnp.where(x < 16, a, 0.0)
    p2 = _bmm(ad, ad)
    e = p2 - ad - _bmm(ad, p2)
    p4 = _bmm(p2, p2)
    e = e + p4 + _bmm(e, p4)
    p8 = _bmm(p4, p4)
    e = e + p8 + _bmm(e, p8)
    size = 16
    while size < n:
        m = jnp.where(jnp.logical_and(x < 2 * size, x >= size), a, 0.0)
        f = m + _bmm(m, e)
        e = e - f - _bmm(e, f)
        size *= 2
    return e + eye


@jax.custom_vjp
def _tri_inv(a, known):
    return _tri_inv_impl(a) if known is None else known


def _tri_inv_fwd(a, known):
    t = _tri_inv(a, known)
    return t, (t, known)


def _tri_inv_bwd(res, dt):
    t, known = res
    return -_bmm_tn(t, _bmm_nt(dt, t)), None if known is None else jnp.zeros_like(known)


_tri_inv.defvjp(_tri_inv_fwd, _tri_inv_bwd)


@jax.custom_vjp
def _standardized(x):
    xc = x - jnp.mean(x, axis=-1, keepdims=True)
    return xc * lax.rsqrt(jnp.mean(xc * xc, axis=-1, keepdims=True) + EPS)


def _standardized_fwd(x):
    xc = x - jnp.mean(x, axis=-1, keepdims=True)
    rstd = lax.rsqrt(jnp.mean(xc * xc, axis=-1, keepdims=True) + EPS)
    y = xc * rstd
    return y, (y, rstd)


def _standardized_bwd(res, ct):
    y, rstd = res
    return (rstd * (ct - jnp.mean(ct, axis=-1, keepdims=True) - y * jnp.mean(ct * y, axis=-1, keepdims=True)),)


_standardized.defvjp(_standardized_fwd, _standardized_bwd)


def _sgu_core(u, v, z, lg, lb, ws, bcol):
    n = ws.shape[0]
    r, c = _iota2(n)
    wm = jnp.where(r >= c, ws, 0.0)
    gu = _gelu(u)
    gv = _gelu(v)
    ln = _standardized(gv) * lg + lb
    s = _bdot(wm, ln) + bcol
    return gu * s * _silu(z)


def _lanes_of(x):
    return jnp.concatenate([x[i] for i in range(x.shape[0])], axis=1)


def _batch_of(x, width):
    return jnp.concatenate([x[None, :, i * width:(i + 1) * width] for i in range(x.shape[1] // width)], axis=0)


def _mask_dot(mask, x):
    hi = x.astype(BF16)
    lo = (x - hi.astype(F32)).astype(BF16)
    m = mask.astype(BF16)
    return jnp.dot(m, hi, preferred_element_type=F32) + jnp.dot(m, lo, preferred_element_type=F32)


def _split_dot(x, mask, dims):
    hi = x.astype(BF16)
    lo = (x - hi.astype(F32)).astype(BF16)
    m = mask.astype(BF16)
    return (lax.dot_general(hi, m, dims, preferred_element_type=F32)
            + lax.dot_general(lo, m, dims, preferred_element_type=F32))


def _lane_select(lanes, blocks, first_lane):
    src = lax.broadcasted_iota(jnp.int32, (lanes, blocks * LANES), 0)
    dst = lax.broadcasted_iota(jnp.int32, (lanes, blocks * LANES), 1) // LANES
    return src == dst + first_lane


def _pick_lanes(x, blocks, first_lane):
    return _pick_lanes_vjp(blocks, first_lane, x)


@functools.partial(jax.custom_vjp, nondiff_argnums=(0, 1))
def _pick_lanes_vjp(blocks, first_lane, x):
    return _split_dot(x, _lane_select(x.shape[-1], blocks, first_lane), (((1,), (0,)), ((), ())))


def _pick_lanes_fwd(blocks, first_lane, x):
    return _pick_lanes_vjp(blocks, first_lane, x), x.shape[-1]


def _pick_lanes_bwd(blocks, first_lane, lanes, ct):
    return (sum(jnp.sum(ct[:, h * LANES:(h + 1) * LANES], axis=-1, keepdims=True) * _onehot_row(first_lane + h, lanes)
                for h in range(blocks)),)


_pick_lanes_vjp.defvjp(_pick_lanes_fwd, _pick_lanes_bwd)


def _tri_mask(n, upper):
    r, c = _iota2(n)
    return (r <= c) if upper else (r >= c)


@jax.custom_vjp
def _cumsum_rows(x):
    return _mask_dot(_tri_mask(x.shape[0], False), x)


def _cumsum_rows_fwd(x):
    return _cumsum_rows(x), None


def _cumsum_rows_bwd(_, ct):
    return (_mask_dot(_tri_mask(ct.shape[0], True), ct),)


_cumsum_rows.defvjp(_cumsum_rows_fwd, _cumsum_rows_bwd)


@jax.custom_vjp
def _colsum_all_rows(x):
    return _mask_dot(jnp.ones((x.shape[0], x.shape[0]), jnp.bool_), x)


def _colsum_all_rows_fwd(x):
    return _colsum_all_rows(x), None


def _colsum_all_rows_bwd(_, ct):
    return (_mask_dot(jnp.ones((ct.shape[0], ct.shape[0]), jnp.bool_), ct),)


_colsum_all_rows.defvjp(_colsum_all_rows_fwd, _colsum_all_rows_bwd)


def _dn_core(cq, ck, cv, z, logits, state, alog, dtb, og, t_known=None):
    gn, cn, dh = cq.shape
    heads = gn // logits.shape[0]
    q = _l2n(_silu(cq)) * (dh ** -0.5)
    k = _l2n(_silu(ck))
    v = _silu(cv)
    beta_lanes = _sigmoid(logits)
    g_lanes = -jnp.exp(alog) * _softplus(logits + dtb)
    beta_all = jnp.concatenate([_pick_lanes(beta_lanes[b], heads, 0) for b in range(logits.shape[0])], axis=1)
    g_all = jnp.concatenate([_pick_lanes(g_lanes[b], heads, heads) for b in range(logits.shape[0])], axis=1)
    beta = _batch_of(beta_all, dh)
    g_wide = _batch_of(g_all, dh)
    r, c = _iota2(cn)
    tril = r >= c
    rw = lax.broadcasted_iota(jnp.int32, (cn, dh), 0)
    cw = lax.broadcasted_iota(jnp.int32, (cn, dh), 1)
    upper_wide = (rw <= cw).astype(F32)
    gc_wide = _batch_of(_cumsum_rows(g_all), dh)
    gc_cols = _batch_of(_colsum_all_rows(_lanes_of(g_wide * upper_wide)), dh)[:, :, :cn]
    decay = jnp.exp(jnp.where(tril, gc_wide[:, :, :cn] - gc_cols, -1e30))
    kb = k * beta
    kk = _bmm_nt(kb, k) * decay
    t = _tri_inv(jnp.where(r > c, kk, 0.0), t_known)
    eg = jnp.exp(gc_wide)
    sol = _bmm(t, jnp.concatenate([v * beta, kb * eg], axis=-1))
    u_val, w_dec = sol[:, :, :dh], sol[:, :, dh:]
    qk = _bmm_nt(q, k) * decay
    g_last = jnp.sum(g_wide, axis=1, keepdims=True)
    k_dec = k * jnp.exp(g_last - gc_wide)
    ws = _bmm(jnp.concatenate([w_dec, q * eg], axis=1), state)
    v_new = u_val - ws[:, :cn]
    o = ws[:, cn:] + _bmm(qk, v_new)
    new_state = state * jnp.exp(g_last) + _bmm_tn(k_dec, v_new)
    return _rms_normed(o) * og * _silu(z), new_state, t


N_CHIPS = 4
HBM_SPEC = pl.BlockSpec(memory_space=pl.ANY)


def _place():
    return lax.axis_index("x"), lax.axis_index("y"), lax.axis_index("c")


def _other_chip(k):
    x, y, _ = _place()
    px = 1 - x if k & 2 else x
    py = 1 - y if k & 1 else y
    return px, py, 2 * px + py


def _remote(src, dst, send_sem, recv_sem, device):
    return pltpu.make_async_remote_copy(src_ref=src, dst_ref=dst, send_sem=send_sem, recv_sem=recv_sem,
                                        device_id=device, device_id_type=MESH)


def _other_device(k):
    x, y, c = _place()
    px = 1 - x if k & 4 else x
    py = 1 - y if k & 2 else y
    pc = 1 - c if k & 1 else c
    return (px, py, pc), 4 * px + 2 * py + pc


def _direct_exchange(srcs, outs, send_sems, recv_sems, local_sems, gather):
    x, y, c = _place()
    me = 4 * x + 2 * y + c

    def copies(arriving):
        out_list = []
        for a, (src, out) in enumerate(zip(srcs, outs)):
            for k in range(1, N_DEV):
                peer, index = _other_device(k)
                mine = src if gather else src.at[index]
                out_list.append(_remote(mine, out.at[index if arriving else me], send_sems.at[a, k - 1],
                                        recv_sems.at[a, k - 1], peer))
        return out_list

    def local_copies():
        return [pltpu.make_async_copy(src if gather else src.at[me], out.at[me], local_sems.at[a])
                for a, (src, out) in enumerate(zip(srcs, outs))]

    def start():
        for cp in local_copies() + copies(False):
            cp.start()

    def wait():
        for cp in copies(True):
            cp.wait_recv()
        for cp in copies(False):
            cp.wait_send()
        for cp in local_copies():
            cp.wait()

    return start, wait


def _exchange_scratch(n):
    return [pltpu.SemaphoreType.DMA((n, N_DEV - 1)), pltpu.SemaphoreType.DMA((n, N_DEV - 1)), pltpu.SemaphoreType.DMA((n,))]


def _all_gather(shards):
    n = len(shards)

    def body(*refs):
        srcs, outs = refs[:n], refs[n:2 * n]
        send_sems, recv_sems, local_sems = refs[2 * n:]
        x, y, c = _place()
        me = 4 * x + 2 * y + c
        sibling = (x, y, 1 - c)
        local = [pltpu.make_async_copy(srcs[a], outs[a].at[me], local_sems.at[a]) for a in range(n)]
        for cp in local:
            cp.start()
        sends = []
        for a in range(n):
            sends.append(_remote(srcs[a], outs[a].at[me], send_sems.at[a, 0], recv_sems.at[a, 0], sibling))
        for k in range(1, N_CHIPS):
            px, py, _ = _other_chip(k)
            for a in range(n):
                sends.append(_remote(srcs[a], outs[a].at[me], send_sems.at[a, k], recv_sems.at[a, k], (px, py, c)))
        for cp in sends:
            cp.start()
        passed = []
        for k in range(1, N_CHIPS):
            px, py, _ = _other_chip(k)
            blk = 4 * px + 2 * py + c
            for a in range(n):
                _remote(srcs[a], outs[a].at[blk], send_sems.at[a, k], recv_sems.at[a, k], (px, py, c)).wait_recv()
            for a in range(n):
                cp = _remote(outs[a].at[blk], outs[a].at[blk], send_sems.at[a, 3 + k], recv_sems.at[a, 3 + k], sibling)
                cp.start()
                passed.append(cp)
        for a in range(n):
            _remote(srcs[a], outs[a].at[me + 1 - 2 * c], send_sems.at[a, 0], recv_sems.at[a, 0], sibling).wait_recv()
        for k in range(1, N_CHIPS):
            px, py, _ = _other_chip(k)
            blk = 4 * px + 2 * py + 1 - c
            for a in range(n):
                _remote(srcs[a], outs[a].at[blk], send_sems.at[a, 3 + k], recv_sems.at[a, 3 + k], sibling).wait_recv()
        for cp in sends + passed:
            cp.wait_send()
        for cp in local:
            cp.wait()

    return pl.pallas_call(
        body, name="all_gather_weights",
        out_shape=tuple(jax.ShapeDtypeStruct((N_DEV,) + a.shape, a.dtype) for a in shards),
        in_specs=[HBM_SPEC] * n, out_specs=(HBM_SPEC,) * n,
        scratch_shapes=[pltpu.SemaphoreType.DMA((n, N_DEV - 1)), pltpu.SemaphoreType.DMA((n, N_DEV - 1)),
                        pltpu.SemaphoreType.DMA((n,))],
    )(*shards)


def _reduce_exchange(by_device, small):
    _, rows, cols = by_device.shape

    def body(g_ref, small_ref, out_ref, small_out_ref, from_sibling, small_from_sibling, stage, sums, small_own, small_sum,
             pair_send, pair_recv, chip_send, chip_recv, local_sems):
        x, y, c = _place()
        mine = 2 * x + y
        sibling = (x, y, 1 - c)
        chips = [(x, y, mine)] + [_other_chip(k) for k in range(1, N_CHIPS)]
        to_sibling = [_remote(g_ref.at[2 * chips[k][2] + 1 - c], from_sibling.at[k], pair_send.at[k], pair_recv.at[k], sibling)
                      for k in range(N_CHIPS)]
        to_sibling.append(_remote(small_ref, small_from_sibling, pair_send.at[N_CHIPS], pair_recv.at[N_CHIPS], sibling))
        for cp in to_sibling:
            cp.start()
        small_mine = pltpu.make_async_copy(small_ref, small_own, local_sems.at[0])
        small_mine.start()
        to_chips = []
        for k in (1, 2, 3, 0):
            px, py, chip = chips[k]
            mine_k = pltpu.make_async_copy(g_ref.at[2 * chip + c], stage, local_sems.at[1])
            mine_k.start()
            to_sibling[k].wait_recv()
            mine_k.wait()
            sums[k] = (stage[...] + from_sibling[k]).astype(sums.dtype)
            if k:
                cp = _remote(sums.at[k], out_ref.at[mine], chip_send.at[0, k - 1], chip_recv.at[0, k - 1], (px, py, c))
                cp.start()
                to_chips.append(cp)
        own_block = pltpu.make_async_copy(sums.at[0], out_ref.at[mine], local_sems.at[2])
        own_block.start()
        to_sibling[N_CHIPS].wait_recv()
        small_mine.wait()
        small_sum[...] = small_own[...] + small_from_sibling[...]
        for k in range(1, N_CHIPS):
            px, py, _ = chips[k]
            cp = _remote(small_sum, small_out_ref.at[mine], chip_send.at[1, k - 1], chip_recv.at[1, k - 1], (px, py, c))
            cp.start()
            to_chips.append(cp)
        own_small = pltpu.make_async_copy(small_sum, small_out_ref.at[mine], local_sems.at[3])
        own_small.start()
        for k in range(1, N_CHIPS):
            px, py, chip = chips[k]
            _remote(sums.at[k], out_ref.at[chip], chip_send.at[0, k - 1], chip_recv.at[0, k - 1], (px, py, c)).wait_recv()
            _remote(small_sum, small_out_ref.at[chip], chip_send.at[1, k - 1], chip_recv.at[1, k - 1], (px, py, c)).wait_recv()
        for cp in to_sibling + to_chips:
            cp.wait_send()
        own_block.wait()
        own_small.wait()

    return pl.pallas_call(
        body, name="grad_reduce_exchange",
        out_shape=(jax.ShapeDtypeStruct((N_CHIPS, rows, cols), BF16), jax.ShapeDtypeStruct((N_CHIPS,) + small.shape, F32)),
        in_specs=[HBM_SPEC, HBM_SPEC], out_specs=(HBM_SPEC, HBM_SPEC),
        scratch_shapes=[pltpu.VMEM((N_CHIPS, rows, cols), F32), pltpu.VMEM(small.shape, F32), pltpu.VMEM((rows, cols), F32),
                        pltpu.VMEM((N_CHIPS, rows, cols), BF16), pltpu.VMEM(small.shape, F32), pltpu.VMEM(small.shape, F32),
                        pltpu.SemaphoreType.DMA((N_CHIPS + 1,)), pltpu.SemaphoreType.DMA((N_CHIPS + 1,)),
                        pltpu.SemaphoreType.DMA((2, N_CHIPS - 1)), pltpu.SemaphoreType.DMA((2, N_CHIPS - 1)),
                        pltpu.SemaphoreType.DMA((4,))],
        compiler_params=pltpu.CompilerParams(vmem_limit_bytes=VMEM_LIMIT),
    )(by_device, small)


def _params(n_axes):
    return pltpu.CompilerParams(dimension_semantics=("arbitrary",) * n_axes, vmem_limit_bytes=VMEM_LIMIT)


def _whole(shape):
    return pl.BlockSpec(shape, lambda *_: (0,) * len(shape))


VMEM_SPEC = pl.BlockSpec(memory_space=pltpu.VMEM)


def _inproj_fwd(x2, seq_len, norm_g, wt, wgt, sgu_weights, conv_w, later_shards):
    t = x2.shape[0]
    tm = min(512, seq_len)
    tiles_per_seq = seq_len // tm
    steps = t // tm
    ns = len(later_shards)

    widths = IN_GROUPS + (wgt.shape[0],)
    starts = (0, IN_GROUPS[0], IN_GROUPS[0] + IN_GROUPS[1], 0)

    def body(x_ref, g_ref, wt_ref, wg_ref, lg_ref, lb_ref, ws_ref, bt_ref, cw_ref, *rest):
        shard_refs, rest = rest[:ns], rest[ns:]
        a_ref, q_ref, z_ref, l_ref, sgu_ref, c_ref = rest[:6]
        gathered_refs, (xpad_ref, send_sems, recv_sems, local_sems) = rest[6:6 + ns], rest[6 + ns:]
        start_gather, wait_gather = _direct_exchange(shard_refs, gathered_refs, send_sems, recv_sems, local_sems, True)
        pl.when(pl.program_id(0) == 0)(start_gather)

        @pl.when(pl.program_id(0) % tiles_per_seq == 0)
        def _():
            xpad_ref[0:CONV_HALO, :] = jnp.zeros((CONV_HALO, xpad_ref.shape[1]), F32)

        n, _ = _rms(x_ref[...])
        xn = (n * g_ref[...]).astype(BF16)

        def project(w_ref, row0, width, o_ref):
            for c0 in range(0, width, 512):
                c1 = min(c0 + 512, width)
                o_ref[:, c0:c1] = lax.dot_general(xn, w_ref[row0 + c0:row0 + c1, :], (((1,), (1,)), ((), ())),
                                                  preferred_element_type=F32)

        def sgu_rows(row0):
            for grp in range(SGU_GROUPS):
                args = _sgu_pieces(a_ref, lg_ref, lb_ref, ws_ref, bt_ref, row0, grp)
                sgu_ref[pl.ds(row0, SGU_CHUNK), pl.ds(grp * 128, 128)] = _sgu_core(*args).astype(sgu_ref.dtype)

        def conv():
            xpad_ref[CONV_HALO:, :] = q_ref[...]
            acc = None
            padded = xpad_ref[...]
            for j in range(CONV_K):
                rows_up = CONV_HALO - CONV_K + 1 + j
                shifted = (padded[rows_up:rows_up + tm] if rows_up % 8 == 0
                           else pltpu.roll(padded, padded.shape[0] - rows_up, axis=0)[0:tm])
                term = cw_ref[j:j + 1, :] * shifted
                acc = term if acc is None else acc + term
            c_ref[...] = acc
            xpad_ref[0:CONV_HALO, :] = xpad_ref[tm:tm + CONV_HALO, :]

        groups = tuple(zip((wt_ref, wt_ref, wt_ref, wg_ref), starts, widths, (a_ref, q_ref, z_ref, l_ref)))
        row_chunks = list(range(0, tm, SGU_CHUNK))
        project(*groups[0])
        for row0 in row_chunks[:len(row_chunks) // 2]:
            sgu_rows(row0)
        project(*groups[1])
        for row0 in row_chunks[len(row_chunks) // 2:]:
            sgu_rows(row0)
        project(*groups[3])
        conv()
        project(*groups[2])
        pl.when(pl.program_id(0) == steps - 1)(wait_gather)

    tile = lambda w: pl.BlockSpec((tm, w), lambda i: (i, 0))
    sgu_shapes = ((1, SGU_WIDTH), (1, SGU_WIDTH), (SGU_GROUPS, SGU_CHUNK, SGU_CHUNK), (SGU_CHUNK, SGU_GROUPS))
    return pl.pallas_call(
        body, name="inproj_sgu_conv_fwd", grid=(steps,),
        out_shape=tuple(jax.ShapeDtypeStruct((t, w), F32) for w in widths)
        + (jax.ShapeDtypeStruct((t, SGU_WIDTH), BF16), jax.ShapeDtypeStruct((t, widths[1]), F32))
        + tuple(jax.ShapeDtypeStruct((N_DEV,) + a.shape, a.dtype) for a in later_shards),
        in_specs=[tile(D_MODEL), _whole((1, D_MODEL)), VMEM_SPEC, VMEM_SPEC]
        + [_whole(s) for s in sgu_shapes] + [_whole((CONV_K, widths[1]))] + [HBM_SPEC] * ns,
        out_specs=tuple(tile(w) for w in widths) + (tile(SGU_WIDTH), tile(widths[1])) + (HBM_SPEC,) * ns,
        scratch_shapes=[pltpu.VMEM((CONV_HALO + tm, widths[1]), F32)] + _exchange_scratch(ns),
        compiler_params=_params(1),
    )(x2, norm_g, wt, wgt, *sgu_weights, conv_w, *later_shards)


def _sgu_pieces(uvz_ref, lg_ref, lb_ref, ws_ref, bt_ref, row0, grp):
    rows = pl.ds(row0, SGU_CHUNK)
    lanes = pl.ds(grp * 128, 128)
    u = uvz_ref[rows, pl.ds(grp * 128, 128)]
    v = uvz_ref[rows, pl.ds(SGU_WIDTH + grp * 128, 128)]
    z = uvz_ref[rows, pl.ds(2 * SGU_WIDTH + grp * 128, 128)]
    bcol = jnp.sum(bt_ref[...] * _onehot_row(grp, SGU_GROUPS), axis=-1, keepdims=True)
    return u, v, z, lg_ref[:, lanes], lb_ref[:, lanes], ws_ref[grp], bcol


def _sgu_bwd_tile(uvz_ref, do_ref, sgu_refs, duvz_ref, grad_refs, pieces):
    lg_ref, lb_ref, ws_ref, bt_ref = sgu_refs
    dlg_ref, dlb_ref, dws_ref, dbt_ref = grad_refs
    for piece in pieces:
        row0, grp = piece // SGU_GROUPS * SGU_CHUNK, piece % SGU_GROUPS
        rows = pl.ds(row0, SGU_CHUNK)
        lanes = pl.ds(grp * 128, 128)
        args = _sgu_pieces(uvz_ref, lg_ref, lb_ref, ws_ref, bt_ref, row0, grp)
        _, pull = jax.vjp(_sgu_core, *args)
        du, dv, dz, dlg, dlb, dws, dbcol = pull(do_ref[rows, lanes])
        duvz_ref[rows, pl.ds(grp * 128, 128)] = du.astype(duvz_ref.dtype)
        duvz_ref[rows, pl.ds(SGU_WIDTH + grp * 128, 128)] = dv.astype(duvz_ref.dtype)
        duvz_ref[rows, pl.ds(2 * SGU_WIDTH + grp * 128, 128)] = dz.astype(duvz_ref.dtype)
        dlg_ref[:, lanes] += dlg
        dlb_ref[:, lanes] += dlb
        dws_ref[grp] += dws
        dbt_ref[...] += dbcol * _onehot_row(grp, SGU_GROUPS)


def _dn_pairs(nb):
    return [(b, h) for b in range(nb) for h in range(DN_HEADS)]


def _dn_batch_args(c_ref, z_ref, rows=slice(None)):
    pairs = _dn_pairs(c_ref.shape[0])
    pick = lambda ref, b, col: ref[b, rows, pl.ds(col, DN_HEAD_DIM)]
    cq = jnp.stack([pick(c_ref, b, h * DN_HEAD_DIM) for b, h in pairs])
    ck = jnp.stack([pick(c_ref, b, DN_WIDTH + h * DN_HEAD_DIM) for b, h in pairs])
    cv = jnp.stack([pick(c_ref, b, 2 * DN_WIDTH + h * DN_HEAD_DIM) for b, h in pairs])
    z = jnp.stack([pick(z_ref, b, h * DN_HEAD_DIM) for b, h in pairs])
    return cq, ck, cv, z


def _dn_weight_specs():
    return [_whole((CONV_K, 3 * DN_WIDTH)), _whole((1, GATE_PAD)), _whole((1, GATE_PAD)), _whole((1, DN_HEAD_DIM))]


def _dn_fwd(conv_out, zg, logits, alog, dtb, og):
    nb, s, _ = conv_out.shape
    nc = s // DN_CHUNK
    pairs = _dn_pairs(nb)
    gn = len(pairs)
    per_step = 2 if nc % 2 == 0 else 1
    chunk = lambda w: pl.BlockSpec((nb, per_step * DN_CHUNK, w), lambda n: (0, n, 0))

    def body(c_ref, z_ref, l_ref, alog_ref, dtb_ref, og_ref, out_ref, st_ref, inv_ref, state_ref):
        n = pl.program_id(0)

        @pl.when(n == 0)
        def _():
            state_ref[...] = jnp.zeros_like(state_ref)

        state = state_ref[...]
        for sub in range(per_step):
            rows = pl.ds(sub * DN_CHUNK, DN_CHUNK)
            cq, ck, cv, z = _dn_batch_args(c_ref, z_ref, rows)
            st_ref[sub] = state
            out, state, t = _dn_core(cq, ck, cv, z, l_ref[:, rows, :], state, alog_ref[...], dtb_ref[...], og_ref[...])
            inv_ref[sub] = t.astype(inv_ref.dtype)
            for i, (b, h) in enumerate(pairs):
                out_ref[b, rows, pl.ds(h * DN_HEAD_DIM, DN_HEAD_DIM)] = out[i].astype(out_ref.dtype)
        state_ref[...] = state

    per_chunk = pl.BlockSpec((per_step, gn, DN_HEAD_DIM, DN_HEAD_DIM), lambda n: (n, 0, 0, 0))
    return pl.pallas_call(
        body, name="deltanet_fwd", grid=(nc // per_step,),
        out_shape=(jax.ShapeDtypeStruct((nb, s, DN_WIDTH), BF16),
                   jax.ShapeDtypeStruct((nc, gn, DN_HEAD_DIM, DN_HEAD_DIM), F32),
                   jax.ShapeDtypeStruct((nc, gn, DN_CHUNK, DN_CHUNK), BF16)),
        in_specs=[chunk(3 * DN_WIDTH), chunk(DN_WIDTH), chunk(GATE_PAD)] + _dn_weight_specs()[1:],
        out_specs=(chunk(DN_WIDTH), per_chunk, pl.BlockSpec((per_step, gn, DN_CHUNK, DN_CHUNK), lambda n: (n, 0, 0, 0))),
        scratch_shapes=[pltpu.VMEM((gn, DN_HEAD_DIM, DN_HEAD_DIM), F32)],
        compiler_params=_params(1),
    )(conv_out, zg, logits, alog, dtb, og)


def _dn_bwd(qkv, conv_out, zg, logits, conv_w, alog, dtb, og, states, inverses, d_out, head_grads):
    nb, s, _ = qkv.shape
    nc = s // DN_CHUNK
    rev = lambda n: nc - 1 - n
    pairs = _dn_pairs(nb)
    gn = len(pairs)
    ng = len(head_grads)

    def body(cur_ref, c_ref, z_ref, l_ref, w_ref, alog_ref, dtb_ref, og_ref, st_ref, inv_ref, do_ref, *rest):
        grad_refs, rest = rest[:ng], rest[ng:]
        dqkv_ref, dz_ref, dl_ref, dw_ref, dalog_ref, ddtb_ref, dog_ref = rest[:7]
        recv_refs, (dstate_ref, dcpad_ref, dw_part_ref, send_sems, recv_sems, local_sems) = rest[7:7 + ng], rest[7 + ng:]
        n = pl.program_id(0)
        start_exchange, wait_exchange = _direct_exchange(grad_refs, recv_refs, send_sems, recv_sems, local_sems, False)
        pl.when(n == 0)(start_exchange)

        @pl.when(n == 0)
        def _():
            dw_part_ref[...] = jnp.zeros_like(dw_part_ref)
            dalog_ref[...] = jnp.zeros_like(dalog_ref)
            ddtb_ref[...] = jnp.zeros_like(ddtb_ref)
            dog_ref[...] = jnp.zeros_like(dog_ref)
            dstate_ref[...] = jnp.zeros_like(dstate_ref)
            dcpad_ref[:, DN_CHUNK:, :] = jnp.zeros((nb, CONV_HALO, 3 * DN_WIDTH), F32)

        cq, ck, cv, z = _dn_batch_args(c_ref, z_ref)
        d_out_g = jnp.stack([do_ref[b, :, pl.ds(h * DN_HEAD_DIM, DN_HEAD_DIM)] for b, h in pairs])
        t_known = inv_ref[...].astype(F32)
        core = lambda *args: _dn_core(*args, t_known=t_known)[:2]
        _, pull = jax.vjp(core, cq, ck, cv, z, l_ref[...], st_ref[...], alog_ref[...], dtb_ref[...], og_ref[...])
        dcq, dck, dcv, dz, dlog, dstate, dalog, ddtb, dog = pull((d_out_g, dstate_ref[...]))
        dstate_ref[...] = dstate
        dl_ref[...] = dlog.astype(dl_ref.dtype)
        dalog_ref[...] += dalog
        ddtb_ref[...] += ddtb
        dog_ref[...] += dog
        for i, (b, h) in enumerate(pairs):
            dcpad_ref[b, 0:DN_CHUNK, pl.ds(h * DN_HEAD_DIM, DN_HEAD_DIM)] = dcq[i]
            dcpad_ref[b, 0:DN_CHUNK, pl.ds(DN_WIDTH + h * DN_HEAD_DIM, DN_HEAD_DIM)] = dck[i]
            dcpad_ref[b, 0:DN_CHUNK, pl.ds(2 * DN_WIDTH + h * DN_HEAD_DIM, DN_HEAD_DIM)] = dcv[i]
            dz_ref[b, :, pl.ds(h * DN_HEAD_DIM, DN_HEAD_DIM)] = dz[i].astype(dz_ref.dtype)
        for b in range(nb):
            for c0 in range(0, 3 * DN_WIDTH, CONV_STRIP):
                cols = pl.ds(c0, CONV_STRIP)
                xb = cur_ref[b, :, cols]
                dx = None
                padded = dcpad_ref[b, :, cols]
                for j in range(CONV_K):
                    rows_up = CONV_K - 1 - j
                    shifted = (pltpu.roll(padded, padded.shape[0] - rows_up, axis=0) if rows_up else padded)[0:DN_CHUNK]
                    term = w_ref[j:j + 1, cols] * shifted
                    dx = term if dx is None else dx + term
                    dw_part_ref[j, :, cols] += jnp.sum((shifted * xb).reshape(DN_CHUNK // 8, 8, CONV_STRIP), axis=0)
                dqkv_ref[b, :, cols] = dx.astype(dqkv_ref.dtype)
            dcpad_ref[b, DN_CHUNK:, :] = dcpad_ref[b, 0:CONV_HALO, :]

        @pl.when(n == nc - 1)
        def _():
            dw_ref[...] = jnp.sum(dw_part_ref[...], axis=1)

        pl.when(n == nc - 1)(wait_exchange)

    chunk = lambda w: pl.BlockSpec((nb, DN_CHUNK, w), lambda n: (0, rev(n), 0))
    return pl.pallas_call(
        body, name="deltanet_bwd", grid=(nc,),
        out_shape=(jax.ShapeDtypeStruct((nb, s, 3 * DN_WIDTH), BF16), jax.ShapeDtypeStruct((nb, s, DN_WIDTH), BF16),
                   jax.ShapeDtypeStruct((nb, s, GATE_PAD), BF16), jax.ShapeDtypeStruct((CONV_K, 3 * DN_WIDTH), F32),
                   jax.ShapeDtypeStruct((1, GATE_PAD), F32), jax.ShapeDtypeStruct((1, GATE_PAD), F32),
                   jax.ShapeDtypeStruct((1, DN_HEAD_DIM), F32))
        + tuple(jax.ShapeDtypeStruct(a.shape, a.dtype) for a in head_grads),
        in_specs=[chunk(3 * DN_WIDTH), chunk(3 * DN_WIDTH), chunk(DN_WIDTH), chunk(GATE_PAD)] + _dn_weight_specs() + [
            pl.BlockSpec((None, gn, DN_HEAD_DIM, DN_HEAD_DIM), lambda n: (rev(n), 0, 0, 0)),
            pl.BlockSpec((None, gn, DN_CHUNK, DN_CHUNK), lambda n: (rev(n), 0, 0, 0)),
            chunk(DN_WIDTH)] + [HBM_SPEC] * ng,
        out_specs=(chunk(3 * DN_WIDTH), chunk(DN_WIDTH), chunk(GATE_PAD), _whole((CONV_K, 3 * DN_WIDTH)),
                   _whole((1, GATE_PAD)), _whole((1, GATE_PAD)), _whole((1, DN_HEAD_DIM))) + (HBM_SPEC,) * ng,
        scratch_shapes=[pltpu.VMEM((gn, DN_HEAD_DIM, DN_HEAD_DIM), F32),
                        pltpu.VMEM((nb, DN_CHUNK + CONV_HALO, 3 * DN_WIDTH), F32),
                        pltpu.VMEM((CONV_K, 8, 3 * DN_WIDTH), F32)] + _exchange_scratch(ng),
        compiler_params=_params(1),
    )(qkv, conv_out, zg, logits, conv_w, alog, dtb, og, states, inverses, d_out, *head_grads)


def _head(a_out, b_out, x2, p2, target, w_out, w_gate, w_proj, ple_g, fin_g):
    t = x2.shape[0]
    tm = min(512, t)
    steps = t // tm

    def body(a_ref, b_ref, x_ref, p_ref, y_ref, wo_ref, wg_ref, wp_ref, pg_ref, fg_ref,
             da_ref, db_ref, dh_ref, dwo_hbm, dwg_hbm, dwp_hbm, dpg_ref, dfg_ref, loss_ref,
             dwo_acc, dwg_acc, dwp_acc, rows_stage, cols_stage):
        i = pl.program_id(0)

        @pl.when(i == 0)
        def _():
            dwo_acc[...] = jnp.zeros_like(dwo_acc)
            dwg_acc[...] = jnp.zeros_like(dwg_acc)
            dwp_acc[...] = jnp.zeros_like(dwp_acc)
            dpg_ref[...] = jnp.zeros_like(dpg_ref)
            dfg_ref[...] = jnp.zeros_like(dfg_ref)
            loss_ref[...] = jnp.zeros_like(loss_ref)

        pg = pg_ref[...]
        fg = fg_ref[...]
        nt = (((1,), (1,)), ((), ()))
        tn = (((0,), (0,)), ((), ()))

        def to_first_norm(rows):
            h1 = (x_ref[rows, :] + jnp.dot(a_ref[rows, :], wo_ref[0:SGU_WIDTH, :], preferred_element_type=F32)
                  + jnp.dot(b_ref[rows, :], wo_ref[SGU_WIDTH:, :], preferred_element_type=F32))
            n1, r1 = _rms(h1)
            pp = jnp.dot(p_ref[rows, :].astype(BF16), wp_ref[...], preferred_element_type=F32)
            return h1, n1, r1, (n1 * pg).astype(BF16), pp

        def to_gate_cotangents(rows, h1, rn, pp):
            gate = _sigmoid(jnp.dot(rn, wg_ref[...], preferred_element_type=F32))
            h2 = h1 + gate * pp
            n2, r2 = _rms(h2)
            err = n2 * fg - y_ref[rows, :]
            loss = _rowsum(jnp.sum(err * err, axis=-1, keepdims=True))
            dy = err * (1.0 / D_MODEL)
            dh2 = _rms_bwd(dy * fg, n2, r2)
            return loss, _rowsum(dy * n2), dh2, (dh2 * gate).astype(BF16), (dh2 * pp * gate * (1.0 - gate)).astype(BF16)

        def to_branch_cotangents(rows, dgl, dh2, n1, r1):
            drn = lax.dot_general(dgl, wg_ref[...], nt, preferred_element_type=F32)
            dh1 = dh2 + _rms_bwd(drn * pg, n1, r1)
            dh_ref[rows, :] = dh1
            dhb = dh1.astype(BF16)
            da_ref[rows, :] = lax.dot_general(dhb, wo_ref[0:SGU_WIDTH, :], nt, preferred_element_type=F32)
            db_ref[rows, :] = lax.dot_general(dhb, wo_ref[SGU_WIDTH:, :], nt, preferred_element_type=F32)
            return _rowsum(drn * n1), dhb

        parts = [pl.ds(k * (tm // 2), tm // 2) for k in range(2)]
        first = [to_first_norm(rows) for rows in parts]
        mid = [to_gate_cotangents(rows, h1, rn, pp) for rows, (h1, _, _, rn, pp) in zip(parts, first)]
        loss_ref[...] += jnp.broadcast_to(mid[0][0] + mid[1][0], loss_ref.shape)
        dfg_ref[...] += mid[0][1] + mid[1][1]
        last = [to_branch_cotangents(rows, m[4], m[2], f[1], f[2]) for rows, m, f in zip(parts, mid, first)]
        rn = jnp.concatenate([f[3] for f in first], axis=0)
        dpp = jnp.concatenate([m[3] for m in mid], axis=0)
        dgl = jnp.concatenate([m[4] for m in mid], axis=0)
        dwp_acc[...] += lax.dot_general(p_ref[...].astype(BF16), dpp, tn, preferred_element_type=F32)
        dwg_acc[...] += lax.dot_general(rn, dgl, tn, preferred_element_type=F32)
        dpg_ref[...] += last[0][0] + last[1][0]
        dhb = jnp.concatenate([l[1] for l in last], axis=0)
        dwo_acc[0:SGU_WIDTH, :] += lax.dot_general(a_ref[...], dhb, tn, preferred_element_type=F32)
        dwo_acc[SGU_WIDTH:, :] += lax.dot_general(b_ref[...], dhb, tn, preferred_element_type=F32)

        @pl.when(i == steps - 1)
        def _():
            for j in range(N_DEV):
                for acc, hbm in ((dwo_acc, dwo_hbm), (dwg_acc, dwg_hbm)):
                    rows_stage[...] = acc[j * LANES:(j + 1) * LANES, :].astype(BF16)
                    pltpu.sync_copy(rows_stage, hbm.at[j])
                cols_stage[...] = dwp_acc[:, j * LANES:(j + 1) * LANES].astype(BF16)
                pltpu.sync_copy(cols_stage, dwp_hbm.at[j])

    tile = lambda w: pl.BlockSpec((tm, w), lambda i: (i, 0))
    return pl.pallas_call(
        body, name="head_fwd_bwd", grid=(steps,),
        out_shape=(jax.ShapeDtypeStruct((t, SGU_WIDTH), F32), jax.ShapeDtypeStruct((t, DN_WIDTH), F32),
                   jax.ShapeDtypeStruct((t, D_MODEL), F32), jax.ShapeDtypeStruct((N_DEV, LANES, D_MODEL), BF16),
                   jax.ShapeDtypeStruct((N_DEV, LANES, D_MODEL), BF16), jax.ShapeDtypeStruct((N_DEV, PLE_DIM, LANES), BF16),
                   jax.ShapeDtypeStruct((1, D_MODEL), F32), jax.ShapeDtypeStruct((1, D_MODEL), F32),
                   jax.ShapeDtypeStruct((8, LANES), F32)),
        in_specs=[tile(SGU_WIDTH), tile(DN_WIDTH), tile(D_MODEL), tile(PLE_DIM), tile(D_MODEL),
                  VMEM_SPEC, VMEM_SPEC, VMEM_SPEC, _whole((1, D_MODEL)), _whole((1, D_MODEL))],
        out_specs=(tile(SGU_WIDTH), tile(DN_WIDTH), tile(D_MODEL), HBM_SPEC, HBM_SPEC, HBM_SPEC,
                   _whole((1, D_MODEL)), _whole((1, D_MODEL)), _whole((8, LANES))),
        scratch_shapes=[pltpu.VMEM((D_MODEL, D_MODEL), F32), pltpu.VMEM((D_MODEL, D_MODEL), F32),
                        pltpu.VMEM((PLE_DIM, D_MODEL), F32), pltpu.VMEM((LANES, D_MODEL), BF16),
                        pltpu.VMEM((PLE_DIM, LANES), BF16)],
        compiler_params=_params(1),
    )(a_out, b_out, x2, p2, target, w_out, w_gate, w_proj, ple_g, fin_g)


def _inproj_bwd(x2, dh1, a_uvz, d_sgu, d_q, d_z, d_l, norm_g, sgu_weights, wt, wgt):
    t = x2.shape[0]
    tm = min(256, t)
    steps = t // tm

    widths = (a_uvz.shape[1], d_q.shape[1], d_z.shape[1], d_l.shape[1])
    starts = (0, widths[0], widths[0] + widths[1], widths[0] + widths[1] + widths[2])

    def body(x_ref, dh_ref, uvz_ref, dsgu_ref, dq_ref, dz_ref, dl_ref, g_ref, lg_ref, lb_ref, ws_ref, bt_ref,
             wt_ref, wgt_ref,
             dx_ref, dw_hbm, dg_ref, dlg_ref, dlb_ref, dws_ref, dbt_ref, dw_acc, stage_ref, da_ref):
        i = pl.program_id(0)

        @pl.when(i == 0)
        def _():
            dw_acc[...] = jnp.zeros_like(dw_acc)
            for ref in (dg_ref, dlg_ref, dlb_ref, dws_ref, dbt_ref):
                ref[...] = jnp.zeros_like(ref)

        g = g_ref[...]
        n, r = _rms(x_ref[...])
        xn = (n * g).astype(BF16)
        dxn = None
        sgu_done = 0

        def sgu_pieces(count):
            nonlocal sgu_done
            _sgu_bwd_tile(uvz_ref, dsgu_ref, (lg_ref, lb_ref, ws_ref, bt_ref), da_ref,
                          (dlg_ref, dlb_ref, dws_ref, dbt_ref), range(sgu_done, sgu_done + count))
            sgu_done += count

        sgu_total = tm // SGU_CHUNK * SGU_GROUPS
        before_q, after_q_chunk = sgu_total // 2, (sgu_total // 4, sgu_total // 8, sgu_total // 8)
        for d_ref, col0 in reversed(tuple(zip((da_ref, dq_ref, dz_ref, dl_ref), starts))):
            if d_ref is dq_ref:
                sgu_pieces(before_q)
            if d_ref is da_ref:
                sgu_pieces(sgu_total - sgu_done)
            width = d_ref.shape[1]
            rows = wgt_ref[...] if d_ref is dl_ref else wt_ref[col0:col0 + width, :]
            term = jnp.dot(d_ref[...], rows, preferred_element_type=F32)
            dxn = term if dxn is None else dxn + term
            for c0 in range(0, width, 512):
                c1 = min(c0 + 512, width)
                dw_acc[col0 + c0:col0 + c1, :] += lax.dot_general(d_ref[:, c0:c1], xn, (((0,), (0,)), ((), ())),
                                                                  preferred_element_type=F32)
                if d_ref is dq_ref:
                    sgu_pieces(after_q_chunk[c0 // 512])
        dg_ref[...] += _rowsum(dxn * n)
        dx_ref[...] = dh_ref[...] + _rms_bwd(dxn * g, n, r)

        @pl.when(i == steps - 1)
        def _():
            for j in range(N_DEV):
                stage_ref[...] = dw_acc[j * IN_SHARD:(j + 1) * IN_SHARD, :]
                pltpu.sync_copy(stage_ref, dw_hbm.at[j])

    tile = lambda w: pl.BlockSpec((tm, w), lambda i: (i, 0))
    sgu_shapes = ((1, SGU_WIDTH), (1, SGU_WIDTH), (SGU_GROUPS, SGU_CHUNK, SGU_CHUNK), (SGU_CHUNK, SGU_GROUPS))
    return pl.pallas_call(
        body, name="inproj_sgu_bwd", grid=(steps,),
        out_shape=(jax.ShapeDtypeStruct((t, D_MODEL), F32), jax.ShapeDtypeStruct((N_DEV, IN_SHARD, D_MODEL), F32),
                   jax.ShapeDtypeStruct((1, D_MODEL), F32)) + tuple(jax.ShapeDtypeStruct(s, F32) for s in sgu_shapes),
        in_specs=[tile(D_MODEL), tile(D_MODEL), tile(widths[0]), tile(SGU_WIDTH)] + [tile(w) for w in widths[1:]]
        + [_whole((1, D_MODEL))] + [_whole(s) for s in sgu_shapes] + [VMEM_SPEC] * 2,
        out_specs=(tile(D_MODEL), HBM_SPEC, _whole((1, D_MODEL))) + tuple(_whole(s) for s in sgu_shapes),
        scratch_shapes=[pltpu.VMEM((sum(widths), D_MODEL), F32), pltpu.VMEM((IN_SHARD, D_MODEL), F32),
                        pltpu.VMEM((tm, widths[0]), BF16)],
        compiler_params=_params(1),
    )(x2, dh1, a_uvz, d_sgu, d_q, d_z, d_l, norm_g, *sgu_weights, wt, wgt)


def _reduce_adamw(recv, w, m, v, name, col_block=None):
    n, rows, cols = recv.shape
    cb = col_block or cols
    lead = w.ndim - 2

    def body(r_ref, w_ref, m_ref, v_ref, g_ref, d_ref, nm_ref, nv_ref):
        g = r_ref[0].astype(F32)
        for i in range(1, n):
            g = g + r_ref[i].astype(F32)
        m_new = ADAM_B1 * m_ref[...] + (1.0 - ADAM_B1) * g
        v_new = ADAM_B2 * v_ref[...] + (1.0 - ADAM_B2) * jnp.square(g)
        m_hat = m_new / (1.0 - ADAM_B1 ** ADAM_STEP)
        v_hat = v_new / (1.0 - ADAM_B2 ** ADAM_STEP)
        g_ref[...] = g
        d_ref[...] = -ADAM_LR * (m_hat / (jnp.sqrt(v_hat) + ADAM_EPS) + ADAM_WD * w_ref[...])
        nm_ref[...] = m_new
        nv_ref[...] = v_new

    blk = pl.BlockSpec((None,) * lead + (rows, cb), lambda i: (0,) * lead + (0, i))
    return pl.pallas_call(
        body, name=name, grid=(cols // cb,),
        out_shape=tuple(jax.ShapeDtypeStruct(w.shape, F32) for _ in range(4)),
        in_specs=[pl.BlockSpec((n, rows, cb), lambda i: (0, 0, i)), blk, blk, blk],
        out_specs=(blk, blk, blk, blk),
        compiler_params=_params(1),
    )(recv, w, m, v)


def _adamw_replicated(received, ws, ms, vs):
    nw = len(ws)
    starts = [sum(SMALL_PIECE_ROWS[:i]) for i in range(len(SMALL_PIECE_ROWS))]

    def natural(g_ref, row0, shape):
        cols, rows = shape[-1], _size(shape[:-1])
        if cols == LANES:
            return g_ref[row0:row0 + rows, :].reshape(shape)
        if cols < LANES:
            return g_ref[row0:row0 + 1, 0:cols].reshape(shape)
        per = cols // LANES
        return jnp.concatenate(
            [jnp.concatenate([g_ref[row0 + r * per + k:row0 + r * per + k + 1, :] for k in range(per)], axis=1)
             for r in range(rows)], axis=0).reshape(shape)

    def body(r_ref, *refs):
        w_refs, m_refs, v_refs = refs[:nw], refs[nw:2 * nw], refs[2 * nw:3 * nw]
        conv_ref, loss_ref = refs[3 * nw], refs[3 * nw + 1]
        out_refs, g_ref = refs[3 * nw + 2:-1], refs[-1]
        g = r_ref[0]
        for q in range(1, N_CHIPS):
            g = g + r_ref[q]
        g_ref[...] = g
        conv_ref[...] = natural(g_ref, starts[0], (CONV_K, 3 * DN_WIDTH))
        loss_ref[...] = natural(g_ref, starts[-1], (1, 1))
        for i in range(nw):
            gi = natural(g_ref, starts[1 + i], w_refs[i].shape)
            m_new = ADAM_B1 * m_refs[i][...] + (1.0 - ADAM_B1) * gi
            v_new = ADAM_B2 * v_refs[i][...] + (1.0 - ADAM_B2) * jnp.square(gi)
            m_hat = m_new / (1.0 - ADAM_B1 ** ADAM_STEP)
            v_hat = v_new / (1.0 - ADAM_B2 ** ADAM_STEP)
            out_refs[4 * i][...] = gi
            out_refs[4 * i + 1][...] = -ADAM_LR * (m_hat / (jnp.sqrt(v_hat) + ADAM_EPS) + ADAM_WD * w_refs[i][...])
            out_refs[4 * i + 2][...] = m_new
            out_refs[4 * i + 3][...] = v_new

    def spec(a):
        lead = max(a.ndim - 3, 0)
        return pl.BlockSpec((None,) * lead + a.shape[lead:], lambda: (0,) * a.ndim)

    weight_specs = [spec(a) for a in ws]
    return pl.pallas_call(
        body, name="adamw_replicated",
        out_shape=(jax.ShapeDtypeStruct((CONV_K, 3 * DN_WIDTH), F32), jax.ShapeDtypeStruct((1, 1), F32))
        + tuple(jax.ShapeDtypeStruct(a.shape, F32) for a in ws for _ in range(4)),
        in_specs=[pl.BlockSpec(received.shape, lambda: (0, 0, 0))] + weight_specs * 3,
        out_specs=(pl.BlockSpec((CONV_K, 3 * DN_WIDTH), lambda: (0, 0)), pl.BlockSpec((1, 1), lambda: (0, 0)))
        + tuple(s for s in weight_specs for _ in range(4)),
        scratch_shapes=[pltpu.VMEM(received.shape[1:], F32)],
        compiler_params=pltpu.CompilerParams(vmem_limit_bytes=VMEM_LIMIT),
    )(received, *ws, *ms, *vs)


def _pack_rows(pieces, rows):
    padded = [jnp.pad(jnp.ravel(p), (0, -p.size % LANES)) for p in pieces]
    flat = jnp.concatenate(padded)
    return jnp.pad(flat, (0, rows * LANES - flat.shape[0])).reshape(rows, LANES)


def kernel(x, p, norm_g, w_in, sgu_ln_g, sgu_ln_b, sgu_w_s, sgu_b_s, dn_conv_w, dn_a_log, dn_dt_bias, dn_o_norm_g, w_out, ple_norm_g, ple_gate_w, ple_proj_w, final_norm_g, loss_target, m_norm_g, m_w_in, m_sgu_ln_g, m_sgu_ln_b, m_sgu_w_s, m_sgu_b_s, m_dn_conv_w, m_dn_a_log, m_dn_dt_bias, m_dn_o_norm_g, m_w_out, m_ple_norm_g, m_ple_gate_w, m_ple_proj_w, m_final_norm_g, v_norm_g, v_w_in, v_sgu_ln_g, v_sgu_ln_b, v_sgu_w_s, v_sgu_b_s, v_dn_conv_w, v_dn_a_log, v_dn_dt_bias, v_dn_o_norm_g, v_w_out, v_ple_norm_g, v_ple_gate_w, v_ple_proj_w, v_final_norm_g):
    weights = dict(norm_g=norm_g, w_in=w_in, sgu_ln_g=sgu_ln_g, sgu_ln_b=sgu_ln_b, sgu_w_s=sgu_w_s, sgu_b_s=sgu_b_s,
                   dn_conv_w=dn_conv_w, dn_a_log=dn_a_log, dn_dt_bias=dn_dt_bias, dn_o_norm_g=dn_o_norm_g, w_out=w_out,
                   ple_norm_g=ple_norm_g, ple_gate_w=ple_gate_w, ple_proj_w=ple_proj_w, final_norm_g=final_norm_g)
    mom1 = dict(norm_g=m_norm_g, w_in=m_w_in, sgu_ln_g=m_sgu_ln_g, sgu_ln_b=m_sgu_ln_b, sgu_w_s=m_sgu_w_s,
                sgu_b_s=m_sgu_b_s, dn_conv_w=m_dn_conv_w, dn_a_log=m_dn_a_log, dn_dt_bias=m_dn_dt_bias,
                dn_o_norm_g=m_dn_o_norm_g, w_out=m_w_out, ple_norm_g=m_ple_norm_g, ple_gate_w=m_ple_gate_w,
                ple_proj_w=m_ple_proj_w, final_norm_g=m_final_norm_g)
    mom2 = dict(norm_g=v_norm_g, w_in=v_w_in, sgu_ln_g=v_sgu_ln_g, sgu_ln_b=v_sgu_ln_b, sgu_w_s=v_sgu_w_s,
                sgu_b_s=v_sgu_b_s, dn_conv_w=v_dn_conv_w, dn_a_log=v_dn_a_log, dn_dt_bias=v_dn_dt_bias,
                dn_o_norm_g=v_dn_o_norm_g, w_out=v_w_out, ple_norm_g=v_ple_norm_g, ple_gate_w=v_ple_gate_w,
                ple_proj_w=v_ple_proj_w, final_norm_g=v_final_norm_g)
    nb, s, _ = x.shape
    t = nb * s

    transposed = lambda a: jnp.transpose(a, (2, 0, 1)).reshape(IN_SHARD, D_MODEL)
    w_in_t, m_in_t, v_in_t = transposed(w_in), transposed(m_w_in), transposed(v_w_in)
    w_in_blocks, conv_blocks = _all_gather([w_in_t.astype(BF16), dn_conv_w[0]])
    w_in_full_t = w_in_blocks.reshape(IN_COLS, D_MODEL)
    wgt = jnp.pad(w_in_full_t[sum(IN_GROUPS):], ((0, GATE_PAD - 2 * DN_HEADS), (0, 0)))
    conv_full = jnp.moveaxis(conv_blocks, 0, 1).reshape(CONV_K, 3 * DN_WIDTH)
    later_shards = [w_out[0].astype(BF16), ple_gate_w[0].astype(BF16), ple_proj_w[0].astype(BF16)]

    pad_row = lambda a: jnp.pad(a.reshape(1, -1), ((0, 0), (DN_HEADS, GATE_PAD - DN_HEADS - a.size)))
    alog, dtb = pad_row(dn_a_log), pad_row(dn_dt_bias)
    og = dn_o_norm_g.reshape(1, DN_HEAD_DIM)
    ws = sgu_w_s.reshape(SGU_GROUPS, SGU_CHUNK, SGU_CHUNK)
    b_t = sgu_b_s.reshape(SGU_GROUPS, SGU_CHUNK).T
    fin_g = final_norm_g.reshape(1, D_MODEL)

    x2 = x.reshape(t, D_MODEL)
    sgu_weights = (sgu_ln_g, sgu_ln_b, ws, b_t)
    a_uvz, b_qkv, b_z, b_l, a_out, conv_out, w_out_blocks, w_gate_blocks, w_proj_blocks = _inproj_fwd(
        x2, s, norm_g, w_in_full_t, wgt, sgu_weights, conv_full, later_shards)
    w_out_full = w_out_blocks.reshape(D_MODEL, D_MODEL)
    w_gate_full = w_gate_blocks.reshape(D_MODEL, D_MODEL)
    w_proj_full = jnp.moveaxis(w_proj_blocks, 0, 1).reshape(PLE_DIM, D_MODEL)
    qkv3 = b_qkv.reshape(nb, s, 3 * DN_WIDTH)
    conv_out = conv_out.reshape(nb, s, 3 * DN_WIDTH)
    z3 = b_z.reshape(nb, s, DN_WIDTH)
    l3 = b_l.reshape(nb, s, GATE_PAD)
    b_out, states, inverses = _dn_fwd(conv_out, z3, l3, alog, dtb, og)

    d_a, d_b, dh1, g_w_out, g_gate, g_proj, g_ple_g, g_fin_g, loss_tile = _head(
        a_out, b_out.reshape(t, DN_WIDTH), x2, p.reshape(t, PLE_DIM), loss_target.reshape(t, D_MODEL),
        w_out_full, w_gate_full, w_proj_full, ple_norm_g, fin_g)
    d_qkv, d_z, d_l, g_conv, g_alog, g_dtb, g_og, *head_received = _dn_bwd(
        qkv3, conv_out, z3, l3, conv_full, alog, dtb, og, states, inverses, d_b.reshape(nb, s, DN_WIDTH),
        [g_w_out, g_gate, g_proj])
    grad_x, g_w_in, g_norm, g_ln_g, g_ln_b, g_ws, g_bt = _inproj_bwd(
        x2, dh1, a_uvz, d_a, d_qkv.reshape(t, 3 * DN_WIDTH), d_z.reshape(t, DN_WIDTH), d_l.reshape(t, GATE_PAD),
        norm_g, sgu_weights, w_in_full_t, wgt)

    small = _pack_rows([g_conv, g_norm, g_ln_g, g_ln_b, g_ws, g_bt.T, g_alog[:, DN_HEADS:2 * DN_HEADS], g_dtb[:, DN_HEADS:2 * DN_HEADS], g_og,
                        g_ple_g, g_fin_g, (0.5 / D_MODEL) * loss_tile[0:1, 0:1]], SMALL_ROWS)
    w_in_received, small_received = _reduce_exchange(g_w_in, small)

    results = {}
    outs = _reduce_adamw(w_in_received, w_in_t, m_in_t, v_in_t, "adamw_w_in", 4 * LANES)
    results["w_in"] = [jnp.transpose(a.reshape(IN_SHARD, 1, D_MODEL), (1, 2, 0)) for a in outs]
    for name, recv in zip(("w_out", "ple_gate_w", "ple_proj_w"), head_received):
        results[name] = _reduce_adamw(recv, weights[name], mom1[name], mom2[name], "adamw_" + name)
    names = [name for name, _ in REPLICATED]
    two_d = lambda a: a.reshape(1, -1) if a.ndim == 1 else a
    g_conv_sum, loss_sum, *flat_outs = _adamw_replicated(
        small_received, *[[two_d(src[k]) for k in names] for src in (weights, mom1, mom2)])
    for i, k in enumerate(names):
        results[k] = [a.reshape(weights[k].shape) for a in flat_outs[4 * i:4 * i + 4]]
    loss = loss_sum[0, 0]
    me = 4 * lax.axis_index("x") + 2 * lax.axis_index("y") + lax.axis_index("c")
    conv_mine = lax.dynamic_slice(g_conv_sum, (0, me * 192), (CONV_K, 192))
    results["dn_conv_w"] = _reduce_adamw(conv_mine[None], dn_conv_w, m_dn_conv_w, v_dn_conv_w, "adamw_dn_conv_w")

    return (loss, grad_x.reshape(nb, s, D_MODEL), *[results[k][0] for k in WEIGHT_ORDER],
            *[results[k][1] for k in WEIGHT_ORDER], *[results[k][2] for k in WEIGHT_ORDER],
            *[results[k][3] for k in WEIGHT_ORDER])
```

```python
import functools

import jax
import jax.numpy as jnp
from jax import lax
from jax.experimental import pallas as pl
from jax.experimental.pallas import tpu as pltpu

F32 = jnp.float32
BF16 = jnp.bfloat16

N_DEV = 8
D_MODEL = 1024
SGU_WIDTH = 512
SGU_GROUPS = 4
SGU_CHUNK = 128
CONV_STRIP = 256
DN_WIDTH = 512
DN_HEADS = 4
DN_HEAD_DIM = 128
DN_CHUNK = 128
CONV_K = 4
CONV_HALO = 8
PLE_DIM = 256
EPS = 1e-6
IN_COLS = 3592
IN_SHARD = IN_COLS // N_DEV
GATE_PAD = 128
IN_GROUPS = (3 * SGU_WIDTH, 3 * DN_WIDTH, DN_WIDTH)

ADAM_LR = 0.001
ADAM_B1 = 0.9
ADAM_B2 = 0.999
ADAM_EPS = 1e-08
ADAM_WD = 0.01
ADAM_STEP = 10

LANES = 128
VMEM_LIMIT = 56 * 1024 * 1024
MESH = pl.DeviceIdType.MESH

REPLICATED = (("norm_g", (1, D_MODEL)), ("sgu_ln_g", (1, SGU_WIDTH)), ("sgu_ln_b", (1, SGU_WIDTH)),
              ("sgu_w_s", (1, SGU_GROUPS, SGU_CHUNK, SGU_CHUNK)), ("sgu_b_s", (1, SGU_GROUPS, SGU_CHUNK)),
              ("dn_a_log", (1, DN_HEADS)), ("dn_dt_bias", (1, DN_HEADS)), ("dn_o_norm_g", (1, DN_HEAD_DIM)),
              ("ple_norm_g", (1, D_MODEL)), ("final_norm_g", (D_MODEL,)))
WEIGHT_ORDER = ("norm_g", "w_in", "sgu_ln_g", "sgu_ln_b", "sgu_w_s", "sgu_b_s", "dn_conv_w", "dn_a_log",
                "dn_dt_bias", "dn_o_norm_g", "w_out", "ple_norm_g", "ple_gate_w", "ple_proj_w", "final_norm_g")


def _size(shape):
    n = 1
    for s in shape:
        n *= s
    return n


SMALL_LAYOUT = (("conv", (CONV_K, 3 * DN_WIDTH)),) + REPLICATED + (("loss", (1,)),)
SMALL_PIECE_ROWS = tuple(-(-_size(s) // LANES) for _, s in SMALL_LAYOUT)
SMALL_ROWS = -(-sum(SMALL_PIECE_ROWS) // 8) * 8


def _bdot(a, b):
    return jnp.dot(a.astype(BF16), b.astype(BF16), preferred_element_type=F32)


def _sigmoid(x):
    return 0.5 * jnp.tanh(0.5 * x) + 0.5


@jax.custom_vjp
def _silu(x):
    return x * _sigmoid(x)


def _silu_fwd(x):
    s = _sigmoid(x)
    return x * s, (x, s)


def _silu_bwd(res, ct):
    x, s = res
    return (ct * (s * (1.0 + x * (1.0 - s))),)


_silu.defvjp(_silu_fwd, _silu_bwd)


def _normal_cdf(x):
    return 0.5 + 0.5 * lax.erf(x * (0.5 ** 0.5))


@jax.custom_vjp
def _gelu(x):
    return x * _normal_cdf(x)


def _gelu_fwd(x):
    cdf = _normal_cdf(x)
    return x * cdf, (x, cdf)


def _gelu_bwd(res, ct):
    x, cdf = res
    pdf = jnp.exp(-0.5 * x * x) * ((2.0 * jnp.pi) ** -0.5)
    return (ct * (cdf + x * pdf),)


_gelu.defvjp(_gelu_fwd, _gelu_bwd)


def _softplus(x):
    return jnp.maximum(x, 0.0) + jnp.log1p(jnp.exp(-jnp.abs(x)))


@jax.custom_vjp
def _l2n(x):
    return x * lax.rsqrt(jnp.sum(x * x, axis=-1, keepdims=True) + EPS)


def _l2n_fwd(x):
    r = lax.rsqrt(jnp.sum(x * x, axis=-1, keepdims=True) + EPS)
    n = x * r
    return n, (n, r)


def _l2n_bwd(res, ct):
    n, r = res
    return (r * (ct - n * jnp.sum(ct * n, axis=-1, keepdims=True)),)


_l2n.defvjp(_l2n_fwd, _l2n_bwd)


def _rms(x):
    r = lax.rsqrt(jnp.mean(x * x, axis=-1, keepdims=True) + EPS)
    return x * r, r


def _rms_bwd(dn, n, r):
    return r * (dn - n * jnp.mean(dn * n, axis=-1, keepdims=True))


@jax.custom_vjp
def _rms_normed(x):
    return _rms(x)[0]


def _rms_normed_fwd(x):
    n, r = _rms(x)
    return n, (n, r)


def _rms_normed_bwd(res, ct):
    return (_rms_bwd(ct, *res),)


_rms_normed.defvjp(_rms_normed_fwd, _rms_normed_bwd)


def _onehot_row(idx, width):
    return (lax.broadcasted_iota(jnp.int32, (1, width), 1) == idx).astype(F32)


def _rowsum(x):
    return jnp.sum(x, axis=0, keepdims=True)


def _iota2(n):
    return lax.broadcasted_iota(jnp.int32, (n, n), 0), lax.broadcasted_iota(jnp.int32, (n, n), 1)


def _bmm(a, b):
    return lax.dot_general(a.astype(BF16), b.astype(BF16), (((2,), (1,)), ((0,), (0,))), preferred_element_type=F32)


def _bmm_nt(a, b):
    return lax.dot_general(a.astype(BF16), b.astype(BF16), (((2,), (2,)), ((0,), (0,))), preferred_element_type=F32)


def _bmm_tn(a, b):
    return lax.dot_general(a.astype(BF16), b.astype(BF16), (((1,), (1,)), ((0,), (0,))), preferred_element_type=F32)


def _tri_inv_impl(a):
    n = a.shape[-1]
    r, c = _iota2(n)
    x = r ^ c
    eye = (r == c).astype(F32)
    ad = jnp.where(x < 16, a, 0.0)
    p2 = _bmm(ad, ad)
    e = p2 - ad - _bmm(ad, p2)
    p4 = _bmm(p2, p2)
    e = e + p4 + _bmm(e, p4)
    p8 = _bmm(p4, p4)
    e = e + p8 + _bmm(e, p8)
    size = 16
    while size < n:
        m = jnp.where(jnp.logical_and(x < 2 * size, x >= size), a, 0.0)
        f = m + _bmm(m, e)
        e = e - f - _bmm(e, f)
        size *= 2
    return e + eye


@jax.custom_vjp
def _tri_inv(a, known):
    return _tri_inv_impl(a) if known is None else known


def _tri_inv_fwd(a, known):
    t = _tri_inv(a, known)
    return t, (t, known)


def _tri_inv_bwd(res, dt):
    t, known = res
    return -_bmm_tn(t, _bmm_nt(dt, t)), None if known is None else jnp.zeros_like(known)


_tri_inv.defvjp(_tri_inv_fwd, _tri_inv_bwd)


@jax.custom_vjp
def _standardized(x):
    xc = x - jnp.mean(x, axis=-1, keepdims=True)
    return xc * lax.rsqrt(jnp.mean(xc * xc, axis=-1, keepdims=True) + EPS)


def _standardized_fwd(x):
    xc = x - jnp.mean(x, axis=-1, keepdims=True)
    rstd = lax.rsqrt(jnp.mean(xc * xc, axis=-1, keepdims=True) + EPS)
    y = xc * rstd
    return y, (y, rstd)


def _standardized_bwd(res, ct):
    y, rstd = res
    return (rstd * (ct - jnp.mean(ct, axis=-1, keepdims=True) - y * jnp.mean(ct * y, axis=-1, keepdims=True)),)


_standardized.defvjp(_standardized_fwd, _standardized_bwd)


def _sgu_core(u, v, z, lg, lb, ws, bcol):
    n = ws.shape[0]
    r, c = _iota2(n)
    wm = jnp.where(r >= c, ws, 0.0)
    gu = _gelu(u)
    gv = _gelu(v)
    ln = _standardized(gv) * lg + lb
    s = _bdot(wm, ln) + bcol
    return gu * s * _silu(z)


def _lanes_of(x):
    return jnp.concatenate([x[i] for i in range(x.shape[0])], axis=1)


def _batch_of(x, width):
    return jnp.concatenate([x[None, :, i * width:(i + 1) * width] for i in range(x.shape[1] // width)], axis=0)


def _mask_dot(mask, x):
    hi = x.astype(BF16)
    lo = (x - hi.astype(F32)).astype(BF16)
    m = mask.astype(BF16)
    return jnp.dot(m, hi, preferred_element_type=F32) + jnp.dot(m, lo, preferred_element_type=F32)


def _split_dot(x, mask, dims):
    hi = x.astype(BF16)
    lo = (x - hi.astype(F32)).astype(BF16)
    m = mask.astype(BF16)
    return (lax.dot_general(hi, m, dims, preferred_element_type=F32)
            + lax.dot_general(lo, m, dims, preferred_element_type=F32))


def _lane_select(lanes, blocks, first_lane):
    src = lax.broadcasted_iota(jnp.int32, (lanes, blocks * LANES), 0)
    dst = lax.broadcasted_iota(jnp.int32, (lanes, blocks * LANES), 1) // LANES
    return src == dst + first_lane


def _pick_lanes(x, blocks, first_lane):
    return _pick_lanes_vjp(blocks, first_lane, x)


@functools.partial(jax.custom_vjp, nondiff_argnums=(0, 1))
def _pick_lanes_vjp(blocks, first_lane, x):
    return _split_dot(x, _lane_select(x.shape[-1], blocks, first_lane), (((1,), (0,)), ((), ())))


def _pick_lanes_fwd(blocks, first_lane, x):
    return _pick_lanes_vjp(blocks, first_lane, x), x.shape[-1]


def _pick_lanes_bwd(blocks, first_lane, lanes, ct):
    return (sum(jnp.sum(ct[:, h * LANES:(h + 1) * LANES], axis=-1, keepdims=True) * _onehot_row(first_lane + h, lanes)
                for h in range(blocks)),)


_pick_lanes_vjp.defvjp(_pick_lanes_fwd, _pick_lanes_bwd)


def _tri_mask(n, upper):
    r, c = _iota2(n)
    return (r <= c) if upper else (r >= c)


@jax.custom_vjp
def _cumsum_rows(x):
    return _mask_dot(_tri_mask(x.shape[0], False), x)


def _cumsum_rows_fwd(x):
    return _cumsum_rows(x), None


def _cumsum_rows_bwd(_, ct):
    return (_mask_dot(_tri_mask(ct.shape[0], True), ct),)


_cumsum_rows.defvjp(_cumsum_rows_fwd, _cumsum_rows_bwd)


@jax.custom_vjp
def _colsum_all_rows(x):
    return _mask_dot(jnp.ones((x.shape[0], x.shape[0]), jnp.bool_), x)


def _colsum_all_rows_fwd(x):
    return _colsum_all_rows(x), None


def _colsum_all_rows_bwd(_, ct):
    return (_mask_dot(jnp.ones((ct.shape[0], ct.shape[0]), jnp.bool_), ct),)


_colsum_all_rows.defvjp(_colsum_all_rows_fwd, _colsum_all_rows_bwd)


def _dn_core(cq, ck, cv, z, logits, state, alog, dtb, og, t_known=None):
    gn, cn, dh = cq.shape
    heads = gn // logits.shape[0]
    q = _l2n(_silu(cq)) * (dh ** -0.5)
    k = _l2n(_silu(ck))
    v = _silu(cv)
    beta_lanes = _sigmoid(logits)
    g_lanes = -jnp.exp(alog) * _softplus(logits + dtb)
    beta_all = jnp.concatenate([_pick_lanes(beta_lanes[b], heads, 0) for b in range(logits.shape[0])], axis=1)
    g_all = jnp.concatenate([_pick_lanes(g_lanes[b], heads, heads) for b in range(logits.shape[0])], axis=1)
    beta = _batch_of(beta_all, dh)
    g_wide = _batch_of(g_all, dh)
    r, c = _iota2(cn)
    tril = r >= c
    rw = lax.broadcasted_iota(jnp.int32, (cn, dh), 0)
    cw = lax.broadcasted_iota(jnp.int32, (cn, dh), 1)
    upper_wide = (rw <= cw).astype(F32)
    gc_wide = _batch_of(_cumsum_rows(g_all), dh)
    gc_cols = _batch_of(_colsum_all_rows(_lanes_of(g_wide * upper_wide)), dh)[:, :, :cn]
    decay = jnp.exp(jnp.where(tril, gc_wide[:, :, :cn] - gc_cols, -1e30))
    kb = k * beta
    kk = _bmm_nt(kb, k) * decay
    t = _tri_inv(jnp.where(r > c, kk, 0.0), t_known)
    eg = jnp.exp(gc_wide)
    sol = _bmm(t, jnp.concatenate([v * beta, kb * eg], axis=-1))
    u_val, w_dec = sol[:, :, :dh], sol[:, :, dh:]
    qk = _bmm_nt(q, k) * decay
    g_last = jnp.sum(g_wide, axis=1, keepdims=True)
    k_dec = k * jnp.exp(g_last - gc_wide)
    ws = _bmm(jnp.concatenate([w_dec, q * eg], axis=1), state)
    v_new = u_val - ws[:, :cn]
    o = ws[:, cn:] + _bmm(qk, v_new)
    new_state = state * jnp.exp(g_last) + _bmm_tn(k_dec, v_new)
    return _rms_normed(o) * og * _silu(z), new_state, t


N_CHIPS = 4
HBM_SPEC = pl.BlockSpec(memory_space=pl.ANY)


def _place():
    return lax.axis_index("x"), lax.axis_index("y"), lax.axis_index("c")


def _other_chip(k):
    x, y, _ = _place()
    px = 1 - x if k & 2 else x
    py = 1 - y if k & 1 else y
    return px, py, 2 * px + py


def _remote(src, dst, send_sem, recv_sem, device):
    return pltpu.make_async_remote_copy(src_ref=src, dst_ref=dst, send_sem=send_sem, recv_sem=recv_sem,
                                        device_id=device, device_id_type=MESH)


def _other_device(k):
    x, y, c = _place()
    px = 1 - x if k & 4 else x
    py = 1 - y if k & 2 else y
    pc = 1 - c if k & 1 else c
    return (px, py, pc), 4 * px + 2 * py + pc


def _direct_exchange(srcs, outs, send_sems, recv_sems, local_sems, gather):
    x, y, c = _place()
    me = 4 * x + 2 * y + c

    def copies(arriving):
        out_list = []
        for a, (src, out) in enumerate(zip(srcs, outs)):
            for k in range(1, N_DEV):
                peer, index = _other_device(k)
                mine = src if gather else src.at[index]
                out_list.append(_remote(mine, out.at[index if arriving else me], send_sems.at[a, k - 1],
                                        recv_sems.at[a, k - 1], peer))
        return out_list

    def local_copies():
        return [pltpu.make_async_copy(src if gather else src.at[me], out.at[me], local_sems.at[a])
                for a, (src, out) in enumerate(zip(srcs, outs))]

    def start():
        for cp in local_copies() + copies(False):
            cp.start()

    def wait():
        for cp in copies(True):
            cp.wait_recv()
        for cp in copies(False):
            cp.wait_send()
        for cp in local_copies():
            cp.wait()

    return start, wait


def _exchange_scratch(n):
    return [pltpu.SemaphoreType.DMA((n, N_DEV - 1)), pltpu.SemaphoreType.DMA((n, N_DEV - 1)), pltpu.SemaphoreType.DMA((n,))]


def _all_gather(shards):
    n = len(shards)

    def body(*refs):
        srcs, outs = refs[:n], refs[n:2 * n]
        send_sems, recv_sems, local_sems = refs[2 * n:]
        x, y, c = _place()
        me = 4 * x + 2 * y + c
        sibling = (x, y, 1 - c)
        local = [pltpu.make_async_copy(srcs[a], outs[a].at[me], local_sems.at[a]) for a in range(n)]
        for cp in local:
            cp.start()
        sends = []
        for a in range(n):
            sends.append(_remote(srcs[a], outs[a].at[me], send_sems.at[a, 0], recv_sems.at[a, 0], sibling))
        for k in range(1, N_CHIPS):
            px, py, _ = _other_chip(k)
            for a in range(n):
                sends.append(_remote(srcs[a], outs[a].at[me], send_sems.at[a, k], recv_sems.at[a, k], (px, py, c)))
        for cp in sends:
            cp.start()
        passed = []
        for k in range(1, N_CHIPS):
            px, py, _ = _other_chip(k)
            blk = 4 * px + 2 * py + c
            for a in range(n):
                _remote(srcs[a], outs[a].at[blk], send_sems.at[a, k], recv_sems.at[a, k], (px, py, c)).wait_recv()
            for a in range(n):
                cp = _remote(outs[a].at[blk], outs[a].at[blk], send_sems.at[a, 3 + k], recv_sems.at[a, 3 + k], sibling)
                cp.start()
                passed.append(cp)
        for a in range(n):
            _remote(srcs[a], outs[a].at[me + 1 - 2 * c], send_sems.at[a, 0], recv_sems.at[a, 0], sibling).wait_recv()
        for k in range(1, N_CHIPS):
            px, py, _ = _other_chip(k)
            blk = 4 * px + 2 * py + 1 - c
            for a in range(n):
                _remote(srcs[a], outs[a].at[blk], send_sems.at[a, 3 + k], recv_sems.at[a, 3 + k], sibling).wait_recv()
        for cp in sends + passed:
            cp.wait_send()
        for cp in local:
            cp.wait()

    return pl.pallas_call(
        body, name="all_gather_weights",
        out_shape=tuple(jax.ShapeDtypeStruct((N_DEV,) + a.shape, a.dtype) for a in shards),
        in_specs=[HBM_SPEC] * n, out_specs=(HBM_SPEC,) * n,
        scratch_shapes=[pltpu.SemaphoreType.DMA((n, N_DEV - 1)), pltpu.SemaphoreType.DMA((n, N_DEV - 1)),
                        pltpu.SemaphoreType.DMA((n,))],
    )(*shards)


def _reduce_exchange(by_device, small):
    _, rows, cols = by_device.shape

    def body(g_ref, small_ref, out_ref, small_out_ref, from_sibling, small_from_sibling, stage, sums, small_own, small_sum,
             pair_send, pair_recv, chip_send, chip_recv, local_sems):
        x, y, c = _place()
        mine = 2 * x + y
        sibling = (x, y, 1 - c)
        chips = [(x, y, mine)] + [_other_chip(k) for k in range(1, N_CHIPS)]
        to_sibling = [_remote(g_ref.at[2 * chips[k][2] + 1 - c], from_sibling.at[k], pair_send.at[k], pair_recv.at[k], sibling)
                      for k in range(N_CHIPS)]
        to_sibling.append(_remote(small_ref, small_from_sibling, pair_send.at[N_CHIPS], pair_recv.at[N_CHIPS], sibling))
        for cp in to_sibling:
            cp.start()
        small_mine = pltpu.make_async_copy(small_ref, small_own, local_sems.at[0])
        small_mine.start()
        to_chips = []
        for k in (1, 2, 3, 0):
            px, py, chip = chips[k]
            mine_k = pltpu.make_async_copy(g_ref.at[2 * chip + c], stage, local_sems.at[1])
            mine_k.start()
            to_sibling[k].wait_recv()
            mine_k.wait()
            sums[k] = (stage[...] + from_sibling[k]).astype(sums.dtype)
            if k:
                cp = _remote(sums.at[k], out_ref.at[mine], chip_send.at[0, k - 1], chip_recv.at[0, k - 1], (px, py, c))
                cp.start()
                to_chips.append(cp)
        own_block = pltpu.make_async_copy(sums.at[0], out_ref.at[mine], local_sems.at[2])
        own_block.start()
        to_sibling[N_CHIPS].wait_recv()
        small_mine.wait()
        small_sum[...] = small_own[...] + small_from_sibling[...]
        for k in range(1, N_CHIPS):
            px, py, _ = chips[k]
            cp = _remote(small_sum, small_out_ref.at[mine], chip_send.at[1, k - 1], chip_recv.at[1, k - 1], (px, py, c))
            cp.start()
            to_chips.append(cp)
        own_small = pltpu.make_async_copy(small_sum, small_out_ref.at[mine], local_sems.at[3])
        own_small.start()
        for k in range(1, N_CHIPS):
            px, py, chip = chips[k]
            _remote(sums.at[k], out_ref.at[chip], chip_send.at[0, k - 1], chip_recv.at[0, k - 1], (px, py, c)).wait_recv()
            _remote(small_sum, small_out_ref.at[chip], chip_send.at[1, k - 1], chip_recv.at[1, k - 1], (px, py, c)).wait_recv()
        for cp in to_sibling + to_chips:
            cp.wait_send()
        own_block.wait()
        own_small.wait()

    return pl.pallas_call(
        body, name="grad_reduce_exchange",
        out_shape=(jax.ShapeDtypeStruct((N_CHIPS, rows, cols), BF16), jax.ShapeDtypeStruct((N_CHIPS,) + small.shape, F32)),
        in_specs=[HBM_SPEC, HBM_SPEC], out_specs=(HBM_SPEC, HBM_SPEC),
        scratch_shapes=[pltpu.VMEM((N_CHIPS, rows, cols), F32), pltpu.VMEM(small.shape, F32), pltpu.VMEM((rows, cols), F32),
                        pltpu.VMEM((N_CHIPS, rows, cols), BF16), pltpu.VMEM(small.shape, F32), pltpu.VMEM(small.shape, F32),
                        pltpu.SemaphoreType.DMA((N_CHIPS + 1,)), pltpu.SemaphoreType.DMA((N_CHIPS + 1,)),
                        pltpu.SemaphoreType.DMA((2, N_CHIPS - 1)), pltpu.SemaphoreType.DMA((2, N_CHIPS - 1)),
                        pltpu.SemaphoreType.DMA((4,))],
        compiler_params=pltpu.CompilerParams(vmem_limit_bytes=VMEM_LIMIT),
    )(by_device, small)


def _params(n_axes):
    return pltpu.CompilerParams(dimension_semantics=("arbitrary",) * n_axes, vmem_limit_bytes=VMEM_LIMIT)


def _whole(shape):
    return pl.BlockSpec(shape, lambda *_: (0,) * len(shape))


VMEM_SPEC = pl.BlockSpec(memory_space=pltpu.VMEM)


def _inproj_fwd(x2, seq_len, norm_g, wt, wgt, sgu_weights, conv_w, later_shards):
    t = x2.shape[0]
    tm = min(512, seq_len)
    tiles_per_seq = seq_len // tm
    steps = t // tm
    ns = len(later_shards)

    widths = IN_GROUPS + (wgt.shape[0],)
    starts = (0, IN_GROUPS[0], IN_GROUPS[0] + IN_GROUPS[1], 0)

    def body(x_ref, g_ref, wt_ref, wg_ref, lg_ref, lb_ref, ws_ref, bt_ref, cw_ref, *rest):
        shard_refs, rest = rest[:ns], rest[ns:]
        a_ref, q_ref, z_ref, l_ref, sgu_ref, c_ref = rest[:6]
        gathered_refs, (xpad_ref, send_sems, recv_sems, local_sems) = rest[6:6 + ns], rest[6 + ns:]
        start_gather, wait_gather = _direct_exchange(shard_refs, gathered_refs, send_sems, recv_sems, local_sems, True)
        pl.when(pl.program_id(0) == 0)(start_gather)

        @pl.when(pl.program_id(0) % tiles_per_seq == 0)
        def _():
            xpad_ref[0:CONV_HALO, :] = jnp.zeros((CONV_HALO, xpad_ref.shape[1]), F32)

        n, _ = _rms(x_ref[...])
        xn = (n * g_ref[...]).astype(BF16)

        def project(w_ref, row0, width, o_ref):
            for c0 in range(0, width, 512):
                c1 = min(c0 + 512, width)
                o_ref[:, c0:c1] = lax.dot_general(xn, w_ref[row0 + c0:row0 + c1, :], (((1,), (1,)), ((), ())),
                                                  preferred_element_type=F32)

        def sgu_rows(row0):
            for grp in range(SGU_GROUPS):
                args = _sgu_pieces(a_ref, lg_ref, lb_ref, ws_ref, bt_ref, row0, grp)
                sgu_ref[pl.ds(row0, SGU_CHUNK), pl.ds(grp * 128, 128)] = _sgu_core(*args).astype(sgu_ref.dtype)

        def conv():
            xpad_ref[CONV_HALO:, :] = q_ref[...]
            acc = None
            padded = xpad_ref[...]
            for j in range(CONV_K):
                rows_up = CONV_HALO - CONV_K + 1 + j
                shifted = (padded[rows_up:rows_up + tm] if rows_up % 8 == 0
                           else pltpu.roll(padded, padded.shape[0] - rows_up, axis=0)[0:tm])
                term = cw_ref[j:j + 1, :] * shifted
                acc = term if acc is None else acc + term
            c_ref[...] = acc
            xpad_ref[0:CONV_HALO, :] = xpad_ref[tm:tm + CONV_HALO, :]

        groups = tuple(zip((wt_ref, wt_ref, wt_ref, wg_ref), starts, widths, (a_ref, q_ref, z_ref, l_ref)))
        row_chunks = list(range(0, tm, SGU_CHUNK))
        project(*groups[0])
        for row0 in row_chunks[:len(row_chunks) // 2]:
            sgu_rows(row0)
        project(*groups[1])
        for row0 in row_chunks[len(row_chunks) // 2:]:
            sgu_rows(row0)
        project(*groups[3])
        conv()
        project(*groups[2])
        pl.when(pl.program_id(0) == steps - 1)(wait_gather)

    tile = lambda w: pl.BlockSpec((tm, w), lambda i: (i, 0))
    sgu_shapes = ((1, SGU_WIDTH), (1, SGU_WIDTH), (SGU_GROUPS, SGU_CHUNK, SGU_CHUNK), (SGU_CHUNK, SGU_GROUPS))
    return pl.pallas_call(
        body, name="inproj_sgu_conv_fwd", grid=(steps,),
        out_shape=tuple(jax.ShapeDtypeStruct((t, w), F32) for w in widths)
        + (jax.ShapeDtypeStruct((t, SGU_WIDTH), BF16), jax.ShapeDtypeStruct((t, widths[1]), F32))
        + tuple(jax.ShapeDtypeStruct((N_DEV,) + a.shape, a.dtype) for a in later_shards),
        in_specs=[tile(D_MODEL), _whole((1, D_MODEL)), VMEM_SPEC, VMEM_SPEC]
        + [_whole(s) for s in sgu_shapes] + [_whole((CONV_K, widths[1]))] + [HBM_SPEC] * ns,
        out_specs=tuple(tile(w) for w in widths) + (tile(SGU_WIDTH), tile(widths[1])) + (HBM_SPEC,) * ns,
        scratch_shapes=[pltpu.VMEM((CONV_HALO + tm, widths[1]), F32)] + _exchange_scratch(ns),
        compiler_params=_params(1),
    )(x2, norm_g, wt, wgt, *sgu_weights, conv_w, *later_shards)


def _sgu_pieces(uvz_ref, lg_ref, lb_ref, ws_ref, bt_ref, row0, grp):
    rows = pl.ds(row0, SGU_CHUNK)
    lanes = pl.ds(grp * 128, 128)
    u = uvz_ref[rows, pl.ds(grp * 128, 128)]
    v = uvz_ref[rows, pl.ds(SGU_WIDTH + grp * 128, 128)]
    z = uvz_ref[rows, pl.ds(2 * SGU_WIDTH + grp * 128, 128)]
    bcol = jnp.sum(bt_ref[...] * _onehot_row(grp, SGU_GROUPS), axis=-1, keepdims=True)
    return u, v, z, lg_ref[:, lanes], lb_ref[:, lanes], ws_ref[grp], bcol


def _sgu_bwd_tile(uvz_ref, do_ref, sgu_refs, duvz_ref, grad_refs, pieces):
    lg_ref, lb_ref, ws_ref, bt_ref = sgu_refs
    dlg_ref, dlb_ref, dws_ref, dbt_ref = grad_refs
    for piece in pieces:
        row0, grp = piece // SGU_GROUPS * SGU_CHUNK, piece % SGU_GROUPS
        rows = pl.ds(row0, SGU_CHUNK)
        lanes = pl.ds(grp * 128, 128)
        args = _sgu_pieces(uvz_ref, lg_ref, lb_ref, ws_ref, bt_ref, row0, grp)
        _, pull = jax.vjp(_sgu_core, *args)
        du, dv, dz, dlg, dlb, dws, dbcol = pull(do_ref[rows, lanes])
        duvz_ref[rows, pl.ds(grp * 128, 128)] = du.astype(duvz_ref.dtype)
        duvz_ref[rows, pl.ds(SGU_WIDTH + grp * 128, 128)] = dv.astype(duvz_ref.dtype)
        duvz_ref[rows, pl.ds(2 * SGU_WIDTH + grp * 128, 128)] = dz.astype(duvz_ref.dtype)
        dlg_ref[:, lanes] += dlg
        dlb_ref[:, lanes] += dlb
        dws_ref[grp] += dws
        dbt_ref[...] += dbcol * _onehot_row(grp, SGU_GROUPS)


def _dn_pairs(nb):
    return [(b, h) for b in range(nb) for h in range(DN_HEADS)]


def _dn_batch_args(c_ref, z_ref, rows=slice(None)):
    pairs = _dn_pairs(c_ref.shape[0])
    pick = lambda ref, b, col: ref[b, rows, pl.ds(col, DN_HEAD_DIM)]
    cq = jnp.stack([pick(c_ref, b, h * DN_HEAD_DIM) for b, h in pairs])
    ck = jnp.stack([pick(c_ref, b, DN_WIDTH + h * DN_HEAD_DIM) for b, h in pairs])
    cv = jnp.stack([pick(c_ref, b, 2 * DN_WIDTH + h * DN_HEAD_DIM) for b, h in pairs])
    z = jnp.stack([pick(z_ref, b, h * DN_HEAD_DIM) for b, h in pairs])
    return cq, ck, cv, z


def _dn_weight_specs():
    return [_whole((CONV_K, 3 * DN_WIDTH)), _whole((1, GATE_PAD)), _whole((1, GATE_PAD)), _whole((1, DN_HEAD_DIM))]


def _dn_fwd(conv_out, zg, logits, alog, dtb, og):
    nb, s, _ = conv_out.shape
    nc = s // DN_CHUNK
    pairs = _dn_pairs(nb)
    gn = len(pairs)
    per_step = 4 if nc % 4 == 0 else 2 if nc % 2 == 0 else 1
    chunk = lambda w: pl.BlockSpec((nb, per_step * DN_CHUNK, w), lambda n: (0, n, 0))

    def body(c_ref, z_ref, l_ref, alog_ref, dtb_ref, og_ref, out_ref, st_ref, inv_ref, state_ref):
        n = pl.program_id(0)

        @pl.when(n == 0)
        def _():
            state_ref[...] = jnp.zeros_like(state_ref)

        state = state_ref[...]
        for sub in range(per_step):
            rows = pl.ds(sub * DN_CHUNK, DN_CHUNK)
            cq, ck, cv, z = _dn_batch_args(c_ref, z_ref, rows)
            st_ref[sub] = state
            out, state, t = _dn_core(cq, ck, cv, z, l_ref[:, rows, :], state, alog_ref[...], dtb_ref[...], og_ref[...])
            inv_ref[sub] = t.astype(inv_ref.dtype)
            for i, (b, h) in enumerate(pairs):
                out_ref[b, rows, pl.ds(h * DN_HEAD_DIM, DN_HEAD_DIM)] = out[i].astype(out_ref.dtype)
        state_ref[...] = state

    per_chunk = pl.BlockSpec((per_step, gn, DN_HEAD_DIM, DN_HEAD_DIM), lambda n: (n, 0, 0, 0))
    return pl.pallas_call(
        body, name="deltanet_fwd", grid=(nc // per_step,),
        out_shape=(jax.ShapeDtypeStruct((nb, s, DN_WIDTH), BF16),
                   jax.ShapeDtypeStruct((nc, gn, DN_HEAD_DIM, DN_HEAD_DIM), F32),
                   jax.ShapeDtypeStruct((nc, gn, DN_CHUNK, DN_CHUNK), BF16)),
        in_specs=[chunk(3 * DN_WIDTH), chunk(DN_WIDTH), chunk(GATE_PAD)] + _dn_weight_specs()[1:],
        out_specs=(chunk(DN_WIDTH), per_chunk, pl.BlockSpec((per_step, gn, DN_CHUNK, DN_CHUNK), lambda n: (n, 0, 0, 0))),
        scratch_shapes=[pltpu.VMEM((gn, DN_HEAD_DIM, DN_HEAD_DIM), F32)],
        compiler_params=_params(1),
    )(conv_out, zg, logits, alog, dtb, og)


def _dn_bwd(qkv, conv_out, zg, logits, conv_w, alog, dtb, og, states, inverses, d_out, head_grads):
    nb, s, _ = qkv.shape
    nc = s // DN_CHUNK
    rev = lambda n: nc - 1 - n
    pairs = _dn_pairs(nb)
    gn = len(pairs)
    ng = len(head_grads)

    def body(cur_ref, c_ref, z_ref, l_ref, w_ref, alog_ref, dtb_ref, og_ref, st_ref, inv_ref, do_ref, *rest):
        grad_refs, rest = rest[:ng], rest[ng:]
        dqkv_ref, dz_ref, dl_ref, dw_ref, dalog_ref, ddtb_ref, dog_ref = rest[:7]
        recv_refs, (dstate_ref, dcpad_ref, dw_part_ref, send_sems, recv_sems, local_sems) = rest[7:7 + ng], rest[7 + ng:]
        n = pl.program_id(0)
        start_exchange, wait_exchange = _direct_exchange(grad_refs, recv_refs, send_sems, recv_sems, local_sems, False)
        pl.when(n == 0)(start_exchange)

        @pl.when(n == 0)
        def _():
            dw_part_ref[...] = jnp.zeros_like(dw_part_ref)
            dalog_ref[...] = jnp.zeros_like(dalog_ref)
            ddtb_ref[...] = jnp.zeros_like(ddtb_ref)
            dog_ref[...] = jnp.zeros_like(dog_ref)
            dstate_ref[...] = jnp.zeros_like(dstate_ref)
            dcpad_ref[:, DN_CHUNK:, :] = jnp.zeros((nb, CONV_HALO, 3 * DN_WIDTH), F32)

        cq, ck, cv, z = _dn_batch_args(c_ref, z_ref)
        d_out_g = jnp.stack([do_ref[b, :, pl.ds(h * DN_HEAD_DIM, DN_HEAD_DIM)] for b, h in pairs])
        t_known = inv_ref[...].astype(F32)
        core = lambda *args: _dn_core(*args, t_known=t_known)[:2]
        _, pull = jax.vjp(core, cq, ck, cv, z, l_ref[...], st_ref[...], alog_ref[...], dtb_ref[...], og_ref[...])
        dcq, dck, dcv, dz, dlog, dstate, dalog, ddtb, dog = pull((d_out_g, dstate_ref[...]))
        dstate_ref[...] = dstate
        dl_ref[...] = dlog.astype(dl_ref.dtype)
        dalog_ref[...] += dalog
        ddtb_ref[...] += ddtb
        dog_ref[...] += dog
        for i, (b, h) in enumerate(pairs):
            dcpad_ref[b, 0:DN_CHUNK, pl.ds(h * DN_HEAD_DIM, DN_HEAD_DIM)] = dcq[i]
            dcpad_ref[b, 0:DN_CHUNK, pl.ds(DN_WIDTH + h * DN_HEAD_DIM, DN_HEAD_DIM)] = dck[i]
            dcpad_ref[b, 0:DN_CHUNK, pl.ds(2 * DN_WIDTH + h * DN_HEAD_DIM, DN_HEAD_DIM)] = dcv[i]
            dz_ref[b, :, pl.ds(h * DN_HEAD_DIM, DN_HEAD_DIM)] = dz[i].astype(dz_ref.dtype)
        for b in range(nb):
            for c0 in range(0, 3 * DN_WIDTH, CONV_STRIP):
                cols = pl.ds(c0, CONV_STRIP)
                xb = cur_ref[b, :, cols]
                dx = None
                padded = dcpad_ref[b, :, cols]
                for j in range(CONV_K):
                    rows_up = CONV_K - 1 - j
                    shifted = (pltpu.roll(padded, padded.shape[0] - rows_up, axis=0) if rows_up else padded)[0:DN_CHUNK]
                    term = w_ref[j:j + 1, cols] * shifted
                    dx = term if dx is None else dx + term
                    dw_part_ref[j, :, cols] += jnp.sum((shifted * xb).reshape(DN_CHUNK // 8, 8, CONV_STRIP), axis=0)
                dqkv_ref[b, :, cols] = dx.astype(dqkv_ref.dtype)
            dcpad_ref[b, DN_CHUNK:, :] = dcpad_ref[b, 0:CONV_HALO, :]

        @pl.when(n == nc - 1)
        def _():
            dw_ref[...] = jnp.sum(dw_part_ref[...], axis=1)

        pl.when(n == nc - 1)(wait_exchange)

    chunk = lambda w: pl.BlockSpec((nb, DN_CHUNK, w), lambda n: (0, rev(n), 0))
    return pl.pallas_call(
        body, name="deltanet_bwd", grid=(nc,),
        out_shape=(jax.ShapeDtypeStruct((nb, s, 3 * DN_WIDTH), BF16), jax.ShapeDtypeStruct((nb, s, DN_WIDTH), BF16),
                   jax.ShapeDtypeStruct((nb, s, GATE_PAD), BF16), jax.ShapeDtypeStruct((CONV_K, 3 * DN_WIDTH), F32),
                   jax.ShapeDtypeStruct((1, GATE_PAD), F32), jax.ShapeDtypeStruct((1, GATE_PAD), F32),
                   jax.ShapeDtypeStruct((1, DN_HEAD_DIM), F32))
        + tuple(jax.ShapeDtypeStruct(a.shape, a.dtype) for a in head_grads),
        in_specs=[chunk(3 * DN_WIDTH), chunk(3 * DN_WIDTH), chunk(DN_WIDTH), chunk(GATE_PAD)] + _dn_weight_specs() + [
            pl.BlockSpec((None, gn, DN_HEAD_DIM, DN_HEAD_DIM), lambda n: (rev(n), 0, 0, 0)),
            pl.BlockSpec((None, gn, DN_CHUNK, DN_CHUNK), lambda n: (rev(n), 0, 0, 0)),
            chunk(DN_WIDTH)] + [HBM_SPEC] * ng,
        out_specs=(chunk(3 * DN_WIDTH), chunk(DN_WIDTH), chunk(GATE_PAD), _whole((CONV_K, 3 * DN_WIDTH)),
                   _whole((1, GATE_PAD)), _whole((1, GATE_PAD)), _whole((1, DN_HEAD_DIM))) + (HBM_SPEC,) * ng,
        scratch_shapes=[pltpu.VMEM((gn, DN_HEAD_DIM, DN_HEAD_DIM), F32),
                        pltpu.VMEM((nb, DN_CHUNK + CONV_HALO, 3 * DN_WIDTH), F32),
                        pltpu.VMEM((CONV_K, 8, 3 * DN_WIDTH), F32)] + _exchange_scratch(ng),
        compiler_params=_params(1),
    )(qkv, conv_out, zg, logits, conv_w, alog, dtb, og, states, inverses, d_out, *head_grads)


def _head(a_out, b_out, x2, p2, target, w_out, w_gate, w_proj, ple_g, fin_g):
    t = x2.shape[0]
    tm = min(512, t)
    steps = t // tm

    def body(a_ref, b_ref, x_ref, p_ref, y_ref, wo_ref, wg_ref, wp_ref, pg_ref, fg_ref,
             da_ref, db_ref, dh_ref, dwo_hbm, dwg_hbm, dwp_hbm, dpg_ref, dfg_ref, loss_ref,
             dwo_acc, dwg_acc, dwp_acc, rows_stage, cols_stage):
        i = pl.program_id(0)

        @pl.when(i == 0)
        def _():
            dwo_acc[...] = jnp.zeros_like(dwo_acc)
            dwg_acc[...] = jnp.zeros_like(dwg_acc)
            dwp_acc[...] = jnp.zeros_like(dwp_acc)
            dpg_ref[...] = jnp.zeros_like(dpg_ref)
            dfg_ref[...] = jnp.zeros_like(dfg_ref)
            loss_ref[...] = jnp.zeros_like(loss_ref)

        pg = pg_ref[...]
        fg = fg_ref[...]
        nt = (((1,), (1,)), ((), ()))
        tn = (((0,), (0,)), ((), ()))

        def to_first_norm(rows):
            h1 = (x_ref[rows, :] + jnp.dot(a_ref[rows, :], wo_ref[0:SGU_WIDTH, :], preferred_element_type=F32)
                  + jnp.dot(b_ref[rows, :], wo_ref[SGU_WIDTH:, :], preferred_element_type=F32))
            n1, r1 = _rms(h1)
            pp = jnp.dot(p_ref[rows, :].astype(BF16), wp_ref[...], preferred_element_type=F32)
            return h1, n1, r1, (n1 * pg).astype(BF16), pp

        def to_gate_cotangents(rows, h1, rn, pp):
            gate = _sigmoid(jnp.dot(rn, wg_ref[...], preferred_element_type=F32))
            h2 = h1 + gate * pp
            n2, r2 = _rms(h2)
            err = n2 * fg - y_ref[rows, :]
            loss = _rowsum(jnp.sum(err * err, axis=-1, keepdims=True))
            dy = err * (1.0 / D_MODEL)
            dh2 = _rms_bwd(dy * fg, n2, r2)
            return loss, _rowsum(dy * n2), dh2, (dh2 * gate).astype(BF16), (dh2 * pp * gate * (1.0 - gate)).astype(BF16)

        def to_branch_cotangents(rows, dgl, dh2, n1, r1):
            drn = lax.dot_general(dgl, wg_ref[...], nt, preferred_element_type=F32)
            dh1 = dh2 + _rms_bwd(drn * pg, n1, r1)
            dh_ref[rows, :] = dh1
            dhb = dh1.astype(BF16)
            da_ref[rows, :] = lax.dot_general(dhb, wo_ref[0:SGU_WIDTH, :], nt, preferred_element_type=F32)
            db_ref[rows, :] = lax.dot_general(dhb, wo_ref[SGU_WIDTH:, :], nt, preferred_element_type=F32)
            return _rowsum(drn * n1), dhb

        parts = [pl.ds(k * (tm // 2), tm // 2) for k in range(2)]
        first = [to_first_norm(rows) for rows in parts]
        mid = [to_gate_cotangents(rows, h1, rn, pp) for rows, (h1, _, _, rn, pp) in zip(parts, first)]
        loss_ref[...] += jnp.broadcast_to(mid[0][0] + mid[1][0], loss_ref.shape)
        dfg_ref[...] += mid[0][1] + mid[1][1]
        last = [to_branch_cotangents(rows, m[4], m[2], f[1], f[2]) for rows, m, f in zip(parts, mid, first)]
        rn = jnp.concatenate([f[3] for f in first], axis=0)
        dpp = jnp.concatenate([m[3] for m in mid], axis=0)
        dgl = jnp.concatenate([m[4] for m in mid], axis=0)
        dwp_acc[...] += lax.dot_general(p_ref[...].astype(BF16), dpp, tn, preferred_element_type=F32)
        dwg_acc[...] += lax.dot_general(rn, dgl, tn, preferred_element_type=F32)
        dpg_ref[...] += last[0][0] + last[1][0]
        dhb = jnp.concatenate([l[1] for l in last], axis=0)
        dwo_acc[0:SGU_WIDTH, :] += lax.dot_general(a_ref[...], dhb, tn, preferred_element_type=F32)
        dwo_acc[SGU_WIDTH:, :] += lax.dot_general(b_ref[...], dhb, tn, preferred_element_type=F32)

        @pl.when(i == steps - 1)
        def _():
            for j in range(N_DEV):
                for acc, hbm in ((dwo_acc, dwo_hbm), (dwg_acc, dwg_hbm)):
                    rows_stage[...] = acc[j * LANES:(j + 1) * LANES, :].astype(BF16)
                    pltpu.sync_copy(rows_stage, hbm.at[j])
                cols_stage[...] = dwp_acc[:, j * LANES:(j + 1) * LANES].astype(BF16)
                pltpu.sync_copy(cols_stage, dwp_hbm.at[j])

    tile = lambda w: pl.BlockSpec((tm, w), lambda i: (i, 0))
    return pl.pallas_call(
        body, name="head_fwd_bwd", grid=(steps,),
        out_shape=(jax.ShapeDtypeStruct((t, SGU_WIDTH), F32), jax.ShapeDtypeStruct((t, DN_WIDTH), F32),
                   jax.ShapeDtypeStruct((t, D_MODEL), F32), jax.ShapeDtypeStruct((N_DEV, LANES, D_MODEL), BF16),
                   jax.ShapeDtypeStruct((N_DEV, LANES, D_MODEL), BF16), jax.ShapeDtypeStruct((N_DEV, PLE_DIM, LANES), BF16),
                   jax.ShapeDtypeStruct((1, D_MODEL), F32), jax.ShapeDtypeStruct((1, D_MODEL), F32),
                   jax.ShapeDtypeStruct((8, LANES), F32)),
        in_specs=[tile(SGU_WIDTH), tile(DN_WIDTH), tile(D_MODEL), tile(PLE_DIM), tile(D_MODEL),
                  VMEM_SPEC, VMEM_SPEC, VMEM_SPEC, _whole((1, D_MODEL)), _whole((1, D_MODEL))],
        out_specs=(tile(SGU_WIDTH), tile(DN_WIDTH), tile(D_MODEL), HBM_SPEC, HBM_SPEC, HBM_SPEC,
                   _whole((1, D_MODEL)), _whole((1, D_MODEL)), _whole((8, LANES))),
        scratch_shapes=[pltpu.VMEM((D_MODEL, D_MODEL), F32), pltpu.VMEM((D_MODEL, D_MODEL), F32),
                        pltpu.VMEM((PLE_DIM, D_MODEL), F32), pltpu.VMEM((LANES, D_MODEL), BF16),
                        pltpu.VMEM((PLE_DIM, LANES), BF16)],
        compiler_params=_params(1),
    )(a_out, b_out, x2, p2, target, w_out, w_gate, w_proj, ple_g, fin_g)


def _inproj_bwd(x2, dh1, a_uvz, d_sgu, d_q, d_z, d_l, norm_g, sgu_weights, wt, wgt):
    t = x2.shape[0]
    tm = min(256, t)
    steps = t // tm

    widths = (a_uvz.shape[1], d_q.shape[1], d_z.shape[1], d_l.shape[1])
    starts = (0, widths[0], widths[0] + widths[1], widths[0] + widths[1] + widths[2])

    def body(x_ref, dh_ref, uvz_ref, dsgu_ref, dq_ref, dz_ref, dl_ref, g_ref, lg_ref, lb_ref, ws_ref, bt_ref,
             wt_ref, wgt_ref,
             dx_ref, dw_hbm, dg_ref, dlg_ref, dlb_ref, dws_ref, dbt_ref, dw_acc, stage_ref, da_ref):
        i = pl.program_id(0)

        @pl.when(i == 0)
        def _():
            dw_acc[...] = jnp.zeros_like(dw_acc)
            for ref in (dg_ref, dlg_ref, dlb_ref, dws_ref, dbt_ref):
                ref[...] = jnp.zeros_like(ref)

        g = g_ref[...]
        n, r = _rms(x_ref[...])
        xn = (n * g).astype(BF16)
        dxn = None
        sgu_done = 0

        def sgu_pieces(count):
            nonlocal sgu_done
            _sgu_bwd_tile(uvz_ref, dsgu_ref, (lg_ref, lb_ref, ws_ref, bt_ref), da_ref,
                          (dlg_ref, dlb_ref, dws_ref, dbt_ref), range(sgu_done, sgu_done + count))
            sgu_done += count

        sgu_total = tm // SGU_CHUNK * SGU_GROUPS
        before_q, after_q_chunk = sgu_total // 2, (sgu_total // 4, sgu_total // 8, sgu_total // 8)
        for d_ref, col0 in reversed(tuple(zip((da_ref, dq_ref, dz_ref, dl_ref), starts))):
            if d_ref is dq_ref:
                sgu_pieces(before_q)
            if d_ref is da_ref:
                sgu_pieces(sgu_total - sgu_done)
            width = d_ref.shape[1]
            rows = wgt_ref[...] if d_ref is dl_ref else wt_ref[col0:col0 + width, :]
            term = jnp.dot(d_ref[...], rows, preferred_element_type=F32)
            dxn = term if dxn is None else dxn + term
            for c0 in range(0, width, 512):
                c1 = min(c0 + 512, width)
                dw_acc[col0 + c0:col0 + c1, :] += lax.dot_general(d_ref[:, c0:c1], xn, (((0,), (0,)), ((), ())),
                                                                  preferred_element_type=F32)
                if d_ref is dq_ref:
                    sgu_pieces(after_q_chunk[c0 // 512])
        dg_ref[...] += _rowsum(dxn * n)
        dx_ref[...] = dh_ref[...] + _rms_bwd(dxn * g, n, r)

        @pl.when(i == steps - 1)
        def _():
            for j in range(N_DEV):
                stage_ref[...] = dw_acc[j * IN_SHARD:(j + 1) * IN_SHARD, :]
                pltpu.sync_copy(stage_ref, dw_hbm.at[j])

    tile = lambda w: pl.BlockSpec((tm, w), lambda i: (i, 0))
    sgu_shapes = ((1, SGU_WIDTH), (1, SGU_WIDTH), (SGU_GROUPS, SGU_CHUNK, SGU_CHUNK), (SGU_CHUNK, SGU_GROUPS))
    return pl.pallas_call(
        body, name="inproj_sgu_bwd", grid=(steps,),
        out_shape=(jax.ShapeDtypeStruct((t, D_MODEL), F32), jax.ShapeDtypeStruct((N_DEV, IN_SHARD, D_MODEL), F32),
                   jax.ShapeDtypeStruct((1, D_MODEL), F32)) + tuple(jax.ShapeDtypeStruct(s, F32) for s in sgu_shapes),
        in_specs=[tile(D_MODEL), tile(D_MODEL), tile(widths[0]), tile(SGU_WIDTH)] + [tile(w) for w in widths[1:]]
        + [_whole((1, D_MODEL))] + [_whole(s) for s in sgu_shapes] + [VMEM_SPEC] * 2,
        out_specs=(tile(D_MODEL), HBM_SPEC, _whole((1, D_MODEL))) + tuple(_whole(s) for s in sgu_shapes),
        scratch_shapes=[pltpu.VMEM((sum(widths), D_MODEL), F32), pltpu.VMEM((IN_SHARD, D_MODEL), F32),
                        pltpu.VMEM((tm, widths[0]), BF16)],
        compiler_params=_params(1),
    )(x2, dh1, a_uvz, d_sgu, d_q, d_z, d_l, norm_g, *sgu_weights, wt, wgt)


def _reduce_adamw(recv, w, m, v, name, col_block=None):
    n, rows, cols = recv.shape
    cb = col_block or cols
    lead = w.ndim - 2

    def body(r_ref, w_ref, m_ref, v_ref, g_ref, d_ref, nm_ref, nv_ref):
        g = r_ref[0].astype(F32)
        for i in range(1, n):
            g = g + r_ref[i].astype(F32)
        m_new = ADAM_B1 * m_ref[...] + (1.0 - ADAM_B1) * g
        v_new = ADAM_B2 * v_ref[...] + (1.0 - ADAM_B2) * jnp.square(g)
        m_hat = m_new / (1.0 - ADAM_B1 ** ADAM_STEP)
        v_hat = v_new / (1.0 - ADAM_B2 ** ADAM_STEP)
        g_ref[...] = g
        d_ref[...] = -ADAM_LR * (m_hat / (jnp.sqrt(v_hat) + ADAM_EPS) + ADAM_WD * w_ref[...])
        nm_ref[...] = m_new
        nv_ref[...] = v_new

    blk = pl.BlockSpec((None,) * lead + (rows, cb), lambda i: (0,) * lead + (0, i))
    return pl.pallas_call(
        body, name=name, grid=(cols // cb,),
        out_shape=tuple(jax.ShapeDtypeStruct(w.shape, F32) for _ in range(4)),
        in_specs=[pl.BlockSpec((n, rows, cb), lambda i: (0, 0, i)), blk, blk, blk],
        out_specs=(blk, blk, blk, blk),
        compiler_params=_params(1),
    )(recv, w, m, v)


def _adamw_replicated(received, ws, ms, vs):
    nw = len(ws)
    starts = [sum(SMALL_PIECE_ROWS[:i]) for i in range(len(SMALL_PIECE_ROWS))]

    def natural(g_ref, row0, shape):
        cols, rows = shape[-1], _size(shape[:-1])
        if cols == LANES:
            return g_ref[row0:row0 + rows, :].reshape(shape)
        if cols < LANES:
            return g_ref[row0:row0 + 1, 0:cols].reshape(shape)
        per = cols // LANES
        return jnp.concatenate(
            [jnp.concatenate([g_ref[row0 + r * per + k:row0 + r * per + k + 1, :] for k in range(per)], axis=1)
             for r in range(rows)], axis=0).reshape(shape)

    def body(r_ref, *refs):
        w_refs, m_refs, v_refs = refs[:nw], refs[nw:2 * nw], refs[2 * nw:3 * nw]
        conv_ref, loss_ref = refs[3 * nw], refs[3 * nw + 1]
        out_refs, g_ref = refs[3 * nw + 2:-1], refs[-1]
        g = r_ref[0]
        for q in range(1, N_CHIPS):
            g = g + r_ref[q]
        g_ref[...] = g
        conv_ref[...] = natural(g_ref, starts[0], (CONV_K, 3 * DN_WIDTH))
        loss_ref[...] = natural(g_ref, starts[-1], (1, 1))
        for i in range(nw):
            gi = natural(g_ref, starts[1 + i], w_refs[i].shape)
            m_new = ADAM_B1 * m_refs[i][...] + (1.0 - ADAM_B1) * gi
            v_new = ADAM_B2 * v_refs[i][...] + (1.0 - ADAM_B2) * jnp.square(gi)
            m_hat = m_new / (1.0 - ADAM_B1 ** ADAM_STEP)
            v_hat = v_new / (1.0 - ADAM_B2 ** ADAM_STEP)
            out_refs[4 * i][...] = gi
            out_refs[4 * i + 1][...] = -ADAM_LR * (m_hat / (jnp.sqrt(v_hat) + ADAM_EPS) + ADAM_WD * w_refs[i][...])
            out_refs[4 * i + 2][...] = m_new
            out_refs[4 * i + 3][...] = v_new

    def spec(a):
        lead = max(a.ndim - 3, 0)
        return pl.BlockSpec((None,) * lead + a.shape[lead:], lambda: (0,) * a.ndim)

    weight_specs = [spec(a) for a in ws]
    return pl.pallas_call(
        body, name="adamw_replicated",
        out_shape=(jax.ShapeDtypeStruct((CONV_K, 3 * DN_WIDTH), F32), jax.ShapeDtypeStruct((1, 1), F32))
        + tuple(jax.ShapeDtypeStruct(a.shape, F32) for a in ws for _ in range(4)),
        in_specs=[pl.BlockSpec(received.shape, lambda: (0, 0, 0))] + weight_specs * 3,
        out_specs=(pl.BlockSpec((CONV_K, 3 * DN_WIDTH), lambda: (0, 0)), pl.BlockSpec((1, 1), lambda: (0, 0)))
        + tuple(s for s in weight_specs for _ in range(4)),
        scratch_shapes=[pltpu.VMEM(received.shape[1:], F32)],
        compiler_params=pltpu.CompilerParams(vmem_limit_bytes=VMEM_LIMIT),
    )(received, *ws, *ms, *vs)


def _pack_rows(pieces, rows):
    padded = [jnp.pad(jnp.ravel(p), (0, -p.size % LANES)) for p in pieces]
    flat = jnp.concatenate(padded)
    return jnp.pad(flat, (0, rows * LANES - flat.shape[0])).reshape(rows, LANES)


def kernel(x, p, norm_g, w_in, sgu_ln_g, sgu_ln_b, sgu_w_s, sgu_b_s, dn_conv_w, dn_a_log, dn_dt_bias, dn_o_norm_g, w_out, ple_norm_g, ple_gate_w, ple_proj_w, final_norm_g, loss_target, m_norm_g, m_w_in, m_sgu_ln_g, m_sgu_ln_b, m_sgu_w_s, m_sgu_b_s, m_dn_conv_w, m_dn_a_log, m_dn_dt_bias, m_dn_o_norm_g, m_w_out, m_ple_norm_g, m_ple_gate_w, m_ple_proj_w, m_final_norm_g, v_norm_g, v_w_in, v_sgu_ln_g, v_sgu_ln_b, v_sgu_w_s, v_sgu_b_s, v_dn_conv_w, v_dn_a_log, v_dn_dt_bias, v_dn_o_norm_g, v_w_out, v_ple_norm_g, v_ple_gate_w, v_ple_proj_w, v_final_norm_g):
    weights = dict(norm_g=norm_g, w_in=w_in, sgu_ln_g=sgu_ln_g, sgu_ln_b=sgu_ln_b, sgu_w_s=sgu_w_s, sgu_b_s=sgu_b_s,
                   dn_conv_w=dn_conv_w, dn_a_log=dn_a_log, dn_dt_bias=dn_dt_bias, dn_o_norm_g=dn_o_norm_g, w_out=w_out,
                   ple_norm_g=ple_norm_g, ple_gate_w=ple_gate_w, ple_proj_w=ple_proj_w, final_norm_g=final_norm_g)
    mom1 = dict(norm_g=m_norm_g, w_in=m_w_in, sgu_ln_g=m_sgu_ln_g, sgu_ln_b=m_sgu_ln_b, sgu_w_s=m_sgu_w_s,
                sgu_b_s=m_sgu_b_s, dn_conv_w=m_dn_conv_w, dn_a_log=m_dn_a_log, dn_dt_bias=m_dn_dt_bias,
                dn_o_norm_g=m_dn_o_norm_g, w_out=m_w_out, ple_norm_g=m_ple_norm_g, ple_gate_w=m_ple_gate_w,
                ple_proj_w=m_ple_proj_w, final_norm_g=m_final_norm_g)
    mom2 = dict(norm_g=v_norm_g, w_in=v_w_in, sgu_ln_g=v_sgu_ln_g, sgu_ln_b=v_sgu_ln_b, sgu_w_s=v_sgu_w_s,
                sgu_b_s=v_sgu_b_s, dn_conv_w=v_dn_conv_w, dn_a_log=v_dn_a_log, dn_dt_bias=v_dn_dt_bias,
                dn_o_norm_g=v_dn_o_norm_g, w_out=v_w_out, ple_norm_g=v_ple_norm_g, ple_gate_w=v_ple_gate_w,
                ple_proj_w=v_ple_proj_w, final_norm_g=v_final_norm_g)
    nb, s, _ = x.shape
    t = nb * s

    transposed = lambda a: jnp.transpose(a, (2, 0, 1)).reshape(IN_SHARD, D_MODEL)
    w_in_t, m_in_t, v_in_t = transposed(w_in), transposed(m_w_in), transposed(v_w_in)
    w_in_blocks, conv_blocks = _all_gather([w_in_t.astype(BF16), dn_conv_w[0]])
    w_in_full_t = w_in_blocks.reshape(IN_COLS, D_MODEL)
    wgt = jnp.pad(w_in_full_t[sum(IN_GROUPS):], ((0, GATE_PAD - 2 * DN_HEADS), (0, 0)))
    conv_full = jnp.moveaxis(conv_blocks, 0, 1).reshape(CONV_K, 3 * DN_WIDTH)
    later_shards = [w_out[0].astype(BF16), ple_gate_w[0].astype(BF16), ple_proj_w[0].astype(BF16)]

    pad_row = lambda a: jnp.pad(a.reshape(1, -1), ((0, 0), (DN_HEADS, GATE_PAD - DN_HEADS - a.size)))
    alog, dtb = pad_row(dn_a_log), pad_row(dn_dt_bias)
    og = dn_o_norm_g.reshape(1, DN_HEAD_DIM)
    ws = sgu_w_s.reshape(SGU_GROUPS, SGU_CHUNK, SGU_CHUNK)
    b_t = sgu_b_s.reshape(SGU_GROUPS, SGU_CHUNK).T
    fin_g = final_norm_g.reshape(1, D_MODEL)

    x2 = x.reshape(t, D_MODEL)
    sgu_weights = (sgu_ln_g, sgu_ln_b, ws, b_t)
    a_uvz, b_qkv, b_z, b_l, a_out, conv_out, w_out_blocks, w_gate_blocks, w_proj_blocks = _inproj_fwd(
        x2, s, norm_g, w_in_full_t, wgt, sgu_weights, conv_full, later_shards)
    w_out_full = w_out_blocks.reshape(D_MODEL, D_MODEL)
    w_gate_full = w_gate_blocks.reshape(D_MODEL, D_MODEL)
    w_proj_full = jnp.moveaxis(w_proj_blocks, 0, 1).reshape(PLE_DIM, D_MODEL)
    qkv3 = b_qkv.reshape(nb, s, 3 * DN_WIDTH)
    conv_out = conv_out.reshape(nb, s, 3 * DN_WIDTH)
    z3 = b_z.reshape(nb, s, DN_WIDTH)
    l3 = b_l.reshape(nb, s, GATE_PAD)
    b_out, states, inverses = _dn_fwd(conv_out, z3, l3, alog, dtb, og)

    d_a, d_b, dh1, g_w_out, g_gate, g_proj, g_ple_g, g_fin_g, loss_tile = _head(
        a_out, b_out.reshape(t, DN_WIDTH), x2, p.reshape(t, PLE_DIM), loss_target.reshape(t, D_MODEL),
        w_out_full, w_gate_full, w_proj_full, ple_norm_g, fin_g)
    d_qkv, d_z, d_l, g_conv, g_alog, g_dtb, g_og, *head_received = _dn_bwd(
        qkv3, conv_out, z3, l3, conv_full, alog, dtb, og, states, inverses, d_b.reshape(nb, s, DN_WIDTH),
        [g_w_out, g_gate, g_proj])
    grad_x, g_w_in, g_norm, g_ln_g, g_ln_b, g_ws, g_bt = _inproj_bwd(
        x2, dh1, a_uvz, d_a, d_qkv.reshape(t, 3 * DN_WIDTH), d_z.reshape(t, DN_WIDTH), d_l.reshape(t, GATE_PAD),
        norm_g, sgu_weights, w_in_full_t, wgt)

    small = _pack_rows([g_conv, g_norm, g_ln_g, g_ln_b, g_ws, g_bt.T, g_alog[:, DN_HEADS:2 * DN_HEADS], g_dtb[:, DN_HEADS:2 * DN_HEADS], g_og,
                        g_ple_g, g_fin_g, (0.5 / D_MODEL) * loss_tile[0:1, 0:1]], SMALL_ROWS)
    w_in_received, small_received = _reduce_exchange(g_w_in, small)

    results = {}
    outs = _reduce_adamw(w_in_received, w_in_t, m_in_t, v_in_t, "adamw_w_in", 4 * LANES)
    results["w_in"] = [jnp.transpose(a.reshape(IN_SHARD, 1, D_MODEL), (1, 2, 0)) for a in outs]
    for name, recv in zip(("w_out", "ple_gate_w", "ple_proj_w"), head_received):
        results[name] = _reduce_adamw(recv, weights[name], mom1[name], mom2[name], "adamw_" + name)
    names = [name for name, _ in REPLICATED]
    two_d = lambda a: a.reshape(1, -1) if a.ndim == 1 else a
    g_conv_sum, loss_sum, *flat_outs = _adamw_replicated(
        small_received, *[[two_d(src[k]) for k in names] for src in (weights, mom1, mom2)])
    for i, k in enumerate(names):
        results[k] = [a.reshape(weights[k].shape) for a in flat_outs[4 * i:4 * i + 4]]
    loss = loss_sum[0, 0]
    me = 4 * lax.axis_index("x") + 2 * lax.axis_index("y") + lax.axis_index("c")
    conv_mine = lax.dynamic_slice(g_conv_sum, (0, me * 192), (CONV_K, 192))
    results["dn_conv_w"] = _reduce_adamw(conv_mine[None], dn_conv_w, m_dn_conv_w, v_dn_conv_w, "adamw_dn_conv_w")

    return (loss, grad_x.reshape(nb, s, D_MODEL), *[results[k][0] for k in WEIGHT_ORDER],
            *[results[k][1] for k in WEIGHT_ORDER], *[results[k][2] for k in WEIGHT_ORDER],
            *[results[k][3] for k in WEIGHT_ORDER])
```
